```python
import math
import jax, jax.numpy as jnp
from jax import lax
import numpy as np


D_MODEL = 2048
BATCH = 8
SEQ = 2048
DEPTH = 1

CHUNK = 64
Q_BLOCK = 128
N_MEM = 256
HG_HEADS = 16
HG_KDIM = 128
HG_VDIM = D_MODEL // HG_HEADS
HG_K = HG_HEADS * HG_KDIM
HG_V = HG_HEADS * HG_VDIM
MLA_HEADS = 16
Q_LORA = 512
KV_LORA = 512
QK_NOPE = 128
QK_ROPE = 64
V_HEAD = 128
MLA_QK = QK_NOPE + QK_ROPE
MLA_V = MLA_HEADS * V_HEAD
ROPE_THETA = 10000.0
XA_HEADS = 4
XA_HEAD_DIM = 128
XA_WIDTH = XA_HEADS * XA_HEAD_DIM
D_FF = 5632
FFN_RESIDUAL_WEIGHT = 0.5
EPS = 1e-6
IN_SPLITS = (HG_K, HG_K, HG_V, HG_V, Q_LORA, KV_LORA, QK_ROPE, D_MODEL, D_MODEL)
IN_DIM = HG_K + HG_K + HG_V + HG_V + Q_LORA + KV_LORA + QK_ROPE + D_MODEL + D_MODEL

kernel_name = 'hybrid_hgrn2_mla_macaron_sandwich_memory_block'


def rms_norm(x, g):
    xf = x.astype(jnp.float32)
    y = xf * lax.rsqrt(jnp.mean(xf * xf, axis=-1, keepdims=True) + EPS)
    return (y * g.astype(jnp.float32)).astype(x.dtype)


def rotary(x, cos, sin):
    x1, x2 = jnp.split(x, 2, axis=-1)
    return jnp.concatenate([x1 * cos - x2 * sin, x2 * cos + x1 * sin], axis=-1)


def split_columns(u):
    parts, start = [], 0
    for width in IN_SPLITS:
        parts.append(u[..., start:start + width])
        start += width
    return parts


def swiglu_half_step(x, pre_g, w_gate, w_up, w_down, post_g):
    h = rms_norm(x, pre_g)
    y = (jax.nn.silu(h @ w_gate) * (h @ w_up)) @ w_down
    return x + FFN_RESIDUAL_WEIGHT * rms_norm(y, post_g)


def hgrn2_chunk_scan(q, k, log_f, v):
    B, S, H, K = q.shape
    V = v.shape[-1]
    n = S // CHUNK

    def to_chunks(t):
        return t.reshape(B, n, CHUNK, H, t.shape[-1]).transpose(1, 0, 3, 2, 4)

    causal = jnp.tril(jnp.ones((CHUNK, CHUNK), dtype=bool))

    def step(state, inp):
        qc, kc, gc, vc = inp
        b = jnp.cumsum(gc, axis=2)
        diff = b[:, :, :, None, :] - b[:, :, None, :, :]
        decay = jnp.exp(jnp.where(causal[None, None, :, :, None], diff, -jnp.inf))
        scores = jnp.einsum('bhtk,bhsk,bhtsk->bhts', qc, kc, decay)
        o = (jnp.einsum('bhts,bhsv->bhtv', scores, vc)
             + jnp.einsum('bhtk,bhkv->bhtv', qc * jnp.exp(b), state))
        b_last = b[:, :, -1:, :]
        new_state = (jnp.exp(b_last[:, :, 0, :])[..., None] * state
                     + jnp.einsum('bhsk,bhsv->bhkv', kc * jnp.exp(b_last - b), vc))
        return new_state, o

    state0 = jnp.zeros((B, H, K, V), jnp.float32)
    _, o = lax.scan(step, state0, (to_chunks(q), to_chunks(k), to_chunks(log_f), to_chunks(v)))
    return o.transpose(1, 0, 3, 2, 4).reshape(B, S, H, V)


def chunk_causal_attention(q, k, v, scale):
    B, S, H, Dqk = q.shape
    Dv = v.shape[-1]
    nblk = S // Q_BLOCK
    qb = q.reshape(B, nblk, Q_BLOCK, H, Dqk).transpose(1, 0, 2, 3, 4)
    key_chunk = jnp.arange(S) // CHUNK

    def one_block(args):
        blk, q_blk = args
        query_chunk = (blk * Q_BLOCK + jnp.arange(Q_BLOCK)) // CHUNK
        mask = key_chunk[None, :] <= query_chunk[:, None]
        s = jnp.einsum('bqhd,bkhd->bhqk', q_blk, k, preferred_element_type=jnp.float32) * scale
        s = jnp.where(mask[None, None], s, -jnp.inf)
        p = jax.nn.softmax(s, axis=-1).astype(v.dtype)
        return jnp.einsum('bhqk,bkhd->bqhd', p, v)

    out = lax.map(one_block, (jnp.arange(nblk), qb))
    return out.transpose(1, 0, 2, 3, 4).reshape(B, S, H, Dv)


def hybrid_mixer(x, cos, sin, lb, pre_g, w_in, hg_norm_g, q_norm_g, w_q_up, kv_norm_g,
                 w_kv_up, w_branch_a, w_branch_b, w_out, post_g):
    B, S, _ = x.shape
    f32 = jnp.float32
    h = rms_norm(x, pre_g)
    u = h @ w_in
    q_hg, f_hg, i_hg, og_hg, c_q, c_kv, k_pe, gate_a, gate_b = split_columns(u)

    f_raw = f_hg.astype(f32).reshape(B, S, HG_HEADS, HG_KDIM)
    lb_h = lb.reshape(HG_HEADS, HG_KDIM)
    log_f = jnp.logaddexp(jnp.log(lb_h), jnp.log1p(-lb_h) + jax.nn.log_sigmoid(f_raw))
    k_in = (1.0 - lb_h) * jax.nn.sigmoid(-f_raw)
    q_in = jax.nn.silu(q_hg.astype(f32)).reshape(B, S, HG_HEADS, HG_KDIM)
    v_in = i_hg.astype(f32).reshape(B, S, HG_HEADS, HG_VDIM)
    o_a = hgrn2_chunk_scan(q_in, k_in, log_f, v_in).astype(x.dtype)
    o_a = (rms_norm(o_a, hg_norm_g.reshape(HG_HEADS, HG_VDIM))
           * jax.nn.silu(og_hg).reshape(B, S, HG_HEADS, HG_VDIM))
    y_a = o_a.reshape(B, S, HG_V) @ w_branch_a

    q = (rms_norm(c_q, q_norm_g) @ w_q_up).reshape(B, S, MLA_HEADS, MLA_QK)
    q_nope, q_pe = q[..., :QK_NOPE], q[..., QK_NOPE:]
    q_pe = rotary(q_pe, cos[:, :, None, :], sin[:, :, None, :])
    kv = (rms_norm(c_kv, kv_norm_g) @ w_kv_up).reshape(B, S, MLA_HEADS, QK_NOPE + V_HEAD)
    k_nope, v = kv[..., :QK_NOPE], kv[..., QK_NOPE:]
    k_pe = rotary(k_pe, cos, sin)
    q_full = jnp.concatenate([q_nope, q_pe], axis=-1)
    k_full = jnp.concatenate(
        [k_nope, jnp.broadcast_to(k_pe[:, :, None, :], (B, S, MLA_HEADS, QK_ROPE))], axis=-1)
    o_b = chunk_causal_attention(q_full, k_full, v, MLA_QK ** -0.5)
    y_b = o_b.reshape(B, S, MLA_V) @ w_branch_b

    y = jax.nn.sigmoid(gate_a) * y_a + jax.nn.sigmoid(gate_b) * y_b
    return x + rms_norm(y @ w_out, post_g)


def memory_cross_attention(x, mem, pre_g, mem_g, w_q, w_k, w_v, w_o, post_g):
    B, S, _ = x.shape
    M = mem.shape[1]
    h = rms_norm(x, pre_g)
    m = rms_norm(mem, mem_g)
    q = (h @ w_q).reshape(B, S, XA_HEADS, XA_HEAD_DIM)
    k = (m @ w_k).reshape(B, M, XA_HEADS, XA_HEAD_DIM)
    v = (m @ w_v).reshape(B, M, XA_HEADS, XA_HEAD_DIM)
    s = jnp.einsum('bqhd,bkhd->bhqk', q, k, preferred_element_type=jnp.float32) * XA_HEAD_DIM ** -0.5
    p = jax.nn.softmax(s, axis=-1).astype(v.dtype)
    o = jnp.einsum('bhqk,bkhd->bqhd', p, v).reshape(B, S, XA_WIDTH)
    return x + rms_norm(o @ w_o, post_g)


def _fwd_setup_inputs(seed: int = 0) -> dict:
    key = jax.random.key(seed)
    ks = iter(jax.random.split(key, 40))
    L = DEPTH

    def w(shape, fan_in):
        return jax.random.normal(next(ks), shape, jnp.float32) * fan_in ** -0.5

    def gain(n):
        return 1.0 + 0.02 * jax.random.normal(next(ks), (L, n), jnp.float32)

    x = jax.random.normal(next(ks), (BATCH, SEQ, D_MODEL), jnp.float32)
    mem = jax.random.normal(next(ks), (BATCH, N_MEM, D_MODEL), jnp.float32)
    offset = jax.random.randint(next(ks), (BATCH, 1), 0, 64, dtype=jnp.int32) * CHUNK
    positions = (offset + jnp.arange(SEQ, dtype=jnp.int32)[None, :]).astype(jnp.int32)
    hgrn_lb_logits = 0.5 * jax.random.normal(next(ks), (L + 1, HG_K), jnp.float32)
    return {
        'x': x, 'mem': mem, 'positions': positions, 'hgrn_lb_logits': hgrn_lb_logits,
        'ffn1_pre_g': gain(D_MODEL),
        'ffn1_w_gate': w((L, D_MODEL, D_FF), D_MODEL),
        'ffn1_w_up': w((L, D_MODEL, D_FF), D_MODEL),
        'ffn1_w_down': w((L, D_FF, D_MODEL), D_FF),
        'ffn1_post_g': gain(D_MODEL),
        'mix_pre_g': gain(D_MODEL),
        'w_in': w((L, D_MODEL, IN_DIM), D_MODEL),
        'hg_norm_g': gain(HG_V),
        'mla_q_norm_g': gain(Q_LORA),
        'mla_w_q_up': w((L, Q_LORA, MLA_HEADS * MLA_QK), Q_LORA),
        'mla_kv_norm_g': gain(KV_LORA),
        'mla_w_kv_up': w((L, KV_LORA, MLA_HEADS * (QK_NOPE + V_HEAD)), KV_LORA),
        'w_branch_a': w((L, HG_V, D_MODEL), HG_V),
        'w_branch_b': w((L, MLA_V, D_MODEL), MLA_V),
        'w_out': w((L, D_MODEL, D_MODEL), D_MODEL),
        'mix_post_g': gain(D_MODEL),
        'xa_pre_g': gain(D_MODEL),
        'xa_mem_g': gain(D_MODEL),
        'xa_w_q': w((L, D_MODEL, XA_WIDTH), D_MODEL),
        'xa_w_k': w((L, D_MODEL, XA_WIDTH), D_MODEL),
        'xa_w_v': w((L, D_MODEL, XA_WIDTH), D_MODEL),
        'xa_w_o': w((L, XA_WIDTH, D_MODEL), XA_WIDTH),
        'xa_post_g': gain(D_MODEL),
        'ffn2_pre_g': gain(D_MODEL),
        'ffn2_w_gate': w((L, D_MODEL, D_FF), D_MODEL),
        'ffn2_w_up': w((L, D_MODEL, D_FF), D_MODEL),
        'ffn2_w_down': w((L, D_FF, D_MODEL), D_FF),
        'ffn2_post_g': gain(D_MODEL),
    }


def _fwd_reference(x, mem, positions, hgrn_lb_logits,
              ffn1_pre_g, ffn1_w_gate, ffn1_w_up, ffn1_w_down, ffn1_post_g,
              mix_pre_g, w_in, hg_norm_g, mla_q_norm_g, mla_w_q_up, mla_kv_norm_g, mla_w_kv_up,
              w_branch_a, w_branch_b, w_out, mix_post_g,
              xa_pre_g, xa_mem_g, xa_w_q, xa_w_k, xa_w_v, xa_w_o, xa_post_g,
              ffn2_pre_g, ffn2_w_gate, ffn2_w_up, ffn2_w_down, ffn2_post_g):
    f32 = jnp.float32
    inv_freq = 1.0 / (ROPE_THETA ** (jnp.arange(0, QK_ROPE, 2, dtype=f32) / QK_ROPE))
    ang = positions.astype(f32)[..., None] * inv_freq
    cos = jnp.cos(ang).astype(x.dtype)
    sin = jnp.sin(ang).astype(x.dtype)
    lower_bounds = jnp.cumsum(jax.nn.softmax(hgrn_lb_logits.astype(f32), axis=0), axis=0)

    for l in range(DEPTH):
        x = swiglu_half_step(x, ffn1_pre_g[l], ffn1_w_gate[l], ffn1_w_up[l], ffn1_w_down[l], ffn1_post_g[l])
        x = hybrid_mixer(x, cos, sin, lower_bounds[l], mix_pre_g[l], w_in[l], hg_norm_g[l],
                         mla_q_norm_g[l], mla_w_q_up[l], mla_kv_norm_g[l], mla_w_kv_up[l],
                         w_branch_a[l], w_branch_b[l], w_out[l], mix_post_g[l])
        x = memory_cross_attention(x, mem, xa_pre_g[l], xa_mem_g[l], xa_w_q[l], xa_w_k[l],
                                   xa_w_v[l], xa_w_o[l], xa_post_g[l])
        x = swiglu_half_step(x, ffn2_pre_g[l], ffn2_w_gate[l], ffn2_w_up[l], ffn2_w_down[l], ffn2_post_g[l])
    return x


import jax as _jax
import jax.numpy as _jnp

TWIN_FORMAT = 'train_step'
FWD_PARAMS = ['x', 'mem', 'positions', 'hgrn_lb_logits', 'ffn1_pre_g', 'ffn1_w_gate', 'ffn1_w_up', 'ffn1_w_down', 'ffn1_post_g', 'mix_pre_g', 'w_in', 'hg_norm_g', 'mla_q_norm_g', 'mla_w_q_up', 'mla_kv_norm_g', 'mla_w_kv_up', 'w_branch_a', 'w_branch_b', 'w_out', 'mix_post_g', 'xa_pre_g', 'xa_mem_g', 'xa_w_q', 'xa_w_k', 'xa_w_v', 'xa_w_o', 'xa_post_g', 'ffn2_pre_g', 'ffn2_w_gate', 'ffn2_w_up', 'ffn2_w_down', 'ffn2_post_g']
TWIN_WEIGHTS = ['hgrn_lb_logits', 'ffn1_pre_g', 'ffn1_w_gate', 'ffn1_w_up', 'ffn1_w_down', 'ffn1_post_g', 'mix_pre_g', 'w_in', 'hg_norm_g', 'mla_q_norm_g', 'mla_w_q_up', 'mla_kv_norm_g', 'mla_w_kv_up', 'w_branch_a', 'w_branch_b', 'w_out', 'mix_post_g', 'xa_pre_g', 'xa_mem_g', 'xa_w_q', 'xa_w_k', 'xa_w_v', 'xa_w_o', 'xa_post_g', 'ffn2_pre_g', 'ffn2_w_gate', 'ffn2_w_up', 'ffn2_w_down', 'ffn2_post_g']
TWIN_DIFF_INPUT = 'x'
TWIN_INPUTS = ['x', 'mem', 'positions', 'hgrn_lb_logits', 'ffn1_pre_g', 'ffn1_w_gate', 'ffn1_w_up', 'ffn1_w_down', 'ffn1_post_g', 'mix_pre_g', 'w_in', 'hg_norm_g', 'mla_q_norm_g', 'mla_w_q_up', 'mla_kv_norm_g', 'mla_w_kv_up', 'w_branch_a', 'w_branch_b', 'w_out', 'mix_post_g', 'xa_pre_g', 'xa_mem_g', 'xa_w_q', 'xa_w_k', 'xa_w_v', 'xa_w_o', 'xa_post_g', 'ffn2_pre_g', 'ffn2_w_gate', 'ffn2_w_up', 'ffn2_w_down', 'ffn2_post_g', 'loss_target', 'm_hgrn_lb_logits', 'm_ffn1_pre_g', 'm_ffn1_w_gate', 'm_ffn1_w_up', 'm_ffn1_w_down', 'm_ffn1_post_g', 'm_mix_pre_g', 'm_w_in', 'm_hg_norm_g', 'm_mla_q_norm_g', 'm_mla_w_q_up', 'm_mla_kv_norm_g', 'm_mla_w_kv_up', 'm_w_branch_a', 'm_w_branch_b', 'm_w_out', 'm_mix_post_g', 'm_xa_pre_g', 'm_xa_mem_g', 'm_xa_w_q', 'm_xa_w_k', 'm_xa_w_v', 'm_xa_w_o', 'm_xa_post_g', 'm_ffn2_pre_g', 'm_ffn2_w_gate', 'm_ffn2_w_up', 'm_ffn2_w_down', 'm_ffn2_post_g', 'v_hgrn_lb_logits', 'v_ffn1_pre_g', 'v_ffn1_w_gate', 'v_ffn1_w_up', 'v_ffn1_w_down', 'v_ffn1_post_g', 'v_mix_pre_g', 'v_w_in', 'v_hg_norm_g', 'v_mla_q_norm_g', 'v_mla_w_q_up', 'v_mla_kv_norm_g', 'v_mla_w_kv_up', 'v_w_branch_a', 'v_w_branch_b', 'v_w_out', 'v_mix_post_g', 'v_xa_pre_g', 'v_xa_mem_g', 'v_xa_w_q', 'v_xa_w_k', 'v_xa_w_v', 'v_xa_w_o', 'v_xa_post_g', 'v_ffn2_pre_g', 'v_ffn2_w_gate', 'v_ffn2_w_up', 'v_ffn2_w_down', 'v_ffn2_post_g']
TWIN_OUTPUTS = ['loss', 'grad_x', 'grad_hgrn_lb_logits', 'grad_ffn1_pre_g', 'grad_ffn1_w_gate', 'grad_ffn1_w_up', 'grad_ffn1_w_down', 'grad_ffn1_post_g', 'grad_mix_pre_g', 'grad_w_in', 'grad_hg_norm_g', 'grad_mla_q_norm_g', 'grad_mla_w_q_up', 'grad_mla_kv_norm_g', 'grad_mla_w_kv_up', 'grad_w_branch_a', 'grad_w_branch_b', 'grad_w_out', 'grad_mix_post_g', 'grad_xa_pre_g', 'grad_xa_mem_g', 'grad_xa_w_q', 'grad_xa_w_k', 'grad_xa_w_v', 'grad_xa_w_o', 'grad_xa_post_g', 'grad_ffn2_pre_g', 'grad_ffn2_w_gate', 'grad_ffn2_w_up', 'grad_ffn2_w_down', 'grad_ffn2_post_g', 'delta_hgrn_lb_logits', 'delta_ffn1_pre_g', 'delta_ffn1_w_gate', 'delta_ffn1_w_up', 'delta_ffn1_w_down', 'delta_ffn1_post_g', 'delta_mix_pre_g', 'delta_w_in', 'delta_hg_norm_g', 'delta_mla_q_norm_g', 'delta_mla_w_q_up', 'delta_mla_kv_norm_g', 'delta_mla_w_kv_up', 'delta_w_branch_a', 'delta_w_branch_b', 'delta_w_out', 'delta_mix_post_g', 'delta_xa_pre_g', 'delta_xa_mem_g', 'delta_xa_w_q', 'delta_xa_w_k', 'delta_xa_w_v', 'delta_xa_w_o', 'delta_xa_post_g', 'delta_ffn2_pre_g', 'delta_ffn2_w_gate', 'delta_ffn2_w_up', 'delta_ffn2_w_down', 'delta_ffn2_post_g', 'new_m_hgrn_lb_logits', 'new_m_ffn1_pre_g', 'new_m_ffn1_w_gate', 'new_m_ffn1_w_up', 'new_m_ffn1_w_down', 'new_m_ffn1_post_g', 'new_m_mix_pre_g', 'new_m_w_in', 'new_m_hg_norm_g', 'new_m_mla_q_norm_g', 'new_m_mla_w_q_up', 'new_m_mla_kv_norm_g', 'new_m_mla_w_kv_up', 'new_m_w_branch_a', 'new_m_w_branch_b', 'new_m_w_out', 'new_m_mix_post_g', 'new_m_xa_pre_g', 'new_m_xa_mem_g', 'new_m_xa_w_q', 'new_m_xa_w_k', 'new_m_xa_w_v', 'new_m_xa_w_o', 'new_m_xa_post_g', 'new_m_ffn2_pre_g', 'new_m_ffn2_w_gate', 'new_m_ffn2_w_up', 'new_m_ffn2_w_down', 'new_m_ffn2_post_g', 'new_v_hgrn_lb_logits', 'new_v_ffn1_pre_g', 'new_v_ffn1_w_gate', 'new_v_ffn1_w_up', 'new_v_ffn1_w_down', 'new_v_ffn1_post_g', 'new_v_mix_pre_g', 'new_v_w_in', 'new_v_hg_norm_g', 'new_v_mla_q_norm_g', 'new_v_mla_w_q_up', 'new_v_mla_kv_norm_g', 'new_v_mla_w_kv_up', 'new_v_w_branch_a', 'new_v_w_branch_b', 'new_v_w_out', 'new_v_mix_post_g', 'new_v_xa_pre_g', 'new_v_xa_mem_g', 'new_v_xa_w_q', 'new_v_xa_w_k', 'new_v_xa_w_v', 'new_v_xa_w_o', 'new_v_xa_post_g', 'new_v_ffn2_pre_g', 'new_v_ffn2_w_gate', 'new_v_ffn2_w_up', 'new_v_ffn2_w_down', 'new_v_ffn2_post_g']
TWIN_LEAF_KINDS = {'loss': 'loss', 'grad_x': 'grad_x', 'grad_hgrn_lb_logits': 'grad_w', 'grad_ffn1_pre_g': 'grad_w', 'grad_ffn1_w_gate': 'grad_w', 'grad_ffn1_w_up': 'grad_w', 'grad_ffn1_w_down': 'grad_w', 'grad_ffn1_post_g': 'grad_w', 'grad_mix_pre_g': 'grad_w', 'grad_w_in': 'grad_w', 'grad_hg_norm_g': 'grad_w', 'grad_mla_q_norm_g': 'grad_w', 'grad_mla_w_q_up': 'grad_w', 'grad_mla_kv_norm_g': 'grad_w', 'grad_mla_w_kv_up': 'grad_w', 'grad_w_branch_a': 'grad_w', 'grad_w_branch_b': 'grad_w', 'grad_w_out': 'grad_w', 'grad_mix_post_g': 'grad_w', 'grad_xa_pre_g': 'grad_w', 'grad_xa_mem_g': 'grad_w', 'grad_xa_w_q': 'grad_w', 'grad_xa_w_k': 'grad_w', 'grad_xa_w_v': 'grad_w', 'grad_xa_w_o': 'grad_w', 'grad_xa_post_g': 'grad_w', 'grad_ffn2_pre_g': 'grad_w', 'grad_ffn2_w_gate': 'grad_w', 'grad_ffn2_w_up': 'grad_w', 'grad_ffn2_w_down': 'grad_w', 'grad_ffn2_post_g': 'grad_w', 'delta_hgrn_lb_logits': 'delta_w', 'delta_ffn1_pre_g': 'delta_w', 'delta_ffn1_w_gate': 'delta_w', 'delta_ffn1_w_up': 'delta_w', 'delta_ffn1_w_down': 'delta_w', 'delta_ffn1_post_g': 'delta_w', 'delta_mix_pre_g': 'delta_w', 'delta_w_in': 'delta_w', 'delta_hg_norm_g': 'delta_w', 'delta_mla_q_norm_g': 'delta_w', 'delta_mla_w_q_up': 'delta_w', 'delta_mla_kv_norm_g': 'delta_w', 'delta_mla_w_kv_up': 'delta_w', 'delta_w_branch_a': 'delta_w', 'delta_w_branch_b': 'delta_w', 'delta_w_out': 'delta_w', 'delta_mix_post_g': 'delta_w', 'delta_xa_pre_g': 'delta_w', 'delta_xa_mem_g': 'delta_w', 'delta_xa_w_q': 'delta_w', 'delta_xa_w_k': 'delta_w', 'delta_xa_w_v': 'delta_w', 'delta_xa_w_o': 'delta_w', 'delta_xa_post_g': 'delta_w', 'delta_ffn2_pre_g': 'delta_w', 'delta_ffn2_w_gate': 'delta_w', 'delta_ffn2_w_up': 'delta_w', 'delta_ffn2_w_down': 'delta_w', 'delta_ffn2_post_g': 'delta_w', 'new_m_hgrn_lb_logits': 'new_m', 'new_m_ffn1_pre_g': 'new_m', 'new_m_ffn1_w_gate': 'new_m', 'new_m_ffn1_w_up': 'new_m', 'new_m_ffn1_w_down': 'new_m', 'new_m_ffn1_post_g': 'new_m', 'new_m_mix_pre_g': 'new_m', 'new_m_w_in': 'new_m', 'new_m_hg_norm_g': 'new_m', 'new_m_mla_q_norm_g': 'new_m', 'new_m_mla_w_q_up': 'new_m', 'new_m_mla_kv_norm_g': 'new_m', 'new_m_mla_w_kv_up': 'new_m', 'new_m_w_branch_a': 'new_m', 'new_m_w_branch_b': 'new_m', 'new_m_w_out': 'new_m', 'new_m_mix_post_g': 'new_m', 'new_m_xa_pre_g': 'new_m', 'new_m_xa_mem_g': 'new_m', 'new_m_xa_w_q': 'new_m', 'new_m_xa_w_k': 'new_m', 'new_m_xa_w_v': 'new_m', 'new_m_xa_w_o': 'new_m', 'new_m_xa_post_g': 'new_m', 'new_m_ffn2_pre_g': 'new_m', 'new_m_ffn2_w_gate': 'new_m', 'new_m_ffn2_w_up': 'new_m', 'new_m_ffn2_w_down': 'new_m', 'new_m_ffn2_post_g': 'new_m', 'new_v_hgrn_lb_logits': 'new_v', 'new_v_ffn1_pre_g': 'new_v', 'new_v_ffn1_w_gate': 'new_v', 'new_v_ffn1_w_up': 'new_v', 'new_v_ffn1_w_down': 'new_v', 'new_v_ffn1_post_g': 'new_v', 'new_v_mix_pre_g': 'new_v', 'new_v_w_in': 'new_v', 'new_v_hg_norm_g': 'new_v', 'new_v_mla_q_norm_g': 'new_v', 'new_v_mla_w_q_up': 'new_v', 'new_v_mla_kv_norm_g': 'new_v', 'new_v_mla_w_kv_up': 'new_v', 'new_v_w_branch_a': 'new_v', 'new_v_w_branch_b': 'new_v', 'new_v_w_out': 'new_v', 'new_v_mix_post_g': 'new_v', 'new_v_xa_pre_g': 'new_v', 'new_v_xa_mem_g': 'new_v', 'new_v_xa_w_q': 'new_v', 'new_v_xa_w_k': 'new_v', 'new_v_xa_w_v': 'new_v', 'new_v_xa_w_o': 'new_v', 'new_v_xa_post_g': 'new_v', 'new_v_ffn2_pre_g': 'new_v', 'new_v_ffn2_w_gate': 'new_v', 'new_v_ffn2_w_up': 'new_v', 'new_v_ffn2_w_down': 'new_v', 'new_v_ffn2_post_g': 'new_v'}


def _forward(args):
    return _fwd_reference(*[args[k] for k in FWD_PARAMS])


def _output_shape():
    out = _jax.eval_shape(lambda: _forward(_fwd_setup_inputs(0)))
    return out.shape, out.dtype

N_MICROBATCH = 1
ADAM_LR = 0.001
ADAM_B1 = 0.9
ADAM_B2 = 0.999
ADAM_EPS = 1e-08
ADAM_WD = 0.01
ADAM_STEP = 10
PER_EXAMPLE_BATCH_AXIS = {'x': 0, 'mem': 0, 'positions': 0, 'loss_target': 0}
SHARED_INPUTS = []
_WEIGHT_DTYPES = {'hgrn_lb_logits': _jnp.float32, 'ffn1_pre_g': _jnp.float32, 'ffn1_w_gate': _jnp.float32, 'ffn1_w_up': _jnp.float32, 'ffn1_w_down': _jnp.float32, 'ffn1_post_g': _jnp.float32, 'mix_pre_g': _jnp.float32, 'w_in': _jnp.float32, 'hg_norm_g': _jnp.float32, 'mla_q_norm_g': _jnp.float32, 'mla_w_q_up': _jnp.float32, 'mla_kv_norm_g': _jnp.float32, 'mla_w_kv_up': _jnp.float32, 'w_branch_a': _jnp.float32, 'w_branch_b': _jnp.float32, 'w_out': _jnp.float32, 'mix_post_g': _jnp.float32, 'xa_pre_g': _jnp.float32, 'xa_mem_g': _jnp.float32, 'xa_w_q': _jnp.float32, 'xa_w_k': _jnp.float32, 'xa_w_v': _jnp.float32, 'xa_w_o': _jnp.float32, 'xa_post_g': _jnp.float32, 'ffn2_pre_g': _jnp.float32, 'ffn2_w_gate': _jnp.float32, 'ffn2_w_up': _jnp.float32, 'ffn2_w_down': _jnp.float32, 'ffn2_post_g': _jnp.float32}
MOMENT_SCALE = {'hgrn_lb_logits': 1.268725e-02, 'ffn1_pre_g': 2.070923e-01, 'ffn1_w_gate': 8.700348e-02, 'ffn1_w_up': 8.554668e-02, 'ffn1_w_down': 1.422869e-01, 'ffn1_post_g': 1.956993e+00, 'mix_pre_g': 2.501304e-01, 'w_in': 9.920621e-02, 'hg_norm_g': 1.680495e-01, 'mla_q_norm_g': 7.433482e-02, 'mla_w_q_up': 2.958464e-02, 'mla_kv_norm_g': 2.101159e-01, 'mla_w_kv_up': 6.718815e-02, 'w_branch_a': 1.659966e-01, 'w_branch_b': 9.341754e-02, 'w_out': 1.952126e-01, 'mix_post_g': 7.987762e+00, 'xa_pre_g': 1.506289e-01, 'xa_mem_g': 3.291420e-01, 'xa_w_q': 2.963994e-01, 'xa_w_k': 3.005435e-01, 'xa_w_v': 6.001049e-01, 'xa_w_o': 2.995121e-01, 'xa_post_g': 8.255893e+00, 'ffn2_pre_g': 2.186742e-01, 'ffn2_w_gate': 7.260674e-02, 'ffn2_w_up': 1.084075e-01, 'ffn2_w_down': 1.837554e-01, 'ffn2_post_g': 1.999695e+00}


def _to_microbatches(a, axis):
    t = _jnp.moveaxis(a, axis, 0)
    t = t.reshape((N_MICROBATCH, t.shape[0] // N_MICROBATCH) + t.shape[1:])
    return _jnp.moveaxis(t, 1, axis + 1)


def setup_inputs(seed: int = 0) -> dict:
    inp = _fwd_setup_inputs(seed)
    key = _jax.random.fold_in(_jax.random.key(seed), 7919)
    shape, _ = _output_shape()
    out = dict(inp)
    out["loss_target"] = _jax.random.normal(_jax.random.fold_in(key, 0), shape, _jnp.float32)
    for i, name in enumerate(TWIN_WEIGHTS):
        w = inp[name].astype(_jnp.float32)
        if MOMENT_SCALE is None:
            s = _jnp.sqrt(_jnp.mean(_jnp.square(w)) + 1e-30)
        else:
            s = MOMENT_SCALE[name]
        km, kv = _jax.random.split(_jax.random.fold_in(key, i + 1))
        out[name] = w
        out["m_" + name] = s * _jax.random.normal(km, w.shape, _jnp.float32)
        out["v_" + name] = (s * s) * _jax.random.uniform(kv, w.shape, _jnp.float32, 0.5, 1.5)
    if N_MICROBATCH > 1:
        for name, axis in PER_EXAMPLE_BATCH_AXIS.items():
            out[name] = _to_microbatches(out[name], axis)
    return {'x': out['x'], 'mem': out['mem'], 'positions': out['positions'], 'hgrn_lb_logits': out['hgrn_lb_logits'], 'ffn1_pre_g': out['ffn1_pre_g'], 'ffn1_w_gate': out['ffn1_w_gate'], 'ffn1_w_up': out['ffn1_w_up'], 'ffn1_w_down': out['ffn1_w_down'], 'ffn1_post_g': out['ffn1_post_g'], 'mix_pre_g': out['mix_pre_g'], 'w_in': out['w_in'], 'hg_norm_g': out['hg_norm_g'], 'mla_q_norm_g': out['mla_q_norm_g'], 'mla_w_q_up': out['mla_w_q_up'], 'mla_kv_norm_g': out['mla_kv_norm_g'], 'mla_w_kv_up': out['mla_w_kv_up'], 'w_branch_a': out['w_branch_a'], 'w_branch_b': out['w_branch_b'], 'w_out': out['w_out'], 'mix_post_g': out['mix_post_g'], 'xa_pre_g': out['xa_pre_g'], 'xa_mem_g': out['xa_mem_g'], 'xa_w_q': out['xa_w_q'], 'xa_w_k': out['xa_w_k'], 'xa_w_v': out['xa_w_v'], 'xa_w_o': out['xa_w_o'], 'xa_post_g': out['xa_post_g'], 'ffn2_pre_g': out['ffn2_pre_g'], 'ffn2_w_gate': out['ffn2_w_gate'], 'ffn2_w_up': out['ffn2_w_up'], 'ffn2_w_down': out['ffn2_w_down'], 'ffn2_post_g': out['ffn2_post_g'], 'loss_target': out['loss_target'], 'm_hgrn_lb_logits': out['m_hgrn_lb_logits'], 'm_ffn1_pre_g': out['m_ffn1_pre_g'], 'm_ffn1_w_gate': out['m_ffn1_w_gate'], 'm_ffn1_w_up': out['m_ffn1_w_up'], 'm_ffn1_w_down': out['m_ffn1_w_down'], 'm_ffn1_post_g': out['m_ffn1_post_g'], 'm_mix_pre_g': out['m_mix_pre_g'], 'm_w_in': out['m_w_in'], 'm_hg_norm_g': out['m_hg_norm_g'], 'm_mla_q_norm_g': out['m_mla_q_norm_g'], 'm_mla_w_q_up': out['m_mla_w_q_up'], 'm_mla_kv_norm_g': out['m_mla_kv_norm_g'], 'm_mla_w_kv_up': out['m_mla_w_kv_up'], 'm_w_branch_a': out['m_w_branch_a'], 'm_w_branch_b': out['m_w_branch_b'], 'm_w_out': out['m_w_out'], 'm_mix_post_g': out['m_mix_post_g'], 'm_xa_pre_g': out['m_xa_pre_g'], 'm_xa_mem_g': out['m_xa_mem_g'], 'm_xa_w_q': out['m_xa_w_q'], 'm_xa_w_k': out['m_xa_w_k'], 'm_xa_w_v': out['m_xa_w_v'], 'm_xa_w_o': out['m_xa_w_o'], 'm_xa_post_g': out['m_xa_post_g'], 'm_ffn2_pre_g': out['m_ffn2_pre_g'], 'm_ffn2_w_gate': out['m_ffn2_w_gate'], 'm_ffn2_w_up': out['m_ffn2_w_up'], 'm_ffn2_w_down': out['m_ffn2_w_down'], 'm_ffn2_post_g': out['m_ffn2_post_g'], 'v_hgrn_lb_logits': out['v_hgrn_lb_logits'], 'v_ffn1_pre_g': out['v_ffn1_pre_g'], 'v_ffn1_w_gate': out['v_ffn1_w_gate'], 'v_ffn1_w_up': out['v_ffn1_w_up'], 'v_ffn1_w_down': out['v_ffn1_w_down'], 'v_ffn1_post_g': out['v_ffn1_post_g'], 'v_mix_pre_g': out['v_mix_pre_g'], 'v_w_in': out['v_w_in'], 'v_hg_norm_g': out['v_hg_norm_g'], 'v_mla_q_norm_g': out['v_mla_q_norm_g'], 'v_mla_w_q_up': out['v_mla_w_q_up'], 'v_mla_kv_norm_g': out['v_mla_kv_norm_g'], 'v_mla_w_kv_up': out['v_mla_w_kv_up'], 'v_w_branch_a': out['v_w_branch_a'], 'v_w_branch_b': out['v_w_branch_b'], 'v_w_out': out['v_w_out'], 'v_mix_post_g': out['v_mix_post_g'], 'v_xa_pre_g': out['v_xa_pre_g'], 'v_xa_mem_g': out['v_xa_mem_g'], 'v_xa_w_q': out['v_xa_w_q'], 'v_xa_w_k': out['v_xa_w_k'], 'v_xa_w_v': out['v_xa_w_v'], 'v_xa_w_o': out['v_xa_w_o'], 'v_xa_post_g': out['v_xa_post_g'], 'v_ffn2_pre_g': out['v_ffn2_pre_g'], 'v_ffn2_w_gate': out['v_ffn2_w_gate'], 'v_ffn2_w_up': out['v_ffn2_w_up'], 'v_ffn2_w_down': out['v_ffn2_w_down'], 'v_ffn2_post_g': out['v_ffn2_post_g']}


def _loss(weights, diff, rest, loss_target):
    with _jax.named_scope("forward"):
        args = {**rest, TWIN_DIFF_INPUT: diff, **{k: w.astype(_WEIGHT_DTYPES[k]) for k, w in weights.items()}}
        y = _forward(args)
    with _jax.named_scope("loss_head"):
        err = _jnp.square(y.astype(_jnp.float32) - loss_target)
        return 0.5 * _jnp.sum(_jnp.mean(err, axis=-1)) if err.ndim else 0.5 * err


def _adamw(w, g, m, v):
    m = ADAM_B1 * m + (1.0 - ADAM_B1) * g
    v = ADAM_B2 * v + (1.0 - ADAM_B2) * _jnp.square(g)
    m_hat = m / (1.0 - ADAM_B1 ** ADAM_STEP)
    v_hat = v / (1.0 - ADAM_B2 ** ADAM_STEP)
    delta = -ADAM_LR * (m_hat / (_jnp.sqrt(v_hat) + ADAM_EPS) + ADAM_WD * w)
    return delta, m, v


def reference(x, mem, positions, hgrn_lb_logits, ffn1_pre_g, ffn1_w_gate, ffn1_w_up, ffn1_w_down, ffn1_post_g, mix_pre_g, w_in, hg_norm_g, mla_q_norm_g, mla_w_q_up, mla_kv_norm_g, mla_w_kv_up, w_branch_a, w_branch_b, w_out, mix_post_g, xa_pre_g, xa_mem_g, xa_w_q, xa_w_k, xa_w_v, xa_w_o, xa_post_g, ffn2_pre_g, ffn2_w_gate, ffn2_w_up, ffn2_w_down, ffn2_post_g, loss_target, m_hgrn_lb_logits, m_ffn1_pre_g, m_ffn1_w_gate, m_ffn1_w_up, m_ffn1_w_down, m_ffn1_post_g, m_mix_pre_g, m_w_in, m_hg_norm_g, m_mla_q_norm_g, m_mla_w_q_up, m_mla_kv_norm_g, m_mla_w_kv_up, m_w_branch_a, m_w_branch_b, m_w_out, m_mix_post_g, m_xa_pre_g, m_xa_mem_g, m_xa_w_q, m_xa_w_k, m_xa_w_v, m_xa_w_o, m_xa_post_g, m_ffn2_pre_g, m_ffn2_w_gate, m_ffn2_w_up, m_ffn2_w_down, m_ffn2_post_g, v_hgrn_lb_logits, v_ffn1_pre_g, v_ffn1_w_gate, v_ffn1_w_up, v_ffn1_w_down, v_ffn1_post_g, v_mix_pre_g, v_w_in, v_hg_norm_g, v_mla_q_norm_g, v_mla_w_q_up, v_mla_kv_norm_g, v_mla_w_kv_up, v_w_branch_a, v_w_branch_b, v_w_out, v_mix_post_g, v_xa_pre_g, v_xa_mem_g, v_xa_w_q, v_xa_w_k, v_xa_w_v, v_xa_w_o, v_xa_post_g, v_ffn2_pre_g, v_ffn2_w_gate, v_ffn2_w_up, v_ffn2_w_down, v_ffn2_post_g):
    given = dict(x=x, mem=mem, positions=positions, hgrn_lb_logits=hgrn_lb_logits, ffn1_pre_g=ffn1_pre_g, ffn1_w_gate=ffn1_w_gate, ffn1_w_up=ffn1_w_up, ffn1_w_down=ffn1_w_down, ffn1_post_g=ffn1_post_g, mix_pre_g=mix_pre_g, w_in=w_in, hg_norm_g=hg_norm_g, mla_q_norm_g=mla_q_norm_g, mla_w_q_up=mla_w_q_up, mla_kv_norm_g=mla_kv_norm_g, mla_w_kv_up=mla_w_kv_up, w_branch_a=w_branch_a, w_branch_b=w_branch_b, w_out=w_out, mix_post_g=mix_post_g, xa_pre_g=xa_pre_g, xa_mem_g=xa_mem_g, xa_w_q=xa_w_q, xa_w_k=xa_w_k, xa_w_v=xa_w_v, xa_w_o=xa_w_o, xa_post_g=xa_post_g, ffn2_pre_g=ffn2_pre_g, ffn2_w_gate=ffn2_w_gate, ffn2_w_up=ffn2_w_up, ffn2_w_down=ffn2_w_down, ffn2_post_g=ffn2_post_g, loss_target=loss_target, m_hgrn_lb_logits=m_hgrn_lb_logits, m_ffn1_pre_g=m_ffn1_pre_g, m_ffn1_w_gate=m_ffn1_w_gate, m_ffn1_w_up=m_ffn1_w_up, m_ffn1_w_down=m_ffn1_w_down, m_ffn1_post_g=m_ffn1_post_g, m_mix_pre_g=m_mix_pre_g, m_w_in=m_w_in, m_hg_norm_g=m_hg_norm_g, m_mla_q_norm_g=m_mla_q_norm_g, m_mla_w_q_up=m_mla_w_q_up, m_mla_kv_norm_g=m_mla_kv_norm_g, m_mla_w_kv_up=m_mla_w_kv_up, m_w_branch_a=m_w_branch_a, m_w_branch_b=m_w_branch_b, m_w_out=m_w_out, m_mix_post_g=m_mix_post_g, m_xa_pre_g=m_xa_pre_g, m_xa_mem_g=m_xa_mem_g, m_xa_w_q=m_xa_w_q, m_xa_w_k=m_xa_w_k, m_xa_w_v=m_xa_w_v, m_xa_w_o=m_xa_w_o, m_xa_post_g=m_xa_post_g, m_ffn2_pre_g=m_ffn2_pre_g, m_ffn2_w_gate=m_ffn2_w_gate, m_ffn2_w_up=m_ffn2_w_up, m_ffn2_w_down=m_ffn2_w_down, m_ffn2_post_g=m_ffn2_post_g, v_hgrn_lb_logits=v_hgrn_lb_logits, v_ffn1_pre_g=v_ffn1_pre_g, v_ffn1_w_gate=v_ffn1_w_gate, v_ffn1_w_up=v_ffn1_w_up, v_ffn1_w_down=v_ffn1_w_down, v_ffn1_post_g=v_ffn1_post_g, v_mix_pre_g=v_mix_pre_g, v_w_in=v_w_in, v_hg_norm_g=v_hg_norm_g, v_mla_q_norm_g=v_mla_q_norm_g, v_mla_w_q_up=v_mla_w_q_up, v_mla_kv_norm_g=v_mla_kv_norm_g, v_mla_w_kv_up=v_mla_w_kv_up, v_w_branch_a=v_w_branch_a, v_w_branch_b=v_w_branch_b, v_w_out=v_w_out, v_mix_post_g=v_mix_post_g, v_xa_pre_g=v_xa_pre_g, v_xa_mem_g=v_xa_mem_g, v_xa_w_q=v_xa_w_q, v_xa_w_k=v_xa_w_k, v_xa_w_v=v_xa_w_v, v_xa_w_o=v_xa_w_o, v_xa_post_g=v_xa_post_g, v_ffn2_pre_g=v_ffn2_pre_g, v_ffn2_w_gate=v_ffn2_w_gate, v_ffn2_w_up=v_ffn2_w_up, v_ffn2_w_down=v_ffn2_w_down, v_ffn2_post_g=v_ffn2_post_g)
    weights = {n: given[n] for n in TWIN_WEIGHTS}
    shared = {n: given[n] for n in SHARED_INPUTS}
    per_example = {n: given[n] for n in ['x', 'mem', 'positions']}
    grad_fn = _jax.value_and_grad(_loss, argnums=(0, 1))

    def one_microbatch(ex, loss_target):
        ex = dict(ex)
        diff = ex.pop(TWIN_DIFF_INPUT)
        return grad_fn(weights, diff, {**shared, **ex}, loss_target)

    if N_MICROBATCH == 1:
        loss, (grad_w, grad_x) = one_microbatch(per_example, given["loss_target"])
    else:
        def body(carry, xs):
            loss_sum, grad_sum = carry
            l_k, (gw_k, gx_k) = one_microbatch(xs[0], xs[1])
            with _jax.named_scope("update"):
                return (loss_sum + l_k, _jax.tree.map(_jnp.add, grad_sum, gw_k)), gx_k

        init = (_jnp.zeros((), _jnp.float32), _jax.tree.map(_jnp.zeros_like, weights))
        (loss, grad_w), grad_x = _jax.lax.scan(body, init, (per_example, given["loss_target"]))
    with _jax.named_scope("update"):
        delta_w, new_m, new_v = {}, {}, {}
        for n in TWIN_WEIGHTS:
            delta_w[n], new_m[n], new_v[n] = _adamw(weights[n], grad_w[n], given["m_" + n], given["v_" + n])
    return (loss, grad_x, *[grad_w[n] for n in TWIN_WEIGHTS], *[delta_w[n] for n in TWIN_WEIGHTS],
            *[new_m[n] for n in TWIN_WEIGHTS], *[new_v[n] for n in TWIN_WEIGHTS])
```

```python
import functools

import jax
import jax.numpy as jnp
from jax import lax
from jax.experimental import pallas as pl
from jax.experimental.pallas import tpu as pltpu

F32 = jnp.float32
BF16 = jnp.bfloat16

N_DEV = 8
D_MODEL = 2048
CHUNK = 64
CHUNK_SHIFT = 6
SUB = 16
HG_HEADS = 16
HEAD_W = 128
MLA_HEADS = 16
Q_LORA = 512
KV_LORA = 512
QK_ROPE = 64
MLA_QK = 192
XA_HEADS = 4
ROPE_THETA = 10000.0
EPS = 1e-6
PACK_W = 1024
VMEM_LIMIT = 56 * 1024 * 1024

ADAM_LR = 0.001
ADAM_B1 = 0.9
ADAM_B2 = 0.999
ADAM_EPS = 1e-08
ADAM_WD = 0.01
ADAM_STEP = 10

U_W = 4 * 2048 + 2 * 512 + 2 * 2048 + 2 * HEAD_W

WEIGHTS = ['hgrn_lb_logits', 'ffn1_pre_g', 'ffn1_w_gate', 'ffn1_w_up', 'ffn1_w_down', 'ffn1_post_g', 'mix_pre_g',
           'w_in', 'hg_norm_g', 'mla_q_norm_g', 'mla_w_q_up', 'mla_kv_norm_g', 'mla_w_kv_up', 'w_branch_a',
           'w_branch_b', 'w_out', 'mix_post_g', 'xa_pre_g', 'xa_mem_g', 'xa_w_q', 'xa_w_k', 'xa_w_v', 'xa_w_o',
           'xa_post_g', 'ffn2_pre_g', 'ffn2_w_gate', 'ffn2_w_up', 'ffn2_w_down', 'ffn2_post_g']
BIG = {'ffn1_w_gate': True, 'ffn1_w_up': True, 'ffn1_w_down': False, 'w_in': True, 'mla_w_q_up': True,
       'mla_w_kv_up': True, 'w_branch_a': False, 'w_branch_b': False, 'w_out': False, 'xa_w_q': False,
       'xa_w_k': False, 'xa_w_v': False, 'xa_w_o': True, 'ffn2_w_gate': True, 'ffn2_w_up': True,
       'ffn2_w_down': False}
SMALL = [n for n in WEIGHTS if n not in BIG]


def _cparams(**kw):
    return pltpu.CompilerParams(vmem_limit_bytes=VMEM_LIMIT, **kw)


def _pick(dim, cands):
    for c in cands:
        if dim % c == 0:
            return c
    return dim


def R_(arr, w=None, off=0, stride=0):
    return ('r', arr, arr.shape[1] if w is None else w, off, stride)


def P_(arr, w=None, off=0, stride=0):
    return ('p', arr, arr.shape[1] if w is None else w, off, stride)


def rowmap(name, fn, ins, outs=None, *, tile, ncol=1, rows_inner=True, cts=None, wrt=None, gdt=None, cat=False):
    rows = next(a.shape[0] for k, a, *_ in ins if k == 'r')
    nrow = rows // tile
    assert nrow * tile == rows
    grid = (ncol, nrow) if rows_inner else (nrow, ncol)

    def ij(g0, g1):
        return (g1, g0) if rows_inner else (g0, g1)

    def spec(kind, arr, w, off, stride):
        if kind == 'r':
            return pl.BlockSpec((tile, w), lambda g0, g1: (ij(g0, g1)[0], off + stride * ij(g0, g1)[1]))
        return pl.BlockSpec((arr.shape[0], w), lambda g0, g1: (0, off + stride * ij(g0, g1)[1]))

    ops = list(ins) + list(cts or [])
    in_specs = [spec(*o) for o in ops]
    n_in = len(ins)
    fwd = cts is None
    out_shape, out_specs, acc = [], [], []
    if fwd:
        for dt, w, stride in outs:
            out_shape.append(jax.ShapeDtypeStruct((rows, w * (ncol if stride else 1)), dt))
            out_specs.append(spec('r', None, w, 0, stride))
            acc.append(None)
    elif cat:
        widths = [ins[i][2] for i in wrt]
        assert ncol == 1 and all(ins[i][0] == 'r' for i in wrt)
        out_shape.append(jax.ShapeDtypeStruct((rows, sum(widths)), gdt))
        out_specs.append(spec('r', None, sum(widths), 0, 0))
    else:
        for n, i in enumerate(wrt):
            kind, arr, w, off, stride = ins[i]
            width = w * (ncol if stride else 1)
            if kind == 'r':
                out_shape.append(jax.ShapeDtypeStruct((rows, width), gdt[n]))
                out_specs.append(spec('r', None, w, 0, 1 if stride else 0))
                shared = stride == 0 and ncol > 1
                assert not shared or (not rows_inner and gdt[n] == F32)
                acc.append('col' if shared else None)
            else:
                out_shape.append(jax.ShapeDtypeStruct((arr.shape[0], width), F32))
                out_specs.append(spec('p', arr, w, 0, 1 if stride else 0))
                assert rows_inner or ncol == 1
                acc.append('row')

    def body(*refs):
        i, j = ij(pl.program_id(0), pl.program_id(1))
        vals = [r[...].astype(F32) for r in refs[:n_in]]
        out_refs = refs[len(ops):]
        if fwd:
            for o_ref, o in zip(out_refs, fn(*vals)):
                o_ref[...] = o.astype(o_ref.dtype)
            return

        def f(*d):
            full = list(vals)
            for n, idx in enumerate(wrt):
                full[idx] = d[n]
            return fn(*full)

        _, vjp = jax.vjp(f, *[vals[idx] for idx in wrt])
        grads = vjp(tuple(r[...].astype(F32) for r in refs[n_in:len(ops)]))
        if cat:
            o_ref, at = out_refs[0], 0
            for g in grads:
                o_ref[:, at:at + g.shape[1]] = g.astype(o_ref.dtype)
                at += g.shape[1]
            return
        for o_ref, g, a in zip(out_refs, grads, acc):
            if a is None:
                o_ref[...] = g.astype(o_ref.dtype)
            else:
                first = (i if a == 'row' else j) == 0

                @pl.when(first)
                def _(o_ref=o_ref):
                    o_ref[...] = jnp.zeros_like(o_ref)

                o_ref[...] += g

    res = pl.pallas_call(
        body, name=name, grid=grid, in_specs=in_specs, out_specs=out_specs, out_shape=out_shape,
        compiler_params=_cparams(),
    )(*[o[1] for o in ops])
    return list(res)


def _rms(x, g):
    return x * lax.rsqrt(jnp.mean(x * x, axis=-1, keepdims=True) + EPS) * g


_DIMS = {'nn': (((1,), (0,)), ((), ())), 'nt': (((1,), (1,)), ((), ())), 'tn': (((0,), (0,)), ((), ()))}


def mm(name, a, b, mode, out_dtype=F32):
    if mode == 'tn':
        k_dim, m_dim = a.shape
    else:
        m_dim, k_dim = a.shape
    n_dim = b.shape[0] if mode == 'nt' else b.shape[1]
    tm = _pick(m_dim, (1024, 512, 256, 128))
    tn = _pick(n_dim, (1024, 512, 256, 128))
    tk = _pick(k_dim, (512, 256, 128))
    nk = k_dim // tk
    a_spec = (pl.BlockSpec((tk, tm), lambda i, j, k: (k, i)) if mode == 'tn'
              else pl.BlockSpec((tm, tk), lambda i, j, k: (i, k)))
    b_spec = (pl.BlockSpec((tn, tk), lambda i, j, k: (j, k)) if mode == 'nt'
              else pl.BlockSpec((tk, tn), lambda i, j, k: (k, j)))
    dims = _DIMS[mode]

    def body(a_ref, b_ref, o_ref, acc_ref):
        k = pl.program_id(2)

        @pl.when(k == 0)
        def _():
            acc_ref[...] = jnp.zeros_like(acc_ref)

        acc_ref[...] += lax.dot_general(a_ref[...].astype(BF16), b_ref[...].astype(BF16), dims,
                                        preferred_element_type=F32)

        @pl.when(k == nk - 1)
        def _():
            o_ref[...] = acc_ref[...].astype(o_ref.dtype)

    return pl.pallas_call(
        body, name=name, grid=(m_dim // tm, n_dim // tn, nk), in_specs=[a_spec, b_spec],
        out_specs=pl.BlockSpec((tm, tn), lambda i, j, k: (i, j)),
        out_shape=jax.ShapeDtypeStruct((m_dim, n_dim), out_dtype),
        scratch_shapes=[pltpu.VMEM((tm, tn), F32)], compiler_params=_cparams(),
    )(a, b)


def _probs(q, k, i, tq, scale, causal):
    s = lax.dot_general(q, k, _DIMS['nt'], preferred_element_type=F32) * scale
    if causal:
        shape = s.shape
        q_chunk = jnp.right_shift(i * tq + lax.broadcasted_iota(jnp.int32, shape, 0), CHUNK_SHIFT)
        k_chunk = jnp.right_shift(lax.broadcasted_iota(jnp.int32, shape, 1), CHUNK_SHIFT)
        s = jnp.where(k_chunk <= q_chunk, s, -jnp.inf)
    e = jnp.exp(s - jnp.max(s, axis=-1, keepdims=True))
    return e, jnp.sum(e, axis=-1, keepdims=True)


def _attn_specs(tq, sk, dq, dv, koff, kstride, voff, vstride):
    return [pl.BlockSpec((tq, dq), lambda h, i: (i, h)),
            pl.BlockSpec((sk, dq), lambda h, i: (0, koff + kstride * h)),
            pl.BlockSpec((sk, dv), lambda h, i: (0, voff + vstride * h))]


def attn_fwd(name, q, k, v, *, heads, dq, dv, koff, kstride, voff, vstride, scale, causal):
    sq, sk = q.shape[0], k.shape[0]
    tq = min(sq, 256)

    def body(q_ref, k_ref, v_ref, o_ref):
        e, l = _probs(q_ref[...].astype(BF16), k_ref[...].astype(BF16), pl.program_id(1), tq, scale, causal)
        o = jnp.dot(e.astype(BF16), v_ref[...].astype(BF16), preferred_element_type=F32)
        o_ref[...] = o / l

    return pl.pallas_call(
        body, name=name, grid=(heads, sq // tq), in_specs=_attn_specs(tq, sk, dq, dv, koff, kstride, voff, vstride),
        out_specs=pl.BlockSpec((tq, dv), lambda h, i: (i, h)),
        out_shape=jax.ShapeDtypeStruct((sq, heads * dv), F32), compiler_params=_cparams(),
    )(q, k, v)


def attn_bwd(name, q, k, v, do, *, heads, dq, dv, koff, kstride, voff, vstride, scale, causal):
    sq, sk = q.shape[0], k.shape[0]
    tq = min(sq, 256)

    def body(q_ref, k_ref, v_ref, do_ref, dq_ref, dk_ref, dv_ref):
        i = pl.program_id(1)
        qb, kb, vb = q_ref[...].astype(BF16), k_ref[...].astype(BF16), v_ref[...].astype(BF16)
        dob = do_ref[...].astype(BF16)
        e, l = _probs(qb, kb, i, tq, scale, causal)
        p = e / l
        dp = lax.dot_general(dob, vb, _DIMS['nt'], preferred_element_type=F32)
        ds = (p * (dp - jnp.sum(dp * p, axis=-1, keepdims=True)) * scale).astype(BF16)

        @pl.when(i == 0)
        def _():
            dk_ref[...] = jnp.zeros_like(dk_ref)
            dv_ref[...] = jnp.zeros_like(dv_ref)

        dv_ref[...] += lax.dot_general(p.astype(BF16), dob, _DIMS['tn'], preferred_element_type=F32)
        dk_ref[...] += lax.dot_general(ds, qb, _DIMS['tn'], preferred_element_type=F32)
        dq_ref[...] = jnp.dot(ds, kb, preferred_element_type=F32)

    return pl.pallas_call(
        body, name=name, grid=(heads, sq // tq),
        in_specs=_attn_specs(tq, sk, dq, dv, koff, kstride, voff, vstride) + [pl.BlockSpec((tq, dv), lambda h, i: (i, h))],
        out_specs=[pl.BlockSpec((tq, dq), lambda h, i: (i, h)), pl.BlockSpec((sk, dq), lambda h, i: (0, h)),
                   pl.BlockSpec((sk, dv), lambda h, i: (0, h))],
        out_shape=[jax.ShapeDtypeStruct((sq, heads * dq), F32), jax.ShapeDtypeStruct((sk, heads * dq), F32),
                   jax.ShapeDtypeStruct((sk, heads * dv), F32)],
        compiler_params=_cparams(),
    )(q, k, v, do)


def _hg_chunk(q, k, g, v, state):
    c = q.shape[0]
    row = lax.broadcasted_iota(jnp.int32, (c, c), 0)
    col = lax.broadcasted_iota(jnp.int32, (c, c), 1)
    tril = (col <= row).astype(F32)
    b = jnp.dot(tril, g, precision=lax.Precision.HIGHEST, preferred_element_type=F32)
    rows = lax.broadcasted_iota(jnp.int32, (c, 1), 0)
    o = jnp.dot((q * jnp.exp(b)).astype(BF16), state.astype(BF16), preferred_element_type=F32)
    t3 = lax.broadcasted_iota(jnp.int32, (SUB, SUB, 1), 0)
    s3 = lax.broadcasted_iota(jnp.int32, (SUB, SUB, 1), 1)
    parts = []
    for n in range(c // SUB):
        lo = n * SUB
        qn, kn, bn, vn = q[lo:lo + SUB], k[lo:lo + SUB], b[lo:lo + SUB], v[lo:lo + SUB]
        decay = jnp.exp(jnp.where(s3 <= t3, bn[:, None, :] - bn[None, :, :], -jnp.inf))
        sc = jnp.sum(qn[:, None, :] * kn[None, :, :] * decay, axis=-1)
        on = jnp.dot(sc.astype(BF16), vn.astype(BF16), preferred_element_type=F32)
        if n > 0:
            ref = jnp.sum(jnp.where(rows == lo - 1, b, 0.0), axis=0, keepdims=True)
            qd = (qn * jnp.exp(bn - ref)).astype(BF16)
            kd = (k[:lo] * jnp.exp(ref - b[:lo])).astype(BF16)
            so = lax.dot_general(qd, kd, _DIMS['nt'], preferred_element_type=F32)
            on = on + jnp.dot(so.astype(BF16), v[:lo].astype(BF16), preferred_element_type=F32)
        parts.append(on)
    o = o + jnp.concatenate(parts, axis=0)
    b_last = jnp.sum(g, axis=0, keepdims=True)
    ones = jnp.ones((c, 1), F32)
    b_last_col = lax.dot_general(g, ones, _DIMS['tn'], precision=lax.Precision.HIGHEST, preferred_element_type=F32)
    kd = (k * jnp.exp(b_last - b)).astype(BF16)
    new_state = jnp.exp(b_last_col) * state + lax.dot_general(kd, v.astype(BF16), _DIMS['tn'],
                                                              preferred_element_type=F32)
    return o, new_state


def hg_scan_fwd(name, q, k, g, u, v_off):
    s = q.shape[0]
    n = s // CHUNK

    def body(q_ref, k_ref, g_ref, v_ref, o_ref, st_ref, state):
        @pl.when(pl.program_id(1) == 0)
        def _():
            state[...] = jnp.zeros_like(state)

        st = state[...]
        st_ref[...] = st
        o, new = _hg_chunk(q_ref[...], k_ref[...], g_ref[...], v_ref[...], st)
        o_ref[...] = o
        state[...] = new

    blk = pl.BlockSpec((CHUNK, HEAD_W), lambda h, c: (c, h))
    return pl.pallas_call(
        body, name=name, grid=(HG_HEADS, n),
        in_specs=[blk, blk, blk, pl.BlockSpec((CHUNK, HEAD_W), lambda h, c: (c, v_off + h))],
        out_specs=[blk, pl.BlockSpec((None, None, HEAD_W, HEAD_W), lambda h, c: (h, c, 0, 0))],
        out_shape=[jax.ShapeDtypeStruct((s, HG_HEADS * HEAD_W), F32),
                   jax.ShapeDtypeStruct((HG_HEADS, n, HEAD_W, HEAD_W), F32)],
        scratch_shapes=[pltpu.VMEM((HEAD_W, HEAD_W), F32)], compiler_params=_cparams(),
    )(q, k, g, u)


def hg_scan_bwd(name, q, k, g, u, v_off, states, do):
    s = q.shape[0]
    n = s // CHUNK

    def body(q_ref, k_ref, g_ref, v_ref, st_ref, do_ref, dq_ref, dk_ref, dg_ref, dv_ref, dstate):
        @pl.when(pl.program_id(1) == 0)
        def _():
            dstate[...] = jnp.zeros_like(dstate)

        _, vjp = jax.vjp(_hg_chunk, q_ref[...], k_ref[...], g_ref[...], v_ref[...], st_ref[...])
        dq, dk, dg, dv, dst = vjp((do_ref[...], dstate[...]))
        dq_ref[...] = dq
        dk_ref[...] = dk
        dg_ref[...] = dg
        dv_ref[...] = dv
        dstate[...] = dst

    blk = pl.BlockSpec((CHUNK, HEAD_W), lambda h, c: (n - 1 - c, h))
    out = jax.ShapeDtypeStruct((s, HG_HEADS * HEAD_W), F32)
    return pl.pallas_call(
        body, name=name, grid=(HG_HEADS, n),
        in_specs=[blk, blk, blk, pl.BlockSpec((CHUNK, HEAD_W), lambda h, c: (n - 1 - c, v_off + h)),
                  pl.BlockSpec((None, None, HEAD_W, HEAD_W), lambda h, c: (h, n - 1 - c, 0, 0)), blk],
        out_specs=[blk, blk, blk, blk], out_shape=[out, out, out, out],
        scratch_shapes=[pltpu.VMEM((HEAD_W, HEAD_W), F32)], compiler_params=_cparams(),
    )(q, k, g, u, states, do)


def _peer(k):
    x, y, c = lax.axis_index("x"), lax.axis_index("y"), lax.axis_index("c")
    px, py, pc = x ^ ((k >> 2) & 1), y ^ ((k >> 1) & 1), c ^ (k & 1)
    return (px, py, pc), 4 * px + 2 * py + pc, 4 * x + 2 * y + c


def all_gather(name, shard):
    def body(x_ref, out_ref, send_sems, recv_sems, local_sem):
        me = _peer(0)[2]
        mine = pltpu.make_async_copy(x_ref, out_ref.at[me], local_sem)
        mine.start()
        sends = []
        for k in range(1, N_DEV):
            to, _, _ = _peer(k)
            cp = pltpu.make_async_remote_copy(src_ref=x_ref, dst_ref=out_ref.at[me], send_sem=send_sems.at[k - 1],
                                              recv_sem=recv_sems.at[k - 1], device_id=to,
                                              device_id_type=pl.DeviceIdType.MESH)
            cp.start()
            sends.append(cp)
        for k in range(1, N_DEV):
            frm, slot, _ = _peer(k)
            pltpu.make_async_remote_copy(src_ref=x_ref, dst_ref=out_ref.at[slot], send_sem=send_sems.at[k - 1],
                                         recv_sem=recv_sems.at[k - 1], device_id=frm,
                                         device_id_type=pl.DeviceIdType.MESH).wait_recv()
        for cp in sends:
            cp.wait_send()
        mine.wait()

    hbm = pl.BlockSpec(memory_space=pltpu.HBM)
    return pl.pallas_call(
        body, name=name, in_specs=[hbm], out_specs=hbm,
        out_shape=jax.ShapeDtypeStruct((N_DEV,) + shard.shape, shard.dtype),
        scratch_shapes=[pltpu.SemaphoreType.DMA((N_DEV - 1,)), pltpu.SemaphoreType.DMA((N_DEV - 1,)),
                        pltpu.SemaphoreType.DMA],
    )(shard)


def exchange(name, blocks):
    def body(x_ref, out_ref, send_sems, recv_sems, local_sem):
        me = _peer(0)[2]
        mine = pltpu.make_async_copy(x_ref.at[me], out_ref.at[me], local_sem)
        mine.start()
        sends = []
        for k in range(1, N_DEV):
            to, slot, _ = _peer(k)
            cp = pltpu.make_async_remote_copy(src_ref=x_ref.at[slot], dst_ref=out_ref.at[me],
                                              send_sem=send_sems.at[k - 1], recv_sem=recv_sems.at[k - 1],
                                              device_id=to, device_id_type=pl.DeviceIdType.MESH)
            cp.start()
            sends.append(cp)
        for k in range(1, N_DEV):
            frm, slot, _ = _peer(k)
            pltpu.make_async_remote_copy(src_ref=x_ref.at[slot], dst_ref=out_ref.at[slot],
                                         send_sem=send_sems.at[k - 1], recv_sem=recv_sems.at[k - 1],
                                         device_id=frm, device_id_type=pl.DeviceIdType.MESH).wait_recv()
        for cp in sends:
            cp.wait_send()
        mine.wait()

    hbm = pl.BlockSpec(memory_space=pltpu.HBM)
    return pl.pallas_call(
        body, name=name, in_specs=[hbm], out_specs=hbm,
        out_shape=jax.ShapeDtypeStruct(blocks.shape, blocks.dtype),
        scratch_shapes=[pltpu.SemaphoreType.DMA((N_DEV - 1,)), pltpu.SemaphoreType.DMA((N_DEV - 1,)),
                        pltpu.SemaphoreType.DMA],
    )(blocks)


def sum_slots(name, buf):
    _, rows, w = buf.shape
    tile = _pick(rows, (256, 128, 64, 32, 16, 8))

    def body(x_ref, o_ref):
        acc = x_ref[0].astype(F32)
        for s in range(1, N_DEV):
            acc = acc + x_ref[s].astype(F32)
        o_ref[...] = acc

    return pl.pallas_call(
        body, name=name, grid=(rows // tile,), in_specs=[pl.BlockSpec((N_DEV, tile, w), lambda i: (0, i, 0))],
        out_specs=pl.BlockSpec((tile, w), lambda i: (i, 0)), out_shape=jax.ShapeDtypeStruct((rows, w), F32),
        compiler_params=_cparams(),
    )(buf)


def adamw(name, g, w, m, v):
    rows, cols = w.shape
    tile = next((t for t in (512, 256, 128, 64, 32, 16, 8) if rows % t == 0 and t * cols * 4 <= (1 << 21)), rows)

    def body(g_ref, w_ref, m_ref, v_ref, d_ref, nm_ref, nv_ref):
        gv = g_ref[...]
        nm = ADAM_B1 * m_ref[...] + (1.0 - ADAM_B1) * gv
        nv = ADAM_B2 * v_ref[...] + (1.0 - ADAM_B2) * jnp.square(gv)
        m_hat = nm / (1.0 - ADAM_B1 ** ADAM_STEP)
        v_hat = nv / (1.0 - ADAM_B2 ** ADAM_STEP)
        d_ref[...] = -ADAM_LR * (m_hat / (jnp.sqrt(v_hat) + ADAM_EPS) + ADAM_WD * w_ref[...])
        nm_ref[...] = nm
        nv_ref[...] = nv

    blk = pl.BlockSpec((tile, cols), lambda i: (i, 0))
    out = jax.ShapeDtypeStruct((rows, cols), F32)
    return pl.pallas_call(
        body, name=name, grid=(rows // tile,), in_specs=[blk] * 4, out_specs=[blk] * 3, out_shape=[out] * 3,
        compiler_params=_cparams(),
    )(g, w, m, v)


def _pack_rows(shape):
    return shape[0] * shape[1] // PACK_W


def _swap_halves(pe):
    half = QK_ROPE // 2
    return jnp.concatenate([-pe[..., half:], pe[..., :half]], axis=-1)


def _unswap_halves(dsw):
    half = QK_ROPE // 2
    return jnp.concatenate([dsw[..., half:], -dsw[..., :half]], axis=-1)


def _w_in_ext(w):
    kpe = w[:, 9216:9280]
    z = jnp.zeros((w.shape[0], HEAD_W - QK_ROPE), w.dtype)
    return jnp.concatenate([w[:, :9216], w[:, 9280:], kpe, z, _swap_halves(kpe), z], axis=1)


def _w_in_grad(d):
    dkpe = d[:, 13312:13376] + _unswap_halves(d[:, 13440:13504])
    return jnp.concatenate([d[:, :9216], dkpe, d[:, 9216:13312]], axis=1)


def _w_q_ext(w):
    w3 = w.reshape(Q_LORA, MLA_HEADS, MLA_QK)
    pe = w3[:, :, HEAD_W:]
    z = jnp.zeros((Q_LORA, MLA_HEADS, HEAD_W - QK_ROPE), w.dtype)
    wide = MLA_HEADS * HEAD_W
    return jnp.concatenate([w3[:, :, :HEAD_W].reshape(Q_LORA, wide),
                            jnp.concatenate([pe, z], axis=2).reshape(Q_LORA, wide),
                            jnp.concatenate([_swap_halves(pe), z], axis=2).reshape(Q_LORA, wide)], axis=1)


def _w_q_grad(d):
    wide = MLA_HEADS * HEAD_W
    d3 = [d[:, n * wide:(n + 1) * wide].reshape(Q_LORA, MLA_HEADS, HEAD_W) for n in range(3)]
    dpe = d3[1][:, :, :QK_ROPE] + _unswap_halves(d3[2][:, :, :QK_ROPE])
    return jnp.concatenate([d3[0], dpe], axis=2).reshape(Q_LORA, MLA_HEADS * MLA_QK)


def _hg_prep(f_raw, q_hg, logits):
    lb = jax.nn.softmax(logits, axis=0)[0:1, :]
    log_f = jnp.logaddexp(jnp.log(lb), jnp.log1p(-lb) + jax.nn.log_sigmoid(f_raw))
    k_in = (1.0 - lb) * jax.nn.sigmoid(-f_raw)
    return log_f, k_in, jax.nn.silu(q_hg)


def _rope(q_nope, q_pe, q_sw, k_nope, k_pe, k_sw, cos, sin):
    qf = jnp.concatenate([q_nope, q_pe * cos + q_sw * sin], axis=1)
    kf = jnp.concatenate([k_nope, k_pe * cos + k_sw * sin], axis=1)
    return qf, kf


def _step(a):
    x, mem, target = a['x'][0], a['mem'][0], a['loss_target'][0]
    s, d = x.shape
    nm = mem.shape[0]
    ff = N_DEV * a['ffn1_w_gate'].shape[-1]
    tr = min(s, 128)
    th = min(s, 1024)

    shards = {n: a[n][0] for n in BIG}
    seg, at = {}, 0
    for n in BIG:
        seg[n] = (at, _pack_rows(shards[n].shape))
        at += seg[n][1]
    packed = jnp.concatenate([shards[n].astype(BF16).reshape(-1, PACK_W) for n in BIG], axis=0)
    gathered = all_gather("gather_w", packed)
    full = {}
    for n, by_col in BIG.items():
        r, c = shards[n].shape
        blk = gathered[:, seg[n][0]:seg[n][0] + seg[n][1], :].reshape(N_DEV, r, c)
        full[n] = blk.transpose(1, 0, 2).reshape(r, N_DEV * c) if by_col else blk.reshape(N_DEV * r, c)
    w_gu = {t: jnp.concatenate([full[t + '_w_gate'], full[t + '_w_up']], axis=1) for t in ('ffn1', 'ffn2')}
    w_in = _w_in_ext(full['w_in'])
    w_q = _w_q_ext(full['mla_w_q_up'])
    w_kv = full['mla_w_kv_up']
    w_xkv = jnp.concatenate([full['xa_w_k'], full['xa_w_v']], axis=1)

    inv_freq = 1.0 / (ROPE_THETA ** (jnp.arange(0, QK_ROPE, 2, dtype=F32) / QK_ROPE))
    ang = a['positions'][0].astype(F32)[:, None] * inv_freq
    zero = jnp.zeros((s, HEAD_W - QK_ROPE), F32)
    cos = jnp.concatenate([jnp.cos(ang), jnp.cos(ang), zero], axis=1)
    sin = jnp.concatenate([jnp.sin(ang), jnp.sin(ang), zero], axis=1)

    gs = {}

    pre_fn = lambda xv, g: (_rms(xv, g),)
    pre_res_fn = lambda xv, g: (_rms(xv, g), xv)

    def pre_norm(tag, x_in, g):
        return rowmap(tag + "_pre", pre_fn, [R_(x_in), P_(g)], [(BF16, x_in.shape[1], 0)], tile=tr)[0]

    def pre_norm_bwd(tag, x_in, g, dh, d_out):
        return rowmap(tag + "_pre_bwd", pre_res_fn, [R_(x_in), P_(g)], tile=tr, cts=[R_(dh), R_(d_out)],
                      wrt=[0, 1], gdt=[F32, F32])

    def post_fn(weight):
        return lambda xv, y, g: (xv + weight * _rms(y, g),)

    def post_norm(tag, x_in, y, g, weight):
        return rowmap(tag + "_post", post_fn(weight), [R_(x_in), R_(y), P_(g)], [(F32, d, 0)], tile=tr)[0]

    def post_norm_bwd(tag, x_in, y, g, weight, d_out):
        return rowmap(tag + "_post_bwd", post_fn(weight), [R_(x_in), R_(y), P_(g)], tile=tr, cts=[R_(d_out)],
                      wrt=[1, 2], gdt=[BF16, F32])

    act_fn = lambda av, bv: (jax.nn.silu(av) * bv,)

    def ffn_fwd(tag, x_in):
        h = pre_norm(tag, x_in, a[tag + '_pre_g'])
        ab = mm(tag + "_gu", h, w_gu[tag], 'nn')
        z = rowmap(tag + "_act", act_fn, [R_(ab, ff, 0), R_(ab, ff, 1)], [(BF16, ff, 0)], tile=tr)[0]
        y = mm(tag + "_dn", z, full[tag + '_w_down'], 'nn')
        return post_norm(tag, x_in, y, a[tag + '_post_g'], 0.5), (x_in, h, ab, z, y)

    def ffn_bwd(tag, res, d_out):
        x_in, h, ab, z, y = res
        dy, gs[tag + '_post_g'] = post_norm_bwd(tag, x_in, y, a[tag + '_post_g'], 0.5, d_out)
        dz = mm(tag + "_dn_dx", dy, full[tag + '_w_down'], 'nt')
        gs[tag + '_w_down'] = mm(tag + "_dn_dw", z, dy, 'tn')
        dab = rowmap(tag + "_act_bwd", act_fn, [R_(ab, ff, 0), R_(ab, ff, 1)], tile=tr, cts=[R_(dz)],
                     wrt=[0, 1], gdt=BF16, cat=True)[0]
        dh = mm(tag + "_gu_dx", dab, w_gu[tag], 'nt')
        dw = mm(tag + "_gu_dw", h, dab, 'tn')
        gs[tag + '_w_gate'], gs[tag + '_w_up'] = dw[:, :ff], dw[:, ff:]
        d_in, gs[tag + '_pre_g'] = pre_norm_bwd(tag, x_in, a[tag + '_pre_g'], dh, d_out)
        return d_in

    hg_out_fn = lambda o, og, g: (_rms(o, g) * jax.nn.silu(og),)
    mla_norm_fn = lambda cq, ckv, gq, gkv: (_rms(cq, gq), _rms(ckv, gkv))
    gate_fn = lambda ga, gb, ya, yb: (jax.nn.sigmoid(ga) * ya + jax.nn.sigmoid(gb) * yb,)
    mla = dict(heads=MLA_HEADS, dq=2 * HEAD_W, dv=HEAD_W, koff=0, kstride=1, voff=1, vstride=2,
               scale=MLA_QK ** -0.5, causal=True)

    def mix_fwd(x_in):
        h = pre_norm("mix", x_in, a['mix_pre_g'])
        u = mm("mix_in", h, w_in, 'nn')
        hg_ins = [R_(u, 2048, 1), R_(u, 2048, 0), P_(a['hgrn_lb_logits'])]
        log_f, k_in, q_in = rowmap("hg_prep", _hg_prep, hg_ins, [(F32, 2048, 0)] * 3, tile=tr)
        o_a, states = hg_scan_fwd("hg_scan", q_in, k_in, log_f, u, 32)
        out_ins = [R_(o_a, HEAD_W, 0, 1), R_(u, HEAD_W, 48, 1), P_(a['hg_norm_g'], HEAD_W, 0, 1)]
        oag = rowmap("hg_out", hg_out_fn, out_ins, [(BF16, HEAD_W, 1)], tile=th, ncol=HG_HEADS)[0]
        y_a = mm("mix_a", oag, full['w_branch_a'], 'nn')
        norm_ins = [R_(u, 512, 16), R_(u, 512, 17), P_(a['mla_q_norm_g']), P_(a['mla_kv_norm_g'])]
        cqn, ckvn = rowmap("mla_norm", mla_norm_fn, norm_ins, [(BF16, 512, 0)] * 2, tile=tr)
        q_all = mm("mla_qup", cqn, w_q, 'nn')
        kv = mm("mla_kvup", ckvn, w_kv, 'nn')
        rope_ins = [R_(q_all, HEAD_W, 0, 1), R_(q_all, HEAD_W, 16, 1), R_(q_all, HEAD_W, 32, 1),
                    R_(kv, HEAD_W, 0, 2), R_(u, HEAD_W, 104, 0), R_(u, HEAD_W, 105, 0), R_(cos), R_(sin)]
        qf, kf = rowmap("mla_rope", _rope, rope_ins, [(BF16, 2 * HEAD_W, 1)] * 2, tile=th, ncol=MLA_HEADS,
                        rows_inner=False)
        o_b = attn_fwd("mla_attn", qf, kf, kv, **mla)
        y_b = mm("mix_b", o_b, full['w_branch_b'], 'nn')
        gate_ins = [R_(u, 1024, 9, 1), R_(u, 1024, 11, 1), R_(y_a, 1024, 0, 1), R_(y_b, 1024, 0, 1)]
        y = rowmap("mix_gate", gate_fn, gate_ins, [(BF16, 1024, 1)], tile=tr, ncol=2)[0]
        yo = mm("mix_out", y, full['w_out'], 'nn')
        res = (x_in, h, hg_ins, q_in, k_in, log_f, u, states, out_ins, oag, norm_ins, cqn, ckvn, rope_ins, qf, kf,
               kv, o_b, gate_ins, y, yo)
        return post_norm("mix", x_in, yo, a['mix_post_g'], 1.0), res

    def mix_bwd(res, d_out):
        (x_in, h, hg_ins, q_in, k_in, log_f, u, states, out_ins, oag, norm_ins, cqn, ckvn, rope_ins, qf, kf, kv,
         o_b, gate_ins, y, yo) = res
        dyo, gs['mix_post_g'] = post_norm_bwd("mix", x_in, yo, a['mix_post_g'], 1.0, d_out)
        dy = mm("mix_out_dx", dyo, full['w_out'], 'nt')
        gs['w_out'] = mm("mix_out_dw", y, dyo, 'tn')
        dga, dgb, dya, dyb = rowmap("mix_gate_bwd", gate_fn, gate_ins, tile=tr, ncol=2, cts=[R_(dy, 1024, 0, 1)],
                                    wrt=[0, 1, 2, 3], gdt=[BF16] * 4)
        gs['w_branch_b'] = mm("mix_b_dw", o_b, dyb, 'tn')
        do_b = mm("mix_b_dx", dyb, full['w_branch_b'], 'nt')
        dqf, dkf, dv = attn_bwd("mla_attn_bwd", qf, kf, kv, do_b, **mla)
        dqn, dqp, dqs, dkn, dkpe, dksw = rowmap(
            "mla_rope_bwd", _rope, rope_ins, tile=th, ncol=MLA_HEADS, rows_inner=False,
            cts=[R_(dqf, 2 * HEAD_W, 0, 1), R_(dkf, 2 * HEAD_W, 0, 1)], wrt=[0, 1, 2, 3, 4, 5],
            gdt=[BF16, BF16, BF16, BF16, F32, F32])
        dq_all = jnp.concatenate([dqn, dqp, dqs], axis=1)
        dkv = jnp.concatenate([dkn.reshape(s, MLA_HEADS, HEAD_W), dv.astype(BF16).reshape(s, MLA_HEADS, HEAD_W)],
                              axis=2).reshape(s, 2 * MLA_HEADS * HEAD_W)
        gs['mla_w_q_up'] = _w_q_grad(mm("mla_qup_dw", cqn, dq_all, 'tn'))
        dcqn = mm("mla_qup_dx", dq_all, w_q, 'nt')
        gs['mla_w_kv_up'] = mm("mla_kvup_dw", ckvn, dkv, 'tn')
        dckvn = mm("mla_kvup_dx", dkv, w_kv, 'nt')
        dcq, dckv, gs['mla_q_norm_g'], gs['mla_kv_norm_g'] = rowmap(
            "mla_norm_bwd", mla_norm_fn, norm_ins, tile=tr, cts=[R_(dcqn), R_(dckvn)], wrt=[0, 1, 2, 3],
            gdt=[BF16, BF16, F32, F32])
        gs['w_branch_a'] = mm("mix_a_dw", oag, dya, 'tn')
        doag = mm("mix_a_dx", dya, full['w_branch_a'], 'nt')
        do_a, dog, gs['hg_norm_g'] = rowmap("hg_out_bwd", hg_out_fn, out_ins, tile=th, ncol=HG_HEADS,
                                            cts=[R_(doag, HEAD_W, 0, 1)], wrt=[0, 1, 2], gdt=[F32, BF16, F32])
        dq_in, dk_in, dlog_f, di = hg_scan_bwd("hg_scan_bwd", q_in, k_in, log_f, u, 32, states, do_a)
        df, dq_hg, gs['hgrn_lb_logits'] = rowmap("hg_prep_bwd", _hg_prep, hg_ins, tile=tr,
                                                 cts=[R_(dlog_f), R_(dk_in), R_(dq_in)], wrt=[0, 1, 2],
                                                 gdt=[BF16, BF16, F32])
        du = jnp.concatenate([dq_hg, df, di.astype(BF16), dog, dcq, dckv, dga, dgb, dkpe.astype(BF16),
                              dksw.astype(BF16)], axis=1)
        gs['w_in'] = _w_in_grad(mm("mix_in_dw", h, du, 'tn'))
        dh = mm("mix_in_dx", du, w_in, 'nt')
        d_in, gs['mix_pre_g'] = pre_norm_bwd("mix", x_in, a['mix_pre_g'], dh, d_out)
        return d_in

    xa = dict(heads=XA_HEADS, dq=HEAD_W, dv=HEAD_W, koff=0, kstride=1, voff=XA_HEADS, vstride=1,
              scale=HEAD_W ** -0.5, causal=False)
    tm_ = min(nm, 128)

    def xa_fwd(x_in):
        h = pre_norm("xa", x_in, a['xa_pre_g'])
        mn = rowmap("xa_mem", pre_fn, [R_(mem), P_(a['xa_mem_g'])], [(BF16, d, 0)], tile=tm_)[0]
        q = mm("xa_q", h, full['xa_w_q'], 'nn')
        kv = mm("xa_kv", mn, w_xkv, 'nn')
        o = attn_fwd("xa_attn", q, kv, kv, **xa)
        yo = mm("xa_o", o, full['xa_w_o'], 'nn')
        return post_norm("xa", x_in, yo, a['xa_post_g'], 1.0), (x_in, h, mn, q, kv, o, yo)

    def xa_bwd(res, d_out):
        x_in, h, mn, q, kv, o, yo = res
        dyo, gs['xa_post_g'] = post_norm_bwd("xa", x_in, yo, a['xa_post_g'], 1.0, d_out)
        do = mm("xa_o_dx", dyo, full['xa_w_o'], 'nt')
        gs['xa_w_o'] = mm("xa_o_dw", o, dyo, 'tn')
        dq, dk, dv = attn_bwd("xa_attn_bwd", q, kv, kv, do, **xa)
        dkv = jnp.concatenate([dk, dv], axis=1).astype(BF16)
        dw = mm("xa_kv_dw", mn, dkv, 'tn')
        gs['xa_w_k'], gs['xa_w_v'] = dw[:, :XA_HEADS * HEAD_W], dw[:, XA_HEADS * HEAD_W:]
        dmn = mm("xa_kv_dx", dkv, w_xkv, 'nt')
        gs['xa_mem_g'] = rowmap("xa_mem_bwd", pre_fn, [R_(mem), P_(a['xa_mem_g'])], tile=tm_, cts=[R_(dmn)],
                                wrt=[1], gdt=[F32])[0]
        gs['xa_w_q'] = mm("xa_q_dw", h, dq, 'tn')
        dh = mm("xa_q_dx", dq, full['xa_w_q'], 'nt')
        d_in, gs['xa_pre_g'] = pre_norm_bwd("xa", x_in, a['xa_pre_g'], dh, d_out)
        return d_in

    x1, r1 = ffn_fwd('ffn1', x)
    x2, r2 = mix_fwd(x1)
    x3, r3 = xa_fwd(x2)
    x4, r4 = ffn_fwd('ffn2', x3)

    def loss_fn(y, t):
        diff = y - t
        return diff * (1.0 / d), jnp.mean(diff * diff, axis=-1, keepdims=True)

    d4, row_loss = rowmap("loss", loss_fn, [R_(x4), R_(target)], [(F32, d, 0), (F32, 1, 0)], tile=tr)
    loss = lax.psum(0.5 * jnp.sum(row_loss), ("x", "y", "c"))
    d3 = ffn_bwd('ffn2', r4, d4)
    d2 = xa_bwd(r3, d3)
    d1 = mix_bwd(r2, d2)
    grad_x = ffn_bwd('ffn1', r1, d1)

    blocks = []
    for n, by_col in BIG.items():
        r, c = shards[n].shape
        g = gs[n]
        g = g.reshape(r, N_DEV, c).transpose(1, 0, 2) if by_col else g
        blocks.append(g.astype(BF16).reshape(N_DEV, seg[n][1], PACK_W))
    landed = exchange("exchange_g", jnp.concatenate(blocks, axis=1))
    g_big = sum_slots("sum_g", landed)

    def pack_small(vals):
        flat = jnp.concatenate([vals[n].reshape(-1) for n in SMALL])
        rows = -(-flat.shape[0] // PACK_W)
        rows = -(-rows // 8) * 8
        return jnp.pad(flat, (0, rows * PACK_W - flat.shape[0])).reshape(rows, PACK_W)

    def unpack_small(buf):
        flat, out, at = buf.reshape(-1), {}, 0
        for n in SMALL:
            size = a[n].shape[0] * a[n].shape[1]
            out[n] = flat[at:at + size].reshape(a[n].shape)
            at += size
        return out

    g_small = sum_slots("sum_g_small", all_gather("gather_g_small", pack_small(gs)))

    grads, delta, new_m, new_v = {}, {}, {}, {}
    d_s, m_s, v_s = adamw("adamw_small", g_small, pack_small(a), pack_small({n: a['m_' + n] for n in SMALL}),
                          pack_small({n: a['v_' + n] for n in SMALL}))
    for dst, buf in ((grads, g_small), (delta, d_s), (new_m, m_s), (new_v, v_s)):
        dst.update(unpack_small(buf))
    for n in BIG:
        shape = a[n].shape
        g = g_big[seg[n][0]:seg[n][0] + seg[n][1]].reshape(shape[1:])
        dl, m2, v2 = adamw("adamw_" + n, g, a[n][0], a['m_' + n][0], a['v_' + n][0])
        grads[n], delta[n], new_m[n], new_v[n] = (t.reshape(shape) for t in (g, dl, m2, v2))

    return (loss, grad_x[None], *[grads[n] for n in WEIGHTS], *[delta[n] for n in WEIGHTS],
            *[new_m[n] for n in WEIGHTS], *[new_v[n] for n in WEIGHTS])


def kernel(x, mem, positions, hgrn_lb_logits, ffn1_pre_g, ffn1_w_gate, ffn1_w_up, ffn1_w_down, ffn1_post_g, mix_pre_g, w_in, hg_norm_g, mla_q_norm_g, mla_w_q_up, mla_kv_norm_g, mla_w_kv_up, w_branch_a, w_branch_b, w_out, mix_post_g, xa_pre_g, xa_mem_g, xa_w_q, xa_w_k, xa_w_v, xa_w_o, xa_post_g, ffn2_pre_g, ffn2_w_gate, ffn2_w_up, ffn2_w_down, ffn2_post_g, loss_target, m_hgrn_lb_logits, m_ffn1_pre_g, m_ffn1_w_gate, m_ffn1_w_up, m_ffn1_w_down, m_ffn1_post_g, m_mix_pre_g, m_w_in, m_hg_norm_g, m_mla_q_norm_g, m_mla_w_q_up, m_mla_kv_norm_g, m_mla_w_kv_up, m_w_branch_a, m_w_branch_b, m_w_out, m_mix_post_g, m_xa_pre_g, m_xa_mem_g, m_xa_w_q, m_xa_w_k, m_xa_w_v, m_xa_w_o, m_xa_post_g, m_ffn2_pre_g, m_ffn2_w_gate, m_ffn2_w_up, m_ffn2_w_down, m_ffn2_post_g, v_hgrn_lb_logits, v_ffn1_pre_g, v_ffn1_w_gate, v_ffn1_w_up, v_ffn1_w_down, v_ffn1_post_g, v_mix_pre_g, v_w_in, v_hg_norm_g, v_mla_q_norm_g, v_mla_w_q_up, v_mla_kv_norm_g, v_mla_w_kv_up, v_w_branch_a, v_w_branch_b, v_w_out, v_mix_post_g, v_xa_pre_g, v_xa_mem_g, v_xa_w_q, v_xa_w_k, v_xa_w_v, v_xa_w_o, v_xa_post_g, v_ffn2_pre_g, v_ffn2_w_gate, v_ffn2_w_up, v_ffn2_w_down, v_ffn2_post_g):
    return _step(dict(locals()))
```

```python
import functools

import jax
import jax.numpy as jnp
from jax import lax
from jax.experimental import pallas as pl
from jax.experimental.pallas import tpu as pltpu

F32 = jnp.float32
BF16 = jnp.bfloat16

N_DEV = 8
D_MODEL = 2048
CHUNK = 64
CHUNK_SHIFT = 6
SUB = 16
HG_HEADS = 16
HEAD_W = 128
MLA_HEADS = 16
Q_LORA = 512
KV_LORA = 512
QK_ROPE = 64
MLA_QK = 192
XA_HEADS = 4
ROPE_THETA = 10000.0
EPS = 1e-6
PACK_W = 1024
VMEM_LIMIT = 56 * 1024 * 1024

ADAM_LR = 0.001
ADAM_B1 = 0.9
ADAM_B2 = 0.999
ADAM_EPS = 1e-08
ADAM_WD = 0.01
ADAM_STEP = 10

U_PAD = 256

WEIGHTS = ['hgrn_lb_logits', 'ffn1_pre_g', 'ffn1_w_gate', 'ffn1_w_up', 'ffn1_w_down', 'ffn1_post_g', 'mix_pre_g',
           'w_in', 'hg_norm_g', 'mla_q_norm_g', 'mla_w_q_up', 'mla_kv_norm_g', 'mla_w_kv_up', 'w_branch_a',
           'w_branch_b', 'w_out', 'mix_post_g', 'xa_pre_g', 'xa_mem_g', 'xa_w_q', 'xa_w_k', 'xa_w_v', 'xa_w_o',
           'xa_post_g', 'ffn2_pre_g', 'ffn2_w_gate', 'ffn2_w_up', 'ffn2_w_down', 'ffn2_post_g']
BIG = {'ffn1_w_gate': True, 'ffn1_w_up': True, 'ffn1_w_down': False, 'w_in': True, 'mla_w_q_up': True,
       'mla_w_kv_up': True, 'w_branch_a': False, 'w_branch_b': False, 'w_out': False, 'xa_w_q': False,
       'xa_w_k': False, 'xa_w_v': False, 'xa_w_o': True, 'ffn2_w_gate': True, 'ffn2_w_up': True,
       'ffn2_w_down': False}
SMALL = [n for n in WEIGHTS if n not in BIG]


def _cparams(**kw):
    return pltpu.CompilerParams(vmem_limit_bytes=VMEM_LIMIT, **kw)


def _pick(dim, cands):
    for c in cands:
        if dim % c == 0:
            return c
    return dim


def _place():
    return lax.axis_index("x"), lax.axis_index("y"), lax.axis_index("c")


def _slot(px, py, pc):
    return 4 * px + 2 * py + pc


class Gather:
    def __init__(self, tensors):
        self.operands = list(tensors)
        self.out_shape = [jax.ShapeDtypeStruct((N_DEV,) + t.shape, t.dtype) for t in tensors]
        n = len(tensors)
        self.scratch = [pltpu.SemaphoreType.DMA((n, N_DEV - 1)), pltpu.SemaphoreType.DMA((n, N_DEV - 1)),
                        pltpu.SemaphoreType.DMA((n,))]
        self.results = None

    def _copies(self, t, x_ref, out_ref, send, recv):
        x, y, c = _place()
        chips = [(1 - x, y), (x, 1 - y), (1 - x, 1 - y)]

        def copy(k, block, to, src=None):
            rows = out_ref.at[_slot(*block)]
            return pltpu.make_async_remote_copy(src_ref=rows if src is None else src, dst_ref=rows,
                                                send_sem=send.at[t, k], recv_sem=recv.at[t, k], device_id=to,
                                                device_id_type=pl.DeviceIdType.MESH)

        return (x, y, c), chips, copy

    def start(self, ins, outs, scr):
        send, recv, local = scr
        for t, (x_ref, out_ref) in enumerate(zip(ins, outs)):
            (x, y, c), chips, copy = self._copies(t, x_ref, out_ref, send, recv)
            pltpu.make_async_copy(x_ref, out_ref.at[_slot(x, y, c)], local.at[t]).start()
            copy(0, (x, y, c), (x, y, 1 - c), src=x_ref).start()
            for j, chip in enumerate(chips):
                copy(1 + j, (x, y, c), (*chip, c), src=x_ref).start()

    def finish(self, ins, outs, scr):
        send, recv, local = scr
        for t, (x_ref, out_ref) in enumerate(zip(ins, outs)):
            (x, y, c), chips, copy = self._copies(t, x_ref, out_ref, send, recv)
            passed = [copy(4 + j, (*chip, c), (x, y, 1 - c)) for j, chip in enumerate(chips)]
            for j, chip in enumerate(chips):
                copy(1 + j, (*chip, c), (x, y, c)).wait_recv()
                passed[j].start()
            copy(0, (x, y, 1 - c), (x, y, c)).wait_recv()
            for j, chip in enumerate(chips):
                copy(4 + j, (*chip, 1 - c), (x, y, c)).wait_recv()
            copy(0, (x, y, c), (x, y, 1 - c), src=x_ref).wait_send()
            for j, chip in enumerate(chips):
                copy(1 + j, (x, y, c), (*chip, c), src=x_ref).wait_send()
                passed[j].wait_send()
            pltpu.make_async_copy(x_ref, out_ref.at[_slot(x, y, c)], local.at[t]).wait()


class Exchange:
    def __init__(self, tensors):
        self.operands = list(tensors)
        self.out_shape = [jax.ShapeDtypeStruct(t.shape, t.dtype) for t in tensors]
        n = len(tensors)
        self.scratch = [pltpu.SemaphoreType.DMA((n, N_DEV - 1)), pltpu.SemaphoreType.DMA((n, N_DEV - 1)),
                        pltpu.SemaphoreType.DMA((n,))]
        self.results = None

    def _copy(self, t, k, x_ref, out_ref, send, recv, outgoing):
        x, y, c = _place()
        peer = (x ^ ((k >> 2) & 1), y ^ ((k >> 1) & 1), c ^ (k & 1))
        me_slot, peer_slot = _slot(x, y, c), _slot(*peer)
        return pltpu.make_async_remote_copy(
            src_ref=x_ref.at[peer_slot], dst_ref=out_ref.at[me_slot if outgoing else peer_slot],
            send_sem=send.at[t, k - 1], recv_sem=recv.at[t, k - 1], device_id=peer,
            device_id_type=pl.DeviceIdType.MESH)

    def _mine(self, t, x_ref, out_ref, local):
        me = _slot(*_place())
        return pltpu.make_async_copy(x_ref.at[me], out_ref.at[me], local.at[t])

    def start(self, ins, outs, scr):
        send, recv, local = scr
        for t, (x_ref, out_ref) in enumerate(zip(ins, outs)):
            self._mine(t, x_ref, out_ref, local).start()
            for k in range(1, N_DEV):
                self._copy(t, k, x_ref, out_ref, send, recv, True).start()

    def finish(self, ins, outs, scr):
        send, recv, local = scr
        for t, (x_ref, out_ref) in enumerate(zip(ins, outs)):
            for k in range(1, N_DEV):
                self._copy(t, k, x_ref, out_ref, send, recv, False).wait_recv()
            for k in range(1, N_DEV):
                self._copy(t, k, x_ref, out_ref, send, recv, True).wait_send()
            self._mine(t, x_ref, out_ref, local).wait()


_HBM = pl.BlockSpec(memory_space=pltpu.HBM)


def run_alone(name, carry):
    n_in, n_out = len(carry.operands), len(carry.out_shape)

    def body(*refs):
        ins, outs, scr = refs[:n_in], refs[n_in:n_in + n_out], refs[n_in + n_out:]
        carry.start(ins, outs, scr)
        carry.finish(ins, outs, scr)

    res = pl.pallas_call(body, name=name, in_specs=[_HBM] * n_in, out_specs=[_HBM] * n_out,
                         out_shape=carry.out_shape, scratch_shapes=carry.scratch)(*carry.operands)
    carry.results = list(res)
    return carry.results


def _call(name, body, *, grid, in_specs, out_specs, out_shape, args, scratch=(), carry=None):
    in_specs, out_specs, out_shape, scratch = list(in_specs), list(out_specs), list(out_shape), list(scratch)
    if carry is None:
        return list(pl.pallas_call(body, name=name, grid=grid, in_specs=in_specs, out_specs=out_specs,
                                   out_shape=out_shape, scratch_shapes=scratch, compiler_params=_cparams())(*args))
    n_in, n_out, n_scr = len(in_specs), len(out_shape), len(scratch)
    c_in, c_out = len(carry.operands), len(carry.out_shape)

    def wrapped(*refs):
        ins, c_ins = refs[:n_in], refs[n_in:n_in + c_in]
        at = n_in + c_in
        outs, c_outs = refs[at:at + n_out], refs[at + n_out:at + n_out + c_out]
        at += n_out + c_out
        scr, c_scr = refs[at:at + n_scr], refs[at + n_scr:]
        first = functools.reduce(jnp.logical_and, [pl.program_id(d) == 0 for d in range(len(grid))])
        last = functools.reduce(jnp.logical_and, [pl.program_id(d) == grid[d] - 1 for d in range(len(grid))])

        @pl.when(first)
        def _():
            carry.start(c_ins, c_outs, c_scr)

        body(*ins, *outs, *scr)

        @pl.when(last)
        def _():
            carry.finish(c_ins, c_outs, c_scr)

    res = pl.pallas_call(
        wrapped, name=name, grid=grid, in_specs=in_specs + [_HBM] * c_in, out_specs=out_specs + [_HBM] * c_out,
        out_shape=out_shape + carry.out_shape, scratch_shapes=scratch + carry.scratch, compiler_params=_cparams(),
    )(*args, *carry.operands)
    carry.results = list(res[n_out:])
    return list(res[:n_out])


def R_(arr, w=None, off=0, stride=0):
    return ('r', arr, arr.shape[1] if w is None else w, off, stride)


def P_(arr, w=None, off=0, stride=0):
    return ('p', arr, arr.shape[1] if w is None else w, off, stride)


def rowmap(name, fn, ins, outs=None, *, tile, ncol=1, rows_inner=True, cts=None, wrt=None, gdt=None, cat=False,
           carry=None):
    rows = next(a.shape[0] for k, a, *_ in ins if k == 'r')
    nrow = rows // tile
    assert nrow * tile == rows
    grid = (ncol, nrow) if rows_inner else (nrow, ncol)

    def ij(g0, g1):
        return (g1, g0) if rows_inner else (g0, g1)

    def spec(kind, arr, w, off, stride):
        if kind == 'r':
            return pl.BlockSpec((tile, w), lambda g0, g1: (ij(g0, g1)[0], off + stride * ij(g0, g1)[1]))
        return pl.BlockSpec((arr.shape[0], w), lambda g0, g1: (0, off + stride * ij(g0, g1)[1]))

    ops = list(ins) + list(cts or [])
    in_specs = [spec(*o) for o in ops]
    n_in = len(ins)
    fwd = cts is None
    out_shape, out_specs, acc = [], [], []
    if fwd:
        for dt, w, stride in outs:
            out_shape.append(jax.ShapeDtypeStruct((rows, w * (ncol if stride else 1)), dt))
            out_specs.append(spec('r', None, w, 0, stride))
            acc.append(None)
    elif cat:
        widths = [ins[i][2] for i in wrt]
        assert ncol == 1 and all(ins[i][0] == 'r' for i in wrt)
        out_shape.append(jax.ShapeDtypeStruct((rows, sum(widths)), gdt))
        out_specs.append(spec('r', None, sum(widths), 0, 0))
    else:
        for n, i in enumerate(wrt):
            kind, arr, w, off, stride = ins[i]
            width = w * (ncol if stride else 1)
            if kind == 'r':
                out_shape.append(jax.ShapeDtypeStruct((rows, width), gdt[n]))
                out_specs.append(spec('r', None, w, 0, 1 if stride else 0))
                shared = stride == 0 and ncol > 1
                assert not shared or (not rows_inner and gdt[n] == F32)
                acc.append('col' if shared else None)
            else:
                out_shape.append(jax.ShapeDtypeStruct((arr.shape[0], width), F32))
                out_specs.append(spec('p', arr, w, 0, 1 if stride else 0))
                assert rows_inner or ncol == 1
                acc.append('row')

    def body(*refs):
        i, j = ij(pl.program_id(0), pl.program_id(1))
        vals = [r[...].astype(F32) for r in refs[:n_in]]
        out_refs = refs[len(ops):]
        if fwd:
            for o_ref, o in zip(out_refs, fn(*vals)):
                o_ref[...] = o.astype(o_ref.dtype)
            return

        def f(*d):
            full = list(vals)
            for n, idx in enumerate(wrt):
                full[idx] = d[n]
            return fn(*full)

        _, vjp = jax.vjp(f, *[vals[idx] for idx in wrt])
        grads = vjp(tuple(r[...].astype(F32) for r in refs[n_in:len(ops)]))
        if cat:
            o_ref, at = out_refs[0], 0
            for g in grads:
                o_ref[:, at:at + g.shape[1]] = g.astype(o_ref.dtype)
                at += g.shape[1]
            return
        for o_ref, g, a in zip(out_refs, grads, acc):
            if a is None:
                o_ref[...] = g.astype(o_ref.dtype)
            else:
                first = (i if a == 'row' else j) == 0

                @pl.when(first)
                def _(o_ref=o_ref):
                    o_ref[...] = jnp.zeros_like(o_ref)

                o_ref[...] += g

    return _call(name, body, grid=grid, in_specs=in_specs, out_specs=out_specs, out_shape=out_shape,
                 args=[o[1] for o in ops], carry=carry)


def _rms(x, g):
    return x * lax.rsqrt(jnp.mean(x * x, axis=-1, keepdims=True) + EPS) * g


_DIMS = {'nn': (((1,), (0,)), ((), ())), 'nt': (((1,), (1,)), ((), ())), 'tn': (((0,), (0,)), ((), ()))}


def mm(name, a, b, mode, *, jdim=None, out_dtype=F32, carry=None):
    a_list = list(a) if isinstance(a, (list, tuple)) else [a]
    b_list = list(b) if isinstance(b, (list, tuple)) else [b]
    a_order = ('k', 'm') if mode == 'tn' else ('m', 'k')
    b_order = ('n', 'k') if mode == 'nt' else ('k', 'n')
    size, blocks = {}, 1
    for arr, order in ((a_list[0], a_order), (b_list[0], b_order)):
        shape = arr.shape
        if jdim in order:
            blocks, shape = shape[0], shape[1:]
        for dname, extent in zip(order, shape):
            assert size.setdefault(dname, extent) == extent
    tile = {'m': _pick(size['m'], (1024, 512, 256, 128)), 'n': _pick(size['n'], (512, 256, 128)),
            'k': size['k'] if size['k'] <= 2048 else _pick(size['k'], (1024, 512, 256, 128))}
    if jdim is not None:
        tile[jdim] = size[jdim]
    if a_list[0].dtype == F32 and tile['k'] * tile['m'] > (1 << 20):
        tile['m'] = _pick(size['m'], (512, 256, 128))
    grid = tuple(blocks if d == jdim else size[d] // tile[d] for d in ('m', 'n', 'k'))
    nk = grid[2]

    def spec(order):
        shape = tuple(tile[d] for d in order)

        def imap(i, j, k):
            g = {'m': i, 'n': j, 'k': k}
            idx = tuple(0 if d == jdim else g[d] for d in order)
            return ((g[jdim],) + idx) if jdim in order else idx

        return pl.BlockSpec(((None,) + shape) if jdim in order else shape, imap)

    dims = _DIMS[mode]
    nt = len(a_list)

    def product(refs):
        acc = None
        for a_ref, b_ref in zip(refs[:nt], refs[nt:2 * nt]):
            p = lax.dot_general(a_ref[...].astype(BF16), b_ref[...].astype(BF16), dims, preferred_element_type=F32)
            acc = p if acc is None else acc + p
        return acc

    def body_once(*refs):
        refs[2 * nt][...] = product(refs).astype(refs[2 * nt].dtype)

    def body_acc(*refs):
        o_ref, acc_ref = refs[2 * nt], refs[2 * nt + 1]
        k = pl.program_id(2)

        @pl.when(k == 0)
        def _():
            acc_ref[...] = jnp.zeros_like(acc_ref)

        acc_ref[...] += product(refs)

        @pl.when(k == nk - 1)
        def _():
            o_ref[...] = acc_ref[...].astype(o_ref.dtype)

    out_dims = (size['m'], size['n'])
    out_shape = jax.ShapeDtypeStruct(((blocks,) + out_dims) if jdim in ('m', 'n') else out_dims, out_dtype)
    return _call(name, body_once if nk == 1 else body_acc, grid=grid,
                 in_specs=[spec(a_order)] * nt + [spec(b_order)] * nt, out_specs=[spec(('m', 'n'))],
                 out_shape=[out_shape], args=a_list + b_list,
                 scratch=[] if nk == 1 else [pltpu.VMEM((tile['m'], tile['n']), F32)], carry=carry)[0]


def _probs(q, k, i, tq, scale, causal):
    s = lax.dot_general(q, k, _DIMS['nt'], preferred_element_type=F32) * scale
    if causal:
        shape = s.shape
        q_chunk = jnp.right_shift(i * tq + lax.broadcasted_iota(jnp.int32, shape, 0), CHUNK_SHIFT)
        k_chunk = jnp.right_shift(lax.broadcasted_iota(jnp.int32, shape, 1), CHUNK_SHIFT)
        s = jnp.where(k_chunk <= q_chunk, s, -jnp.inf)
    e = jnp.exp(s - jnp.max(s, axis=-1, keepdims=True))
    return e, jnp.sum(e, axis=-1, keepdims=True)


def _attn_specs(tq, sk, dq, dv, koff, kstride, voff, vstride):
    return [pl.BlockSpec((tq, dq), lambda h, i: (i, h)),
            pl.BlockSpec((sk, dq), lambda h, i: (0, koff + kstride * h)),
            pl.BlockSpec((sk, dv), lambda h, i: (0, voff + vstride * h))]


def attn_fwd(name, q, k, v, *, heads, dq, dv, koff, kstride, voff, vstride, scale, causal, carry=None):
    sq, sk = q.shape[0], k.shape[0]
    tq = min(sq, 256)

    def body(q_ref, k_ref, v_ref, o_ref):
        e, l = _probs(q_ref[...].astype(BF16), k_ref[...].astype(BF16), pl.program_id(1), tq, scale, causal)
        o = jnp.dot(e.astype(BF16), v_ref[...].astype(BF16), preferred_element_type=F32)
        o_ref[...] = o / l

    return _call(name, body, grid=(heads, sq // tq),
                 in_specs=_attn_specs(tq, sk, dq, dv, koff, kstride, voff, vstride),
                 out_specs=[pl.BlockSpec((tq, dv), lambda h, i: (i, h))],
                 out_shape=[jax.ShapeDtypeStruct((sq, heads * dv), F32)], args=[q, k, v], carry=carry)[0]


def attn_bwd(name, q, k, v, do, *, heads, dq, dv, koff, kstride, voff, vstride, scale, causal, carry=None):
    sq, sk = q.shape[0], k.shape[0]
    tq = min(sq, 256)

    def body(q_ref, k_ref, v_ref, do_ref, dq_ref, dk_ref, dv_ref):
        i = pl.program_id(1)
        qb, kb, vb = q_ref[...].astype(BF16), k_ref[...].astype(BF16), v_ref[...].astype(BF16)
        dob = do_ref[...].astype(BF16)
        e, l = _probs(qb, kb, i, tq, scale, causal)
        p = e / l
        dp = lax.dot_general(dob, vb, _DIMS['nt'], preferred_element_type=F32)
        ds = (p * (dp - jnp.sum(dp * p, axis=-1, keepdims=True)) * scale).astype(BF16)

        @pl.when(i == 0)
        def _():
            dk_ref[...] = jnp.zeros_like(dk_ref)
            dv_ref[...] = jnp.zeros_like(dv_ref)

        dv_ref[...] += lax.dot_general(p.astype(BF16), dob, _DIMS['tn'], preferred_element_type=F32)
        dk_ref[...] += lax.dot_general(ds, qb, _DIMS['tn'], preferred_element_type=F32)
        dq_ref[...] = jnp.dot(ds, kb, preferred_element_type=F32)

    return _call(
        name, body, grid=(heads, sq // tq),
        in_specs=_attn_specs(tq, sk, dq, dv, koff, kstride, voff, vstride) + [pl.BlockSpec((tq, dv), lambda h, i: (i, h))],
        out_specs=[pl.BlockSpec((tq, dq), lambda h, i: (i, h)), pl.BlockSpec((sk, dq), lambda h, i: (0, h)),
                   pl.BlockSpec((sk, dv), lambda h, i: (0, h))],
        out_shape=[jax.ShapeDtypeStruct((sq, heads * dq), F32), jax.ShapeDtypeStruct((sk, heads * dq), F32),
                   jax.ShapeDtypeStruct((sk, heads * dv), F32)],
        args=[q, k, v, do], carry=carry)


def _hg_chunk(q, k, g, v, state):
    c = q.shape[0]
    row = lax.broadcasted_iota(jnp.int32, (c, c), 0)
    col = lax.broadcasted_iota(jnp.int32, (c, c), 1)
    tril = (col <= row).astype(F32)
    b = jnp.dot(tril, g, precision=lax.Precision.HIGHEST, preferred_element_type=F32)
    rows = lax.broadcasted_iota(jnp.int32, (c, 1), 0)
    o = jnp.dot((q * jnp.exp(b)).astype(BF16), state.astype(BF16), preferred_element_type=F32)
    t3 = lax.broadcasted_iota(jnp.int32, (SUB, SUB, 1), 0)
    s3 = lax.broadcasted_iota(jnp.int32, (SUB, SUB, 1), 1)
    parts = []
    for n in range(c // SUB):
        lo = n * SUB
        qn, kn, bn, vn = q[lo:lo + SUB], k[lo:lo + SUB], b[lo:lo + SUB], v[lo:lo + SUB]
        decay = jnp.exp(jnp.where(s3 <= t3, bn[:, None, :] - bn[None, :, :], -jnp.inf))
        sc = jnp.sum(qn[:, None, :] * kn[None, :, :] * decay, axis=-1)
        on = jnp.dot(sc.astype(BF16), vn.astype(BF16), preferred_element_type=F32)
        if n > 0:
            ref = jnp.sum(jnp.where(rows == lo - 1, b, 0.0), axis=0, keepdims=True)
            qd = (qn * jnp.exp(bn - ref)).astype(BF16)
            kd = (k[:lo] * jnp.exp(ref - b[:lo])).astype(BF16)
            so = lax.dot_general(qd, kd, _DIMS['nt'], preferred_element_type=F32)
            on = on + jnp.dot(so.astype(BF16), v[:lo].astype(BF16), preferred_element_type=F32)
        parts.append(on)
    o = o + jnp.concatenate(parts, axis=0)
    b_last = jnp.sum(g, axis=0, keepdims=True)
    ones = jnp.ones((c, 1), F32)
    b_last_col = lax.dot_general(g, ones, _DIMS['tn'], precision=lax.Precision.HIGHEST, preferred_element_type=F32)
    kd = (k * jnp.exp(b_last - b)).astype(BF16)
    new_state = jnp.exp(b_last_col) * state + lax.dot_general(kd, v.astype(BF16), _DIMS['tn'],
                                                              preferred_element_type=F32)
    return o, new_state


def hg_scan_fwd(name, q, k, g, u, v_off, carry=None):
    s = q.shape[0]
    n = s // CHUNK

    def body(q_ref, k_ref, g_ref, v_ref, o_ref, st_ref, state):
        @pl.when(pl.program_id(1) == 0)
        def _():
            state[...] = jnp.zeros_like(state)

        st = state[...]
        st_ref[...] = st
        o, new = _hg_chunk(q_ref[...], k_ref[...], g_ref[...], v_ref[...], st)
        o_ref[...] = o
        state[...] = new

    blk = pl.BlockSpec((CHUNK, HEAD_W), lambda h, c: (c, h))
    return _call(
        name, body, grid=(HG_HEADS, n),
        in_specs=[blk, blk, blk, pl.BlockSpec((CHUNK, HEAD_W), lambda h, c: (c, v_off + h))],
        out_specs=[blk, pl.BlockSpec((None, None, HEAD_W, HEAD_W), lambda h, c: (h, c, 0, 0))],
        out_shape=[jax.ShapeDtypeStruct((s, HG_HEADS * HEAD_W), F32),
                   jax.ShapeDtypeStruct((HG_HEADS, n, HEAD_W, HEAD_W), F32)],
        scratch=[pltpu.VMEM((HEAD_W, HEAD_W), F32)], args=[q, k, g, u], carry=carry)


def hg_scan_bwd(name, q, k, g, u, v_off, states, do, carry=None):
    s = q.shape[0]
    n = s // CHUNK

    def body(q_ref, k_ref, g_ref, v_ref, st_ref, do_ref, dq_ref, dk_ref, dg_ref, dv_ref, dstate):
        @pl.when(pl.program_id(1) == 0)
        def _():
            dstate[...] = jnp.zeros_like(dstate)

        _, vjp = jax.vjp(_hg_chunk, q_ref[...], k_ref[...], g_ref[...], v_ref[...], st_ref[...])
        dq, dk, dg, dv, dst = vjp((do_ref[...], dstate[...]))
        dq_ref[...] = dq
        dk_ref[...] = dk
        dg_ref[...] = dg
        dv_ref[...] = dv
        dstate[...] = dst

    blk = pl.BlockSpec((CHUNK, HEAD_W), lambda h, c: (n - 1 - c, h))
    out = jax.ShapeDtypeStruct((s, HG_HEADS * HEAD_W), F32)
    return _call(
        name, body, grid=(HG_HEADS, n),
        in_specs=[blk, blk, blk, pl.BlockSpec((CHUNK, HEAD_W), lambda h, c: (n - 1 - c, v_off + h)),
                  pl.BlockSpec((None, None, HEAD_W, HEAD_W), lambda h, c: (h, n - 1 - c, 0, 0)), blk],
        out_specs=[blk, blk, blk, blk], out_shape=[out, out, out, out],
        scratch=[pltpu.VMEM((HEAD_W, HEAD_W), F32)], args=[q, k, g, u, states, do], carry=carry)


def adamw(name, landed, w, m, v):
    rows, cols = w.shape
    tile = next((t for t in (512, 256, 128, 64, 32, 16, 8) if rows % t == 0 and t * cols <= (1 << 17)), rows)

    def body(l_ref, w_ref, m_ref, v_ref, g_ref, d_ref, nm_ref, nv_ref):
        gv = l_ref[0].astype(F32)
        for s in range(1, N_DEV):
            gv = gv + l_ref[s].astype(F32)
        nm = ADAM_B1 * m_ref[...] + (1.0 - ADAM_B1) * gv
        nv = ADAM_B2 * v_ref[...] + (1.0 - ADAM_B2) * jnp.square(gv)
        m_hat = nm / (1.0 - ADAM_B1 ** ADAM_STEP)
        v_hat = nv / (1.0 - ADAM_B2 ** ADAM_STEP)
        g_ref[...] = gv
        d_ref[...] = -ADAM_LR * (m_hat / (jnp.sqrt(v_hat) + ADAM_EPS) + ADAM_WD * w_ref[...])
        nm_ref[...] = nm
        nv_ref[...] = nv

    blk = pl.BlockSpec((tile, cols), lambda i: (i, 0))
    out = jax.ShapeDtypeStruct((rows, cols), F32)
    return _call(name, body, grid=(rows // tile,),
                 in_specs=[pl.BlockSpec((N_DEV, tile, cols), lambda i: (0, i, 0)), blk, blk, blk],
                 out_specs=[blk] * 4, out_shape=[out] * 4, args=[landed, w, m, v])


def _swap_halves(pe):
    half = QK_ROPE // 2
    return jnp.concatenate([-pe[..., half:], pe[..., :half]], axis=-1)


def _unswap_halves(dsw):
    half = QK_ROPE // 2
    return jnp.concatenate([dsw[..., half:], -dsw[..., :half]], axis=-1)


def _w_in_ext(w):
    kpe = w[:, 9216:9280]
    z = jnp.zeros((w.shape[0], HEAD_W - QK_ROPE), w.dtype)
    pad = jnp.zeros((w.shape[0], U_PAD), w.dtype)
    return jnp.concatenate([w[:, :9216], w[:, 9280:], kpe, z, _swap_halves(kpe), z, pad], axis=1)


def _w_in_grad(d):
    dkpe = d[:, 13312:13376] + _unswap_halves(d[:, 13440:13504])
    return jnp.concatenate([d[:, :9216], dkpe, d[:, 9216:13312]], axis=1)


def _w_q_ext(w):
    w3 = w.reshape(Q_LORA, MLA_HEADS, MLA_QK)
    pe = w3[:, :, HEAD_W:]
    z = jnp.zeros((Q_LORA, MLA_HEADS, HEAD_W - QK_ROPE), w.dtype)
    wide = MLA_HEADS * HEAD_W
    return jnp.concatenate([w3[:, :, :HEAD_W].reshape(Q_LORA, wide),
                            jnp.concatenate([pe, z], axis=2).reshape(Q_LORA, wide),
                            jnp.concatenate([_swap_halves(pe), z], axis=2).reshape(Q_LORA, wide)], axis=1)


def _w_q_grad(d):
    wide = MLA_HEADS * HEAD_W
    d3 = [d[:, n * wide:(n + 1) * wide].reshape(Q_LORA, MLA_HEADS, HEAD_W) for n in range(3)]
    dpe = d3[1][:, :, :QK_ROPE] + _unswap_halves(d3[2][:, :, :QK_ROPE])
    return jnp.concatenate([d3[0], dpe], axis=2).reshape(Q_LORA, MLA_HEADS * MLA_QK)


def _hg_prep(f_raw, q_hg, logits):
    lb = jax.nn.softmax(logits, axis=0)[0:1, :]
    log_f = jnp.logaddexp(jnp.log(lb), jnp.log1p(-lb) + jax.nn.log_sigmoid(f_raw))
    k_in = (1.0 - lb) * jax.nn.sigmoid(-f_raw)
    return log_f, k_in, jax.nn.silu(q_hg)


def _rope(q_nope, q_pe, q_sw, k_nope, k_pe, k_sw, cos, sin):
    qf = jnp.concatenate([q_nope, q_pe * cos + q_sw * sin], axis=1)
    kf = jnp.concatenate([k_nope, k_pe * cos + k_sw * sin], axis=1)
    return qf, kf


def _step(a):
    x, mem, target = a['x'][0], a['mem'][0], a['loss_target'][0]
    s, d = x.shape
    nm = mem.shape[0]
    ff = N_DEV * a['ffn1_w_gate'].shape[-1]
    cs = ff // N_DEV
    tr = min(s, 128)
    th = min(s, 1024)
    ta = 512

    bf = {n: a[n][0].astype(BF16) for n in BIG}
    gat, full = {}, {}

    def gather_of(names):
        return Gather([bf[n] for n in names])

    def gathered(names, carry):
        for n, g8 in zip(names, carry.results):
            r, c = bf[n].shape
            gat[n] = g8
            if not n.startswith('ffn'):
                full[n] = g8.transpose(1, 0, 2).reshape(r, N_DEV * c) if BIG[n] else g8.reshape(N_DEV * r, c)

    group_a = ['ffn1_w_gate', 'ffn1_w_up']
    group_b = ['ffn1_w_down']
    group_c = ['w_in', 'mla_w_q_up', 'mla_w_kv_up', 'w_branch_a', 'w_branch_b', 'w_out']
    group_d = ['xa_w_q', 'xa_w_k', 'xa_w_v', 'xa_w_o', 'ffn2_w_gate', 'ffn2_w_up', 'ffn2_w_down']
    first = gather_of(group_a)
    run_alone("gather_a", first)
    gathered(group_a, first)

    inv_freq = 1.0 / (ROPE_THETA ** (jnp.arange(0, QK_ROPE, 2, dtype=F32) / QK_ROPE))
    ang = a['positions'][0].astype(F32)[:, None] * inv_freq
    zero = jnp.zeros((s, HEAD_W - QK_ROPE), F32)
    cos = jnp.concatenate([jnp.cos(ang), jnp.cos(ang), zero], axis=1)
    sin = jnp.concatenate([jnp.sin(ang), jnp.sin(ang), zero], axis=1)

    gs = {}
    gb = {}

    def by_rows(g):
        return g.reshape(N_DEV, g.shape[0] // N_DEV, g.shape[1])

    def by_cols(g):
        return g.reshape(g.shape[0], N_DEV, g.shape[1] // N_DEV).transpose(1, 0, 2)

    pre_fn = lambda xv, g: (_rms(xv, g),)
    pre_res_fn = lambda xv, g: (_rms(xv, g), xv)

    def pre_norm(tag, x_in, g):
        return rowmap(tag + "_pre", pre_fn, [R_(x_in), P_(g)], [(BF16, x_in.shape[1], 0)], tile=tr)[0]

    def pre_norm_bwd(tag, x_in, g, dh, d_out):
        return rowmap(tag + "_pre_bwd", pre_res_fn, [R_(x_in), P_(g)], tile=tr, cts=[R_(dh), R_(d_out)],
                      wrt=[0, 1], gdt=[F32, F32])

    def post_fn(weight):
        return lambda xv, y, g: (xv + weight * _rms(y, g),)

    def post_norm(tag, x_in, y, g, weight):
        return rowmap(tag + "_post", post_fn(weight), [R_(x_in), R_(y), P_(g)], [(F32, d, 0)], tile=tr)[0]

    def post_norm_bwd(tag, x_in, y, g, weight, d_out):
        return rowmap(tag + "_post_bwd", post_fn(weight), [R_(x_in), R_(y), P_(g)], tile=tr, cts=[R_(d_out)],
                      wrt=[1, 2], gdt=[BF16, F32])

    act_fn = lambda av, bv: (jax.nn.silu(av) * bv,)

    def ffn_fwd(tag, x_in, gate_carries=None, up_carries=None):
        h = pre_norm(tag, x_in, a[tag + '_pre_g'])
        av = mm(tag + "_gate", h, gat[tag + '_w_gate'], 'nn', jdim='n', carry=gate_carries and gate_carries[1])
        if gate_carries:
            gathered(*gate_carries)
        bv = mm(tag + "_up", h, gat[tag + '_w_up'], 'nn', jdim='n', carry=up_carries and up_carries[1])
        if up_carries:
            gathered(*up_carries)
        a2, b2 = av.reshape(N_DEV * s, cs), bv.reshape(N_DEV * s, cs)
        z = rowmap(tag + "_act", act_fn, [R_(a2), R_(b2)], [(BF16, cs, 0)], tile=ta)[0].reshape(N_DEV, s, cs)
        y = mm(tag + "_dn", z, gat[tag + '_w_down'], 'nn', jdim='k')
        return post_norm(tag, x_in, y, a[tag + '_post_g'], 0.5), (x_in, h, a2, b2, z, y)

    def ffn_bwd(tag, res, d_out, dx_carry=None):
        x_in, h, a2, b2, z, y = res
        dy, gs[tag + '_post_g'] = post_norm_bwd(tag, x_in, y, a[tag + '_post_g'], 0.5, d_out)
        dz = mm(tag + "_dn_dx", dy, gat[tag + '_w_down'], 'nt', jdim='n')
        gb[tag + '_w_down'] = mm(tag + "_dn_dw", z, dy, 'tn', jdim='m', out_dtype=BF16)
        da, db = rowmap(tag + "_act_bwd", act_fn, [R_(a2), R_(b2)], tile=ta, cts=[R_(dz.reshape(N_DEV * s, cs))],
                        wrt=[0, 1], gdt=[BF16, BF16])
        da, db = da.reshape(N_DEV, s, cs), db.reshape(N_DEV, s, cs)
        dh = mm(tag + "_gu_dx", [da, db], [gat[tag + '_w_gate'], gat[tag + '_w_up']], 'nt', jdim='k', carry=dx_carry)
        gb[tag + '_w_gate'] = mm(tag + "_gate_dw", h, da, 'tn', jdim='n', out_dtype=BF16)
        gb[tag + '_w_up'] = mm(tag + "_up_dw", h, db, 'tn', jdim='n', out_dtype=BF16)
        d_in, gs[tag + '_pre_g'] = pre_norm_bwd(tag, x_in, a[tag + '_pre_g'], dh, d_out)
        return d_in

    hg_out_fn = lambda o, og, g: (_rms(o, g) * jax.nn.silu(og),)
    mla_norm_fn = lambda cq, ckv, gq, gkv: (_rms(cq, gq), _rms(ckv, gkv))
    gate_fn = lambda ga, gb, ya, yb: (jax.nn.sigmoid(ga) * ya + jax.nn.sigmoid(gb) * yb,)
    mla = dict(heads=MLA_HEADS, dq=2 * HEAD_W, dv=HEAD_W, koff=0, kstride=1, voff=1, vstride=2,
               scale=MLA_QK ** -0.5, causal=True)

    def mix_fwd(x_in, in_carries):
        w_in, w_q, w_kv = _w_in_ext(full['w_in']), _w_q_ext(full['mla_w_q_up']), full['mla_w_kv_up']
        h = pre_norm("mix", x_in, a['mix_pre_g'])
        u = mm("mix_in", h, w_in, 'nn', carry=in_carries[1])
        gathered(*in_carries)
        hg_ins = [R_(u, 2048, 1), R_(u, 2048, 0), P_(a['hgrn_lb_logits'])]
        log_f, k_in, q_in = rowmap("hg_prep", _hg_prep, hg_ins, [(F32, 2048, 0)] * 3, tile=tr)
        o_a, states = hg_scan_fwd("hg_scan", q_in, k_in, log_f, u, 32)
        out_ins = [R_(o_a, HEAD_W, 0, 1), R_(u, HEAD_W, 48, 1), P_(a['hg_norm_g'], HEAD_W, 0, 1)]
        oag = rowmap("hg_out", hg_out_fn, out_ins, [(BF16, HEAD_W, 1)], tile=th, ncol=HG_HEADS)[0]
        y_a = mm("mix_a", oag, full['w_branch_a'], 'nn')
        norm_ins = [R_(u, 512, 16), R_(u, 512, 17), P_(a['mla_q_norm_g']), P_(a['mla_kv_norm_g'])]
        cqn, ckvn = rowmap("mla_norm", mla_norm_fn, norm_ins, [(BF16, 512, 0)] * 2, tile=tr)
        q_all = mm("mla_qup", cqn, w_q, 'nn')
        kv = mm("mla_kvup", ckvn, w_kv, 'nn')
        rope_ins = [R_(q_all, HEAD_W, 0, 1), R_(q_all, HEAD_W, 16, 1), R_(q_all, HEAD_W, 32, 1),
                    R_(kv, HEAD_W, 0, 2), R_(u, HEAD_W, 104, 0), R_(u, HEAD_W, 105, 0), R_(cos), R_(sin)]
        qf, kf = rowmap("mla_rope", _rope, rope_ins, [(BF16, 2 * HEAD_W, 1)] * 2, tile=th, ncol=MLA_HEADS,
                        rows_inner=False)
        o_b = attn_fwd("mla_attn", qf, kf, kv, **mla)
        y_b = mm("mix_b", o_b, full['w_branch_b'], 'nn')
        gate_ins = [R_(u, 1024, 9, 1), R_(u, 1024, 11, 1), R_(y_a, 1024, 0, 1), R_(y_b, 1024, 0, 1)]
        y = rowmap("mix_gate", gate_fn, gate_ins, [(BF16, 1024, 1)], tile=tr, ncol=2)[0]
        yo = mm("mix_out", y, full['w_out'], 'nn')
        res = (x_in, h, hg_ins, q_in, k_in, log_f, u, states, out_ins, oag, norm_ins, cqn, ckvn, rope_ins, qf, kf,
               kv, o_b, gate_ins, y, yo, w_in, w_q, w_kv)
        return post_norm("mix", x_in, yo, a['mix_post_g'], 1.0), res

    def mix_bwd(res, d_out, attn_carries, scan_names):
        (x_in, h, hg_ins, q_in, k_in, log_f, u, states, out_ins, oag, norm_ins, cqn, ckvn, rope_ins, qf, kf, kv,
         o_b, gate_ins, y, yo, w_in, w_q, w_kv) = res
        dyo, gs['mix_post_g'] = post_norm_bwd("mix", x_in, yo, a['mix_post_g'], 1.0, d_out)
        dy = mm("mix_out_dx", dyo, full['w_out'], 'nt')
        gb['w_out'] = by_rows(mm("mix_out_dw", y, dyo, 'tn', out_dtype=BF16))
        dga, dgb, dya, dyb = rowmap("mix_gate_bwd", gate_fn, gate_ins, tile=tr, ncol=2, cts=[R_(dy, 1024, 0, 1)],
                                    wrt=[0, 1, 2, 3], gdt=[BF16] * 4)
        gb['w_branch_b'] = by_rows(mm("mix_b_dw", o_b, dyb, 'tn', out_dtype=BF16))
        do_b = mm("mix_b_dx", dyb, full['w_branch_b'], 'nt')
        dqf, dkf, dv = attn_bwd("mla_attn_bwd", qf, kf, kv, do_b, carry=attn_carries[1], **mla)
        exchanged(*attn_carries)
        dqn, dqp, dqs, dkn, dkpe, dksw = rowmap(
            "mla_rope_bwd", _rope, rope_ins, tile=th, ncol=MLA_HEADS, rows_inner=False,
            cts=[R_(dqf, 2 * HEAD_W, 0, 1), R_(dkf, 2 * HEAD_W, 0, 1)], wrt=[0, 1, 2, 3, 4, 5],
            gdt=[BF16, BF16, BF16, BF16, F32, F32])
        dq_all = jnp.concatenate([dqn, dqp, dqs], axis=1)
        dkv = jnp.concatenate([dkn.reshape(s, MLA_HEADS, HEAD_W), dv.astype(BF16).reshape(s, MLA_HEADS, HEAD_W)],
                              axis=2).reshape(s, 2 * MLA_HEADS * HEAD_W)
        gb['mla_w_q_up'] = by_cols(_w_q_grad(mm("mla_qup_dw", cqn, dq_all, 'tn', out_dtype=BF16)))
        dcqn = mm("mla_qup_dx", dq_all, w_q, 'nt')
        gb['mla_w_kv_up'] = by_cols(mm("mla_kvup_dw", ckvn, dkv, 'tn', out_dtype=BF16))
        dckvn = mm("mla_kvup_dx", dkv, w_kv, 'nt')
        dcq, dckv, gs['mla_q_norm_g'], gs['mla_kv_norm_g'] = rowmap(
            "mla_norm_bwd", mla_norm_fn, norm_ins, tile=tr, cts=[R_(dcqn), R_(dckvn)], wrt=[0, 1, 2, 3],
            gdt=[BF16, BF16, F32, F32])
        gb['w_branch_a'] = by_rows(mm("mix_a_dw", oag, dya, 'tn', out_dtype=BF16))
        doag = mm("mix_a_dx", dya, full['w_branch_a'], 'nt')
        do_a, dog, gs['hg_norm_g'] = rowmap("hg_out_bwd", hg_out_fn, out_ins, tile=th, ncol=HG_HEADS,
                                            cts=[R_(doag, HEAD_W, 0, 1)], wrt=[0, 1, 2], gdt=[F32, BF16, F32])
        scan_carry = exchange_of(scan_names)
        dq_in, dk_in, dlog_f, di = hg_scan_bwd("hg_scan_bwd", q_in, k_in, log_f, u, 32, states, do_a,
                                               carry=scan_carry)
        exchanged(scan_names, scan_carry)
        df, dq_hg, gs['hgrn_lb_logits'] = rowmap("hg_prep_bwd", _hg_prep, hg_ins, tile=tr,
                                                 cts=[R_(dlog_f), R_(dk_in), R_(dq_in)], wrt=[0, 1, 2],
                                                 gdt=[BF16, BF16, F32])
        du = jnp.concatenate([dq_hg, df, di.astype(BF16), dog, dcq, dckv, dga, dgb, dkpe.astype(BF16),
                              dksw.astype(BF16), jnp.zeros((s, U_PAD), BF16)], axis=1)
        gb['w_in'] = by_cols(_w_in_grad(mm("mix_in_dw", h, du, 'tn', out_dtype=BF16)))
        dh = mm("mix_in_dx", du, w_in, 'nt')
        d_in, gs['mix_pre_g'] = pre_norm_bwd("mix", x_in, a['mix_pre_g'], dh, d_out)
        return d_in

    xa = dict(heads=XA_HEADS, dq=HEAD_W, dv=HEAD_W, koff=0, kstride=1, voff=XA_HEADS, vstride=1,
              scale=HEAD_W ** -0.5, causal=False)
    tm_ = min(nm, 128)

    def xa_fwd(x_in):
        w_xkv = jnp.concatenate([full['xa_w_k'], full['xa_w_v']], axis=1)
        h = pre_norm("xa", x_in, a['xa_pre_g'])
        mn = rowmap("xa_mem", pre_fn, [R_(mem), P_(a['xa_mem_g'])], [(BF16, d, 0)], tile=tm_)[0]
        q = mm("xa_q", h, full['xa_w_q'], 'nn')
        kv = mm("xa_kv", mn, w_xkv, 'nn')
        o = attn_fwd("xa_attn", q, kv, kv, **xa)
        yo = mm("xa_o", o, full['xa_w_o'], 'nn')
        return post_norm("xa", x_in, yo, a['xa_post_g'], 1.0), (x_in, h, mn, q, kv, o, yo, w_xkv)

    def xa_bwd(res, d_out):
        x_in, h, mn, q, kv, o, yo, w_xkv = res
        dyo, gs['xa_post_g'] = post_norm_bwd("xa", x_in, yo, a['xa_post_g'], 1.0, d_out)
        do = mm("xa_o_dx", dyo, full['xa_w_o'], 'nt')
        gb['xa_w_o'] = by_cols(mm("xa_o_dw", o, dyo, 'tn', out_dtype=BF16))
        dq, dk, dv = attn_bwd("xa_attn_bwd", q, kv, kv, do, **xa)
        dkv = jnp.concatenate([dk, dv], axis=1).astype(BF16)
        dw = mm("xa_kv_dw", mn, dkv, 'tn', out_dtype=BF16)
        gb['xa_w_k'], gb['xa_w_v'] = by_rows(dw[:, :XA_HEADS * HEAD_W]), by_rows(dw[:, XA_HEADS * HEAD_W:])
        dmn = mm("xa_kv_dx", dkv, w_xkv, 'nt')
        gs['xa_mem_g'] = rowmap("xa_mem_bwd", pre_fn, [R_(mem), P_(a['xa_mem_g'])], tile=tm_, cts=[R_(dmn)],
                                wrt=[1], gdt=[F32])[0]
        gb['xa_w_q'] = by_rows(mm("xa_q_dw", h, dq, 'tn', out_dtype=BF16))
        dh = mm("xa_q_dx", dq, full['xa_w_q'], 'nt')
        d_in, gs['xa_pre_g'] = pre_norm_bwd("xa", x_in, a['xa_pre_g'], dh, d_out)
        return d_in

    landed = {}

    def exchange_of(names):
        return Exchange([gb[n] for n in names])

    def exchanged(names, carry):
        landed.update(zip(names, carry.results))

    x1, r1 = ffn_fwd('ffn1', x, (group_b, gather_of(group_b)), (group_c, gather_of(group_c)))
    x2, r2 = mix_fwd(x1, (group_d, gather_of(group_d)))
    x3, r3 = xa_fwd(x2)
    x4, r4 = ffn_fwd('ffn2', x3)

    def loss_fn(y, t):
        diff = y - t
        return diff * (1.0 / d), jnp.mean(diff * diff, axis=-1, keepdims=True)

    d4, row_loss = rowmap("loss", loss_fn, [R_(x4), R_(target)], [(F32, d, 0), (F32, 1, 0)], tile=tr)
    loss = lax.psum(0.5 * jnp.sum(row_loss), ("x", "y", "c"))
    d3 = ffn_bwd('ffn2', r4, d4)
    d2 = xa_bwd(r3, d3)
    d1 = mix_bwd(r2, d2, (group_d, exchange_of(group_d)),
                 ['w_out', 'w_branch_b', 'mla_w_q_up', 'mla_w_kv_up', 'w_branch_a'])
    in_carry = exchange_of(['w_in'])
    grad_x = ffn_bwd('ffn1', r1, d1, dx_carry=in_carry)
    exchanged(['w_in'], in_carry)
    last = exchange_of(group_a + group_b)
    run_alone("exchange_ffn1", last)
    exchanged(group_a + group_b, last)

    def pack_small(vals):
        flat = jnp.concatenate([vals[n].reshape(-1) for n in SMALL])
        rows = -(-flat.shape[0] // PACK_W)
        rows = -(-rows // 8) * 8
        return jnp.pad(flat, (0, rows * PACK_W - flat.shape[0])).reshape(rows, PACK_W)

    def unpack_small(buf):
        flat, out, at = buf.reshape(-1), {}, 0
        for n in SMALL:
            size = a[n].shape[0] * a[n].shape[1]
            out[n] = flat[at:at + size].reshape(a[n].shape)
            at += size
        return out

    small = Gather([pack_small(gs)])
    run_alone("gather_g_small", small)

    grads, delta, new_m, new_v = {}, {}, {}, {}
    packs = adamw("adamw_small", small.results[0], pack_small(a), pack_small({n: a['m_' + n] for n in SMALL}),
                  pack_small({n: a['v_' + n] for n in SMALL}))
    for dst, buf in zip((grads, delta, new_m, new_v), packs):
        dst.update(unpack_small(buf))
    for n in BIG:
        outs = adamw("adamw_" + n, landed[n], a[n][0], a['m_' + n][0], a['v_' + n][0])
        grads[n], delta[n], new_m[n], new_v[n] = (t.reshape(a[n].shape) for t in outs)

    return (loss, grad_x[None], *[grads[n] for n in WEIGHTS], *[delta[n] for n in WEIGHTS],
            *[new_m[n] for n in WEIGHTS], *[new_v[n] for n in WEIGHTS])


def kernel(x, mem, positions, hgrn_lb_logits, ffn1_pre_g, ffn1_w_gate, ffn1_w_up, ffn1_w_down, ffn1_post_g, mix_pre_g, w_in, hg_norm_g, mla_q_norm_g, mla_w_q_up, mla_kv_norm_g, mla_w_kv_up, w_branch_a, w_branch_b, w_out, mix_post_g, xa_pre_g, xa_mem_g, xa_w_q, xa_w_k, xa_w_v, xa_w_o, xa_post_g, ffn2_pre_g, ffn2_w_gate, ffn2_w_up, ffn2_w_down, ffn2_post_g, loss_target, m_hgrn_lb_logits, m_ffn1_pre_g, m_ffn1_w_gate, m_ffn1_w_up, m_ffn1_w_down, m_ffn1_post_g, m_mix_pre_g, m_w_in, m_hg_norm_g, m_mla_q_norm_g, m_mla_w_q_up, m_mla_kv_norm_g, m_mla_w_kv_up, m_w_branch_a, m_w_branch_b, m_w_out, m_mix_post_g, m_xa_pre_g, m_xa_mem_g, m_xa_w_q, m_xa_w_k, m_xa_w_v, m_xa_w_o, m_xa_post_g, m_ffn2_pre_g, m_ffn2_w_gate, m_ffn2_w_up, m_ffn2_w_down, m_ffn2_post_g, v_hgrn_lb_logits, v_ffn1_pre_g, v_ffn1_w_gate, v_ffn1_w_up, v_ffn1_w_down, v_ffn1_post_g, v_mix_pre_g, v_w_in, v_hg_norm_g, v_mla_q_norm_g, v_mla_w_q_up, v_mla_kv_norm_g, v_mla_w_kv_up, v_w_branch_a, v_w_branch_b, v_w_out, v_mix_post_g, v_xa_pre_g, v_xa_mem_g, v_xa_w_q, v_xa_w_k, v_xa_w_v, v_xa_w_o, v_xa_post_g, v_ffn2_pre_g, v_ffn2_w_gate, v_ffn2_w_up, v_ffn2_w_down, v_ffn2_post_g):
    return _step(dict(locals()))
```

```python
import functools

import jax
import jax.numpy as jnp
from jax import lax
from jax.experimental import pallas as pl
from jax.experimental.pallas import tpu as pltpu

F32 = jnp.float32
BF16 = jnp.bfloat16

N_DEV = 8
D_MODEL = 2048
CHUNK = 64
CHUNK_SHIFT = 6
SUB = 16
HG_HEADS = 16
HEAD_W = 128
MLA_HEADS = 16
Q_LORA = 512
KV_LORA = 512
QK_ROPE = 64
MLA_QK = 192
XA_HEADS = 4
ROPE_THETA = 10000.0
EPS = 1e-6
PACK_W = 1024
VMEM_LIMIT = 56 * 1024 * 1024

ADAM_LR = 0.001
ADAM_B1 = 0.9
ADAM_B2 = 0.999
ADAM_EPS = 1e-08
ADAM_WD = 0.01
ADAM_STEP = 10

U_PAD = 256

WEIGHTS = ['hgrn_lb_logits', 'ffn1_pre_g', 'ffn1_w_gate', 'ffn1_w_up', 'ffn1_w_down', 'ffn1_post_g', 'mix_pre_g',
           'w_in', 'hg_norm_g', 'mla_q_norm_g', 'mla_w_q_up', 'mla_kv_norm_g', 'mla_w_kv_up', 'w_branch_a',
           'w_branch_b', 'w_out', 'mix_post_g', 'xa_pre_g', 'xa_mem_g', 'xa_w_q', 'xa_w_k', 'xa_w_v', 'xa_w_o',
           'xa_post_g', 'ffn2_pre_g', 'ffn2_w_gate', 'ffn2_w_up', 'ffn2_w_down', 'ffn2_post_g']
BIG = {'ffn1_w_gate': True, 'ffn1_w_up': True, 'ffn1_w_down': False, 'w_in': True, 'mla_w_q_up': True,
       'mla_w_kv_up': True, 'w_branch_a': False, 'w_branch_b': False, 'w_out': False, 'xa_w_q': False,
       'xa_w_k': False, 'xa_w_v': False, 'xa_w_o': True, 'ffn2_w_gate': True, 'ffn2_w_up': True,
       'ffn2_w_down': False}
SMALL = [n for n in WEIGHTS if n not in BIG]


def _cparams(**kw):
    return pltpu.CompilerParams(vmem_limit_bytes=VMEM_LIMIT, **kw)


def _pick(dim, cands):
    for c in cands:
        if dim % c == 0:
            return c
    return dim


def _place():
    return lax.axis_index("x"), lax.axis_index("y"), lax.axis_index("c")


def _slot(px, py, pc):
    return 4 * px + 2 * py + pc


class Gather:
    def __init__(self, tensors):
        self.operands = list(tensors)
        self.out_shape = [jax.ShapeDtypeStruct((N_DEV,) + t.shape, t.dtype) for t in tensors]
        n = len(tensors)
        self.scratch = [pltpu.SemaphoreType.DMA((n, N_DEV - 1)), pltpu.SemaphoreType.DMA((n, N_DEV - 1)),
                        pltpu.SemaphoreType.DMA((n,))]
        self.results = None

    def _copies(self, t, x_ref, out_ref, send, recv):
        x, y, c = _place()
        chips = [(1 - x, y), (x, 1 - y), (1 - x, 1 - y)]

        def copy(k, block, to, src=None):
            rows = out_ref.at[_slot(*block)]
            return pltpu.make_async_remote_copy(src_ref=rows if src is None else src, dst_ref=rows,
                                                send_sem=send.at[t, k], recv_sem=recv.at[t, k], device_id=to,
                                                device_id_type=pl.DeviceIdType.MESH)

        return (x, y, c), chips, copy

    def start(self, ins, outs, scr):
        send, recv, local = scr
        for t, (x_ref, out_ref) in enumerate(zip(ins, outs)):
            (x, y, c), chips, copy = self._copies(t, x_ref, out_ref, send, recv)
            pltpu.make_async_copy(x_ref, out_ref.at[_slot(x, y, c)], local.at[t]).start()
            copy(0, (x, y, c), (x, y, 1 - c), src=x_ref).start()
            for j, chip in enumerate(chips):
                copy(1 + j, (x, y, c), (*chip, c), src=x_ref).start()

    def finish(self, ins, outs, scr):
        send, recv, local = scr
        for t, (x_ref, out_ref) in enumerate(zip(ins, outs)):
            (x, y, c), chips, copy = self._copies(t, x_ref, out_ref, send, recv)
            passed = [copy(4 + j, (*chip, c), (x, y, 1 - c)) for j, chip in enumerate(chips)]
            for j, chip in enumerate(chips):
                copy(1 + j, (*chip, c), (x, y, c)).wait_recv()
                passed[j].start()
            copy(0, (x, y, 1 - c), (x, y, c)).wait_recv()
            for j, chip in enumerate(chips):
                copy(4 + j, (*chip, 1 - c), (x, y, c)).wait_recv()
            copy(0, (x, y, c), (x, y, 1 - c), src=x_ref).wait_send()
            for j, chip in enumerate(chips):
                copy(1 + j, (x, y, c), (*chip, c), src=x_ref).wait_send()
                passed[j].wait_send()
            pltpu.make_async_copy(x_ref, out_ref.at[_slot(x, y, c)], local.at[t]).wait()


class Exchange:
    def __init__(self, tensors):
        self.operands = list(tensors)
        self.out_shape = [jax.ShapeDtypeStruct(t.shape, t.dtype) for t in tensors]
        n = len(tensors)
        self.scratch = [pltpu.SemaphoreType.DMA((n, N_DEV - 1)), pltpu.SemaphoreType.DMA((n, N_DEV - 1)),
                        pltpu.SemaphoreType.DMA((n,))]
        self.results = None

    def _copy(self, t, k, x_ref, out_ref, send, recv, outgoing):
        x, y, c = _place()
        peer = (x ^ ((k >> 2) & 1), y ^ ((k >> 1) & 1), c ^ (k & 1))
        me_slot, peer_slot = _slot(x, y, c), _slot(*peer)
        return pltpu.make_async_remote_copy(
            src_ref=x_ref.at[peer_slot], dst_ref=out_ref.at[me_slot if outgoing else peer_slot],
            send_sem=send.at[t, k - 1], recv_sem=recv.at[t, k - 1], device_id=peer,
            device_id_type=pl.DeviceIdType.MESH)

    def _mine(self, t, x_ref, out_ref, local):
        me = _slot(*_place())
        return pltpu.make_async_copy(x_ref.at[me], out_ref.at[me], local.at[t])

    def start(self, ins, outs, scr):
        send, recv, local = scr
        for t, (x_ref, out_ref) in enumerate(zip(ins, outs)):
            self._mine(t, x_ref, out_ref, local).start()
            for k in range(1, N_DEV):
                self._copy(t, k, x_ref, out_ref, send, recv, True).start()

    def finish(self, ins, outs, scr):
        send, recv, local = scr
        for t, (x_ref, out_ref) in enumerate(zip(ins, outs)):
            for k in range(1, N_DEV):
                self._copy(t, k, x_ref, out_ref, send, recv, False).wait_recv()
            for k in range(1, N_DEV):
                self._copy(t, k, x_ref, out_ref, send, recv, True).wait_send()
            self._mine(t, x_ref, out_ref, local).wait()


_HBM = pl.BlockSpec(memory_space=pltpu.HBM)


def run_alone(name, carry):
    n_in, n_out = len(carry.operands), len(carry.out_shape)

    def body(*refs):
        ins, outs, scr = refs[:n_in], refs[n_in:n_in + n_out], refs[n_in + n_out:]
        carry.start(ins, outs, scr)
        carry.finish(ins, outs, scr)

    res = pl.pallas_call(body, name=name, in_specs=[_HBM] * n_in, out_specs=[_HBM] * n_out,
                         out_shape=carry.out_shape, scratch_shapes=carry.scratch)(*carry.operands)
    carry.results = list(res)
    return carry.results


def _call(name, body, *, grid, in_specs, out_specs, out_shape, args, scratch=(), carry=None):
    in_specs, out_specs, out_shape, scratch = list(in_specs), list(out_specs), list(out_shape), list(scratch)
    if carry is None:
        return list(pl.pallas_call(body, name=name, grid=grid, in_specs=in_specs, out_specs=out_specs,
                                   out_shape=out_shape, scratch_shapes=scratch, compiler_params=_cparams())(*args))
    n_in, n_out, n_scr = len(in_specs), len(out_shape), len(scratch)
    c_in, c_out = len(carry.operands), len(carry.out_shape)

    def wrapped(*refs):
        ins, c_ins = refs[:n_in], refs[n_in:n_in + c_in]
        at = n_in + c_in
        outs, c_outs = refs[at:at + n_out], refs[at + n_out:at + n_out + c_out]
        at += n_out + c_out
        scr, c_scr = refs[at:at + n_scr], refs[at + n_scr:]
        first = functools.reduce(jnp.logical_and, [pl.program_id(d) == 0 for d in range(len(grid))])
        last = functools.reduce(jnp.logical_and, [pl.program_id(d) == grid[d] - 1 for d in range(len(grid))])

        @pl.when(first)
        def _():
            carry.start(c_ins, c_outs, c_scr)

        body(*ins, *outs, *scr)

        @pl.when(last)
        def _():
            carry.finish(c_ins, c_outs, c_scr)

    res = pl.pallas_call(
        wrapped, name=name, grid=grid, in_specs=in_specs + [_HBM] * c_in, out_specs=out_specs + [_HBM] * c_out,
        out_shape=out_shape + carry.out_shape, scratch_shapes=scratch + carry.scratch, compiler_params=_cparams(),
    )(*args, *carry.operands)
    carry.results = list(res[n_out:])
    return list(res[:n_out])


def R_(arr, w=None, off=0, stride=0):
    return ('r', arr, arr.shape[1] if w is None else w, off, stride)


def P_(arr, w=None, off=0, stride=0):
    return ('p', arr, arr.shape[1] if w is None else w, off, stride)


def rowmap(name, fn, ins, outs=None, *, tile, ncol=1, rows_inner=True, cts=None, wrt=None, gdt=None, cat=False,
           carry=None):
    rows = next(a.shape[0] for k, a, *_ in ins if k == 'r')
    nrow = rows // tile
    assert nrow * tile == rows
    grid = (ncol, nrow) if rows_inner else (nrow, ncol)

    def ij(g0, g1):
        return (g1, g0) if rows_inner else (g0, g1)

    def spec(kind, arr, w, off, stride):
        if kind == 'r':
            return pl.BlockSpec((tile, w), lambda g0, g1: (ij(g0, g1)[0], off + stride * ij(g0, g1)[1]))
        return pl.BlockSpec((arr.shape[0], w), lambda g0, g1: (0, off + stride * ij(g0, g1)[1]))

    ops = list(ins) + list(cts or [])
    in_specs = [spec(*o) for o in ops]
    n_in = len(ins)
    fwd = cts is None
    out_shape, out_specs, acc = [], [], []
    if fwd:
        for dt, w, stride in outs:
            out_shape.append(jax.ShapeDtypeStruct((rows, w * (ncol if stride else 1)), dt))
            out_specs.append(spec('r', None, w, 0, stride))
            acc.append(None)
    elif cat:
        widths = [ins[i][2] for i in wrt]
        assert ncol == 1 and all(ins[i][0] == 'r' for i in wrt)
        out_shape.append(jax.ShapeDtypeStruct((rows, sum(widths)), gdt))
        out_specs.append(spec('r', None, sum(widths), 0, 0))
    else:
        for n, i in enumerate(wrt):
            kind, arr, w, off, stride = ins[i]
            width = w * (ncol if stride else 1)
            if kind == 'r':
                out_shape.append(jax.ShapeDtypeStruct((rows, width), gdt[n]))
                out_specs.append(spec('r', None, w, 0, 1 if stride else 0))
                shared = stride == 0 and ncol > 1
                assert not shared or (not rows_inner and gdt[n] == F32)
                acc.append('col' if shared else None)
            else:
                out_shape.append(jax.ShapeDtypeStruct((arr.shape[0], width), F32))
                out_specs.append(spec('p', arr, w, 0, 1 if stride else 0))
                assert rows_inner or ncol == 1
                acc.append('row')

    def body(*refs):
        i, j = ij(pl.program_id(0), pl.program_id(1))
        vals = [r[...].astype(F32) for r in refs[:n_in]]
        out_refs = refs[len(ops):]
        if fwd:
            for o_ref, o in zip(out_refs, fn(*vals)):
                o_ref[...] = o.astype(o_ref.dtype)
            return

        def f(*d):
            full = list(vals)
            for n, idx in enumerate(wrt):
                full[idx] = d[n]
            return fn(*full)

        _, vjp = jax.vjp(f, *[vals[idx] for idx in wrt])
        grads = vjp(tuple(r[...].astype(F32) for r in refs[n_in:len(ops)]))
        if cat:
            o_ref, at = out_refs[0], 0
            for g in grads:
                o_ref[:, at:at + g.shape[1]] = g.astype(o_ref.dtype)
                at += g.shape[1]
            return
        for o_ref, g, a in zip(out_refs, grads, acc):
            if a is None:
                o_ref[...] = g.astype(o_ref.dtype)
            else:
                first = (i if a == 'row' else j) == 0

                @pl.when(first)
                def _(o_ref=o_ref):
                    o_ref[...] = jnp.zeros_like(o_ref)

                o_ref[...] += g

    return _call(name, body, grid=grid, in_specs=in_specs, out_specs=out_specs, out_shape=out_shape,
                 args=[o[1] for o in ops], carry=carry)


def _rms(x, g):
    return x * lax.rsqrt(jnp.mean(x * x, axis=-1, keepdims=True) + EPS) * g


_DIMS = {'nn': (((1,), (0,)), ((), ())), 'nt': (((1,), (1,)), ((), ())), 'tn': (((0,), (0,)), ((), ()))}


def mm(name, a, b, mode, *, jdim=None, out_dtype=F32, carry=None):
    a_list = list(a) if isinstance(a, (list, tuple)) else [a]
    b_list = list(b) if isinstance(b, (list, tuple)) else [b]
    a_order = ('k', 'm') if mode == 'tn' else ('m', 'k')
    b_order = ('n', 'k') if mode == 'nt' else ('k', 'n')
    size, blocks = {}, 1
    for arr, order in ((a_list[0], a_order), (b_list[0], b_order)):
        shape = arr.shape
        if jdim in order:
            blocks, shape = shape[0], shape[1:]
        for dname, extent in zip(order, shape):
            assert size.setdefault(dname, extent) == extent
    tile = {'m': _pick(size['m'], (1024, 512, 256, 128)), 'n': _pick(size['n'], (512, 256, 128)),
            'k': size['k'] if size['k'] <= 2048 else _pick(size['k'], (2304, 2048, 1024, 512, 256, 128))}
    if jdim is not None:
        tile[jdim] = size[jdim]
    if a_list[0].dtype == F32 and tile['k'] * tile['m'] > (1 << 20):
        tile['m'] = _pick(size['m'], (512, 256, 128))
    grid = tuple(blocks if d == jdim else size[d] // tile[d] for d in ('m', 'n', 'k'))
    nk = grid[2]

    def spec(order):
        shape = tuple(tile[d] for d in order)

        def imap(i, j, k):
            g = {'m': i, 'n': j, 'k': k}
            idx = tuple(0 if d == jdim else g[d] for d in order)
            return ((g[jdim],) + idx) if jdim in order else idx

        return pl.BlockSpec(((None,) + shape) if jdim in order else shape, imap)

    dims = _DIMS[mode]
    nt = len(a_list)

    def product(refs):
        acc = None
        for a_ref, b_ref in zip(refs[:nt], refs[nt:2 * nt]):
            p = lax.dot_general(a_ref[...].astype(BF16), b_ref[...].astype(BF16), dims, preferred_element_type=F32)
            acc = p if acc is None else acc + p
        return acc

    def body_once(*refs):
        refs[2 * nt][...] = product(refs).astype(refs[2 * nt].dtype)

    def body_acc(*refs):
        o_ref, acc_ref = refs[2 * nt], refs[2 * nt + 1]
        k = pl.program_id(2)

        @pl.when(k == 0)
        def _():
            acc_ref[...] = jnp.zeros_like(acc_ref)

        acc_ref[...] += product(refs)

        @pl.when(k == nk - 1)
        def _():
            o_ref[...] = acc_ref[...].astype(o_ref.dtype)

    out_dims = (size['m'], size['n'])
    out_shape = jax.ShapeDtypeStruct(((blocks,) + out_dims) if jdim in ('m', 'n') else out_dims, out_dtype)
    return _call(name, body_once if nk == 1 else body_acc, grid=grid,
                 in_specs=[spec(a_order)] * nt + [spec(b_order)] * nt, out_specs=[spec(('m', 'n'))],
                 out_shape=[out_shape], args=a_list + b_list,
                 scratch=[] if nk == 1 else [pltpu.VMEM((tile['m'], tile['n']), F32)], carry=carry)[0]


def _probs(q, k, i, tq, scale, causal):
    s = lax.dot_general(q, k, _DIMS['nt'], preferred_element_type=F32) * scale
    if causal:
        shape = s.shape
        q_chunk = jnp.right_shift(i * tq + lax.broadcasted_iota(jnp.int32, shape, 0), CHUNK_SHIFT)
        k_chunk = jnp.right_shift(lax.broadcasted_iota(jnp.int32, shape, 1), CHUNK_SHIFT)
        s = jnp.where(k_chunk <= q_chunk, s, -jnp.inf)
    e = jnp.exp(s - jnp.max(s, axis=-1, keepdims=True))
    return e, jnp.sum(e, axis=-1, keepdims=True)


def _attn_specs(tq, sk, dq, dv, koff, kstride, voff, vstride):
    return [pl.BlockSpec((tq, dq), lambda h, i: (i, h)),
            pl.BlockSpec((sk, dq), lambda h, i: (0, koff + kstride * h)),
            pl.BlockSpec((sk, dv), lambda h, i: (0, voff + vstride * h))]


def attn_fwd(name, q, k, v, *, heads, dq, dv, koff, kstride, voff, vstride, scale, causal, carry=None):
    sq, sk = q.shape[0], k.shape[0]
    tq = min(sq, 256)

    def body(q_ref, k_ref, v_ref, o_ref):
        e, l = _probs(q_ref[...].astype(BF16), k_ref[...].astype(BF16), pl.program_id(1), tq, scale, causal)
        o = jnp.dot(e.astype(BF16), v_ref[...].astype(BF16), preferred_element_type=F32)
        o_ref[...] = o / l

    return _call(name, body, grid=(heads, sq // tq),
                 in_specs=_attn_specs(tq, sk, dq, dv, koff, kstride, voff, vstride),
                 out_specs=[pl.BlockSpec((tq, dv), lambda h, i: (i, h))],
                 out_shape=[jax.ShapeDtypeStruct((sq, heads * dv), F32)], args=[q, k, v], carry=carry)[0]


def attn_bwd(name, q, k, v, do, *, heads, dq, dv, koff, kstride, voff, vstride, scale, causal, carry=None):
    sq, sk = q.shape[0], k.shape[0]
    tq = min(sq, 256)

    def body(q_ref, k_ref, v_ref, do_ref, dq_ref, dk_ref, dv_ref):
        i = pl.program_id(1)
        qb, kb, vb = q_ref[...].astype(BF16), k_ref[...].astype(BF16), v_ref[...].astype(BF16)
        dob = do_ref[...].astype(BF16)
        e, l = _probs(qb, kb, i, tq, scale, causal)
        p = e / l
        dp = lax.dot_general(dob, vb, _DIMS['nt'], preferred_element_type=F32)
        ds = (p * (dp - jnp.sum(dp * p, axis=-1, keepdims=True)) * scale).astype(BF16)

        @pl.when(i == 0)
        def _():
            dk_ref[...] = jnp.zeros_like(dk_ref)
            dv_ref[...] = jnp.zeros_like(dv_ref)

        dv_ref[...] += lax.dot_general(p.astype(BF16), dob, _DIMS['tn'], preferred_element_type=F32)
        dk_ref[...] += lax.dot_general(ds, qb, _DIMS['tn'], preferred_element_type=F32)
        dq_ref[...] = jnp.dot(ds, kb, preferred_element_type=F32)

    return _call(
        name, body, grid=(heads, sq // tq),
        in_specs=_attn_specs(tq, sk, dq, dv, koff, kstride, voff, vstride) + [pl.BlockSpec((tq, dv), lambda h, i: (i, h))],
        out_specs=[pl.BlockSpec((tq, dq), lambda h, i: (i, h)), pl.BlockSpec((sk, dq), lambda h, i: (0, h)),
                   pl.BlockSpec((sk, dv), lambda h, i: (0, h))],
        out_shape=[jax.ShapeDtypeStruct((sq, heads * dq), F32), jax.ShapeDtypeStruct((sk, heads * dq), F32),
                   jax.ShapeDtypeStruct((sk, heads * dv), F32)],
        args=[q, k, v, do], carry=carry)


def _hg_chunk(q, k, g, v, state):
    c = q.shape[0]
    row = lax.broadcasted_iota(jnp.int32, (c, c), 0)
    col = lax.broadcasted_iota(jnp.int32, (c, c), 1)
    tril = (col <= row).astype(F32)
    b = jnp.dot(tril, g, precision=lax.Precision.HIGHEST, preferred_element_type=F32)
    rows = lax.broadcasted_iota(jnp.int32, (c, 1), 0)
    o = jnp.dot((q * jnp.exp(b)).astype(BF16), state.astype(BF16), preferred_element_type=F32)
    t3 = lax.broadcasted_iota(jnp.int32, (SUB, SUB, 1), 0)
    s3 = lax.broadcasted_iota(jnp.int32, (SUB, SUB, 1), 1)
    parts = []
    for n in range(c // SUB):
        lo = n * SUB
        qn, kn, bn, vn = q[lo:lo + SUB], k[lo:lo + SUB], b[lo:lo + SUB], v[lo:lo + SUB]
        decay = jnp.exp(jnp.where(s3 <= t3, bn[:, None, :] - bn[None, :, :], -jnp.inf))
        sc = jnp.sum(qn[:, None, :] * kn[None, :, :] * decay, axis=-1)
        on = jnp.dot(sc.astype(BF16), vn.astype(BF16), preferred_element_type=F32)
        if n > 0:
            ref = jnp.sum(jnp.where(rows == lo - 1, b, 0.0), axis=0, keepdims=True)
            qd = (qn * jnp.exp(bn - ref)).astype(BF16)
            kd = (k[:lo] * jnp.exp(ref - b[:lo])).astype(BF16)
            so = lax.dot_general(qd, kd, _DIMS['nt'], preferred_element_type=F32)
            on = on + jnp.dot(so.astype(BF16), v[:lo].astype(BF16), preferred_element_type=F32)
        parts.append(on)
    o = o + jnp.concatenate(parts, axis=0)
    b_last = jnp.sum(g, axis=0, keepdims=True)
    ones = jnp.ones((c, 1), F32)
    b_last_col = lax.dot_general(g, ones, _DIMS['tn'], precision=lax.Precision.HIGHEST, preferred_element_type=F32)
    kd = (k * jnp.exp(b_last - b)).astype(BF16)
    new_state = jnp.exp(b_last_col) * state + lax.dot_general(kd, v.astype(BF16), _DIMS['tn'],
                                                              preferred_element_type=F32)
    return o, new_state


def hg_scan_fwd(name, q, k, g, u, v_off, carry=None):
    s = q.shape[0]
    n = s // CHUNK

    def body(q_ref, k_ref, g_ref, v_ref, o_ref, st_ref, state):
        @pl.when(pl.program_id(1) == 0)
        def _():
            state[...] = jnp.zeros_like(state)

        st = state[...]
        st_ref[...] = st
        o, new = _hg_chunk(q_ref[...], k_ref[...], g_ref[...], v_ref[...], st)
        o_ref[...] = o
        state[...] = new

    blk = pl.BlockSpec((CHUNK, HEAD_W), lambda h, c: (c, h))
    return _call(
        name, body, grid=(HG_HEADS, n),
        in_specs=[blk, blk, blk, pl.BlockSpec((CHUNK, HEAD_W), lambda h, c: (c, v_off + h))],
        out_specs=[blk, pl.BlockSpec((None, None, HEAD_W, HEAD_W), lambda h, c: (h, c, 0, 0))],
        out_shape=[jax.ShapeDtypeStruct((s, HG_HEADS * HEAD_W), F32),
                   jax.ShapeDtypeStruct((HG_HEADS, n, HEAD_W, HEAD_W), F32)],
        scratch=[pltpu.VMEM((HEAD_W, HEAD_W), F32)], args=[q, k, g, u], carry=carry)


def hg_scan_bwd(name, q, k, g, u, v_off, states, do, carry=None):
    s = q.shape[0]
    n = s // CHUNK

    def body(q_ref, k_ref, g_ref, v_ref, st_ref, do_ref, dq_ref, dk_ref, dg_ref, dv_ref, dstate):
        @pl.when(pl.program_id(1) == 0)
        def _():
            dstate[...] = jnp.zeros_like(dstate)

        _, vjp = jax.vjp(_hg_chunk, q_ref[...], k_ref[...], g_ref[...], v_ref[...], st_ref[...])
        dq, dk, dg, dv, dst = vjp((do_ref[...], dstate[...]))
        dq_ref[...] = dq
        dk_ref[...] = dk
        dg_ref[...] = dg
        dv_ref[...] = dv
        dstate[...] = dst

    blk = pl.BlockSpec((CHUNK, HEAD_W), lambda h, c: (n - 1 - c, h))
    out = jax.ShapeDtypeStruct((s, HG_HEADS * HEAD_W), F32)
    return _call(
        name, body, grid=(HG_HEADS, n),
        in_specs=[blk, blk, blk, pl.BlockSpec((CHUNK, HEAD_W), lambda h, c: (n - 1 - c, v_off + h)),
                  pl.BlockSpec((None, None, HEAD_W, HEAD_W), lambda h, c: (h, n - 1 - c, 0, 0)), blk],
        out_specs=[blk, blk, blk, blk], out_shape=[out, out, out, out],
        scratch=[pltpu.VMEM((HEAD_W, HEAD_W), F32)], args=[q, k, g, u, states, do], carry=carry)


def adamw(name, landed, w, m, v):
    rows, cols = w.shape
    tile = next((t for t in (512, 256, 128, 64, 32, 16, 8) if rows % t == 0 and t * cols <= (1 << 17)), rows)

    def body(l_ref, w_ref, m_ref, v_ref, g_ref, d_ref, nm_ref, nv_ref):
        gv = l_ref[0].astype(F32)
        for s in range(1, N_DEV):
            gv = gv + l_ref[s].astype(F32)
        nm = ADAM_B1 * m_ref[...] + (1.0 - ADAM_B1) * gv
        nv = ADAM_B2 * v_ref[...] + (1.0 - ADAM_B2) * jnp.square(gv)
        m_hat = nm / (1.0 - ADAM_B1 ** ADAM_STEP)
        v_hat = nv / (1.0 - ADAM_B2 ** ADAM_STEP)
        g_ref[...] = gv
        d_ref[...] = -ADAM_LR * (m_hat / (jnp.sqrt(v_hat) + ADAM_EPS) + ADAM_WD * w_ref[...])
        nm_ref[...] = nm
        nv_ref[...] = nv

    blk = pl.BlockSpec((tile, cols), lambda i: (i, 0))
    out = jax.ShapeDtypeStruct((rows, cols), F32)
    return _call(name, body, grid=(rows // tile,),
                 in_specs=[pl.BlockSpec((N_DEV, tile, cols), lambda i: (0, i, 0)), blk, blk, blk],
                 out_specs=[blk] * 4, out_shape=[out] * 4, args=[landed, w, m, v])


def _swap_halves(pe):
    half = QK_ROPE // 2
    return jnp.concatenate([-pe[..., half:], pe[..., :half]], axis=-1)


def _unswap_halves(dsw):
    half = QK_ROPE // 2
    return jnp.concatenate([dsw[..., half:], -dsw[..., :half]], axis=-1)


def _w_in_ext(w):
    kpe = w[:, 9216:9280]
    z = jnp.zeros((w.shape[0], HEAD_W - QK_ROPE), w.dtype)
    pad = jnp.zeros((w.shape[0], U_PAD), w.dtype)
    return jnp.concatenate([w[:, :9216], w[:, 9280:], kpe, z, _swap_halves(kpe), z, pad], axis=1)


def _w_in_grad(d):
    dkpe = d[:, 13312:13376] + _unswap_halves(d[:, 13440:13504])
    return jnp.concatenate([d[:, :9216], dkpe, d[:, 9216:13312]], axis=1)


def _w_q_ext(w):
    w3 = w.reshape(Q_LORA, MLA_HEADS, MLA_QK)
    pe = w3[:, :, HEAD_W:]
    z = jnp.zeros((Q_LORA, MLA_HEADS, HEAD_W - QK_ROPE), w.dtype)
    wide = MLA_HEADS * HEAD_W
    return jnp.concatenate([w3[:, :, :HEAD_W].reshape(Q_LORA, wide),
                            jnp.concatenate([pe, z], axis=2).reshape(Q_LORA, wide),
                            jnp.concatenate([_swap_halves(pe), z], axis=2).reshape(Q_LORA, wide)], axis=1)


def _w_q_grad(d):
    wide = MLA_HEADS * HEAD_W
    d3 = [d[:, n * wide:(n + 1) * wide].reshape(Q_LORA, MLA_HEADS, HEAD_W) for n in range(3)]
    dpe = d3[1][:, :, :QK_ROPE] + _unswap_halves(d3[2][:, :, :QK_ROPE])
    return jnp.concatenate([d3[0], dpe], axis=2).reshape(Q_LORA, MLA_HEADS * MLA_QK)


def _hg_prep(f_raw, q_hg, logits):
    lb = jax.nn.softmax(logits, axis=0)[0:1, :]
    log_f = jnp.logaddexp(jnp.log(lb), jnp.log1p(-lb) + jax.nn.log_sigmoid(f_raw))
    k_in = (1.0 - lb) * jax.nn.sigmoid(-f_raw)
    return log_f, k_in, jax.nn.silu(q_hg)


def _rope(q_nope, q_pe, q_sw, k_nope, k_pe, k_sw, cos, sin):
    qf = jnp.concatenate([q_nope, q_pe * cos + q_sw * sin], axis=1)
    kf = jnp.concatenate([k_nope, k_pe * cos + k_sw * sin], axis=1)
    return qf, kf


def _step(a):
    x, mem, target = a['x'][0], a['mem'][0], a['loss_target'][0]
    s, d = x.shape
    nm = mem.shape[0]
    ff = N_DEV * a['ffn1_w_gate'].shape[-1]
    cs = ff // N_DEV
    tr = min(s, 128)
    th = min(s, 1024)
    ta = 512

    bf = {n: a[n][0].astype(BF16) for n in BIG}
    gat, full = {}, {}

    def gathered(names, carry):
        for n, g8 in zip(names or [], carry.results if carry else []):
            r, c = bf[n].shape
            gat[n] = g8
            if not n.startswith('ffn'):
                full[n] = g8.transpose(1, 0, 2).reshape(r, N_DEV * c) if BIG[n] else g8.reshape(N_DEV * r, c)

    xa_names = ['xa_w_q', 'xa_w_k', 'xa_w_v', 'xa_w_o']
    ffn2_names = ['ffn2_w_gate', 'ffn2_w_up', 'ffn2_w_down']
    first = Gather([bf['ffn1_w_gate']])
    run_alone("gather_first", first)
    gathered(['ffn1_w_gate'], first)

    inv_freq = 1.0 / (ROPE_THETA ** (jnp.arange(0, QK_ROPE, 2, dtype=F32) / QK_ROPE))
    ang = a['positions'][0].astype(F32)[:, None] * inv_freq
    zero = jnp.zeros((s, HEAD_W - QK_ROPE), F32)
    cos = jnp.concatenate([jnp.cos(ang), jnp.cos(ang), zero], axis=1)
    sin = jnp.concatenate([jnp.sin(ang), jnp.sin(ang), zero], axis=1)

    gs = {}
    gb = {}

    def by_rows(g):
        return g.reshape(N_DEV, g.shape[0] // N_DEV, g.shape[1])

    def by_cols(g):
        return g.reshape(g.shape[0], N_DEV, g.shape[1] // N_DEV).transpose(1, 0, 2)

    pre_fn = lambda xv, g: (_rms(xv, g),)
    pre_res_fn = lambda xv, g: (_rms(xv, g), xv)

    def pre_norm(tag, x_in, g):
        return rowmap(tag + "_pre", pre_fn, [R_(x_in), P_(g)], [(BF16, x_in.shape[1], 0)], tile=tr)[0]

    def pre_norm_bwd(tag, x_in, g, dh, d_out, carry=None):
        return rowmap(tag + "_pre_bwd", pre_res_fn, [R_(x_in), P_(g)], tile=tr, cts=[R_(dh), R_(d_out)],
                      wrt=[0, 1], gdt=[F32, F32], carry=carry)

    def post_fn(weight):
        return lambda xv, y, g: (xv + weight * _rms(y, g),)

    def post_norm(tag, x_in, y, g, weight):
        return rowmap(tag + "_post", post_fn(weight), [R_(x_in), R_(y), P_(g)], [(F32, d, 0)], tile=tr)[0]

    def post_norm_bwd(tag, x_in, y, g, weight, d_out):
        return rowmap(tag + "_post_bwd", post_fn(weight), [R_(x_in), R_(y), P_(g)], tile=tr, cts=[R_(d_out)],
                      wrt=[1, 2], gdt=[BF16, F32])

    act_fn = lambda av, bv: (jax.nn.silu(av) * bv,)

    def fetch(plan, key):
        names = plan.get(key)
        return (names, Gather([bf[n] for n in names])) if names else (None, None)

    def send(plan, key):
        names = plan.get(key)
        return (names, Exchange([gb[n] for n in names])) if names else (None, None)

    def ffn_fwd(tag, x_in, plan):
        h = pre_norm(tag, x_in, a[tag + '_pre_g'])
        names, carry = fetch(plan, 'gate')
        av = mm(tag + "_gate", h, gat[tag + '_w_gate'], 'nn', jdim='n', carry=carry)
        gathered(names, carry)
        names, carry = fetch(plan, 'up')
        bv = mm(tag + "_up", h, gat[tag + '_w_up'], 'nn', jdim='n', carry=carry)
        gathered(names, carry)
        a2, b2 = av.reshape(N_DEV * s, cs), bv.reshape(N_DEV * s, cs)
        z = rowmap(tag + "_act", act_fn, [R_(a2), R_(b2)], [(BF16, cs, 0)], tile=ta)[0].reshape(N_DEV, s, cs)
        names, carry = fetch(plan, 'dn')
        y = mm(tag + "_dn", z, gat[tag + '_w_down'], 'nn', jdim='k', carry=carry)
        gathered(names, carry)
        return post_norm(tag, x_in, y, a[tag + '_post_g'], 0.5), (x_in, h, a2, b2, z, y)

    def ffn_bwd(tag, res, d_out, plan):
        x_in, h, a2, b2, z, y = res
        dy, gs[tag + '_post_g'] = post_norm_bwd(tag, x_in, y, a[tag + '_post_g'], 0.5, d_out)
        dz = mm(tag + "_dn_dx", dy, gat[tag + '_w_down'], 'nt', jdim='n')
        gb[tag + '_w_down'] = mm(tag + "_dn_dw", z, dy, 'tn', jdim='m', out_dtype=BF16)
        da, db = rowmap(tag + "_act_bwd", act_fn, [R_(a2), R_(b2)], tile=ta, cts=[R_(dz.reshape(N_DEV * s, cs))],
                        wrt=[0, 1], gdt=[BF16, BF16])
        da, db = da.reshape(N_DEV, s, cs), db.reshape(N_DEV, s, cs)
        names, carry = send(plan, 'gu_dx')
        dh = mm(tag + "_gu_dx", [da, db], [gat[tag + '_w_gate'], gat[tag + '_w_up']], 'nt', jdim='k', carry=carry)
        exchanged(names, carry)
        gb[tag + '_w_gate'] = mm(tag + "_gate_dw", h, da, 'tn', jdim='n', out_dtype=BF16)
        names, carry = send(plan, 'up_dw')
        gb[tag + '_w_up'] = mm(tag + "_up_dw", h, db, 'tn', jdim='n', out_dtype=BF16, carry=carry)
        exchanged(names, carry)
        names, carry = send(plan, 'pre_bwd')
        d_in, gs[tag + '_pre_g'] = pre_norm_bwd(tag, x_in, a[tag + '_pre_g'], dh, d_out, carry)
        exchanged(names, carry)
        return d_in

    hg_out_fn = lambda o, og, g: (_rms(o, g) * jax.nn.silu(og),)
    mla_norm_fn = lambda cq, ckv, gq, gkv: (_rms(cq, gq), _rms(ckv, gkv))
    gate_fn = lambda ga, gb, ya, yb: (jax.nn.sigmoid(ga) * ya + jax.nn.sigmoid(gb) * yb,)
    mla = dict(heads=MLA_HEADS, dq=2 * HEAD_W, dv=HEAD_W, koff=0, kstride=1, voff=1, vstride=2,
               scale=MLA_QK ** -0.5, causal=True)

    def mix_fwd(x_in, plan):
        w_in = _w_in_ext(full['w_in'])
        h = pre_norm("mix", x_in, a['mix_pre_g'])
        names, carry = fetch(plan, 'in')
        u = mm("mix_in", h, w_in, 'nn', carry=carry)
        gathered(names, carry)
        w_q, w_kv = _w_q_ext(full['mla_w_q_up']), full['mla_w_kv_up']
        hg_ins = [R_(u, 2048, 1), R_(u, 2048, 0), P_(a['hgrn_lb_logits'])]
        log_f, k_in, q_in = rowmap("hg_prep", _hg_prep, hg_ins, [(F32, 2048, 0)] * 3, tile=tr)
        names, carry = fetch(plan, 'scan')
        o_a, states = hg_scan_fwd("hg_scan", q_in, k_in, log_f, u, 32, carry=carry)
        gathered(names, carry)
        out_ins = [R_(o_a, HEAD_W, 0, 1), R_(u, HEAD_W, 48, 1), P_(a['hg_norm_g'], HEAD_W, 0, 1)]
        oag = rowmap("hg_out", hg_out_fn, out_ins, [(BF16, HEAD_W, 1)], tile=th, ncol=HG_HEADS)[0]
        y_a = mm("mix_a", oag, full['w_branch_a'], 'nn')
        norm_ins = [R_(u, 512, 16), R_(u, 512, 17), P_(a['mla_q_norm_g']), P_(a['mla_kv_norm_g'])]
        cqn, ckvn = rowmap("mla_norm", mla_norm_fn, norm_ins, [(BF16, 512, 0)] * 2, tile=tr)
        q_all = mm("mla_qup", cqn, w_q, 'nn')
        kv = mm("mla_kvup", ckvn, w_kv, 'nn')
        rope_ins = [R_(q_all, HEAD_W, 0, 1), R_(q_all, HEAD_W, 16, 1), R_(q_all, HEAD_W, 32, 1),
                    R_(kv, HEAD_W, 0, 2), R_(u, HEAD_W, 104, 0), R_(u, HEAD_W, 105, 0), R_(cos), R_(sin)]
        qf, kf = rowmap("mla_rope", _rope, rope_ins, [(BF16, 2 * HEAD_W, 1)] * 2, tile=th, ncol=MLA_HEADS,
                        rows_inner=False)
        o_b = attn_fwd("mla_attn", qf, kf, kv, **mla)
        y_b = mm("mix_b", o_b, full['w_branch_b'], 'nn')
        gate_ins = [R_(u, 1024, 9, 1), R_(u, 1024, 11, 1), R_(y_a, 1024, 0, 1), R_(y_b, 1024, 0, 1)]
        y = rowmap("mix_gate", gate_fn, gate_ins, [(BF16, 1024, 1)], tile=tr, ncol=2)[0]
        yo = mm("mix_out", y, full['w_out'], 'nn')
        res = (x_in, h, hg_ins, q_in, k_in, log_f, u, states, out_ins, oag, norm_ins, cqn, ckvn, rope_ins, qf, kf,
               kv, o_b, gate_ins, y, yo, w_in, w_q, w_kv)
        return post_norm("mix", x_in, yo, a['mix_post_g'], 1.0), res

    def mix_bwd(res, d_out, plan):
        (x_in, h, hg_ins, q_in, k_in, log_f, u, states, out_ins, oag, norm_ins, cqn, ckvn, rope_ins, qf, kf, kv,
         o_b, gate_ins, y, yo, w_in, w_q, w_kv) = res
        dyo, gs['mix_post_g'] = post_norm_bwd("mix", x_in, yo, a['mix_post_g'], 1.0, d_out)
        dy = mm("mix_out_dx", dyo, full['w_out'], 'nt')
        gb['w_out'] = by_rows(mm("mix_out_dw", y, dyo, 'tn', out_dtype=BF16))
        dga, dgb, dya, dyb = rowmap("mix_gate_bwd", gate_fn, gate_ins, tile=tr, ncol=2, cts=[R_(dy, 1024, 0, 1)],
                                    wrt=[0, 1, 2, 3], gdt=[BF16] * 4)
        gb['w_branch_b'] = by_rows(mm("mix_b_dw", o_b, dyb, 'tn', out_dtype=BF16))
        do_b = mm("mix_b_dx", dyb, full['w_branch_b'], 'nt')
        names, carry = send(plan, 'attn_bwd')
        dqf, dkf, dv = attn_bwd("mla_attn_bwd", qf, kf, kv, do_b, carry=carry, **mla)
        exchanged(names, carry)
        dqn, dqp, dqs, dkn, dkpe, dksw = rowmap(
            "mla_rope_bwd", _rope, rope_ins, tile=th, ncol=MLA_HEADS, rows_inner=False,
            cts=[R_(dqf, 2 * HEAD_W, 0, 1), R_(dkf, 2 * HEAD_W, 0, 1)], wrt=[0, 1, 2, 3, 4, 5],
            gdt=[BF16, BF16, BF16, BF16, F32, F32])
        dq_all = jnp.concatenate([dqn, dqp, dqs], axis=1)
        dkv = jnp.concatenate([dkn.reshape(s, MLA_HEADS, HEAD_W), dv.astype(BF16).reshape(s, MLA_HEADS, HEAD_W)],
                              axis=2).reshape(s, 2 * MLA_HEADS * HEAD_W)
        gb['mla_w_q_up'] = by_cols(_w_q_grad(mm("mla_qup_dw", cqn, dq_all, 'tn', out_dtype=BF16)))
        dcqn = mm("mla_qup_dx", dq_all, w_q, 'nt')
        gb['mla_w_kv_up'] = by_cols(mm("mla_kvup_dw", ckvn, dkv, 'tn', out_dtype=BF16))
        dckvn = mm("mla_kvup_dx", dkv, w_kv, 'nt')
        dcq, dckv, gs['mla_q_norm_g'], gs['mla_kv_norm_g'] = rowmap(
            "mla_norm_bwd", mla_norm_fn, norm_ins, tile=tr, cts=[R_(dcqn), R_(dckvn)], wrt=[0, 1, 2, 3],
            gdt=[BF16, BF16, F32, F32])
        gb['w_branch_a'] = by_rows(mm("mix_a_dw", oag, dya, 'tn', out_dtype=BF16))
        doag = mm("mix_a_dx", dya, full['w_branch_a'], 'nt')
        do_a, dog, gs['hg_norm_g'] = rowmap("hg_out_bwd", hg_out_fn, out_ins, tile=th, ncol=HG_HEADS,
                                            cts=[R_(doag, HEAD_W, 0, 1)], wrt=[0, 1, 2], gdt=[F32, BF16, F32])
        names, carry = send(plan, 'scan_bwd')
        dq_in, dk_in, dlog_f, di = hg_scan_bwd("hg_scan_bwd", q_in, k_in, log_f, u, 32, states, do_a, carry=carry)
        exchanged(names, carry)
        df, dq_hg, gs['hgrn_lb_logits'] = rowmap("hg_prep_bwd", _hg_prep, hg_ins, tile=tr,
                                                 cts=[R_(dlog_f), R_(dk_in), R_(dq_in)], wrt=[0, 1, 2],
                                                 gdt=[BF16, BF16, F32])
        du = jnp.concatenate([dq_hg, df, di.astype(BF16), dog, dcq, dckv, dga, dgb, dkpe.astype(BF16),
                              dksw.astype(BF16), jnp.zeros((s, U_PAD), BF16)], axis=1)
        gb['w_in'] = by_cols(_w_in_grad(mm("mix_in_dw", h, du, 'tn', out_dtype=BF16)))
        names, carry = send(plan, 'in_dx')
        dh = mm("mix_in_dx", du, w_in, 'nt', carry=carry)
        exchanged(names, carry)
        d_in, gs['mix_pre_g'] = pre_norm_bwd("mix", x_in, a['mix_pre_g'], dh, d_out)
        return d_in

    xa = dict(heads=XA_HEADS, dq=HEAD_W, dv=HEAD_W, koff=0, kstride=1, voff=XA_HEADS, vstride=1,
              scale=HEAD_W ** -0.5, causal=False)
    tm_ = min(nm, 128)

    def xa_fwd(x_in):
        w_xkv = jnp.concatenate([full['xa_w_k'], full['xa_w_v']], axis=1)
        h = pre_norm("xa", x_in, a['xa_pre_g'])
        mn = rowmap("xa_mem", pre_fn, [R_(mem), P_(a['xa_mem_g'])], [(BF16, d, 0)], tile=tm_)[0]
        q = mm("xa_q", h, full['xa_w_q'], 'nn')
        kv = mm("xa_kv", mn, w_xkv, 'nn')
        o = attn_fwd("xa_attn", q, kv, kv, **xa)
        yo = mm("xa_o", o, full['xa_w_o'], 'nn')
        return post_norm("xa", x_in, yo, a['xa_post_g'], 1.0), (x_in, h, mn, q, kv, o, yo, w_xkv)

    def xa_bwd(res, d_out):
        x_in, h, mn, q, kv, o, yo, w_xkv = res
        dyo, gs['xa_post_g'] = post_norm_bwd("xa", x_in, yo, a['xa_post_g'], 1.0, d_out)
        do = mm("xa_o_dx", dyo, full['xa_w_o'], 'nt')
        gb['xa_w_o'] = by_cols(mm("xa_o_dw", o, dyo, 'tn', out_dtype=BF16))
        dq, dk, dv = attn_bwd("xa_attn_bwd", q, kv, kv, do, **xa)
        dkv = jnp.concatenate([dk, dv], axis=1).astype(BF16)
        dw = mm("xa_kv_dw", mn, dkv, 'tn', out_dtype=BF16)
        gb['xa_w_k'], gb['xa_w_v'] = by_rows(dw[:, :XA_HEADS * HEAD_W]), by_rows(dw[:, XA_HEADS * HEAD_W:])
        dmn = mm("xa_kv_dx", dkv, w_xkv, 'nt')
        gs['xa_mem_g'] = rowmap("xa_mem_bwd", pre_fn, [R_(mem), P_(a['xa_mem_g'])], tile=tm_, cts=[R_(dmn)],
                                wrt=[1], gdt=[F32])[0]
        gb['xa_w_q'] = by_rows(mm("xa_q_dw", h, dq, 'tn', out_dtype=BF16))
        dh = mm("xa_q_dx", dq, full['xa_w_q'], 'nt')
        d_in, gs['xa_pre_g'] = pre_norm_bwd("xa", x_in, a['xa_pre_g'], dh, d_out)
        return d_in

    landed = {}

    def exchanged(names, carry):
        landed.update(zip(names or [], carry.results if carry else []))

    x1, r1 = ffn_fwd('ffn1', x, {'gate': ['ffn1_w_up'], 'up': ['ffn1_w_down'], 'dn': ['w_in']})
    x2, r2 = mix_fwd(x1, {'in': ['mla_w_q_up', 'mla_w_kv_up', 'w_branch_a', 'w_branch_b', 'w_out'],
                          'scan': xa_names + ffn2_names})
    x3, r3 = xa_fwd(x2)
    x4, r4 = ffn_fwd('ffn2', x3, {})

    def loss_fn(y, t):
        diff = y - t
        return diff * (1.0 / d), jnp.mean(diff * diff, axis=-1, keepdims=True)

    d4, row_loss = rowmap("loss", loss_fn, [R_(x4), R_(target)], [(F32, d, 0), (F32, 1, 0)], tile=tr)
    loss = lax.psum(0.5 * jnp.sum(row_loss), ("x", "y", "c"))
    d3 = ffn_bwd('ffn2', r4, d4, {})
    d2 = xa_bwd(r3, d3)
    d1 = mix_bwd(r2, d2, {'attn_bwd': ['w_out', 'w_branch_b'],
                          'scan_bwd': ffn2_names + xa_names + ['mla_w_q_up', 'mla_w_kv_up', 'w_branch_a'],
                          'in_dx': ['w_in']})
    grad_x = ffn_bwd('ffn1', r1, d1, {'gu_dx': ['ffn1_w_down'], 'up_dw': ['ffn1_w_gate'],
                                      'pre_bwd': ['ffn1_w_up']})

    def pack_small(vals):
        flat = jnp.concatenate([vals[n].reshape(-1) for n in SMALL])
        rows = -(-flat.shape[0] // PACK_W)
        rows = -(-rows // 8) * 8
        return jnp.pad(flat, (0, rows * PACK_W - flat.shape[0])).reshape(rows, PACK_W)

    def unpack_small(buf):
        flat, out, at = buf.reshape(-1), {}, 0
        for n in SMALL:
            size = a[n].shape[0] * a[n].shape[1]
            out[n] = flat[at:at + size].reshape(a[n].shape)
            at += size
        return out

    small = Gather([pack_small(gs)])
    run_alone("gather_g_small", small)

    grads, delta, new_m, new_v = {}, {}, {}, {}
    packs = adamw("adamw_small", small.results[0], pack_small(a), pack_small({n: a['m_' + n] for n in SMALL}),
                  pack_small({n: a['v_' + n] for n in SMALL}))
    for dst, buf in zip((grads, delta, new_m, new_v), packs):
        dst.update(unpack_small(buf))
    for n in BIG:
        outs = adamw("adamw_" + n, landed[n], a[n][0], a['m_' + n][0], a['v_' + n][0])
        grads[n], delta[n], new_m[n], new_v[n] = (t.reshape(a[n].shape) for t in outs)

    return (loss, grad_x[None], *[grads[n] for n in WEIGHTS], *[delta[n] for n in WEIGHTS],
            *[new_m[n] for n in WEIGHTS], *[new_v[n] for n in WEIGHTS])


def kernel(x, mem, positions, hgrn_lb_logits, ffn1_pre_g, ffn1_w_gate, ffn1_w_up, ffn1_w_down, ffn1_post_g, mix_pre_g, w_in, hg_norm_g, mla_q_norm_g, mla_w_q_up, mla_kv_norm_g, mla_w_kv_up, w_branch_a, w_branch_b, w_out, mix_post_g, xa_pre_g, xa_mem_g, xa_w_q, xa_w_k, xa_w_v, xa_w_o, xa_post_g, ffn2_pre_g, ffn2_w_gate, ffn2_w_up, ffn2_w_down, ffn2_post_g, loss_target, m_hgrn_lb_logits, m_ffn1_pre_g, m_ffn1_w_gate, m_ffn1_w_up, m_ffn1_w_down, m_ffn1_post_g, m_mix_pre_g, m_w_in, m_hg_norm_g, m_mla_q_norm_g, m_mla_w_q_up, m_mla_kv_norm_g, m_mla_w_kv_up, m_w_branch_a, m_w_branch_b, m_w_out, m_mix_post_g, m_xa_pre_g, m_xa_mem_g, m_xa_w_q, m_xa_w_k, m_xa_w_v, m_xa_w_o, m_xa_post_g, m_ffn2_pre_g, m_ffn2_w_gate, m_ffn2_w_up, m_ffn2_w_down, m_ffn2_post_g, v_hgrn_lb_logits, v_ffn1_pre_g, v_ffn1_w_gate, v_ffn1_w_up, v_ffn1_w_down, v_ffn1_post_g, v_mix_pre_g, v_w_in, v_hg_norm_g, v_mla_q_norm_g, v_mla_w_q_up, v_mla_kv_norm_g, v_mla_w_kv_up, v_w_branch_a, v_w_branch_b, v_w_out, v_mix_post_g, v_xa_pre_g, v_xa_mem_g, v_xa_w_q, v_xa_w_k, v_xa_w_v, v_xa_w_o, v_xa_post_g, v_ffn2_pre_g, v_ffn2_w_gate, v_ffn2_w_up, v_ffn2_w_down, v_ffn2_post_g):
    return _step(dict(locals()))
```

```python
import functools

import jax
import jax.numpy as jnp
from jax import lax
from jax.experimental import pallas as pl
from jax.experimental.pallas import tpu as pltpu

F32 = jnp.float32
BF16 = jnp.bfloat16

N_DEV = 8
D_MODEL = 2048
CHUNK = 64
CHUNK_SHIFT = 6
SUB = 16
HG_HEADS = 16
HG_GROUP = 4
HEAD_W = 128
MLA_HEADS = 16
Q_LORA = 512
KV_LORA = 512
QK_ROPE = 64
MLA_QK = 192
XA_HEADS = 4
ROPE_THETA = 10000.0
EPS = 1e-6
PACK_W = 1024
VMEM_LIMIT = 56 * 1024 * 1024

ADAM_LR = 0.001
ADAM_B1 = 0.9
ADAM_B2 = 0.999
ADAM_EPS = 1e-08
ADAM_WD = 0.01
ADAM_STEP = 10

U_PAD = 256

WEIGHTS = ['hgrn_lb_logits', 'ffn1_pre_g', 'ffn1_w_gate', 'ffn1_w_up', 'ffn1_w_down', 'ffn1_post_g', 'mix_pre_g',
           'w_in', 'hg_norm_g', 'mla_q_norm_g', 'mla_w_q_up', 'mla_kv_norm_g', 'mla_w_kv_up', 'w_branch_a',
           'w_branch_b', 'w_out', 'mix_post_g', 'xa_pre_g', 'xa_mem_g', 'xa_w_q', 'xa_w_k', 'xa_w_v', 'xa_w_o',
           'xa_post_g', 'ffn2_pre_g', 'ffn2_w_gate', 'ffn2_w_up', 'ffn2_w_down', 'ffn2_post_g']
BIG = {'ffn1_w_gate': True, 'ffn1_w_up': True, 'ffn1_w_down': False, 'w_in': True, 'mla_w_q_up': True,
       'mla_w_kv_up': True, 'w_branch_a': False, 'w_branch_b': False, 'w_out': False, 'xa_w_q': False,
       'xa_w_k': False, 'xa_w_v': False, 'xa_w_o': True, 'ffn2_w_gate': True, 'ffn2_w_up': True,
       'ffn2_w_down': False}
SMALL = [n for n in WEIGHTS if n not in BIG]


def _cparams(**kw):
    return pltpu.CompilerParams(vmem_limit_bytes=VMEM_LIMIT, **kw)


def _pick(dim, cands):
    for c in cands:
        if dim % c == 0:
            return c
    return dim


def _place():
    return lax.axis_index("x"), lax.axis_index("y"), lax.axis_index("c")


def _slot(px, py, pc):
    return 4 * px + 2 * py + pc


class Gather:
    def __init__(self, tensors):
        self.operands = list(tensors)
        self.out_shape = [jax.ShapeDtypeStruct((N_DEV,) + t.shape, t.dtype) for t in tensors]
        n = len(tensors)
        self.scratch = [pltpu.SemaphoreType.DMA((n, N_DEV - 1)), pltpu.SemaphoreType.DMA((n, N_DEV - 1)),
                        pltpu.SemaphoreType.DMA((n,))]
        self.results = None

    def _copies(self, t, x_ref, out_ref, send, recv):
        x, y, c = _place()
        chips = [(1 - x, y), (x, 1 - y), (1 - x, 1 - y)]

        def copy(k, block, to, src=None):
            rows = out_ref.at[_slot(*block)]
            return pltpu.make_async_remote_copy(src_ref=rows if src is None else src, dst_ref=rows,
                                                send_sem=send.at[t, k], recv_sem=recv.at[t, k], device_id=to,
                                                device_id_type=pl.DeviceIdType.MESH)

        return (x, y, c), chips, copy

    def start(self, ins, outs, scr):
        send, recv, local = scr
        for t, (x_ref, out_ref) in enumerate(zip(ins, outs)):
            (x, y, c), chips, copy = self._copies(t, x_ref, out_ref, send, recv)
            pltpu.make_async_copy(x_ref, out_ref.at[_slot(x, y, c)], local.at[t]).start()
            copy(0, (x, y, c), (x, y, 1 - c), src=x_ref).start()
            for j, chip in enumerate(chips):
                copy(1 + j, (x, y, c), (*chip, c), src=x_ref).start()

    def finish(self, ins, outs, scr):
        send, recv, local = scr
        for t, (x_ref, out_ref) in enumerate(zip(ins, outs)):
            (x, y, c), chips, copy = self._copies(t, x_ref, out_ref, send, recv)
            passed = [copy(4 + j, (*chip, c), (x, y, 1 - c)) for j, chip in enumerate(chips)]
            for j, chip in enumerate(chips):
                copy(1 + j, (*chip, c), (x, y, c)).wait_recv()
                passed[j].start()
            copy(0, (x, y, 1 - c), (x, y, c)).wait_recv()
            for j, chip in enumerate(chips):
                copy(4 + j, (*chip, 1 - c), (x, y, c)).wait_recv()
            copy(0, (x, y, c), (x, y, 1 - c), src=x_ref).wait_send()
            for j, chip in enumerate(chips):
                copy(1 + j, (x, y, c), (*chip, c), src=x_ref).wait_send()
                passed[j].wait_send()
            pltpu.make_async_copy(x_ref, out_ref.at[_slot(x, y, c)], local.at[t]).wait()

    def set_results(self, res):
        self.results = list(res)


N_CHIP = N_DEV // 2


class Halve:
    def __init__(self, tensors):
        self.operands = list(tensors)
        self.out_shape = [jax.ShapeDtypeStruct((N_CHIP,) + t.shape[1:], t.dtype) for t in tensors for _ in range(2)]
        n = len(tensors)
        self.scratch = [pltpu.SemaphoreType.DMA((n, N_CHIP)) for _ in range(3)]

    def _copies(self, t, x_ref, mine_ref, theirs_ref, scr):
        send, recv, local = scr
        x, y, c = _place()
        for q in range(N_CHIP):
            keep = pltpu.make_async_copy(x_ref.at[2 * q + c], mine_ref.at[q], local.at[t, q])
            give = pltpu.make_async_remote_copy(
                src_ref=x_ref.at[2 * q + 1 - c], dst_ref=theirs_ref.at[q], send_sem=send.at[t, q],
                recv_sem=recv.at[t, q], device_id=(x, y, 1 - c), device_id_type=pl.DeviceIdType.MESH)
            yield keep, give

    def start(self, ins, outs, scr):
        for t, x_ref in enumerate(ins):
            for keep, give in self._copies(t, x_ref, outs[2 * t], outs[2 * t + 1], scr):
                keep.start()
                give.start()

    def finish(self, ins, outs, scr):
        for t, x_ref in enumerate(ins):
            for keep, give in self._copies(t, x_ref, outs[2 * t], outs[2 * t + 1], scr):
                give.wait_recv()
                give.wait_send()
                keep.wait()

    def set_results(self, res):
        self.results = [(res[2 * t], res[2 * t + 1]) for t in range(len(self.operands))]


class Spread:
    def __init__(self, tensors):
        self.operands = list(tensors)
        self.out_shape = [jax.ShapeDtypeStruct(t.shape, t.dtype) for t in tensors]
        n = len(tensors)
        self.scratch = [pltpu.SemaphoreType.DMA((n, N_CHIP - 1)), pltpu.SemaphoreType.DMA((n, N_CHIP - 1)),
                        pltpu.SemaphoreType.DMA((n,))]

    def _copies(self, t, y_ref, out_ref, scr):
        send, recv, local = scr
        x, y, c = _place()
        me = 2 * x + y
        keep = pltpu.make_async_copy(y_ref.at[me], out_ref.at[me], local.at[t])
        gives, takes = [], []
        for k in range(1, N_CHIP):
            px, py = x ^ (k >> 1), y ^ (k & 1)
            peer = 2 * px + py
            for dst, bucket in ((me, gives), (peer, takes)):
                bucket.append(pltpu.make_async_remote_copy(
                    src_ref=y_ref.at[peer], dst_ref=out_ref.at[dst], send_sem=send.at[t, k - 1],
                    recv_sem=recv.at[t, k - 1], device_id=(px, py, c), device_id_type=pl.DeviceIdType.MESH))
        return keep, gives, takes

    def start(self, ins, outs, scr):
        for t, (y_ref, out_ref) in enumerate(zip(ins, outs)):
            keep, gives, _ = self._copies(t, y_ref, out_ref, scr)
            keep.start()
            for give in gives:
                give.start()

    def finish(self, ins, outs, scr):
        for t, (y_ref, out_ref) in enumerate(zip(ins, outs)):
            keep, gives, takes = self._copies(t, y_ref, out_ref, scr)
            for take in takes:
                take.wait_recv()
            for give in gives:
                give.wait_send()
            keep.wait()

    def set_results(self, res):
        self.results = list(res)


class Joint:
    def __init__(self, parts):
        self.parts = list(parts)
        self.operands = [o for p in parts for o in p.operands]
        self.out_shape = [o for p in parts for o in p.out_shape]
        self.scratch = [o for p in parts for o in p.scratch]

    def _split(self, ins, outs, scr):
        i = o = s = 0
        for p in self.parts:
            ni, no, ns = len(p.operands), len(p.out_shape), len(p.scratch)
            yield p, ins[i:i + ni], outs[o:o + no], scr[s:s + ns]
            i, o, s = i + ni, o + no, s + ns

    def start(self, ins, outs, scr):
        for p, a, b, c in self._split(ins, outs, scr):
            p.start(a, b, c)

    def finish(self, ins, outs, scr):
        for p, a, b, c in self._split(ins, outs, scr):
            p.finish(a, b, c)

    def set_results(self, res):
        for p, _, part, _ in self._split([], list(res), []):
            p.set_results(part)


_HBM = pl.BlockSpec(memory_space=pltpu.HBM)


def run_alone(name, carry):
    n_in, n_out = len(carry.operands), len(carry.out_shape)

    def body(*refs):
        ins, outs, scr = refs[:n_in], refs[n_in:n_in + n_out], refs[n_in + n_out:]
        carry.start(ins, outs, scr)
        carry.finish(ins, outs, scr)

    res = pl.pallas_call(body, name=name, in_specs=[_HBM] * n_in, out_specs=[_HBM] * n_out,
                         out_shape=carry.out_shape, scratch_shapes=carry.scratch)(*carry.operands)
    carry.set_results(list(res))


def _call(name, body, *, grid, in_specs, out_specs, out_shape, args, scratch=(), carry=None):
    in_specs, out_specs, out_shape, scratch = list(in_specs), list(out_specs), list(out_shape), list(scratch)
    if carry is None:
        return list(pl.pallas_call(body, name=name, grid=grid, in_specs=in_specs, out_specs=out_specs,
                                   out_shape=out_shape, scratch_shapes=scratch, compiler_params=_cparams())(*args))
    n_in, n_out, n_scr = len(in_specs), len(out_shape), len(scratch)
    c_in, c_out = len(carry.operands), len(carry.out_shape)

    def wrapped(*refs):
        ins, c_ins = refs[:n_in], refs[n_in:n_in + c_in]
        at = n_in + c_in
        outs, c_outs = refs[at:at + n_out], refs[at + n_out:at + n_out + c_out]
        at += n_out + c_out
        scr, c_scr = refs[at:at + n_scr], refs[at + n_scr:]
        first = functools.reduce(jnp.logical_and, [pl.program_id(d) == 0 for d in range(len(grid))])
        last = functools.reduce(jnp.logical_and, [pl.program_id(d) == grid[d] - 1 for d in range(len(grid))])

        @pl.when(first)
        def _():
            carry.start(c_ins, c_outs, c_scr)

        body(*ins, *outs, *scr)

        @pl.when(last)
        def _():
            carry.finish(c_ins, c_outs, c_scr)

    res = pl.pallas_call(
        wrapped, name=name, grid=grid, in_specs=in_specs + [_HBM] * c_in, out_specs=out_specs + [_HBM] * c_out,
        out_shape=out_shape + carry.out_shape, scratch_shapes=scratch + carry.scratch, compiler_params=_cparams(),
    )(*args, *carry.operands)
    carry.set_results(list(res[n_out:]))
    return list(res[:n_out])


def R_(arr, w=None, off=0, stride=0):
    return ('r', arr, arr.shape[1] if w is None else w, off, stride)


def P_(arr, w=None, off=0, stride=0):
    return ('p', arr, arr.shape[1] if w is None else w, off, stride)


def rowmap(name, fn, ins, outs=None, *, tile, ncol=1, rows_inner=True, cts=None, wrt=None, gdt=None, cat=False,
           carry=None):
    rows = next(a.shape[0] for k, a, *_ in ins if k == 'r')
    nrow = rows // tile
    assert nrow * tile == rows
    grid = (ncol, nrow) if rows_inner else (nrow, ncol)

    def ij(g0, g1):
        return (g1, g0) if rows_inner else (g0, g1)

    def spec(kind, arr, w, off, stride):
        if kind == 'r':
            return pl.BlockSpec((tile, w), lambda g0, g1: (ij(g0, g1)[0], off + stride * ij(g0, g1)[1]))
        return pl.BlockSpec((arr.shape[0], w), lambda g0, g1: (0, off + stride * ij(g0, g1)[1]))

    ops = list(ins) + list(cts or [])
    in_specs = [spec(*o) for o in ops]
    n_in = len(ins)
    fwd = cts is None
    out_shape, out_specs, acc = [], [], []
    if fwd:
        for dt, w, stride in outs:
            out_shape.append(jax.ShapeDtypeStruct((rows, w * (ncol if stride else 1)), dt))
            out_specs.append(spec('r', None, w, 0, stride))
            acc.append(None)
    elif cat:
        widths = [ins[i][2] for i in wrt]
        assert ncol == 1 and all(ins[i][0] == 'r' for i in wrt)
        out_shape.append(jax.ShapeDtypeStruct((rows, sum(widths)), gdt))
        out_specs.append(spec('r', None, sum(widths), 0, 0))
    else:
        for n, i in enumerate(wrt):
            kind, arr, w, off, stride = ins[i]
            width = w * (ncol if stride else 1)
            if kind == 'r':
                out_shape.append(jax.ShapeDtypeStruct((rows, width), gdt[n]))
                out_specs.append(spec('r', None, w, 0, 1 if stride else 0))
                shared = stride == 0 and ncol > 1
                assert not shared or (not rows_inner and gdt[n] == F32)
                acc.append('col' if shared else None)
            else:
                out_shape.append(jax.ShapeDtypeStruct((arr.shape[0], width), F32))
                out_specs.append(spec('p', arr, w, 0, 1 if stride else 0))
                assert rows_inner or ncol == 1
                acc.append('row')

    def body(*refs):
        i, j = ij(pl.program_id(0), pl.program_id(1))
        vals = [r[...].astype(F32) for r in refs[:n_in]]
        out_refs = refs[len(ops):]
        if fwd:
            for o_ref, o in zip(out_refs, fn(*vals)):
                o_ref[...] = o.astype(o_ref.dtype)
            return

        def f(*d):
            full = list(vals)
            for n, idx in enumerate(wrt):
                full[idx] = d[n]
            return fn(*full)

        _, vjp = jax.vjp(f, *[vals[idx] for idx in wrt])
        grads = vjp(tuple(r[...].astype(F32) for r in refs[n_in:len(ops)]))
        if cat:
            o_ref, at = out_refs[0], 0
            for g in grads:
                o_ref[:, at:at + g.shape[1]] = g.astype(o_ref.dtype)
                at += g.shape[1]
            return
        for o_ref, g, a in zip(out_refs, grads, acc):
            if a is None:
                o_ref[...] = g.astype(o_ref.dtype)
            else:
                first = (i if a == 'row' else j) == 0

                @pl.when(first)
                def _(o_ref=o_ref):
                    o_ref[...] = jnp.zeros_like(o_ref)

                o_ref[...] += g

    return _call(name, body, grid=grid, in_specs=in_specs, out_specs=out_specs, out_shape=out_shape,
                 args=[o[1] for o in ops], carry=carry)


def _rms(x, g):
    return x * lax.rsqrt(jnp.mean(x * x, axis=-1, keepdims=True) + EPS) * g


_DIMS = {'nn': (((1,), (0,)), ((), ())), 'nt': (((1,), (1,)), ((), ())), 'tn': (((0,), (0,)), ((), ()))}


def mm(name, a, b, mode, *, jdim=None, out_dtype=F32, carry=None):
    a_list = list(a) if isinstance(a, (list, tuple)) else [a]
    b_list = list(b) if isinstance(b, (list, tuple)) else [b]
    a_order = ('k', 'm') if mode == 'tn' else ('m', 'k')
    b_order = ('n', 'k') if mode == 'nt' else ('k', 'n')
    size, blocks = {}, 1
    for arr, order in ((a_list[0], a_order), (b_list[0], b_order)):
        shape = arr.shape
        if jdim in order:
            blocks, shape = shape[0], shape[1:]
        for dname, extent in zip(order, shape):
            assert size.setdefault(dname, extent) == extent
    tile = {'m': _pick(size['m'], (1024, 512, 256, 128)), 'n': _pick(size['n'], (512, 256, 128)),
            'k': size['k'] if size['k'] <= 2048 else _pick(size['k'], (2304, 2048, 1024, 512, 256, 128))}
    if jdim is not None:
        tile[jdim] = size[jdim]
    if a_list[0].dtype == F32 and tile['k'] * tile['m'] > (1 << 20):
        tile['m'] = _pick(size['m'], (512, 256, 128))
    grid = tuple(blocks if d == jdim else size[d] // tile[d] for d in ('m', 'n', 'k'))
    nk = grid[2]

    def spec(order):
        shape = tuple(tile[d] for d in order)

        def imap(i, j, k):
            g = {'m': i, 'n': j, 'k': k}
            idx = tuple(0 if d == jdim else g[d] for d in order)
            return ((g[jdim],) + idx) if jdim in order else idx

        return pl.BlockSpec(((None,) + shape) if jdim in order else shape, imap)

    dims = _DIMS[mode]
    nt = len(a_list)

    def product(refs):
        acc = None
        for a_ref, b_ref in zip(refs[:nt], refs[nt:2 * nt]):
            p = lax.dot_general(a_ref[...].astype(BF16), b_ref[...].astype(BF16), dims, preferred_element_type=F32)
            acc = p if acc is None else acc + p
        return acc

    def body_once(*refs):
        refs[2 * nt][...] = product(refs).astype(refs[2 * nt].dtype)

    def body_acc(*refs):
        o_ref, acc_ref = refs[2 * nt], refs[2 * nt + 1]
        k = pl.program_id(2)

        @pl.when(k == 0)
        def _():
            acc_ref[...] = jnp.zeros_like(acc_ref)

        acc_ref[...] += product(refs)

        @pl.when(k == nk - 1)
        def _():
            o_ref[...] = acc_ref[...].astype(o_ref.dtype)

    out_dims = (size['m'], size['n'])
    out_shape = jax.ShapeDtypeStruct(((blocks,) + out_dims) if jdim in ('m', 'n') else out_dims, out_dtype)
    return _call(name, body_once if nk == 1 else body_acc, grid=grid,
                 in_specs=[spec(a_order)] * nt + [spec(b_order)] * nt, out_specs=[spec(('m', 'n'))],
                 out_shape=[out_shape], args=a_list + b_list,
                 scratch=[] if nk == 1 else [pltpu.VMEM((tile['m'], tile['n']), F32)], carry=carry)[0]


def _probs(q, k, i, tq, scale, causal):
    s = lax.dot_general(q, k, _DIMS['nt'], preferred_element_type=F32) * scale
    if causal:
        shape = s.shape
        q_chunk = jnp.right_shift(i * tq + lax.broadcasted_iota(jnp.int32, shape, 0), CHUNK_SHIFT)
        k_chunk = jnp.right_shift(lax.broadcasted_iota(jnp.int32, shape, 1), CHUNK_SHIFT)
        s = jnp.where(k_chunk <= q_chunk, s, -jnp.inf)
    e = jnp.exp(s - jnp.max(s, axis=-1, keepdims=True))
    return e, jnp.sum(e, axis=-1, keepdims=True)


def _attn_specs(tq, sk, dq, dv, koff, kstride, voff, vstride):
    return [pl.BlockSpec((tq, dq), lambda h, i: (i, h)),
            pl.BlockSpec((sk, dq), lambda h, i: (0, koff + kstride * h)),
            pl.BlockSpec((sk, dv), lambda h, i: (0, voff + vstride * h))]


def attn_fwd(name, q, k, v, *, heads, dq, dv, koff, kstride, voff, vstride, scale, causal, carry=None):
    sq, sk = q.shape[0], k.shape[0]
    tq = min(sq, 256)

    def body(q_ref, k_ref, v_ref, o_ref):
        e, l = _probs(q_ref[...].astype(BF16), k_ref[...].astype(BF16), pl.program_id(1), tq, scale, causal)
        o = jnp.dot(e.astype(BF16), v_ref[...].astype(BF16), preferred_element_type=F32)
        o_ref[...] = o / l

    return _call(name, body, grid=(heads, sq // tq),
                 in_specs=_attn_specs(tq, sk, dq, dv, koff, kstride, voff, vstride),
                 out_specs=[pl.BlockSpec((tq, dv), lambda h, i: (i, h))],
                 out_shape=[jax.ShapeDtypeStruct((sq, heads * dv), F32)], args=[q, k, v], carry=carry)[0]


def attn_bwd(name, q, k, v, do, *, heads, dq, dv, koff, kstride, voff, vstride, scale, causal, carry=None):
    sq, sk = q.shape[0], k.shape[0]
    tq = min(sq, 256)

    def body(q_ref, k_ref, v_ref, do_ref, dq_ref, dk_ref, dv_ref):
        i = pl.program_id(1)
        qb, kb, vb = q_ref[...].astype(BF16), k_ref[...].astype(BF16), v_ref[...].astype(BF16)
        dob = do_ref[...].astype(BF16)
        e, l = _probs(qb, kb, i, tq, scale, causal)
        p = e / l
        dp = lax.dot_general(dob, vb, _DIMS['nt'], preferred_element_type=F32)
        ds = (p * (dp - jnp.sum(dp * p, axis=-1, keepdims=True)) * scale).astype(BF16)

        @pl.when(i == 0)
        def _():
            dk_ref[...] = jnp.zeros_like(dk_ref)
            dv_ref[...] = jnp.zeros_like(dv_ref)

        dv_ref[...] += lax.dot_general(p.astype(BF16), dob, _DIMS['tn'], preferred_element_type=F32)
        dk_ref[...] += lax.dot_general(ds, qb, _DIMS['tn'], preferred_element_type=F32)
        dq_ref[...] = jnp.dot(ds, kb, preferred_element_type=F32)

    return _call(
        name, body, grid=(heads, sq // tq),
        in_specs=_attn_specs(tq, sk, dq, dv, koff, kstride, voff, vstride) + [pl.BlockSpec((tq, dv), lambda h, i: (i, h))],
        out_specs=[pl.BlockSpec((tq, dq), lambda h, i: (i, h)), pl.BlockSpec((sk, dq), lambda h, i: (0, h)),
                   pl.BlockSpec((sk, dv), lambda h, i: (0, h))],
        out_shape=[jax.ShapeDtypeStruct((sq, heads * dq), F32), jax.ShapeDtypeStruct((sk, heads * dq), F32),
                   jax.ShapeDtypeStruct((sk, heads * dv), F32)],
        args=[q, k, v, do], carry=carry)


def _hg_chunk(q, k, g, v, state):
    c = q.shape[0]
    row = lax.broadcasted_iota(jnp.int32, (c, c), 0)
    col = lax.broadcasted_iota(jnp.int32, (c, c), 1)
    tril = (col <= row).astype(F32)
    b = jnp.dot(tril, g, precision=lax.Precision.HIGHEST, preferred_element_type=F32)
    rows = lax.broadcasted_iota(jnp.int32, (c, 1), 0)
    o = jnp.dot((q * jnp.exp(b)).astype(BF16), state.astype(BF16), preferred_element_type=F32)
    t3 = lax.broadcasted_iota(jnp.int32, (SUB, SUB, 1), 0)
    s3 = lax.broadcasted_iota(jnp.int32, (SUB, SUB, 1), 1)
    parts = []
    for n in range(c // SUB):
        lo = n * SUB
        qn, kn, bn, vn = q[lo:lo + SUB], k[lo:lo + SUB], b[lo:lo + SUB], v[lo:lo + SUB]
        decay = jnp.exp(jnp.where(s3 <= t3, bn[:, None, :] - bn[None, :, :], -jnp.inf))
        sc = jnp.sum(qn[:, None, :] * kn[None, :, :] * decay, axis=-1)
        on = jnp.dot(sc.astype(BF16), vn.astype(BF16), preferred_element_type=F32)
        if n > 0:
            ref = jnp.sum(jnp.where(rows == lo - 1, b, 0.0), axis=0, keepdims=True)
            qd = (qn * jnp.exp(bn - ref)).astype(BF16)
            kd = (k[:lo] * jnp.exp(ref - b[:lo])).astype(BF16)
            so = lax.dot_general(qd, kd, _DIMS['nt'], preferred_element_type=F32)
            on = on + jnp.dot(so.astype(BF16), v[:lo].astype(BF16), preferred_element_type=F32)
        parts.append(on)
    o = o + jnp.concatenate(parts, axis=0)
    b_last = jnp.sum(g, axis=0, keepdims=True)
    ones = jnp.ones((c, 1), F32)
    b_last_col = lax.dot_general(g, ones, _DIMS['tn'], precision=lax.Precision.HIGHEST, preferred_element_type=F32)
    kd = (k * jnp.exp(b_last - b)).astype(BF16)
    new_state = jnp.exp(b_last_col) * state + lax.dot_general(kd, v.astype(BF16), _DIMS['tn'],
                                                              preferred_element_type=F32)
    return o, new_state


def hg_scan_fwd(name, q, k, g, u, v_off, carry=None):
    s = q.shape[0]
    n = s // CHUNK

    def body(q_ref, k_ref, g_ref, v_ref, o_ref, st_ref, state):
        @pl.when(pl.program_id(1) == 0)
        def _():
            state[...] = jnp.zeros_like(state)

        for j in range(HG_GROUP):
            cols = slice(j * HEAD_W, (j + 1) * HEAD_W)
            st = state[j]
            st_ref[j] = st
            o, new = _hg_chunk(q_ref[:, cols], k_ref[:, cols], g_ref[:, cols], v_ref[:, cols], st)
            o_ref[:, cols] = o
            state[j] = new

    wide = HG_GROUP * HEAD_W
    blk = pl.BlockSpec((CHUNK, wide), lambda h, c: (c, h))
    return _call(
        name, body, grid=(HG_HEADS // HG_GROUP, n),
        in_specs=[blk, blk, blk, pl.BlockSpec((CHUNK, wide), lambda h, c: (c, v_off // HG_GROUP + h))],
        out_specs=[blk, pl.BlockSpec((HG_GROUP, None, HEAD_W, HEAD_W), lambda h, c: (h, c, 0, 0))],
        out_shape=[jax.ShapeDtypeStruct((s, HG_HEADS * HEAD_W), F32),
                   jax.ShapeDtypeStruct((HG_HEADS, n, HEAD_W, HEAD_W), F32)],
        scratch=[pltpu.VMEM((HG_GROUP, HEAD_W, HEAD_W), F32)], args=[q, k, g, u], carry=carry)


def hg_scan_bwd(name, q, k, g, u, v_off, states, do, carry=None):
    s = q.shape[0]
    n = s // CHUNK

    def body(q_ref, k_ref, g_ref, v_ref, st_ref, do_ref, dq_ref, dk_ref, dg_ref, dv_ref, dstate):
        @pl.when(pl.program_id(1) == 0)
        def _():
            dstate[...] = jnp.zeros_like(dstate)

        for j in range(HG_GROUP):
            cols = slice(j * HEAD_W, (j + 1) * HEAD_W)
            _, vjp = jax.vjp(_hg_chunk, q_ref[:, cols], k_ref[:, cols], g_ref[:, cols], v_ref[:, cols], st_ref[j])
            dq, dk, dg, dv, dst = vjp((do_ref[:, cols], dstate[j]))
            dq_ref[:, cols] = dq
            dk_ref[:, cols] = dk
            dg_ref[:, cols] = dg
            dv_ref[:, cols] = dv
            dstate[j] = dst

    wide = HG_GROUP * HEAD_W
    blk = pl.BlockSpec((CHUNK, wide), lambda h, c: (n - 1 - c, h))
    out = jax.ShapeDtypeStruct((s, HG_HEADS * HEAD_W), F32)
    return _call(
        name, body, grid=(HG_HEADS // HG_GROUP, n),
        in_specs=[blk, blk, blk, pl.BlockSpec((CHUNK, wide), lambda h, c: (n - 1 - c, v_off // HG_GROUP + h)),
                  pl.BlockSpec((HG_GROUP, None, HEAD_W, HEAD_W), lambda h, c: (h, n - 1 - c, 0, 0)), blk],
        out_specs=[blk, blk, blk, blk], out_shape=[out, out, out, out],
        scratch=[pltpu.VMEM((HG_GROUP, HEAD_W, HEAD_W), F32)], args=[q, k, g, u, states, do], carry=carry)


def pair_sum(name, mine, theirs):
    shape = mine.shape
    rows, cols = shape[0] * shape[1], shape[2]
    tile = next((t for t in (1024, 512, 256, 128, 64, 32, 16) if rows % t == 0 and t * cols <= (1 << 19)), rows)

    def body(a_ref, b_ref, o_ref):
        o_ref[...] = (a_ref[...].astype(F32) + b_ref[...].astype(F32)).astype(o_ref.dtype)

    blk = pl.BlockSpec((tile, cols), lambda i: (i, 0))
    return _call(name, body, grid=(rows // tile,), in_specs=[blk, blk], out_specs=[blk],
                 out_shape=[jax.ShapeDtypeStruct((rows, cols), mine.dtype)],
                 args=[mine.reshape(rows, cols), theirs.reshape(rows, cols)])[0].reshape(shape)


def adamw(name, landed, w, m, v):
    rows, cols = w.shape
    slots = landed.shape[0]
    tile = next((t for t in (512, 256, 128, 64, 32, 16, 8) if rows % t == 0 and t * cols <= (1 << 17)), rows)

    def body(l_ref, w_ref, m_ref, v_ref, g_ref, d_ref, nm_ref, nv_ref):
        gv = l_ref[0].astype(F32)
        for s in range(1, slots):
            gv = gv + l_ref[s].astype(F32)
        nm = ADAM_B1 * m_ref[...] + (1.0 - ADAM_B1) * gv
        nv = ADAM_B2 * v_ref[...] + (1.0 - ADAM_B2) * jnp.square(gv)
        m_hat = nm / (1.0 - ADAM_B1 ** ADAM_STEP)
        v_hat = nv / (1.0 - ADAM_B2 ** ADAM_STEP)
        g_ref[...] = gv
        d_ref[...] = -ADAM_LR * (m_hat / (jnp.sqrt(v_hat) + ADAM_EPS) + ADAM_WD * w_ref[...])
        nm_ref[...] = nm
        nv_ref[...] = nv

    blk = pl.BlockSpec((tile, cols), lambda i: (i, 0))
    out = jax.ShapeDtypeStruct((rows, cols), F32)
    return _call(name, body, grid=(rows // tile,),
                 in_specs=[pl.BlockSpec((slots, tile, cols), lambda i: (0, i, 0)), blk, blk, blk],
                 out_specs=[blk] * 4, out_shape=[out] * 4, args=[landed, w, m, v])


def _swap_halves(pe):
    half = QK_ROPE // 2
    return jnp.concatenate([-pe[..., half:], pe[..., :half]], axis=-1)


def _unswap_halves(dsw):
    half = QK_ROPE // 2
    return jnp.concatenate([dsw[..., half:], -dsw[..., :half]], axis=-1)


def _w_in_ext(w):
    kpe = w[:, 9216:9280]
    z = jnp.zeros((w.shape[0], HEAD_W - QK_ROPE), w.dtype)
    pad = jnp.zeros((w.shape[0], U_PAD), w.dtype)
    return jnp.concatenate([w[:, :9216], w[:, 9280:], kpe, z, _swap_halves(kpe), z, pad], axis=1)


def _w_in_grad(d):
    dkpe = d[:, 13312:13376] + _unswap_halves(d[:, 13440:13504])
    return jnp.concatenate([d[:, :9216], dkpe, d[:, 9216:13312]], axis=1)


def _w_q_ext(w):
    w3 = w.reshape(Q_LORA, MLA_HEADS, MLA_QK)
    pe = w3[:, :, HEAD_W:]
    z = jnp.zeros((Q_LORA, MLA_HEADS, HEAD_W - QK_ROPE), w.dtype)
    wide = MLA_HEADS * HEAD_W
    return jnp.concatenate([w3[:, :, :HEAD_W].reshape(Q_LORA, wide),
                            jnp.concatenate([pe, z], axis=2).reshape(Q_LORA, wide),
                            jnp.concatenate([_swap_halves(pe), z], axis=2).reshape(Q_LORA, wide)], axis=1)


def _w_q_grad(d):
    wide = MLA_HEADS * HEAD_W
    d3 = [d[:, n * wide:(n + 1) * wide].reshape(Q_LORA, MLA_HEADS, HEAD_W) for n in range(3)]
    dpe = d3[1][:, :, :QK_ROPE] + _unswap_halves(d3[2][:, :, :QK_ROPE])
    return jnp.concatenate([d3[0], dpe], axis=2).reshape(Q_LORA, MLA_HEADS * MLA_QK)


def _hg_prep(f_raw, q_hg, logits):
    lb = jax.nn.softmax(logits, axis=0)[0:1, :]
    log_f = jnp.logaddexp(jnp.log(lb), jnp.log1p(-lb) + jax.nn.log_sigmoid(f_raw))
    k_in = (1.0 - lb) * jax.nn.sigmoid(-f_raw)
    return log_f, k_in, jax.nn.silu(q_hg)


def _rope(q_nope, q_pe, q_sw, k_nope, k_pe, k_sw, cos, sin):
    qf = jnp.concatenate([q_nope, q_pe * cos + q_sw * sin], axis=1)
    kf = jnp.concatenate([k_nope, k_pe * cos + k_sw * sin], axis=1)
    return qf, kf


def _step(a):
    x, mem, target = a['x'][0], a['mem'][0], a['loss_target'][0]
    s, d = x.shape
    nm = mem.shape[0]
    ff = N_DEV * a['ffn1_w_gate'].shape[-1]
    cs = ff // N_DEV
    tr = min(s, 128)
    th = min(s, 1024)
    ta = 512

    bf = {n: a[n][0].astype(BF16) for n in BIG}
    gat, full = {}, {}

    def gathered(names, carry):
        for n, g8 in zip(names or [], carry.results if carry else []):
            r, c = bf[n].shape
            gat[n] = g8
            if not n.startswith('ffn'):
                full[n] = g8.transpose(1, 0, 2).reshape(r, N_DEV * c) if BIG[n] else g8.reshape(N_DEV * r, c)

    xa_names = ['xa_w_q', 'xa_w_k', 'xa_w_v', 'xa_w_o']
    ffn2_names = ['ffn2_w_gate', 'ffn2_w_up', 'ffn2_w_down']
    first = Gather([bf['ffn1_w_gate']])
    run_alone("gather_first", first)
    gathered(['ffn1_w_gate'], first)

    inv_freq = 1.0 / (ROPE_THETA ** (jnp.arange(0, QK_ROPE, 2, dtype=F32) / QK_ROPE))
    ang = a['positions'][0].astype(F32)[:, None] * inv_freq
    zero = jnp.zeros((s, HEAD_W - QK_ROPE), F32)
    cos = jnp.concatenate([jnp.cos(ang), jnp.cos(ang), zero], axis=1)
    sin = jnp.concatenate([jnp.sin(ang), jnp.sin(ang), zero], axis=1)

    gs = {}
    gb = {}

    def by_rows(g):
        return g.reshape(N_DEV, g.shape[0] // N_DEV, g.shape[1])

    def by_cols(g):
        return g.reshape(g.shape[0], N_DEV, g.shape[1] // N_DEV).transpose(1, 0, 2)

    pre_fn = lambda xv, g: (_rms(xv, g),)
    pre_res_fn = lambda xv, g: (_rms(xv, g), xv)

    def pre_norm(tag, x_in, g):
        return rowmap(tag + "_pre", pre_fn, [R_(x_in), P_(g)], [(BF16, x_in.shape[1], 0)], tile=tr)[0]

    def pre_norm_bwd(tag, x_in, g, dh, d_out, carry=None):
        return rowmap(tag + "_pre_bwd", pre_res_fn, [R_(x_in), P_(g)], tile=tr, cts=[R_(dh), R_(d_out)],
                      wrt=[0, 1], gdt=[F32, F32], carry=carry)

    def post_fn(weight):
        return lambda xv, y, g: (xv + weight * _rms(y, g),)

    def post_norm(tag, x_in, y, g, weight):
        return rowmap(tag + "_post", post_fn(weight), [R_(x_in), R_(y), P_(g)], [(F32, d, 0)], tile=tr)[0]

    def post_norm_bwd(tag, x_in, y, g, weight, d_out):
        return rowmap(tag + "_post_bwd", post_fn(weight), [R_(x_in), R_(y), P_(g)], tile=tr, cts=[R_(d_out)],
                      wrt=[1, 2], gdt=[BF16, F32])

    act_fn = lambda av, bv: (jax.nn.silu(av) * bv,)

    def fetch(plan, key):
        names = plan.get(key)
        return (names, Gather([bf[n] for n in names])) if names else (None, None)

    def send(plan, key):
        acts = plan.get(key)
        if not acts:
            return None, None
        parts = [Halve([gb[n] for n in names]) if kind == 'halve'
                 else Spread([pair_sum("pair_" + n, *halves[n]) for n in names]) for kind, names in acts]
        return acts, Joint(parts)

    def ffn_fwd(tag, x_in, plan):
        h = pre_norm(tag, x_in, a[tag + '_pre_g'])
        names, carry = fetch(plan, 'gate')
        av = mm(tag + "_gate", h, gat[tag + '_w_gate'], 'nn', jdim='n', carry=carry)
        gathered(names, carry)
        names, carry = fetch(plan, 'up')
        bv = mm(tag + "_up", h, gat[tag + '_w_up'], 'nn', jdim='n', carry=carry)
        gathered(names, carry)
        a2, b2 = av.reshape(N_DEV * s, cs), bv.reshape(N_DEV * s, cs)
        z = rowmap(tag + "_act", act_fn, [R_(a2), R_(b2)], [(BF16, cs, 0)], tile=ta)[0].reshape(N_DEV, s, cs)
        names, carry = fetch(plan, 'dn')
        y = mm(tag + "_dn", z, gat[tag + '_w_down'], 'nn', jdim='k', carry=carry)
        gathered(names, carry)
        return post_norm(tag, x_in, y, a[tag + '_post_g'], 0.5), (x_in, h, a2, b2, z, y)

    def ffn_bwd(tag, res, d_out, plan):
        x_in, h, a2, b2, z, y = res
        dy, gs[tag + '_post_g'] = post_norm_bwd(tag, x_in, y, a[tag + '_post_g'], 0.5, d_out)
        acts, carry = send(plan, 'dn_dx')
        dz = mm(tag + "_dn_dx", dy, gat[tag + '_w_down'], 'nt', jdim='n', carry=carry)
        exchanged(acts, carry)
        gb[tag + '_w_down'] = mm(tag + "_dn_dw", z, dy, 'tn', jdim='m', out_dtype=BF16)
        acts, carry = send(plan, 'act_bwd')
        da, db = rowmap(tag + "_act_bwd", act_fn, [R_(a2), R_(b2)], tile=ta, cts=[R_(dz.reshape(N_DEV * s, cs))],
                        wrt=[0, 1], gdt=[BF16, BF16], carry=carry)
        exchanged(acts, carry)
        da, db = da.reshape(N_DEV, s, cs), db.reshape(N_DEV, s, cs)
        names, carry = send(plan, 'gu_dx')
        dh = mm(tag + "_gu_dx", [da, db], [gat[tag + '_w_gate'], gat[tag + '_w_up']], 'nt', jdim='k', carry=carry)
        exchanged(names, carry)
        gb[tag + '_w_gate'] = mm(tag + "_gate_dw", h, da, 'tn', jdim='n', out_dtype=BF16)
        names, carry = send(plan, 'up_dw')
        gb[tag + '_w_up'] = mm(tag + "_up_dw", h, db, 'tn', jdim='n', out_dtype=BF16, carry=carry)
        exchanged(names, carry)
        names, carry = send(plan, 'pre_bwd')
        d_in, gs[tag + '_pre_g'] = pre_norm_bwd(tag, x_in, a[tag + '_pre_g'], dh, d_out, carry)
        exchanged(names, carry)
        return d_in

    hg_out_fn = lambda o, og, g: (_rms(o, g) * jax.nn.silu(og),)
    mla_norm_fn = lambda cq, ckv, gq, gkv: (_rms(cq, gq), _rms(ckv, gkv))
    gate_fn = lambda ga, gb, ya, yb: (jax.nn.sigmoid(ga) * ya + jax.nn.sigmoid(gb) * yb,)
    mla = dict(heads=MLA_HEADS, dq=2 * HEAD_W, dv=HEAD_W, koff=0, kstride=1, voff=1, vstride=2,
               scale=MLA_QK ** -0.5, causal=True)

    def mix_fwd(x_in, plan):
        w_in = _w_in_ext(full['w_in'])
        h = pre_norm("mix", x_in, a['mix_pre_g'])
        names, carry = fetch(plan, 'in')
        u = mm("mix_in", h, w_in, 'nn', carry=carry)
        gathered(names, carry)
        w_q, w_kv = _w_q_ext(full['mla_w_q_up']), full['mla_w_kv_up']
        hg_ins = [R_(u, 2048, 1), R_(u, 2048, 0), P_(a['hgrn_lb_logits'])]
        log_f, k_in, q_in = rowmap("hg_prep", _hg_prep, hg_ins, [(F32, 2048, 0)] * 3, tile=tr)
        names, carry = fetch(plan, 'scan')
        o_a, states = hg_scan_fwd("hg_scan", q_in, k_in, log_f, u, 32, carry=carry)
        gathered(names, carry)
        out_ins = [R_(o_a, HEAD_W, 0, 1), R_(u, HEAD_W, 48, 1), P_(a['hg_norm_g'], HEAD_W, 0, 1)]
        oag = rowmap("hg_out", hg_out_fn, out_ins, [(BF16, HEAD_W, 1)], tile=th, ncol=HG_HEADS)[0]
        y_a = mm("mix_a", oag, full['w_branch_a'], 'nn')
        norm_ins = [R_(u, 512, 16), R_(u, 512, 17), P_(a['mla_q_norm_g']), P_(a['mla_kv_norm_g'])]
        cqn, ckvn = rowmap("mla_norm", mla_norm_fn, norm_ins, [(BF16, 512, 0)] * 2, tile=tr)
        q_all = mm("mla_qup", cqn, w_q, 'nn')
        kv = mm("mla_kvup", ckvn, w_kv, 'nn')
        rope_ins = [R_(q_all, HEAD_W, 0, 1), R_(q_all, HEAD_W, 16, 1), R_(q_all, HEAD_W, 32, 1),
                    R_(kv, HEAD_W, 0, 2), R_(u, HEAD_W, 104, 0), R_(u, HEAD_W, 105, 0), R_(cos), R_(sin)]
        qf, kf = rowmap("mla_rope", _rope, rope_ins, [(BF16, 2 * HEAD_W, 1)] * 2, tile=th, ncol=MLA_HEADS,
                        rows_inner=False)
        o_b = attn_fwd("mla_attn", qf, kf, kv, **mla)
        y_b = mm("mix_b", o_b, full['w_branch_b'], 'nn')
        gate_ins = [R_(u, 1024, 9, 1), R_(u, 1024, 11, 1), R_(y_a, 1024, 0, 1), R_(y_b, 1024, 0, 1)]
        y = rowmap("mix_gate", gate_fn, gate_ins, [(BF16, 1024, 1)], tile=tr, ncol=2)[0]
        yo = mm("mix_out", y, full['w_out'], 'nn')
        res = (x_in, h, hg_ins, q_in, k_in, log_f, u, states, out_ins, oag, norm_ins, cqn, ckvn, rope_ins, qf, kf,
               kv, o_b, gate_ins, y, yo, w_in, w_q, w_kv)
        return post_norm("mix", x_in, yo, a['mix_post_g'], 1.0), res

    def mix_bwd(res, d_out, plan):
        (x_in, h, hg_ins, q_in, k_in, log_f, u, states, out_ins, oag, norm_ins, cqn, ckvn, rope_ins, qf, kf, kv,
         o_b, gate_ins, y, yo, w_in, w_q, w_kv) = res
        dyo, gs['mix_post_g'] = post_norm_bwd("mix", x_in, yo, a['mix_post_g'], 1.0, d_out)
        dy = mm("mix_out_dx", dyo, full['w_out'], 'nt')
        gb['w_out'] = by_rows(mm("mix_out_dw", y, dyo, 'tn', out_dtype=BF16))
        dga, dgb, dya, dyb = rowmap("mix_gate_bwd", gate_fn, gate_ins, tile=tr, ncol=2, cts=[R_(dy, 1024, 0, 1)],
                                    wrt=[0, 1, 2, 3], gdt=[BF16] * 4)
        gb['w_branch_b'] = by_rows(mm("mix_b_dw", o_b, dyb, 'tn', out_dtype=BF16))
        do_b = mm("mix_b_dx", dyb, full['w_branch_b'], 'nt')
        names, carry = send(plan, 'attn_bwd')
        dqf, dkf, dv = attn_bwd("mla_attn_bwd", qf, kf, kv, do_b, carry=carry, **mla)
        exchanged(names, carry)
        dqn, dqp, dqs, dkn, dkpe, dksw = rowmap(
            "mla_rope_bwd", _rope, rope_ins, tile=th, ncol=MLA_HEADS, rows_inner=False,
            cts=[R_(dqf, 2 * HEAD_W, 0, 1), R_(dkf, 2 * HEAD_W, 0, 1)], wrt=[0, 1, 2, 3, 4, 5],
            gdt=[BF16, BF16, BF16, BF16, F32, F32])
        dq_all = jnp.concatenate([dqn, dqp, dqs], axis=1)
        dkv = jnp.concatenate([dkn.reshape(s, MLA_HEADS, HEAD_W), dv.astype(BF16).reshape(s, MLA_HEADS, HEAD_W)],
                              axis=2).reshape(s, 2 * MLA_HEADS * HEAD_W)
        gb['mla_w_q_up'] = by_cols(_w_q_grad(mm("mla_qup_dw", cqn, dq_all, 'tn', out_dtype=BF16)))
        dcqn = mm("mla_qup_dx", dq_all, w_q, 'nt')
        gb['mla_w_kv_up'] = by_cols(mm("mla_kvup_dw", ckvn, dkv, 'tn', out_dtype=BF16))
        dckvn = mm("mla_kvup_dx", dkv, w_kv, 'nt')
        dcq, dckv, gs['mla_q_norm_g'], gs['mla_kv_norm_g'] = rowmap(
            "mla_norm_bwd", mla_norm_fn, norm_ins, tile=tr, cts=[R_(dcqn), R_(dckvn)], wrt=[0, 1, 2, 3],
            gdt=[BF16, BF16, F32, F32])
        gb['w_branch_a'] = by_rows(mm("mix_a_dw", oag, dya, 'tn', out_dtype=BF16))
        doag = mm("mix_a_dx", dya, full['w_branch_a'], 'nt')
        acts, carry = send(plan, 'out_bwd')
        do_a, dog, gs['hg_norm_g'] = rowmap("hg_out_bwd", hg_out_fn, out_ins, tile=th, ncol=HG_HEADS,
                                            cts=[R_(doag, HEAD_W, 0, 1)], wrt=[0, 1, 2], gdt=[F32, BF16, F32],
                                            carry=carry)
        exchanged(acts, carry)
        names, carry = send(plan, 'scan_bwd')
        dq_in, dk_in, dlog_f, di = hg_scan_bwd("hg_scan_bwd", q_in, k_in, log_f, u, 32, states, do_a, carry=carry)
        exchanged(names, carry)
        df, dq_hg, gs['hgrn_lb_logits'] = rowmap("hg_prep_bwd", _hg_prep, hg_ins, tile=tr,
                                                 cts=[R_(dlog_f), R_(dk_in), R_(dq_in)], wrt=[0, 1, 2],
                                                 gdt=[BF16, BF16, F32])
        du = jnp.concatenate([dq_hg, df, di.astype(BF16), dog, dcq, dckv, dga, dgb, dkpe.astype(BF16),
                              dksw.astype(BF16), jnp.zeros((s, U_PAD), BF16)], axis=1)
        gb['w_in'] = by_cols(_w_in_grad(mm("mix_in_dw", h, du, 'tn', out_dtype=BF16)))
        names, carry = send(plan, 'in_dx')
        dh = mm("mix_in_dx", du, w_in, 'nt', carry=carry)
        exchanged(names, carry)
        d_in, gs['mix_pre_g'] = pre_norm_bwd("mix", x_in, a['mix_pre_g'], dh, d_out)
        return d_in

    xa = dict(heads=XA_HEADS, dq=HEAD_W, dv=HEAD_W, koff=0, kstride=1, voff=XA_HEADS, vstride=1,
              scale=HEAD_W ** -0.5, causal=False)
    tm_ = min(nm, 128)

    def xa_fwd(x_in):
        w_xkv = jnp.concatenate([full['xa_w_k'], full['xa_w_v']], axis=1)
        h = pre_norm("xa", x_in, a['xa_pre_g'])
        mn = rowmap("xa_mem", pre_fn, [R_(mem), P_(a['xa_mem_g'])], [(BF16, d, 0)], tile=tm_)[0]
        q = mm("xa_q", h, full['xa_w_q'], 'nn')
        kv = mm("xa_kv", mn, w_xkv, 'nn')
        o = attn_fwd("xa_attn", q, kv, kv, **xa)
        yo = mm("xa_o", o, full['xa_w_o'], 'nn')
        return post_norm("xa", x_in, yo, a['xa_post_g'], 1.0), (x_in, h, mn, q, kv, o, yo, w_xkv)

    def xa_bwd(res, d_out, plan):
        x_in, h, mn, q, kv, o, yo, w_xkv = res
        dyo, gs['xa_post_g'] = post_norm_bwd("xa", x_in, yo, a['xa_post_g'], 1.0, d_out)
        do = mm("xa_o_dx", dyo, full['xa_w_o'], 'nt')
        gb['xa_w_o'] = by_cols(mm("xa_o_dw", o, dyo, 'tn', out_dtype=BF16))
        dq, dk, dv = attn_bwd("xa_attn_bwd", q, kv, kv, do, **xa)
        dkv = jnp.concatenate([dk, dv], axis=1).astype(BF16)
        dw = mm("xa_kv_dw", mn, dkv, 'tn', out_dtype=BF16)
        gb['xa_w_k'], gb['xa_w_v'] = by_rows(dw[:, :XA_HEADS * HEAD_W]), by_rows(dw[:, XA_HEADS * HEAD_W:])
        dmn = mm("xa_kv_dx", dkv, w_xkv, 'nt')
        gs['xa_mem_g'] = rowmap("xa_mem_bwd", pre_fn, [R_(mem), P_(a['xa_mem_g'])], tile=tm_, cts=[R_(dmn)],
                                wrt=[1], gdt=[F32])[0]
        gb['xa_w_q'] = by_rows(mm("xa_q_dw", h, dq, 'tn', out_dtype=BF16))
        dh = mm("xa_q_dx", dq, full['xa_w_q'], 'nt')
        acts, carry = send(plan, 'pre_bwd')
        d_in, gs['xa_pre_g'] = pre_norm_bwd("xa", x_in, a['xa_pre_g'], dh, d_out, carry)
        exchanged(acts, carry)
        return d_in

    landed = {}

    halves = {}

    def exchanged(acts, carry):
        for (kind, names), part in zip(acts or [], carry.parts if carry else []):
            (halves if kind == 'halve' else landed).update(zip(names, part.results))

    x1, r1 = ffn_fwd('ffn1', x, {'gate': ['ffn1_w_up'], 'up': ['ffn1_w_down'], 'dn': ['w_in']})
    x2, r2 = mix_fwd(x1, {'in': ['mla_w_q_up', 'mla_w_kv_up', 'w_branch_a', 'w_branch_b', 'w_out'],
                          'scan': xa_names + ffn2_names})
    x3, r3 = xa_fwd(x2)
    x4, r4 = ffn_fwd('ffn2', x3, {})

    def loss_fn(y, t):
        diff = y - t
        return diff * (1.0 / d), jnp.mean(diff * diff, axis=-1, keepdims=True)

    d4, row_loss = rowmap("loss", loss_fn, [R_(x4), R_(target)], [(F32, d, 0), (F32, 1, 0)], tile=tr)
    loss = lax.psum(0.5 * jnp.sum(row_loss), ("x", "y", "c"))
    late = ['mla_w_q_up', 'mla_w_kv_up', 'w_branch_a']
    d3 = ffn_bwd('ffn2', r4, d4, {'act_bwd': [('halve', ['ffn2_w_down'])], 'gu_dx': [('spread', ['ffn2_w_down'])],
                                  'up_dw': [('halve', ['ffn2_w_gate'])], 'pre_bwd': [('halve', ['ffn2_w_up'])]})
    d2 = xa_bwd(r3, d3, {'pre_bwd': [('halve', xa_names)]})
    d1 = mix_bwd(r2, d2, {'attn_bwd': [('spread', ['ffn2_w_gate', 'ffn2_w_up']), ('halve', ['w_out', 'w_branch_b'])],
                          'out_bwd': [('halve', late)],
                          'scan_bwd': [('spread', xa_names + ['w_out', 'w_branch_b'] + late)],
                          'in_dx': [('halve', ['w_in'])]})
    grad_x = ffn_bwd('ffn1', r1, d1, {'dn_dx': [('spread', ['w_in'])], 'act_bwd': [('halve', ['ffn1_w_down'])],
                                      'gu_dx': [('spread', ['ffn1_w_down'])], 'up_dw': [('halve', ['ffn1_w_gate'])],
                                      'pre_bwd': [('spread', ['ffn1_w_gate']), ('halve', ['ffn1_w_up'])]})
    acts, carry = send({'end': [('spread', ['ffn1_w_up'])]}, 'end')
    run_alone("spread_last", carry)
    exchanged(acts, carry)

    def pack_small(vals):
        flat = jnp.concatenate([vals[n].reshape(-1) for n in SMALL])
        rows = -(-flat.shape[0] // PACK_W)
        rows = -(-rows // 8) * 8
        return jnp.pad(flat, (0, rows * PACK_W - flat.shape[0])).reshape(rows, PACK_W)

    def unpack_small(buf):
        flat, out, at = buf.reshape(-1), {}, 0
        for n in SMALL:
            size = a[n].shape[0] * a[n].shape[1]
            out[n] = flat[at:at + size].reshape(a[n].shape)
            at += size
        return out

    small = Gather([pack_small(gs)])
    run_alone("gather_g_small", small)

    grads, delta, new_m, new_v = {}, {}, {}, {}
    packs = adamw("adamw_small", small.results[0], pack_small(a), pack_small({n: a['m_' + n] for n in SMALL}),
                  pack_small({n: a['v_' + n] for n in SMALL}))
    for dst, buf in zip((grads, delta, new_m, new_v), packs):
        dst.update(unpack_small(buf))
    for n in BIG:
        outs = adamw("adamw_" + n, landed[n], a[n][0], a['m_' + n][0], a['v_' + n][0])
        grads[n], delta[n], new_m[n], new_v[n] = (t.reshape(a[n].shape) for t in outs)

    return (loss, grad_x[None], *[grads[n] for n in WEIGHTS], *[delta[n] for n in WEIGHTS],
            *[new_m[n] for n in WEIGHTS], *[new_v[n] for n in WEIGHTS])


def kernel(x, mem, positions, hgrn_lb_logits, ffn1_pre_g, ffn1_w_gate, ffn1_w_up, ffn1_w_down, ffn1_post_g, mix_pre_g, w_in, hg_norm_g, mla_q_norm_g, mla_w_q_up, mla_kv_norm_g, mla_w_kv_up, w_branch_a, w_branch_b, w_out, mix_post_g, xa_pre_g, xa_mem_g, xa_w_q, xa_w_k, xa_w_v, xa_w_o, xa_post_g, ffn2_pre_g, ffn2_w_gate, ffn2_w_up, ffn2_w_down, ffn2_post_g, loss_target, m_hgrn_lb_logits, m_ffn1_pre_g, m_ffn1_w_gate, m_ffn1_w_up, m_ffn1_w_down, m_ffn1_post_g, m_mix_pre_g, m_w_in, m_hg_norm_g, m_mla_q_norm_g, m_mla_w_q_up, m_mla_kv_norm_g, m_mla_w_kv_up, m_w_branch_a, m_w_branch_b, m_w_out, m_mix_post_g, m_xa_pre_g, m_xa_mem_g, m_xa_w_q, m_xa_w_k, m_xa_w_v, m_xa_w_o, m_xa_post_g, m_ffn2_pre_g, m_ffn2_w_gate, m_ffn2_w_up, m_ffn2_w_down, m_ffn2_post_g, v_hgrn_lb_logits, v_ffn1_pre_g, v_ffn1_w_gate, v_ffn1_w_up, v_ffn1_w_down, v_ffn1_post_g, v_mix_pre_g, v_w_in, v_hg_norm_g, v_mla_q_norm_g, v_mla_w_q_up, v_mla_kv_norm_g, v_mla_w_kv_up, v_w_branch_a, v_w_branch_b, v_w_out, v_mix_post_g, v_xa_pre_g, v_xa_mem_g, v_xa_w_q, v_xa_w_k, v_xa_w_v, v_xa_w_o, v_xa_post_g, v_ffn2_pre_g, v_ffn2_w_gate, v_ffn2_w_up, v_ffn2_w_down, v_ffn2_post_g):
    return _step(dict(locals()))
```

```python
import functools

import jax
import jax.numpy as jnp
from jax import lax
from jax.experimental import pallas as pl
from jax.experimental.pallas import tpu as pltpu

F32 = jnp.float32
BF16 = jnp.bfloat16

N_DEV = 8
D_MODEL = 2048
CHUNK = 64
CHUNK_SHIFT = 6
SUB = 16
HG_HEADS = 16
HG_GROUP = 4
HEAD_W = 128
MLA_HEADS = 16
Q_LORA = 512
KV_LORA = 512
QK_ROPE = 64
MLA_QK = 192
XA_HEADS = 4
ROPE_THETA = 10000.0
EPS = 1e-6
PACK_W = 1024
VMEM_LIMIT = 56 * 1024 * 1024

ADAM_LR = 0.001
ADAM_B1 = 0.9
ADAM_B2 = 0.999
ADAM_EPS = 1e-08
ADAM_WD = 0.01
ADAM_STEP = 10

U_PAD = 256

WEIGHTS = ['hgrn_lb_logits', 'ffn1_pre_g', 'ffn1_w_gate', 'ffn1_w_up', 'ffn1_w_down', 'ffn1_post_g', 'mix_pre_g',
           'w_in', 'hg_norm_g', 'mla_q_norm_g', 'mla_w_q_up', 'mla_kv_norm_g', 'mla_w_kv_up', 'w_branch_a',
           'w_branch_b', 'w_out', 'mix_post_g', 'xa_pre_g', 'xa_mem_g', 'xa_w_q', 'xa_w_k', 'xa_w_v', 'xa_w_o',
           'xa_post_g', 'ffn2_pre_g', 'ffn2_w_gate', 'ffn2_w_up', 'ffn2_w_down', 'ffn2_post_g']
BIG = {'ffn1_w_gate': True, 'ffn1_w_up': True, 'ffn1_w_down': False, 'w_in': True, 'mla_w_q_up': True,
       'mla_w_kv_up': True, 'w_branch_a': False, 'w_branch_b': False, 'w_out': False, 'xa_w_q': False,
       'xa_w_k': False, 'xa_w_v': False, 'xa_w_o': True, 'ffn2_w_gate': True, 'ffn2_w_up': True,
       'ffn2_w_down': False}
SMALL = [n for n in WEIGHTS if n not in BIG]


def _cparams(**kw):
    return pltpu.CompilerParams(vmem_limit_bytes=VMEM_LIMIT, **kw)


def _pick(dim, cands):
    for c in cands:
        if dim % c == 0:
            return c
    return dim


def _place():
    return lax.axis_index("x"), lax.axis_index("y"), lax.axis_index("c")


def _slot(px, py, pc):
    return 4 * px + 2 * py + pc


class Gather:
    def __init__(self, tensors):
        self.operands = list(tensors)
        self.out_shape = [jax.ShapeDtypeStruct((N_DEV,) + t.shape, t.dtype) for t in tensors]
        n = len(tensors)
        self.scratch = [pltpu.SemaphoreType.DMA((n, N_DEV - 1)), pltpu.SemaphoreType.DMA((n, N_DEV - 1))]

    def _copies(self, t, x_ref, out_ref, send, recv):
        x, y, c = _place()
        chips = [(1 - x, y), (x, 1 - y), (1 - x, 1 - y)]

        def copy(k, block, to, src=None):
            rows = out_ref.at[_slot(*block)]
            return pltpu.make_async_remote_copy(src_ref=rows if src is None else src, dst_ref=rows,
                                                send_sem=send.at[t, k], recv_sem=recv.at[t, k], device_id=to,
                                                device_id_type=pl.DeviceIdType.MESH)

        return (x, y, c), chips, copy

    def start(self, ins, outs, scr):
        send, recv = scr
        for t, (x_ref, out_ref) in enumerate(zip(ins, outs)):
            (x, y, c), chips, copy = self._copies(t, x_ref, out_ref, send, recv)
            copy(0, (x, y, c), (x, y, 1 - c), src=x_ref).start()
            for j, chip in enumerate(chips):
                copy(1 + j, (x, y, c), (*chip, c), src=x_ref).start()

    def finish(self, ins, outs, scr):
        send, recv = scr
        for t, (x_ref, out_ref) in enumerate(zip(ins, outs)):
            (x, y, c), chips, copy = self._copies(t, x_ref, out_ref, send, recv)
            passed = [copy(4 + j, (*chip, c), (x, y, 1 - c)) for j, chip in enumerate(chips)]
            for j, chip in enumerate(chips):
                copy(1 + j, (*chip, c), (x, y, c)).wait_recv()
                passed[j].start()
            copy(0, (x, y, 1 - c), (x, y, c)).wait_recv()
            for j, chip in enumerate(chips):
                copy(4 + j, (*chip, 1 - c), (x, y, c)).wait_recv()
            copy(0, (x, y, c), (x, y, 1 - c), src=x_ref).wait_send()
            for j, chip in enumerate(chips):
                copy(1 + j, (x, y, c), (*chip, c), src=x_ref).wait_send()
                passed[j].wait_send()

    def set_results(self, res):
        self.results = list(res)


N_CHIP = N_DEV // 2


class Halve:
    def __init__(self, tensors):
        self.operands = list(tensors)
        self.out_shape = [jax.ShapeDtypeStruct((N_CHIP,) + t.shape[1:], t.dtype) for t in tensors]
        n = len(tensors)
        self.scratch = [pltpu.SemaphoreType.DMA((n, N_CHIP)), pltpu.SemaphoreType.DMA((n, N_CHIP))]

    def _copies(self, t, x_ref, theirs_ref, scr):
        send, recv = scr
        x, y, c = _place()
        return [pltpu.make_async_remote_copy(
            src_ref=x_ref.at[2 * q + 1 - c], dst_ref=theirs_ref.at[q], send_sem=send.at[t, q],
            recv_sem=recv.at[t, q], device_id=(x, y, 1 - c), device_id_type=pl.DeviceIdType.MESH)
            for q in range(N_CHIP)]

    def start(self, ins, outs, scr):
        for t, (x_ref, theirs_ref) in enumerate(zip(ins, outs)):
            for give in self._copies(t, x_ref, theirs_ref, scr):
                give.start()

    def finish(self, ins, outs, scr):
        for t, (x_ref, theirs_ref) in enumerate(zip(ins, outs)):
            for give in self._copies(t, x_ref, theirs_ref, scr):
                give.wait_recv()
                give.wait_send()

    def set_results(self, res):
        self.results = list(res)


class Spread:
    def __init__(self, tensors):
        self.operands = list(tensors)
        self.out_shape = [jax.ShapeDtypeStruct(t.shape, t.dtype) for t in tensors]
        n = len(tensors)
        self.scratch = [pltpu.SemaphoreType.DMA((n, N_CHIP - 1)), pltpu.SemaphoreType.DMA((n, N_CHIP - 1))]

    def _copies(self, t, y_ref, out_ref, scr, outgoing):
        send, recv = scr
        x, y, c = _place()
        copies = []
        for k in range(1, N_CHIP):
            px, py = x ^ (k >> 1), y ^ (k & 1)
            copies.append(pltpu.make_async_remote_copy(
                src_ref=y_ref.at[2 * px + py], dst_ref=out_ref.at[2 * x + y if outgoing else 2 * px + py],
                send_sem=send.at[t, k - 1], recv_sem=recv.at[t, k - 1], device_id=(px, py, c),
                device_id_type=pl.DeviceIdType.MESH))
        return copies

    def start(self, ins, outs, scr):
        for t, (y_ref, out_ref) in enumerate(zip(ins, outs)):
            for give in self._copies(t, y_ref, out_ref, scr, True):
                give.start()

    def finish(self, ins, outs, scr):
        for t, (y_ref, out_ref) in enumerate(zip(ins, outs)):
            for take in self._copies(t, y_ref, out_ref, scr, False):
                take.wait_recv()
            for give in self._copies(t, y_ref, out_ref, scr, True):
                give.wait_send()

    def set_results(self, res):
        self.results = list(res)


class Joint:
    def __init__(self, parts):
        self.parts = list(parts)
        self.operands = [o for p in parts for o in p.operands]
        self.out_shape = [o for p in parts for o in p.out_shape]
        self.scratch = [o for p in parts for o in p.scratch]

    def _split(self, ins, outs, scr):
        i = o = s = 0
        for p in self.parts:
            ni, no, ns = len(p.operands), len(p.out_shape), len(p.scratch)
            yield p, ins[i:i + ni], outs[o:o + no], scr[s:s + ns]
            i, o, s = i + ni, o + no, s + ns

    def start(self, ins, outs, scr):
        for p, a, b, c in self._split(ins, outs, scr):
            p.start(a, b, c)

    def finish(self, ins, outs, scr):
        for p, a, b, c in self._split(ins, outs, scr):
            p.finish(a, b, c)

    def set_results(self, res):
        for p, _, part, _ in self._split([], list(res), []):
            p.set_results(part)


_HBM = pl.BlockSpec(memory_space=pltpu.HBM)


def run_alone(name, carry):
    n_in, n_out = len(carry.operands), len(carry.out_shape)

    def body(*refs):
        ins, outs, scr = refs[:n_in], refs[n_in:n_in + n_out], refs[n_in + n_out:]
        carry.start(ins, outs, scr)
        carry.finish(ins, outs, scr)

    res = pl.pallas_call(body, name=name, in_specs=[_HBM] * n_in, out_specs=[_HBM] * n_out,
                         out_shape=carry.out_shape, scratch_shapes=carry.scratch)(*carry.operands)
    carry.set_results(list(res))


def _call(name, body, *, grid, in_specs, out_specs, out_shape, args, scratch=(), carry=None):
    in_specs, out_specs, out_shape, scratch = list(in_specs), list(out_specs), list(out_shape), list(scratch)
    if carry is None:
        return list(pl.pallas_call(body, name=name, grid=grid, in_specs=in_specs, out_specs=out_specs,
                                   out_shape=out_shape, scratch_shapes=scratch, compiler_params=_cparams())(*args))
    n_in, n_out, n_scr = len(in_specs), len(out_shape), len(scratch)
    c_in, c_out = len(carry.operands), len(carry.out_shape)

    def wrapped(*refs):
        ins, c_ins = refs[:n_in], refs[n_in:n_in + c_in]
        at = n_in + c_in
        outs, c_outs = refs[at:at + n_out], refs[at + n_out:at + n_out + c_out]
        at += n_out + c_out
        scr, c_scr = refs[at:at + n_scr], refs[at + n_scr:]
        first = functools.reduce(jnp.logical_and, [pl.program_id(d) == 0 for d in range(len(grid))])
        last = functools.reduce(jnp.logical_and, [pl.program_id(d) == grid[d] - 1 for d in range(len(grid))])

        @pl.when(first)
        def _():
            carry.start(c_ins, c_outs, c_scr)

        body(*ins, *outs, *scr)

        @pl.when(last)
        def _():
            carry.finish(c_ins, c_outs, c_scr)

    res = pl.pallas_call(
        wrapped, name=name, grid=grid, in_specs=in_specs + [_HBM] * c_in, out_specs=out_specs + [_HBM] * c_out,
        out_shape=out_shape + carry.out_shape, scratch_shapes=scratch + carry.scratch, compiler_params=_cparams(),
    )(*args, *carry.operands)
    carry.set_results(list(res[n_out:]))
    return list(res[:n_out])


def R_(arr, w=None, off=0, stride=0):
    return ('r', arr, arr.shape[1] if w is None else w, off, stride)


def P_(arr, w=None, off=0, stride=0):
    return ('p', arr, arr.shape[1] if w is None else w, off, stride)


def rowmap(name, fn, ins, outs=None, *, tile, ncol=1, rows_inner=True, cts=None, wrt=None, gdt=None, cat=False,
           carry=None):
    rows = next(a.shape[0] for k, a, *_ in ins if k == 'r')
    nrow = rows // tile
    assert nrow * tile == rows
    grid = (ncol, nrow) if rows_inner else (nrow, ncol)

    def ij(g0, g1):
        return (g1, g0) if rows_inner else (g0, g1)

    def spec(kind, arr, w, off, stride):
        if kind == 'r':
            return pl.BlockSpec((tile, w), lambda g0, g1: (ij(g0, g1)[0], off + stride * ij(g0, g1)[1]))
        return pl.BlockSpec((arr.shape[0], w), lambda g0, g1: (0, off + stride * ij(g0, g1)[1]))

    ops = list(ins) + list(cts or [])
    in_specs = [spec(*o) for o in ops]
    n_in = len(ins)
    fwd = cts is None
    out_shape, out_specs, acc = [], [], []
    if fwd:
        for dt, w, stride in outs:
            out_shape.append(jax.ShapeDtypeStruct((rows, w * (ncol if stride else 1)), dt))
            out_specs.append(spec('r', None, w, 0, stride))
            acc.append(None)
    elif cat:
        widths = [ins[i][2] for i in wrt]
        assert ncol == 1 and all(ins[i][0] == 'r' for i in wrt)
        out_shape.append(jax.ShapeDtypeStruct((rows, sum(widths)), gdt))
        out_specs.append(spec('r', None, sum(widths), 0, 0))
    else:
        for n, i in enumerate(wrt):
            kind, arr, w, off, stride = ins[i]
            width = w * (ncol if stride else 1)
            if kind == 'r':
                out_shape.append(jax.ShapeDtypeStruct((rows, width), gdt[n]))
                out_specs.append(spec('r', None, w, 0, 1 if stride else 0))
                shared = stride == 0 and ncol > 1
                assert not shared or (not rows_inner and gdt[n] == F32)
                acc.append('col' if shared else None)
            else:
                out_shape.append(jax.ShapeDtypeStruct((arr.shape[0], width), F32))
                out_specs.append(spec('p', arr, w, 0, 1 if stride else 0))
                assert rows_inner or ncol == 1
                acc.append('row')

    def body(*refs):
        i, j = ij(pl.program_id(0), pl.program_id(1))
        vals = [r[...].astype(F32) for r in refs[:n_in]]
        out_refs = refs[len(ops):]
        if fwd:
            for o_ref, o in zip(out_refs, fn(*vals)):
                o_ref[...] = o.astype(o_ref.dtype)
            return

        def f(*d):
            full = list(vals)
            for n, idx in enumerate(wrt):
                full[idx] = d[n]
            return fn(*full)

        _, vjp = jax.vjp(f, *[vals[idx] for idx in wrt])
        grads = vjp(tuple(r[...].astype(F32) for r in refs[n_in:len(ops)]))
        if cat:
            o_ref, at = out_refs[0], 0
            for g in grads:
                o_ref[:, at:at + g.shape[1]] = g.astype(o_ref.dtype)
                at += g.shape[1]
            return
        for o_ref, g, a in zip(out_refs, grads, acc):
            if a is None:
                o_ref[...] = g.astype(o_ref.dtype)
            else:
                first = (i if a == 'row' else j) == 0

                @pl.when(first)
                def _(o_ref=o_ref):
                    o_ref[...] = jnp.zeros_like(o_ref)

                o_ref[...] += g

    return _call(name, body, grid=grid, in_specs=in_specs, out_specs=out_specs, out_shape=out_shape,
                 args=[o[1] for o in ops], carry=carry)


def _rms(x, g):
    return x * lax.rsqrt(jnp.mean(x * x, axis=-1, keepdims=True) + EPS) * g


_DIMS = {'nn': (((1,), (0,)), ((), ())), 'nt': (((1,), (1,)), ((), ())), 'tn': (((0,), (0,)), ((), ()))}


def mm(name, a, b, mode, *, jdim=None, out_dtype=F32, carry=None):
    a_list = list(a) if isinstance(a, (list, tuple)) else [a]
    b_list = list(b) if isinstance(b, (list, tuple)) else [b]
    a_order = ('k', 'm') if mode == 'tn' else ('m', 'k')
    b_order = ('n', 'k') if mode == 'nt' else ('k', 'n')
    size, blocks = {}, 1
    for arr, order in ((a_list[0], a_order), (b_list[0], b_order)):
        shape = arr.shape
        if jdim in order:
            blocks, shape = shape[0], shape[1:]
        for dname, extent in zip(order, shape):
            assert size.setdefault(dname, extent) == extent
    tile = {'m': _pick(size['m'], (1024, 512, 256, 128)), 'n': _pick(size['n'], (512, 256, 128)),
            'k': size['k'] if size['k'] <= 2048 else _pick(size['k'], (2304, 2048, 1024, 512, 256, 128))}
    if jdim is not None:
        tile[jdim] = size[jdim]
    if a_list[0].dtype == F32 and tile['k'] * tile['m'] > (1 << 20):
        tile['m'] = _pick(size['m'], (512, 256, 128))
    grid = tuple(blocks if d == jdim else size[d] // tile[d] for d in ('m', 'n', 'k'))
    nk = grid[2]

    def spec(order):
        shape = tuple(tile[d] for d in order)

        def imap(i, j, k):
            g = {'m': i, 'n': j, 'k': k}
            idx = tuple(0 if d == jdim else g[d] for d in order)
            return ((g[jdim],) + idx) if jdim in order else idx

        return pl.BlockSpec(((None,) + shape) if jdim in order else shape, imap)

    dims = _DIMS[mode]
    nt = len(a_list)

    def product(refs):
        acc = None
        for a_ref, b_ref in zip(refs[:nt], refs[nt:2 * nt]):
            p = lax.dot_general(a_ref[...].astype(BF16), b_ref[...].astype(BF16), dims, preferred_element_type=F32)
            acc = p if acc is None else acc + p
        return acc

    def body_once(*refs):
        refs[2 * nt][...] = product(refs).astype(refs[2 * nt].dtype)

    def body_acc(*refs):
        o_ref, acc_ref = refs[2 * nt], refs[2 * nt + 1]
        k = pl.program_id(2)

        @pl.when(k == 0)
        def _():
            acc_ref[...] = jnp.zeros_like(acc_ref)

        acc_ref[...] += product(refs)

        @pl.when(k == nk - 1)
        def _():
            o_ref[...] = acc_ref[...].astype(o_ref.dtype)

    out_dims = (size['m'], size['n'])
    out_shape = jax.ShapeDtypeStruct(((blocks,) + out_dims) if jdim in ('m', 'n') else out_dims, out_dtype)
    return _call(name, body_once if nk == 1 else body_acc, grid=grid,
                 in_specs=[spec(a_order)] * nt + [spec(b_order)] * nt, out_specs=[spec(('m', 'n'))],
                 out_shape=[out_shape], args=a_list + b_list,
                 scratch=[] if nk == 1 else [pltpu.VMEM((tile['m'], tile['n']), F32)], carry=carry)[0]


def _probs(q, k, i, tq, scale, causal):
    s = lax.dot_general(q, k, _DIMS['nt'], preferred_element_type=F32) * scale
    if causal:
        shape = s.shape
        q_chunk = jnp.right_shift(i * tq + lax.broadcasted_iota(jnp.int32, shape, 0), CHUNK_SHIFT)
        k_chunk = jnp.right_shift(lax.broadcasted_iota(jnp.int32, shape, 1), CHUNK_SHIFT)
        s = jnp.where(k_chunk <= q_chunk, s, -jnp.inf)
    e = jnp.exp(s - jnp.max(s, axis=-1, keepdims=True))
    return e, jnp.sum(e, axis=-1, keepdims=True)


def _attn_specs(tq, sk, dq, dv, koff, kstride, voff, vstride):
    return [pl.BlockSpec((tq, dq), lambda h, i: (i, h)),
            pl.BlockSpec((sk, dq), lambda h, i: (0, koff + kstride * h)),
            pl.BlockSpec((sk, dv), lambda h, i: (0, voff + vstride * h))]


def attn_fwd(name, q, k, v, *, heads, dq, dv, koff, kstride, voff, vstride, scale, causal, carry=None):
    sq, sk = q.shape[0], k.shape[0]
    tq = min(sq, 256)

    def body(q_ref, k_ref, v_ref, o_ref):
        e, l = _probs(q_ref[...].astype(BF16), k_ref[...].astype(BF16), pl.program_id(1), tq, scale, causal)
        o = jnp.dot(e.astype(BF16), v_ref[...].astype(BF16), preferred_element_type=F32)
        o_ref[...] = o / l

    return _call(name, body, grid=(heads, sq // tq),
                 in_specs=_attn_specs(tq, sk, dq, dv, koff, kstride, voff, vstride),
                 out_specs=[pl.BlockSpec((tq, dv), lambda h, i: (i, h))],
                 out_shape=[jax.ShapeDtypeStruct((sq, heads * dv), F32)], args=[q, k, v], carry=carry)[0]


def attn_bwd(name, q, k, v, do, *, heads, dq, dv, koff, kstride, voff, vstride, scale, causal, carry=None):
    sq, sk = q.shape[0], k.shape[0]
    tq = min(sq, 256)

    def body(q_ref, k_ref, v_ref, do_ref, dq_ref, dk_ref, dv_ref):
        i = pl.program_id(1)
        qb, kb, vb = q_ref[...].astype(BF16), k_ref[...].astype(BF16), v_ref[...].astype(BF16)
        dob = do_ref[...].astype(BF16)
        e, l = _probs(qb, kb, i, tq, scale, causal)
        p = e / l
        dp = lax.dot_general(dob, vb, _DIMS['nt'], preferred_element_type=F32)
        ds = (p * (dp - jnp.sum(dp * p, axis=-1, keepdims=True)) * scale).astype(BF16)

        @pl.when(i == 0)
        def _():
            dk_ref[...] = jnp.zeros_like(dk_ref)
            dv_ref[...] = jnp.zeros_like(dv_ref)

        dv_ref[...] += lax.dot_general(p.astype(BF16), dob, _DIMS['tn'], preferred_element_type=F32)
        dk_ref[...] += lax.dot_general(ds, qb, _DIMS['tn'], preferred_element_type=F32)
        dq_ref[...] = jnp.dot(ds, kb, preferred_element_type=F32)

    return _call(
        name, body, grid=(heads, sq // tq),
        in_specs=_attn_specs(tq, sk, dq, dv, koff, kstride, voff, vstride) + [pl.BlockSpec((tq, dv), lambda h, i: (i, h))],
        out_specs=[pl.BlockSpec((tq, dq), lambda h, i: (i, h)), pl.BlockSpec((sk, dq), lambda h, i: (0, h)),
                   pl.BlockSpec((sk, dv), lambda h, i: (0, h))],
        out_shape=[jax.ShapeDtypeStruct((sq, heads * dq), F32), jax.ShapeDtypeStruct((sk, heads * dq), F32),
                   jax.ShapeDtypeStruct((sk, heads * dv), F32)],
        args=[q, k, v, do], carry=carry)


def _hg_chunk(q, k, g, v, state):
    c = q.shape[0]
    row = lax.broadcasted_iota(jnp.int32, (c, c), 0)
    col = lax.broadcasted_iota(jnp.int32, (c, c), 1)
    tril = (col <= row).astype(F32)
    b = jnp.dot(tril, g, precision=lax.Precision.HIGHEST, preferred_element_type=F32)
    rows = lax.broadcasted_iota(jnp.int32, (c, 1), 0)
    o = jnp.dot((q * jnp.exp(b)).astype(BF16), state.astype(BF16), preferred_element_type=F32)
    t3 = lax.broadcasted_iota(jnp.int32, (SUB, SUB, 1), 0)
    s3 = lax.broadcasted_iota(jnp.int32, (SUB, SUB, 1), 1)
    parts = []
    for n in range(c // SUB):
        lo = n * SUB
        qn, kn, bn, vn = q[lo:lo + SUB], k[lo:lo + SUB], b[lo:lo + SUB], v[lo:lo + SUB]
        decay = jnp.exp(jnp.where(s3 <= t3, bn[:, None, :] - bn[None, :, :], -jnp.inf))
        sc = jnp.sum(qn[:, None, :] * kn[None, :, :] * decay, axis=-1)
        on = jnp.dot(sc.astype(BF16), vn.astype(BF16), preferred_element_type=F32)
        if n > 0:
            ref = jnp.sum(jnp.where(rows == lo - 1, b, 0.0), axis=0, keepdims=True)
            qd = (qn * jnp.exp(bn - ref)).astype(BF16)
            kd = (k[:lo] * jnp.exp(ref - b[:lo])).astype(BF16)
            so = lax.dot_general(qd, kd, _DIMS['nt'], preferred_element_type=F32)
            on = on + jnp.dot(so.astype(BF16), v[:lo].astype(BF16), preferred_element_type=F32)
        parts.append(on)
    o = o + jnp.concatenate(parts, axis=0)
    b_last = jnp.sum(g, axis=0, keepdims=True)
    ones = jnp.ones((c, 1), F32)
    b_last_col = lax.dot_general(g, ones, _DIMS['tn'], precision=lax.Precision.HIGHEST, preferred_element_type=F32)
    kd = (k * jnp.exp(b_last - b)).astype(BF16)
    new_state = jnp.exp(b_last_col) * state + lax.dot_general(kd, v.astype(BF16), _DIMS['tn'],
                                                              preferred_element_type=F32)
    return o, new_state


def hg_scan_fwd(name, q, k, g, u, v_off, carry=None):
    s = q.shape[0]
    n = s // CHUNK

    def body(q_ref, k_ref, g_ref, v_ref, o_ref, st_ref, state):
        @pl.when(pl.program_id(1) == 0)
        def _():
            state[...] = jnp.zeros_like(state)

        for j in range(HG_GROUP):
            cols = slice(j * HEAD_W, (j + 1) * HEAD_W)
            st = state[j]
            st_ref[j] = st
            o, new = _hg_chunk(q_ref[:, cols], k_ref[:, cols], g_ref[:, cols], v_ref[:, cols], st)
            o_ref[:, cols] = o
            state[j] = new

    wide = HG_GROUP * HEAD_W
    blk = pl.BlockSpec((CHUNK, wide), lambda h, c: (c, h))
    return _call(
        name, body, grid=(HG_HEADS // HG_GROUP, n),
        in_specs=[blk, blk, blk, pl.BlockSpec((CHUNK, wide), lambda h, c: (c, v_off // HG_GROUP + h))],
        out_specs=[blk, pl.BlockSpec((HG_GROUP, None, HEAD_W, HEAD_W), lambda h, c: (h, c, 0, 0))],
        out_shape=[jax.ShapeDtypeStruct((s, HG_HEADS * HEAD_W), F32),
                   jax.ShapeDtypeStruct((HG_HEADS, n, HEAD_W, HEAD_W), F32)],
        scratch=[pltpu.VMEM((HG_GROUP, HEAD_W, HEAD_W), F32)], args=[q, k, g, u], carry=carry)


def hg_scan_bwd(name, q, k, g, u, v_off, states, do, carry=None):
    s = q.shape[0]
    n = s // CHUNK

    def body(q_ref, k_ref, g_ref, v_ref, st_ref, do_ref, dq_ref, dk_ref, dg_ref, dv_ref, dstate):
        @pl.when(pl.program_id(1) == 0)
        def _():
            dstate[...] = jnp.zeros_like(dstate)

        for j in range(HG_GROUP):
            cols = slice(j * HEAD_W, (j + 1) * HEAD_W)
            _, vjp = jax.vjp(_hg_chunk, q_ref[:, cols], k_ref[:, cols], g_ref[:, cols], v_ref[:, cols], st_ref[j])
            dq, dk, dg, dv, dst = vjp((do_ref[:, cols], dstate[j]))
            dq_ref[:, cols] = dq
            dk_ref[:, cols] = dk
            dg_ref[:, cols] = dg
            dv_ref[:, cols] = dv
            dstate[j] = dst

    wide = HG_GROUP * HEAD_W
    blk = pl.BlockSpec((CHUNK, wide), lambda h, c: (n - 1 - c, h))
    out = jax.ShapeDtypeStruct((s, HG_HEADS * HEAD_W), F32)
    return _call(
        name, body, grid=(HG_HEADS // HG_GROUP, n),
        in_specs=[blk, blk, blk, pl.BlockSpec((CHUNK, wide), lambda h, c: (n - 1 - c, v_off // HG_GROUP + h)),
                  pl.BlockSpec((HG_GROUP, None, HEAD_W, HEAD_W), lambda h, c: (h, n - 1 - c, 0, 0)), blk],
        out_specs=[blk, blk, blk, blk], out_shape=[out, out, out, out],
        scratch=[pltpu.VMEM((HG_GROUP, HEAD_W, HEAD_W), F32)], args=[q, k, g, u, states, do], carry=carry)


def pair_sum(name, blocks, theirs):
    _, rows, cols = theirs.shape
    tile = next((t for t in (1024, 512, 256, 128, 64, 32, 16) if rows % t == 0 and t * cols <= (1 << 18)), rows)

    def body(a_ref, b_ref, o_ref):
        mine = jnp.where(lax.axis_index("c") == 0, a_ref[0].astype(F32), a_ref[1].astype(F32))
        o_ref[...] = (mine + b_ref[...].astype(F32)).astype(o_ref.dtype)

    blk = pl.BlockSpec((None, tile, cols), lambda q, i: (q, i, 0))
    return _call(name, body, grid=(N_CHIP, rows // tile),
                 in_specs=[pl.BlockSpec((None, 2, tile, cols), lambda q, i: (q, 0, i, 0)), blk], out_specs=[blk],
                 out_shape=[jax.ShapeDtypeStruct(theirs.shape, theirs.dtype)],
                 args=[blocks.reshape(N_CHIP, 2, rows, cols), theirs])[0]


def adamw(name, landed, w, m, v):
    rows, cols = w.shape
    slots = landed.shape[0]
    tile = next((t for t in (512, 256, 128, 64, 32, 16, 8) if rows % t == 0 and t * cols <= (1 << 17)), rows)

    def body(l_ref, w_ref, m_ref, v_ref, g_ref, d_ref, nm_ref, nv_ref):
        gv = l_ref[0].astype(F32)
        for s in range(1, slots):
            gv = gv + l_ref[s].astype(F32)
        nm = ADAM_B1 * m_ref[...] + (1.0 - ADAM_B1) * gv
        nv = ADAM_B2 * v_ref[...] + (1.0 - ADAM_B2) * jnp.square(gv)
        m_hat = nm / (1.0 - ADAM_B1 ** ADAM_STEP)
        v_hat = nv / (1.0 - ADAM_B2 ** ADAM_STEP)
        g_ref[...] = gv
        d_ref[...] = -ADAM_LR * (m_hat / (jnp.sqrt(v_hat) + ADAM_EPS) + ADAM_WD * w_ref[...])
        nm_ref[...] = nm
        nv_ref[...] = nv

    blk = pl.BlockSpec((tile, cols), lambda i: (i, 0))
    out = jax.ShapeDtypeStruct((rows, cols), F32)
    return _call(name, body, grid=(rows // tile,),
                 in_specs=[pl.BlockSpec((slots, tile, cols), lambda i: (0, i, 0)), blk, blk, blk],
                 out_specs=[blk] * 4, out_shape=[out] * 4, args=[landed, w, m, v])


def _swap_halves(pe):
    half = QK_ROPE // 2
    return jnp.concatenate([-pe[..., half:], pe[..., :half]], axis=-1)


def _unswap_halves(dsw):
    half = QK_ROPE // 2
    return jnp.concatenate([dsw[..., half:], -dsw[..., :half]], axis=-1)


def _w_in_ext(w):
    kpe = w[:, 9216:9280]
    z = jnp.zeros((w.shape[0], HEAD_W - QK_ROPE), w.dtype)
    pad = jnp.zeros((w.shape[0], U_PAD), w.dtype)
    return jnp.concatenate([w[:, :9216], w[:, 9280:], kpe, z, _swap_halves(kpe), z, pad], axis=1)


def _w_in_grad(d):
    dkpe = d[:, 13312:13376] + _unswap_halves(d[:, 13440:13504])
    return jnp.concatenate([d[:, :9216], dkpe, d[:, 9216:13312]], axis=1)


def _w_q_ext(w):
    w3 = w.reshape(Q_LORA, MLA_HEADS, MLA_QK)
    pe = w3[:, :, HEAD_W:]
    z = jnp.zeros((Q_LORA, MLA_HEADS, HEAD_W - QK_ROPE), w.dtype)
    wide = MLA_HEADS * HEAD_W
    return jnp.concatenate([w3[:, :, :HEAD_W].reshape(Q_LORA, wide),
                            jnp.concatenate([pe, z], axis=2).reshape(Q_LORA, wide),
                            jnp.concatenate([_swap_halves(pe), z], axis=2).reshape(Q_LORA, wide)], axis=1)


def _w_q_grad(d):
    wide = MLA_HEADS * HEAD_W
    d3 = [d[:, n * wide:(n + 1) * wide].reshape(Q_LORA, MLA_HEADS, HEAD_W) for n in range(3)]
    dpe = d3[1][:, :, :QK_ROPE] + _unswap_halves(d3[2][:, :, :QK_ROPE])
    return jnp.concatenate([d3[0], dpe], axis=2).reshape(Q_LORA, MLA_HEADS * MLA_QK)


def _hg_prep(f_raw, q_hg, logits):
    lb = jax.nn.softmax(logits, axis=0)[0:1, :]
    log_f = jnp.logaddexp(jnp.log(lb), jnp.log1p(-lb) + jax.nn.log_sigmoid(f_raw))
    k_in = (1.0 - lb) * jax.nn.sigmoid(-f_raw)
    return log_f, k_in, jax.nn.silu(q_hg)


def _rope(q_nope, q_pe, q_sw, k_nope, k_pe, k_sw, cos, sin):
    qf = jnp.concatenate([q_nope, q_pe * cos + q_sw * sin], axis=1)
    kf = jnp.concatenate([k_nope, k_pe * cos + k_sw * sin], axis=1)
    return qf, kf


def _step(a):
    x, mem, target = a['x'][0], a['mem'][0], a['loss_target'][0]
    s, d = x.shape
    nm = mem.shape[0]
    ff = N_DEV * a['ffn1_w_gate'].shape[-1]
    cs = ff // N_DEV
    tr = min(s, 128)
    th = min(s, 1024)
    ta = 512

    bf = {n: a[n][0].astype(BF16) for n in BIG}
    gat, full = {}, {}

    my_chip = 2 * lax.axis_index("x") + lax.axis_index("y")
    me = 2 * my_chip + lax.axis_index("c")

    def gathered(names, carry):
        for n, g8 in zip(names or [], carry.results if carry else []):
            r, c = bf[n].shape
            g8 = lax.dynamic_update_index_in_dim(g8, bf[n], me, 0)
            gat[n] = g8
            if not n.startswith('ffn'):
                full[n] = g8.transpose(1, 0, 2).reshape(r, N_DEV * c) if BIG[n] else g8.reshape(N_DEV * r, c)

    xa_names = ['xa_w_q', 'xa_w_k', 'xa_w_v', 'xa_w_o']
    ffn2_names = ['ffn2_w_gate', 'ffn2_w_up', 'ffn2_w_down']
    first = Gather([bf['ffn1_w_gate']])
    run_alone("gather_first", first)
    gathered(['ffn1_w_gate'], first)

    inv_freq = 1.0 / (ROPE_THETA ** (jnp.arange(0, QK_ROPE, 2, dtype=F32) / QK_ROPE))
    ang = a['positions'][0].astype(F32)[:, None] * inv_freq
    zero = jnp.zeros((s, HEAD_W - QK_ROPE), F32)
    cos = jnp.concatenate([jnp.cos(ang), jnp.cos(ang), zero], axis=1)
    sin = jnp.concatenate([jnp.sin(ang), jnp.sin(ang), zero], axis=1)

    gs = {}
    gb = {}

    def by_rows(g):
        return g.reshape(N_DEV, g.shape[0] // N_DEV, g.shape[1])

    def by_cols(g):
        return g.reshape(g.shape[0], N_DEV, g.shape[1] // N_DEV).transpose(1, 0, 2)

    pre_fn = lambda xv, g: (_rms(xv, g),)
    pre_res_fn = lambda xv, g: (_rms(xv, g), xv)

    def pre_norm(tag, x_in, g):
        return rowmap(tag + "_pre", pre_fn, [R_(x_in), P_(g)], [(BF16, x_in.shape[1], 0)], tile=tr)[0]

    def pre_norm_bwd(tag, x_in, g, dh, d_out, carry=None):
        return rowmap(tag + "_pre_bwd", pre_res_fn, [R_(x_in), P_(g)], tile=tr, cts=[R_(dh), R_(d_out)],
                      wrt=[0, 1], gdt=[F32, F32], carry=carry)

    def post_fn(weight):
        return lambda xv, y, g: (xv + weight * _rms(y, g),)

    def post_norm(tag, x_in, y, g, weight):
        return rowmap(tag + "_post", post_fn(weight), [R_(x_in), R_(y), P_(g)], [(F32, d, 0)], tile=tr)[0]

    def post_norm_bwd(tag, x_in, y, g, weight, d_out):
        return rowmap(tag + "_post_bwd", post_fn(weight), [R_(x_in), R_(y), P_(g)], tile=tr, cts=[R_(d_out)],
                      wrt=[1, 2], gdt=[BF16, F32])

    act_fn = lambda av, bv: (jax.nn.silu(av) * bv,)

    def fetch(plan, key):
        names = plan.get(key)
        return (names, Gather([bf[n] for n in names])) if names else (None, None)

    def send(plan, key):
        acts = plan.get(key)
        if not acts:
            return None, None
        parts = []
        for kind, names in acts:
            if kind == 'spread':
                for n in names:
                    chip_sums[n] = pair_sum("pair_" + n, gb[n], halves[n])
            parts.append(Halve([gb[n] for n in names]) if kind == 'halve' else Spread([chip_sums[n] for n in names]))
        return acts, Joint(parts)

    def ffn_fwd(tag, x_in, plan):
        h = pre_norm(tag, x_in, a[tag + '_pre_g'])
        names, carry = fetch(plan, 'gate')
        av = mm(tag + "_gate", h, gat[tag + '_w_gate'], 'nn', jdim='n', carry=carry)
        gathered(names, carry)
        names, carry = fetch(plan, 'up')
        bv = mm(tag + "_up", h, gat[tag + '_w_up'], 'nn', jdim='n', carry=carry)
        gathered(names, carry)
        a2, b2 = av.reshape(N_DEV * s, cs), bv.reshape(N_DEV * s, cs)
        z = rowmap(tag + "_act", act_fn, [R_(a2), R_(b2)], [(BF16, cs, 0)], tile=ta)[0].reshape(N_DEV, s, cs)
        names, carry = fetch(plan, 'dn')
        y = mm(tag + "_dn", z, gat[tag + '_w_down'], 'nn', jdim='k', carry=carry)
        gathered(names, carry)
        return post_norm(tag, x_in, y, a[tag + '_post_g'], 0.5), (x_in, h, a2, b2, z, y)

    def ffn_bwd(tag, res, d_out, plan):
        x_in, h, a2, b2, z, y = res
        dy, gs[tag + '_post_g'] = post_norm_bwd(tag, x_in, y, a[tag + '_post_g'], 0.5, d_out)
        acts, carry = send(plan, 'dn_dx')
        dz = mm(tag + "_dn_dx", dy, gat[tag + '_w_down'], 'nt', jdim='n', carry=carry)
        exchanged(acts, carry)
        gb[tag + '_w_down'] = mm(tag + "_dn_dw", z, dy, 'tn', jdim='m', out_dtype=BF16)
        acts, carry = send(plan, 'act_bwd')
        da, db = rowmap(tag + "_act_bwd", act_fn, [R_(a2), R_(b2)], tile=ta, cts=[R_(dz.reshape(N_DEV * s, cs))],
                        wrt=[0, 1], gdt=[BF16, BF16], carry=carry)
        exchanged(acts, carry)
        da, db = da.reshape(N_DEV, s, cs), db.reshape(N_DEV, s, cs)
        names, carry = send(plan, 'gu_dx')
        dh = mm(tag + "_gu_dx", [da, db], [gat[tag + '_w_gate'], gat[tag + '_w_up']], 'nt', jdim='k', carry=carry)
        exchanged(names, carry)
        gb[tag + '_w_gate'] = mm(tag + "_gate_dw", h, da, 'tn', jdim='n', out_dtype=BF16)
        names, carry = send(plan, 'up_dw')
        gb[tag + '_w_up'] = mm(tag + "_up_dw", h, db, 'tn', jdim='n', out_dtype=BF16, carry=carry)
        exchanged(names, carry)
        names, carry = send(plan, 'pre_bwd')
        d_in, gs[tag + '_pre_g'] = pre_norm_bwd(tag, x_in, a[tag + '_pre_g'], dh, d_out, carry)
        exchanged(names, carry)
        return d_in

    hg_out_fn = lambda o, og, g: (_rms(o, g) * jax.nn.silu(og),)
    mla_norm_fn = lambda cq, ckv, gq, gkv: (_rms(cq, gq), _rms(ckv, gkv))
    gate_fn = lambda ga, gb, ya, yb: (jax.nn.sigmoid(ga) * ya + jax.nn.sigmoid(gb) * yb,)
    mla = dict(heads=MLA_HEADS, dq=2 * HEAD_W, dv=HEAD_W, koff=0, kstride=1, voff=1, vstride=2,
               scale=MLA_QK ** -0.5, causal=True)

    def mix_fwd(x_in, plan):
        w_in = _w_in_ext(full['w_in'])
        h = pre_norm("mix", x_in, a['mix_pre_g'])
        names, carry = fetch(plan, 'in')
        u = mm("mix_in", h, w_in, 'nn', carry=carry)
        gathered(names, carry)
        w_q, w_kv = _w_q_ext(full['mla_w_q_up']), full['mla_w_kv_up']
        hg_ins = [R_(u, 2048, 1), R_(u, 2048, 0), P_(a['hgrn_lb_logits'])]
        log_f, k_in, q_in = rowmap("hg_prep", _hg_prep, hg_ins, [(F32, 2048, 0)] * 3, tile=tr)
        names, carry = fetch(plan, 'scan')
        o_a, states = hg_scan_fwd("hg_scan", q_in, k_in, log_f, u, 32, carry=carry)
        gathered(names, carry)
        out_ins = [R_(o_a, HEAD_W, 0, 1), R_(u, HEAD_W, 48, 1), P_(a['hg_norm_g'], HEAD_W, 0, 1)]
        oag = rowmap("hg_out", hg_out_fn, out_ins, [(BF16, HEAD_W, 1)], tile=th, ncol=HG_HEADS)[0]
        y_a = mm("mix_a", oag, full['w_branch_a'], 'nn')
        norm_ins = [R_(u, 512, 16), R_(u, 512, 17), P_(a['mla_q_norm_g']), P_(a['mla_kv_norm_g'])]
        cqn, ckvn = rowmap("mla_norm", mla_norm_fn, norm_ins, [(BF16, 512, 0)] * 2, tile=tr)
        q_all = mm("mla_qup", cqn, w_q, 'nn')
        kv = mm("mla_kvup", ckvn, w_kv, 'nn')
        rope_ins = [R_(q_all, HEAD_W, 0, 1), R_(q_all, HEAD_W, 16, 1), R_(q_all, HEAD_W, 32, 1),
                    R_(kv, HEAD_W, 0, 2), R_(u, HEAD_W, 104, 0), R_(u, HEAD_W, 105, 0), R_(cos), R_(sin)]
        qf, kf = rowmap("mla_rope", _rope, rope_ins, [(BF16, 2 * HEAD_W, 1)] * 2, tile=th, ncol=MLA_HEADS,
                        rows_inner=False)
        o_b = attn_fwd("mla_attn", qf, kf, kv, **mla)
        y_b = mm("mix_b", o_b, full['w_branch_b'], 'nn')
        gate_ins = [R_(u, 1024, 9, 1), R_(u, 1024, 11, 1), R_(y_a, 1024, 0, 1), R_(y_b, 1024, 0, 1)]
        y = rowmap("mix_gate", gate_fn, gate_ins, [(BF16, 1024, 1)], tile=tr, ncol=2)[0]
        yo = mm("mix_out", y, full['w_out'], 'nn')
        res = (x_in, h, hg_ins, q_in, k_in, log_f, u, states, out_ins, oag, norm_ins, cqn, ckvn, rope_ins, qf, kf,
               kv, o_b, gate_ins, y, yo, w_in, w_q, w_kv)
        return post_norm("mix", x_in, yo, a['mix_post_g'], 1.0), res

    def mix_bwd(res, d_out, plan):
        (x_in, h, hg_ins, q_in, k_in, log_f, u, states, out_ins, oag, norm_ins, cqn, ckvn, rope_ins, qf, kf, kv,
         o_b, gate_ins, y, yo, w_in, w_q, w_kv) = res
        dyo, gs['mix_post_g'] = post_norm_bwd("mix", x_in, yo, a['mix_post_g'], 1.0, d_out)
        dy = mm("mix_out_dx", dyo, full['w_out'], 'nt')
        gb['w_out'] = by_rows(mm("mix_out_dw", y, dyo, 'tn', out_dtype=BF16))
        dga, dgb, dya, dyb = rowmap("mix_gate_bwd", gate_fn, gate_ins, tile=tr, ncol=2, cts=[R_(dy, 1024, 0, 1)],
                                    wrt=[0, 1, 2, 3], gdt=[BF16] * 4)
        gb['w_branch_b'] = by_rows(mm("mix_b_dw", o_b, dyb, 'tn', out_dtype=BF16))
        do_b = mm("mix_b_dx", dyb, full['w_branch_b'], 'nt')
        names, carry = send(plan, 'attn_bwd')
        dqf, dkf, dv = attn_bwd("mla_attn_bwd", qf, kf, kv, do_b, carry=carry, **mla)
        exchanged(names, carry)
        dqn, dqp, dqs, dkn, dkpe, dksw = rowmap(
            "mla_rope_bwd", _rope, rope_ins, tile=th, ncol=MLA_HEADS, rows_inner=False,
            cts=[R_(dqf, 2 * HEAD_W, 0, 1), R_(dkf, 2 * HEAD_W, 0, 1)], wrt=[0, 1, 2, 3, 4, 5],
            gdt=[BF16, BF16, BF16, BF16, F32, F32])
        dq_all = jnp.concatenate([dqn, dqp, dqs], axis=1)
        dkv = jnp.concatenate([dkn.reshape(s, MLA_HEADS, HEAD_W), dv.astype(BF16).reshape(s, MLA_HEADS, HEAD_W)],
                              axis=2).reshape(s, 2 * MLA_HEADS * HEAD_W)
        gb['mla_w_q_up'] = by_cols(_w_q_grad(mm("mla_qup_dw", cqn, dq_all, 'tn', out_dtype=BF16)))
        dcqn = mm("mla_qup_dx", dq_all, w_q, 'nt')
        gb['mla_w_kv_up'] = by_cols(mm("mla_kvup_dw", ckvn, dkv, 'tn', out_dtype=BF16))
        dckvn = mm("mla_kvup_dx", dkv, w_kv, 'nt')
        dcq, dckv, gs['mla_q_norm_g'], gs['mla_kv_norm_g'] = rowmap(
            "mla_norm_bwd", mla_norm_fn, norm_ins, tile=tr, cts=[R_(dcqn), R_(dckvn)], wrt=[0, 1, 2, 3],
            gdt=[BF16, BF16, F32, F32])
        gb['w_branch_a'] = by_rows(mm("mix_a_dw", oag, dya, 'tn', out_dtype=BF16))
        doag = mm("mix_a_dx", dya, full['w_branch_a'], 'nt')
        acts, carry = send(plan, 'out_bwd')
        do_a, dog, gs['hg_norm_g'] = rowmap("hg_out_bwd", hg_out_fn, out_ins, tile=th, ncol=HG_HEADS,
                                            cts=[R_(doag, HEAD_W, 0, 1)], wrt=[0, 1, 2], gdt=[F32, BF16, F32],
                                            carry=carry)
        exchanged(acts, carry)
        names, carry = send(plan, 'scan_bwd')
        dq_in, dk_in, dlog_f, di = hg_scan_bwd("hg_scan_bwd", q_in, k_in, log_f, u, 32, states, do_a, carry=carry)
        exchanged(names, carry)
        df, dq_hg, gs['hgrn_lb_logits'] = rowmap("hg_prep_bwd", _hg_prep, hg_ins, tile=tr,
                                                 cts=[R_(dlog_f), R_(dk_in), R_(dq_in)], wrt=[0, 1, 2],
                                                 gdt=[BF16, BF16, F32])
        du = jnp.concatenate([dq_hg, df, di.astype(BF16), dog, dcq, dckv, dga, dgb, dkpe.astype(BF16),
                              dksw.astype(BF16), jnp.zeros((s, U_PAD), BF16)], axis=1)
        gb['w_in'] = by_cols(_w_in_grad(mm("mix_in_dw", h, du, 'tn', out_dtype=BF16)))
        names, carry = send(plan, 'in_dx')
        dh = mm("mix_in_dx", du, w_in, 'nt', carry=carry)
        exchanged(names, carry)
        d_in, gs['mix_pre_g'] = pre_norm_bwd("mix", x_in, a['mix_pre_g'], dh, d_out)
        return d_in

    xa = dict(heads=XA_HEADS, dq=HEAD_W, dv=HEAD_W, koff=0, kstride=1, voff=XA_HEADS, vstride=1,
              scale=HEAD_W ** -0.5, causal=False)
    tm_ = min(nm, 128)

    def xa_fwd(x_in):
        w_xkv = jnp.concatenate([full['xa_w_k'], full['xa_w_v']], axis=1)
        h = pre_norm("xa", x_in, a['xa_pre_g'])
        mn = rowmap("xa_mem", pre_fn, [R_(mem), P_(a['xa_mem_g'])], [(BF16, d, 0)], tile=tm_)[0]
        q = mm("xa_q", h, full['xa_w_q'], 'nn')
        kv = mm("xa_kv", mn, w_xkv, 'nn')
        o = attn_fwd("xa_attn", q, kv, kv, **xa)
        yo = mm("xa_o", o, full['xa_w_o'], 'nn')
        return post_norm("xa", x_in, yo, a['xa_post_g'], 1.0), (x_in, h, mn, q, kv, o, yo, w_xkv)

    def xa_bwd(res, d_out, plan):
        x_in, h, mn, q, kv, o, yo, w_xkv = res
        dyo, gs['xa_post_g'] = post_norm_bwd("xa", x_in, yo, a['xa_post_g'], 1.0, d_out)
        do = mm("xa_o_dx", dyo, full['xa_w_o'], 'nt')
        gb['xa_w_o'] = by_cols(mm("xa_o_dw", o, dyo, 'tn', out_dtype=BF16))
        dq, dk, dv = attn_bwd("xa_attn_bwd", q, kv, kv, do, **xa)
        dkv = jnp.concatenate([dk, dv], axis=1).astype(BF16)
        dw = mm("xa_kv_dw", mn, dkv, 'tn', out_dtype=BF16)
        gb['xa_w_k'], gb['xa_w_v'] = by_rows(dw[:, :XA_HEADS * HEAD_W]), by_rows(dw[:, XA_HEADS * HEAD_W:])
        dmn = mm("xa_kv_dx", dkv, w_xkv, 'nt')
        gs['xa_mem_g'] = rowmap("xa_mem_bwd", pre_fn, [R_(mem), P_(a['xa_mem_g'])], tile=tm_, cts=[R_(dmn)],
                                wrt=[1], gdt=[F32])[0]
        gb['xa_w_q'] = by_rows(mm("xa_q_dw", h, dq, 'tn', out_dtype=BF16))
        dh = mm("xa_q_dx", dq, full['xa_w_q'], 'nt')
        acts, carry = send(plan, 'pre_bwd')
        d_in, gs['xa_pre_g'] = pre_norm_bwd("xa", x_in, a['xa_pre_g'], dh, d_out, carry)
        exchanged(acts, carry)
        return d_in

    landed = {}

    halves = {}
    chip_sums = {}

    def exchanged(acts, carry):
        for (kind, names), part in zip(acts or [], carry.parts if carry else []):
            for n, got in zip(names, part.results):
                if kind == 'halve':
                    halves[n] = got
                else:
                    own = lax.dynamic_index_in_dim(chip_sums[n], my_chip, 0, keepdims=True)
                    landed[n] = lax.dynamic_update_slice_in_dim(got, own, my_chip, 0)

    x1, r1 = ffn_fwd('ffn1', x, {'gate': ['ffn1_w_up'], 'up': ['ffn1_w_down'], 'dn': ['w_in']})
    x2, r2 = mix_fwd(x1, {'in': ['mla_w_q_up', 'mla_w_kv_up', 'w_branch_a', 'w_branch_b', 'w_out'],
                          'scan': xa_names + ffn2_names})
    x3, r3 = xa_fwd(x2)
    x4, r4 = ffn_fwd('ffn2', x3, {})

    def loss_fn(y, t):
        diff = y - t
        return diff * (1.0 / d), jnp.mean(diff * diff, axis=-1, keepdims=True)

    d4, row_loss = rowmap("loss", loss_fn, [R_(x4), R_(target)], [(F32, d, 0), (F32, 1, 0)], tile=tr)
    loss = lax.psum(0.5 * jnp.sum(row_loss), ("x", "y", "c"))
    late = ['mla_w_q_up', 'mla_w_kv_up', 'w_branch_a']
    d3 = ffn_bwd('ffn2', r4, d4, {'act_bwd': [('halve', ['ffn2_w_down'])], 'gu_dx': [('spread', ['ffn2_w_down'])],
                                  'up_dw': [('halve', ['ffn2_w_gate'])], 'pre_bwd': [('halve', ['ffn2_w_up'])]})
    d2 = xa_bwd(r3, d3, {'pre_bwd': [('halve', xa_names)]})
    d1 = mix_bwd(r2, d2, {'attn_bwd': [('spread', ['ffn2_w_gate', 'ffn2_w_up']), ('halve', ['w_out', 'w_branch_b'])],
                          'out_bwd': [('halve', late)],
                          'scan_bwd': [('spread', xa_names + ['w_out', 'w_branch_b'] + late)],
                          'in_dx': [('halve', ['w_in'])]})
    grad_x = ffn_bwd('ffn1', r1, d1, {'dn_dx': [('spread', ['w_in'])], 'act_bwd': [('halve', ['ffn1_w_down'])],
                                      'gu_dx': [('spread', ['ffn1_w_down'])], 'up_dw': [('halve', ['ffn1_w_gate'])],
                                      'pre_bwd': [('spread', ['ffn1_w_gate']), ('halve', ['ffn1_w_up'])]})
    acts, carry = send({'end': [('spread', ['ffn1_w_up'])]}, 'end')
    run_alone("spread_last", carry)
    exchanged(acts, carry)

    def pack_small(vals):
        flat = jnp.concatenate([vals[n].reshape(-1) for n in SMALL])
        rows = -(-flat.shape[0] // PACK_W)
        rows = -(-rows // 8) * 8
        return jnp.pad(flat, (0, rows * PACK_W - flat.shape[0])).reshape(rows, PACK_W)

    def unpack_small(buf):
        flat, out, at = buf.reshape(-1), {}, 0
        for n in SMALL:
            size = a[n].shape[0] * a[n].shape[1]
            out[n] = flat[at:at + size].reshape(a[n].shape)
            at += size
        return out

    g_small = pack_small(gs)
    small = Gather([g_small])
    run_alone("gather_g_small", small)
    g_small = lax.dynamic_update_index_in_dim(small.results[0], g_small, me, 0)

    grads, delta, new_m, new_v = {}, {}, {}, {}
    packs = adamw("adamw_small", g_small, pack_small(a), pack_small({n: a['m_' + n] for n in SMALL}),
                  pack_small({n: a['v_' + n] for n in SMALL}))
    for dst, buf in zip((grads, delta, new_m, new_v), packs):
        dst.update(unpack_small(buf))
    for n in BIG:
        outs = adamw("adamw_" + n, landed[n], a[n][0], a['m_' + n][0], a['v_' + n][0])
        grads[n], delta[n], new_m[n], new_v[n] = (t.reshape(a[n].shape) for t in outs)

    return (loss, grad_x[None], *[grads[n] for n in WEIGHTS], *[delta[n] for n in WEIGHTS],
            *[new_m[n] for n in WEIGHTS], *[new_v[n] for n in WEIGHTS])


def kernel(x, mem, positions, hgrn_lb_logits, ffn1_pre_g, ffn1_w_gate, ffn1_w_up, ffn1_w_down, ffn1_post_g, mix_pre_g, w_in, hg_norm_g, mla_q_norm_g, mla_w_q_up, mla_kv_norm_g, mla_w_kv_up, w_branch_a, w_branch_b, w_out, mix_post_g, xa_pre_g, xa_mem_g, xa_w_q, xa_w_k, xa_w_v, xa_w_o, xa_post_g, ffn2_pre_g, ffn2_w_gate, ffn2_w_up, ffn2_w_down, ffn2_post_g, loss_target, m_hgrn_lb_logits, m_ffn1_pre_g, m_ffn1_w_gate, m_ffn1_w_up, m_ffn1_w_down, m_ffn1_post_g, m_mix_pre_g, m_w_in, m_hg_norm_g, m_mla_q_norm_g, m_mla_w_q_up, m_mla_kv_norm_g, m_mla_w_kv_up, m_w_branch_a, m_w_branch_b, m_w_out, m_mix_post_g, m_xa_pre_g, m_xa_mem_g, m_xa_w_q, m_xa_w_k, m_xa_w_v, m_xa_w_o, m_xa_post_g, m_ffn2_pre_g, m_ffn2_w_gate, m_ffn2_w_up, m_ffn2_w_down, m_ffn2_post_g, v_hgrn_lb_logits, v_ffn1_pre_g, v_ffn1_w_gate, v_ffn1_w_up, v_ffn1_w_down, v_ffn1_post_g, v_mix_pre_g, v_w_in, v_hg_norm_g, v_mla_q_norm_g, v_mla_w_q_up, v_mla_kv_norm_g, v_mla_w_kv_up, v_w_branch_a, v_w_branch_b, v_w_out, v_mix_post_g, v_xa_pre_g, v_xa_mem_g, v_xa_w_q, v_xa_w_k, v_xa_w_v, v_xa_w_o, v_xa_post_g, v_ffn2_pre_g, v_ffn2_w_gate, v_ffn2_w_up, v_ffn2_w_down, v_ffn2_post_g):
    return _step(dict(locals()))
```

```python
import functools

import jax
import jax.numpy as jnp
from jax import lax
from jax.experimental import pallas as pl
from jax.experimental.pallas import tpu as pltpu

F32 = jnp.float32
BF16 = jnp.bfloat16

N_DEV = 8
D_MODEL = 2048
CHUNK = 64
CHUNK_SHIFT = 6
SUB = 16
HG_HEADS = 16
HG_GROUP = 8
HEAD_W = 128
MLA_HEADS = 16
Q_LORA = 512
KV_LORA = 512
QK_ROPE = 64
MLA_QK = 192
XA_HEADS = 4
ROPE_THETA = 10000.0
EPS = 1e-6
PACK_W = 1024
VMEM_LIMIT = 56 * 1024 * 1024

ADAM_LR = 0.001
ADAM_B1 = 0.9
ADAM_B2 = 0.999
ADAM_EPS = 1e-08
ADAM_WD = 0.01
ADAM_STEP = 10

U_PAD = 256

WEIGHTS = ['hgrn_lb_logits', 'ffn1_pre_g', 'ffn1_w_gate', 'ffn1_w_up', 'ffn1_w_down', 'ffn1_post_g', 'mix_pre_g',
           'w_in', 'hg_norm_g', 'mla_q_norm_g', 'mla_w_q_up', 'mla_kv_norm_g', 'mla_w_kv_up', 'w_branch_a',
           'w_branch_b', 'w_out', 'mix_post_g', 'xa_pre_g', 'xa_mem_g', 'xa_w_q', 'xa_w_k', 'xa_w_v', 'xa_w_o',
           'xa_post_g', 'ffn2_pre_g', 'ffn2_w_gate', 'ffn2_w_up', 'ffn2_w_down', 'ffn2_post_g']
BIG = {'ffn1_w_gate': True, 'ffn1_w_up': True, 'ffn1_w_down': False, 'w_in': True, 'mla_w_q_up': True,
       'mla_w_kv_up': True, 'w_branch_a': False, 'w_branch_b': False, 'w_out': False, 'xa_w_q': False,
       'xa_w_k': False, 'xa_w_v': False, 'xa_w_o': True, 'ffn2_w_gate': True, 'ffn2_w_up': True,
       'ffn2_w_down': False}
SMALL = [n for n in WEIGHTS if n not in BIG]


def _cparams(**kw):
    return pltpu.CompilerParams(vmem_limit_bytes=VMEM_LIMIT, **kw)


def _pick(dim, cands):
    for c in cands:
        if dim % c == 0:
            return c
    return dim


def _place():
    return lax.axis_index("x"), lax.axis_index("y"), lax.axis_index("c")


def _slot(px, py, pc):
    return 4 * px + 2 * py + pc


class Gather:
    def __init__(self, tensors):
        self.operands = list(tensors)
        self.out_shape = [jax.ShapeDtypeStruct((N_DEV,) + t.shape, t.dtype) for t in tensors]
        n = len(tensors)
        self.scratch = [pltpu.SemaphoreType.DMA((n, N_DEV - 1)), pltpu.SemaphoreType.DMA((n, N_DEV - 1))]

    def _copies(self, t, x_ref, out_ref, send, recv):
        x, y, c = _place()
        chips = [(1 - x, y), (x, 1 - y), (1 - x, 1 - y)]

        def copy(k, block, to, src=None):
            rows = out_ref.at[_slot(*block)]
            return pltpu.make_async_remote_copy(src_ref=rows if src is None else src, dst_ref=rows,
                                                send_sem=send.at[t, k], recv_sem=recv.at[t, k], device_id=to,
                                                device_id_type=pl.DeviceIdType.MESH)

        return (x, y, c), chips, copy

    def start(self, ins, outs, scr):
        send, recv = scr
        for t, (x_ref, out_ref) in enumerate(zip(ins, outs)):
            (x, y, c), chips, copy = self._copies(t, x_ref, out_ref, send, recv)
            copy(0, (x, y, c), (x, y, 1 - c), src=x_ref).start()
            for j, chip in enumerate(chips):
                copy(1 + j, (x, y, c), (*chip, c), src=x_ref).start()

    def finish(self, ins, outs, scr):
        send, recv = scr
        for t, (x_ref, out_ref) in enumerate(zip(ins, outs)):
            (x, y, c), chips, copy = self._copies(t, x_ref, out_ref, send, recv)
            passed = [copy(4 + j, (*chip, c), (x, y, 1 - c)) for j, chip in enumerate(chips)]
            for j, chip in enumerate(chips):
                copy(1 + j, (*chip, c), (x, y, c)).wait_recv()
                passed[j].start()
            copy(0, (x, y, 1 - c), (x, y, c)).wait_recv()
            for j, chip in enumerate(chips):
                copy(4 + j, (*chip, 1 - c), (x, y, c)).wait_recv()
            copy(0, (x, y, c), (x, y, 1 - c), src=x_ref).wait_send()
            for j, chip in enumerate(chips):
                copy(1 + j, (x, y, c), (*chip, c), src=x_ref).wait_send()
                passed[j].wait_send()

    def set_results(self, res):
        self.results = list(res)


N_CHIP = N_DEV // 2


class Halve:
    def __init__(self, tensors):
        self.operands = list(tensors)
        self.out_shape = [jax.ShapeDtypeStruct((N_CHIP,) + t.shape[1:], t.dtype) for t in tensors]
        n = len(tensors)
        self.scratch = [pltpu.SemaphoreType.DMA((n, N_CHIP)), pltpu.SemaphoreType.DMA((n, N_CHIP))]

    def _copies(self, t, x_ref, theirs_ref, scr):
        send, recv = scr
        x, y, c = _place()
        return [pltpu.make_async_remote_copy(
            src_ref=x_ref.at[2 * q + 1 - c], dst_ref=theirs_ref.at[q], send_sem=send.at[t, q],
            recv_sem=recv.at[t, q], device_id=(x, y, 1 - c), device_id_type=pl.DeviceIdType.MESH)
            for q in range(N_CHIP)]

    def start(self, ins, outs, scr):
        for t, (x_ref, theirs_ref) in enumerate(zip(ins, outs)):
            for give in self._copies(t, x_ref, theirs_ref, scr):
                give.start()

    def finish(self, ins, outs, scr):
        for t, (x_ref, theirs_ref) in enumerate(zip(ins, outs)):
            for give in self._copies(t, x_ref, theirs_ref, scr):
                give.wait_recv()
                give.wait_send()

    def set_results(self, res):
        self.results = list(res)


class Spread:
    def __init__(self, tensors):
        self.operands = list(tensors)
        self.out_shape = [jax.ShapeDtypeStruct(t.shape, t.dtype) for t in tensors]
        n = len(tensors)
        self.scratch = [pltpu.SemaphoreType.DMA((n, N_CHIP - 1)), pltpu.SemaphoreType.DMA((n, N_CHIP - 1))]

    def _copies(self, t, y_ref, out_ref, scr, outgoing):
        send, recv = scr
        x, y, c = _place()
        copies = []
        for k in range(1, N_CHIP):
            px, py = x ^ (k >> 1), y ^ (k & 1)
            copies.append(pltpu.make_async_remote_copy(
                src_ref=y_ref.at[2 * px + py], dst_ref=out_ref.at[2 * x + y if outgoing else 2 * px + py],
                send_sem=send.at[t, k - 1], recv_sem=recv.at[t, k - 1], device_id=(px, py, c),
                device_id_type=pl.DeviceIdType.MESH))
        return copies

    def start(self, ins, outs, scr):
        for t, (y_ref, out_ref) in enumerate(zip(ins, outs)):
            for give in self._copies(t, y_ref, out_ref, scr, True):
                give.start()

    def finish(self, ins, outs, scr):
        for t, (y_ref, out_ref) in enumerate(zip(ins, outs)):
            for take in self._copies(t, y_ref, out_ref, scr, False):
                take.wait_recv()
            for give in self._copies(t, y_ref, out_ref, scr, True):
                give.wait_send()

    def set_results(self, res):
        self.results = list(res)


class Joint:
    def __init__(self, parts):
        self.parts = list(parts)
        self.operands = [o for p in parts for o in p.operands]
        self.out_shape = [o for p in parts for o in p.out_shape]
        self.scratch = [o for p in parts for o in p.scratch]

    def _split(self, ins, outs, scr):
        i = o = s = 0
        for p in self.parts:
            ni, no, ns = len(p.operands), len(p.out_shape), len(p.scratch)
            yield p, ins[i:i + ni], outs[o:o + no], scr[s:s + ns]
            i, o, s = i + ni, o + no, s + ns

    def start(self, ins, outs, scr):
        for p, a, b, c in self._split(ins, outs, scr):
            p.start(a, b, c)

    def finish(self, ins, outs, scr):
        for p, a, b, c in self._split(ins, outs, scr):
            p.finish(a, b, c)

    def set_results(self, res):
        for p, _, part, _ in self._split([], list(res), []):
            p.set_results(part)


_HBM = pl.BlockSpec(memory_space=pltpu.HBM)


def run_alone(name, carry):
    n_in, n_out = len(carry.operands), len(carry.out_shape)

    def body(*refs):
        ins, outs, scr = refs[:n_in], refs[n_in:n_in + n_out], refs[n_in + n_out:]
        carry.start(ins, outs, scr)
        carry.finish(ins, outs, scr)

    res = pl.pallas_call(body, name=name, in_specs=[_HBM] * n_in, out_specs=[_HBM] * n_out,
                         out_shape=carry.out_shape, scratch_shapes=carry.scratch)(*carry.operands)
    carry.set_results(list(res))


def _call(name, body, *, grid, in_specs, out_specs, out_shape, args, scratch=(), carry=None):
    in_specs, out_specs, out_shape, scratch = list(in_specs), list(out_specs), list(out_shape), list(scratch)
    if carry is None:
        return list(pl.pallas_call(body, name=name, grid=grid, in_specs=in_specs, out_specs=out_specs,
                                   out_shape=out_shape, scratch_shapes=scratch, compiler_params=_cparams())(*args))
    n_in, n_out, n_scr = len(in_specs), len(out_shape), len(scratch)
    c_in, c_out = len(carry.operands), len(carry.out_shape)

    def wrapped(*refs):
        ins, c_ins = refs[:n_in], refs[n_in:n_in + c_in]
        at = n_in + c_in
        outs, c_outs = refs[at:at + n_out], refs[at + n_out:at + n_out + c_out]
        at += n_out + c_out
        scr, c_scr = refs[at:at + n_scr], refs[at + n_scr:]
        first = functools.reduce(jnp.logical_and, [pl.program_id(d) == 0 for d in range(len(grid))])
        last = functools.reduce(jnp.logical_and, [pl.program_id(d) == grid[d] - 1 for d in range(len(grid))])

        @pl.when(first)
        def _():
            carry.start(c_ins, c_outs, c_scr)

        body(*ins, *outs, *scr)

        @pl.when(last)
        def _():
            carry.finish(c_ins, c_outs, c_scr)

    res = pl.pallas_call(
        wrapped, name=name, grid=grid, in_specs=in_specs + [_HBM] * c_in, out_specs=out_specs + [_HBM] * c_out,
        out_shape=out_shape + carry.out_shape, scratch_shapes=scratch + carry.scratch, compiler_params=_cparams(),
    )(*args, *carry.operands)
    carry.set_results(list(res[n_out:]))
    return list(res[:n_out])


def R_(arr, w=None, off=0, stride=0):
    return ('r', arr, arr.shape[1] if w is None else w, off, stride)


def P_(arr, w=None, off=0, stride=0):
    return ('p', arr, arr.shape[1] if w is None else w, off, stride)


def rowmap(name, fn, ins, outs=None, *, tile, ncol=1, rows_inner=True, cts=None, wrt=None, gdt=None, cat=False,
           carry=None):
    rows = next(a.shape[0] for k, a, *_ in ins if k == 'r')
    nrow = rows // tile
    assert nrow * tile == rows
    grid = (ncol, nrow) if rows_inner else (nrow, ncol)

    def ij(g0, g1):
        return (g1, g0) if rows_inner else (g0, g1)

    def spec(kind, arr, w, off, stride):
        if kind == 'r':
            return pl.BlockSpec((tile, w), lambda g0, g1: (ij(g0, g1)[0], off + stride * ij(g0, g1)[1]))
        return pl.BlockSpec((arr.shape[0], w), lambda g0, g1: (0, off + stride * ij(g0, g1)[1]))

    ops = list(ins) + list(cts or [])
    in_specs = [spec(*o) for o in ops]
    n_in = len(ins)
    fwd = cts is None
    out_shape, out_specs, acc = [], [], []
    if fwd:
        for dt, w, stride in outs:
            out_shape.append(jax.ShapeDtypeStruct((rows, w * (ncol if stride else 1)), dt))
            out_specs.append(spec('r', None, w, 0, stride))
            acc.append(None)
    elif cat:
        widths = [ins[i][2] for i in wrt]
        assert ncol == 1 and all(ins[i][0] == 'r' for i in wrt)
        out_shape.append(jax.ShapeDtypeStruct((rows, sum(widths)), gdt))
        out_specs.append(spec('r', None, sum(widths), 0, 0))
    else:
        for n, i in enumerate(wrt):
            kind, arr, w, off, stride = ins[i]
            width = w * (ncol if stride else 1)
            if kind == 'r':
                out_shape.append(jax.ShapeDtypeStruct((rows, width), gdt[n]))
                out_specs.append(spec('r', None, w, 0, 1 if stride else 0))
                shared = stride == 0 and ncol > 1
                assert not shared or (not rows_inner and gdt[n] == F32)
                acc.append('col' if shared else None)
            else:
                out_shape.append(jax.ShapeDtypeStruct((arr.shape[0], width), F32))
                out_specs.append(spec('p', arr, w, 0, 1 if stride else 0))
                assert rows_inner or ncol == 1
                acc.append('row')

    def body(*refs):
        i, j = ij(pl.program_id(0), pl.program_id(1))
        vals = [r[...].astype(F32) for r in refs[:n_in]]
        out_refs = refs[len(ops):]
        if fwd:
            for o_ref, o in zip(out_refs, fn(*vals)):
                o_ref[...] = o.astype(o_ref.dtype)
            return

        def f(*d):
            full = list(vals)
            for n, idx in enumerate(wrt):
                full[idx] = d[n]
            return fn(*full)

        _, vjp = jax.vjp(f, *[vals[idx] for idx in wrt])
        grads = vjp(tuple(r[...].astype(F32) for r in refs[n_in:len(ops)]))
        if cat:
            o_ref, at = out_refs[0], 0
            for g in grads:
                o_ref[:, at:at + g.shape[1]] = g.astype(o_ref.dtype)
                at += g.shape[1]
            return
        for o_ref, g, a in zip(out_refs, grads, acc):
            if a is None:
                o_ref[...] = g.astype(o_ref.dtype)
            else:
                first = (i if a == 'row' else j) == 0

                @pl.when(first)
                def _(o_ref=o_ref):
                    o_ref[...] = jnp.zeros_like(o_ref)

                o_ref[...] += g

    return _call(name, body, grid=grid, in_specs=in_specs, out_specs=out_specs, out_shape=out_shape,
                 args=[o[1] for o in ops], carry=carry)


def _rms(x, g):
    return x * lax.rsqrt(jnp.mean(x * x, axis=-1, keepdims=True) + EPS) * g


_DIMS = {'nn': (((1,), (0,)), ((), ())), 'nt': (((1,), (1,)), ((), ())), 'tn': (((0,), (0,)), ((), ()))}


def mm(name, a, b, mode, *, jdim=None, out_dtype=F32, carry=None):
    a_list = list(a) if isinstance(a, (list, tuple)) else [a]
    b_list = list(b) if isinstance(b, (list, tuple)) else [b]
    a_order = ('k', 'm') if mode == 'tn' else ('m', 'k')
    b_order = ('n', 'k') if mode == 'nt' else ('k', 'n')
    size, blocks = {}, 1
    for arr, order in ((a_list[0], a_order), (b_list[0], b_order)):
        shape = arr.shape
        if jdim in order:
            blocks, shape = shape[0], shape[1:]
        for dname, extent in zip(order, shape):
            assert size.setdefault(dname, extent) == extent
    tile = {'m': _pick(size['m'], (1024, 512, 256, 128)), 'n': _pick(size['n'], (512, 256, 128)),
            'k': size['k'] if size['k'] <= 2048 else _pick(size['k'], (2304, 2048, 1024, 512, 256, 128))}
    if jdim is not None:
        tile[jdim] = size[jdim]
    if a_list[0].dtype == F32 and tile['k'] * tile['m'] > (1 << 20):
        tile['m'] = _pick(size['m'], (512, 256, 128))
    grid = tuple(blocks if d == jdim else size[d] // tile[d] for d in ('m', 'n', 'k'))
    nk = grid[2]

    def spec(order):
        shape = tuple(tile[d] for d in order)

        def imap(i, j, k):
            g = {'m': i, 'n': j, 'k': k}
            idx = tuple(0 if d == jdim else g[d] for d in order)
            return ((g[jdim],) + idx) if jdim in order else idx

        return pl.BlockSpec(((None,) + shape) if jdim in order else shape, imap)

    dims = _DIMS[mode]
    nt = len(a_list)

    def product(refs):
        acc = None
        for a_ref, b_ref in zip(refs[:nt], refs[nt:2 * nt]):
            p = lax.dot_general(a_ref[...].astype(BF16), b_ref[...].astype(BF16), dims, preferred_element_type=F32)
            acc = p if acc is None else acc + p
        return acc

    def body_once(*refs):
        refs[2 * nt][...] = product(refs).astype(refs[2 * nt].dtype)

    def body_acc(*refs):
        o_ref, acc_ref = refs[2 * nt], refs[2 * nt + 1]
        k = pl.program_id(2)

        @pl.when(k == 0)
        def _():
            acc_ref[...] = jnp.zeros_like(acc_ref)

        acc_ref[...] += product(refs)

        @pl.when(k == nk - 1)
        def _():
            o_ref[...] = acc_ref[...].astype(o_ref.dtype)

    out_dims = (size['m'], size['n'])
    out_shape = jax.ShapeDtypeStruct(((blocks,) + out_dims) if jdim in ('m', 'n') else out_dims, out_dtype)
    return _call(name, body_once if nk == 1 else body_acc, grid=grid,
                 in_specs=[spec(a_order)] * nt + [spec(b_order)] * nt, out_specs=[spec(('m', 'n'))],
                 out_shape=[out_shape], args=a_list + b_list,
                 scratch=[] if nk == 1 else [pltpu.VMEM((tile['m'], tile['n']), F32)], carry=carry)[0]


def _probs(q, k, i, tq, scale, causal):
    s = lax.dot_general(q, k, _DIMS['nt'], preferred_element_type=F32) * scale
    if causal:
        shape = s.shape
        q_chunk = jnp.right_shift(i * tq + lax.broadcasted_iota(jnp.int32, shape, 0), CHUNK_SHIFT)
        k_chunk = jnp.right_shift(lax.broadcasted_iota(jnp.int32, shape, 1), CHUNK_SHIFT)
        s = jnp.where(k_chunk <= q_chunk, s, -jnp.inf)
    e = jnp.exp(s - jnp.max(s, axis=-1, keepdims=True))
    return e, jnp.sum(e, axis=-1, keepdims=True)


def _per_prefix(work, i, tq, sq, sk, causal):
    if not causal:
        work(sk)
        return
    assert sq == sk and tq % CHUNK == 0
    for j in range(sq // tq):
        @pl.when(i == j)
        def _(j=j):
            work((j + 1) * tq)


def _attn_specs(tq, sk, dq, dv, koff, kstride, voff, vstride):
    return [pl.BlockSpec((tq, dq), lambda h, i: (i, h)),
            pl.BlockSpec((sk, dq), lambda h, i: (0, koff + kstride * h)),
            pl.BlockSpec((sk, dv), lambda h, i: (0, voff + vstride * h))]


def attn_fwd(name, q, k, v, *, heads, dq, dv, koff, kstride, voff, vstride, scale, causal, carry=None):
    sq, sk = q.shape[0], k.shape[0]
    tq = min(sq, 256)

    def body(q_ref, k_ref, v_ref, o_ref):
        i = pl.program_id(1)

        def work(keys):
            e, l = _probs(q_ref[...].astype(BF16), k_ref[0:keys, :].astype(BF16), i, tq, scale, causal)
            o = jnp.dot(e.astype(BF16), v_ref[0:keys, :].astype(BF16), preferred_element_type=F32)
            o_ref[...] = o / l

        _per_prefix(work, i, tq, sq, sk, causal)

    return _call(name, body, grid=(heads, sq // tq),
                 in_specs=_attn_specs(tq, sk, dq, dv, koff, kstride, voff, vstride),
                 out_specs=[pl.BlockSpec((tq, dv), lambda h, i: (i, h))],
                 out_shape=[jax.ShapeDtypeStruct((sq, heads * dv), F32)], args=[q, k, v], carry=carry)[0]


def attn_bwd(name, q, k, v, do, *, heads, dq, dv, koff, kstride, voff, vstride, scale, causal, carry=None):
    sq, sk = q.shape[0], k.shape[0]
    tq = min(sq, 256)

    def body(q_ref, k_ref, v_ref, do_ref, dq_ref, dk_ref, dv_ref):
        i = pl.program_id(1)

        @pl.when(i == 0)
        def _():
            dk_ref[...] = jnp.zeros_like(dk_ref)
            dv_ref[...] = jnp.zeros_like(dv_ref)

        def work(keys):
            qb, kb, vb = q_ref[...].astype(BF16), k_ref[0:keys, :].astype(BF16), v_ref[0:keys, :].astype(BF16)
            dob = do_ref[...].astype(BF16)
            e, l = _probs(qb, kb, i, tq, scale, causal)
            p = e / l
            dp = lax.dot_general(dob, vb, _DIMS['nt'], preferred_element_type=F32)
            ds = (p * (dp - jnp.sum(dp * p, axis=-1, keepdims=True)) * scale).astype(BF16)
            dv_ref[0:keys, :] += lax.dot_general(p.astype(BF16), dob, _DIMS['tn'], preferred_element_type=F32)
            dk_ref[0:keys, :] += lax.dot_general(ds, qb, _DIMS['tn'], preferred_element_type=F32)
            dq_ref[...] = jnp.dot(ds, kb, preferred_element_type=F32)

        _per_prefix(work, i, tq, sq, sk, causal)

    return _call(
        name, body, grid=(heads, sq // tq),
        in_specs=_attn_specs(tq, sk, dq, dv, koff, kstride, voff, vstride) + [pl.BlockSpec((tq, dv), lambda h, i: (i, h))],
        out_specs=[pl.BlockSpec((tq, dq), lambda h, i: (i, h)), pl.BlockSpec((sk, dq), lambda h, i: (0, h)),
                   pl.BlockSpec((sk, dv), lambda h, i: (0, h))],
        out_shape=[jax.ShapeDtypeStruct((sq, heads * dq), F32), jax.ShapeDtypeStruct((sk, heads * dq), F32),
                   jax.ShapeDtypeStruct((sk, heads * dv), F32)],
        args=[q, k, v, do], carry=carry)


def _hg_chunk(q, k, g, v, state):
    c = q.shape[0]
    row = lax.broadcasted_iota(jnp.int32, (c, c), 0)
    col = lax.broadcasted_iota(jnp.int32, (c, c), 1)
    tril = (col <= row).astype(F32)
    b = jnp.dot(tril, g, precision=lax.Precision.HIGHEST, preferred_element_type=F32)
    rows = lax.broadcasted_iota(jnp.int32, (c, 1), 0)
    o = jnp.dot((q * jnp.exp(b)).astype(BF16), state.astype(BF16), preferred_element_type=F32)
    t3 = lax.broadcasted_iota(jnp.int32, (SUB, SUB, 1), 0)
    s3 = lax.broadcasted_iota(jnp.int32, (SUB, SUB, 1), 1)
    parts = []
    for n in range(c // SUB):
        lo = n * SUB
        qn, kn, bn, vn = q[lo:lo + SUB], k[lo:lo + SUB], b[lo:lo + SUB], v[lo:lo + SUB]
        decay = jnp.exp(jnp.where(s3 <= t3, bn[:, None, :] - bn[None, :, :], -jnp.inf))
        sc = jnp.sum(qn[:, None, :] * kn[None, :, :] * decay, axis=-1)
        on = jnp.dot(sc.astype(BF16), vn.astype(BF16), preferred_element_type=F32)
        if n > 0:
            ref = jnp.sum(jnp.where(rows == lo - 1, b, 0.0), axis=0, keepdims=True)
            qd = (qn * jnp.exp(bn - ref)).astype(BF16)
            kd = (k[:lo] * jnp.exp(ref - b[:lo])).astype(BF16)
            so = lax.dot_general(qd, kd, _DIMS['nt'], preferred_element_type=F32)
            on = on + jnp.dot(so.astype(BF16), v[:lo].astype(BF16), preferred_element_type=F32)
        parts.append(on)
    o = o + jnp.concatenate(parts, axis=0)
    b_last = jnp.sum(g, axis=0, keepdims=True)
    ones = jnp.ones((c, 1), F32)
    b_last_col = lax.dot_general(g, ones, _DIMS['tn'], precision=lax.Precision.HIGHEST, preferred_element_type=F32)
    kd = (k * jnp.exp(b_last - b)).astype(BF16)
    new_state = jnp.exp(b_last_col) * state + lax.dot_general(kd, v.astype(BF16), _DIMS['tn'],
                                                              preferred_element_type=F32)
    return o, new_state


def hg_scan_fwd(name, q, k, g, u, v_off, carry=None):
    s = q.shape[0]
    n = s // CHUNK

    def body(q_ref, k_ref, g_ref, v_ref, o_ref, st_ref, state):
        @pl.when(pl.program_id(1) == 0)
        def _():
            state[...] = jnp.zeros_like(state)

        for j in range(HG_GROUP):
            cols = slice(j * HEAD_W, (j + 1) * HEAD_W)
            st = state[j]
            st_ref[j] = st
            o, new = _hg_chunk(q_ref[:, cols], k_ref[:, cols], g_ref[:, cols], v_ref[:, cols], st)
            o_ref[:, cols] = o
            state[j] = new

    wide = HG_GROUP * HEAD_W
    blk = pl.BlockSpec((CHUNK, wide), lambda h, c: (c, h))
    return _call(
        name, body, grid=(HG_HEADS // HG_GROUP, n),
        in_specs=[blk, blk, blk, pl.BlockSpec((CHUNK, wide), lambda h, c: (c, v_off // HG_GROUP + h))],
        out_specs=[blk, pl.BlockSpec((HG_GROUP, None, HEAD_W, HEAD_W), lambda h, c: (h, c, 0, 0))],
        out_shape=[jax.ShapeDtypeStruct((s, HG_HEADS * HEAD_W), F32),
                   jax.ShapeDtypeStruct((HG_HEADS, n, HEAD_W, HEAD_W), F32)],
        scratch=[pltpu.VMEM((HG_GROUP, HEAD_W, HEAD_W), F32)], args=[q, k, g, u], carry=carry)


def hg_scan_bwd(name, q, k, g, u, v_off, states, do, carry=None):
    s = q.shape[0]
    n = s // CHUNK

    def body(q_ref, k_ref, g_ref, v_ref, st_ref, do_ref, dq_ref, dk_ref, dg_ref, dv_ref, dstate):
        @pl.when(pl.program_id(1) == 0)
        def _():
            dstate[...] = jnp.zeros_like(dstate)

        for j in range(HG_GROUP):
            cols = slice(j * HEAD_W, (j + 1) * HEAD_W)
            _, vjp = jax.vjp(_hg_chunk, q_ref[:, cols], k_ref[:, cols], g_ref[:, cols], v_ref[:, cols], st_ref[j])
            dq, dk, dg, dv, dst = vjp((do_ref[:, cols], dstate[j]))
            dq_ref[:, cols] = dq
            dk_ref[:, cols] = dk
            dg_ref[:, cols] = dg
            dv_ref[:, cols] = dv
            dstate[j] = dst

    wide = HG_GROUP * HEAD_W
    blk = pl.BlockSpec((CHUNK, wide), lambda h, c: (n - 1 - c, h))
    out = jax.ShapeDtypeStruct((s, HG_HEADS * HEAD_W), F32)
    return _call(
        name, body, grid=(HG_HEADS // HG_GROUP, n),
        in_specs=[blk, blk, blk, pl.BlockSpec((CHUNK, wide), lambda h, c: (n - 1 - c, v_off // HG_GROUP + h)),
                  pl.BlockSpec((HG_GROUP, None, HEAD_W, HEAD_W), lambda h, c: (h, n - 1 - c, 0, 0)), blk],
        out_specs=[blk, blk, blk, blk], out_shape=[out, out, out, out],
        scratch=[pltpu.VMEM((HG_GROUP, HEAD_W, HEAD_W), F32)], args=[q, k, g, u, states, do], carry=carry)


def pair_sum(name, blocks, theirs):
    _, rows, cols = theirs.shape
    tile = next((t for t in (1024, 512, 256, 128, 64, 32, 16) if rows % t == 0 and t * cols <= (1 << 18)), rows)

    def body(a_ref, b_ref, o_ref):
        mine = jnp.where(lax.axis_index("c") == 0, a_ref[0].astype(F32), a_ref[1].astype(F32))
        o_ref[...] = (mine + b_ref[...].astype(F32)).astype(o_ref.dtype)

    blk = pl.BlockSpec((None, tile, cols), lambda q, i: (q, i, 0))
    return _call(name, body, grid=(N_CHIP, rows // tile),
                 in_specs=[pl.BlockSpec((None, 2, tile, cols), lambda q, i: (q, 0, i, 0)), blk], out_specs=[blk],
                 out_shape=[jax.ShapeDtypeStruct(theirs.shape, theirs.dtype)],
                 args=[blocks.reshape(N_CHIP, 2, rows, cols), theirs])[0]


def adamw(name, landed, w, m, v):
    rows, cols = w.shape
    slots = landed.shape[0]
    tile = next((t for t in (512, 256, 128, 64, 32, 16, 8) if rows % t == 0 and t * cols <= (1 << 17)), rows)

    def body(l_ref, w_ref, m_ref, v_ref, g_ref, d_ref, nm_ref, nv_ref):
        gv = l_ref[0].astype(F32)
        for s in range(1, slots):
            gv = gv + l_ref[s].astype(F32)
        nm = ADAM_B1 * m_ref[...] + (1.0 - ADAM_B1) * gv
        nv = ADAM_B2 * v_ref[...] + (1.0 - ADAM_B2) * jnp.square(gv)
        m_hat = nm / (1.0 - ADAM_B1 ** ADAM_STEP)
        v_hat = nv / (1.0 - ADAM_B2 ** ADAM_STEP)
        g_ref[...] = gv
        d_ref[...] = -ADAM_LR * (m_hat / (jnp.sqrt(v_hat) + ADAM_EPS) + ADAM_WD * w_ref[...])
        nm_ref[...] = nm
        nv_ref[...] = nv

    blk = pl.BlockSpec((tile, cols), lambda i: (i, 0))
    out = jax.ShapeDtypeStruct((rows, cols), F32)
    return _call(name, body, grid=(rows // tile,),
                 in_specs=[pl.BlockSpec((slots, tile, cols), lambda i: (0, i, 0)), blk, blk, blk],
                 out_specs=[blk] * 4, out_shape=[out] * 4, args=[landed, w, m, v])


def _swap_halves(pe):
    half = QK_ROPE // 2
    return jnp.concatenate([-pe[..., half:], pe[..., :half]], axis=-1)


def _unswap_halves(dsw):
    half = QK_ROPE // 2
    return jnp.concatenate([dsw[..., half:], -dsw[..., :half]], axis=-1)


def _w_in_ext(w):
    kpe = w[:, 9216:9280]
    z = jnp.zeros((w.shape[0], HEAD_W - QK_ROPE), w.dtype)
    pad = jnp.zeros((w.shape[0], U_PAD), w.dtype)
    return jnp.concatenate([w[:, :9216], w[:, 9280:], kpe, z, _swap_halves(kpe), z, pad], axis=1)


def _w_in_grad(d):
    dkpe = d[:, 13312:13376] + _unswap_halves(d[:, 13440:13504])
    return jnp.concatenate([d[:, :9216], dkpe, d[:, 9216:13312]], axis=1)


def _w_q_ext(w):
    w3 = w.reshape(Q_LORA, MLA_HEADS, MLA_QK)
    pe = w3[:, :, HEAD_W:]
    z = jnp.zeros((Q_LORA, MLA_HEADS, HEAD_W - QK_ROPE), w.dtype)
    wide = MLA_HEADS * HEAD_W
    return jnp.concatenate([w3[:, :, :HEAD_W].reshape(Q_LORA, wide),
                            jnp.concatenate([pe, z], axis=2).reshape(Q_LORA, wide),
                            jnp.concatenate([_swap_halves(pe), z], axis=2).reshape(Q_LORA, wide)], axis=1)


def _w_q_grad(d):
    wide = MLA_HEADS * HEAD_W
    d3 = [d[:, n * wide:(n + 1) * wide].reshape(Q_LORA, MLA_HEADS, HEAD_W) for n in range(3)]
    dpe = d3[1][:, :, :QK_ROPE] + _unswap_halves(d3[2][:, :, :QK_ROPE])
    return jnp.concatenate([d3[0], dpe], axis=2).reshape(Q_LORA, MLA_HEADS * MLA_QK)


def _hg_prep(f_raw, q_hg, logits):
    lb = jax.nn.softmax(logits, axis=0)[0:1, :]
    log_f = jnp.logaddexp(jnp.log(lb), jnp.log1p(-lb) + jax.nn.log_sigmoid(f_raw))
    k_in = (1.0 - lb) * jax.nn.sigmoid(-f_raw)
    return log_f, k_in, jax.nn.silu(q_hg)


def _rope(q_nope, q_pe, q_sw, k_nope, k_pe, k_sw, cos, sin):
    qf = jnp.concatenate([q_nope, q_pe * cos + q_sw * sin], axis=1)
    kf = jnp.concatenate([k_nope, k_pe * cos + k_sw * sin], axis=1)
    return qf, kf


def _step(a):
    x, mem, target = a['x'][0], a['mem'][0], a['loss_target'][0]
    s, d = x.shape
    nm = mem.shape[0]
    ff = N_DEV * a['ffn1_w_gate'].shape[-1]
    cs = ff // N_DEV
    tr = min(s, 128)
    th = min(s, 1024)
    ta = 512

    bf = {n: a[n][0].astype(BF16) for n in BIG}
    gat, full = {}, {}

    my_chip = 2 * lax.axis_index("x") + lax.axis_index("y")
    me = 2 * my_chip + lax.axis_index("c")

    def gathered(names, carry):
        for n, g8 in zip(names or [], carry.results if carry else []):
            r, c = bf[n].shape
            g8 = lax.dynamic_update_index_in_dim(g8, bf[n], me, 0)
            gat[n] = g8
            if not n.startswith('ffn'):
                full[n] = g8.transpose(1, 0, 2).reshape(r, N_DEV * c) if BIG[n] else g8.reshape(N_DEV * r, c)

    xa_names = ['xa_w_q', 'xa_w_k', 'xa_w_v', 'xa_w_o']
    ffn2_names = ['ffn2_w_gate', 'ffn2_w_up', 'ffn2_w_down']
    first = Gather([bf['ffn1_w_gate']])
    run_alone("gather_first", first)
    gathered(['ffn1_w_gate'], first)

    inv_freq = 1.0 / (ROPE_THETA ** (jnp.arange(0, QK_ROPE, 2, dtype=F32) / QK_ROPE))
    ang = a['positions'][0].astype(F32)[:, None] * inv_freq
    zero = jnp.zeros((s, HEAD_W - QK_ROPE), F32)
    cos = jnp.concatenate([jnp.cos(ang), jnp.cos(ang), zero], axis=1)
    sin = jnp.concatenate([jnp.sin(ang), jnp.sin(ang), zero], axis=1)

    gs = {}
    gb = {}

    def by_rows(g):
        return g.reshape(N_DEV, g.shape[0] // N_DEV, g.shape[1])

    def by_cols(g):
        return g.reshape(g.shape[0], N_DEV, g.shape[1] // N_DEV).transpose(1, 0, 2)

    pre_fn = lambda xv, g: (_rms(xv, g),)
    pre_res_fn = lambda xv, g: (_rms(xv, g), xv)

    def pre_norm(tag, x_in, g):
        return rowmap(tag + "_pre", pre_fn, [R_(x_in), P_(g)], [(BF16, x_in.shape[1], 0)], tile=tr)[0]

    def pre_norm_bwd(tag, x_in, g, dh, d_out, carry=None):
        return rowmap(tag + "_pre_bwd", pre_res_fn, [R_(x_in), P_(g)], tile=tr, cts=[R_(dh), R_(d_out)],
                      wrt=[0, 1], gdt=[F32, F32], carry=carry)

    def post_fn(weight):
        return lambda xv, y, g: (xv + weight * _rms(y, g),)

    def post_norm(tag, x_in, y, g, weight):
        return rowmap(tag + "_post", post_fn(weight), [R_(x_in), R_(y), P_(g)], [(F32, d, 0)], tile=tr)[0]

    def post_norm_bwd(tag, x_in, y, g, weight, d_out):
        return rowmap(tag + "_post_bwd", post_fn(weight), [R_(x_in), R_(y), P_(g)], tile=tr, cts=[R_(d_out)],
                      wrt=[1, 2], gdt=[BF16, F32])

    act_fn = lambda av, bv: (jax.nn.silu(av) * bv,)

    def fetch(plan, key):
        names = plan.get(key)
        return (names, Gather([bf[n] for n in names])) if names else (None, None)

    def send(plan, key):
        acts = plan.get(key)
        if not acts:
            return None, None
        parts = []
        for kind, names in acts:
            if kind == 'spread':
                for n in names:
                    chip_sums[n] = pair_sum("pair_" + n, gb[n], halves[n])
            parts.append(Halve([gb[n] for n in names]) if kind == 'halve' else Spread([chip_sums[n] for n in names]))
        return acts, Joint(parts)

    def ffn_fwd(tag, x_in, plan):
        h = pre_norm(tag, x_in, a[tag + '_pre_g'])
        names, carry = fetch(plan, 'gate')
        av = mm(tag + "_gate", h, gat[tag + '_w_gate'], 'nn', jdim='n', carry=carry)
        gathered(names, carry)
        names, carry = fetch(plan, 'up')
        bv = mm(tag + "_up", h, gat[tag + '_w_up'], 'nn', jdim='n', carry=carry)
        gathered(names, carry)
        a2, b2 = av.reshape(N_DEV * s, cs), bv.reshape(N_DEV * s, cs)
        z = rowmap(tag + "_act", act_fn, [R_(a2), R_(b2)], [(BF16, cs, 0)], tile=ta)[0].reshape(N_DEV, s, cs)
        names, carry = fetch(plan, 'dn')
        y = mm(tag + "_dn", z, gat[tag + '_w_down'], 'nn', jdim='k', carry=carry)
        gathered(names, carry)
        return post_norm(tag, x_in, y, a[tag + '_post_g'], 0.5), (x_in, h, a2, b2, z, y)

    def ffn_bwd(tag, res, d_out, plan):
        x_in, h, a2, b2, z, y = res
        dy, gs[tag + '_post_g'] = post_norm_bwd(tag, x_in, y, a[tag + '_post_g'], 0.5, d_out)
        acts, carry = send(plan, 'dn_dx')
        dz = mm(tag + "_dn_dx", dy, gat[tag + '_w_down'], 'nt', jdim='n', carry=carry)
        exchanged(acts, carry)
        gb[tag + '_w_down'] = mm(tag + "_dn_dw", z, dy, 'tn', jdim='m', out_dtype=BF16)
        acts, carry = send(plan, 'act_bwd')
        da, db = rowmap(tag + "_act_bwd", act_fn, [R_(a2), R_(b2)], tile=ta, cts=[R_(dz.reshape(N_DEV * s, cs))],
                        wrt=[0, 1], gdt=[BF16, BF16], carry=carry)
        exchanged(acts, carry)
        da, db = da.reshape(N_DEV, s, cs), db.reshape(N_DEV, s, cs)
        names, carry = send(plan, 'gu_dx')
        dh = mm(tag + "_gu_dx", [da, db], [gat[tag + '_w_gate'], gat[tag + '_w_up']], 'nt', jdim='k', carry=carry)
        exchanged(names, carry)
        gb[tag + '_w_gate'] = mm(tag + "_gate_dw", h, da, 'tn', jdim='n', out_dtype=BF16)
        names, carry = send(plan, 'up_dw')
        gb[tag + '_w_up'] = mm(tag + "_up_dw", h, db, 'tn', jdim='n', out_dtype=BF16, carry=carry)
        exchanged(names, carry)
        names, carry = send(plan, 'pre_bwd')
        d_in, gs[tag + '_pre_g'] = pre_norm_bwd(tag, x_in, a[tag + '_pre_g'], dh, d_out, carry)
        exchanged(names, carry)
        return d_in

    hg_out_fn = lambda o, og, g: (_rms(o, g) * jax.nn.silu(og),)
    mla_norm_fn = lambda cq, ckv, gq, gkv: (_rms(cq, gq), _rms(ckv, gkv))
    gate_fn = lambda ga, gb, ya, yb: (jax.nn.sigmoid(ga) * ya + jax.nn.sigmoid(gb) * yb,)
    mla = dict(heads=MLA_HEADS, dq=2 * HEAD_W, dv=HEAD_W, koff=0, kstride=1, voff=1, vstride=2,
               scale=MLA_QK ** -0.5, causal=True)

    def mix_fwd(x_in, plan):
        w_in = _w_in_ext(full['w_in'])
        h = pre_norm("mix", x_in, a['mix_pre_g'])
        names, carry = fetch(plan, 'in')
        u = mm("mix_in", h, w_in, 'nn', carry=carry)
        gathered(names, carry)
        w_q, w_kv = _w_q_ext(full['mla_w_q_up']), full['mla_w_kv_up']
        hg_ins = [R_(u, 2048, 1), R_(u, 2048, 0), P_(a['hgrn_lb_logits'])]
        log_f, k_in, q_in = rowmap("hg_prep", _hg_prep, hg_ins, [(F32, 2048, 0)] * 3, tile=tr)
        names, carry = fetch(plan, 'scan')
        o_a, states = hg_scan_fwd("hg_scan", q_in, k_in, log_f, u, 32, carry=carry)
        gathered(names, carry)
        out_ins = [R_(o_a, HEAD_W, 0, 1), R_(u, HEAD_W, 48, 1), P_(a['hg_norm_g'], HEAD_W, 0, 1)]
        oag = rowmap("hg_out", hg_out_fn, out_ins, [(BF16, HEAD_W, 1)], tile=th, ncol=HG_HEADS)[0]
        y_a = mm("mix_a", oag, full['w_branch_a'], 'nn')
        norm_ins = [R_(u, 512, 16), R_(u, 512, 17), P_(a['mla_q_norm_g']), P_(a['mla_kv_norm_g'])]
        cqn, ckvn = rowmap("mla_norm", mla_norm_fn, norm_ins, [(BF16, 512, 0)] * 2, tile=tr)
        q_all = mm("mla_qup", cqn, w_q, 'nn')
        kv = mm("mla_kvup", ckvn, w_kv, 'nn')
        rope_ins = [R_(q_all, HEAD_W, 0, 1), R_(q_all, HEAD_W, 16, 1), R_(q_all, HEAD_W, 32, 1),
                    R_(kv, HEAD_W, 0, 2), R_(u, HEAD_W, 104, 0), R_(u, HEAD_W, 105, 0), R_(cos), R_(sin)]
        qf, kf = rowmap("mla_rope", _rope, rope_ins, [(BF16, 2 * HEAD_W, 1)] * 2, tile=th, ncol=MLA_HEADS,
                        rows_inner=False)
        names, carry = fetch(plan, 'attn')
        o_b = attn_fwd("mla_attn", qf, kf, kv, carry=carry, **mla)
        gathered(names, carry)
        y_b = mm("mix_b", o_b, full['w_branch_b'], 'nn')
        gate_ins = [R_(u, 1024, 9, 1), R_(u, 1024, 11, 1), R_(y_a, 1024, 0, 1), R_(y_b, 1024, 0, 1)]
        y = rowmap("mix_gate", gate_fn, gate_ins, [(BF16, 1024, 1)], tile=tr, ncol=2)[0]
        yo = mm("mix_out", y, full['w_out'], 'nn')
        res = (x_in, h, hg_ins, q_in, k_in, log_f, u, states, out_ins, oag, norm_ins, cqn, ckvn, rope_ins, qf, kf,
               kv, o_b, gate_ins, y, yo, w_in, w_q, w_kv)
        return post_norm("mix", x_in, yo, a['mix_post_g'], 1.0), res

    def mix_bwd(res, d_out, plan):
        (x_in, h, hg_ins, q_in, k_in, log_f, u, states, out_ins, oag, norm_ins, cqn, ckvn, rope_ins, qf, kf, kv,
         o_b, gate_ins, y, yo, w_in, w_q, w_kv) = res
        dyo, gs['mix_post_g'] = post_norm_bwd("mix", x_in, yo, a['mix_post_g'], 1.0, d_out)
        dy = mm("mix_out_dx", dyo, full['w_out'], 'nt')
        gb['w_out'] = by_rows(mm("mix_out_dw", y, dyo, 'tn', out_dtype=BF16))
        dga, dgb, dya, dyb = rowmap("mix_gate_bwd", gate_fn, gate_ins, tile=tr, ncol=2, cts=[R_(dy, 1024, 0, 1)],
                                    wrt=[0, 1, 2, 3], gdt=[BF16] * 4)
        gb['w_branch_b'] = by_rows(mm("mix_b_dw", o_b, dyb, 'tn', out_dtype=BF16))
        do_b = mm("mix_b_dx", dyb, full['w_branch_b'], 'nt')
        names, carry = send(plan, 'attn_bwd')
        dqf, dkf, dv = attn_bwd("mla_attn_bwd", qf, kf, kv, do_b, carry=carry, **mla)
        exchanged(names, carry)
        dqn, dqp, dqs, dkn, dkpe, dksw = rowmap(
            "mla_rope_bwd", _rope, rope_ins, tile=th, ncol=MLA_HEADS, rows_inner=False,
            cts=[R_(dqf, 2 * HEAD_W, 0, 1), R_(dkf, 2 * HEAD_W, 0, 1)], wrt=[0, 1, 2, 3, 4, 5],
            gdt=[BF16, BF16, BF16, BF16, F32, F32])
        dq_all = jnp.concatenate([dqn, dqp, dqs], axis=1)
        dkv = jnp.concatenate([dkn.reshape(s, MLA_HEADS, HEAD_W), dv.astype(BF16).reshape(s, MLA_HEADS, HEAD_W)],
                              axis=2).reshape(s, 2 * MLA_HEADS * HEAD_W)
        gb['mla_w_q_up'] = by_cols(_w_q_grad(mm("mla_qup_dw", cqn, dq_all, 'tn', out_dtype=BF16)))
        dcqn = mm("mla_qup_dx", dq_all, w_q, 'nt')
        gb['mla_w_kv_up'] = by_cols(mm("mla_kvup_dw", ckvn, dkv, 'tn', out_dtype=BF16))
        dckvn = mm("mla_kvup_dx", dkv, w_kv, 'nt')
        dcq, dckv, gs['mla_q_norm_g'], gs['mla_kv_norm_g'] = rowmap(
            "mla_norm_bwd", mla_norm_fn, norm_ins, tile=tr, cts=[R_(dcqn), R_(dckvn)], wrt=[0, 1, 2, 3],
            gdt=[BF16, BF16, F32, F32])
        gb['w_branch_a'] = by_rows(mm("mix_a_dw", oag, dya, 'tn', out_dtype=BF16))
        doag = mm("mix_a_dx", dya, full['w_branch_a'], 'nt')
        acts, carry = send(plan, 'out_bwd')
        do_a, dog, gs['hg_norm_g'] = rowmap("hg_out_bwd", hg_out_fn, out_ins, tile=th, ncol=HG_HEADS,
                                            cts=[R_(doag, HEAD_W, 0, 1)], wrt=[0, 1, 2], gdt=[F32, BF16, F32],
                                            carry=carry)
        exchanged(acts, carry)
        names, carry = send(plan, 'scan_bwd')
        dq_in, dk_in, dlog_f, di = hg_scan_bwd("hg_scan_bwd", q_in, k_in, log_f, u, 32, states, do_a, carry=carry)
        exchanged(names, carry)
        df, dq_hg, gs['hgrn_lb_logits'] = rowmap("hg_prep_bwd", _hg_prep, hg_ins, tile=tr,
                                                 cts=[R_(dlog_f), R_(dk_in), R_(dq_in)], wrt=[0, 1, 2],
                                                 gdt=[BF16, BF16, F32])
        du = jnp.concatenate([dq_hg, df, di.astype(BF16), dog, dcq, dckv, dga, dgb, dkpe.astype(BF16),
                              dksw.astype(BF16), jnp.zeros((s, U_PAD), BF16)], axis=1)
        gb['w_in'] = by_cols(_w_in_grad(mm("mix_in_dw", h, du, 'tn', out_dtype=BF16)))
        names, carry = send(plan, 'in_dx')
        dh = mm("mix_in_dx", du, w_in, 'nt', carry=carry)
        exchanged(names, carry)
        d_in, gs['mix_pre_g'] = pre_norm_bwd("mix", x_in, a['mix_pre_g'], dh, d_out)
        return d_in

    xa = dict(heads=XA_HEADS, dq=HEAD_W, dv=HEAD_W, koff=0, kstride=1, voff=XA_HEADS, vstride=1,
              scale=HEAD_W ** -0.5, causal=False)
    tm_ = min(nm, 128)

    def xa_fwd(x_in):
        w_xkv = jnp.concatenate([full['xa_w_k'], full['xa_w_v']], axis=1)
        h = pre_norm("xa", x_in, a['xa_pre_g'])
        mn = rowmap("xa_mem", pre_fn, [R_(mem), P_(a['xa_mem_g'])], [(BF16, d, 0)], tile=tm_)[0]
        q = mm("xa_q", h, full['xa_w_q'], 'nn')
        kv = mm("xa_kv", mn, w_xkv, 'nn')
        o = attn_fwd("xa_attn", q, kv, kv, **xa)
        yo = mm("xa_o", o, full['xa_w_o'], 'nn')
        return post_norm("xa", x_in, yo, a['xa_post_g'], 1.0), (x_in, h, mn, q, kv, o, yo, w_xkv)

    def xa_bwd(res, d_out, plan):
        x_in, h, mn, q, kv, o, yo, w_xkv = res
        dyo, gs['xa_post_g'] = post_norm_bwd("xa", x_in, yo, a['xa_post_g'], 1.0, d_out)
        do = mm("xa_o_dx", dyo, full['xa_w_o'], 'nt')
        gb['xa_w_o'] = by_cols(mm("xa_o_dw", o, dyo, 'tn', out_dtype=BF16))
        dq, dk, dv = attn_bwd("xa_attn_bwd", q, kv, kv, do, **xa)
        dkv = jnp.concatenate([dk, dv], axis=1).astype(BF16)
        dw = mm("xa_kv_dw", mn, dkv, 'tn', out_dtype=BF16)
        gb['xa_w_k'], gb['xa_w_v'] = by_rows(dw[:, :XA_HEADS * HEAD_W]), by_rows(dw[:, XA_HEADS * HEAD_W:])
        dmn = mm("xa_kv_dx", dkv, w_xkv, 'nt')
        gs['xa_mem_g'] = rowmap("xa_mem_bwd", pre_fn, [R_(mem), P_(a['xa_mem_g'])], tile=tm_, cts=[R_(dmn)],
                                wrt=[1], gdt=[F32])[0]
        gb['xa_w_q'] = by_rows(mm("xa_q_dw", h, dq, 'tn', out_dtype=BF16))
        dh = mm("xa_q_dx", dq, full['xa_w_q'], 'nt')
        acts, carry = send(plan, 'pre_bwd')
        d_in, gs['xa_pre_g'] = pre_norm_bwd("xa", x_in, a['xa_pre_g'], dh, d_out, carry)
        exchanged(acts, carry)
        return d_in

    landed = {}

    halves = {}
    chip_sums = {}

    def exchanged(acts, carry):
        for (kind, names), part in zip(acts or [], carry.parts if carry else []):
            for n, got in zip(names, part.results):
                if kind == 'halve':
                    halves[n] = got
                else:
                    own = lax.dynamic_index_in_dim(chip_sums[n], my_chip, 0, keepdims=True)
                    landed[n] = lax.dynamic_update_slice_in_dim(got, own, my_chip, 0)

    x1, r1 = ffn_fwd('ffn1', x, {'gate': ['ffn1_w_up'], 'up': ['ffn1_w_down'], 'dn': ['w_in']})
    x2, r2 = mix_fwd(x1, {'in': ['mla_w_q_up', 'mla_w_kv_up', 'w_branch_a', 'w_branch_b', 'w_out'],
                          'scan': xa_names + ['ffn2_w_gate'], 'attn': ['ffn2_w_up']})
    x3, r3 = xa_fwd(x2)
    x4, r4 = ffn_fwd('ffn2', x3, {'gate': ['ffn2_w_down']})

    def loss_fn(y, t):
        diff = y - t
        return diff * (1.0 / d), jnp.mean(diff * diff, axis=-1, keepdims=True)

    d4, row_loss = rowmap("loss", loss_fn, [R_(x4), R_(target)], [(F32, d, 0), (F32, 1, 0)], tile=tr)
    loss = lax.psum(0.5 * jnp.sum(row_loss), ("x", "y", "c"))
    late = ['mla_w_q_up', 'mla_w_kv_up', 'w_branch_a']
    d3 = ffn_bwd('ffn2', r4, d4, {'act_bwd': [('halve', ['ffn2_w_down'])], 'gu_dx': [('spread', ['ffn2_w_down'])],
                                  'up_dw': [('halve', ['ffn2_w_gate'])], 'pre_bwd': [('halve', ['ffn2_w_up'])]})
    d2 = xa_bwd(r3, d3, {'pre_bwd': [('halve', xa_names)]})
    d1 = mix_bwd(r2, d2, {'attn_bwd': [('spread', ['ffn2_w_gate', 'ffn2_w_up']), ('halve', ['w_out', 'w_branch_b'])],
                          'out_bwd': [('halve', late)],
                          'scan_bwd': [('spread', xa_names + ['w_out', 'w_branch_b'] + late)],
                          'in_dx': [('halve', ['w_in'])]})
    grad_x = ffn_bwd('ffn1', r1, d1, {'dn_dx': [('spread', ['w_in'])], 'act_bwd': [('halve', ['ffn1_w_down'])],
                                      'gu_dx': [('spread', ['ffn1_w_down'])], 'up_dw': [('halve', ['ffn1_w_gate'])],
                                      'pre_bwd': [('spread', ['ffn1_w_gate']), ('halve', ['ffn1_w_up'])]})
    acts, carry = send({'end': [('spread', ['ffn1_w_up'])]}, 'end')
    run_alone("spread_last", carry)
    exchanged(acts, carry)

    def pack_small(vals):
        flat = jnp.concatenate([vals[n].reshape(-1) for n in SMALL])
        rows = -(-flat.shape[0] // PACK_W)
        rows = -(-rows // 8) * 8
        return jnp.pad(flat, (0, rows * PACK_W - flat.shape[0])).reshape(rows, PACK_W)

    def unpack_small(buf):
        flat, out, at = buf.reshape(-1), {}, 0
        for n in SMALL:
            size = a[n].shape[0] * a[n].shape[1]
            out[n] = flat[at:at + size].reshape(a[n].shape)
            at += size
        return out

    g_small = pack_small(gs)
    small = Gather([g_small])
    run_alone("gather_g_small", small)
    g_small = lax.dynamic_update_index_in_dim(small.results[0], g_small, me, 0)

    grads, delta, new_m, new_v = {}, {}, {}, {}
    packs = adamw("adamw_small", g_small, pack_small(a), pack_small({n: a['m_' + n] for n in SMALL}),
                  pack_small({n: a['v_' + n] for n in SMALL}))
    for dst, buf in zip((grads, delta, new_m, new_v), packs):
        dst.update(unpack_small(buf))
    for n in BIG:
        outs = adamw("adamw_" + n, landed[n], a[n][0], a['m_' + n][0], a['v_' + n][0])
        grads[n], delta[n], new_m[n], new_v[n] = (t.reshape(a[n].shape) for t in outs)

    return (loss, grad_x[None], *[grads[n] for n in WEIGHTS], *[delta[n] for n in WEIGHTS],
            *[new_m[n] for n in WEIGHTS], *[new_v[n] for n in WEIGHTS])


def kernel(x, mem, positions, hgrn_lb_logits, ffn1_pre_g, ffn1_w_gate, ffn1_w_up, ffn1_w_down, ffn1_post_g, mix_pre_g, w_in, hg_norm_g, mla_q_norm_g, mla_w_q_up, mla_kv_norm_g, mla_w_kv_up, w_branch_a, w_branch_b, w_out, mix_post_g, xa_pre_g, xa_mem_g, xa_w_q, xa_w_k, xa_w_v, xa_w_o, xa_post_g, ffn2_pre_g, ffn2_w_gate, ffn2_w_up, ffn2_w_down, ffn2_post_g, loss_target, m_hgrn_lb_logits, m_ffn1_pre_g, m_ffn1_w_gate, m_ffn1_w_up, m_ffn1_w_down, m_ffn1_post_g, m_mix_pre_g, m_w_in, m_hg_norm_g, m_mla_q_norm_g, m_mla_w_q_up, m_mla_kv_norm_g, m_mla_w_kv_up, m_w_branch_a, m_w_branch_b, m_w_out, m_mix_post_g, m_xa_pre_g, m_xa_mem_g, m_xa_w_q, m_xa_w_k, m_xa_w_v, m_xa_w_o, m_xa_post_g, m_ffn2_pre_g, m_ffn2_w_gate, m_ffn2_w_up, m_ffn2_w_down, m_ffn2_post_g, v_hgrn_lb_logits, v_ffn1_pre_g, v_ffn1_w_gate, v_ffn1_w_up, v_ffn1_w_down, v_ffn1_post_g, v_mix_pre_g, v_w_in, v_hg_norm_g, v_mla_q_norm_g, v_mla_w_q_up, v_mla_kv_norm_g, v_mla_w_kv_up, v_w_branch_a, v_w_branch_b, v_w_out, v_mix_post_g, v_xa_pre_g, v_xa_mem_g, v_xa_w_q, v_xa_w_k, v_xa_w_v, v_xa_w_o, v_xa_post_g, v_ffn2_pre_g, v_ffn2_w_gate, v_ffn2_w_up, v_ffn2_w_down, v_ffn2_post_g):
    return _step(dict(locals()))
```

```python
import functools

import jax
import jax.numpy as jnp
from jax import lax
from jax.experimental import pallas as pl
from jax.experimental.pallas import tpu as pltpu

F32 = jnp.float32
BF16 = jnp.bfloat16

N_DEV = 8
D_MODEL = 2048
CHUNK = 64
CHUNK_SHIFT = 6
SUB = 16
HG_HEADS = 16
HG_GROUP = 8
HEAD_W = 128
MLA_HEADS = 16
Q_LORA = 512
KV_LORA = 512
QK_ROPE = 64
MLA_QK = 192
XA_HEADS = 4
ROPE_THETA = 10000.0
EPS = 1e-6
PACK_W = 1024
VMEM_LIMIT = 56 * 1024 * 1024

ADAM_LR = 0.001
ADAM_B1 = 0.9
ADAM_B2 = 0.999
ADAM_EPS = 1e-08
ADAM_WD = 0.01
ADAM_STEP = 10

U_PAD = 256

WEIGHTS = ['hgrn_lb_logits', 'ffn1_pre_g', 'ffn1_w_gate', 'ffn1_w_up', 'ffn1_w_down', 'ffn1_post_g', 'mix_pre_g',
           'w_in', 'hg_norm_g', 'mla_q_norm_g', 'mla_w_q_up', 'mla_kv_norm_g', 'mla_w_kv_up', 'w_branch_a',
           'w_branch_b', 'w_out', 'mix_post_g', 'xa_pre_g', 'xa_mem_g', 'xa_w_q', 'xa_w_k', 'xa_w_v', 'xa_w_o',
           'xa_post_g', 'ffn2_pre_g', 'ffn2_w_gate', 'ffn2_w_up', 'ffn2_w_down', 'ffn2_post_g']
BIG = {'ffn1_w_gate': True, 'ffn1_w_up': True, 'ffn1_w_down': False, 'w_in': True, 'mla_w_q_up': True,
       'mla_w_kv_up': True, 'w_branch_a': False, 'w_branch_b': False, 'w_out': False, 'xa_w_q': False,
       'xa_w_k': False, 'xa_w_v': False, 'xa_w_o': True, 'ffn2_w_gate': True, 'ffn2_w_up': True,
       'ffn2_w_down': False}
SMALL = [n for n in WEIGHTS if n not in BIG]


def _cparams(**kw):
    return pltpu.CompilerParams(vmem_limit_bytes=VMEM_LIMIT, **kw)


def _pick(dim, cands):
    for c in cands:
        if dim % c == 0:
            return c
    return dim


def _place():
    return lax.axis_index("x"), lax.axis_index("y"), lax.axis_index("c")


def _slot(px, py, pc):
    return 4 * px + 2 * py + pc


class Gather:
    def __init__(self, tensors):
        self.operands = list(tensors)
        self.out_shape = [jax.ShapeDtypeStruct((N_DEV,) + t.shape, t.dtype) for t in tensors]
        n = len(tensors)
        self.scratch = [pltpu.SemaphoreType.DMA((n, N_DEV - 1)), pltpu.SemaphoreType.DMA((n, N_DEV - 1))]

    def _copies(self, t, x_ref, out_ref, send, recv):
        x, y, c = _place()
        chips = [(1 - x, y), (x, 1 - y), (1 - x, 1 - y)]

        def copy(k, block, to, src=None):
            rows = out_ref.at[_slot(*block)]
            return pltpu.make_async_remote_copy(src_ref=rows if src is None else src, dst_ref=rows,
                                                send_sem=send.at[t, k], recv_sem=recv.at[t, k], device_id=to,
                                                device_id_type=pl.DeviceIdType.MESH)

        return (x, y, c), chips, copy

    def start(self, ins, outs, scr):
        send, recv = scr
        for t, (x_ref, out_ref) in enumerate(zip(ins, outs)):
            (x, y, c), chips, copy = self._copies(t, x_ref, out_ref, send, recv)
            copy(0, (x, y, c), (x, y, 1 - c), src=x_ref).start()
            for j, chip in enumerate(chips):
                copy(1 + j, (x, y, c), (*chip, c), src=x_ref).start()

    def finish(self, ins, outs, scr):
        send, recv = scr
        for t, (x_ref, out_ref) in enumerate(zip(ins, outs)):
            (x, y, c), chips, copy = self._copies(t, x_ref, out_ref, send, recv)
            passed = [copy(4 + j, (*chip, c), (x, y, 1 - c)) for j, chip in enumerate(chips)]
            for j, chip in enumerate(chips):
                copy(1 + j, (*chip, c), (x, y, c)).wait_recv()
                passed[j].start()
            copy(0, (x, y, 1 - c), (x, y, c)).wait_recv()
            for j, chip in enumerate(chips):
                copy(4 + j, (*chip, 1 - c), (x, y, c)).wait_recv()
            copy(0, (x, y, c), (x, y, 1 - c), src=x_ref).wait_send()
            for j, chip in enumerate(chips):
                copy(1 + j, (x, y, c), (*chip, c), src=x_ref).wait_send()
                passed[j].wait_send()

    def set_results(self, res):
        self.results = list(res)


N_CHIP = N_DEV // 2


class Halve:
    def __init__(self, tensors):
        self.operands = list(tensors)
        self.out_shape = [jax.ShapeDtypeStruct((N_CHIP,) + t.shape[1:], t.dtype) for t in tensors]
        n = len(tensors)
        self.scratch = [pltpu.SemaphoreType.DMA((n, N_CHIP)), pltpu.SemaphoreType.DMA((n, N_CHIP))]

    def _copies(self, t, x_ref, theirs_ref, scr):
        send, recv = scr
        x, y, c = _place()
        return [pltpu.make_async_remote_copy(
            src_ref=x_ref.at[2 * q + 1 - c], dst_ref=theirs_ref.at[q], send_sem=send.at[t, q],
            recv_sem=recv.at[t, q], device_id=(x, y, 1 - c), device_id_type=pl.DeviceIdType.MESH)
            for q in range(N_CHIP)]

    def start(self, ins, outs, scr):
        for t, (x_ref, theirs_ref) in enumerate(zip(ins, outs)):
            for give in self._copies(t, x_ref, theirs_ref, scr):
                give.start()

    def finish(self, ins, outs, scr):
        for t, (x_ref, theirs_ref) in enumerate(zip(ins, outs)):
            for give in self._copies(t, x_ref, theirs_ref, scr):
                give.wait_recv()
                give.wait_send()

    def set_results(self, res):
        self.results = list(res)


class Spread:
    def __init__(self, tensors):
        self.operands = list(tensors)
        self.out_shape = [jax.ShapeDtypeStruct(t.shape, t.dtype) for t in tensors]
        n = len(tensors)
        self.scratch = [pltpu.SemaphoreType.DMA((n, N_CHIP - 1)), pltpu.SemaphoreType.DMA((n, N_CHIP - 1))]

    def _copies(self, t, y_ref, out_ref, scr, outgoing):
        send, recv = scr
        x, y, c = _place()
        copies = []
        for k in range(1, N_CHIP):
            px, py = x ^ (k >> 1), y ^ (k & 1)
            copies.append(pltpu.make_async_remote_copy(
                src_ref=y_ref.at[2 * px + py], dst_ref=out_ref.at[2 * x + y if outgoing else 2 * px + py],
                send_sem=send.at[t, k - 1], recv_sem=recv.at[t, k - 1], device_id=(px, py, c),
                device_id_type=pl.DeviceIdType.MESH))
        return copies

    def start(self, ins, outs, scr):
        for t, (y_ref, out_ref) in enumerate(zip(ins, outs)):
            for give in self._copies(t, y_ref, out_ref, scr, True):
                give.start()

    def finish(self, ins, outs, scr):
        for t, (y_ref, out_ref) in enumerate(zip(ins, outs)):
            for take in self._copies(t, y_ref, out_ref, scr, False):
                take.wait_recv()
            for give in self._copies(t, y_ref, out_ref, scr, True):
                give.wait_send()

    def set_results(self, res):
        self.results = list(res)


class Joint:
    def __init__(self, parts):
        self.parts = list(parts)
        self.operands = [o for p in parts for o in p.operands]
        self.out_shape = [o for p in parts for o in p.out_shape]
        self.scratch = [o for p in parts for o in p.scratch]

    def _split(self, ins, outs, scr):
        i = o = s = 0
        for p in self.parts:
            ni, no, ns = len(p.operands), len(p.out_shape), len(p.scratch)
            yield p, ins[i:i + ni], outs[o:o + no], scr[s:s + ns]
            i, o, s = i + ni, o + no, s + ns

    def start(self, ins, outs, scr):
        for p, a, b, c in self._split(ins, outs, scr):
            p.start(a, b, c)

    def finish(self, ins, outs, scr):
        for p, a, b, c in self._split(ins, outs, scr):
            p.finish(a, b, c)

    def set_results(self, res):
        for p, _, part, _ in self._split([], list(res), []):
            p.set_results(part)


_HBM = pl.BlockSpec(memory_space=pltpu.HBM)


def run_alone(name, carry):
    n_in, n_out = len(carry.operands), len(carry.out_shape)

    def body(*refs):
        ins, outs, scr = refs[:n_in], refs[n_in:n_in + n_out], refs[n_in + n_out:]
        carry.start(ins, outs, scr)
        carry.finish(ins, outs, scr)

    res = pl.pallas_call(body, name=name, in_specs=[_HBM] * n_in, out_specs=[_HBM] * n_out,
                         out_shape=carry.out_shape, scratch_shapes=carry.scratch)(*carry.operands)
    carry.set_results(list(res))


def _call(name, body, *, grid, in_specs, out_specs, out_shape, args, scratch=(), carry=None):
    in_specs, out_specs, out_shape, scratch = list(in_specs), list(out_specs), list(out_shape), list(scratch)
    if carry is None:
        return list(pl.pallas_call(body, name=name, grid=grid, in_specs=in_specs, out_specs=out_specs,
                                   out_shape=out_shape, scratch_shapes=scratch, compiler_params=_cparams())(*args))
    n_in, n_out, n_scr = len(in_specs), len(out_shape), len(scratch)
    c_in, c_out = len(carry.operands), len(carry.out_shape)

    def wrapped(*refs):
        ins, c_ins = refs[:n_in], refs[n_in:n_in + c_in]
        at = n_in + c_in
        outs, c_outs = refs[at:at + n_out], refs[at + n_out:at + n_out + c_out]
        at += n_out + c_out
        scr, c_scr = refs[at:at + n_scr], refs[at + n_scr:]
        first = functools.reduce(jnp.logical_and, [pl.program_id(d) == 0 for d in range(len(grid))])
        last = functools.reduce(jnp.logical_and, [pl.program_id(d) == grid[d] - 1 for d in range(len(grid))])

        @pl.when(first)
        def _():
            carry.start(c_ins, c_outs, c_scr)

        body(*ins, *outs, *scr)

        @pl.when(last)
        def _():
            carry.finish(c_ins, c_outs, c_scr)

    res = pl.pallas_call(
        wrapped, name=name, grid=grid, in_specs=in_specs + [_HBM] * c_in, out_specs=out_specs + [_HBM] * c_out,
        out_shape=out_shape + carry.out_shape, scratch_shapes=scratch + carry.scratch, compiler_params=_cparams(),
    )(*args, *carry.operands)
    carry.set_results(list(res[n_out:]))
    return list(res[:n_out])


def R_(arr, w=None, off=0, stride=0):
    return ('r', arr, arr.shape[1] if w is None else w, off, stride)


def P_(arr, w=None, off=0, stride=0):
    return ('p', arr, arr.shape[1] if w is None else w, off, stride)


def rowmap(name, fn, ins, outs=None, *, tile, ncol=1, rows_inner=True, cts=None, wrt=None, gdt=None, cat=False,
           carry=None):
    rows = next(a.shape[0] for k, a, *_ in ins if k == 'r')
    nrow = rows // tile
    assert nrow * tile == rows
    grid = (ncol, nrow) if rows_inner else (nrow, ncol)

    def ij(g0, g1):
        return (g1, g0) if rows_inner else (g0, g1)

    def spec(kind, arr, w, off, stride):
        if kind == 'r':
            return pl.BlockSpec((tile, w), lambda g0, g1: (ij(g0, g1)[0], off + stride * ij(g0, g1)[1]))
        return pl.BlockSpec((arr.shape[0], w), lambda g0, g1: (0, off + stride * ij(g0, g1)[1]))

    ops = list(ins) + list(cts or [])
    in_specs = [spec(*o) for o in ops]
    n_in = len(ins)
    fwd = cts is None
    out_shape, out_specs, acc = [], [], []
    if fwd:
        for dt, w, stride in outs:
            out_shape.append(jax.ShapeDtypeStruct((rows, w * (ncol if stride else 1)), dt))
            out_specs.append(spec('r', None, w, 0, stride))
            acc.append(None)
    elif cat:
        widths = [ins[i][2] for i in wrt]
        assert ncol == 1 and all(ins[i][0] == 'r' for i in wrt)
        out_shape.append(jax.ShapeDtypeStruct((rows, sum(widths)), gdt))
        out_specs.append(spec('r', None, sum(widths), 0, 0))
    else:
        for n, i in enumerate(wrt):
            kind, arr, w, off, stride = ins[i]
            width = w * (ncol if stride else 1)
            if kind == 'r':
                out_shape.append(jax.ShapeDtypeStruct((rows, width), gdt[n]))
                out_specs.append(spec('r', None, w, 0, 1 if stride else 0))
                shared = stride == 0 and ncol > 1
                assert not shared or (not rows_inner and gdt[n] == F32)
                acc.append('col' if shared else None)
            else:
                out_shape.append(jax.ShapeDtypeStruct((arr.shape[0], width), F32))
                out_specs.append(spec('p', arr, w, 0, 1 if stride else 0))
                assert rows_inner or ncol == 1
                acc.append('row')

    def body(*refs):
        i, j = ij(pl.program_id(0), pl.program_id(1))
        vals = [r[...].astype(F32) for r in refs[:n_in]]
        out_refs = refs[len(ops):]
        if fwd:
            for o_ref, o in zip(out_refs, fn(*vals)):
                o_ref[...] = o.astype(o_ref.dtype)
            return

        def f(*d):
            full = list(vals)
            for n, idx in enumerate(wrt):
                full[idx] = d[n]
            return fn(*full)

        _, vjp = jax.vjp(f, *[vals[idx] for idx in wrt])
        grads = vjp(tuple(r[...].astype(F32) for r in refs[n_in:len(ops)]))
        if cat:
            o_ref, at = out_refs[0], 0
            for g in grads:
                o_ref[:, at:at + g.shape[1]] = g.astype(o_ref.dtype)
                at += g.shape[1]
            return
        for o_ref, g, a in zip(out_refs, grads, acc):
            if a is None:
                o_ref[...] = g.astype(o_ref.dtype)
            else:
                first = (i if a == 'row' else j) == 0

                @pl.when(first)
                def _(o_ref=o_ref):
                    o_ref[...] = jnp.zeros_like(o_ref)

                o_ref[...] += g

    return _call(name, body, grid=grid, in_specs=in_specs, out_specs=out_specs, out_shape=out_shape,
                 args=[o[1] for o in ops], carry=carry)


def _rms(x, g):
    return x * lax.rsqrt(jnp.mean(x * x, axis=-1, keepdims=True) + EPS) * g


_DIMS = {'nn': (((1,), (0,)), ((), ())), 'nt': (((1,), (1,)), ((), ())), 'tn': (((0,), (0,)), ((), ()))}


def mm(name, a, b, mode, *, jdim=None, out_dtype=F32, carry=None):
    a_list = list(a) if isinstance(a, (list, tuple)) else [a]
    b_list = list(b) if isinstance(b, (list, tuple)) else [b]
    a_order = ('k', 'm') if mode == 'tn' else ('m', 'k')
    b_order = ('n', 'k') if mode == 'nt' else ('k', 'n')
    size, blocks = {}, 1
    for arr, order in ((a_list[0], a_order), (b_list[0], b_order)):
        shape = arr.shape
        if jdim in order:
            blocks, shape = shape[0], shape[1:]
        for dname, extent in zip(order, shape):
            assert size.setdefault(dname, extent) == extent
    tile = {'m': _pick(size['m'], (1024, 512, 256, 128)), 'n': _pick(size['n'], (512, 256, 128)),
            'k': size['k'] if size['k'] <= 2048 else _pick(size['k'], (2304, 2048, 1024, 512, 256, 128))}
    if jdim is not None:
        tile[jdim] = size[jdim]
    if a_list[0].dtype == F32 and tile['k'] * tile['m'] > (1 << 20):
        tile['m'] = _pick(size['m'], (512, 256, 128))
    grid = tuple(blocks if d == jdim else size[d] // tile[d] for d in ('m', 'n', 'k'))
    nk = grid[2]

    def spec(order):
        shape = tuple(tile[d] for d in order)

        def imap(i, j, k):
            g = {'m': i, 'n': j, 'k': k}
            idx = tuple(0 if d == jdim else g[d] for d in order)
            return ((g[jdim],) + idx) if jdim in order else idx

        return pl.BlockSpec(((None,) + shape) if jdim in order else shape, imap)

    dims = _DIMS[mode]
    nt = len(a_list)

    def product(refs):
        acc = None
        for a_ref, b_ref in zip(refs[:nt], refs[nt:2 * nt]):
            p = lax.dot_general(a_ref[...].astype(BF16), b_ref[...].astype(BF16), dims, preferred_element_type=F32)
            acc = p if acc is None else acc + p
        return acc

    def body_once(*refs):
        refs[2 * nt][...] = product(refs).astype(refs[2 * nt].dtype)

    def body_acc(*refs):
        o_ref, acc_ref = refs[2 * nt], refs[2 * nt + 1]
        k = pl.program_id(2)

        @pl.when(k == 0)
        def _():
            acc_ref[...] = jnp.zeros_like(acc_ref)

        acc_ref[...] += product(refs)

        @pl.when(k == nk - 1)
        def _():
            o_ref[...] = acc_ref[...].astype(o_ref.dtype)

    out_dims = (size['m'], size['n'])
    out_shape = jax.ShapeDtypeStruct(((blocks,) + out_dims) if jdim in ('m', 'n') else out_dims, out_dtype)
    return _call(name, body_once if nk == 1 else body_acc, grid=grid,
                 in_specs=[spec(a_order)] * nt + [spec(b_order)] * nt, out_specs=[spec(('m', 'n'))],
                 out_shape=[out_shape], args=a_list + b_list,
                 scratch=[] if nk == 1 else [pltpu.VMEM((tile['m'], tile['n']), F32)], carry=carry)[0]


def _probs(q, k, i, tq, scale, causal):
    s = lax.dot_general(q, k, _DIMS['nt'], preferred_element_type=F32) * scale
    if causal:
        shape = s.shape
        q_chunk = jnp.right_shift(i * tq + lax.broadcasted_iota(jnp.int32, shape, 0), CHUNK_SHIFT)
        k_chunk = jnp.right_shift(lax.broadcasted_iota(jnp.int32, shape, 1), CHUNK_SHIFT)
        s = jnp.where(k_chunk <= q_chunk, s, -jnp.inf)
    e = jnp.exp(s - jnp.max(s, axis=-1, keepdims=True))
    return e, jnp.sum(e, axis=-1, keepdims=True)


def _per_prefix(work, i, tq, sq, sk, causal):
    if not causal:
        work(sk)
        return
    assert sq == sk and tq % CHUNK == 0
    for j in range(sq // tq):
        @pl.when(i == j)
        def _(j=j):
            work((j + 1) * tq)


def _attn_specs(tq, sk, dq, dv, koff, kstride, voff, vstride):
    return [pl.BlockSpec((tq, dq), lambda h, i: (i, h)),
            pl.BlockSpec((sk, dq), lambda h, i: (0, koff + kstride * h)),
            pl.BlockSpec((sk, dv), lambda h, i: (0, voff + vstride * h))]


def attn_fwd(name, q, k, v, *, heads, dq, dv, koff, kstride, voff, vstride, scale, causal, carry=None):
    sq, sk = q.shape[0], k.shape[0]
    tq = min(sq, 256)

    def body(q_ref, k_ref, v_ref, o_ref):
        i = pl.program_id(1)

        def work(keys):
            e, l = _probs(q_ref[...].astype(BF16), k_ref[0:keys, :].astype(BF16), i, tq, scale, causal)
            o = jnp.dot(e.astype(BF16), v_ref[0:keys, :].astype(BF16), preferred_element_type=F32)
            o_ref[...] = o / l

        _per_prefix(work, i, tq, sq, sk, causal)

    return _call(name, body, grid=(heads, sq // tq),
                 in_specs=_attn_specs(tq, sk, dq, dv, koff, kstride, voff, vstride),
                 out_specs=[pl.BlockSpec((tq, dv), lambda h, i: (i, h))],
                 out_shape=[jax.ShapeDtypeStruct((sq, heads * dv), F32)], args=[q, k, v], carry=carry)[0]


def attn_bwd(name, q, k, v, do, *, heads, dq, dv, koff, kstride, voff, vstride, scale, causal, carry=None):
    sq, sk = q.shape[0], k.shape[0]
    tq = min(sq, 256)

    def body(q_ref, k_ref, v_ref, do_ref, dq_ref, dk_ref, dv_ref):
        i = pl.program_id(1)

        @pl.when(i == 0)
        def _():
            dk_ref[...] = jnp.zeros_like(dk_ref)
            dv_ref[...] = jnp.zeros_like(dv_ref)

        def work(keys):
            qb, kb, vb = q_ref[...].astype(BF16), k_ref[0:keys, :].astype(BF16), v_ref[0:keys, :].astype(BF16)
            dob = do_ref[...].astype(BF16)
            e, l = _probs(qb, kb, i, tq, scale, causal)
            p = e / l
            dp = lax.dot_general(dob, vb, _DIMS['nt'], preferred_element_type=F32)
            ds = (p * (dp - jnp.sum(dp * p, axis=-1, keepdims=True)) * scale).astype(BF16)
            dv_ref[0:keys, :] += lax.dot_general(p.astype(BF16), dob, _DIMS['tn'], preferred_element_type=F32)
            dk_ref[0:keys, :] += lax.dot_general(ds, qb, _DIMS['tn'], preferred_element_type=F32)
            dq_ref[...] = jnp.dot(ds, kb, preferred_element_type=F32)

        _per_prefix(work, i, tq, sq, sk, causal)

    return _call(
        name, body, grid=(heads, sq // tq),
        in_specs=_attn_specs(tq, sk, dq, dv, koff, kstride, voff, vstride) + [pl.BlockSpec((tq, dv), lambda h, i: (i, h))],
        out_specs=[pl.BlockSpec((tq, dq), lambda h, i: (i, h)), pl.BlockSpec((sk, dq), lambda h, i: (0, h)),
                   pl.BlockSpec((sk, dv), lambda h, i: (0, h))],
        out_shape=[jax.ShapeDtypeStruct((sq, heads * dq), F32), jax.ShapeDtypeStruct((sk, heads * dq), F32),
                   jax.ShapeDtypeStruct((sk, heads * dv), F32)],
        args=[q, k, v, do], carry=carry)


def _hg_chunk(q, k, g, v, state):
    c = q.shape[0]
    row = lax.broadcasted_iota(jnp.int32, (c, c), 0)
    col = lax.broadcasted_iota(jnp.int32, (c, c), 1)
    tril = (col <= row).astype(F32)
    b = jnp.dot(tril, g, precision=lax.Precision.HIGHEST, preferred_element_type=F32)
    rows = lax.broadcasted_iota(jnp.int32, (c, 1), 0)
    o = jnp.dot((q * jnp.exp(b)).astype(BF16), state.astype(BF16), preferred_element_type=F32)
    t3 = lax.broadcasted_iota(jnp.int32, (SUB, SUB, 1), 0)
    s3 = lax.broadcasted_iota(jnp.int32, (SUB, SUB, 1), 1)
    parts = []
    for n in range(c // SUB):
        lo = n * SUB
        qn, kn, bn, vn = q[lo:lo + SUB], k[lo:lo + SUB], b[lo:lo + SUB], v[lo:lo + SUB]
        decay = jnp.exp(jnp.where(s3 <= t3, bn[:, None, :] - bn[None, :, :], -jnp.inf))
        sc = jnp.sum(qn[:, None, :] * kn[None, :, :] * decay, axis=-1)
        on = jnp.dot(sc.astype(BF16), vn.astype(BF16), preferred_element_type=F32)
        if n > 0:
            ref = jnp.sum(jnp.where(rows == lo - 1, b, 0.0), axis=0, keepdims=True)
            qd = (qn * jnp.exp(bn - ref)).astype(BF16)
            kd = (k[:lo] * jnp.exp(ref - b[:lo])).astype(BF16)
            so = lax.dot_general(qd, kd, _DIMS['nt'], preferred_element_type=F32)
            on = on + jnp.dot(so.astype(BF16), v[:lo].astype(BF16), preferred_element_type=F32)
        parts.append(on)
    o = o + jnp.concatenate(parts, axis=0)
    b_last = jnp.sum(g, axis=0, keepdims=True)
    ones = jnp.ones((c, 1), F32)
    b_last_col = lax.dot_general(g, ones, _DIMS['tn'], precision=lax.Precision.HIGHEST, preferred_element_type=F32)
    kd = (k * jnp.exp(b_last - b)).astype(BF16)
    new_state = jnp.exp(b_last_col) * state + lax.dot_general(kd, v.astype(BF16), _DIMS['tn'],
                                                              preferred_element_type=F32)
    return o, new_state


def hg_scan_fwd(name, q, k, g, u, v_off, carry=None):
    s = q.shape[0]
    n = s // CHUNK

    def body(q_ref, k_ref, g_ref, v_ref, o_ref, st_ref, state):
        @pl.when(pl.program_id(1) == 0)
        def _():
            state[...] = jnp.zeros_like(state)

        for j in range(HG_GROUP):
            cols = slice(j * HEAD_W, (j + 1) * HEAD_W)
            st = state[j]
            st_ref[j] = st
            o, new = _hg_chunk(q_ref[:, cols], k_ref[:, cols], g_ref[:, cols], v_ref[:, cols], st)
            o_ref[:, cols] = o
            state[j] = new

    wide = HG_GROUP * HEAD_W
    blk = pl.BlockSpec((CHUNK, wide), lambda h, c: (c, h))
    return _call(
        name, body, grid=(HG_HEADS // HG_GROUP, n),
        in_specs=[blk, blk, blk, pl.BlockSpec((CHUNK, wide), lambda h, c: (c, v_off // HG_GROUP + h))],
        out_specs=[blk, pl.BlockSpec((HG_GROUP, None, HEAD_W, HEAD_W), lambda h, c: (h, c, 0, 0))],
        out_shape=[jax.ShapeDtypeStruct((s, HG_HEADS * HEAD_W), F32),
                   jax.ShapeDtypeStruct((HG_HEADS, n, HEAD_W, HEAD_W), F32)],
        scratch=[pltpu.VMEM((HG_GROUP, HEAD_W, HEAD_W), F32)], args=[q, k, g, u], carry=carry)


def hg_scan_bwd(name, q, k, g, u, v_off, states, do, carry=None):
    s = q.shape[0]
    n = s // CHUNK

    def body(q_ref, k_ref, g_ref, v_ref, st_ref, do_ref, dq_ref, dk_ref, dg_ref, dv_ref, dstate):
        @pl.when(pl.program_id(1) == 0)
        def _():
            dstate[...] = jnp.zeros_like(dstate)

        for j in range(HG_GROUP):
            cols = slice(j * HEAD_W, (j + 1) * HEAD_W)
            _, vjp = jax.vjp(_hg_chunk, q_ref[:, cols], k_ref[:, cols], g_ref[:, cols], v_ref[:, cols], st_ref[j])
            dq, dk, dg, dv, dst = vjp((do_ref[:, cols], dstate[j]))
            dq_ref[:, cols] = dq
            dk_ref[:, cols] = dk
            dg_ref[:, cols] = dg
            dv_ref[:, cols] = dv
            dstate[j] = dst

    wide = HG_GROUP * HEAD_W
    blk = pl.BlockSpec((CHUNK, wide), lambda h, c: (n - 1 - c, h))
    out = jax.ShapeDtypeStruct((s, HG_HEADS * HEAD_W), F32)
    return _call(
        name, body, grid=(HG_HEADS // HG_GROUP, n),
        in_specs=[blk, blk, blk, pl.BlockSpec((CHUNK, wide), lambda h, c: (n - 1 - c, v_off // HG_GROUP + h)),
                  pl.BlockSpec((HG_GROUP, None, HEAD_W, HEAD_W), lambda h, c: (h, n - 1 - c, 0, 0)), blk],
        out_specs=[blk, blk, blk, blk], out_shape=[out, out, out, out],
        scratch=[pltpu.VMEM((HG_GROUP, HEAD_W, HEAD_W), F32)], args=[q, k, g, u, states, do], carry=carry)


def pair_sum(name, blocks, theirs):
    _, rows, cols = theirs.shape
    tile = next((t for t in (1024, 512, 256, 128, 64, 32, 16) if rows % t == 0 and t * cols <= (1 << 20)), rows)

    def body(a_ref, b_ref, o_ref):
        mine = jnp.where(lax.axis_index("c") == 0, a_ref[0].astype(F32), a_ref[1].astype(F32))
        o_ref[...] = (mine + b_ref[...].astype(F32)).astype(o_ref.dtype)

    blk = pl.BlockSpec((None, tile, cols), lambda q, i: (q, i, 0))
    return _call(name, body, grid=(N_CHIP, rows // tile),
                 in_specs=[pl.BlockSpec((None, 2, tile, cols), lambda q, i: (q, 0, i, 0)), blk], out_specs=[blk],
                 out_shape=[jax.ShapeDtypeStruct(theirs.shape, theirs.dtype)],
                 args=[blocks.reshape(N_CHIP, 2, rows, cols), theirs])[0]


def adamw(name, landed, w, m, v):
    rows, cols = w.shape
    slots = landed.shape[0]
    tile = next((t for t in (512, 256, 128, 64, 32, 16, 8) if rows % t == 0 and t * cols <= (1 << 18)), rows)

    def body(l_ref, w_ref, m_ref, v_ref, g_ref, d_ref, nm_ref, nv_ref):
        gv = l_ref[0].astype(F32)
        for s in range(1, slots):
            gv = gv + l_ref[s].astype(F32)
        nm = ADAM_B1 * m_ref[...] + (1.0 - ADAM_B1) * gv
        nv = ADAM_B2 * v_ref[...] + (1.0 - ADAM_B2) * jnp.square(gv)
        m_hat = nm / (1.0 - ADAM_B1 ** ADAM_STEP)
        v_hat = nv / (1.0 - ADAM_B2 ** ADAM_STEP)
        g_ref[...] = gv
        d_ref[...] = -ADAM_LR * (m_hat / (jnp.sqrt(v_hat) + ADAM_EPS) + ADAM_WD * w_ref[...])
        nm_ref[...] = nm
        nv_ref[...] = nv

    blk = pl.BlockSpec((tile, cols), lambda i: (i, 0))
    out = jax.ShapeDtypeStruct((rows, cols), F32)
    return _call(name, body, grid=(rows // tile,),
                 in_specs=[pl.BlockSpec((slots, tile, cols), lambda i: (0, i, 0)), blk, blk, blk],
                 out_specs=[blk] * 4, out_shape=[out] * 4, args=[landed, w, m, v])


def _swap_halves(pe):
    half = QK_ROPE // 2
    return jnp.concatenate([-pe[..., half:], pe[..., :half]], axis=-1)


def _unswap_halves(dsw):
    half = QK_ROPE // 2
    return jnp.concatenate([dsw[..., half:], -dsw[..., :half]], axis=-1)


def _w_in_ext(w):
    kpe = w[:, 9216:9280]
    z = jnp.zeros((w.shape[0], HEAD_W - QK_ROPE), w.dtype)
    pad = jnp.zeros((w.shape[0], U_PAD), w.dtype)
    return jnp.concatenate([w[:, :9216], w[:, 9280:], kpe, z, _swap_halves(kpe), z, pad], axis=1)


def _w_in_grad(d):
    dkpe = d[:, 13312:13376] + _unswap_halves(d[:, 13440:13504])
    return jnp.concatenate([d[:, :9216], dkpe, d[:, 9216:13312]], axis=1)


def _w_q_ext(w):
    w3 = w.reshape(Q_LORA, MLA_HEADS, MLA_QK)
    pe = w3[:, :, HEAD_W:]
    z = jnp.zeros((Q_LORA, MLA_HEADS, HEAD_W - QK_ROPE), w.dtype)
    wide = MLA_HEADS * HEAD_W
    return jnp.concatenate([w3[:, :, :HEAD_W].reshape(Q_LORA, wide),
                            jnp.concatenate([pe, z], axis=2).reshape(Q_LORA, wide),
                            jnp.concatenate([_swap_halves(pe), z], axis=2).reshape(Q_LORA, wide)], axis=1)


def _w_q_grad(parts):
    d3 = [p.reshape(Q_LORA, MLA_HEADS, HEAD_W) for p in parts]
    dpe = d3[1][:, :, :QK_ROPE] + _unswap_halves(d3[2][:, :, :QK_ROPE])
    return jnp.concatenate([d3[0], dpe], axis=2).reshape(Q_LORA, MLA_HEADS * MLA_QK)


def _hg_prep(f_raw, q_hg, logits):
    lb = jax.nn.softmax(logits, axis=0)[0:1, :]
    log_f = jnp.logaddexp(jnp.log(lb), jnp.log1p(-lb) + jax.nn.log_sigmoid(f_raw))
    k_in = (1.0 - lb) * jax.nn.sigmoid(-f_raw)
    return log_f, k_in, jax.nn.silu(q_hg)


def _rope(q_nope, q_pe, q_sw, k_nope, k_pe, k_sw, cos, sin):
    qf = jnp.concatenate([q_nope, q_pe * cos + q_sw * sin], axis=1)
    kf = jnp.concatenate([k_nope, k_pe * cos + k_sw * sin], axis=1)
    return qf, kf


def _step(a):
    x, mem, target = a['x'][0], a['mem'][0], a['loss_target'][0]
    s, d = x.shape
    nm = mem.shape[0]
    ff = N_DEV * a['ffn1_w_gate'].shape[-1]
    cs = ff // N_DEV
    tr = min(s, 128)
    th = min(s, 1024)
    ta = 512

    bf = {n: a[n][0].astype(BF16) for n in BIG}
    half_in = bf['w_in'].shape[0] // 2
    bf['w_in#0'], bf['w_in#1'] = bf['w_in'][:half_in], bf['w_in'][half_in:]
    gat, full = {}, {}

    my_chip = 2 * lax.axis_index("x") + lax.axis_index("y")
    me = 2 * my_chip + lax.axis_index("c")

    def gathered(names, carry):
        for n, g8 in zip(names or [], carry.results if carry else []):
            r, c = bf[n].shape
            g8 = lax.dynamic_update_index_in_dim(g8, bf[n], me, 0)
            gat[n] = g8
            if not n.startswith('ffn'):
                by_col = BIG[n.split('#')[0]]
                full[n] = g8.transpose(1, 0, 2).reshape(r, N_DEV * c) if by_col else g8.reshape(N_DEV * r, c)

    xa_names = ['xa_w_q', 'xa_w_k', 'xa_w_v', 'xa_w_o']
    ffn2_names = ['ffn2_w_gate', 'ffn2_w_up', 'ffn2_w_down']
    first = Gather([bf['ffn1_w_gate']])
    run_alone("gather_first", first)
    gathered(['ffn1_w_gate'], first)

    inv_freq = 1.0 / (ROPE_THETA ** (jnp.arange(0, QK_ROPE, 2, dtype=F32) / QK_ROPE))
    ang = a['positions'][0].astype(F32)[:, None] * inv_freq
    zero = jnp.zeros((s, HEAD_W - QK_ROPE), F32)
    cos = jnp.concatenate([jnp.cos(ang), jnp.cos(ang), zero], axis=1)
    sin = jnp.concatenate([jnp.sin(ang), jnp.sin(ang), zero], axis=1)

    gs = {}
    gb = {}

    def by_rows(g):
        return g.reshape(N_DEV, g.shape[0] // N_DEV, g.shape[1])

    def by_cols(g):
        return g.reshape(g.shape[0], N_DEV, g.shape[1] // N_DEV).transpose(1, 0, 2)

    pre_fn = lambda xv, g: (_rms(xv, g),)
    pre_res_fn = lambda xv, g: (_rms(xv, g), xv)

    def pre_norm(tag, x_in, g):
        return rowmap(tag + "_pre", pre_fn, [R_(x_in), P_(g)], [(BF16, x_in.shape[1], 0)], tile=tr)[0]

    def pre_norm_bwd(tag, x_in, g, dh, d_out, carry=None):
        return rowmap(tag + "_pre_bwd", pre_res_fn, [R_(x_in), P_(g)], tile=tr, cts=[R_(dh), R_(d_out)],
                      wrt=[0, 1], gdt=[F32, F32], carry=carry)

    def post_fn(weight):
        return lambda xv, y, g: (xv + weight * _rms(y, g),)

    def post_norm(tag, x_in, y, g, weight):
        return rowmap(tag + "_post", post_fn(weight), [R_(x_in), R_(y), P_(g)], [(F32, d, 0)], tile=tr)[0]

    def post_norm_bwd(tag, x_in, y, g, weight, d_out):
        return rowmap(tag + "_post_bwd", post_fn(weight), [R_(x_in), R_(y), P_(g)], tile=tr, cts=[R_(d_out)],
                      wrt=[1, 2], gdt=[BF16, F32])

    act_fn = lambda av, bv: (jax.nn.silu(av) * bv,)

    def fetch(plan, key):
        names = plan.get(key)
        return (names, Gather([bf[n] for n in names])) if names else (None, None)

    def send(plan, key):
        acts = plan.get(key)
        if not acts:
            return None, None
        parts = []
        for kind, names in acts:
            if kind == 'spread':
                for n in names:
                    chip_sums[n] = pair_sum("pair_" + n.replace('#', '_'), gb[n], halves[n])
            parts.append(Halve([gb[n] for n in names]) if kind == 'halve' else Spread([chip_sums[n] for n in names]))
        return acts, Joint(parts)

    def ffn_fwd(tag, x_in, plan):
        h = pre_norm(tag, x_in, a[tag + '_pre_g'])
        names, carry = fetch(plan, 'gate')
        av = mm(tag + "_gate", h, gat[tag + '_w_gate'], 'nn', jdim='n', carry=carry)
        gathered(names, carry)
        names, carry = fetch(plan, 'up')
        bv = mm(tag + "_up", h, gat[tag + '_w_up'], 'nn', jdim='n', carry=carry)
        gathered(names, carry)
        a2, b2 = av.reshape(N_DEV * s, cs), bv.reshape(N_DEV * s, cs)
        names, carry = fetch(plan, 'act')
        z = rowmap(tag + "_act", act_fn, [R_(a2), R_(b2)], [(BF16, cs, 0)], tile=ta, carry=carry)[0]
        z = z.reshape(N_DEV, s, cs)
        gathered(names, carry)
        names, carry = fetch(plan, 'dn')
        y = mm(tag + "_dn", z, gat[tag + '_w_down'], 'nn', jdim='k', carry=carry)
        gathered(names, carry)
        return post_norm(tag, x_in, y, a[tag + '_post_g'], 0.5), (x_in, h, a2, b2, z, y)

    def ffn_bwd(tag, res, d_out, plan):
        x_in, h, a2, b2, z, y = res
        dy, gs[tag + '_post_g'] = post_norm_bwd(tag, x_in, y, a[tag + '_post_g'], 0.5, d_out)
        acts, carry = send(plan, 'dn_dx')
        dz = mm(tag + "_dn_dx", dy, gat[tag + '_w_down'], 'nt', jdim='n', carry=carry)
        exchanged(acts, carry)
        acts, carry = send(plan, 'dn_dw')
        gb[tag + '_w_down'] = mm(tag + "_dn_dw", z, dy, 'tn', jdim='m', out_dtype=BF16, carry=carry)
        exchanged(acts, carry)
        acts, carry = send(plan, 'act_bwd')
        da, db = rowmap(tag + "_act_bwd", act_fn, [R_(a2), R_(b2)], tile=ta, cts=[R_(dz.reshape(N_DEV * s, cs))],
                        wrt=[0, 1], gdt=[BF16, BF16], carry=carry)
        exchanged(acts, carry)
        da, db = da.reshape(N_DEV, s, cs), db.reshape(N_DEV, s, cs)
        names, carry = send(plan, 'gu_dx')
        dh = mm(tag + "_gu_dx", [da, db], [gat[tag + '_w_gate'], gat[tag + '_w_up']], 'nt', jdim='k', carry=carry)
        exchanged(names, carry)
        gb[tag + '_w_gate'] = mm(tag + "_gate_dw", h, da, 'tn', jdim='n', out_dtype=BF16)
        names, carry = send(plan, 'up_dw')
        gb[tag + '_w_up'] = mm(tag + "_up_dw", h, db, 'tn', jdim='n', out_dtype=BF16, carry=carry)
        exchanged(names, carry)
        names, carry = send(plan, 'pre_bwd')
        d_in, gs[tag + '_pre_g'] = pre_norm_bwd(tag, x_in, a[tag + '_pre_g'], dh, d_out, carry)
        exchanged(names, carry)
        return d_in

    hg_out_fn = lambda o, og, g: (_rms(o, g) * jax.nn.silu(og),)
    mla_norm_fn = lambda cq, ckv, gq, gkv: (_rms(cq, gq), _rms(ckv, gkv))
    gate_fn = lambda ga, gb, ya, yb: (jax.nn.sigmoid(ga) * ya + jax.nn.sigmoid(gb) * yb,)
    mla = dict(heads=MLA_HEADS, dq=2 * HEAD_W, dv=HEAD_W, koff=0, kstride=1, voff=1, vstride=2,
               scale=MLA_QK ** -0.5, causal=True)

    def mix_fwd(x_in, plan):
        w_in = _w_in_ext(jnp.concatenate([full['w_in#0'], full['w_in#1']], axis=0))
        h = pre_norm("mix", x_in, a['mix_pre_g'])
        names, carry = fetch(plan, 'in')
        u = mm("mix_in", h, w_in, 'nn', carry=carry)
        gathered(names, carry)
        w_q, w_kv = _w_q_ext(full['mla_w_q_up']), full['mla_w_kv_up']
        hg_ins = [R_(u, 2048, 1), R_(u, 2048, 0), P_(a['hgrn_lb_logits'])]
        log_f, k_in, q_in = rowmap("hg_prep", _hg_prep, hg_ins, [(F32, 2048, 0)] * 3, tile=tr)
        names, carry = fetch(plan, 'scan')
        o_a, states = hg_scan_fwd("hg_scan", q_in, k_in, log_f, u, 32, carry=carry)
        gathered(names, carry)
        out_ins = [R_(o_a, HEAD_W, 0, 1), R_(u, HEAD_W, 48, 1), P_(a['hg_norm_g'], HEAD_W, 0, 1)]
        oag = rowmap("hg_out", hg_out_fn, out_ins, [(BF16, HEAD_W, 1)], tile=th, ncol=HG_HEADS)[0]
        y_a = mm("mix_a", oag, full['w_branch_a'], 'nn')
        norm_ins = [R_(u, 512, 16), R_(u, 512, 17), P_(a['mla_q_norm_g']), P_(a['mla_kv_norm_g'])]
        cqn, ckvn = rowmap("mla_norm", mla_norm_fn, norm_ins, [(BF16, 512, 0)] * 2, tile=tr)
        q_all = mm("mla_qup", cqn, w_q, 'nn')
        kv = mm("mla_kvup", ckvn, w_kv, 'nn')
        rope_ins = [R_(q_all, HEAD_W, 0, 1), R_(q_all, HEAD_W, 16, 1), R_(q_all, HEAD_W, 32, 1),
                    R_(kv, HEAD_W, 0, 2), R_(u, HEAD_W, 104, 0), R_(u, HEAD_W, 105, 0), R_(cos), R_(sin)]
        qf, kf = rowmap("mla_rope", _rope, rope_ins, [(BF16, 2 * HEAD_W, 1)] * 2, tile=th, ncol=MLA_HEADS,
                        rows_inner=False)
        names, carry = fetch(plan, 'attn')
        o_b = attn_fwd("mla_attn", qf, kf, kv, carry=carry, **mla)
        gathered(names, carry)
        y_b = mm("mix_b", o_b, full['w_branch_b'], 'nn')
        gate_ins = [R_(u, 1024, 9, 1), R_(u, 1024, 11, 1), R_(y_a, 1024, 0, 1), R_(y_b, 1024, 0, 1)]
        y = rowmap("mix_gate", gate_fn, gate_ins, [(BF16, 1024, 1)], tile=tr, ncol=2)[0]
        yo = mm("mix_out", y, full['w_out'], 'nn')
        res = (x_in, h, hg_ins, q_in, k_in, log_f, u, states, out_ins, oag, norm_ins, cqn, ckvn, rope_ins, qf, kf,
               kv, o_b, gate_ins, y, yo, w_in, w_q, w_kv)
        return post_norm("mix", x_in, yo, a['mix_post_g'], 1.0), res

    def mix_bwd(res, d_out, plan):
        (x_in, h, hg_ins, q_in, k_in, log_f, u, states, out_ins, oag, norm_ins, cqn, ckvn, rope_ins, qf, kf, kv,
         o_b, gate_ins, y, yo, w_in, w_q, w_kv) = res
        dyo, gs['mix_post_g'] = post_norm_bwd("mix", x_in, yo, a['mix_post_g'], 1.0, d_out)
        dy = mm("mix_out_dx", dyo, full['w_out'], 'nt')
        gb['w_out'] = by_rows(mm("mix_out_dw", y, dyo, 'tn', out_dtype=BF16))
        dga, dgb, dya, dyb = rowmap("mix_gate_bwd", gate_fn, gate_ins, tile=tr, ncol=2, cts=[R_(dy, 1024, 0, 1)],
                                    wrt=[0, 1, 2, 3], gdt=[BF16] * 4)
        gb['w_branch_b'] = by_rows(mm("mix_b_dw", o_b, dyb, 'tn', out_dtype=BF16))
        do_b = mm("mix_b_dx", dyb, full['w_branch_b'], 'nt')
        names, carry = send(plan, 'attn_bwd')
        dqf, dkf, dv = attn_bwd("mla_attn_bwd", qf, kf, kv, do_b, carry=carry, **mla)
        exchanged(names, carry)
        dqn, dqp, dqs, dkn, dkpe, dksw = rowmap(
            "mla_rope_bwd", _rope, rope_ins, tile=th, ncol=MLA_HEADS, rows_inner=False,
            cts=[R_(dqf, 2 * HEAD_W, 0, 1), R_(dkf, 2 * HEAD_W, 0, 1)], wrt=[0, 1, 2, 3, 4, 5],
            gdt=[BF16, BF16, BF16, BF16, F32, F32])
        wide = MLA_HEADS * HEAD_W
        dq_parts = [dqn, dqp, dqs]
        gb['mla_w_q_up'] = by_cols(_w_q_grad([mm("mla_qup_dw%d" % n, cqn, dq_parts[n], 'tn', out_dtype=BF16)
                                              for n in range(3)]))
        dcqn = mm("mla_qup_dx", dq_parts, [w_q[:, n * wide:(n + 1) * wide] for n in range(3)], 'nt')
        w_kv4 = w_kv.reshape(KV_LORA, MLA_HEADS, 2, HEAD_W)
        dw_k = mm("mla_kup_dw", ckvn, dkn, 'tn', out_dtype=BF16).reshape(KV_LORA, MLA_HEADS, 1, HEAD_W)
        dw_v = mm("mla_vup_dw", ckvn, dv, 'tn', out_dtype=BF16).reshape(KV_LORA, MLA_HEADS, 1, HEAD_W)
        gb['mla_w_kv_up'] = by_cols(jnp.concatenate([dw_k, dw_v], axis=2).reshape(KV_LORA, 2 * wide))
        dckvn = mm("mla_kvup_dx", [dkn, dv],
                   [w_kv4[:, :, 0].reshape(KV_LORA, wide), w_kv4[:, :, 1].reshape(KV_LORA, wide)], 'nt')
        dcq, dckv, gs['mla_q_norm_g'], gs['mla_kv_norm_g'] = rowmap(
            "mla_norm_bwd", mla_norm_fn, norm_ins, tile=tr, cts=[R_(dcqn), R_(dckvn)], wrt=[0, 1, 2, 3],
            gdt=[BF16, BF16, F32, F32])
        gb['w_branch_a'] = by_rows(mm("mix_a_dw", oag, dya, 'tn', out_dtype=BF16))
        doag = mm("mix_a_dx", dya, full['w_branch_a'], 'nt')
        acts, carry = send(plan, 'out_bwd')
        do_a, dog, gs['hg_norm_g'] = rowmap("hg_out_bwd", hg_out_fn, out_ins, tile=th, ncol=HG_HEADS,
                                            cts=[R_(doag, HEAD_W, 0, 1)], wrt=[0, 1, 2], gdt=[F32, BF16, F32],
                                            carry=carry)
        exchanged(acts, carry)
        names, carry = send(plan, 'scan_bwd')
        dq_in, dk_in, dlog_f, di = hg_scan_bwd("hg_scan_bwd", q_in, k_in, log_f, u, 32, states, do_a, carry=carry)
        exchanged(names, carry)
        df, dq_hg, gs['hgrn_lb_logits'] = rowmap("hg_prep_bwd", _hg_prep, hg_ins, tile=tr,
                                                 cts=[R_(dlog_f), R_(dk_in), R_(dq_in)], wrt=[0, 1, 2],
                                                 gdt=[BF16, BF16, F32])
        du = jnp.concatenate([dq_hg, df, di.astype(BF16), dog, dcq, dckv, dga, dgb, dkpe.astype(BF16),
                              dksw.astype(BF16), jnp.zeros((s, U_PAD), BF16)], axis=1)
        g_in = by_cols(_w_in_grad(mm("mix_in_dw", h, du, 'tn', out_dtype=BF16)))
        gb['w_in#0'], gb['w_in#1'] = g_in[:, :half_in], g_in[:, half_in:]
        names, carry = send(plan, 'in_dx')
        dh = mm("mix_in_dx", du, w_in, 'nt', carry=carry)
        exchanged(names, carry)
        d_in, gs['mix_pre_g'] = pre_norm_bwd("mix", x_in, a['mix_pre_g'], dh, d_out)
        return d_in

    xa = dict(heads=XA_HEADS, dq=HEAD_W, dv=HEAD_W, koff=0, kstride=1, voff=XA_HEADS, vstride=1,
              scale=HEAD_W ** -0.5, causal=False)
    tm_ = min(nm, 128)

    def xa_fwd(x_in):
        w_xkv = jnp.concatenate([full['xa_w_k'], full['xa_w_v']], axis=1)
        h = pre_norm("xa", x_in, a['xa_pre_g'])
        mn = rowmap("xa_mem", pre_fn, [R_(mem), P_(a['xa_mem_g'])], [(BF16, d, 0)], tile=tm_)[0]
        q = mm("xa_q", h, full['xa_w_q'], 'nn')
        kv = mm("xa_kv", mn, w_xkv, 'nn')
        o = attn_fwd("xa_attn", q, kv, kv, **xa)
        yo = mm("xa_o", o, full['xa_w_o'], 'nn')
        return post_norm("xa", x_in, yo, a['xa_post_g'], 1.0), (x_in, h, mn, q, kv, o, yo, w_xkv)

    def xa_bwd(res, d_out, plan):
        x_in, h, mn, q, kv, o, yo, w_xkv = res
        dyo, gs['xa_post_g'] = post_norm_bwd("xa", x_in, yo, a['xa_post_g'], 1.0, d_out)
        do = mm("xa_o_dx", dyo, full['xa_w_o'], 'nt')
        gb['xa_w_o'] = by_cols(mm("xa_o_dw", o, dyo, 'tn', out_dtype=BF16))
        dq, dk, dv = attn_bwd("xa_attn_bwd", q, kv, kv, do, **xa)
        dkv = jnp.concatenate([dk, dv], axis=1).astype(BF16)
        dw = mm("xa_kv_dw", mn, dkv, 'tn', out_dtype=BF16)
        gb['xa_w_k'], gb['xa_w_v'] = by_rows(dw[:, :XA_HEADS * HEAD_W]), by_rows(dw[:, XA_HEADS * HEAD_W:])
        dmn = mm("xa_kv_dx", dkv, w_xkv, 'nt')
        gs['xa_mem_g'] = rowmap("xa_mem_bwd", pre_fn, [R_(mem), P_(a['xa_mem_g'])], tile=tm_, cts=[R_(dmn)],
                                wrt=[1], gdt=[F32])[0]
        gb['xa_w_q'] = by_rows(mm("xa_q_dw", h, dq, 'tn', out_dtype=BF16))
        dh = mm("xa_q_dx", dq, full['xa_w_q'], 'nt')
        acts, carry = send(plan, 'pre_bwd')
        d_in, gs['xa_pre_g'] = pre_norm_bwd("xa", x_in, a['xa_pre_g'], dh, d_out, carry)
        exchanged(acts, carry)
        return d_in

    landed = {}

    halves = {}
    chip_sums = {}

    def exchanged(acts, carry):
        for (kind, names), part in zip(acts or [], carry.parts if carry else []):
            for n, got in zip(names, part.results):
                if kind == 'halve':
                    halves[n] = got
                else:
                    own = lax.dynamic_index_in_dim(chip_sums[n], my_chip, 0, keepdims=True)
                    landed[n] = lax.dynamic_update_slice_in_dim(got, own, my_chip, 0)

    x1, r1 = ffn_fwd('ffn1', x, {'gate': ['ffn1_w_up'], 'up': ['ffn1_w_down'], 'act': ['w_in#0'], 'dn': ['w_in#1']})
    x2, r2 = mix_fwd(x1, {'in': ['mla_w_q_up', 'mla_w_kv_up', 'w_branch_a', 'w_branch_b', 'w_out'],
                          'scan': xa_names + ['ffn2_w_gate'], 'attn': ['ffn2_w_up']})
    x3, r3 = xa_fwd(x2)
    x4, r4 = ffn_fwd('ffn2', x3, {'gate': ['ffn2_w_down']})

    def loss_fn(y, t):
        diff = y - t
        return diff * (1.0 / d), jnp.mean(diff * diff, axis=-1, keepdims=True)

    d4, row_loss = rowmap("loss", loss_fn, [R_(x4), R_(target)], [(F32, d, 0), (F32, 1, 0)], tile=tr)
    loss = lax.psum(0.5 * jnp.sum(row_loss), ("x", "y", "c"))
    late = ['mla_w_q_up', 'mla_w_kv_up', 'w_branch_a']
    d3 = ffn_bwd('ffn2', r4, d4, {'act_bwd': [('halve', ['ffn2_w_down'])], 'gu_dx': [('spread', ['ffn2_w_down'])],
                                  'up_dw': [('halve', ['ffn2_w_gate'])], 'pre_bwd': [('halve', ['ffn2_w_up'])]})
    d2 = xa_bwd(r3, d3, {'pre_bwd': [('halve', xa_names)]})
    d1 = mix_bwd(r2, d2, {'attn_bwd': [('spread', ['ffn2_w_gate', 'ffn2_w_up']), ('halve', ['w_out', 'w_branch_b'])],
                          'out_bwd': [('halve', late)],
                          'scan_bwd': [('spread', xa_names + ['w_out', 'w_branch_b'] + late)],
                          'in_dx': [('halve', ['w_in#0', 'w_in#1'])]})
    grad_x = ffn_bwd('ffn1', r1, d1, {'dn_dx': [('spread', ['w_in#0'])], 'dn_dw': [('spread', ['w_in#1'])],
                                      'act_bwd': [('halve', ['ffn1_w_down'])],
                                      'gu_dx': [('spread', ['ffn1_w_down'])], 'up_dw': [('halve', ['ffn1_w_gate'])],
                                      'pre_bwd': [('spread', ['ffn1_w_gate']), ('halve', ['ffn1_w_up'])]})
    acts, carry = send({'end': [('spread', ['ffn1_w_up'])]}, 'end')
    run_alone("spread_last", carry)
    exchanged(acts, carry)

    def pack_small(vals):
        flat = jnp.concatenate([vals[n].reshape(-1) for n in SMALL])
        rows = -(-flat.shape[0] // PACK_W)
        rows = -(-rows // 8) * 8
        return jnp.pad(flat, (0, rows * PACK_W - flat.shape[0])).reshape(rows, PACK_W)

    def unpack_small(buf):
        flat, out, at = buf.reshape(-1), {}, 0
        for n in SMALL:
            size = a[n].shape[0] * a[n].shape[1]
            out[n] = flat[at:at + size].reshape(a[n].shape)
            at += size
        return out

    g_small = pack_small(gs)
    small = Gather([g_small])
    run_alone("gather_g_small", small)
    g_small = lax.dynamic_update_index_in_dim(small.results[0], g_small, me, 0)

    grads, delta, new_m, new_v = {}, {}, {}, {}
    packs = adamw("adamw_small", g_small, pack_small(a), pack_small({n: a['m_' + n] for n in SMALL}),
                  pack_small({n: a['v_' + n] for n in SMALL}))
    for dst, buf in zip((grads, delta, new_m, new_v), packs):
        dst.update(unpack_small(buf))
    landed['w_in'] = jnp.concatenate([landed['w_in#0'], landed['w_in#1']], axis=1)
    for n in BIG:
        outs = adamw("adamw_" + n, landed[n], a[n][0], a['m_' + n][0], a['v_' + n][0])
        grads[n], delta[n], new_m[n], new_v[n] = (t.reshape(a[n].shape) for t in outs)

    return (loss, grad_x[None], *[grads[n] for n in WEIGHTS], *[delta[n] for n in WEIGHTS],
            *[new_m[n] for n in WEIGHTS], *[new_v[n] for n in WEIGHTS])


def kernel(x, mem, positions, hgrn_lb_logits, ffn1_pre_g, ffn1_w_gate, ffn1_w_up, ffn1_w_down, ffn1_post_g, mix_pre_g, w_in, hg_norm_g, mla_q_norm_g, mla_w_q_up, mla_kv_norm_g, mla_w_kv_up, w_branch_a, w_branch_b, w_out, mix_post_g, xa_pre_g, xa_mem_g, xa_w_q, xa_w_k, xa_w_v, xa_w_o, xa_post_g, ffn2_pre_g, ffn2_w_gate, ffn2_w_up, ffn2_w_down, ffn2_post_g, loss_target, m_hgrn_lb_logits, m_ffn1_pre_g, m_ffn1_w_gate, m_ffn1_w_up, m_ffn1_w_down, m_ffn1_post_g, m_mix_pre_g, m_w_in, m_hg_norm_g, m_mla_q_norm_g, m_mla_w_q_up, m_mla_kv_norm_g, m_mla_w_kv_up, m_w_branch_a, m_w_branch_b, m_w_out, m_mix_post_g, m_xa_pre_g, m_xa_mem_g, m_xa_w_q, m_xa_w_k, m_xa_w_v, m_xa_w_o, m_xa_post_g, m_ffn2_pre_g, m_ffn2_w_gate, m_ffn2_w_up, m_ffn2_w_down, m_ffn2_post_g, v_hgrn_lb_logits, v_ffn1_pre_g, v_ffn1_w_gate, v_ffn1_w_up, v_ffn1_w_down, v_ffn1_post_g, v_mix_pre_g, v_w_in, v_hg_norm_g, v_mla_q_norm_g, v_mla_w_q_up, v_mla_kv_norm_g, v_mla_w_kv_up, v_w_branch_a, v_w_branch_b, v_w_out, v_mix_post_g, v_xa_pre_g, v_xa_mem_g, v_xa_w_q, v_xa_w_k, v_xa_w_v, v_xa_w_o, v_xa_post_g, v_ffn2_pre_g, v_ffn2_w_gate, v_ffn2_w_up, v_ffn2_w_down, v_ffn2_post_g):
    return _step(dict(locals()))
```

```python
import functools

import jax
import jax.numpy as jnp
from jax import lax
from jax.experimental import pallas as pl
from jax.experimental.pallas import tpu as pltpu

F32 = jnp.float32
BF16 = jnp.bfloat16

N_DEV = 8
D_MODEL = 2048
CHUNK = 64
CHUNK_SHIFT = 6
SUB = 16
HG_HEADS = 16
HG_GROUP = 8
HEAD_W = 128
MLA_HEADS = 16
Q_LORA = 512
KV_LORA = 512
QK_ROPE = 64
MLA_QK = 192
XA_HEADS = 4
ROPE_THETA = 10000.0
EPS = 1e-6
PACK_W = 1024
VMEM_LIMIT = 56 * 1024 * 1024

ADAM_LR = 0.001
ADAM_B1 = 0.9
ADAM_B2 = 0.999
ADAM_EPS = 1e-08
ADAM_WD = 0.01
ADAM_STEP = 10

U_PAD = 256

WEIGHTS = ['hgrn_lb_logits', 'ffn1_pre_g', 'ffn1_w_gate', 'ffn1_w_up', 'ffn1_w_down', 'ffn1_post_g', 'mix_pre_g',
           'w_in', 'hg_norm_g', 'mla_q_norm_g', 'mla_w_q_up', 'mla_kv_norm_g', 'mla_w_kv_up', 'w_branch_a',
           'w_branch_b', 'w_out', 'mix_post_g', 'xa_pre_g', 'xa_mem_g', 'xa_w_q', 'xa_w_k', 'xa_w_v', 'xa_w_o',
           'xa_post_g', 'ffn2_pre_g', 'ffn2_w_gate', 'ffn2_w_up', 'ffn2_w_down', 'ffn2_post_g']
BIG = {'ffn1_w_gate': True, 'ffn1_w_up': True, 'ffn1_w_down': False, 'w_in': True, 'mla_w_q_up': True,
       'mla_w_kv_up': True, 'w_branch_a': False, 'w_branch_b': False, 'w_out': False, 'xa_w_q': False,
       'xa_w_k': False, 'xa_w_v': False, 'xa_w_o': True, 'ffn2_w_gate': True, 'ffn2_w_up': True,
       'ffn2_w_down': False}
SMALL = [n for n in WEIGHTS if n not in BIG]


def _cparams(**kw):
    return pltpu.CompilerParams(vmem_limit_bytes=VMEM_LIMIT, **kw)


def _pick(dim, cands):
    for c in cands:
        if dim % c == 0:
            return c
    return dim


def _place():
    return lax.axis_index("x"), lax.axis_index("y"), lax.axis_index("c")


def _slot(px, py, pc):
    return 4 * px + 2 * py + pc


class Gather:
    def __init__(self, tensors):
        self.operands = list(tensors)
        self.out_shape = [jax.ShapeDtypeStruct((N_DEV,) + t.shape, t.dtype) for t in tensors]
        n = len(tensors)
        self.scratch = [pltpu.SemaphoreType.DMA((n, N_DEV - 1)), pltpu.SemaphoreType.DMA((n, N_DEV - 1))]

    def _copies(self, t, x_ref, out_ref, send, recv):
        x, y, c = _place()
        chips = [(1 - x, y), (x, 1 - y), (1 - x, 1 - y)]

        def copy(k, block, to, src=None):
            rows = out_ref.at[_slot(*block)]
            return pltpu.make_async_remote_copy(src_ref=rows if src is None else src, dst_ref=rows,
                                                send_sem=send.at[t, k], recv_sem=recv.at[t, k], device_id=to,
                                                device_id_type=pl.DeviceIdType.MESH)

        return (x, y, c), chips, copy

    def start(self, ins, outs, scr):
        send, recv = scr
        for t, (x_ref, out_ref) in enumerate(zip(ins, outs)):
            (x, y, c), chips, copy = self._copies(t, x_ref, out_ref, send, recv)
            copy(0, (x, y, c), (x, y, 1 - c), src=x_ref).start()
            for j, chip in enumerate(chips):
                copy(1 + j, (x, y, c), (*chip, c), src=x_ref).start()

    def finish(self, ins, outs, scr):
        send, recv = scr
        for t, (x_ref, out_ref) in enumerate(zip(ins, outs)):
            (x, y, c), chips, copy = self._copies(t, x_ref, out_ref, send, recv)
            passed = [copy(4 + j, (*chip, c), (x, y, 1 - c)) for j, chip in enumerate(chips)]
            for j, chip in enumerate(chips):
                copy(1 + j, (*chip, c), (x, y, c)).wait_recv()
                passed[j].start()
            copy(0, (x, y, 1 - c), (x, y, c)).wait_recv()
            for j, chip in enumerate(chips):
                copy(4 + j, (*chip, 1 - c), (x, y, c)).wait_recv()
            copy(0, (x, y, c), (x, y, 1 - c), src=x_ref).wait_send()
            for j, chip in enumerate(chips):
                copy(1 + j, (x, y, c), (*chip, c), src=x_ref).wait_send()
                passed[j].wait_send()

    def set_results(self, res):
        self.results = list(res)


N_CHIP = N_DEV // 2


class Halve:
    def __init__(self, tensors):
        self.operands = list(tensors)
        self.out_shape = [jax.ShapeDtypeStruct((N_CHIP,) + t.shape[1:], t.dtype) for t in tensors]
        n = len(tensors)
        self.scratch = [pltpu.SemaphoreType.DMA((n, N_CHIP)), pltpu.SemaphoreType.DMA((n, N_CHIP))]

    def _copies(self, t, x_ref, theirs_ref, scr):
        send, recv = scr
        x, y, c = _place()
        return [pltpu.make_async_remote_copy(
            src_ref=x_ref.at[2 * q + 1 - c], dst_ref=theirs_ref.at[q], send_sem=send.at[t, q],
            recv_sem=recv.at[t, q], device_id=(x, y, 1 - c), device_id_type=pl.DeviceIdType.MESH)
            for q in range(N_CHIP)]

    def start(self, ins, outs, scr):
        for t, (x_ref, theirs_ref) in enumerate(zip(ins, outs)):
            for give in self._copies(t, x_ref, theirs_ref, scr):
                give.start()

    def finish(self, ins, outs, scr):
        for t, (x_ref, theirs_ref) in enumerate(zip(ins, outs)):
            for give in self._copies(t, x_ref, theirs_ref, scr):
                give.wait_recv()
                give.wait_send()

    def set_results(self, res):
        self.results = list(res)


class Spread:
    def __init__(self, tensors):
        self.operands = list(tensors)
        self.out_shape = [jax.ShapeDtypeStruct(t.shape, t.dtype) for t in tensors]
        n = len(tensors)
        self.scratch = [pltpu.SemaphoreType.DMA((n, N_CHIP - 1)), pltpu.SemaphoreType.DMA((n, N_CHIP - 1))]

    def _copies(self, t, y_ref, out_ref, scr, outgoing):
        send, recv = scr
        x, y, c = _place()
        copies = []
        for k in range(1, N_CHIP):
            px, py = x ^ (k >> 1), y ^ (k & 1)
            copies.append(pltpu.make_async_remote_copy(
                src_ref=y_ref.at[2 * px + py], dst_ref=out_ref.at[2 * x + y if outgoing else 2 * px + py],
                send_sem=send.at[t, k - 1], recv_sem=recv.at[t, k - 1], device_id=(px, py, c),
                device_id_type=pl.DeviceIdType.MESH))
        return copies

    def start(self, ins, outs, scr):
        for t, (y_ref, out_ref) in enumerate(zip(ins, outs)):
            for give in self._copies(t, y_ref, out_ref, scr, True):
                give.start()

    def finish(self, ins, outs, scr):
        for t, (y_ref, out_ref) in enumerate(zip(ins, outs)):
            for take in self._copies(t, y_ref, out_ref, scr, False):
                take.wait_recv()
            for give in self._copies(t, y_ref, out_ref, scr, True):
                give.wait_send()

    def set_results(self, res):
        self.results = list(res)


class Joint:
    def __init__(self, parts):
        self.parts = list(parts)
        self.operands = [o for p in parts for o in p.operands]
        self.out_shape = [o for p in parts for o in p.out_shape]
        self.scratch = [o for p in parts for o in p.scratch]

    def _split(self, ins, outs, scr):
        i = o = s = 0
        for p in self.parts:
            ni, no, ns = len(p.operands), len(p.out_shape), len(p.scratch)
            yield p, ins[i:i + ni], outs[o:o + no], scr[s:s + ns]
            i, o, s = i + ni, o + no, s + ns

    def start(self, ins, outs, scr):
        for p, a, b, c in self._split(ins, outs, scr):
            p.start(a, b, c)

    def finish(self, ins, outs, scr):
        for p, a, b, c in self._split(ins, outs, scr):
            p.finish(a, b, c)

    def set_results(self, res):
        for p, _, part, _ in self._split([], list(res), []):
            p.set_results(part)


_HBM = pl.BlockSpec(memory_space=pltpu.HBM)


def run_alone(name, carry):
    n_in, n_out = len(carry.operands), len(carry.out_shape)

    def body(*refs):
        ins, outs, scr = refs[:n_in], refs[n_in:n_in + n_out], refs[n_in + n_out:]
        carry.start(ins, outs, scr)
        carry.finish(ins, outs, scr)

    res = pl.pallas_call(body, name=name, in_specs=[_HBM] * n_in, out_specs=[_HBM] * n_out,
                         out_shape=carry.out_shape, scratch_shapes=carry.scratch)(*carry.operands)
    carry.set_results(list(res))


def _call(name, body, *, grid, in_specs, out_specs, out_shape, args, scratch=(), carry=None):
    in_specs, out_specs, out_shape, scratch = list(in_specs), list(out_specs), list(out_shape), list(scratch)
    if carry is None:
        return list(pl.pallas_call(body, name=name, grid=grid, in_specs=in_specs, out_specs=out_specs,
                                   out_shape=out_shape, scratch_shapes=scratch, compiler_params=_cparams())(*args))
    n_in, n_out, n_scr = len(in_specs), len(out_shape), len(scratch)
    c_in, c_out = len(carry.operands), len(carry.out_shape)

    def wrapped(*refs):
        ins, c_ins = refs[:n_in], refs[n_in:n_in + c_in]
        at = n_in + c_in
        outs, c_outs = refs[at:at + n_out], refs[at + n_out:at + n_out + c_out]
        at += n_out + c_out
        scr, c_scr = refs[at:at + n_scr], refs[at + n_scr:]
        first = functools.reduce(jnp.logical_and, [pl.program_id(d) == 0 for d in range(len(grid))])
        last = functools.reduce(jnp.logical_and, [pl.program_id(d) == grid[d] - 1 for d in range(len(grid))])

        @pl.when(first)
        def _():
            carry.start(c_ins, c_outs, c_scr)

        body(*ins, *outs, *scr)

        @pl.when(last)
        def _():
            carry.finish(c_ins, c_outs, c_scr)

    res = pl.pallas_call(
        wrapped, name=name, grid=grid, in_specs=in_specs + [_HBM] * c_in, out_specs=out_specs + [_HBM] * c_out,
        out_shape=out_shape + carry.out_shape, scratch_shapes=scratch + carry.scratch, compiler_params=_cparams(),
    )(*args, *carry.operands)
    carry.set_results(list(res[n_out:]))
    return list(res[:n_out])


def R_(arr, w=None, off=0, stride=0):
    return ('r', arr, arr.shape[1] if w is None else w, off, stride)


def P_(arr, w=None, off=0, stride=0):
    return ('p', arr, arr.shape[1] if w is None else w, off, stride)


def rowmap(name, fn, ins, outs=None, *, tile, ncol=1, rows_inner=True, cts=None, wrt=None, gdt=None, cat=False,
           carry=None):
    rows = next(a.shape[0] for k, a, *_ in ins if k == 'r')
    nrow = rows // tile
    assert nrow * tile == rows
    grid = (ncol, nrow) if rows_inner else (nrow, ncol)

    def ij(g0, g1):
        return (g1, g0) if rows_inner else (g0, g1)

    def spec(kind, arr, w, off, stride):
        if kind == 'r':
            return pl.BlockSpec((tile, w), lambda g0, g1: (ij(g0, g1)[0], off + stride * ij(g0, g1)[1]))
        return pl.BlockSpec((arr.shape[0], w), lambda g0, g1: (0, off + stride * ij(g0, g1)[1]))

    ops = list(ins) + list(cts or [])
    in_specs = [spec(*o) for o in ops]
    n_in = len(ins)
    fwd = cts is None
    out_shape, out_specs, acc = [], [], []
    if fwd:
        for dt, w, stride in outs:
            out_shape.append(jax.ShapeDtypeStruct((rows, w * (ncol if stride else 1)), dt))
            out_specs.append(spec('r', None, w, 0, stride))
            acc.append(None)
    elif cat:
        widths = [ins[i][2] for i in wrt]
        assert ncol == 1 and all(ins[i][0] == 'r' for i in wrt)
        out_shape.append(jax.ShapeDtypeStruct((rows, sum(widths)), gdt))
        out_specs.append(spec('r', None, sum(widths), 0, 0))
    else:
        for n, i in enumerate(wrt):
            kind, arr, w, off, stride = ins[i]
            width = w * (ncol if stride else 1)
            if kind == 'r':
                out_shape.append(jax.ShapeDtypeStruct((rows, width), gdt[n]))
                out_specs.append(spec('r', None, w, 0, 1 if stride else 0))
                shared = stride == 0 and ncol > 1
                assert not shared or (not rows_inner and gdt[n] == F32)
                acc.append('col' if shared else None)
            else:
                out_shape.append(jax.ShapeDtypeStruct((arr.shape[0], width), F32))
                out_specs.append(spec('p', arr, w, 0, 1 if stride else 0))
                assert rows_inner or ncol == 1
                acc.append('row')

    def body(*refs):
        i, j = ij(pl.program_id(0), pl.program_id(1))
        vals = [r[...].astype(F32) for r in refs[:n_in]]
        out_refs = refs[len(ops):]
        if fwd:
            for o_ref, o in zip(out_refs, fn(*vals)):
                o_ref[...] = o.astype(o_ref.dtype)
            return

        def f(*d):
            full = list(vals)
            for n, idx in enumerate(wrt):
                full[idx] = d[n]
            return fn(*full)

        _, vjp = jax.vjp(f, *[vals[idx] for idx in wrt])
        grads = vjp(tuple(r[...].astype(F32) for r in refs[n_in:len(ops)]))
        if cat:
            o_ref, at = out_refs[0], 0
            for g in grads:
                o_ref[:, at:at + g.shape[1]] = g.astype(o_ref.dtype)
                at += g.shape[1]
            return
        for o_ref, g, a in zip(out_refs, grads, acc):
            if a is None:
                o_ref[...] = g.astype(o_ref.dtype)
            else:
                first = (i if a == 'row' else j) == 0

                @pl.when(first)
                def _(o_ref=o_ref):
                    o_ref[...] = jnp.zeros_like(o_ref)

                o_ref[...] += g

    return _call(name, body, grid=grid, in_specs=in_specs, out_specs=out_specs, out_shape=out_shape,
                 args=[o[1] for o in ops], carry=carry)


def _rms(x, g):
    return x * lax.rsqrt(jnp.mean(x * x, axis=-1, keepdims=True) + EPS) * g


_DIMS = {'nn': (((1,), (0,)), ((), ())), 'nt': (((1,), (1,)), ((), ())), 'tn': (((0,), (0,)), ((), ()))}


def mm(name, a, b, mode, *, jdim=None, out_dtype=F32, carry=None):
    a_list = list(a) if isinstance(a, (list, tuple)) else [a]
    b_list = list(b) if isinstance(b, (list, tuple)) else [b]
    a_order = ('k', 'm') if mode == 'tn' else ('m', 'k')
    b_order = ('n', 'k') if mode == 'nt' else ('k', 'n')
    size, blocks = {}, 1
    for arr, order in ((a_list[0], a_order), (b_list[0], b_order)):
        shape = arr.shape
        if jdim in order:
            blocks, shape = shape[0], shape[1:]
        for dname, extent in zip(order, shape):
            assert size.setdefault(dname, extent) == extent
    tile = {'m': _pick(size['m'], (1024, 512, 256, 128)), 'n': _pick(size['n'], (512, 256, 128)),
            'k': size['k'] if size['k'] <= 2048 else _pick(size['k'], (2304, 2048, 1024, 512, 256, 128))}
    if jdim is not None:
        tile[jdim] = size[jdim]
    if a_list[0].dtype == F32 and tile['k'] * tile['m'] > (1 << 20):
        tile['m'] = _pick(size['m'], (512, 256, 128))
    grid = tuple(blocks if d == jdim else size[d] // tile[d] for d in ('m', 'n', 'k'))
    nk = grid[2]

    def spec(order):
        shape = tuple(tile[d] for d in order)

        def imap(i, j, k):
            g = {'m': i, 'n': j, 'k': k}
            idx = tuple(0 if d == jdim else g[d] for d in order)
            return ((g[jdim],) + idx) if jdim in order else idx

        return pl.BlockSpec(((None,) + shape) if jdim in order else shape, imap)

    dims = _DIMS[mode]
    nt = len(a_list)

    def product(refs):
        acc = None
        for a_ref, b_ref in zip(refs[:nt], refs[nt:2 * nt]):
            p = lax.dot_general(a_ref[...].astype(BF16), b_ref[...].astype(BF16), dims, preferred_element_type=F32)
            acc = p if acc is None else acc + p
        return acc

    def body_once(*refs):
        refs[2 * nt][...] = product(refs).astype(refs[2 * nt].dtype)

    def body_acc(*refs):
        o_ref, acc_ref = refs[2 * nt], refs[2 * nt + 1]
        k = pl.program_id(2)

        @pl.when(k == 0)
        def _():
            acc_ref[...] = jnp.zeros_like(acc_ref)

        acc_ref[...] += product(refs)

        @pl.when(k == nk - 1)
        def _():
            o_ref[...] = acc_ref[...].astype(o_ref.dtype)

    out_dims = (size['m'], size['n'])
    out_shape = jax.ShapeDtypeStruct(((blocks,) + out_dims) if jdim in ('m', 'n') else out_dims, out_dtype)
    return _call(name, body_once if nk == 1 else body_acc, grid=grid,
                 in_specs=[spec(a_order)] * nt + [spec(b_order)] * nt, out_specs=[spec(('m', 'n'))],
                 out_shape=[out_shape], args=a_list + b_list,
                 scratch=[] if nk == 1 else [pltpu.VMEM((tile['m'], tile['n']), F32)], carry=carry)[0]


def _probs(q, k, i, tq, scale, causal):
    s = lax.dot_general(q, k, _DIMS['nt'], preferred_element_type=F32) * scale
    if causal:
        shape = s.shape
        q_chunk = jnp.right_shift(i * tq + lax.broadcasted_iota(jnp.int32, shape, 0), CHUNK_SHIFT)
        k_chunk = jnp.right_shift(lax.broadcasted_iota(jnp.int32, shape, 1), CHUNK_SHIFT)
        s = jnp.where(k_chunk <= q_chunk, s, -jnp.inf)
    e = jnp.exp(s - jnp.max(s, axis=-1, keepdims=True))
    return e, jnp.sum(e, axis=-1, keepdims=True)


def _per_prefix(work, i, tq, sq, sk, causal):
    if not causal:
        work(sk)
        return
    assert sq == sk and tq % CHUNK == 0
    for j in range(sq // tq):
        @pl.when(i == j)
        def _(j=j):
            work((j + 1) * tq)


def _attn_specs(tq, sk, dq, dv, koff, kstride, voff, vstride):
    return [pl.BlockSpec((tq, dq), lambda h, i: (i, h)),
            pl.BlockSpec((sk, dq), lambda h, i: (0, koff + kstride * h)),
            pl.BlockSpec((sk, dv), lambda h, i: (0, voff + vstride * h))]


def attn_fwd(name, q, k, v, *, heads, dq, dv, koff, kstride, voff, vstride, scale, causal, carry=None):
    sq, sk = q.shape[0], k.shape[0]
    tq = min(sq, 256)

    def body(q_ref, k_ref, v_ref, o_ref):
        i = pl.program_id(1)

        def work(keys):
            e, l = _probs(q_ref[...].astype(BF16), k_ref[0:keys, :].astype(BF16), i, tq, scale, causal)
            o = jnp.dot(e.astype(BF16), v_ref[0:keys, :].astype(BF16), preferred_element_type=F32)
            o_ref[...] = o / l

        _per_prefix(work, i, tq, sq, sk, causal)

    return _call(name, body, grid=(heads, sq // tq),
                 in_specs=_attn_specs(tq, sk, dq, dv, koff, kstride, voff, vstride),
                 out_specs=[pl.BlockSpec((tq, dv), lambda h, i: (i, h))],
                 out_shape=[jax.ShapeDtypeStruct((sq, heads * dv), F32)], args=[q, k, v], carry=carry)[0]


def attn_bwd(name, q, k, v, do, *, heads, dq, dv, koff, kstride, voff, vstride, scale, causal, carry=None):
    sq, sk = q.shape[0], k.shape[0]
    tq = min(sq, 256)

    def body(q_ref, k_ref, v_ref, do_ref, dq_ref, dk_ref, dv_ref):
        i = pl.program_id(1)

        @pl.when(i == 0)
        def _():
            dk_ref[...] = jnp.zeros_like(dk_ref)
            dv_ref[...] = jnp.zeros_like(dv_ref)

        def work(keys):
            qb, kb, vb = q_ref[...].astype(BF16), k_ref[0:keys, :].astype(BF16), v_ref[0:keys, :].astype(BF16)
            dob = do_ref[...].astype(BF16)
            e, l = _probs(qb, kb, i, tq, scale, causal)
            p = e / l
            dp = lax.dot_general(dob, vb, _DIMS['nt'], preferred_element_type=F32)
            ds = (p * (dp - jnp.sum(dp * p, axis=-1, keepdims=True)) * scale).astype(BF16)
            dv_ref[0:keys, :] += lax.dot_general(p.astype(BF16), dob, _DIMS['tn'], preferred_element_type=F32)
            dk_ref[0:keys, :] += lax.dot_general(ds, qb, _DIMS['tn'], preferred_element_type=F32)
            dq_ref[...] = jnp.dot(ds, kb, preferred_element_type=F32)

        _per_prefix(work, i, tq, sq, sk, causal)

    return _call(
        name, body, grid=(heads, sq // tq),
        in_specs=_attn_specs(tq, sk, dq, dv, koff, kstride, voff, vstride) + [pl.BlockSpec((tq, dv), lambda h, i: (i, h))],
        out_specs=[pl.BlockSpec((tq, dq), lambda h, i: (i, h)), pl.BlockSpec((sk, dq), lambda h, i: (0, h)),
                   pl.BlockSpec((sk, dv), lambda h, i: (0, h))],
        out_shape=[jax.ShapeDtypeStruct((sq, heads * dq), F32), jax.ShapeDtypeStruct((sk, heads * dq), F32),
                   jax.ShapeDtypeStruct((sk, heads * dv), F32)],
        args=[q, k, v, do], carry=carry)


def _hg_chunk(q, k, g, v, state):
    c = q.shape[0]
    row = lax.broadcasted_iota(jnp.int32, (c, c), 0)
    col = lax.broadcasted_iota(jnp.int32, (c, c), 1)
    tril = (col <= row).astype(F32)
    b = jnp.dot(tril, g, precision=lax.Precision.HIGHEST, preferred_element_type=F32)
    rows = lax.broadcasted_iota(jnp.int32, (c, 1), 0)
    o = jnp.dot((q * jnp.exp(b)).astype(BF16), state.astype(BF16), preferred_element_type=F32)
    t3 = lax.broadcasted_iota(jnp.int32, (SUB, SUB, 1), 0)
    s3 = lax.broadcasted_iota(jnp.int32, (SUB, SUB, 1), 1)
    parts = []
    for n in range(c // SUB):
        lo = n * SUB
        qn, kn, bn, vn = q[lo:lo + SUB], k[lo:lo + SUB], b[lo:lo + SUB], v[lo:lo + SUB]
        decay = jnp.exp(jnp.where(s3 <= t3, bn[:, None, :] - bn[None, :, :], -jnp.inf))
        sc = jnp.sum(qn[:, None, :] * kn[None, :, :] * decay, axis=-1)
        on = jnp.dot(sc.astype(BF16), vn.astype(BF16), preferred_element_type=F32)
        if n > 0:
            ref = jnp.sum(jnp.where(rows == lo - 1, b, 0.0), axis=0, keepdims=True)
            qd = (qn * jnp.exp(bn - ref)).astype(BF16)
            kd = (k[:lo] * jnp.exp(ref - b[:lo])).astype(BF16)
            so = lax.dot_general(qd, kd, _DIMS['nt'], preferred_element_type=F32)
            on = on + jnp.dot(so.astype(BF16), v[:lo].astype(BF16), preferred_element_type=F32)
        parts.append(on)
    o = o + jnp.concatenate(parts, axis=0)
    b_last = jnp.sum(g, axis=0, keepdims=True)
    ones = jnp.ones((c, 1), F32)
    b_last_col = lax.dot_general(g, ones, _DIMS['tn'], precision=lax.Precision.HIGHEST, preferred_element_type=F32)
    kd = (k * jnp.exp(b_last - b)).astype(BF16)
    new_state = jnp.exp(b_last_col) * state + lax.dot_general(kd, v.astype(BF16), _DIMS['tn'],
                                                              preferred_element_type=F32)
    return o, new_state


def hg_scan_fwd(name, q, k, g, u, v_off, carry=None):
    s = q.shape[0]
    n = s // CHUNK

    def body(q_ref, k_ref, g_ref, v_ref, o_ref, st_ref, state):
        @pl.when(pl.program_id(1) == 0)
        def _():
            state[...] = jnp.zeros_like(state)

        for j in range(HG_GROUP):
            cols = slice(j * HEAD_W, (j + 1) * HEAD_W)
            st = state[j]
            st_ref[j] = st
            o, new = _hg_chunk(q_ref[:, cols], k_ref[:, cols], g_ref[:, cols], v_ref[:, cols], st)
            o_ref[:, cols] = o
            state[j] = new

    wide = HG_GROUP * HEAD_W
    blk = pl.BlockSpec((CHUNK, wide), lambda h, c: (c, h))
    return _call(
        name, body, grid=(HG_HEADS // HG_GROUP, n),
        in_specs=[blk, blk, blk, pl.BlockSpec((CHUNK, wide), lambda h, c: (c, v_off // HG_GROUP + h))],
        out_specs=[blk, pl.BlockSpec((HG_GROUP, None, HEAD_W, HEAD_W), lambda h, c: (h, c, 0, 0))],
        out_shape=[jax.ShapeDtypeStruct((s, HG_HEADS * HEAD_W), F32),
                   jax.ShapeDtypeStruct((HG_HEADS, n, HEAD_W, HEAD_W), F32)],
        scratch=[pltpu.VMEM((HG_GROUP, HEAD_W, HEAD_W), F32)], args=[q, k, g, u], carry=carry)


def hg_scan_bwd(name, q, k, g, u, v_off, states, do, carry=None):
    s = q.shape[0]
    n = s // CHUNK

    def body(q_ref, k_ref, g_ref, v_ref, st_ref, do_ref, dq_ref, dk_ref, dg_ref, dv_ref, dstate):
        @pl.when(pl.program_id(1) == 0)
        def _():
            dstate[...] = jnp.zeros_like(dstate)

        for j in range(HG_GROUP):
            cols = slice(j * HEAD_W, (j + 1) * HEAD_W)
            _, vjp = jax.vjp(_hg_chunk, q_ref[:, cols], k_ref[:, cols], g_ref[:, cols], v_ref[:, cols], st_ref[j])
            dq, dk, dg, dv, dst = vjp((do_ref[:, cols], dstate[j]))
            dq_ref[:, cols] = dq
            dk_ref[:, cols] = dk
            dg_ref[:, cols] = dg
            dv_ref[:, cols] = dv
            dstate[j] = dst

    wide = HG_GROUP * HEAD_W
    blk = pl.BlockSpec((CHUNK, wide), lambda h, c: (n - 1 - c, h))
    out = jax.ShapeDtypeStruct((s, HG_HEADS * HEAD_W), F32)
    return _call(
        name, body, grid=(HG_HEADS // HG_GROUP, n),
        in_specs=[blk, blk, blk, pl.BlockSpec((CHUNK, wide), lambda h, c: (n - 1 - c, v_off // HG_GROUP + h)),
                  pl.BlockSpec((HG_GROUP, None, HEAD_W, HEAD_W), lambda h, c: (h, n - 1 - c, 0, 0)), blk],
        out_specs=[blk, blk, blk, blk], out_shape=[out, out, out, out],
        scratch=[pltpu.VMEM((HG_GROUP, HEAD_W, HEAD_W), F32)], args=[q, k, g, u, states, do], carry=carry)


def pair_sum(name, blocks, theirs):
    _, rows, cols = theirs.shape
    tile = next((t for t in (1024, 512, 256, 128, 64, 32, 16) if rows % t == 0 and t * cols <= (1 << 20)), rows)

    def body(a_ref, b_ref, o_ref):
        mine = jnp.where(lax.axis_index("c") == 0, a_ref[0].astype(F32), a_ref[1].astype(F32))
        o_ref[...] = (mine + b_ref[...].astype(F32)).astype(o_ref.dtype)

    blk = pl.BlockSpec((None, tile, cols), lambda q, i: (q, i, 0))
    return _call(name, body, grid=(N_CHIP, rows // tile),
                 in_specs=[pl.BlockSpec((None, 2, tile, cols), lambda q, i: (q, 0, i, 0)), blk], out_specs=[blk],
                 out_shape=[jax.ShapeDtypeStruct(theirs.shape, theirs.dtype)],
                 args=[blocks.reshape(N_CHIP, 2, rows, cols), theirs])[0]


def adamw(name, landed, w, m, v):
    rows, cols = w.shape
    slots = landed.shape[0]
    tile = next((t for t in (512, 256, 128, 64, 32, 16, 8) if rows % t == 0 and t * cols <= (1 << 18)), rows)

    def body(l_ref, w_ref, m_ref, v_ref, g_ref, d_ref, nm_ref, nv_ref):
        gv = l_ref[0].astype(F32)
        for s in range(1, slots):
            gv = gv + l_ref[s].astype(F32)
        nm = ADAM_B1 * m_ref[...] + (1.0 - ADAM_B1) * gv
        nv = ADAM_B2 * v_ref[...] + (1.0 - ADAM_B2) * jnp.square(gv)
        m_hat = nm / (1.0 - ADAM_B1 ** ADAM_STEP)
        v_hat = nv / (1.0 - ADAM_B2 ** ADAM_STEP)
        g_ref[...] = gv
        d_ref[...] = -ADAM_LR * (m_hat / (jnp.sqrt(v_hat) + ADAM_EPS) + ADAM_WD * w_ref[...])
        nm_ref[...] = nm
        nv_ref[...] = nv

    blk = pl.BlockSpec((tile, cols), lambda i: (i, 0))
    out = jax.ShapeDtypeStruct((rows, cols), F32)
    return _call(name, body, grid=(rows // tile,),
                 in_specs=[pl.BlockSpec((slots, tile, cols), lambda i: (0, i, 0)), blk, blk, blk],
                 out_specs=[blk] * 4, out_shape=[out] * 4, args=[landed, w, m, v])


def _swap_halves(pe):
    half = QK_ROPE // 2
    return jnp.concatenate([-pe[..., half:], pe[..., :half]], axis=-1)


def _unswap_halves(dsw):
    half = QK_ROPE // 2
    return jnp.concatenate([dsw[..., half:], -dsw[..., :half]], axis=-1)


def _w_in_ext(w):
    kpe = w[:, 9216:9280]
    z = jnp.zeros((w.shape[0], HEAD_W - QK_ROPE), w.dtype)
    pad = jnp.zeros((w.shape[0], U_PAD), w.dtype)
    return jnp.concatenate([w[:, :9216], w[:, 9280:], kpe, z, _swap_halves(kpe), z, pad], axis=1)


def _w_in_grad(d):
    dkpe = d[:, 13312:13376] + _unswap_halves(d[:, 13440:13504])
    return jnp.concatenate([d[:, :9216], dkpe, d[:, 9216:13312]], axis=1)


def _w_q_ext(w):
    w3 = w.reshape(Q_LORA, MLA_HEADS, MLA_QK)
    pe = w3[:, :, HEAD_W:]
    z = jnp.zeros((Q_LORA, MLA_HEADS, HEAD_W - QK_ROPE), w.dtype)
    wide = MLA_HEADS * HEAD_W
    return jnp.concatenate([w3[:, :, :HEAD_W].reshape(Q_LORA, wide),
                            jnp.concatenate([pe, z], axis=2).reshape(Q_LORA, wide),
                            jnp.concatenate([_swap_halves(pe), z], axis=2).reshape(Q_LORA, wide)], axis=1)


def _w_q_grad(parts):
    d3 = [p.reshape(Q_LORA, MLA_HEADS, HEAD_W) for p in parts]
    dpe = d3[1][:, :, :QK_ROPE] + _unswap_halves(d3[2][:, :, :QK_ROPE])
    return jnp.concatenate([d3[0], dpe], axis=2).reshape(Q_LORA, MLA_HEADS * MLA_QK)


def _hg_prep(f_raw, q_hg, logits):
    lb = jax.nn.softmax(logits, axis=0)[0:1, :]
    log_f = jnp.logaddexp(jnp.log(lb), jnp.log1p(-lb) + jax.nn.log_sigmoid(f_raw))
    k_in = (1.0 - lb) * jax.nn.sigmoid(-f_raw)
    return log_f, k_in, jax.nn.silu(q_hg)


def _rope(q_nope, q_pe, q_sw, k_nope, k_pe, k_sw, cos, sin):
    qf = jnp.concatenate([q_nope, q_pe * cos + q_sw * sin], axis=1)
    kf = jnp.concatenate([k_nope, k_pe * cos + k_sw * sin], axis=1)
    return qf, kf


def _step(a):
    x, mem, target = a['x'][0], a['mem'][0], a['loss_target'][0]
    s, d = x.shape
    nm = mem.shape[0]
    ff = N_DEV * a['ffn1_w_gate'].shape[-1]
    cs = ff // N_DEV
    tr = min(s, 256)
    tp = min(s, 128)
    th = s
    ta = 1024

    bf = {n: a[n][0].astype(BF16) for n in BIG}
    half_in = bf['w_in'].shape[0] // 2
    bf['w_in#0'], bf['w_in#1'] = bf['w_in'][:half_in], bf['w_in'][half_in:]
    gat, full = {}, {}

    my_chip = 2 * lax.axis_index("x") + lax.axis_index("y")
    me = 2 * my_chip + lax.axis_index("c")

    def gathered(names, carry):
        for n, g8 in zip(names or [], carry.results if carry else []):
            r, c = bf[n].shape
            g8 = lax.dynamic_update_index_in_dim(g8, bf[n], me, 0)
            gat[n] = g8
            if not n.startswith('ffn'):
                by_col = BIG[n.split('#')[0]]
                full[n] = g8.transpose(1, 0, 2).reshape(r, N_DEV * c) if by_col else g8.reshape(N_DEV * r, c)

    xa_names = ['xa_w_q', 'xa_w_k', 'xa_w_v', 'xa_w_o']
    ffn2_names = ['ffn2_w_gate', 'ffn2_w_up', 'ffn2_w_down']
    first = Gather([bf['ffn1_w_gate']])
    run_alone("gather_first", first)
    gathered(['ffn1_w_gate'], first)

    inv_freq = 1.0 / (ROPE_THETA ** (jnp.arange(0, QK_ROPE, 2, dtype=F32) / QK_ROPE))
    ang = a['positions'][0].astype(F32)[:, None] * inv_freq
    zero = jnp.zeros((s, HEAD_W - QK_ROPE), F32)
    cos = jnp.concatenate([jnp.cos(ang), jnp.cos(ang), zero], axis=1)
    sin = jnp.concatenate([jnp.sin(ang), jnp.sin(ang), zero], axis=1)

    gs = {}
    gb = {}

    def by_rows(g):
        return g.reshape(N_DEV, g.shape[0] // N_DEV, g.shape[1])

    def by_cols(g):
        return g.reshape(g.shape[0], N_DEV, g.shape[1] // N_DEV).transpose(1, 0, 2)

    pre_fn = lambda xv, g: (_rms(xv, g),)
    pre_res_fn = lambda xv, g: (_rms(xv, g), xv)

    def pre_norm(tag, x_in, g):
        return rowmap(tag + "_pre", pre_fn, [R_(x_in), P_(g)], [(BF16, x_in.shape[1], 0)], tile=tr)[0]

    def pre_norm_bwd(tag, x_in, g, dh, d_out, carry=None):
        return rowmap(tag + "_pre_bwd", pre_res_fn, [R_(x_in), P_(g)], tile=tr, cts=[R_(dh), R_(d_out)],
                      wrt=[0, 1], gdt=[F32, F32], carry=carry)

    def post_fn(weight):
        return lambda xv, y, g: (xv + weight * _rms(y, g),)

    def post_norm(tag, x_in, y, g, weight):
        return rowmap(tag + "_post", post_fn(weight), [R_(x_in), R_(y), P_(g)], [(F32, d, 0)], tile=tr)[0]

    def post_norm_bwd(tag, x_in, y, g, weight, d_out):
        return rowmap(tag + "_post_bwd", post_fn(weight), [R_(x_in), R_(y), P_(g)], tile=tr, cts=[R_(d_out)],
                      wrt=[1, 2], gdt=[BF16, F32])

    act_fn = lambda av, bv: (jax.nn.silu(av) * bv,)

    def fetch(plan, key):
        names = plan.get(key)
        return (names, Gather([bf[n] for n in names])) if names else (None, None)

    def send(plan, key):
        acts = plan.get(key)
        if not acts:
            return None, None
        parts = []
        for kind, names in acts:
            if kind == 'spread':
                for n in names:
                    chip_sums[n] = pair_sum("pair_" + n.replace('#', '_'), gb[n], halves[n])
            parts.append(Halve([gb[n] for n in names]) if kind == 'halve' else Spread([chip_sums[n] for n in names]))
        return acts, Joint(parts)

    def ffn_fwd(tag, x_in, plan):
        h = pre_norm(tag, x_in, a[tag + '_pre_g'])
        names, carry = fetch(plan, 'gate')
        av = mm(tag + "_gate", h, gat[tag + '_w_gate'], 'nn', jdim='n', out_dtype=BF16, carry=carry)
        gathered(names, carry)
        names, carry = fetch(plan, 'up')
        bv = mm(tag + "_up", h, gat[tag + '_w_up'], 'nn', jdim='n', out_dtype=BF16, carry=carry)
        gathered(names, carry)
        a2, b2 = av.reshape(N_DEV * s, cs), bv.reshape(N_DEV * s, cs)
        names, carry = fetch(plan, 'act')
        z = rowmap(tag + "_act", act_fn, [R_(a2), R_(b2)], [(BF16, cs, 0)], tile=ta, carry=carry)[0]
        z = z.reshape(N_DEV, s, cs)
        gathered(names, carry)
        names, carry = fetch(plan, 'dn')
        y = mm(tag + "_dn", z, gat[tag + '_w_down'], 'nn', jdim='k', carry=carry)
        gathered(names, carry)
        return post_norm(tag, x_in, y, a[tag + '_post_g'], 0.5), (x_in, h, a2, b2, z, y)

    def ffn_bwd(tag, res, d_out, plan):
        x_in, h, a2, b2, z, y = res
        dy, gs[tag + '_post_g'] = post_norm_bwd(tag, x_in, y, a[tag + '_post_g'], 0.5, d_out)
        acts, carry = send(plan, 'dn_dx')
        dz = mm(tag + "_dn_dx", dy, gat[tag + '_w_down'], 'nt', jdim='n', carry=carry)
        exchanged(acts, carry)
        acts, carry = send(plan, 'dn_dw')
        gb[tag + '_w_down'] = mm(tag + "_dn_dw", z, dy, 'tn', jdim='m', out_dtype=BF16, carry=carry)
        exchanged(acts, carry)
        acts, carry = send(plan, 'act_bwd')
        da, db = rowmap(tag + "_act_bwd", act_fn, [R_(a2), R_(b2)], tile=ta, cts=[R_(dz.reshape(N_DEV * s, cs))],
                        wrt=[0, 1], gdt=[BF16, BF16], carry=carry)
        exchanged(acts, carry)
        da, db = da.reshape(N_DEV, s, cs), db.reshape(N_DEV, s, cs)
        names, carry = send(plan, 'gu_dx')
        dh = mm(tag + "_gu_dx", [da, db], [gat[tag + '_w_gate'], gat[tag + '_w_up']], 'nt', jdim='k', carry=carry)
        exchanged(names, carry)
        gb[tag + '_w_gate'] = mm(tag + "_gate_dw", h, da, 'tn', jdim='n', out_dtype=BF16)
        names, carry = send(plan, 'up_dw')
        gb[tag + '_w_up'] = mm(tag + "_up_dw", h, db, 'tn', jdim='n', out_dtype=BF16, carry=carry)
        exchanged(names, carry)
        names, carry = send(plan, 'pre_bwd')
        d_in, gs[tag + '_pre_g'] = pre_norm_bwd(tag, x_in, a[tag + '_pre_g'], dh, d_out, carry)
        exchanged(names, carry)
        return d_in

    hg_out_fn = lambda o, og, g: (_rms(o, g) * jax.nn.silu(og),)
    mla_norm_fn = lambda cq, ckv, gq, gkv: (_rms(cq, gq), _rms(ckv, gkv))
    gate_fn = lambda ga, gb, ya, yb: (jax.nn.sigmoid(ga) * ya + jax.nn.sigmoid(gb) * yb,)
    mla = dict(heads=MLA_HEADS, dq=2 * HEAD_W, dv=HEAD_W, koff=0, kstride=1, voff=1, vstride=2,
               scale=MLA_QK ** -0.5, causal=True)

    def mix_fwd(x_in, plan):
        w_in = _w_in_ext(jnp.concatenate([full['w_in#0'], full['w_in#1']], axis=0))
        h = pre_norm("mix", x_in, a['mix_pre_g'])
        names, carry = fetch(plan, 'in')
        u = mm("mix_in", h, w_in, 'nn', carry=carry)
        gathered(names, carry)
        w_q, w_kv = _w_q_ext(full['mla_w_q_up']), full['mla_w_kv_up']
        hg_ins = [R_(u, 2048, 1), R_(u, 2048, 0), P_(a['hgrn_lb_logits'])]
        log_f, k_in, q_in = rowmap("hg_prep", _hg_prep, hg_ins, [(F32, 2048, 0)] * 3, tile=tp)
        names, carry = fetch(plan, 'scan')
        o_a, states = hg_scan_fwd("hg_scan", q_in, k_in, log_f, u, 32, carry=carry)
        gathered(names, carry)
        out_ins = [R_(o_a, HEAD_W, 0, 1), R_(u, HEAD_W, 48, 1), P_(a['hg_norm_g'], HEAD_W, 0, 1)]
        oag = rowmap("hg_out", hg_out_fn, out_ins, [(BF16, HEAD_W, 1)], tile=th, ncol=HG_HEADS)[0]
        y_a = mm("mix_a", oag, full['w_branch_a'], 'nn')
        norm_ins = [R_(u, 512, 16), R_(u, 512, 17), P_(a['mla_q_norm_g']), P_(a['mla_kv_norm_g'])]
        cqn, ckvn = rowmap("mla_norm", mla_norm_fn, norm_ins, [(BF16, 512, 0)] * 2, tile=tr)
        q_all = mm("mla_qup", cqn, w_q, 'nn')
        kv = mm("mla_kvup", ckvn, w_kv, 'nn')
        rope_ins = [R_(q_all, HEAD_W, 0, 1), R_(q_all, HEAD_W, 16, 1), R_(q_all, HEAD_W, 32, 1),
                    R_(kv, HEAD_W, 0, 2), R_(u, HEAD_W, 104, 0), R_(u, HEAD_W, 105, 0), R_(cos), R_(sin)]
        qf, kf = rowmap("mla_rope", _rope, rope_ins, [(BF16, 2 * HEAD_W, 1)] * 2, tile=th, ncol=MLA_HEADS,
                        rows_inner=False)
        names, carry = fetch(plan, 'attn')
        o_b = attn_fwd("mla_attn", qf, kf, kv, carry=carry, **mla)
        gathered(names, carry)
        y_b = mm("mix_b", o_b, full['w_branch_b'], 'nn')
        gate_ins = [R_(u, 1024, 9, 1), R_(u, 1024, 11, 1), R_(y_a, 1024, 0, 1), R_(y_b, 1024, 0, 1)]
        y = rowmap("mix_gate", gate_fn, gate_ins, [(BF16, 1024, 1)], tile=tr, ncol=2)[0]
        yo = mm("mix_out", y, full['w_out'], 'nn')
        res = (x_in, h, hg_ins, q_in, k_in, log_f, u, states, out_ins, oag, norm_ins, cqn, ckvn, rope_ins, qf, kf,
               kv, o_b, gate_ins, y, yo, w_in, w_q, w_kv)
        return post_norm("mix", x_in, yo, a['mix_post_g'], 1.0), res

    def mix_bwd(res, d_out, plan):
        (x_in, h, hg_ins, q_in, k_in, log_f, u, states, out_ins, oag, norm_ins, cqn, ckvn, rope_ins, qf, kf, kv,
         o_b, gate_ins, y, yo, w_in, w_q, w_kv) = res
        dyo, gs['mix_post_g'] = post_norm_bwd("mix", x_in, yo, a['mix_post_g'], 1.0, d_out)
        dy = mm("mix_out_dx", dyo, full['w_out'], 'nt')
        gb['w_out'] = by_rows(mm("mix_out_dw", y, dyo, 'tn', out_dtype=BF16))
        dga, dgb, dya, dyb = rowmap("mix_gate_bwd", gate_fn, gate_ins, tile=tr, ncol=2, cts=[R_(dy, 1024, 0, 1)],
                                    wrt=[0, 1, 2, 3], gdt=[BF16] * 4)
        gb['w_branch_b'] = by_rows(mm("mix_b_dw", o_b, dyb, 'tn', out_dtype=BF16))
        do_b = mm("mix_b_dx", dyb, full['w_branch_b'], 'nt')
        names, carry = send(plan, 'attn_bwd')
        dqf, dkf, dv = attn_bwd("mla_attn_bwd", qf, kf, kv, do_b, carry=carry, **mla)
        exchanged(names, carry)
        dqn, dqp, dqs, dkn, dkpe, dksw = rowmap(
            "mla_rope_bwd", _rope, rope_ins, tile=th, ncol=MLA_HEADS, rows_inner=False,
            cts=[R_(dqf, 2 * HEAD_W, 0, 1), R_(dkf, 2 * HEAD_W, 0, 1)], wrt=[0, 1, 2, 3, 4, 5],
            gdt=[BF16, BF16, BF16, BF16, F32, F32])
        wide = MLA_HEADS * HEAD_W
        dq_parts = [dqn, dqp, dqs]
        gb['mla_w_q_up'] = by_cols(_w_q_grad([mm("mla_qup_dw%d" % n, cqn, dq_parts[n], 'tn', out_dtype=BF16)
                                              for n in range(3)]))
        dcqn = mm("mla_qup_dx", dq_parts, [w_q[:, n * wide:(n + 1) * wide] for n in range(3)], 'nt')
        w_kv4 = w_kv.reshape(KV_LORA, MLA_HEADS, 2, HEAD_W)
        dw_k = mm("mla_kup_dw", ckvn, dkn, 'tn', out_dtype=BF16).reshape(KV_LORA, MLA_HEADS, 1, HEAD_W)
        dw_v = mm("mla_vup_dw", ckvn, dv, 'tn', out_dtype=BF16).reshape(KV_LORA, MLA_HEADS, 1, HEAD_W)
        gb['mla_w_kv_up'] = by_cols(jnp.concatenate([dw_k, dw_v], axis=2).reshape(KV_LORA, 2 * wide))
        dckvn = mm("mla_kvup_dx", [dkn, dv],
                   [w_kv4[:, :, 0].reshape(KV_LORA, wide), w_kv4[:, :, 1].reshape(KV_LORA, wide)], 'nt')
        dcq, dckv, gs['mla_q_norm_g'], gs['mla_kv_norm_g'] = rowmap(
            "mla_norm_bwd", mla_norm_fn, norm_ins, tile=tr, cts=[R_(dcqn), R_(dckvn)], wrt=[0, 1, 2, 3],
            gdt=[BF16, BF16, F32, F32])
        gb['w_branch_a'] = by_rows(mm("mix_a_dw", oag, dya, 'tn', out_dtype=BF16))
        doag = mm("mix_a_dx", dya, full['w_branch_a'], 'nt')
        acts, carry = send(plan, 'out_bwd')
        do_a, dog, gs['hg_norm_g'] = rowmap("hg_out_bwd", hg_out_fn, out_ins, tile=th, ncol=HG_HEADS,
                                            cts=[R_(doag, HEAD_W, 0, 1)], wrt=[0, 1, 2], gdt=[F32, BF16, F32],
                                            carry=carry)
        exchanged(acts, carry)
        names, carry = send(plan, 'scan_bwd')
        dq_in, dk_in, dlog_f, di = hg_scan_bwd("hg_scan_bwd", q_in, k_in, log_f, u, 32, states, do_a, carry=carry)
        exchanged(names, carry)
        df, dq_hg, gs['hgrn_lb_logits'] = rowmap("hg_prep_bwd", _hg_prep, hg_ins, tile=tp,
                                                 cts=[R_(dlog_f), R_(dk_in), R_(dq_in)], wrt=[0, 1, 2],
                                                 gdt=[BF16, BF16, F32])
        du = jnp.concatenate([dq_hg, df, di.astype(BF16), dog, dcq, dckv, dga, dgb, dkpe.astype(BF16),
                              dksw.astype(BF16), jnp.zeros((s, U_PAD), BF16)], axis=1)
        g_in = by_cols(_w_in_grad(mm("mix_in_dw", h, du, 'tn', out_dtype=BF16)))
        gb['w_in#0'], gb['w_in#1'] = g_in[:, :half_in], g_in[:, half_in:]
        names, carry = send(plan, 'in_dx')
        dh = mm("mix_in_dx", du, w_in, 'nt', carry=carry)
        exchanged(names, carry)
        d_in, gs['mix_pre_g'] = pre_norm_bwd("mix", x_in, a['mix_pre_g'], dh, d_out)
        return d_in

    xa = dict(heads=XA_HEADS, dq=HEAD_W, dv=HEAD_W, koff=0, kstride=1, voff=XA_HEADS, vstride=1,
              scale=HEAD_W ** -0.5, causal=False)
    tm_ = min(nm, 128)

    def xa_fwd(x_in):
        w_xkv = jnp.concatenate([full['xa_w_k'], full['xa_w_v']], axis=1)
        h = pre_norm("xa", x_in, a['xa_pre_g'])
        mn = rowmap("xa_mem", pre_fn, [R_(mem), P_(a['xa_mem_g'])], [(BF16, d, 0)], tile=tm_)[0]
        q = mm("xa_q", h, full['xa_w_q'], 'nn')
        kv = mm("xa_kv", mn, w_xkv, 'nn')
        o = attn_fwd("xa_attn", q, kv, kv, **xa)
        yo = mm("xa_o", o, full['xa_w_o'], 'nn')
        return post_norm("xa", x_in, yo, a['xa_post_g'], 1.0), (x_in, h, mn, q, kv, o, yo, w_xkv)

    def xa_bwd(res, d_out, plan):
        x_in, h, mn, q, kv, o, yo, w_xkv = res
        dyo, gs['xa_post_g'] = post_norm_bwd("xa", x_in, yo, a['xa_post_g'], 1.0, d_out)
        do = mm("xa_o_dx", dyo, full['xa_w_o'], 'nt')
        gb['xa_w_o'] = by_cols(mm("xa_o_dw", o, dyo, 'tn', out_dtype=BF16))
        dq, dk, dv = attn_bwd("xa_attn_bwd", q, kv, kv, do, **xa)
        dkv = jnp.concatenate([dk, dv], axis=1).astype(BF16)
        dw = mm("xa_kv_dw", mn, dkv, 'tn', out_dtype=BF16)
        gb['xa_w_k'], gb['xa_w_v'] = by_rows(dw[:, :XA_HEADS * HEAD_W]), by_rows(dw[:, XA_HEADS * HEAD_W:])
        dmn = mm("xa_kv_dx", dkv, w_xkv, 'nt')
        gs['xa_mem_g'] = rowmap("xa_mem_bwd", pre_fn, [R_(mem), P_(a['xa_mem_g'])], tile=tm_, cts=[R_(dmn)],
                                wrt=[1], gdt=[F32])[0]
        gb['xa_w_q'] = by_rows(mm("xa_q_dw", h, dq, 'tn', out_dtype=BF16))
        dh = mm("xa_q_dx", dq, full['xa_w_q'], 'nt')
        acts, carry = send(plan, 'pre_bwd')
        d_in, gs['xa_pre_g'] = pre_norm_bwd("xa", x_in, a['xa_pre_g'], dh, d_out, carry)
        exchanged(acts, carry)
        return d_in

    landed = {}

    halves = {}
    chip_sums = {}

    def exchanged(acts, carry):
        for (kind, names), part in zip(acts or [], carry.parts if carry else []):
            for n, got in zip(names, part.results):
                if kind == 'halve':
                    halves[n] = got
                else:
                    own = lax.dynamic_index_in_dim(chip_sums[n], my_chip, 0, keepdims=True)
                    landed[n] = lax.dynamic_update_slice_in_dim(got, own, my_chip, 0)

    x1, r1 = ffn_fwd('ffn1', x, {'gate': ['ffn1_w_up'], 'up': ['ffn1_w_down'], 'act': ['w_in#0'], 'dn': ['w_in#1']})
    x2, r2 = mix_fwd(x1, {'in': ['mla_w_q_up', 'mla_w_kv_up', 'w_branch_a', 'w_branch_b', 'w_out'],
                          'scan': xa_names + ['ffn2_w_gate'], 'attn': ['ffn2_w_up']})
    x3, r3 = xa_fwd(x2)
    x4, r4 = ffn_fwd('ffn2', x3, {'gate': ['ffn2_w_down']})

    def loss_fn(y, t):
        diff = y - t
        return diff * (1.0 / d), jnp.mean(diff * diff, axis=-1, keepdims=True)

    d4, row_loss = rowmap("loss", loss_fn, [R_(x4), R_(target)], [(F32, d, 0), (F32, 1, 0)], tile=tr)
    loss = lax.psum(0.5 * jnp.sum(row_loss), ("x", "y", "c"))
    late = ['mla_w_q_up', 'mla_w_kv_up', 'w_branch_a']
    d3 = ffn_bwd('ffn2', r4, d4, {'act_bwd': [('halve', ['ffn2_w_down'])], 'gu_dx': [('spread', ['ffn2_w_down'])],
                                  'up_dw': [('halve', ['ffn2_w_gate'])], 'pre_bwd': [('halve', ['ffn2_w_up'])]})
    d2 = xa_bwd(r3, d3, {'pre_bwd': [('halve', xa_names)]})
    d1 = mix_bwd(r2, d2, {'attn_bwd': [('spread', ['ffn2_w_gate', 'ffn2_w_up']), ('halve', ['w_out', 'w_branch_b'])],
                          'out_bwd': [('halve', late)],
                          'scan_bwd': [('spread', xa_names + ['w_out', 'w_branch_b'] + late)],
                          'in_dx': [('halve', ['w_in#0', 'w_in#1'])]})
    grad_x = ffn_bwd('ffn1', r1, d1, {'dn_dx': [('spread', ['w_in#0'])], 'dn_dw': [('spread', ['w_in#1'])],
                                      'act_bwd': [('halve', ['ffn1_w_down'])],
                                      'gu_dx': [('spread', ['ffn1_w_down'])], 'up_dw': [('halve', ['ffn1_w_gate'])],
                                      'pre_bwd': [('spread', ['ffn1_w_gate']), ('halve', ['ffn1_w_up'])]})
    acts, carry = send({'end': [('spread', ['ffn1_w_up'])]}, 'end')
    run_alone("spread_last", carry)
    exchanged(acts, carry)

    def pack_small(vals):
        flat = jnp.concatenate([vals[n].reshape(-1) for n in SMALL])
        rows = -(-flat.shape[0] // PACK_W)
        rows = -(-rows // 8) * 8
        return jnp.pad(flat, (0, rows * PACK_W - flat.shape[0])).reshape(rows, PACK_W)

    def unpack_small(buf):
        flat, out, at = buf.reshape(-1), {}, 0
        for n in SMALL:
            size = a[n].shape[0] * a[n].shape[1]
            out[n] = flat[at:at + size].reshape(a[n].shape)
            at += size
        return out

    g_small = pack_small(gs)
    small = Gather([g_small])
    run_alone("gather_g_small", small)
    g_small = lax.dynamic_update_index_in_dim(small.results[0], g_small, me, 0)

    grads, delta, new_m, new_v = {}, {}, {}, {}
    packs = adamw("adamw_small", g_small, pack_small(a), pack_small({n: a['m_' + n] for n in SMALL}),
                  pack_small({n: a['v_' + n] for n in SMALL}))
    for dst, buf in zip((grads, delta, new_m, new_v), packs):
        dst.update(unpack_small(buf))
    landed['w_in'] = jnp.concatenate([landed['w_in#0'], landed['w_in#1']], axis=1)
    for n in BIG:
        outs = adamw("adamw_" + n, landed[n], a[n][0], a['m_' + n][0], a['v_' + n][0])
        grads[n], delta[n], new_m[n], new_v[n] = (t.reshape(a[n].shape) for t in outs)

    return (loss, grad_x[None], *[grads[n] for n in WEIGHTS], *[delta[n] for n in WEIGHTS],
            *[new_m[n] for n in WEIGHTS], *[new_v[n] for n in WEIGHTS])


def kernel(x, mem, positions, hgrn_lb_logits, ffn1_pre_g, ffn1_w_gate, ffn1_w_up, ffn1_w_down, ffn1_post_g, mix_pre_g, w_in, hg_norm_g, mla_q_norm_g, mla_w_q_up, mla_kv_norm_g, mla_w_kv_up, w_branch_a, w_branch_b, w_out, mix_post_g, xa_pre_g, xa_mem_g, xa_w_q, xa_w_k, xa_w_v, xa_w_o, xa_post_g, ffn2_pre_g, ffn2_w_gate, ffn2_w_up, ffn2_w_down, ffn2_post_g, loss_target, m_hgrn_lb_logits, m_ffn1_pre_g, m_ffn1_w_gate, m_ffn1_w_up, m_ffn1_w_down, m_ffn1_post_g, m_mix_pre_g, m_w_in, m_hg_norm_g, m_mla_q_norm_g, m_mla_w_q_up, m_mla_kv_norm_g, m_mla_w_kv_up, m_w_branch_a, m_w_branch_b, m_w_out, m_mix_post_g, m_xa_pre_g, m_xa_mem_g, m_xa_w_q, m_xa_w_k, m_xa_w_v, m_xa_w_o, m_xa_post_g, m_ffn2_pre_g, m_ffn2_w_gate, m_ffn2_w_up, m_ffn2_w_down, m_ffn2_post_g, v_hgrn_lb_logits, v_ffn1_pre_g, v_ffn1_w_gate, v_ffn1_w_up, v_ffn1_w_down, v_ffn1_post_g, v_mix_pre_g, v_w_in, v_hg_norm_g, v_mla_q_norm_g, v_mla_w_q_up, v_mla_kv_norm_g, v_mla_w_kv_up, v_w_branch_a, v_w_branch_b, v_w_out, v_mix_post_g, v_xa_pre_g, v_xa_mem_g, v_xa_w_q, v_xa_w_k, v_xa_w_v, v_xa_w_o, v_xa_post_g, v_ffn2_pre_g, v_ffn2_w_gate, v_ffn2_w_up, v_ffn2_w_down, v_ffn2_post_g):
    return _step(dict(locals()))
```

```python
import functools

import jax
import jax.numpy as jnp
from jax import lax
from jax.experimental import pallas as pl
from jax.experimental.pallas import tpu as pltpu

F32 = jnp.float32
BF16 = jnp.bfloat16

N_DEV = 8
D_MODEL = 2048
CHUNK = 64
CHUNK_SHIFT = 6
SUB = 16
HG_HEADS = 16
HG_GROUP = 16
HEAD_W = 128
MLA_HEADS = 16
Q_LORA = 512
KV_LORA = 512
QK_ROPE = 64
MLA_QK = 192
XA_HEADS = 4
ROPE_THETA = 10000.0
EPS = 1e-6
PACK_W = 1024
VMEM_LIMIT = 56 * 1024 * 1024

ADAM_LR = 0.001
ADAM_B1 = 0.9
ADAM_B2 = 0.999
ADAM_EPS = 1e-08
ADAM_WD = 0.01
ADAM_STEP = 10

U_PAD = 256

WEIGHTS = ['hgrn_lb_logits', 'ffn1_pre_g', 'ffn1_w_gate', 'ffn1_w_up', 'ffn1_w_down', 'ffn1_post_g', 'mix_pre_g',
           'w_in', 'hg_norm_g', 'mla_q_norm_g', 'mla_w_q_up', 'mla_kv_norm_g', 'mla_w_kv_up', 'w_branch_a',
           'w_branch_b', 'w_out', 'mix_post_g', 'xa_pre_g', 'xa_mem_g', 'xa_w_q', 'xa_w_k', 'xa_w_v', 'xa_w_o',
           'xa_post_g', 'ffn2_pre_g', 'ffn2_w_gate', 'ffn2_w_up', 'ffn2_w_down', 'ffn2_post_g']
BIG = {'ffn1_w_gate': True, 'ffn1_w_up': True, 'ffn1_w_down': False, 'w_in': True, 'mla_w_q_up': True,
       'mla_w_kv_up': True, 'w_branch_a': False, 'w_branch_b': False, 'w_out': False, 'xa_w_q': False,
       'xa_w_k': False, 'xa_w_v': False, 'xa_w_o': True, 'ffn2_w_gate': True, 'ffn2_w_up': True,
       'ffn2_w_down': False}
SMALL = [n for n in WEIGHTS if n not in BIG]


def _cparams(**kw):
    return pltpu.CompilerParams(vmem_limit_bytes=VMEM_LIMIT, **kw)


def _pick(dim, cands):
    for c in cands:
        if dim % c == 0:
            return c
    return dim


def _place():
    return lax.axis_index("x"), lax.axis_index("y"), lax.axis_index("c")


def _slot(px, py, pc):
    return 4 * px + 2 * py + pc


class Gather:
    def __init__(self, tensors):
        self.operands = list(tensors)
        self.out_shape = [jax.ShapeDtypeStruct((N_DEV,) + t.shape, t.dtype) for t in tensors]
        n = len(tensors)
        self.scratch = [pltpu.SemaphoreType.DMA((n, N_DEV - 1)), pltpu.SemaphoreType.DMA((n, N_DEV - 1))]

    def _copies(self, t, x_ref, out_ref, send, recv):
        x, y, c = _place()
        chips = [(1 - x, y), (x, 1 - y), (1 - x, 1 - y)]

        def copy(k, block, to, src=None):
            rows = out_ref.at[_slot(*block)]
            return pltpu.make_async_remote_copy(src_ref=rows if src is None else src, dst_ref=rows,
                                                send_sem=send.at[t, k], recv_sem=recv.at[t, k], device_id=to,
                                                device_id_type=pl.DeviceIdType.MESH)

        return (x, y, c), chips, copy

    def start(self, ins, outs, scr):
        send, recv = scr
        for t, (x_ref, out_ref) in enumerate(zip(ins, outs)):
            (x, y, c), chips, copy = self._copies(t, x_ref, out_ref, send, recv)
            copy(0, (x, y, c), (x, y, 1 - c), src=x_ref).start()
            for j, chip in enumerate(chips):
                copy(1 + j, (x, y, c), (*chip, c), src=x_ref).start()

    def finish(self, ins, outs, scr):
        send, recv = scr
        for t, (x_ref, out_ref) in enumerate(zip(ins, outs)):
            (x, y, c), chips, copy = self._copies(t, x_ref, out_ref, send, recv)
            passed = [copy(4 + j, (*chip, c), (x, y, 1 - c)) for j, chip in enumerate(chips)]
            for j, chip in enumerate(chips):
                copy(1 + j, (*chip, c), (x, y, c)).wait_recv()
                passed[j].start()
            copy(0, (x, y, 1 - c), (x, y, c)).wait_recv()
            for j, chip in enumerate(chips):
                copy(4 + j, (*chip, 1 - c), (x, y, c)).wait_recv()
            copy(0, (x, y, c), (x, y, 1 - c), src=x_ref).wait_send()
            for j, chip in enumerate(chips):
                copy(1 + j, (x, y, c), (*chip, c), src=x_ref).wait_send()
                passed[j].wait_send()

    def set_results(self, res):
        self.results = list(res)


N_CHIP = N_DEV // 2


class Halve:
    def __init__(self, tensors):
        self.operands = list(tensors)
        self.out_shape = [jax.ShapeDtypeStruct((N_CHIP,) + t.shape[1:], t.dtype) for t in tensors]
        n = len(tensors)
        self.scratch = [pltpu.SemaphoreType.DMA((n, N_CHIP)), pltpu.SemaphoreType.DMA((n, N_CHIP))]

    def _copies(self, t, x_ref, theirs_ref, scr):
        send, recv = scr
        x, y, c = _place()
        return [pltpu.make_async_remote_copy(
            src_ref=x_ref.at[2 * q + 1 - c], dst_ref=theirs_ref.at[q], send_sem=send.at[t, q],
            recv_sem=recv.at[t, q], device_id=(x, y, 1 - c), device_id_type=pl.DeviceIdType.MESH)
            for q in range(N_CHIP)]

    def start(self, ins, outs, scr):
        for t, (x_ref, theirs_ref) in enumerate(zip(ins, outs)):
            for give in self._copies(t, x_ref, theirs_ref, scr):
                give.start()

    def finish(self, ins, outs, scr):
        for t, (x_ref, theirs_ref) in enumerate(zip(ins, outs)):
            for give in self._copies(t, x_ref, theirs_ref, scr):
                give.wait_recv()
                give.wait_send()

    def set_results(self, res):
        self.results = list(res)


class Spread:
    def __init__(self, tensors):
        self.operands = list(tensors)
        self.out_shape = [jax.ShapeDtypeStruct(t.shape, t.dtype) for t in tensors]
        n = len(tensors)
        self.scratch = [pltpu.SemaphoreType.DMA((n, N_CHIP - 1)), pltpu.SemaphoreType.DMA((n, N_CHIP - 1))]

    def _copies(self, t, y_ref, out_ref, scr, outgoing):
        send, recv = scr
        x, y, c = _place()
        copies = []
        for k in range(1, N_CHIP):
            px, py = x ^ (k >> 1), y ^ (k & 1)
            copies.append(pltpu.make_async_remote_copy(
                src_ref=y_ref.at[2 * px + py], dst_ref=out_ref.at[2 * x + y if outgoing else 2 * px + py],
                send_sem=send.at[t, k - 1], recv_sem=recv.at[t, k - 1], device_id=(px, py, c),
                device_id_type=pl.DeviceIdType.MESH))
        return copies

    def start(self, ins, outs, scr):
        for t, (y_ref, out_ref) in enumerate(zip(ins, outs)):
            for give in self._copies(t, y_ref, out_ref, scr, True):
                give.start()

    def finish(self, ins, outs, scr):
        for t, (y_ref, out_ref) in enumerate(zip(ins, outs)):
            for take in self._copies(t, y_ref, out_ref, scr, False):
                take.wait_recv()
            for give in self._copies(t, y_ref, out_ref, scr, True):
                give.wait_send()

    def set_results(self, res):
        self.results = list(res)


class Joint:
    def __init__(self, parts):
        self.parts = list(parts)
        self.operands = [o for p in parts for o in p.operands]
        self.out_shape = [o for p in parts for o in p.out_shape]
        self.scratch = [o for p in parts for o in p.scratch]

    def _split(self, ins, outs, scr):
        i = o = s = 0
        for p in self.parts:
            ni, no, ns = len(p.operands), len(p.out_shape), len(p.scratch)
            yield p, ins[i:i + ni], outs[o:o + no], scr[s:s + ns]
            i, o, s = i + ni, o + no, s + ns

    def start(self, ins, outs, scr):
        for p, a, b, c in self._split(ins, outs, scr):
            p.start(a, b, c)

    def finish(self, ins, outs, scr):
        for p, a, b, c in self._split(ins, outs, scr):
            p.finish(a, b, c)

    def set_results(self, res):
        for p, _, part, _ in self._split([], list(res), []):
            p.set_results(part)


_HBM = pl.BlockSpec(memory_space=pltpu.HBM)


def run_alone(name, carry):
    n_in, n_out = len(carry.operands), len(carry.out_shape)

    def body(*refs):
        ins, outs, scr = refs[:n_in], refs[n_in:n_in + n_out], refs[n_in + n_out:]
        carry.start(ins, outs, scr)
        carry.finish(ins, outs, scr)

    res = pl.pallas_call(body, name=name, in_specs=[_HBM] * n_in, out_specs=[_HBM] * n_out,
                         out_shape=carry.out_shape, scratch_shapes=carry.scratch)(*carry.operands)
    carry.set_results(list(res))


def _call(name, body, *, grid, in_specs, out_specs, out_shape, args, scratch=(), carry=None):
    in_specs, out_specs, out_shape, scratch = list(in_specs), list(out_specs), list(out_shape), list(scratch)
    if carry is None:
        return list(pl.pallas_call(body, name=name, grid=grid, in_specs=in_specs, out_specs=out_specs,
                                   out_shape=out_shape, scratch_shapes=scratch, compiler_params=_cparams())(*args))
    n_in, n_out, n_scr = len(in_specs), len(out_shape), len(scratch)
    c_in, c_out = len(carry.operands), len(carry.out_shape)

    def wrapped(*refs):
        ins, c_ins = refs[:n_in], refs[n_in:n_in + c_in]
        at = n_in + c_in
        outs, c_outs = refs[at:at + n_out], refs[at + n_out:at + n_out + c_out]
        at += n_out + c_out
        scr, c_scr = refs[at:at + n_scr], refs[at + n_scr:]
        first = functools.reduce(jnp.logical_and, [pl.program_id(d) == 0 for d in range(len(grid))])
        last = functools.reduce(jnp.logical_and, [pl.program_id(d) == grid[d] - 1 for d in range(len(grid))])

        @pl.when(first)
        def _():
            carry.start(c_ins, c_outs, c_scr)

        body(*ins, *outs, *scr)

        @pl.when(last)
        def _():
            carry.finish(c_ins, c_outs, c_scr)

    res = pl.pallas_call(
        wrapped, name=name, grid=grid, in_specs=in_specs + [_HBM] * c_in, out_specs=out_specs + [_HBM] * c_out,
        out_shape=out_shape + carry.out_shape, scratch_shapes=scratch + carry.scratch, compiler_params=_cparams(),
    )(*args, *carry.operands)
    carry.set_results(list(res[n_out:]))
    return list(res[:n_out])


def R_(arr, w=None, off=0, stride=0):
    return ('r', arr, arr.shape[1] if w is None else w, off, stride)


def P_(arr, w=None, off=0, stride=0):
    return ('p', arr, arr.shape[1] if w is None else w, off, stride)


def rowmap(name, fn, ins, outs=None, *, tile, ncol=1, rows_inner=True, cts=None, wrt=None, gdt=None, cat=False,
           carry=None):
    rows = next(a.shape[0] for k, a, *_ in ins if k == 'r')
    nrow = rows // tile
    assert nrow * tile == rows
    grid = (ncol, nrow) if rows_inner else (nrow, ncol)

    def ij(g0, g1):
        return (g1, g0) if rows_inner else (g0, g1)

    def spec(kind, arr, w, off, stride):
        if kind == 'r':
            return pl.BlockSpec((tile, w), lambda g0, g1: (ij(g0, g1)[0], off + stride * ij(g0, g1)[1]))
        return pl.BlockSpec((arr.shape[0], w), lambda g0, g1: (0, off + stride * ij(g0, g1)[1]))

    ops = list(ins) + list(cts or [])
    in_specs = [spec(*o) for o in ops]
    n_in = len(ins)
    fwd = cts is None
    out_shape, out_specs, acc = [], [], []
    if fwd:
        for dt, w, stride in outs:
            out_shape.append(jax.ShapeDtypeStruct((rows, w * (ncol if stride else 1)), dt))
            out_specs.append(spec('r', None, w, 0, stride))
            acc.append(None)
    elif cat:
        widths = [ins[i][2] for i in wrt]
        assert ncol == 1 and all(ins[i][0] == 'r' for i in wrt)
        out_shape.append(jax.ShapeDtypeStruct((rows, sum(widths)), gdt))
        out_specs.append(spec('r', None, sum(widths), 0, 0))
    else:
        for n, i in enumerate(wrt):
            kind, arr, w, off, stride = ins[i]
            width = w * (ncol if stride else 1)
            if kind == 'r':
                out_shape.append(jax.ShapeDtypeStruct((rows, width), gdt[n]))
                out_specs.append(spec('r', None, w, 0, 1 if stride else 0))
                shared = stride == 0 and ncol > 1
                assert not shared or (not rows_inner and gdt[n] == F32)
                acc.append('col' if shared else None)
            else:
                out_shape.append(jax.ShapeDtypeStruct((arr.shape[0], width), F32))
                out_specs.append(spec('p', arr, w, 0, 1 if stride else 0))
                assert rows_inner or ncol == 1
                acc.append('row')

    def body(*refs):
        i, j = ij(pl.program_id(0), pl.program_id(1))
        vals = [r[...].astype(F32) for r in refs[:n_in]]
        out_refs = refs[len(ops):]
        if fwd:
            for o_ref, o in zip(out_refs, fn(*vals)):
                o_ref[...] = o.astype(o_ref.dtype)
            return

        def f(*d):
            full = list(vals)
            for n, idx in enumerate(wrt):
                full[idx] = d[n]
            return fn(*full)

        _, vjp = jax.vjp(f, *[vals[idx] for idx in wrt])
        grads = vjp(tuple(r[...].astype(F32) for r in refs[n_in:len(ops)]))
        if cat:
            o_ref, at = out_refs[0], 0
            for g in grads:
                o_ref[:, at:at + g.shape[1]] = g.astype(o_ref.dtype)
                at += g.shape[1]
            return
        for o_ref, g, a in zip(out_refs, grads, acc):
            if a is None:
                o_ref[...] = g.astype(o_ref.dtype)
            else:
                first = (i if a == 'row' else j) == 0

                @pl.when(first)
                def _(o_ref=o_ref):
                    o_ref[...] = jnp.zeros_like(o_ref)

                o_ref[...] += g

    return _call(name, body, grid=grid, in_specs=in_specs, out_specs=out_specs, out_shape=out_shape,
                 args=[o[1] for o in ops], carry=carry)


def _rms(x, g):
    return x * lax.rsqrt(jnp.mean(x * x, axis=-1, keepdims=True) + EPS) * g


_DIMS = {'nn': (((1,), (0,)), ((), ())), 'nt': (((1,), (1,)), ((), ())), 'tn': (((0,), (0,)), ((), ()))}


def mm(name, a, b, mode, *, jdim=None, out_dtype=F32, carry=None):
    a_list = list(a) if isinstance(a, (list, tuple)) else [a]
    b_list = list(b) if isinstance(b, (list, tuple)) else [b]
    a_order = ('k', 'm') if mode == 'tn' else ('m', 'k')
    b_order = ('n', 'k') if mode == 'nt' else ('k', 'n')
    size, blocks = {}, 1
    for arr, order in ((a_list[0], a_order), (b_list[0], b_order)):
        shape = arr.shape
        if jdim in order:
            blocks, shape = shape[0], shape[1:]
        for dname, extent in zip(order, shape):
            assert size.setdefault(dname, extent) == extent
    tile = {'m': _pick(size['m'], (1024, 512, 256, 128)), 'n': _pick(size['n'], (512, 256, 128)),
            'k': size['k'] if size['k'] <= 2048 else _pick(size['k'], (2304, 2048, 1024, 512, 256, 128))}
    if jdim is not None:
        tile[jdim] = size[jdim]
    if a_list[0].dtype == F32 and tile['k'] * tile['m'] > (1 << 20):
        tile['m'] = _pick(size['m'], (512, 256, 128))
    grid = tuple(blocks if d == jdim else size[d] // tile[d] for d in ('m', 'n', 'k'))
    nk = grid[2]

    def spec(order):
        shape = tuple(tile[d] for d in order)

        def imap(i, j, k):
            g = {'m': i, 'n': j, 'k': k}
            idx = tuple(0 if d == jdim else g[d] for d in order)
            return ((g[jdim],) + idx) if jdim in order else idx

        return pl.BlockSpec(((None,) + shape) if jdim in order else shape, imap)

    dims = _DIMS[mode]
    nt = len(a_list)

    def product(refs):
        acc = None
        for a_ref, b_ref in zip(refs[:nt], refs[nt:2 * nt]):
            p = lax.dot_general(a_ref[...].astype(BF16), b_ref[...].astype(BF16), dims, preferred_element_type=F32)
            acc = p if acc is None else acc + p
        return acc

    def body_once(*refs):
        refs[2 * nt][...] = product(refs).astype(refs[2 * nt].dtype)

    def body_acc(*refs):
        o_ref, acc_ref = refs[2 * nt], refs[2 * nt + 1]
        k = pl.program_id(2)

        @pl.when(k == 0)
        def _():
            acc_ref[...] = jnp.zeros_like(acc_ref)

        acc_ref[...] += product(refs)

        @pl.when(k == nk - 1)
        def _():
            o_ref[...] = acc_ref[...].astype(o_ref.dtype)

    out_dims = (size['m'], size['n'])
    out_shape = jax.ShapeDtypeStruct(((blocks,) + out_dims) if jdim in ('m', 'n') else out_dims, out_dtype)
    return _call(name, body_once if nk == 1 else body_acc, grid=grid,
                 in_specs=[spec(a_order)] * nt + [spec(b_order)] * nt, out_specs=[spec(('m', 'n'))],
                 out_shape=[out_shape], args=a_list + b_list,
                 scratch=[] if nk == 1 else [pltpu.VMEM((tile['m'], tile['n']), F32)], carry=carry)[0]


def _probs(q, k, i, tq, scale, causal):
    s = lax.dot_general(q, k, _DIMS['nt'], preferred_element_type=F32) * scale
    if causal:
        shape = s.shape
        q_chunk = jnp.right_shift(i * tq + lax.broadcasted_iota(jnp.int32, shape, 0), CHUNK_SHIFT)
        k_chunk = jnp.right_shift(lax.broadcasted_iota(jnp.int32, shape, 1), CHUNK_SHIFT)
        s = jnp.where(k_chunk <= q_chunk, s, -jnp.inf)
    e = jnp.exp(s - jnp.max(s, axis=-1, keepdims=True))
    return e, jnp.sum(e, axis=-1, keepdims=True)


def _per_prefix(work, i, tq, sq, sk, causal):
    if not causal:
        work(sk)
        return
    assert sq == sk and tq % CHUNK == 0
    for j in range(sq // tq):
        @pl.when(i == j)
        def _(j=j):
            work((j + 1) * tq)


def _attn_specs(tq, sk, dq, dv, koff, kstride, voff, vstride):
    return [pl.BlockSpec((tq, dq), lambda h, i: (i, h)),
            pl.BlockSpec((sk, dq), lambda h, i: (0, koff + kstride * h)),
            pl.BlockSpec((sk, dv), lambda h, i: (0, voff + vstride * h))]


def attn_fwd(name, q, k, v, *, heads, dq, dv, koff, kstride, voff, vstride, scale, causal, carry=None):
    sq, sk = q.shape[0], k.shape[0]
    tq = min(sq, 256)

    def body(q_ref, k_ref, v_ref, o_ref):
        i = pl.program_id(1)

        def work(keys):
            e, l = _probs(q_ref[...].astype(BF16), k_ref[0:keys, :].astype(BF16), i, tq, scale, causal)
            o = jnp.dot(e.astype(BF16), v_ref[0:keys, :].astype(BF16), preferred_element_type=F32)
            o_ref[...] = o / l

        _per_prefix(work, i, tq, sq, sk, causal)

    return _call(name, body, grid=(heads, sq // tq),
                 in_specs=_attn_specs(tq, sk, dq, dv, koff, kstride, voff, vstride),
                 out_specs=[pl.BlockSpec((tq, dv), lambda h, i: (i, h))],
                 out_shape=[jax.ShapeDtypeStruct((sq, heads * dv), F32)], args=[q, k, v], carry=carry)[0]


def attn_bwd(name, q, k, v, do, *, heads, dq, dv, koff, kstride, voff, vstride, scale, causal, carry=None):
    sq, sk = q.shape[0], k.shape[0]
    tq = min(sq, 256)

    def body(q_ref, k_ref, v_ref, do_ref, dq_ref, dk_ref, dv_ref):
        i = pl.program_id(1)

        @pl.when(i == 0)
        def _():
            dk_ref[...] = jnp.zeros_like(dk_ref)
            dv_ref[...] = jnp.zeros_like(dv_ref)

        def work(keys):
            qb, kb, vb = q_ref[...].astype(BF16), k_ref[0:keys, :].astype(BF16), v_ref[0:keys, :].astype(BF16)
            dob = do_ref[...].astype(BF16)
            e, l = _probs(qb, kb, i, tq, scale, causal)
            p = e / l
            dp = lax.dot_general(dob, vb, _DIMS['nt'], preferred_element_type=F32)
            ds = (p * (dp - jnp.sum(dp * p, axis=-1, keepdims=True)) * scale).astype(BF16)
            dv_ref[0:keys, :] += lax.dot_general(p.astype(BF16), dob, _DIMS['tn'], preferred_element_type=F32)
            dk_ref[0:keys, :] += lax.dot_general(ds, qb, _DIMS['tn'], preferred_element_type=F32)
            dq_ref[...] = jnp.dot(ds, kb, preferred_element_type=F32)

        _per_prefix(work, i, tq, sq, sk, causal)

    return _call(
        name, body, grid=(heads, sq // tq),
        in_specs=_attn_specs(tq, sk, dq, dv, koff, kstride, voff, vstride) + [pl.BlockSpec((tq, dv), lambda h, i: (i, h))],
        out_specs=[pl.BlockSpec((tq, dq), lambda h, i: (i, h)), pl.BlockSpec((sk, dq), lambda h, i: (0, h)),
                   pl.BlockSpec((sk, dv), lambda h, i: (0, h))],
        out_shape=[jax.ShapeDtypeStruct((sq, heads * dq), F32), jax.ShapeDtypeStruct((sk, heads * dq), F32),
                   jax.ShapeDtypeStruct((sk, heads * dv), F32)],
        args=[q, k, v, do], carry=carry)


def _hg_chunk(q, k, g, v, state):
    c = q.shape[0]
    row = lax.broadcasted_iota(jnp.int32, (c, c), 0)
    col = lax.broadcasted_iota(jnp.int32, (c, c), 1)
    tril = (col <= row).astype(F32)
    b = jnp.dot(tril, g, precision=lax.Precision.HIGHEST, preferred_element_type=F32)
    rows = lax.broadcasted_iota(jnp.int32, (c, 1), 0)
    o = jnp.dot((q * jnp.exp(b)).astype(BF16), state.astype(BF16), preferred_element_type=F32)
    t3 = lax.broadcasted_iota(jnp.int32, (SUB, SUB, 1), 0)
    s3 = lax.broadcasted_iota(jnp.int32, (SUB, SUB, 1), 1)
    parts = []
    for n in range(c // SUB):
        lo = n * SUB
        qn, kn, bn, vn = q[lo:lo + SUB], k[lo:lo + SUB], b[lo:lo + SUB], v[lo:lo + SUB]
        decay = jnp.exp(jnp.where(s3 <= t3, bn[:, None, :] - bn[None, :, :], -jnp.inf))
        sc = jnp.sum(qn[:, None, :] * kn[None, :, :] * decay, axis=-1)
        on = jnp.dot(sc.astype(BF16), vn.astype(BF16), preferred_element_type=F32)
        if n > 0:
            ref = jnp.sum(jnp.where(rows == lo - 1, b, 0.0), axis=0, keepdims=True)
            qd = (qn * jnp.exp(bn - ref)).astype(BF16)
            kd = (k[:lo] * jnp.exp(ref - b[:lo])).astype(BF16)
            so = lax.dot_general(qd, kd, _DIMS['nt'], preferred_element_type=F32)
            on = on + jnp.dot(so.astype(BF16), v[:lo].astype(BF16), preferred_element_type=F32)
        parts.append(on)
    o = o + jnp.concatenate(parts, axis=0)
    b_last = jnp.sum(g, axis=0, keepdims=True)
    ones = jnp.ones((c, 1), F32)
    b_last_col = lax.dot_general(g, ones, _DIMS['tn'], precision=lax.Precision.HIGHEST, preferred_element_type=F32)
    kd = (k * jnp.exp(b_last - b)).astype(BF16)
    new_state = jnp.exp(b_last_col) * state + lax.dot_general(kd, v.astype(BF16), _DIMS['tn'],
                                                              preferred_element_type=F32)
    return o, new_state


def _hg_chunk_bwd(q, k, g, v, state, do, dnew):
    c, kw = q.shape
    hi = lax.Precision.HIGHEST
    row = lax.broadcasted_iota(jnp.int32, (c, c), 0)
    col = lax.broadcasted_iota(jnp.int32, (c, c), 1)
    b = jnp.dot((col <= row).astype(F32), g, precision=hi, preferred_element_type=F32)
    rows = lax.broadcasted_iota(jnp.int32, (c, 1), 0)
    b_last = jnp.sum(g, axis=0, keepdims=True)
    b_last_col = lax.dot_general(g, jnp.ones((c, 1), F32), _DIMS['tn'], precision=hi, preferred_element_type=F32)
    eb, to_end = jnp.exp(b), jnp.exp(b_last - b)
    dob, vb, dnb = do.astype(BF16), v.astype(BF16), dnew.astype(BF16)
    k_end = (k * to_end).astype(BF16)
    dq = eb * lax.dot_general(dob, state.astype(BF16), _DIMS['nt'], preferred_element_type=F32)
    dk = to_end * lax.dot_general(vb, dnb, _DIMS['nt'], preferred_element_type=F32)
    dv = jnp.dot(k_end, dnb, preferred_element_type=F32)
    dstate = jnp.exp(b_last_col) * dnew + lax.dot_general((q * eb).astype(BF16), dob, _DIMS['tn'],
                                                          preferred_element_type=F32)
    new_state = jnp.exp(b_last_col) * state + lax.dot_general(k_end, vb, _DIMS['tn'], preferred_element_type=F32)
    db_end = lax.dot_general(jnp.ones((1, dnew.shape[1]), F32), dnew * new_state, _DIMS['nt'], precision=hi,
                             preferred_element_type=F32)
    t3 = lax.broadcasted_iota(jnp.int32, (SUB, SUB, 1), 0)
    s3 = lax.broadcasted_iota(jnp.int32, (SUB, SUB, 1), 1)
    dq_rows, dk_rows, dv_rows = [], [], []
    for n in range(c // SUB):
        lo = n * SUB
        qn, kn, bn, vn, don = q[lo:lo + SUB], k[lo:lo + SUB], b[lo:lo + SUB], vb[lo:lo + SUB], dob[lo:lo + SUB]
        decay = jnp.exp(jnp.where(s3 <= t3, bn[:, None, :] - bn[None, :, :], -jnp.inf))
        sc = jnp.sum(qn[:, None, :] * kn[None, :, :] * decay, axis=-1)
        pull = lax.dot_general(don, vn, _DIMS['nt'], preferred_element_type=F32)[:, :, None] * decay
        dqn = jnp.sum(pull * kn[None, :, :], axis=1)
        dk_rows.append(jnp.sum(pull * qn[:, None, :], axis=0))
        dv_rows.append(lax.dot_general(sc.astype(BF16), don, _DIMS['tn'], preferred_element_type=F32))
        if n > 0:
            ref = jnp.sum(jnp.where(rows == lo - 1, b, 0.0), axis=0, keepdims=True)
            up, down = jnp.exp(bn - ref), jnp.exp(ref - b[:lo])
            qd, kd = (qn * up).astype(BF16), (k[:lo] * down).astype(BF16)
            so = lax.dot_general(qd, kd, _DIMS['nt'], preferred_element_type=F32).astype(BF16)
            ao = lax.dot_general(don, vb[:lo], _DIMS['nt'], preferred_element_type=F32).astype(BF16)
            dqn = dqn + up * jnp.dot(ao, kd, preferred_element_type=F32)
            rest = jnp.zeros((c - lo, kw), F32)
            dk = dk + jnp.concatenate([down * lax.dot_general(ao, qd, _DIMS['tn'], preferred_element_type=F32),
                                       rest], axis=0)
            dv = dv + jnp.concatenate([lax.dot_general(so, don, _DIMS['tn'], preferred_element_type=F32), rest],
                                      axis=0)
        dq_rows.append(dqn)
    dq = dq + jnp.concatenate(dq_rows, axis=0)
    dk = dk + jnp.concatenate(dk_rows, axis=0)
    dv = dv + jnp.concatenate(dv_rows, axis=0)
    db = q * dq - k * dk + jnp.where(rows == c - 1, db_end, 0.0)
    dg = jnp.dot((col >= row).astype(F32), db, precision=hi, preferred_element_type=F32)
    return dq, dk, dg, dv, dstate


def hg_scan_fwd(name, q, k, g, u, v_off, carry=None):
    s = q.shape[0]
    n = s // CHUNK

    def body(q_ref, k_ref, g_ref, v_ref, o_ref, st_ref, state):
        @pl.when(pl.program_id(1) == 0)
        def _():
            state[...] = jnp.zeros_like(state)

        for j in range(HG_GROUP):
            cols = slice(j * HEAD_W, (j + 1) * HEAD_W)
            st = state[j]
            st_ref[j] = st
            o, new = _hg_chunk(q_ref[:, cols], k_ref[:, cols], g_ref[:, cols], v_ref[:, cols], st)
            o_ref[:, cols] = o
            state[j] = new

    wide = HG_GROUP * HEAD_W
    blk = pl.BlockSpec((CHUNK, wide), lambda h, c: (c, h))
    return _call(
        name, body, grid=(HG_HEADS // HG_GROUP, n),
        in_specs=[blk, blk, blk, pl.BlockSpec((CHUNK, wide), lambda h, c: (c, v_off // HG_GROUP + h))],
        out_specs=[blk, pl.BlockSpec((HG_GROUP, None, HEAD_W, HEAD_W), lambda h, c: (h, c, 0, 0))],
        out_shape=[jax.ShapeDtypeStruct((s, HG_HEADS * HEAD_W), F32),
                   jax.ShapeDtypeStruct((HG_HEADS, n, HEAD_W, HEAD_W), F32)],
        scratch=[pltpu.VMEM((HG_GROUP, HEAD_W, HEAD_W), F32)], args=[q, k, g, u], carry=carry)


def hg_scan_bwd(name, q, k, g, u, v_off, states, do, carry=None):
    s = q.shape[0]
    n = s // CHUNK

    def body(q_ref, k_ref, g_ref, v_ref, st_ref, do_ref, dq_ref, dk_ref, dg_ref, dv_ref, dstate):
        @pl.when(pl.program_id(1) == 0)
        def _():
            dstate[...] = jnp.zeros_like(dstate)

        for j in range(HG_GROUP):
            cols = slice(j * HEAD_W, (j + 1) * HEAD_W)
            dq, dk, dg, dv, dst = _hg_chunk_bwd(q_ref[:, cols], k_ref[:, cols], g_ref[:, cols], v_ref[:, cols],
                                                st_ref[j], do_ref[:, cols], dstate[j])
            dq_ref[:, cols] = dq
            dk_ref[:, cols] = dk
            dg_ref[:, cols] = dg
            dv_ref[:, cols] = dv
            dstate[j] = dst

    wide = HG_GROUP * HEAD_W
    blk = pl.BlockSpec((CHUNK, wide), lambda h, c: (n - 1 - c, h))
    out = jax.ShapeDtypeStruct((s, HG_HEADS * HEAD_W), F32)
    return _call(
        name, body, grid=(HG_HEADS // HG_GROUP, n),
        in_specs=[blk, blk, blk, pl.BlockSpec((CHUNK, wide), lambda h, c: (n - 1 - c, v_off // HG_GROUP + h)),
                  pl.BlockSpec((HG_GROUP, None, HEAD_W, HEAD_W), lambda h, c: (h, n - 1 - c, 0, 0)), blk],
        out_specs=[blk, blk, blk, blk], out_shape=[out, out, out, out],
        scratch=[pltpu.VMEM((HG_GROUP, HEAD_W, HEAD_W), F32)], args=[q, k, g, u, states, do], carry=carry)


def pair_sum(name, blocks, theirs):
    _, rows, cols = theirs.shape
    tile = next((t for t in (1024, 512, 256, 128, 64, 32, 16) if rows % t == 0 and t * cols <= (1 << 20)), rows)

    def body(a_ref, b_ref, o_ref):
        mine = jnp.where(lax.axis_index("c") == 0, a_ref[0].astype(F32), a_ref[1].astype(F32))
        o_ref[...] = (mine + b_ref[...].astype(F32)).astype(o_ref.dtype)

    blk = pl.BlockSpec((None, tile, cols), lambda q, i: (q, i, 0))
    return _call(name, body, grid=(N_CHIP, rows // tile),
                 in_specs=[pl.BlockSpec((None, 2, tile, cols), lambda q, i: (q, 0, i, 0)), blk], out_specs=[blk],
                 out_shape=[jax.ShapeDtypeStruct(theirs.shape, theirs.dtype)],
                 args=[blocks.reshape(N_CHIP, 2, rows, cols), theirs])[0]


def adamw(name, landed, w, m, v):
    rows, cols = w.shape
    slots = landed.shape[0]
    tile = next((t for t in (512, 256, 128, 64, 32, 16, 8) if rows % t == 0 and t * cols <= (1 << 18)), rows)

    def body(l_ref, w_ref, m_ref, v_ref, g_ref, d_ref, nm_ref, nv_ref):
        gv = l_ref[0].astype(F32)
        for s in range(1, slots):
            gv = gv + l_ref[s].astype(F32)
        nm = ADAM_B1 * m_ref[...] + (1.0 - ADAM_B1) * gv
        nv = ADAM_B2 * v_ref[...] + (1.0 - ADAM_B2) * jnp.square(gv)
        m_hat = nm / (1.0 - ADAM_B1 ** ADAM_STEP)
        v_hat = nv / (1.0 - ADAM_B2 ** ADAM_STEP)
        g_ref[...] = gv
        d_ref[...] = -ADAM_LR * (m_hat / (jnp.sqrt(v_hat) + ADAM_EPS) + ADAM_WD * w_ref[...])
        nm_ref[...] = nm
        nv_ref[...] = nv

    blk = pl.BlockSpec((tile, cols), lambda i: (i, 0))
    out = jax.ShapeDtypeStruct((rows, cols), F32)
    return _call(name, body, grid=(rows // tile,),
                 in_specs=[pl.BlockSpec((slots, tile, cols), lambda i: (0, i, 0)), blk, blk, blk],
                 out_specs=[blk] * 4, out_shape=[out] * 4, args=[landed, w, m, v])


def _swap_halves(pe):
    half = QK_ROPE // 2
    return jnp.concatenate([-pe[..., half:], pe[..., :half]], axis=-1)


def _unswap_halves(dsw):
    half = QK_ROPE // 2
    return jnp.concatenate([dsw[..., half:], -dsw[..., :half]], axis=-1)


def _w_in_ext(w):
    kpe = w[:, 9216:9280]
    z = jnp.zeros((w.shape[0], HEAD_W - QK_ROPE), w.dtype)
    pad = jnp.zeros((w.shape[0], U_PAD), w.dtype)
    return jnp.concatenate([w[:, :9216], w[:, 9280:], kpe, z, _swap_halves(kpe), z, pad], axis=1)


def _w_in_grad(d):
    dkpe = d[:, 13312:13376] + _unswap_halves(d[:, 13440:13504])
    return jnp.concatenate([d[:, :9216], dkpe, d[:, 9216:13312]], axis=1)


def _w_q_ext(w):
    w3 = w.reshape(Q_LORA, MLA_HEADS, MLA_QK)
    pe = w3[:, :, HEAD_W:]
    z = jnp.zeros((Q_LORA, MLA_HEADS, HEAD_W - QK_ROPE), w.dtype)
    wide = MLA_HEADS * HEAD_W
    return jnp.concatenate([w3[:, :, :HEAD_W].reshape(Q_LORA, wide),
                            jnp.concatenate([pe, z], axis=2).reshape(Q_LORA, wide),
                            jnp.concatenate([_swap_halves(pe), z], axis=2).reshape(Q_LORA, wide)], axis=1)


def _w_q_grad(parts):
    d3 = [p.reshape(Q_LORA, MLA_HEADS, HEAD_W) for p in parts]
    dpe = d3[1][:, :, :QK_ROPE] + _unswap_halves(d3[2][:, :, :QK_ROPE])
    return jnp.concatenate([d3[0], dpe], axis=2).reshape(Q_LORA, MLA_HEADS * MLA_QK)


def _hg_prep(f_raw, q_hg, logits):
    lb = jax.nn.softmax(logits, axis=0)[0:1, :]
    log_f = jnp.logaddexp(jnp.log(lb), jnp.log1p(-lb) + jax.nn.log_sigmoid(f_raw))
    k_in = (1.0 - lb) * jax.nn.sigmoid(-f_raw)
    return log_f, k_in, jax.nn.silu(q_hg)


def _rope(q_nope, q_pe, q_sw, k_nope, k_pe, k_sw, cos, sin):
    qf = jnp.concatenate([q_nope, q_pe * cos + q_sw * sin], axis=1)
    kf = jnp.concatenate([k_nope, k_pe * cos + k_sw * sin], axis=1)
    return qf, kf


def _step(a):
    x, mem, target = a['x'][0], a['mem'][0], a['loss_target'][0]
    s, d = x.shape
    nm = mem.shape[0]
    ff = N_DEV * a['ffn1_w_gate'].shape[-1]
    cs = ff // N_DEV
    tr = min(s, 256)
    tp = min(s, 128)
    th = s
    ta = 1024

    bf = {n: a[n][0].astype(BF16) for n in BIG}
    half_in = bf['w_in'].shape[0] // 2
    bf['w_in#0'], bf['w_in#1'] = bf['w_in'][:half_in], bf['w_in'][half_in:]
    gat, full = {}, {}

    my_chip = 2 * lax.axis_index("x") + lax.axis_index("y")
    me = 2 * my_chip + lax.axis_index("c")

    def gathered(names, carry):
        for n, g8 in zip(names or [], carry.results if carry else []):
            r, c = bf[n].shape
            g8 = lax.dynamic_update_index_in_dim(g8, bf[n], me, 0)
            gat[n] = g8
            if not n.startswith('ffn'):
                by_col = BIG[n.split('#')[0]]
                full[n] = g8.transpose(1, 0, 2).reshape(r, N_DEV * c) if by_col else g8.reshape(N_DEV * r, c)

    xa_names = ['xa_w_q', 'xa_w_k', 'xa_w_v', 'xa_w_o']
    ffn2_names = ['ffn2_w_gate', 'ffn2_w_up', 'ffn2_w_down']
    first = Gather([bf['ffn1_w_gate']])
    run_alone("gather_first", first)
    gathered(['ffn1_w_gate'], first)

    inv_freq = 1.0 / (ROPE_THETA ** (jnp.arange(0, QK_ROPE, 2, dtype=F32) / QK_ROPE))
    ang = a['positions'][0].astype(F32)[:, None] * inv_freq
    zero = jnp.zeros((s, HEAD_W - QK_ROPE), F32)
    cos = jnp.concatenate([jnp.cos(ang), jnp.cos(ang), zero], axis=1)
    sin = jnp.concatenate([jnp.sin(ang), jnp.sin(ang), zero], axis=1)

    gs = {}
    gb = {}

    def by_rows(g):
        return g.reshape(N_DEV, g.shape[0] // N_DEV, g.shape[1])

    def by_cols(g):
        return g.reshape(g.shape[0], N_DEV, g.shape[1] // N_DEV).transpose(1, 0, 2)

    pre_fn = lambda xv, g: (_rms(xv, g),)
    pre_res_fn = lambda xv, g: (_rms(xv, g), xv)

    def pre_norm(tag, x_in, g):
        return rowmap(tag + "_pre", pre_fn, [R_(x_in), P_(g)], [(BF16, x_in.shape[1], 0)], tile=tr)[0]

    def pre_norm_bwd(tag, x_in, g, dh, d_out, carry=None):
        return rowmap(tag + "_pre_bwd", pre_res_fn, [R_(x_in), P_(g)], tile=tr, cts=[R_(dh), R_(d_out)],
                      wrt=[0, 1], gdt=[F32, F32], carry=carry)

    def post_fn(weight):
        return lambda xv, y, g: (xv + weight * _rms(y, g),)

    def post_norm(tag, x_in, y, g, weight):
        return rowmap(tag + "_post", post_fn(weight), [R_(x_in), R_(y), P_(g)], [(F32, d, 0)], tile=tr)[0]

    def post_norm_bwd(tag, x_in, y, g, weight, d_out):
        return rowmap(tag + "_post_bwd", post_fn(weight), [R_(x_in), R_(y), P_(g)], tile=tr, cts=[R_(d_out)],
                      wrt=[1, 2], gdt=[BF16, F32])

    act_fn = lambda av, bv: (jax.nn.silu(av) * bv,)

    def fetch(plan, key):
        names = plan.get(key)
        return (names, Gather([bf[n] for n in names])) if names else (None, None)

    def send(plan, key):
        acts = plan.get(key)
        if not acts:
            return None, None
        parts = []
        for kind, names in acts:
            if kind == 'spread':
                for n in names:
                    chip_sums[n] = pair_sum("pair_" + n.replace('#', '_'), gb[n], halves[n])
            parts.append(Halve([gb[n] for n in names]) if kind == 'halve' else Spread([chip_sums[n] for n in names]))
        return acts, Joint(parts)

    def ffn_fwd(tag, x_in, plan):
        h = pre_norm(tag, x_in, a[tag + '_pre_g'])
        names, carry = fetch(plan, 'gate')
        av = mm(tag + "_gate", h, gat[tag + '_w_gate'], 'nn', jdim='n', out_dtype=BF16, carry=carry)
        gathered(names, carry)
        names, carry = fetch(plan, 'up')
        bv = mm(tag + "_up", h, gat[tag + '_w_up'], 'nn', jdim='n', out_dtype=BF16, carry=carry)
        gathered(names, carry)
        a2, b2 = av.reshape(N_DEV * s, cs), bv.reshape(N_DEV * s, cs)
        names, carry = fetch(plan, 'act')
        z = rowmap(tag + "_act", act_fn, [R_(a2), R_(b2)], [(BF16, cs, 0)], tile=ta, carry=carry)[0]
        z = z.reshape(N_DEV, s, cs)
        gathered(names, carry)
        names, carry = fetch(plan, 'dn')
        y = mm(tag + "_dn", z, gat[tag + '_w_down'], 'nn', jdim='k', carry=carry)
        gathered(names, carry)
        return post_norm(tag, x_in, y, a[tag + '_post_g'], 0.5), (x_in, h, a2, b2, z, y)

    def ffn_bwd(tag, res, d_out, plan):
        x_in, h, a2, b2, z, y = res
        dy, gs[tag + '_post_g'] = post_norm_bwd(tag, x_in, y, a[tag + '_post_g'], 0.5, d_out)
        acts, carry = send(plan, 'dn_dx')
        dz = mm(tag + "_dn_dx", dy, gat[tag + '_w_down'], 'nt', jdim='n', carry=carry)
        exchanged(acts, carry)
        acts, carry = send(plan, 'dn_dw')
        gb[tag + '_w_down'] = mm(tag + "_dn_dw", z, dy, 'tn', jdim='m', out_dtype=BF16, carry=carry)
        exchanged(acts, carry)
        acts, carry = send(plan, 'act_bwd')
        da, db = rowmap(tag + "_act_bwd", act_fn, [R_(a2), R_(b2)], tile=ta, cts=[R_(dz.reshape(N_DEV * s, cs))],
                        wrt=[0, 1], gdt=[BF16, BF16], carry=carry)
        exchanged(acts, carry)
        da, db = da.reshape(N_DEV, s, cs), db.reshape(N_DEV, s, cs)
        names, carry = send(plan, 'gu_dx')
        dh = mm(tag + "_gu_dx", [da, db], [gat[tag + '_w_gate'], gat[tag + '_w_up']], 'nt', jdim='k', carry=carry)
        exchanged(names, carry)
        gb[tag + '_w_gate'] = mm(tag + "_gate_dw", h, da, 'tn', jdim='n', out_dtype=BF16)
        names, carry = send(plan, 'up_dw')
        gb[tag + '_w_up'] = mm(tag + "_up_dw", h, db, 'tn', jdim='n', out_dtype=BF16, carry=carry)
        exchanged(names, carry)
        names, carry = send(plan, 'pre_bwd')
        d_in, gs[tag + '_pre_g'] = pre_norm_bwd(tag, x_in, a[tag + '_pre_g'], dh, d_out, carry)
        exchanged(names, carry)
        return d_in

    hg_out_fn = lambda o, og, g: (_rms(o, g) * jax.nn.silu(og),)
    mla_norm_fn = lambda cq, ckv, gq, gkv: (_rms(cq, gq), _rms(ckv, gkv))
    gate_fn = lambda ga, gb, ya, yb: (jax.nn.sigmoid(ga) * ya + jax.nn.sigmoid(gb) * yb,)
    mla = dict(heads=MLA_HEADS, dq=2 * HEAD_W, dv=HEAD_W, koff=0, kstride=1, voff=1, vstride=2,
               scale=MLA_QK ** -0.5, causal=True)

    def mix_fwd(x_in, plan):
        w_in = _w_in_ext(jnp.concatenate([full['w_in#0'], full['w_in#1']], axis=0))
        h = pre_norm("mix", x_in, a['mix_pre_g'])
        names, carry = fetch(plan, 'in')
        u = mm("mix_in", h, w_in, 'nn', carry=carry)
        gathered(names, carry)
        w_q, w_kv = _w_q_ext(full['mla_w_q_up']), full['mla_w_kv_up']
        hg_ins = [R_(u, 2048, 1), R_(u, 2048, 0), P_(a['hgrn_lb_logits'])]
        log_f, k_in, q_in = rowmap("hg_prep", _hg_prep, hg_ins, [(F32, 2048, 0)] * 3, tile=tp)
        names, carry = fetch(plan, 'scan')
        o_a, states = hg_scan_fwd("hg_scan", q_in, k_in, log_f, u, 32, carry=carry)
        gathered(names, carry)
        out_ins = [R_(o_a, HEAD_W, 0, 1), R_(u, HEAD_W, 48, 1), P_(a['hg_norm_g'], HEAD_W, 0, 1)]
        oag = rowmap("hg_out", hg_out_fn, out_ins, [(BF16, HEAD_W, 1)], tile=th, ncol=HG_HEADS)[0]
        y_a = mm("mix_a", oag, full['w_branch_a'], 'nn')
        norm_ins = [R_(u, 512, 16), R_(u, 512, 17), P_(a['mla_q_norm_g']), P_(a['mla_kv_norm_g'])]
        cqn, ckvn = rowmap("mla_norm", mla_norm_fn, norm_ins, [(BF16, 512, 0)] * 2, tile=tr)
        q_all = mm("mla_qup", cqn, w_q, 'nn')
        kv = mm("mla_kvup", ckvn, w_kv, 'nn')
        rope_ins = [R_(q_all, HEAD_W, 0, 1), R_(q_all, HEAD_W, 16, 1), R_(q_all, HEAD_W, 32, 1),
                    R_(kv, HEAD_W, 0, 2), R_(u, HEAD_W, 104, 0), R_(u, HEAD_W, 105, 0), R_(cos), R_(sin)]
        qf, kf = rowmap("mla_rope", _rope, rope_ins, [(BF16, 2 * HEAD_W, 1)] * 2, tile=th, ncol=MLA_HEADS,
                        rows_inner=False)
        names, carry = fetch(plan, 'attn')
        o_b = attn_fwd("mla_attn", qf, kf, kv, carry=carry, **mla)
        gathered(names, carry)
        y_b = mm("mix_b", o_b, full['w_branch_b'], 'nn')
        gate_ins = [R_(u, 1024, 9, 1), R_(u, 1024, 11, 1), R_(y_a, 1024, 0, 1), R_(y_b, 1024, 0, 1)]
        y = rowmap("mix_gate", gate_fn, gate_ins, [(BF16, 1024, 1)], tile=tr, ncol=2)[0]
        yo = mm("mix_out", y, full['w_out'], 'nn')
        res = (x_in, h, hg_ins, q_in, k_in, log_f, u, states, out_ins, oag, norm_ins, cqn, ckvn, rope_ins, qf, kf,
               kv, o_b, gate_ins, y, yo, w_in, w_q, w_kv)
        return post_norm("mix", x_in, yo, a['mix_post_g'], 1.0), res

    def mix_bwd(res, d_out, plan):
        (x_in, h, hg_ins, q_in, k_in, log_f, u, states, out_ins, oag, norm_ins, cqn, ckvn, rope_ins, qf, kf, kv,
         o_b, gate_ins, y, yo, w_in, w_q, w_kv) = res
        dyo, gs['mix_post_g'] = post_norm_bwd("mix", x_in, yo, a['mix_post_g'], 1.0, d_out)
        dy = mm("mix_out_dx", dyo, full['w_out'], 'nt')
        gb['w_out'] = by_rows(mm("mix_out_dw", y, dyo, 'tn', out_dtype=BF16))
        dga, dgb, dya, dyb = rowmap("mix_gate_bwd", gate_fn, gate_ins, tile=tr, ncol=2, cts=[R_(dy, 1024, 0, 1)],
                                    wrt=[0, 1, 2, 3], gdt=[BF16] * 4)
        gb['w_branch_b'] = by_rows(mm("mix_b_dw", o_b, dyb, 'tn', out_dtype=BF16))
        do_b = mm("mix_b_dx", dyb, full['w_branch_b'], 'nt')
        names, carry = send(plan, 'attn_bwd')
        dqf, dkf, dv = attn_bwd("mla_attn_bwd", qf, kf, kv, do_b, carry=carry, **mla)
        exchanged(names, carry)
        dqn, dqp, dqs, dkn, dkpe, dksw = rowmap(
            "mla_rope_bwd", _rope, rope_ins, tile=th, ncol=MLA_HEADS, rows_inner=False,
            cts=[R_(dqf, 2 * HEAD_W, 0, 1), R_(dkf, 2 * HEAD_W, 0, 1)], wrt=[0, 1, 2, 3, 4, 5],
            gdt=[BF16, BF16, BF16, BF16, F32, F32])
        wide = MLA_HEADS * HEAD_W
        dq_parts = [dqn, dqp, dqs]
        gb['mla_w_q_up'] = by_cols(_w_q_grad([mm("mla_qup_dw%d" % n, cqn, dq_parts[n], 'tn', out_dtype=BF16)
                                              for n in range(3)]))
        dcqn = mm("mla_qup_dx", dq_parts, [w_q[:, n * wide:(n + 1) * wide] for n in range(3)], 'nt')
        w_kv4 = w_kv.reshape(KV_LORA, MLA_HEADS, 2, HEAD_W)
        dw_k = mm("mla_kup_dw", ckvn, dkn, 'tn', out_dtype=BF16).reshape(KV_LORA, MLA_HEADS, 1, HEAD_W)
        dw_v = mm("mla_vup_dw", ckvn, dv, 'tn', out_dtype=BF16).reshape(KV_LORA, MLA_HEADS, 1, HEAD_W)
        gb['mla_w_kv_up'] = by_cols(jnp.concatenate([dw_k, dw_v], axis=2).reshape(KV_LORA, 2 * wide))
        dckvn = mm("mla_kvup_dx", [dkn, dv],
                   [w_kv4[:, :, 0].reshape(KV_LORA, wide), w_kv4[:, :, 1].reshape(KV_LORA, wide)], 'nt')
        dcq, dckv, gs['mla_q_norm_g'], gs['mla_kv_norm_g'] = rowmap(
            "mla_norm_bwd", mla_norm_fn, norm_ins, tile=tr, cts=[R_(dcqn), R_(dckvn)], wrt=[0, 1, 2, 3],
            gdt=[BF16, BF16, F32, F32])
        gb['w_branch_a'] = by_rows(mm("mix_a_dw", oag, dya, 'tn', out_dtype=BF16))
        doag = mm("mix_a_dx", dya, full['w_branch_a'], 'nt')
        acts, carry = send(plan, 'out_bwd')
        do_a, dog, gs['hg_norm_g'] = rowmap("hg_out_bwd", hg_out_fn, out_ins, tile=th, ncol=HG_HEADS,
                                            cts=[R_(doag, HEAD_W, 0, 1)], wrt=[0, 1, 2], gdt=[F32, BF16, F32],
                                            carry=carry)
        exchanged(acts, carry)
        names, carry = send(plan, 'scan_bwd')
        dq_in, dk_in, dlog_f, di = hg_scan_bwd("hg_scan_bwd", q_in, k_in, log_f, u, 32, states, do_a, carry=carry)
        exchanged(names, carry)
        df, dq_hg, gs['hgrn_lb_logits'] = rowmap("hg_prep_bwd", _hg_prep, hg_ins, tile=tp,
                                                 cts=[R_(dlog_f), R_(dk_in), R_(dq_in)], wrt=[0, 1, 2],
                                                 gdt=[BF16, BF16, F32])
        du = jnp.concatenate([dq_hg, df, di.astype(BF16), dog, dcq, dckv, dga, dgb, dkpe.astype(BF16),
                              dksw.astype(BF16), jnp.zeros((s, U_PAD), BF16)], axis=1)
        g_in = by_cols(_w_in_grad(mm("mix_in_dw", h, du, 'tn', out_dtype=BF16)))
        gb['w_in#0'], gb['w_in#1'] = g_in[:, :half_in], g_in[:, half_in:]
        names, carry = send(plan, 'in_dx')
        dh = mm("mix_in_dx", du, w_in, 'nt', carry=carry)
        exchanged(names, carry)
        d_in, gs['mix_pre_g'] = pre_norm_bwd("mix", x_in, a['mix_pre_g'], dh, d_out)
        return d_in

    xa = dict(heads=XA_HEADS, dq=HEAD_W, dv=HEAD_W, koff=0, kstride=1, voff=XA_HEADS, vstride=1,
              scale=HEAD_W ** -0.5, causal=False)
    tm_ = min(nm, 128)

    def xa_fwd(x_in):
        w_xkv = jnp.concatenate([full['xa_w_k'], full['xa_w_v']], axis=1)
        h = pre_norm("xa", x_in, a['xa_pre_g'])
        mn = rowmap("xa_mem", pre_fn, [R_(mem), P_(a['xa_mem_g'])], [(BF16, d, 0)], tile=tm_)[0]
        q = mm("xa_q", h, full['xa_w_q'], 'nn')
        kv = mm("xa_kv", mn, w_xkv, 'nn')
        o = attn_fwd("xa_attn", q, kv, kv, **xa)
        yo = mm("xa_o", o, full['xa_w_o'], 'nn')
        return post_norm("xa", x_in, yo, a['xa_post_g'], 1.0), (x_in, h, mn, q, kv, o, yo, w_xkv)

    def xa_bwd(res, d_out, plan):
        x_in, h, mn, q, kv, o, yo, w_xkv = res
        dyo, gs['xa_post_g'] = post_norm_bwd("xa", x_in, yo, a['xa_post_g'], 1.0, d_out)
        do = mm("xa_o_dx", dyo, full['xa_w_o'], 'nt')
        gb['xa_w_o'] = by_cols(mm("xa_o_dw", o, dyo, 'tn', out_dtype=BF16))
        dq, dk, dv = attn_bwd("xa_attn_bwd", q, kv, kv, do, **xa)
        dkv = jnp.concatenate([dk, dv], axis=1).astype(BF16)
        dw = mm("xa_kv_dw", mn, dkv, 'tn', out_dtype=BF16)
        gb['xa_w_k'], gb['xa_w_v'] = by_rows(dw[:, :XA_HEADS * HEAD_W]), by_rows(dw[:, XA_HEADS * HEAD_W:])
        dmn = mm("xa_kv_dx", dkv, w_xkv, 'nt')
        gs['xa_mem_g'] = rowmap("xa_mem_bwd", pre_fn, [R_(mem), P_(a['xa_mem_g'])], tile=tm_, cts=[R_(dmn)],
                                wrt=[1], gdt=[F32])[0]
        gb['xa_w_q'] = by_rows(mm("xa_q_dw", h, dq, 'tn', out_dtype=BF16))
        dh = mm("xa_q_dx", dq, full['xa_w_q'], 'nt')
        acts, carry = send(plan, 'pre_bwd')
        d_in, gs['xa_pre_g'] = pre_norm_bwd("xa", x_in, a['xa_pre_g'], dh, d_out, carry)
        exchanged(acts, carry)
        return d_in

    landed = {}

    halves = {}
    chip_sums = {}

    def exchanged(acts, carry):
        for (kind, names), part in zip(acts or [], carry.parts if carry else []):
            for n, got in zip(names, part.results):
                if kind == 'halve':
                    halves[n] = got
                else:
                    own = lax.dynamic_index_in_dim(chip_sums[n], my_chip, 0, keepdims=True)
                    landed[n] = lax.dynamic_update_slice_in_dim(got, own, my_chip, 0)

    x1, r1 = ffn_fwd('ffn1', x, {'gate': ['ffn1_w_up'], 'up': ['ffn1_w_down'], 'act': ['w_in#0'], 'dn': ['w_in#1']})
    x2, r2 = mix_fwd(x1, {'in': ['mla_w_q_up', 'mla_w_kv_up', 'w_branch_a', 'w_branch_b', 'w_out'],
                          'scan': xa_names + ['ffn2_w_gate'], 'attn': ['ffn2_w_up']})
    x3, r3 = xa_fwd(x2)
    x4, r4 = ffn_fwd('ffn2', x3, {'gate': ['ffn2_w_down']})

    def loss_fn(y, t):
        diff = y - t
        return diff * (1.0 / d), jnp.mean(diff * diff, axis=-1, keepdims=True)

    d4, row_loss = rowmap("loss", loss_fn, [R_(x4), R_(target)], [(F32, d, 0), (F32, 1, 0)], tile=tr)
    loss = lax.psum(0.5 * jnp.sum(row_loss), ("x", "y", "c"))
    late = ['mla_w_q_up', 'mla_w_kv_up', 'w_branch_a']
    d3 = ffn_bwd('ffn2', r4, d4, {'act_bwd': [('halve', ['ffn2_w_down'])], 'gu_dx': [('spread', ['ffn2_w_down'])],
                                  'up_dw': [('halve', ['ffn2_w_gate'])], 'pre_bwd': [('halve', ['ffn2_w_up'])]})
    d2 = xa_bwd(r3, d3, {'pre_bwd': [('halve', xa_names)]})
    d1 = mix_bwd(r2, d2, {'attn_bwd': [('spread', ['ffn2_w_gate', 'ffn2_w_up']), ('halve', ['w_out', 'w_branch_b'])],
                          'out_bwd': [('halve', late)],
                          'scan_bwd': [('spread', xa_names + ['w_out', 'w_branch_b'] + late)],
                          'in_dx': [('halve', ['w_in#0', 'w_in#1'])]})
    grad_x = ffn_bwd('ffn1', r1, d1, {'dn_dx': [('spread', ['w_in#0'])], 'dn_dw': [('spread', ['w_in#1'])],
                                      'act_bwd': [('halve', ['ffn1_w_down'])],
                                      'gu_dx': [('spread', ['ffn1_w_down'])], 'up_dw': [('halve', ['ffn1_w_gate'])],
                                      'pre_bwd': [('spread', ['ffn1_w_gate']), ('halve', ['ffn1_w_up'])]})
    acts, carry = send({'end': [('spread', ['ffn1_w_up'])]}, 'end')
    run_alone("spread_last", carry)
    exchanged(acts, carry)

    def pack_small(vals):
        flat = jnp.concatenate([vals[n].reshape(-1) for n in SMALL])
        rows = -(-flat.shape[0] // PACK_W)
        rows = -(-rows // 8) * 8
        return jnp.pad(flat, (0, rows * PACK_W - flat.shape[0])).reshape(rows, PACK_W)

    def unpack_small(buf):
        flat, out, at = buf.reshape(-1), {}, 0
        for n in SMALL:
            size = a[n].shape[0] * a[n].shape[1]
            out[n] = flat[at:at + size].reshape(a[n].shape)
            at += size
        return out

    g_small = pack_small(gs)
    small = Gather([g_small])
    run_alone("gather_g_small", small)
    g_small = lax.dynamic_update_index_in_dim(small.results[0], g_small, me, 0)

    grads, delta, new_m, new_v = {}, {}, {}, {}
    packs = adamw("adamw_small", g_small, pack_small(a), pack_small({n: a['m_' + n] for n in SMALL}),
                  pack_small({n: a['v_' + n] for n in SMALL}))
    for dst, buf in zip((grads, delta, new_m, new_v), packs):
        dst.update(unpack_small(buf))
    landed['w_in'] = jnp.concatenate([landed['w_in#0'], landed['w_in#1']], axis=1)
    for n in BIG:
        outs = adamw("adamw_" + n, landed[n], a[n][0], a['m_' + n][0], a['v_' + n][0])
        grads[n], delta[n], new_m[n], new_v[n] = (t.reshape(a[n].shape) for t in outs)

    return (loss, grad_x[None], *[grads[n] for n in WEIGHTS], *[delta[n] for n in WEIGHTS],
            *[new_m[n] for n in WEIGHTS], *[new_v[n] for n in WEIGHTS])


def kernel(x, mem, positions, hgrn_lb_logits, ffn1_pre_g, ffn1_w_gate, ffn1_w_up, ffn1_w_down, ffn1_post_g, mix_pre_g, w_in, hg_norm_g, mla_q_norm_g, mla_w_q_up, mla_kv_norm_g, mla_w_kv_up, w_branch_a, w_branch_b, w_out, mix_post_g, xa_pre_g, xa_mem_g, xa_w_q, xa_w_k, xa_w_v, xa_w_o, xa_post_g, ffn2_pre_g, ffn2_w_gate, ffn2_w_up, ffn2_w_down, ffn2_post_g, loss_target, m_hgrn_lb_logits, m_ffn1_pre_g, m_ffn1_w_gate, m_ffn1_w_up, m_ffn1_w_down, m_ffn1_post_g, m_mix_pre_g, m_w_in, m_hg_norm_g, m_mla_q_norm_g, m_mla_w_q_up, m_mla_kv_norm_g, m_mla_w_kv_up, m_w_branch_a, m_w_branch_b, m_w_out, m_mix_post_g, m_xa_pre_g, m_xa_mem_g, m_xa_w_q, m_xa_w_k, m_xa_w_v, m_xa_w_o, m_xa_post_g, m_ffn2_pre_g, m_ffn2_w_gate, m_ffn2_w_up, m_ffn2_w_down, m_ffn2_post_g, v_hgrn_lb_logits, v_ffn1_pre_g, v_ffn1_w_gate, v_ffn1_w_up, v_ffn1_w_down, v_ffn1_post_g, v_mix_pre_g, v_w_in, v_hg_norm_g, v_mla_q_norm_g, v_mla_w_q_up, v_mla_kv_norm_g, v_mla_w_kv_up, v_w_branch_a, v_w_branch_b, v_w_out, v_mix_post_g, v_xa_pre_g, v_xa_mem_g, v_xa_w_q, v_xa_w_k, v_xa_w_v, v_xa_w_o, v_xa_post_g, v_ffn2_pre_g, v_ffn2_w_gate, v_ffn2_w_up, v_ffn2_w_down, v_ffn2_post_g):
    return _step(dict(locals()))
```

```python
import functools

import jax
import jax.numpy as jnp
from jax import lax
from jax.experimental import pallas as pl
from jax.experimental.pallas import tpu as pltpu

F32 = jnp.float32
BF16 = jnp.bfloat16

N_DEV = 8
D_MODEL = 2048
CHUNK = 64
CHUNK_SHIFT = 6
SUB = 16
HG_HEADS = 16
HG_GROUP = 16
HEAD_W = 128
MLA_HEADS = 16
Q_LORA = 512
KV_LORA = 512
QK_ROPE = 64
MLA_QK = 192
XA_HEADS = 4
ROPE_THETA = 10000.0
EPS = 1e-6
PACK_W = 1024
VMEM_LIMIT = 56 * 1024 * 1024

ADAM_LR = 0.001
ADAM_B1 = 0.9
ADAM_B2 = 0.999
ADAM_EPS = 1e-08
ADAM_WD = 0.01
ADAM_STEP = 10

U_PAD = 256

WEIGHTS = ['hgrn_lb_logits', 'ffn1_pre_g', 'ffn1_w_gate', 'ffn1_w_up', 'ffn1_w_down', 'ffn1_post_g', 'mix_pre_g',
           'w_in', 'hg_norm_g', 'mla_q_norm_g', 'mla_w_q_up', 'mla_kv_norm_g', 'mla_w_kv_up', 'w_branch_a',
           'w_branch_b', 'w_out', 'mix_post_g', 'xa_pre_g', 'xa_mem_g', 'xa_w_q', 'xa_w_k', 'xa_w_v', 'xa_w_o',
           'xa_post_g', 'ffn2_pre_g', 'ffn2_w_gate', 'ffn2_w_up', 'ffn2_w_down', 'ffn2_post_g']
BIG = {'ffn1_w_gate': True, 'ffn1_w_up': True, 'ffn1_w_down': False, 'w_in': True, 'mla_w_q_up': True,
       'mla_w_kv_up': True, 'w_branch_a': False, 'w_branch_b': False, 'w_out': False, 'xa_w_q': False,
       'xa_w_k': False, 'xa_w_v': False, 'xa_w_o': True, 'ffn2_w_gate': True, 'ffn2_w_up': True,
       'ffn2_w_down': False}
SMALL = [n for n in WEIGHTS if n not in BIG]


def _cparams(**kw):
    return pltpu.CompilerParams(vmem_limit_bytes=VMEM_LIMIT, **kw)


def _pick(dim, cands):
    for c in cands:
        if dim % c == 0:
            return c
    return dim


def _place():
    return lax.axis_index("x"), lax.axis_index("y"), lax.axis_index("c")


def _slot(px, py, pc):
    return 4 * px + 2 * py + pc


class Gather:
    def __init__(self, tensors):
        self.operands = list(tensors)
        self.out_shape = [jax.ShapeDtypeStruct((N_DEV,) + t.shape, t.dtype) for t in tensors]
        n = len(tensors)
        self.scratch = [pltpu.SemaphoreType.DMA((n, N_DEV - 1)), pltpu.SemaphoreType.DMA((n, N_DEV - 1))]

    def _copies(self, t, x_ref, out_ref, scr):
        send, recv = scr
        x, y, c = _place()
        me, sibling = (x, y, c), (x, y, 1 - c)
        x_nbr, y_nbr, diag = (1 - x, y, c), (x, 1 - y, c), (1 - x, 1 - y, c)
        relay_from, relay_to = (x ^ (1 - c), y ^ c, c), (x ^ c, y ^ (1 - c), c)

        def copy(k, block, to, src=None):
            rows = out_ref.at[_slot(*block)]
            return pltpu.make_async_remote_copy(src_ref=rows if src is None else src, dst_ref=rows,
                                                send_sem=send.at[t, k], recv_sem=recv.at[t, k], device_id=to,
                                                device_id_type=pl.DeviceIdType.MESH)

        def other(block):
            return block[0], block[1], 1 - c

        sent = [(me, sibling, x_ref), (me, x_nbr, x_ref), (me, y_nbr, x_ref), (relay_from, relay_to, None),
                (x_nbr, sibling, None), (y_nbr, sibling, None), (diag, sibling, None)]
        landing = [sibling, x_nbr, y_nbr, diag, other(x_nbr), other(y_nbr), other(diag)]
        return (lambda k: copy(k, *sent[k])), (lambda k: copy(k, landing[k], me))

    def start(self, ins, outs, scr):
        for t, (x_ref, out_ref) in enumerate(zip(ins, outs)):
            give, _ = self._copies(t, x_ref, out_ref, scr)
            for k in range(3):
                give(k).start()

    def finish(self, ins, outs, scr):
        for t, (x_ref, out_ref) in enumerate(zip(ins, outs)):
            give, take = self._copies(t, x_ref, out_ref, scr)
            take(1).wait_recv()
            take(2).wait_recv()
            for k in (3, 4, 5):
                give(k).start()
            take(3).wait_recv()
            give(6).start()
            for k in (0, 4, 5, 6):
                take(k).wait_recv()
            for k in range(N_DEV - 1):
                give(k).wait_send()

    def set_results(self, res):
        self.results = list(res)


N_CHIP = N_DEV // 2


class Halve:
    def __init__(self, tensors):
        self.operands = list(tensors)
        self.out_shape = [jax.ShapeDtypeStruct((N_CHIP,) + t.shape[1:], t.dtype) for t in tensors]
        n = len(tensors)
        self.scratch = [pltpu.SemaphoreType.DMA((n, N_CHIP)), pltpu.SemaphoreType.DMA((n, N_CHIP))]

    def _copies(self, t, x_ref, theirs_ref, scr):
        send, recv = scr
        x, y, c = _place()
        return [pltpu.make_async_remote_copy(
            src_ref=x_ref.at[2 * q + 1 - c], dst_ref=theirs_ref.at[q], send_sem=send.at[t, q],
            recv_sem=recv.at[t, q], device_id=(x, y, 1 - c), device_id_type=pl.DeviceIdType.MESH)
            for q in range(N_CHIP)]

    def start(self, ins, outs, scr):
        for t, (x_ref, theirs_ref) in enumerate(zip(ins, outs)):
            for give in self._copies(t, x_ref, theirs_ref, scr):
                give.start()

    def finish(self, ins, outs, scr):
        for t, (x_ref, theirs_ref) in enumerate(zip(ins, outs)):
            for give in self._copies(t, x_ref, theirs_ref, scr):
                give.wait_recv()
                give.wait_send()

    def set_results(self, res):
        self.results = list(res)


class Spread:
    def __init__(self, tensors):
        self.operands = list(tensors)
        self.out_shape = [jax.ShapeDtypeStruct(t.shape, t.dtype) for t in tensors]
        n = len(tensors)
        self.scratch = [pltpu.SemaphoreType.DMA((n, N_CHIP - 1)), pltpu.SemaphoreType.DMA((n, N_CHIP - 1))]

    def _copies(self, t, y_ref, out_ref, scr, outgoing):
        send, recv = scr
        x, y, c = _place()
        copies = []
        for k in range(1, N_CHIP):
            px, py = x ^ (k >> 1), y ^ (k & 1)
            copies.append(pltpu.make_async_remote_copy(
                src_ref=y_ref.at[2 * px + py], dst_ref=out_ref.at[2 * x + y if outgoing else 2 * px + py],
                send_sem=send.at[t, k - 1], recv_sem=recv.at[t, k - 1], device_id=(px, py, c),
                device_id_type=pl.DeviceIdType.MESH))
        return copies

    def start(self, ins, outs, scr):
        for t, (y_ref, out_ref) in enumerate(zip(ins, outs)):
            for give in self._copies(t, y_ref, out_ref, scr, True):
                give.start()

    def finish(self, ins, outs, scr):
        for t, (y_ref, out_ref) in enumerate(zip(ins, outs)):
            for take in self._copies(t, y_ref, out_ref, scr, False):
                take.wait_recv()
            for give in self._copies(t, y_ref, out_ref, scr, True):
                give.wait_send()

    def set_results(self, res):
        self.results = list(res)


class Joint:
    def __init__(self, parts):
        self.parts = list(parts)
        self.operands = [o for p in parts for o in p.operands]
        self.out_shape = [o for p in parts for o in p.out_shape]
        self.scratch = [o for p in parts for o in p.scratch]

    def _split(self, ins, outs, scr):
        i = o = s = 0
        for p in self.parts:
            ni, no, ns = len(p.operands), len(p.out_shape), len(p.scratch)
            yield p, ins[i:i + ni], outs[o:o + no], scr[s:s + ns]
            i, o, s = i + ni, o + no, s + ns

    def start(self, ins, outs, scr):
        for p, a, b, c in self._split(ins, outs, scr):
            p.start(a, b, c)

    def finish(self, ins, outs, scr):
        for p, a, b, c in self._split(ins, outs, scr):
            p.finish(a, b, c)

    def set_results(self, res):
        for p, _, part, _ in self._split([], list(res), []):
            p.set_results(part)


_HBM = pl.BlockSpec(memory_space=pltpu.HBM)


def run_alone(name, carry):
    n_in, n_out = len(carry.operands), len(carry.out_shape)

    def body(*refs):
        ins, outs, scr = refs[:n_in], refs[n_in:n_in + n_out], refs[n_in + n_out:]
        carry.start(ins, outs, scr)
        carry.finish(ins, outs, scr)

    res = pl.pallas_call(body, name=name, in_specs=[_HBM] * n_in, out_specs=[_HBM] * n_out,
                         out_shape=carry.out_shape, scratch_shapes=carry.scratch)(*carry.operands)
    carry.set_results(list(res))


def _call(name, body, *, grid, in_specs, out_specs, out_shape, args, scratch=(), carry=None):
    in_specs, out_specs, out_shape, scratch = list(in_specs), list(out_specs), list(out_shape), list(scratch)
    if carry is None:
        return list(pl.pallas_call(body, name=name, grid=grid, in_specs=in_specs, out_specs=out_specs,
                                   out_shape=out_shape, scratch_shapes=scratch, compiler_params=_cparams())(*args))
    n_in, n_out, n_scr = len(in_specs), len(out_shape), len(scratch)
    c_in, c_out = len(carry.operands), len(carry.out_shape)

    def wrapped(*refs):
        ins, c_ins = refs[:n_in], refs[n_in:n_in + c_in]
        at = n_in + c_in
        outs, c_outs = refs[at:at + n_out], refs[at + n_out:at + n_out + c_out]
        at += n_out + c_out
        scr, c_scr = refs[at:at + n_scr], refs[at + n_scr:]
        first = functools.reduce(jnp.logical_and, [pl.program_id(d) == 0 for d in range(len(grid))])
        last = functools.reduce(jnp.logical_and, [pl.program_id(d) == grid[d] - 1 for d in range(len(grid))])

        @pl.when(first)
        def _():
            carry.start(c_ins, c_outs, c_scr)

        body(*ins, *outs, *scr)

        @pl.when(last)
        def _():
            carry.finish(c_ins, c_outs, c_scr)

    res = pl.pallas_call(
        wrapped, name=name, grid=grid, in_specs=in_specs + [_HBM] * c_in, out_specs=out_specs + [_HBM] * c_out,
        out_shape=out_shape + carry.out_shape, scratch_shapes=scratch + carry.scratch, compiler_params=_cparams(),
    )(*args, *carry.operands)
    carry.set_results(list(res[n_out:]))
    return list(res[:n_out])


def R_(arr, w=None, off=0, stride=0):
    return ('r', arr, arr.shape[1] if w is None else w, off, stride)


def P_(arr, w=None, off=0, stride=0):
    return ('p', arr, arr.shape[1] if w is None else w, off, stride)


def rowmap(name, fn, ins, outs=None, *, tile, ncol=1, rows_inner=True, cts=None, wrt=None, gdt=None, cat=False,
           carry=None):
    rows = next(a.shape[0] for k, a, *_ in ins if k == 'r')
    nrow = rows // tile
    assert nrow * tile == rows
    grid = (ncol, nrow) if rows_inner else (nrow, ncol)

    def ij(g0, g1):
        return (g1, g0) if rows_inner else (g0, g1)

    def spec(kind, arr, w, off, stride):
        if kind == 'r':
            return pl.BlockSpec((tile, w), lambda g0, g1: (ij(g0, g1)[0], off + stride * ij(g0, g1)[1]))
        return pl.BlockSpec((arr.shape[0], w), lambda g0, g1: (0, off + stride * ij(g0, g1)[1]))

    ops = list(ins) + list(cts or [])
    in_specs = [spec(*o) for o in ops]
    n_in = len(ins)
    fwd = cts is None
    out_shape, out_specs, acc = [], [], []
    if fwd:
        for dt, w, stride in outs:
            out_shape.append(jax.ShapeDtypeStruct((rows, w * (ncol if stride else 1)), dt))
            out_specs.append(spec('r', None, w, 0, stride))
            acc.append(None)
    elif cat:
        widths = [ins[i][2] for i in wrt]
        assert ncol == 1 and all(ins[i][0] == 'r' for i in wrt)
        out_shape.append(jax.ShapeDtypeStruct((rows, sum(widths)), gdt))
        out_specs.append(spec('r', None, sum(widths), 0, 0))
    else:
        for n, i in enumerate(wrt):
            kind, arr, w, off, stride = ins[i]
            width = w * (ncol if stride else 1)
            if kind == 'r':
                out_shape.append(jax.ShapeDtypeStruct((rows, width), gdt[n]))
                out_specs.append(spec('r', None, w, 0, 1 if stride else 0))
                shared = stride == 0 and ncol > 1
                assert not shared or (not rows_inner and gdt[n] == F32)
                acc.append('col' if shared else None)
            else:
                out_shape.append(jax.ShapeDtypeStruct((arr.shape[0], width), F32))
                out_specs.append(spec('p', arr, w, 0, 1 if stride else 0))
                assert rows_inner or ncol == 1
                acc.append('row')

    def body(*refs):
        i, j = ij(pl.program_id(0), pl.program_id(1))
        vals = [r[...].astype(F32) for r in refs[:n_in]]
        out_refs = refs[len(ops):]
        if fwd:
            for o_ref, o in zip(out_refs, fn(*vals)):
                o_ref[...] = o.astype(o_ref.dtype)
            return

        def f(*d):
            full = list(vals)
            for n, idx in enumerate(wrt):
                full[idx] = d[n]
            return fn(*full)

        _, vjp = jax.vjp(f, *[vals[idx] for idx in wrt])
        grads = vjp(tuple(r[...].astype(F32) for r in refs[n_in:len(ops)]))
        if cat:
            o_ref, at = out_refs[0], 0
            for g in grads:
                o_ref[:, at:at + g.shape[1]] = g.astype(o_ref.dtype)
                at += g.shape[1]
            return
        for o_ref, g, a in zip(out_refs, grads, acc):
            if a is None:
                o_ref[...] = g.astype(o_ref.dtype)
            else:
                first = (i if a == 'row' else j) == 0

                @pl.when(first)
                def _(o_ref=o_ref):
                    o_ref[...] = jnp.zeros_like(o_ref)

                o_ref[...] += g

    return _call(name, body, grid=grid, in_specs=in_specs, out_specs=out_specs, out_shape=out_shape,
                 args=[o[1] for o in ops], carry=carry)


def _rms(x, g):
    return x * lax.rsqrt(jnp.mean(x * x, axis=-1, keepdims=True) + EPS) * g


_DIMS = {'nn': (((1,), (0,)), ((), ())), 'nt': (((1,), (1,)), ((), ())), 'tn': (((0,), (0,)), ((), ()))}


def mm(name, a, b, mode, *, jdim=None, out_dtype=F32, carry=None):
    a_list = list(a) if isinstance(a, (list, tuple)) else [a]
    b_list = list(b) if isinstance(b, (list, tuple)) else [b]
    a_order = ('k', 'm') if mode == 'tn' else ('m', 'k')
    b_order = ('n', 'k') if mode == 'nt' else ('k', 'n')
    size, blocks = {}, 1
    for arr, order in ((a_list[0], a_order), (b_list[0], b_order)):
        shape = arr.shape
        if jdim in order:
            blocks, shape = shape[0], shape[1:]
        for dname, extent in zip(order, shape):
            assert size.setdefault(dname, extent) == extent
    tile = {'m': _pick(size['m'], (1024, 512, 256, 128)), 'n': _pick(size['n'], (512, 256, 128)),
            'k': size['k'] if size['k'] <= 2048 else _pick(size['k'], (2304, 2048, 1024, 512, 256, 128))}
    if jdim is not None:
        tile[jdim] = size[jdim]
    if a_list[0].dtype == F32 and tile['k'] * tile['m'] > (1 << 20):
        tile['m'] = _pick(size['m'], (512, 256, 128))
    grid = tuple(blocks if d == jdim else size[d] // tile[d] for d in ('m', 'n', 'k'))
    nk = grid[2]

    def spec(order):
        shape = tuple(tile[d] for d in order)

        def imap(i, j, k):
            g = {'m': i, 'n': j, 'k': k}
            idx = tuple(0 if d == jdim else g[d] for d in order)
            return ((g[jdim],) + idx) if jdim in order else idx

        return pl.BlockSpec(((None,) + shape) if jdim in order else shape, imap)

    dims = _DIMS[mode]
    nt = len(a_list)

    def product(refs):
        acc = None
        for a_ref, b_ref in zip(refs[:nt], refs[nt:2 * nt]):
            p = lax.dot_general(a_ref[...].astype(BF16), b_ref[...].astype(BF16), dims, preferred_element_type=F32)
            acc = p if acc is None else acc + p
        return acc

    def body_once(*refs):
        refs[2 * nt][...] = product(refs).astype(refs[2 * nt].dtype)

    def body_acc(*refs):
        o_ref, acc_ref = refs[2 * nt], refs[2 * nt + 1]
        k = pl.program_id(2)

        @pl.when(k == 0)
        def _():
            acc_ref[...] = jnp.zeros_like(acc_ref)

        acc_ref[...] += product(refs)

        @pl.when(k == nk - 1)
        def _():
            o_ref[...] = acc_ref[...].astype(o_ref.dtype)

    out_dims = (size['m'], size['n'])
    out_shape = jax.ShapeDtypeStruct(((blocks,) + out_dims) if jdim in ('m', 'n') else out_dims, out_dtype)
    return _call(name, body_once if nk == 1 else body_acc, grid=grid,
                 in_specs=[spec(a_order)] * nt + [spec(b_order)] * nt, out_specs=[spec(('m', 'n'))],
                 out_shape=[out_shape], args=a_list + b_list,
                 scratch=[] if nk == 1 else [pltpu.VMEM((tile['m'], tile['n']), F32)], carry=carry)[0]


def _probs(q, k, i, tq, scale, causal):
    s = lax.dot_general(q, k, _DIMS['nt'], preferred_element_type=F32) * scale
    if causal:
        shape = s.shape
        q_chunk = jnp.right_shift(i * tq + lax.broadcasted_iota(jnp.int32, shape, 0), CHUNK_SHIFT)
        k_chunk = jnp.right_shift(lax.broadcasted_iota(jnp.int32, shape, 1), CHUNK_SHIFT)
        s = jnp.where(k_chunk <= q_chunk, s, -jnp.inf)
    e = jnp.exp(s - jnp.max(s, axis=-1, keepdims=True))
    return e, jnp.sum(e, axis=-1, keepdims=True)


def _per_prefix(work, i, tq, sq, sk, causal):
    if not causal:
        work(sk)
        return
    assert sq == sk and tq % CHUNK == 0
    for j in range(sq // tq):
        @pl.when(i == j)
        def _(j=j):
            work((j + 1) * tq)


def _attn_specs(tq, sk, dq, dv, koff, kstride, voff, vstride):
    return [pl.BlockSpec((tq, dq), lambda h, i: (i, h)),
            pl.BlockSpec((sk, dq), lambda h, i: (0, koff + kstride * h)),
            pl.BlockSpec((sk, dv), lambda h, i: (0, voff + vstride * h))]


def attn_fwd(name, q, k, v, *, heads, dq, dv, koff, kstride, voff, vstride, scale, causal, carry=None):
    sq, sk = q.shape[0], k.shape[0]
    tq = min(sq, 256)

    def body(q_ref, k_ref, v_ref, o_ref):
        i = pl.program_id(1)

        def work(keys):
            e, l = _probs(q_ref[...].astype(BF16), k_ref[0:keys, :].astype(BF16), i, tq, scale, causal)
            o = jnp.dot(e.astype(BF16), v_ref[0:keys, :].astype(BF16), preferred_element_type=F32)
            o_ref[...] = o / l

        _per_prefix(work, i, tq, sq, sk, causal)

    return _call(name, body, grid=(heads, sq // tq),
                 in_specs=_attn_specs(tq, sk, dq, dv, koff, kstride, voff, vstride),
                 out_specs=[pl.BlockSpec((tq, dv), lambda h, i: (i, h))],
                 out_shape=[jax.ShapeDtypeStruct((sq, heads * dv), F32)], args=[q, k, v], carry=carry)[0]


def attn_bwd(name, q, k, v, do, *, heads, dq, dv, koff, kstride, voff, vstride, scale, causal, carry=None):
    sq, sk = q.shape[0], k.shape[0]
    tq = min(sq, 256)

    def body(q_ref, k_ref, v_ref, do_ref, dq_ref, dk_ref, dv_ref):
        i = pl.program_id(1)

        @pl.when(i == 0)
        def _():
            dk_ref[...] = jnp.zeros_like(dk_ref)
            dv_ref[...] = jnp.zeros_like(dv_ref)

        def work(keys):
            qb, kb, vb = q_ref[...].astype(BF16), k_ref[0:keys, :].astype(BF16), v_ref[0:keys, :].astype(BF16)
            dob = do_ref[...].astype(BF16)
            e, l = _probs(qb, kb, i, tq, scale, causal)
            p = e / l
            dp = lax.dot_general(dob, vb, _DIMS['nt'], preferred_element_type=F32)
            ds = (p * (dp - jnp.sum(dp * p, axis=-1, keepdims=True)) * scale).astype(BF16)
            dv_ref[0:keys, :] += lax.dot_general(p.astype(BF16), dob, _DIMS['tn'], preferred_element_type=F32)
            dk_ref[0:keys, :] += lax.dot_general(ds, qb, _DIMS['tn'], preferred_element_type=F32)
            dq_ref[...] = jnp.dot(ds, kb, preferred_element_type=F32)

        _per_prefix(work, i, tq, sq, sk, causal)

    return _call(
        name, body, grid=(heads, sq // tq),
        in_specs=_attn_specs(tq, sk, dq, dv, koff, kstride, voff, vstride) + [pl.BlockSpec((tq, dv), lambda h, i: (i, h))],
        out_specs=[pl.BlockSpec((tq, dq), lambda h, i: (i, h)), pl.BlockSpec((sk, dq), lambda h, i: (0, h)),
                   pl.BlockSpec((sk, dv), lambda h, i: (0, h))],
        out_shape=[jax.ShapeDtypeStruct((sq, heads * dq), F32), jax.ShapeDtypeStruct((sk, heads * dq), F32),
                   jax.ShapeDtypeStruct((sk, heads * dv), F32)],
        args=[q, k, v, do], carry=carry)


def _hg_chunk(q, k, g, v, state):
    c = q.shape[0]
    row = lax.broadcasted_iota(jnp.int32, (c, c), 0)
    col = lax.broadcasted_iota(jnp.int32, (c, c), 1)
    tril = (col <= row).astype(F32)
    b = jnp.dot(tril, g, precision=lax.Precision.HIGHEST, preferred_element_type=F32)
    rows = lax.broadcasted_iota(jnp.int32, (c, 1), 0)
    o = jnp.dot((q * jnp.exp(b)).astype(BF16), state.astype(BF16), preferred_element_type=F32)
    t3 = lax.broadcasted_iota(jnp.int32, (SUB, SUB, 1), 0)
    s3 = lax.broadcasted_iota(jnp.int32, (SUB, SUB, 1), 1)
    parts = []
    for n in range(c // SUB):
        lo = n * SUB
        qn, kn, bn, vn = q[lo:lo + SUB], k[lo:lo + SUB], b[lo:lo + SUB], v[lo:lo + SUB]
        decay = jnp.exp(jnp.where(s3 <= t3, bn[:, None, :] - bn[None, :, :], -jnp.inf))
        sc = jnp.sum(qn[:, None, :] * kn[None, :, :] * decay, axis=-1)
        on = jnp.dot(sc.astype(BF16), vn.astype(BF16), preferred_element_type=F32)
        if n > 0:
            ref = jnp.sum(jnp.where(rows == lo - 1, b, 0.0), axis=0, keepdims=True)
            qd = (qn * jnp.exp(bn - ref)).astype(BF16)
            kd = (k[:lo] * jnp.exp(ref - b[:lo])).astype(BF16)
            so = lax.dot_general(qd, kd, _DIMS['nt'], preferred_element_type=F32)
            on = on + jnp.dot(so.astype(BF16), v[:lo].astype(BF16), preferred_element_type=F32)
        parts.append(on)
    o = o + jnp.concatenate(parts, axis=0)
    b_last = jnp.sum(g, axis=0, keepdims=True)
    ones = jnp.ones((c, 1), F32)
    b_last_col = lax.dot_general(g, ones, _DIMS['tn'], precision=lax.Precision.HIGHEST, preferred_element_type=F32)
    kd = (k * jnp.exp(b_last - b)).astype(BF16)
    new_state = jnp.exp(b_last_col) * state + lax.dot_general(kd, v.astype(BF16), _DIMS['tn'],
                                                              preferred_element_type=F32)
    return o, new_state


def _hg_chunk_bwd(q, k, g, v, state, do, dnew):
    c, kw = q.shape
    hi = lax.Precision.HIGHEST
    row = lax.broadcasted_iota(jnp.int32, (c, c), 0)
    col = lax.broadcasted_iota(jnp.int32, (c, c), 1)
    b = jnp.dot((col <= row).astype(F32), g, precision=hi, preferred_element_type=F32)
    rows = lax.broadcasted_iota(jnp.int32, (c, 1), 0)
    b_last = jnp.sum(g, axis=0, keepdims=True)
    b_last_col = lax.dot_general(g, jnp.ones((c, 1), F32), _DIMS['tn'], precision=hi, preferred_element_type=F32)
    eb, to_end = jnp.exp(b), jnp.exp(b_last - b)
    dob, vb, dnb = do.astype(BF16), v.astype(BF16), dnew.astype(BF16)
    k_end = (k * to_end).astype(BF16)
    dq = eb * lax.dot_general(dob, state.astype(BF16), _DIMS['nt'], preferred_element_type=F32)
    dk = to_end * lax.dot_general(vb, dnb, _DIMS['nt'], preferred_element_type=F32)
    dv = jnp.dot(k_end, dnb, preferred_element_type=F32)
    dstate = jnp.exp(b_last_col) * dnew + lax.dot_general((q * eb).astype(BF16), dob, _DIMS['tn'],
                                                          preferred_element_type=F32)
    new_state = jnp.exp(b_last_col) * state + lax.dot_general(k_end, vb, _DIMS['tn'], preferred_element_type=F32)
    db_end = lax.dot_general(jnp.ones((1, dnew.shape[1]), F32), dnew * new_state, _DIMS['nt'], precision=hi,
                             preferred_element_type=F32)
    t3 = lax.broadcasted_iota(jnp.int32, (SUB, SUB, 1), 0)
    s3 = lax.broadcasted_iota(jnp.int32, (SUB, SUB, 1), 1)
    dq_rows, dk_rows, dv_rows = [], [], []
    for n in range(c // SUB):
        lo = n * SUB
        qn, kn, bn, vn, don = q[lo:lo + SUB], k[lo:lo + SUB], b[lo:lo + SUB], vb[lo:lo + SUB], dob[lo:lo + SUB]
        decay = jnp.exp(jnp.where(s3 <= t3, bn[:, None, :] - bn[None, :, :], -jnp.inf))
        sc = jnp.sum(qn[:, None, :] * kn[None, :, :] * decay, axis=-1)
        pull = lax.dot_general(don, vn, _DIMS['nt'], preferred_element_type=F32)[:, :, None] * decay
        dqn = jnp.sum(pull * kn[None, :, :], axis=1)
        dk_rows.append(jnp.sum(pull * qn[:, None, :], axis=0))
        dv_rows.append(lax.dot_general(sc.astype(BF16), don, _DIMS['tn'], preferred_element_type=F32))
        if n > 0:
            ref = jnp.sum(jnp.where(rows == lo - 1, b, 0.0), axis=0, keepdims=True)
            up, down = jnp.exp(bn - ref), jnp.exp(ref - b[:lo])
            qd, kd = (qn * up).astype(BF16), (k[:lo] * down).astype(BF16)
            so = lax.dot_general(qd, kd, _DIMS['nt'], preferred_element_type=F32).astype(BF16)
            ao = lax.dot_general(don, vb[:lo], _DIMS['nt'], preferred_element_type=F32).astype(BF16)
            dqn = dqn + up * jnp.dot(ao, kd, preferred_element_type=F32)
            rest = jnp.zeros((c - lo, kw), F32)
            dk = dk + jnp.concatenate([down * lax.dot_general(ao, qd, _DIMS['tn'], preferred_element_type=F32),
                                       rest], axis=0)
            dv = dv + jnp.concatenate([lax.dot_general(so, don, _DIMS['tn'], preferred_element_type=F32), rest],
                                      axis=0)
        dq_rows.append(dqn)
    dq = dq + jnp.concatenate(dq_rows, axis=0)
    dk = dk + jnp.concatenate(dk_rows, axis=0)
    dv = dv + jnp.concatenate(dv_rows, axis=0)
    db = q * dq - k * dk + jnp.where(rows == c - 1, db_end, 0.0)
    dg = jnp.dot((col >= row).astype(F32), db, precision=hi, preferred_element_type=F32)
    return dq, dk, dg, dv, dstate


def hg_scan_fwd(name, q, k, g, u, v_off, carry=None):
    s = q.shape[0]
    n = s // CHUNK

    def body(q_ref, k_ref, g_ref, v_ref, o_ref, st_ref, state):
        @pl.when(pl.program_id(1) == 0)
        def _():
            state[...] = jnp.zeros_like(state)

        for j in range(HG_GROUP):
            cols = slice(j * HEAD_W, (j + 1) * HEAD_W)
            st = state[j]
            st_ref[j] = st
            o, new = _hg_chunk(q_ref[:, cols], k_ref[:, cols], g_ref[:, cols], v_ref[:, cols], st)
            o_ref[:, cols] = o
            state[j] = new

    wide = HG_GROUP * HEAD_W
    blk = pl.BlockSpec((CHUNK, wide), lambda h, c: (c, h))
    return _call(
        name, body, grid=(HG_HEADS // HG_GROUP, n),
        in_specs=[blk, blk, blk, pl.BlockSpec((CHUNK, wide), lambda h, c: (c, v_off // HG_GROUP + h))],
        out_specs=[blk, pl.BlockSpec((HG_GROUP, None, HEAD_W, HEAD_W), lambda h, c: (h, c, 0, 0))],
        out_shape=[jax.ShapeDtypeStruct((s, HG_HEADS * HEAD_W), F32),
                   jax.ShapeDtypeStruct((HG_HEADS, n, HEAD_W, HEAD_W), F32)],
        scratch=[pltpu.VMEM((HG_GROUP, HEAD_W, HEAD_W), F32)], args=[q, k, g, u], carry=carry)


def hg_scan_bwd(name, q, k, g, u, v_off, states, do, carry=None):
    s = q.shape[0]
    n = s // CHUNK

    def body(q_ref, k_ref, g_ref, v_ref, st_ref, do_ref, dq_ref, dk_ref, dg_ref, dv_ref, dstate):
        @pl.when(pl.program_id(1) == 0)
        def _():
            dstate[...] = jnp.zeros_like(dstate)

        for j in range(HG_GROUP):
            cols = slice(j * HEAD_W, (j + 1) * HEAD_W)
            dq, dk, dg, dv, dst = _hg_chunk_bwd(q_ref[:, cols], k_ref[:, cols], g_ref[:, cols], v_ref[:, cols],
                                                st_ref[j], do_ref[:, cols], dstate[j])
            dq_ref[:, cols] = dq
            dk_ref[:, cols] = dk
            dg_ref[:, cols] = dg
            dv_ref[:, cols] = dv
            dstate[j] = dst

    wide = HG_GROUP * HEAD_W
    blk = pl.BlockSpec((CHUNK, wide), lambda h, c: (n - 1 - c, h))
    out = jax.ShapeDtypeStruct((s, HG_HEADS * HEAD_W), F32)
    return _call(
        name, body, grid=(HG_HEADS // HG_GROUP, n),
        in_specs=[blk, blk, blk, pl.BlockSpec((CHUNK, wide), lambda h, c: (n - 1 - c, v_off // HG_GROUP + h)),
                  pl.BlockSpec((HG_GROUP, None, HEAD_W, HEAD_W), lambda h, c: (h, n - 1 - c, 0, 0)), blk],
        out_specs=[blk, blk, blk, blk], out_shape=[out, out, out, out],
        scratch=[pltpu.VMEM((HG_GROUP, HEAD_W, HEAD_W), F32)], args=[q, k, g, u, states, do], carry=carry)


def pair_sum(name, blocks, theirs):
    _, rows, cols = theirs.shape
    tile = next((t for t in (1024, 512, 256, 128, 64, 32, 16) if rows % t == 0 and t * cols <= (1 << 20)), rows)

    def body(a_ref, b_ref, o_ref):
        mine = jnp.where(lax.axis_index("c") == 0, a_ref[0].astype(F32), a_ref[1].astype(F32))
        o_ref[...] = (mine + b_ref[...].astype(F32)).astype(o_ref.dtype)

    blk = pl.BlockSpec((None, tile, cols), lambda q, i: (q, i, 0))
    return _call(name, body, grid=(N_CHIP, rows // tile),
                 in_specs=[pl.BlockSpec((None, 2, tile, cols), lambda q, i: (q, 0, i, 0)), blk], out_specs=[blk],
                 out_shape=[jax.ShapeDtypeStruct(theirs.shape, theirs.dtype)],
                 args=[blocks.reshape(N_CHIP, 2, rows, cols), theirs])[0]


def adamw(name, landed, w, m, v):
    rows, cols = w.shape
    slots = landed.shape[0]
    tile = next((t for t in (512, 256, 128, 64, 32, 16, 8) if rows % t == 0 and t * cols <= (1 << 18)), rows)

    def body(l_ref, w_ref, m_ref, v_ref, g_ref, d_ref, nm_ref, nv_ref):
        gv = l_ref[0].astype(F32)
        for s in range(1, slots):
            gv = gv + l_ref[s].astype(F32)
        nm = ADAM_B1 * m_ref[...] + (1.0 - ADAM_B1) * gv
        nv = ADAM_B2 * v_ref[...] + (1.0 - ADAM_B2) * jnp.square(gv)
        m_hat = nm / (1.0 - ADAM_B1 ** ADAM_STEP)
        v_hat = nv / (1.0 - ADAM_B2 ** ADAM_STEP)
        g_ref[...] = gv
        d_ref[...] = -ADAM_LR * (m_hat / (jnp.sqrt(v_hat) + ADAM_EPS) + ADAM_WD * w_ref[...])
        nm_ref[...] = nm
        nv_ref[...] = nv

    blk = pl.BlockSpec((tile, cols), lambda i: (i, 0))
    out = jax.ShapeDtypeStruct((rows, cols), F32)
    return _call(name, body, grid=(rows // tile,),
                 in_specs=[pl.BlockSpec((slots, tile, cols), lambda i: (0, i, 0)), blk, blk, blk],
                 out_specs=[blk] * 4, out_shape=[out] * 4, args=[landed, w, m, v])


def _swap_halves(pe):
    half = QK_ROPE // 2
    return jnp.concatenate([-pe[..., half:], pe[..., :half]], axis=-1)


def _unswap_halves(dsw):
    half = QK_ROPE // 2
    return jnp.concatenate([dsw[..., half:], -dsw[..., :half]], axis=-1)


def _w_in_ext(w):
    kpe = w[:, 9216:9280]
    z = jnp.zeros((w.shape[0], HEAD_W - QK_ROPE), w.dtype)
    pad = jnp.zeros((w.shape[0], U_PAD), w.dtype)
    return jnp.concatenate([w[:, :9216], w[:, 9280:], kpe, z, _swap_halves(kpe), z, pad], axis=1)


def _w_in_grad(d):
    dkpe = d[:, 13312:13376] + _unswap_halves(d[:, 13440:13504])
    return jnp.concatenate([d[:, :9216], dkpe, d[:, 9216:13312]], axis=1)


def _w_q_ext(w):
    w3 = w.reshape(Q_LORA, MLA_HEADS, MLA_QK)
    pe = w3[:, :, HEAD_W:]
    z = jnp.zeros((Q_LORA, MLA_HEADS, HEAD_W - QK_ROPE), w.dtype)
    wide = MLA_HEADS * HEAD_W
    return jnp.concatenate([w3[:, :, :HEAD_W].reshape(Q_LORA, wide),
                            jnp.concatenate([pe, z], axis=2).reshape(Q_LORA, wide),
                            jnp.concatenate([_swap_halves(pe), z], axis=2).reshape(Q_LORA, wide)], axis=1)


def _w_q_grad(parts):
    d3 = [p.reshape(Q_LORA, MLA_HEADS, HEAD_W) for p in parts]
    dpe = d3[1][:, :, :QK_ROPE] + _unswap_halves(d3[2][:, :, :QK_ROPE])
    return jnp.concatenate([d3[0], dpe], axis=2).reshape(Q_LORA, MLA_HEADS * MLA_QK)


def _hg_prep(f_raw, q_hg, logits):
    lb = jax.nn.softmax(logits, axis=0)[0:1, :]
    log_f = jnp.logaddexp(jnp.log(lb), jnp.log1p(-lb) + jax.nn.log_sigmoid(f_raw))
    k_in = (1.0 - lb) * jax.nn.sigmoid(-f_raw)
    return log_f, k_in, jax.nn.silu(q_hg)


def _rope(q_nope, q_pe, q_sw, k_nope, k_pe, k_sw, cos, sin):
    qf = jnp.concatenate([q_nope, q_pe * cos + q_sw * sin], axis=1)
    kf = jnp.concatenate([k_nope, k_pe * cos + k_sw * sin], axis=1)
    return qf, kf


def _step(a):
    x, mem, target = a['x'][0], a['mem'][0], a['loss_target'][0]
    s, d = x.shape
    nm = mem.shape[0]
    ff = N_DEV * a['ffn1_w_gate'].shape[-1]
    cs = ff // N_DEV
    tr = min(s, 256)
    tp = min(s, 128)
    th = s
    ta = 1024

    bf = {n: a[n][0].astype(BF16) for n in BIG}
    half_in = bf['w_in'].shape[0] // 2
    bf['w_in#0'], bf['w_in#1'] = bf['w_in'][:half_in], bf['w_in'][half_in:]
    gat, full = {}, {}

    my_chip = 2 * lax.axis_index("x") + lax.axis_index("y")
    me = 2 * my_chip + lax.axis_index("c")

    def gathered(names, carry):
        for n, g8 in zip(names or [], carry.results if carry else []):
            r, c = bf[n].shape
            g8 = lax.dynamic_update_index_in_dim(g8, bf[n], me, 0)
            gat[n] = g8
            if not n.startswith('ffn'):
                by_col = BIG[n.split('#')[0]]
                full[n] = g8.transpose(1, 0, 2).reshape(r, N_DEV * c) if by_col else g8.reshape(N_DEV * r, c)

    xa_names = ['xa_w_q', 'xa_w_k', 'xa_w_v', 'xa_w_o']
    ffn2_names = ['ffn2_w_gate', 'ffn2_w_up', 'ffn2_w_down']
    first = Gather([bf['ffn1_w_gate']])
    run_alone("gather_first", first)
    gathered(['ffn1_w_gate'], first)

    inv_freq = 1.0 / (ROPE_THETA ** (jnp.arange(0, QK_ROPE, 2, dtype=F32) / QK_ROPE))
    ang = a['positions'][0].astype(F32)[:, None] * inv_freq
    zero = jnp.zeros((s, HEAD_W - QK_ROPE), F32)
    cos = jnp.concatenate([jnp.cos(ang), jnp.cos(ang), zero], axis=1)
    sin = jnp.concatenate([jnp.sin(ang), jnp.sin(ang), zero], axis=1)

    gs = {}
    gb = {}

    def by_rows(g):
        return g.reshape(N_DEV, g.shape[0] // N_DEV, g.shape[1])

    def by_cols(g):
        return g.reshape(g.shape[0], N_DEV, g.shape[1] // N_DEV).transpose(1, 0, 2)

    pre_fn = lambda xv, g: (_rms(xv, g),)
    pre_res_fn = lambda xv, g: (_rms(xv, g), xv)

    def pre_norm(tag, x_in, g):
        return rowmap(tag + "_pre", pre_fn, [R_(x_in), P_(g)], [(BF16, x_in.shape[1], 0)], tile=tr)[0]

    def pre_norm_bwd(tag, x_in, g, dh, d_out, carry=None):
        return rowmap(tag + "_pre_bwd", pre_res_fn, [R_(x_in), P_(g)], tile=tr, cts=[R_(dh), R_(d_out)],
                      wrt=[0, 1], gdt=[F32, F32], carry=carry)

    def post_fn(weight):
        return lambda xv, y, g: (xv + weight * _rms(y, g),)

    def post_norm(tag, x_in, y, g, weight):
        return rowmap(tag + "_post", post_fn(weight), [R_(x_in), R_(y), P_(g)], [(F32, d, 0)], tile=tr)[0]

    def post_norm_bwd(tag, x_in, y, g, weight, d_out):
        return rowmap(tag + "_post_bwd", post_fn(weight), [R_(x_in), R_(y), P_(g)], tile=tr, cts=[R_(d_out)],
                      wrt=[1, 2], gdt=[BF16, F32])

    act_fn = lambda av, bv: (jax.nn.silu(av) * bv,)

    def fetch(plan, key):
        names = plan.get(key)
        return (names, Gather([bf[n] for n in names])) if names else (None, None)

    def send(plan, key):
        acts = plan.get(key)
        if not acts:
            return None, None
        parts = []
        for kind, names in acts:
            if kind == 'spread':
                for n in names:
                    chip_sums[n] = pair_sum("pair_" + n.replace('#', '_'), gb[n], halves[n])
            parts.append(Halve([gb[n] for n in names]) if kind == 'halve' else Spread([chip_sums[n] for n in names]))
        return acts, Joint(parts)

    def ffn_fwd(tag, x_in, plan):
        h = pre_norm(tag, x_in, a[tag + '_pre_g'])
        names, carry = fetch(plan, 'gate')
        av = mm(tag + "_gate", h, gat[tag + '_w_gate'], 'nn', jdim='n', out_dtype=BF16, carry=carry)
        gathered(names, carry)
        names, carry = fetch(plan, 'up')
        bv = mm(tag + "_up", h, gat[tag + '_w_up'], 'nn', jdim='n', out_dtype=BF16, carry=carry)
        gathered(names, carry)
        a2, b2 = av.reshape(N_DEV * s, cs), bv.reshape(N_DEV * s, cs)
        names, carry = fetch(plan, 'act')
        z = rowmap(tag + "_act", act_fn, [R_(a2), R_(b2)], [(BF16, cs, 0)], tile=ta, carry=carry)[0]
        z = z.reshape(N_DEV, s, cs)
        gathered(names, carry)
        names, carry = fetch(plan, 'dn')
        y = mm(tag + "_dn", z, gat[tag + '_w_down'], 'nn', jdim='k', carry=carry)
        gathered(names, carry)
        return post_norm(tag, x_in, y, a[tag + '_post_g'], 0.5), (x_in, h, a2, b2, z, y)

    def ffn_bwd(tag, res, d_out, plan):
        x_in, h, a2, b2, z, y = res
        dy, gs[tag + '_post_g'] = post_norm_bwd(tag, x_in, y, a[tag + '_post_g'], 0.5, d_out)
        acts, carry = send(plan, 'dn_dx')
        dz = mm(tag + "_dn_dx", dy, gat[tag + '_w_down'], 'nt', jdim='n', carry=carry)
        exchanged(acts, carry)
        acts, carry = send(plan, 'dn_dw')
        gb[tag + '_w_down'] = mm(tag + "_dn_dw", z, dy, 'tn', jdim='m', out_dtype=BF16, carry=carry)
        exchanged(acts, carry)
        acts, carry = send(plan, 'act_bwd')
        da, db = rowmap(tag + "_act_bwd", act_fn, [R_(a2), R_(b2)], tile=ta, cts=[R_(dz.reshape(N_DEV * s, cs))],
                        wrt=[0, 1], gdt=[BF16, BF16], carry=carry)
        exchanged(acts, carry)
        da, db = da.reshape(N_DEV, s, cs), db.reshape(N_DEV, s, cs)
        names, carry = send(plan, 'gu_dx')
        dh = mm(tag + "_gu_dx", [da, db], [gat[tag + '_w_gate'], gat[tag + '_w_up']], 'nt', jdim='k', carry=carry)
        exchanged(names, carry)
        gb[tag + '_w_gate'] = mm(tag + "_gate_dw", h, da, 'tn', jdim='n', out_dtype=BF16)
        names, carry = send(plan, 'up_dw')
        gb[tag + '_w_up'] = mm(tag + "_up_dw", h, db, 'tn', jdim='n', out_dtype=BF16, carry=carry)
        exchanged(names, carry)
        names, carry = send(plan, 'pre_bwd')
        d_in, gs[tag + '_pre_g'] = pre_norm_bwd(tag, x_in, a[tag + '_pre_g'], dh, d_out, carry)
        exchanged(names, carry)
        return d_in

    hg_out_fn = lambda o, og, g: (_rms(o, g) * jax.nn.silu(og),)
    mla_norm_fn = lambda cq, ckv, gq, gkv: (_rms(cq, gq), _rms(ckv, gkv))
    gate_fn = lambda ga, gb, ya, yb: (jax.nn.sigmoid(ga) * ya + jax.nn.sigmoid(gb) * yb,)
    mla = dict(heads=MLA_HEADS, dq=2 * HEAD_W, dv=HEAD_W, koff=0, kstride=1, voff=1, vstride=2,
               scale=MLA_QK ** -0.5, causal=True)

    def mix_fwd(x_in, plan):
        w_in = _w_in_ext(jnp.concatenate([full['w_in#0'], full['w_in#1']], axis=0))
        h = pre_norm("mix", x_in, a['mix_pre_g'])
        names, carry = fetch(plan, 'in')
        u = mm("mix_in", h, w_in, 'nn', carry=carry)
        gathered(names, carry)
        w_q, w_kv = _w_q_ext(full['mla_w_q_up']), full['mla_w_kv_up']
        hg_ins = [R_(u, 2048, 1), R_(u, 2048, 0), P_(a['hgrn_lb_logits'])]
        log_f, k_in, q_in = rowmap("hg_prep", _hg_prep, hg_ins, [(F32, 2048, 0)] * 3, tile=tp)
        names, carry = fetch(plan, 'scan')
        o_a, states = hg_scan_fwd("hg_scan", q_in, k_in, log_f, u, 32, carry=carry)
        gathered(names, carry)
        out_ins = [R_(o_a, HEAD_W, 0, 1), R_(u, HEAD_W, 48, 1), P_(a['hg_norm_g'], HEAD_W, 0, 1)]
        oag = rowmap("hg_out", hg_out_fn, out_ins, [(BF16, HEAD_W, 1)], tile=th, ncol=HG_HEADS)[0]
        y_a = mm("mix_a", oag, full['w_branch_a'], 'nn')
        norm_ins = [R_(u, 512, 16), R_(u, 512, 17), P_(a['mla_q_norm_g']), P_(a['mla_kv_norm_g'])]
        cqn, ckvn = rowmap("mla_norm", mla_norm_fn, norm_ins, [(BF16, 512, 0)] * 2, tile=tr)
        q_all = mm("mla_qup", cqn, w_q, 'nn')
        kv = mm("mla_kvup", ckvn, w_kv, 'nn')
        rope_ins = [R_(q_all, HEAD_W, 0, 1), R_(q_all, HEAD_W, 16, 1), R_(q_all, HEAD_W, 32, 1),
                    R_(kv, HEAD_W, 0, 2), R_(u, HEAD_W, 104, 0), R_(u, HEAD_W, 105, 0), R_(cos), R_(sin)]
        qf, kf = rowmap("mla_rope", _rope, rope_ins, [(BF16, 2 * HEAD_W, 1)] * 2, tile=th, ncol=MLA_HEADS,
                        rows_inner=False)
        names, carry = fetch(plan, 'attn')
        o_b = attn_fwd("mla_attn", qf, kf, kv, carry=carry, **mla)
        gathered(names, carry)
        y_b = mm("mix_b", o_b, full['w_branch_b'], 'nn')
        gate_ins = [R_(u, 1024, 9, 1), R_(u, 1024, 11, 1), R_(y_a, 1024, 0, 1), R_(y_b, 1024, 0, 1)]
        y = rowmap("mix_gate", gate_fn, gate_ins, [(BF16, 1024, 1)], tile=tr, ncol=2)[0]
        yo = mm("mix_out", y, full['w_out'], 'nn')
        res = (x_in, h, hg_ins, q_in, k_in, log_f, u, states, out_ins, oag, norm_ins, cqn, ckvn, rope_ins, qf, kf,
               kv, o_b, gate_ins, y, yo, w_in, w_q, w_kv)
        return post_norm("mix", x_in, yo, a['mix_post_g'], 1.0), res

    def mix_bwd(res, d_out, plan):
        (x_in, h, hg_ins, q_in, k_in, log_f, u, states, out_ins, oag, norm_ins, cqn, ckvn, rope_ins, qf, kf, kv,
         o_b, gate_ins, y, yo, w_in, w_q, w_kv) = res
        dyo, gs['mix_post_g'] = post_norm_bwd("mix", x_in, yo, a['mix_post_g'], 1.0, d_out)
        dy = mm("mix_out_dx", dyo, full['w_out'], 'nt')
        gb['w_out'] = by_rows(mm("mix_out_dw", y, dyo, 'tn', out_dtype=BF16))
        dga, dgb, dya, dyb = rowmap("mix_gate_bwd", gate_fn, gate_ins, tile=tr, ncol=2, cts=[R_(dy, 1024, 0, 1)],
                                    wrt=[0, 1, 2, 3], gdt=[BF16] * 4)
        gb['w_branch_b'] = by_rows(mm("mix_b_dw", o_b, dyb, 'tn', out_dtype=BF16))
        do_b = mm("mix_b_dx", dyb, full['w_branch_b'], 'nt')
        names, carry = send(plan, 'attn_bwd')
        dqf, dkf, dv = attn_bwd("mla_attn_bwd", qf, kf, kv, do_b, carry=carry, **mla)
        exchanged(names, carry)
        dqn, dqp, dqs, dkn, dkpe, dksw = rowmap(
            "mla_rope_bwd", _rope, rope_ins, tile=th, ncol=MLA_HEADS, rows_inner=False,
            cts=[R_(dqf, 2 * HEAD_W, 0, 1), R_(dkf, 2 * HEAD_W, 0, 1)], wrt=[0, 1, 2, 3, 4, 5],
            gdt=[BF16, BF16, BF16, BF16, F32, F32])
        wide = MLA_HEADS * HEAD_W
        dq_parts = [dqn, dqp, dqs]
        gb['mla_w_q_up'] = by_cols(_w_q_grad([mm("mla_qup_dw%d" % n, cqn, dq_parts[n], 'tn', out_dtype=BF16)
                                              for n in range(3)]))
        dcqn = mm("mla_qup_dx", dq_parts, [w_q[:, n * wide:(n + 1) * wide] for n in range(3)], 'nt')
        w_kv4 = w_kv.reshape(KV_LORA, MLA_HEADS, 2, HEAD_W)
        dw_k = mm("mla_kup_dw", ckvn, dkn, 'tn', out_dtype=BF16).reshape(KV_LORA, MLA_HEADS, 1, HEAD_W)
        dw_v = mm("mla_vup_dw", ckvn, dv, 'tn', out_dtype=BF16).reshape(KV_LORA, MLA_HEADS, 1, HEAD_W)
        gb['mla_w_kv_up'] = by_cols(jnp.concatenate([dw_k, dw_v], axis=2).reshape(KV_LORA, 2 * wide))
        dckvn = mm("mla_kvup_dx", [dkn, dv],
                   [w_kv4[:, :, 0].reshape(KV_LORA, wide), w_kv4[:, :, 1].reshape(KV_LORA, wide)], 'nt')
        dcq, dckv, gs['mla_q_norm_g'], gs['mla_kv_norm_g'] = rowmap(
            "mla_norm_bwd", mla_norm_fn, norm_ins, tile=tr, cts=[R_(dcqn), R_(dckvn)], wrt=[0, 1, 2, 3],
            gdt=[BF16, BF16, F32, F32])
        gb['w_branch_a'] = by_rows(mm("mix_a_dw", oag, dya, 'tn', out_dtype=BF16))
        doag = mm("mix_a_dx", dya, full['w_branch_a'], 'nt')
        acts, carry = send(plan, 'out_bwd')
        do_a, dog, gs['hg_norm_g'] = rowmap("hg_out_bwd", hg_out_fn, out_ins, tile=th, ncol=HG_HEADS,
                                            cts=[R_(doag, HEAD_W, 0, 1)], wrt=[0, 1, 2], gdt=[F32, BF16, F32],
                                            carry=carry)
        exchanged(acts, carry)
        names, carry = send(plan, 'scan_bwd')
        dq_in, dk_in, dlog_f, di = hg_scan_bwd("hg_scan_bwd", q_in, k_in, log_f, u, 32, states, do_a, carry=carry)
        exchanged(names, carry)
        df, dq_hg, gs['hgrn_lb_logits'] = rowmap("hg_prep_bwd", _hg_prep, hg_ins, tile=tp,
                                                 cts=[R_(dlog_f), R_(dk_in), R_(dq_in)], wrt=[0, 1, 2],
                                                 gdt=[BF16, BF16, F32])
        du = jnp.concatenate([dq_hg, df, di.astype(BF16), dog, dcq, dckv, dga, dgb, dkpe.astype(BF16),
                              dksw.astype(BF16), jnp.zeros((s, U_PAD), BF16)], axis=1)
        g_in = by_cols(_w_in_grad(mm("mix_in_dw", h, du, 'tn', out_dtype=BF16)))
        gb['w_in#0'], gb['w_in#1'] = g_in[:, :half_in], g_in[:, half_in:]
        names, carry = send(plan, 'in_dx')
        dh = mm("mix_in_dx", du, w_in, 'nt', carry=carry)
        exchanged(names, carry)
        d_in, gs['mix_pre_g'] = pre_norm_bwd("mix", x_in, a['mix_pre_g'], dh, d_out)
        return d_in

    xa = dict(heads=XA_HEADS, dq=HEAD_W, dv=HEAD_W, koff=0, kstride=1, voff=XA_HEADS, vstride=1,
              scale=HEAD_W ** -0.5, causal=False)
    tm_ = min(nm, 128)

    def xa_fwd(x_in):
        w_xkv = jnp.concatenate([full['xa_w_k'], full['xa_w_v']], axis=1)
        h = pre_norm("xa", x_in, a['xa_pre_g'])
        mn = rowmap("xa_mem", pre_fn, [R_(mem), P_(a['xa_mem_g'])], [(BF16, d, 0)], tile=tm_)[0]
        q = mm("xa_q", h, full['xa_w_q'], 'nn')
        kv = mm("xa_kv", mn, w_xkv, 'nn')
        o = attn_fwd("xa_attn", q, kv, kv, **xa)
        yo = mm("xa_o", o, full['xa_w_o'], 'nn')
        return post_norm("xa", x_in, yo, a['xa_post_g'], 1.0), (x_in, h, mn, q, kv, o, yo, w_xkv)

    def xa_bwd(res, d_out, plan):
        x_in, h, mn, q, kv, o, yo, w_xkv = res
        dyo, gs['xa_post_g'] = post_norm_bwd("xa", x_in, yo, a['xa_post_g'], 1.0, d_out)
        do = mm("xa_o_dx", dyo, full['xa_w_o'], 'nt')
        gb['xa_w_o'] = by_cols(mm("xa_o_dw", o, dyo, 'tn', out_dtype=BF16))
        dq, dk, dv = attn_bwd("xa_attn_bwd", q, kv, kv, do, **xa)
        dkv = jnp.concatenate([dk, dv], axis=1).astype(BF16)
        dw = mm("xa_kv_dw", mn, dkv, 'tn', out_dtype=BF16)
        gb['xa_w_k'], gb['xa_w_v'] = by_rows(dw[:, :XA_HEADS * HEAD_W]), by_rows(dw[:, XA_HEADS * HEAD_W:])
        dmn = mm("xa_kv_dx", dkv, w_xkv, 'nt')
        gs['xa_mem_g'] = rowmap("xa_mem_bwd", pre_fn, [R_(mem), P_(a['xa_mem_g'])], tile=tm_, cts=[R_(dmn)],
                                wrt=[1], gdt=[F32])[0]
        gb['xa_w_q'] = by_rows(mm("xa_q_dw", h, dq, 'tn', out_dtype=BF16))
        dh = mm("xa_q_dx", dq, full['xa_w_q'], 'nt')
        acts, carry = send(plan, 'pre_bwd')
        d_in, gs['xa_pre_g'] = pre_norm_bwd("xa", x_in, a['xa_pre_g'], dh, d_out, carry)
        exchanged(acts, carry)
        return d_in

    landed = {}

    halves = {}
    chip_sums = {}

    def exchanged(acts, carry):
        for (kind, names), part in zip(acts or [], carry.parts if carry else []):
            for n, got in zip(names, part.results):
                if kind == 'halve':
                    halves[n] = got
                else:
                    own = lax.dynamic_index_in_dim(chip_sums[n], my_chip, 0, keepdims=True)
                    landed[n] = lax.dynamic_update_slice_in_dim(got, own, my_chip, 0)

    x1, r1 = ffn_fwd('ffn1', x, {'gate': ['ffn1_w_up'], 'up': ['ffn1_w_down'], 'act': ['w_in#0'], 'dn': ['w_in#1']})
    x2, r2 = mix_fwd(x1, {'in': ['mla_w_q_up', 'mla_w_kv_up', 'w_branch_a', 'w_branch_b', 'w_out'],
                          'scan': xa_names + ['ffn2_w_gate'], 'attn': ['ffn2_w_up']})
    x3, r3 = xa_fwd(x2)
    x4, r4 = ffn_fwd('ffn2', x3, {'gate': ['ffn2_w_down']})

    def loss_fn(y, t):
        diff = y - t
        return diff * (1.0 / d), jnp.mean(diff * diff, axis=-1, keepdims=True)

    d4, row_loss = rowmap("loss", loss_fn, [R_(x4), R_(target)], [(F32, d, 0), (F32, 1, 0)], tile=tr)
    loss = lax.psum(0.5 * jnp.sum(row_loss), ("x", "y", "c"))
    late = ['mla_w_q_up', 'mla_w_kv_up', 'w_branch_a']
    d3 = ffn_bwd('ffn2', r4, d4, {'act_bwd': [('halve', ['ffn2_w_down'])], 'gu_dx': [('spread', ['ffn2_w_down'])],
                                  'up_dw': [('halve', ['ffn2_w_gate'])], 'pre_bwd': [('halve', ['ffn2_w_up'])]})
    d2 = xa_bwd(r3, d3, {'pre_bwd': [('halve', xa_names)]})
    d1 = mix_bwd(r2, d2, {'attn_bwd': [('spread', ['ffn2_w_gate', 'ffn2_w_up']), ('halve', ['w_out', 'w_branch_b'])],
                          'out_bwd': [('halve', late)],
                          'scan_bwd': [('spread', xa_names + ['w_out', 'w_branch_b'] + late)],
                          'in_dx': [('halve', ['w_in#0', 'w_in#1'])]})
    grad_x = ffn_bwd('ffn1', r1, d1, {'dn_dx': [('spread', ['w_in#0'])], 'dn_dw': [('spread', ['w_in#1'])],
                                      'act_bwd': [('halve', ['ffn1_w_down'])],
                                      'gu_dx': [('spread', ['ffn1_w_down'])], 'up_dw': [('halve', ['ffn1_w_gate'])],
                                      'pre_bwd': [('spread', ['ffn1_w_gate']), ('halve', ['ffn1_w_up'])]})
    acts, carry = send({'end': [('spread', ['ffn1_w_up'])]}, 'end')
    run_alone("spread_last", carry)
    exchanged(acts, carry)

    def pack_small(vals):
        flat = jnp.concatenate([vals[n].reshape(-1) for n in SMALL])
        rows = -(-flat.shape[0] // PACK_W)
        rows = -(-rows // 8) * 8
        return jnp.pad(flat, (0, rows * PACK_W - flat.shape[0])).reshape(rows, PACK_W)

    def unpack_small(buf):
        flat, out, at = buf.reshape(-1), {}, 0
        for n in SMALL:
            size = a[n].shape[0] * a[n].shape[1]
            out[n] = flat[at:at + size].reshape(a[n].shape)
            at += size
        return out

    g_small = pack_small(gs)
    small = Gather([g_small])
    run_alone("gather_g_small", small)
    g_small = lax.dynamic_update_index_in_dim(small.results[0], g_small, me, 0)

    grads, delta, new_m, new_v = {}, {}, {}, {}
    packs = adamw("adamw_small", g_small, pack_small(a), pack_small({n: a['m_' + n] for n in SMALL}),
                  pack_small({n: a['v_' + n] for n in SMALL}))
    for dst, buf in zip((grads, delta, new_m, new_v), packs):
        dst.update(unpack_small(buf))
    landed['w_in'] = jnp.concatenate([landed['w_in#0'], landed['w_in#1']], axis=1)
    for n in BIG:
        outs = adamw("adamw_" + n, landed[n], a[n][0], a['m_' + n][0], a['v_' + n][0])
        grads[n], delta[n], new_m[n], new_v[n] = (t.reshape(a[n].shape) for t in outs)

    return (loss, grad_x[None], *[grads[n] for n in WEIGHTS], *[delta[n] for n in WEIGHTS],
            *[new_m[n] for n in WEIGHTS], *[new_v[n] for n in WEIGHTS])


def kernel(x, mem, positions, hgrn_lb_logits, ffn1_pre_g, ffn1_w_gate, ffn1_w_up, ffn1_w_down, ffn1_post_g, mix_pre_g, w_in, hg_norm_g, mla_q_norm_g, mla_w_q_up, mla_kv_norm_g, mla_w_kv_up, w_branch_a, w_branch_b, w_out, mix_post_g, xa_pre_g, xa_mem_g, xa_w_q, xa_w_k, xa_w_v, xa_w_o, xa_post_g, ffn2_pre_g, ffn2_w_gate, ffn2_w_up, ffn2_w_down, ffn2_post_g, loss_target, m_hgrn_lb_logits, m_ffn1_pre_g, m_ffn1_w_gate, m_ffn1_w_up, m_ffn1_w_down, m_ffn1_post_g, m_mix_pre_g, m_w_in, m_hg_norm_g, m_mla_q_norm_g, m_mla_w_q_up, m_mla_kv_norm_g, m_mla_w_kv_up, m_w_branch_a, m_w_branch_b, m_w_out, m_mix_post_g, m_xa_pre_g, m_xa_mem_g, m_xa_w_q, m_xa_w_k, m_xa_w_v, m_xa_w_o, m_xa_post_g, m_ffn2_pre_g, m_ffn2_w_gate, m_ffn2_w_up, m_ffn2_w_down, m_ffn2_post_g, v_hgrn_lb_logits, v_ffn1_pre_g, v_ffn1_w_gate, v_ffn1_w_up, v_ffn1_w_down, v_ffn1_post_g, v_mix_pre_g, v_w_in, v_hg_norm_g, v_mla_q_norm_g, v_mla_w_q_up, v_mla_kv_norm_g, v_mla_w_kv_up, v_w_branch_a, v_w_branch_b, v_w_out, v_mix_post_g, v_xa_pre_g, v_xa_mem_g, v_xa_w_q, v_xa_w_k, v_xa_w_v, v_xa_w_o, v_xa_post_g, v_ffn2_pre_g, v_ffn2_w_gate, v_ffn2_w_up, v_ffn2_w_down, v_ffn2_post_g):
    return _step(dict(locals()))
```

```python
import functools

import jax
import jax.numpy as jnp
from jax import lax
from jax.experimental import pallas as pl
from jax.experimental.pallas import tpu as pltpu

F32 = jnp.float32
BF16 = jnp.bfloat16

N_DEV = 8
D_MODEL = 2048
CHUNK = 64
CHUNK_SHIFT = 6
SUB = 16
HG_HEADS = 16
HG_GROUP = 16
HEAD_W = 128
MLA_HEADS = 16
Q_LORA = 512
KV_LORA = 512
QK_ROPE = 64
MLA_QK = 192
XA_HEADS = 4
ROPE_THETA = 10000.0
EPS = 1e-6
PACK_W = 1024
VMEM_LIMIT = 56 * 1024 * 1024

ADAM_LR = 0.001
ADAM_B1 = 0.9
ADAM_B2 = 0.999
ADAM_EPS = 1e-08
ADAM_WD = 0.01
ADAM_STEP = 10

U_PAD = 256

WEIGHTS = ['hgrn_lb_logits', 'ffn1_pre_g', 'ffn1_w_gate', 'ffn1_w_up', 'ffn1_w_down', 'ffn1_post_g', 'mix_pre_g',
           'w_in', 'hg_norm_g', 'mla_q_norm_g', 'mla_w_q_up', 'mla_kv_norm_g', 'mla_w_kv_up', 'w_branch_a',
           'w_branch_b', 'w_out', 'mix_post_g', 'xa_pre_g', 'xa_mem_g', 'xa_w_q', 'xa_w_k', 'xa_w_v', 'xa_w_o',
           'xa_post_g', 'ffn2_pre_g', 'ffn2_w_gate', 'ffn2_w_up', 'ffn2_w_down', 'ffn2_post_g']
BIG = {'ffn1_w_gate': True, 'ffn1_w_up': True, 'ffn1_w_down': False, 'w_in': True, 'mla_w_q_up': True,
       'mla_w_kv_up': True, 'w_branch_a': False, 'w_branch_b': False, 'w_out': False, 'xa_w_q': False,
       'xa_w_k': False, 'xa_w_v': False, 'xa_w_o': True, 'ffn2_w_gate': True, 'ffn2_w_up': True,
       'ffn2_w_down': False}
SMALL = [n for n in WEIGHTS if n not in BIG]


def _cparams(**kw):
    return pltpu.CompilerParams(vmem_limit_bytes=VMEM_LIMIT, **kw)


def _pick(dim, cands):
    for c in cands:
        if dim % c == 0:
            return c
    return dim


def _place():
    return lax.axis_index("x"), lax.axis_index("y"), lax.axis_index("c")


def _slot(px, py, pc):
    return 4 * px + 2 * py + pc


class Gather:
    def __init__(self, tensors):
        self.operands = list(tensors)
        self.out_shape = [jax.ShapeDtypeStruct((N_DEV,) + t.shape, t.dtype) for t in tensors]
        n = len(tensors)
        self.scratch = [pltpu.SemaphoreType.DMA((n, N_DEV - 1)), pltpu.SemaphoreType.DMA((n, N_DEV - 1))]

    def _copies(self, t, x_ref, out_ref, scr):
        send, recv = scr
        x, y, c = _place()
        me, sibling = (x, y, c), (x, y, 1 - c)
        x_nbr, y_nbr, diag = (1 - x, y, c), (x, 1 - y, c), (1 - x, 1 - y, c)
        relay_from, relay_to = (x ^ (1 - c), y ^ c, c), (x ^ c, y ^ (1 - c), c)

        def copy(k, block, to, src=None):
            rows = out_ref.at[_slot(*block)]
            return pltpu.make_async_remote_copy(src_ref=rows if src is None else src, dst_ref=rows,
                                                send_sem=send.at[t, k], recv_sem=recv.at[t, k], device_id=to,
                                                device_id_type=pl.DeviceIdType.MESH)

        def other(block):
            return block[0], block[1], 1 - c

        sent = [(me, sibling, x_ref), (me, x_nbr, x_ref), (me, y_nbr, x_ref), (relay_from, relay_to, None),
                (x_nbr, sibling, None), (y_nbr, sibling, None), (diag, sibling, None)]
        landing = [sibling, x_nbr, y_nbr, diag, other(x_nbr), other(y_nbr), other(diag)]
        return (lambda k: copy(k, *sent[k])), (lambda k: copy(k, landing[k], me))

    def start(self, ins, outs, scr):
        for t, (x_ref, out_ref) in enumerate(zip(ins, outs)):
            give, _ = self._copies(t, x_ref, out_ref, scr)
            for k in range(3):
                give(k).start()

    def middle(self, ins, outs, scr):
        for t, (x_ref, out_ref) in enumerate(zip(ins, outs)):
            give, take = self._copies(t, x_ref, out_ref, scr)
            take(1).wait_recv()
            take(2).wait_recv()
            for k in (3, 4, 5):
                give(k).start()

    def finish(self, ins, outs, scr):
        for t, (x_ref, out_ref) in enumerate(zip(ins, outs)):
            give, take = self._copies(t, x_ref, out_ref, scr)
            take(3).wait_recv()
            give(6).start()
            for k in (0, 4, 5, 6):
                take(k).wait_recv()
            for k in range(N_DEV - 1):
                give(k).wait_send()

    def set_results(self, res):
        self.results = list(res)


N_CHIP = N_DEV // 2


class Halve:
    def __init__(self, tensors):
        self.operands = list(tensors)
        self.out_shape = [jax.ShapeDtypeStruct((N_CHIP,) + t.shape[1:], t.dtype) for t in tensors]
        n = len(tensors)
        self.scratch = [pltpu.SemaphoreType.DMA((n, N_CHIP)), pltpu.SemaphoreType.DMA((n, N_CHIP))]

    def _copies(self, t, x_ref, theirs_ref, scr):
        send, recv = scr
        x, y, c = _place()
        return [pltpu.make_async_remote_copy(
            src_ref=x_ref.at[2 * q + 1 - c], dst_ref=theirs_ref.at[q], send_sem=send.at[t, q],
            recv_sem=recv.at[t, q], device_id=(x, y, 1 - c), device_id_type=pl.DeviceIdType.MESH)
            for q in range(N_CHIP)]

    def start(self, ins, outs, scr):
        for t, (x_ref, theirs_ref) in enumerate(zip(ins, outs)):
            for give in self._copies(t, x_ref, theirs_ref, scr):
                give.start()

    def middle(self, ins, outs, scr):
        pass

    def finish(self, ins, outs, scr):
        for t, (x_ref, theirs_ref) in enumerate(zip(ins, outs)):
            for give in self._copies(t, x_ref, theirs_ref, scr):
                give.wait_recv()
                give.wait_send()

    def set_results(self, res):
        self.results = list(res)


class Spread:
    def __init__(self, tensors):
        self.operands = list(tensors)
        self.out_shape = [jax.ShapeDtypeStruct(t.shape, t.dtype) for t in tensors]
        n = len(tensors)
        self.scratch = [pltpu.SemaphoreType.DMA((n, N_CHIP - 1)), pltpu.SemaphoreType.DMA((n, N_CHIP - 1))]

    def _copies(self, t, y_ref, out_ref, scr, outgoing):
        send, recv = scr
        x, y, c = _place()
        copies = []
        for k in range(1, N_CHIP):
            px, py = x ^ (k >> 1), y ^ (k & 1)
            copies.append(pltpu.make_async_remote_copy(
                src_ref=y_ref.at[2 * px + py], dst_ref=out_ref.at[2 * x + y if outgoing else 2 * px + py],
                send_sem=send.at[t, k - 1], recv_sem=recv.at[t, k - 1], device_id=(px, py, c),
                device_id_type=pl.DeviceIdType.MESH))
        return copies

    def start(self, ins, outs, scr):
        for t, (y_ref, out_ref) in enumerate(zip(ins, outs)):
            for give in self._copies(t, y_ref, out_ref, scr, True):
                give.start()

    def middle(self, ins, outs, scr):
        pass

    def finish(self, ins, outs, scr):
        for t, (y_ref, out_ref) in enumerate(zip(ins, outs)):
            for take in self._copies(t, y_ref, out_ref, scr, False):
                take.wait_recv()
            for give in self._copies(t, y_ref, out_ref, scr, True):
                give.wait_send()

    def set_results(self, res):
        self.results = list(res)


class Joint:
    def __init__(self, parts):
        self.parts = list(parts)
        self.operands = [o for p in parts for o in p.operands]
        self.out_shape = [o for p in parts for o in p.out_shape]
        self.scratch = [o for p in parts for o in p.scratch]

    def _split(self, ins, outs, scr):
        i = o = s = 0
        for p in self.parts:
            ni, no, ns = len(p.operands), len(p.out_shape), len(p.scratch)
            yield p, ins[i:i + ni], outs[o:o + no], scr[s:s + ns]
            i, o, s = i + ni, o + no, s + ns

    def start(self, ins, outs, scr):
        for p, a, b, c in self._split(ins, outs, scr):
            p.start(a, b, c)

    def middle(self, ins, outs, scr):
        for p, a, b, c in self._split(ins, outs, scr):
            p.middle(a, b, c)

    def finish(self, ins, outs, scr):
        for p, a, b, c in self._split(ins, outs, scr):
            p.finish(a, b, c)

    def set_results(self, res):
        for p, _, part, _ in self._split([], list(res), []):
            p.set_results(part)


_HBM = pl.BlockSpec(memory_space=pltpu.HBM)


def run_alone(name, carry):
    n_in, n_out = len(carry.operands), len(carry.out_shape)

    def body(*refs):
        ins, outs, scr = refs[:n_in], refs[n_in:n_in + n_out], refs[n_in + n_out:]
        carry.start(ins, outs, scr)
        carry.middle(ins, outs, scr)
        carry.finish(ins, outs, scr)

    res = pl.pallas_call(body, name=name, in_specs=[_HBM] * n_in, out_specs=[_HBM] * n_out,
                         out_shape=carry.out_shape, scratch_shapes=carry.scratch)(*carry.operands)
    carry.set_results(list(res))


def _call(name, body, *, grid, in_specs, out_specs, out_shape, args, scratch=(), carry=None):
    in_specs, out_specs, out_shape, scratch = list(in_specs), list(out_specs), list(out_shape), list(scratch)
    if carry is None:
        return list(pl.pallas_call(body, name=name, grid=grid, in_specs=in_specs, out_specs=out_specs,
                                   out_shape=out_shape, scratch_shapes=scratch, compiler_params=_cparams())(*args))
    n_in, n_out, n_scr = len(in_specs), len(out_shape), len(scratch)
    c_in, c_out = len(carry.operands), len(carry.out_shape)

    def wrapped(*refs):
        ins, c_ins = refs[:n_in], refs[n_in:n_in + c_in]
        at = n_in + c_in
        outs, c_outs = refs[at:at + n_out], refs[at + n_out:at + n_out + c_out]
        at += n_out + c_out
        scr, c_scr = refs[at:at + n_scr], refs[at + n_scr:]
        first = functools.reduce(jnp.logical_and, [pl.program_id(d) == 0 for d in range(len(grid))])
        last = functools.reduce(jnp.logical_and, [pl.program_id(d) == grid[d] - 1 for d in range(len(grid))])

        @pl.when(first)
        def _():
            carry.start(c_ins, c_outs, c_scr)

        body(*ins, *outs, *scr)

        step, total = 0, 1
        for d in range(len(grid)):
            step, total = step * grid[d] + pl.program_id(d), total * grid[d]

        @pl.when(step == total // 2)
        def _():
            carry.middle(c_ins, c_outs, c_scr)

        @pl.when(last)
        def _():
            carry.finish(c_ins, c_outs, c_scr)

    res = pl.pallas_call(
        wrapped, name=name, grid=grid, in_specs=in_specs + [_HBM] * c_in, out_specs=out_specs + [_HBM] * c_out,
        out_shape=out_shape + carry.out_shape, scratch_shapes=scratch + carry.scratch, compiler_params=_cparams(),
    )(*args, *carry.operands)
    carry.set_results(list(res[n_out:]))
    return list(res[:n_out])


def R_(arr, w=None, off=0, stride=0):
    return ('r', arr, arr.shape[1] if w is None else w, off, stride)


def P_(arr, w=None, off=0, stride=0):
    return ('p', arr, arr.shape[1] if w is None else w, off, stride)


def rowmap(name, fn, ins, outs=None, *, tile, ncol=1, rows_inner=True, cts=None, wrt=None, gdt=None, cat=False,
           carry=None):
    rows = next(a.shape[0] for k, a, *_ in ins if k == 'r')
    nrow = rows // tile
    assert nrow * tile == rows
    grid = (ncol, nrow) if rows_inner else (nrow, ncol)

    def ij(g0, g1):
        return (g1, g0) if rows_inner else (g0, g1)

    def spec(kind, arr, w, off, stride):
        if kind == 'r':
            return pl.BlockSpec((tile, w), lambda g0, g1: (ij(g0, g1)[0], off + stride * ij(g0, g1)[1]))
        return pl.BlockSpec((arr.shape[0], w), lambda g0, g1: (0, off + stride * ij(g0, g1)[1]))

    ops = list(ins) + list(cts or [])
    in_specs = [spec(*o) for o in ops]
    n_in = len(ins)
    fwd = cts is None
    out_shape, out_specs, acc = [], [], []
    if fwd:
        for dt, w, stride in outs:
            out_shape.append(jax.ShapeDtypeStruct((rows, w * (ncol if stride else 1)), dt))
            out_specs.append(spec('r', None, w, 0, stride))
            acc.append(None)
    elif cat:
        widths = [ins[i][2] for i in wrt]
        assert ncol == 1 and all(ins[i][0] == 'r' for i in wrt)
        out_shape.append(jax.ShapeDtypeStruct((rows, sum(widths)), gdt))
        out_specs.append(spec('r', None, sum(widths), 0, 0))
    else:
        for n, i in enumerate(wrt):
            kind, arr, w, off, stride = ins[i]
            width = w * (ncol if stride else 1)
            if kind == 'r':
                out_shape.append(jax.ShapeDtypeStruct((rows, width), gdt[n]))
                out_specs.append(spec('r', None, w, 0, 1 if stride else 0))
                shared = stride == 0 and ncol > 1
                assert not shared or (not rows_inner and gdt[n] == F32)
                acc.append('col' if shared else None)
            else:
                out_shape.append(jax.ShapeDtypeStruct((arr.shape[0], width), F32))
                out_specs.append(spec('p', arr, w, 0, 1 if stride else 0))
                assert rows_inner or ncol == 1
                acc.append('row')

    def body(*refs):
        i, j = ij(pl.program_id(0), pl.program_id(1))
        vals = [r[...].astype(F32) for r in refs[:n_in]]
        out_refs = refs[len(ops):]
        if fwd:
            for o_ref, o in zip(out_refs, fn(*vals)):
                o_ref[...] = o.astype(o_ref.dtype)
            return

        def f(*d):
            full = list(vals)
            for n, idx in enumerate(wrt):
                full[idx] = d[n]
            return fn(*full)

        _, vjp = jax.vjp(f, *[vals[idx] for idx in wrt])
        grads = vjp(tuple(r[...].astype(F32) for r in refs[n_in:len(ops)]))
        if cat:
            o_ref, at = out_refs[0], 0
            for g in grads:
                o_ref[:, at:at + g.shape[1]] = g.astype(o_ref.dtype)
                at += g.shape[1]
            return
        for o_ref, g, a in zip(out_refs, grads, acc):
            if a is None:
                o_ref[...] = g.astype(o_ref.dtype)
            else:
                first = (i if a == 'row' else j) == 0

                @pl.when(first)
                def _(o_ref=o_ref):
                    o_ref[...] = jnp.zeros_like(o_ref)

                o_ref[...] += g

    return _call(name, body, grid=grid, in_specs=in_specs, out_specs=out_specs, out_shape=out_shape,
                 args=[o[1] for o in ops], carry=carry)


def _rms(x, g):
    return x * lax.rsqrt(jnp.mean(x * x, axis=-1, keepdims=True) + EPS) * g


_DIMS = {'nn': (((1,), (0,)), ((), ())), 'nt': (((1,), (1,)), ((), ())), 'tn': (((0,), (0,)), ((), ()))}


def mm(name, a, b, mode, *, jdim=None, out_dtype=F32, carry=None):
    a_list = list(a) if isinstance(a, (list, tuple)) else [a]
    b_list = list(b) if isinstance(b, (list, tuple)) else [b]
    a_order = ('k', 'm') if mode == 'tn' else ('m', 'k')
    b_order = ('n', 'k') if mode == 'nt' else ('k', 'n')
    size, blocks = {}, 1
    for arr, order in ((a_list[0], a_order), (b_list[0], b_order)):
        shape = arr.shape
        if jdim in order:
            blocks, shape = shape[0], shape[1:]
        for dname, extent in zip(order, shape):
            assert size.setdefault(dname, extent) == extent
    tile = {'m': _pick(size['m'], (1024, 512, 256, 128)), 'n': _pick(size['n'], (512, 256, 128)),
            'k': size['k'] if size['k'] <= 2048 else _pick(size['k'], (2304, 2048, 1024, 512, 256, 128))}
    if jdim is not None:
        tile[jdim] = size[jdim]
    if a_list[0].dtype == F32 and tile['k'] * tile['m'] > (1 << 20):
        tile['m'] = _pick(size['m'], (512, 256, 128))
    grid = tuple(blocks if d == jdim else size[d] // tile[d] for d in ('m', 'n', 'k'))
    nk = grid[2]

    def spec(order):
        shape = tuple(tile[d] for d in order)

        def imap(i, j, k):
            g = {'m': i, 'n': j, 'k': k}
            idx = tuple(0 if d == jdim else g[d] for d in order)
            return ((g[jdim],) + idx) if jdim in order else idx

        return pl.BlockSpec(((None,) + shape) if jdim in order else shape, imap)

    dims = _DIMS[mode]
    nt = len(a_list)

    def product(refs):
        acc = None
        for a_ref, b_ref in zip(refs[:nt], refs[nt:2 * nt]):
            p = lax.dot_general(a_ref[...].astype(BF16), b_ref[...].astype(BF16), dims, preferred_element_type=F32)
            acc = p if acc is None else acc + p
        return acc

    def body_once(*refs):
        refs[2 * nt][...] = product(refs).astype(refs[2 * nt].dtype)

    def body_acc(*refs):
        o_ref, acc_ref = refs[2 * nt], refs[2 * nt + 1]
        k = pl.program_id(2)

        @pl.when(k == 0)
        def _():
            acc_ref[...] = jnp.zeros_like(acc_ref)

        acc_ref[...] += product(refs)

        @pl.when(k == nk - 1)
        def _():
            o_ref[...] = acc_ref[...].astype(o_ref.dtype)

    out_dims = (size['m'], size['n'])
    out_shape = jax.ShapeDtypeStruct(((blocks,) + out_dims) if jdim in ('m', 'n') else out_dims, out_dtype)
    return _call(name, body_once if nk == 1 else body_acc, grid=grid,
                 in_specs=[spec(a_order)] * nt + [spec(b_order)] * nt, out_specs=[spec(('m', 'n'))],
                 out_shape=[out_shape], args=a_list + b_list,
                 scratch=[] if nk == 1 else [pltpu.VMEM((tile['m'], tile['n']), F32)], carry=carry)[0]


def _probs(q, k, i, tq, scale, causal):
    s = lax.dot_general(q, k, _DIMS['nt'], preferred_element_type=F32) * scale
    if causal:
        shape = s.shape
        q_chunk = jnp.right_shift(i * tq + lax.broadcasted_iota(jnp.int32, shape, 0), CHUNK_SHIFT)
        k_chunk = jnp.right_shift(lax.broadcasted_iota(jnp.int32, shape, 1), CHUNK_SHIFT)
        s = jnp.where(k_chunk <= q_chunk, s, -jnp.inf)
    e = jnp.exp(s - jnp.max(s, axis=-1, keepdims=True))
    return e, jnp.sum(e, axis=-1, keepdims=True)


def _per_prefix(work, i, tq, sq, sk, causal):
    if not causal:
        work(sk)
        return
    assert sq == sk and tq % CHUNK == 0
    for j in range(sq // tq):
        @pl.when(i == j)
        def _(j=j):
            work((j + 1) * tq)


def _attn_specs(tq, sk, dq, dv, koff, kstride, voff, vstride):
    return [pl.BlockSpec((tq, dq), lambda h, i: (i, h)),
            pl.BlockSpec((sk, dq), lambda h, i: (0, koff + kstride * h)),
            pl.BlockSpec((sk, dv), lambda h, i: (0, voff + vstride * h))]


def attn_fwd(name, q, k, v, *, heads, dq, dv, koff, kstride, voff, vstride, scale, causal, carry=None):
    sq, sk = q.shape[0], k.shape[0]
    tq = min(sq, 256)

    def body(q_ref, k_ref, v_ref, o_ref):
        i = pl.program_id(1)

        def work(keys):
            e, l = _probs(q_ref[...].astype(BF16), k_ref[0:keys, :].astype(BF16), i, tq, scale, causal)
            o = jnp.dot(e.astype(BF16), v_ref[0:keys, :].astype(BF16), preferred_element_type=F32)
            o_ref[...] = o / l

        _per_prefix(work, i, tq, sq, sk, causal)

    return _call(name, body, grid=(heads, sq // tq),
                 in_specs=_attn_specs(tq, sk, dq, dv, koff, kstride, voff, vstride),
                 out_specs=[pl.BlockSpec((tq, dv), lambda h, i: (i, h))],
                 out_shape=[jax.ShapeDtypeStruct((sq, heads * dv), F32)], args=[q, k, v], carry=carry)[0]


def attn_bwd(name, q, k, v, do, *, heads, dq, dv, koff, kstride, voff, vstride, scale, causal, carry=None):
    sq, sk = q.shape[0], k.shape[0]
    tq = min(sq, 256)

    def body(q_ref, k_ref, v_ref, do_ref, dq_ref, dk_ref, dv_ref):
        i = pl.program_id(1)

        @pl.when(i == 0)
        def _():
            dk_ref[...] = jnp.zeros_like(dk_ref)
            dv_ref[...] = jnp.zeros_like(dv_ref)

        def work(keys):
            qb, kb, vb = q_ref[...].astype(BF16), k_ref[0:keys, :].astype(BF16), v_ref[0:keys, :].astype(BF16)
            dob = do_ref[...].astype(BF16)
            e, l = _probs(qb, kb, i, tq, scale, causal)
            p = e / l
            dp = lax.dot_general(dob, vb, _DIMS['nt'], preferred_element_type=F32)
            ds = (p * (dp - jnp.sum(dp * p, axis=-1, keepdims=True)) * scale).astype(BF16)
            dv_ref[0:keys, :] += lax.dot_general(p.astype(BF16), dob, _DIMS['tn'], preferred_element_type=F32)
            dk_ref[0:keys, :] += lax.dot_general(ds, qb, _DIMS['tn'], preferred_element_type=F32)
            dq_ref[...] = jnp.dot(ds, kb, preferred_element_type=F32)

        _per_prefix(work, i, tq, sq, sk, causal)

    return _call(
        name, body, grid=(heads, sq // tq),
        in_specs=_attn_specs(tq, sk, dq, dv, koff, kstride, voff, vstride) + [pl.BlockSpec((tq, dv), lambda h, i: (i, h))],
        out_specs=[pl.BlockSpec((tq, dq), lambda h, i: (i, h)), pl.BlockSpec((sk, dq), lambda h, i: (0, h)),
                   pl.BlockSpec((sk, dv), lambda h, i: (0, h))],
        out_shape=[jax.ShapeDtypeStruct((sq, heads * dq), F32), jax.ShapeDtypeStruct((sk, heads * dq), F32),
                   jax.ShapeDtypeStruct((sk, heads * dv), F32)],
        args=[q, k, v, do], carry=carry)


def _hg_chunk(q, k, g, v, state):
    c = q.shape[0]
    row = lax.broadcasted_iota(jnp.int32, (c, c), 0)
    col = lax.broadcasted_iota(jnp.int32, (c, c), 1)
    tril = (col <= row).astype(F32)
    b = jnp.dot(tril, g, precision=lax.Precision.HIGHEST, preferred_element_type=F32)
    rows = lax.broadcasted_iota(jnp.int32, (c, 1), 0)
    o = jnp.dot((q * jnp.exp(b)).astype(BF16), state.astype(BF16), preferred_element_type=F32)
    t3 = lax.broadcasted_iota(jnp.int32, (SUB, SUB, 1), 0)
    s3 = lax.broadcasted_iota(jnp.int32, (SUB, SUB, 1), 1)
    parts = []
    for n in range(c // SUB):
        lo = n * SUB
        qn, kn, bn, vn = q[lo:lo + SUB], k[lo:lo + SUB], b[lo:lo + SUB], v[lo:lo + SUB]
        decay = jnp.exp(jnp.where(s3 <= t3, bn[:, None, :] - bn[None, :, :], -jnp.inf))
        sc = jnp.sum(qn[:, None, :] * kn[None, :, :] * decay, axis=-1)
        on = jnp.dot(sc.astype(BF16), vn.astype(BF16), preferred_element_type=F32)
        if n > 0:
            ref = jnp.sum(jnp.where(rows == lo - 1, b, 0.0), axis=0, keepdims=True)
            qd = (qn * jnp.exp(bn - ref)).astype(BF16)
            kd = (k[:lo] * jnp.exp(ref - b[:lo])).astype(BF16)
            so = lax.dot_general(qd, kd, _DIMS['nt'], preferred_element_type=F32)
            on = on + jnp.dot(so.astype(BF16), v[:lo].astype(BF16), preferred_element_type=F32)
        parts.append(on)
    o = o + jnp.concatenate(parts, axis=0)
    b_last = jnp.sum(g, axis=0, keepdims=True)
    ones = jnp.ones((c, 1), F32)
    b_last_col = lax.dot_general(g, ones, _DIMS['tn'], precision=lax.Precision.HIGHEST, preferred_element_type=F32)
    kd = (k * jnp.exp(b_last - b)).astype(BF16)
    new_state = jnp.exp(b_last_col) * state + lax.dot_general(kd, v.astype(BF16), _DIMS['tn'],
                                                              preferred_element_type=F32)
    return o, new_state


def _hg_chunk_bwd(q, k, g, v, state, do, dnew):
    c, kw = q.shape
    hi = lax.Precision.HIGHEST
    row = lax.broadcasted_iota(jnp.int32, (c, c), 0)
    col = lax.broadcasted_iota(jnp.int32, (c, c), 1)
    b = jnp.dot((col <= row).astype(F32), g, precision=hi, preferred_element_type=F32)
    rows = lax.broadcasted_iota(jnp.int32, (c, 1), 0)
    b_last = jnp.sum(g, axis=0, keepdims=True)
    b_last_col = lax.dot_general(g, jnp.ones((c, 1), F32), _DIMS['tn'], precision=hi, preferred_element_type=F32)
    eb, to_end = jnp.exp(b), jnp.exp(b_last - b)
    dob, vb, dnb = do.astype(BF16), v.astype(BF16), dnew.astype(BF16)
    k_end = (k * to_end).astype(BF16)
    dq = eb * lax.dot_general(dob, state.astype(BF16), _DIMS['nt'], preferred_element_type=F32)
    dk = to_end * lax.dot_general(vb, dnb, _DIMS['nt'], preferred_element_type=F32)
    dv = jnp.dot(k_end, dnb, preferred_element_type=F32)
    dstate = jnp.exp(b_last_col) * dnew + lax.dot_general((q * eb).astype(BF16), dob, _DIMS['tn'],
                                                          preferred_element_type=F32)
    new_state = jnp.exp(b_last_col) * state + lax.dot_general(k_end, vb, _DIMS['tn'], preferred_element_type=F32)
    db_end = lax.dot_general(jnp.ones((1, dnew.shape[1]), F32), dnew * new_state, _DIMS['nt'], precision=hi,
                             preferred_element_type=F32)
    t3 = lax.broadcasted_iota(jnp.int32, (SUB, SUB, 1), 0)
    s3 = lax.broadcasted_iota(jnp.int32, (SUB, SUB, 1), 1)
    dq_rows, dk_rows, dv_rows = [], [], []
    for n in range(c // SUB):
        lo = n * SUB
        qn, kn, bn, vn, don = q[lo:lo + SUB], k[lo:lo + SUB], b[lo:lo + SUB], vb[lo:lo + SUB], dob[lo:lo + SUB]
        decay = jnp.exp(jnp.where(s3 <= t3, bn[:, None, :] - bn[None, :, :], -jnp.inf))
        sc = jnp.sum(qn[:, None, :] * kn[None, :, :] * decay, axis=-1)
        pull = lax.dot_general(don, vn, _DIMS['nt'], preferred_element_type=F32)[:, :, None] * decay
        dqn = jnp.sum(pull * kn[None, :, :], axis=1)
        dk_rows.append(jnp.sum(pull * qn[:, None, :], axis=0))
        dv_rows.append(lax.dot_general(sc.astype(BF16), don, _DIMS['tn'], preferred_element_type=F32))
        if n > 0:
            ref = jnp.sum(jnp.where(rows == lo - 1, b, 0.0), axis=0, keepdims=True)
            up, down = jnp.exp(bn - ref), jnp.exp(ref - b[:lo])
            qd, kd = (qn * up).astype(BF16), (k[:lo] * down).astype(BF16)
            so = lax.dot_general(qd, kd, _DIMS['nt'], preferred_element_type=F32).astype(BF16)
            ao = lax.dot_general(don, vb[:lo], _DIMS['nt'], preferred_element_type=F32).astype(BF16)
            dqn = dqn + up * jnp.dot(ao, kd, preferred_element_type=F32)
            rest = jnp.zeros((c - lo, kw), F32)
            dk = dk + jnp.concatenate([down * lax.dot_general(ao, qd, _DIMS['tn'], preferred_element_type=F32),
                                       rest], axis=0)
            dv = dv + jnp.concatenate([lax.dot_general(so, don, _DIMS['tn'], preferred_element_type=F32), rest],
                                      axis=0)
        dq_rows.append(dqn)
    dq = dq + jnp.concatenate(dq_rows, axis=0)
    dk = dk + jnp.concatenate(dk_rows, axis=0)
    dv = dv + jnp.concatenate(dv_rows, axis=0)
    db = q * dq - k * dk + jnp.where(rows == c - 1, db_end, 0.0)
    dg = jnp.dot((col >= row).astype(F32), db, precision=hi, preferred_element_type=F32)
    return dq, dk, dg, dv, dstate


def hg_scan_fwd(name, q, k, g, u, v_off, carry=None):
    s = q.shape[0]
    n = s // CHUNK

    def body(q_ref, k_ref, g_ref, v_ref, o_ref, st_ref, state):
        @pl.when(pl.program_id(1) == 0)
        def _():
            state[...] = jnp.zeros_like(state)

        for j in range(HG_GROUP):
            cols = slice(j * HEAD_W, (j + 1) * HEAD_W)
            st = state[j]
            st_ref[j] = st
            o, new = _hg_chunk(q_ref[:, cols], k_ref[:, cols], g_ref[:, cols], v_ref[:, cols], st)
            o_ref[:, cols] = o
            state[j] = new

    wide = HG_GROUP * HEAD_W
    blk = pl.BlockSpec((CHUNK, wide), lambda h, c: (c, h))
    return _call(
        name, body, grid=(HG_HEADS // HG_GROUP, n),
        in_specs=[blk, blk, blk, pl.BlockSpec((CHUNK, wide), lambda h, c: (c, v_off // HG_GROUP + h))],
        out_specs=[blk, pl.BlockSpec((HG_GROUP, None, HEAD_W, HEAD_W), lambda h, c: (h, c, 0, 0))],
        out_shape=[jax.ShapeDtypeStruct((s, HG_HEADS * HEAD_W), F32),
                   jax.ShapeDtypeStruct((HG_HEADS, n, HEAD_W, HEAD_W), F32)],
        scratch=[pltpu.VMEM((HG_GROUP, HEAD_W, HEAD_W), F32)], args=[q, k, g, u], carry=carry)


def hg_scan_bwd(name, q, k, g, u, v_off, states, do, carry=None):
    s = q.shape[0]
    n = s // CHUNK

    def body(q_ref, k_ref, g_ref, v_ref, st_ref, do_ref, dq_ref, dk_ref, dg_ref, dv_ref, dstate):
        @pl.when(pl.program_id(1) == 0)
        def _():
            dstate[...] = jnp.zeros_like(dstate)

        for j in range(HG_GROUP):
            cols = slice(j * HEAD_W, (j + 1) * HEAD_W)
            dq, dk, dg, dv, dst = _hg_chunk_bwd(q_ref[:, cols], k_ref[:, cols], g_ref[:, cols], v_ref[:, cols],
                                                st_ref[j], do_ref[:, cols], dstate[j])
            dq_ref[:, cols] = dq
            dk_ref[:, cols] = dk
            dg_ref[:, cols] = dg
            dv_ref[:, cols] = dv
            dstate[j] = dst

    wide = HG_GROUP * HEAD_W
    blk = pl.BlockSpec((CHUNK, wide), lambda h, c: (n - 1 - c, h))
    out = jax.ShapeDtypeStruct((s, HG_HEADS * HEAD_W), F32)
    return _call(
        name, body, grid=(HG_HEADS // HG_GROUP, n),
        in_specs=[blk, blk, blk, pl.BlockSpec((CHUNK, wide), lambda h, c: (n - 1 - c, v_off // HG_GROUP + h)),
                  pl.BlockSpec((HG_GROUP, None, HEAD_W, HEAD_W), lambda h, c: (h, n - 1 - c, 0, 0)), blk],
        out_specs=[blk, blk, blk, blk], out_shape=[out, out, out, out],
        scratch=[pltpu.VMEM((HG_GROUP, HEAD_W, HEAD_W), F32)], args=[q, k, g, u, states, do], carry=carry)


def pair_sum(name, blocks, theirs):
    _, rows, cols = theirs.shape
    tile = next((t for t in (1024, 512, 256, 128, 64, 32, 16) if rows % t == 0 and t * cols <= (1 << 20)), rows)

    def body(a_ref, b_ref, o_ref):
        mine = jnp.where(lax.axis_index("c") == 0, a_ref[0].astype(F32), a_ref[1].astype(F32))
        o_ref[...] = (mine + b_ref[...].astype(F32)).astype(o_ref.dtype)

    blk = pl.BlockSpec((None, tile, cols), lambda q, i: (q, i, 0))
    return _call(name, body, grid=(N_CHIP, rows // tile),
                 in_specs=[pl.BlockSpec((None, 2, tile, cols), lambda q, i: (q, 0, i, 0)), blk], out_specs=[blk],
                 out_shape=[jax.ShapeDtypeStruct(theirs.shape, theirs.dtype)],
                 args=[blocks.reshape(N_CHIP, 2, rows, cols), theirs])[0]


def adamw(name, landed, w, m, v):
    rows, cols = w.shape
    slots = landed.shape[0]
    tile = next((t for t in (512, 256, 128, 64, 32, 16, 8) if rows % t == 0 and t * cols <= (1 << 18)), rows)

    def body(l_ref, w_ref, m_ref, v_ref, g_ref, d_ref, nm_ref, nv_ref):
        gv = l_ref[0].astype(F32)
        for s in range(1, slots):
            gv = gv + l_ref[s].astype(F32)
        nm = ADAM_B1 * m_ref[...] + (1.0 - ADAM_B1) * gv
        nv = ADAM_B2 * v_ref[...] + (1.0 - ADAM_B2) * jnp.square(gv)
        m_hat = nm / (1.0 - ADAM_B1 ** ADAM_STEP)
        v_hat = nv / (1.0 - ADAM_B2 ** ADAM_STEP)
        g_ref[...] = gv
        d_ref[...] = -ADAM_LR * (m_hat / (jnp.sqrt(v_hat) + ADAM_EPS) + ADAM_WD * w_ref[...])
        nm_ref[...] = nm
        nv_ref[...] = nv

    blk = pl.BlockSpec((tile, cols), lambda i: (i, 0))
    out = jax.ShapeDtypeStruct((rows, cols), F32)
    return _call(name, body, grid=(rows // tile,),
                 in_specs=[pl.BlockSpec((slots, tile, cols), lambda i: (0, i, 0)), blk, blk, blk],
                 out_specs=[blk] * 4, out_shape=[out] * 4, args=[landed, w, m, v])


def _swap_halves(pe):
    half = QK_ROPE // 2
    return jnp.concatenate([-pe[..., half:], pe[..., :half]], axis=-1)


def _unswap_halves(dsw):
    half = QK_ROPE // 2
    return jnp.concatenate([dsw[..., half:], -dsw[..., :half]], axis=-1)


def _w_in_ext(w):
    kpe = w[:, 9216:9280]
    z = jnp.zeros((w.shape[0], HEAD_W - QK_ROPE), w.dtype)
    pad = jnp.zeros((w.shape[0], U_PAD), w.dtype)
    return jnp.concatenate([w[:, :9216], w[:, 9280:], kpe, z, _swap_halves(kpe), z, pad], axis=1)


def _w_in_grad(d):
    dkpe = d[:, 13312:13376] + _unswap_halves(d[:, 13440:13504])
    return jnp.concatenate([d[:, :9216], dkpe, d[:, 9216:13312]], axis=1)


def _w_q_ext(w):
    w3 = w.reshape(Q_LORA, MLA_HEADS, MLA_QK)
    pe = w3[:, :, HEAD_W:]
    z = jnp.zeros((Q_LORA, MLA_HEADS, HEAD_W - QK_ROPE), w.dtype)
    wide = MLA_HEADS * HEAD_W
    return jnp.concatenate([w3[:, :, :HEAD_W].reshape(Q_LORA, wide),
                            jnp.concatenate([pe, z], axis=2).reshape(Q_LORA, wide),
                            jnp.concatenate([_swap_halves(pe), z], axis=2).reshape(Q_LORA, wide)], axis=1)


def _w_q_grad(parts):
    d3 = [p.reshape(Q_LORA, MLA_HEADS, HEAD_W) for p in parts]
    dpe = d3[1][:, :, :QK_ROPE] + _unswap_halves(d3[2][:, :, :QK_ROPE])
    return jnp.concatenate([d3[0], dpe], axis=2).reshape(Q_LORA, MLA_HEADS * MLA_QK)


def _hg_prep(f_raw, q_hg, logits):
    lb = jax.nn.softmax(logits, axis=0)[0:1, :]
    log_f = jnp.logaddexp(jnp.log(lb), jnp.log1p(-lb) + jax.nn.log_sigmoid(f_raw))
    k_in = (1.0 - lb) * jax.nn.sigmoid(-f_raw)
    return log_f, k_in, jax.nn.silu(q_hg)


def _rope(q_nope, q_pe, q_sw, k_nope, k_pe, k_sw, cos, sin):
    qf = jnp.concatenate([q_nope, q_pe * cos + q_sw * sin], axis=1)
    kf = jnp.concatenate([k_nope, k_pe * cos + k_sw * sin], axis=1)
    return qf, kf


def _step(a):
    x, mem, target = a['x'][0], a['mem'][0], a['loss_target'][0]
    s, d = x.shape
    nm = mem.shape[0]
    ff = N_DEV * a['ffn1_w_gate'].shape[-1]
    cs = ff // N_DEV
    tr = min(s, 256)
    tp = min(s, 128)
    th = s
    ta = 1024

    bf = {n: a[n][0].astype(BF16) for n in BIG}
    half_in = bf['w_in'].shape[0] // 2
    bf['w_in#0'], bf['w_in#1'] = bf['w_in'][:half_in], bf['w_in'][half_in:]
    gat, full = {}, {}

    my_chip = 2 * lax.axis_index("x") + lax.axis_index("y")
    me = 2 * my_chip + lax.axis_index("c")

    def gathered(names, carry):
        for n, g8 in zip(names or [], carry.results if carry else []):
            r, c = bf[n].shape
            g8 = lax.dynamic_update_index_in_dim(g8, bf[n], me, 0)
            gat[n] = g8
            if not n.startswith('ffn'):
                by_col = BIG[n.split('#')[0]]
                full[n] = g8.transpose(1, 0, 2).reshape(r, N_DEV * c) if by_col else g8.reshape(N_DEV * r, c)

    xa_names = ['xa_w_q', 'xa_w_k', 'xa_w_v', 'xa_w_o']
    ffn2_names = ['ffn2_w_gate', 'ffn2_w_up', 'ffn2_w_down']
    first = Gather([bf['ffn1_w_gate']])
    run_alone("gather_first", first)
    gathered(['ffn1_w_gate'], first)

    inv_freq = 1.0 / (ROPE_THETA ** (jnp.arange(0, QK_ROPE, 2, dtype=F32) / QK_ROPE))
    ang = a['positions'][0].astype(F32)[:, None] * inv_freq
    zero = jnp.zeros((s, HEAD_W - QK_ROPE), F32)
    cos = jnp.concatenate([jnp.cos(ang), jnp.cos(ang), zero], axis=1)
    sin = jnp.concatenate([jnp.sin(ang), jnp.sin(ang), zero], axis=1)

    gs = {}
    gb = {}

    def by_rows(g):
        return g.reshape(N_DEV, g.shape[0] // N_DEV, g.shape[1])

    def by_cols(g):
        return g.reshape(g.shape[0], N_DEV, g.shape[1] // N_DEV).transpose(1, 0, 2)

    pre_fn = lambda xv, g: (_rms(xv, g),)
    pre_res_fn = lambda xv, g: (_rms(xv, g), xv)

    def pre_norm(tag, x_in, g):
        return rowmap(tag + "_pre", pre_fn, [R_(x_in), P_(g)], [(BF16, x_in.shape[1], 0)], tile=tr)[0]

    def pre_norm_bwd(tag, x_in, g, dh, d_out, carry=None):
        return rowmap(tag + "_pre_bwd", pre_res_fn, [R_(x_in), P_(g)], tile=tr, cts=[R_(dh), R_(d_out)],
                      wrt=[0, 1], gdt=[F32, F32], carry=carry)

    def post_fn(weight):
        return lambda xv, y, g: (xv + weight * _rms(y, g),)

    def post_norm(tag, x_in, y, g, weight):
        return rowmap(tag + "_post", post_fn(weight), [R_(x_in), R_(y), P_(g)], [(F32, d, 0)], tile=tr)[0]

    def post_norm_bwd(tag, x_in, y, g, weight, d_out):
        return rowmap(tag + "_post_bwd", post_fn(weight), [R_(x_in), R_(y), P_(g)], tile=tr, cts=[R_(d_out)],
                      wrt=[1, 2], gdt=[BF16, F32])

    act_fn = lambda av, bv: (jax.nn.silu(av) * bv,)

    def fetch(plan, key):
        names = plan.get(key)
        return (names, Gather([bf[n] for n in names])) if names else (None, None)

    def send(plan, key):
        acts = plan.get(key)
        if not acts:
            return None, None
        parts = []
        for kind, names in acts:
            if kind == 'spread':
                for n in names:
                    chip_sums[n] = pair_sum("pair_" + n.replace('#', '_'), gb[n], halves[n])
            parts.append(Halve([gb[n] for n in names]) if kind == 'halve' else Spread([chip_sums[n] for n in names]))
        return acts, Joint(parts)

    def ffn_fwd(tag, x_in, plan):
        h = pre_norm(tag, x_in, a[tag + '_pre_g'])
        names, carry = fetch(plan, 'gate')
        av = mm(tag + "_gate", h, gat[tag + '_w_gate'], 'nn', jdim='n', out_dtype=BF16, carry=carry)
        gathered(names, carry)
        names, carry = fetch(plan, 'up')
        bv = mm(tag + "_up", h, gat[tag + '_w_up'], 'nn', jdim='n', out_dtype=BF16, carry=carry)
        gathered(names, carry)
        a2, b2 = av.reshape(N_DEV * s, cs), bv.reshape(N_DEV * s, cs)
        names, carry = fetch(plan, 'act')
        z = rowmap(tag + "_act", act_fn, [R_(a2), R_(b2)], [(BF16, cs, 0)], tile=ta, carry=carry)[0]
        z = z.reshape(N_DEV, s, cs)
        gathered(names, carry)
        names, carry = fetch(plan, 'dn')
        y = mm(tag + "_dn", z, gat[tag + '_w_down'], 'nn', jdim='k', carry=carry)
        gathered(names, carry)
        return post_norm(tag, x_in, y, a[tag + '_post_g'], 0.5), (x_in, h, a2, b2, z, y)

    def ffn_bwd(tag, res, d_out, plan):
        x_in, h, a2, b2, z, y = res
        dy, gs[tag + '_post_g'] = post_norm_bwd(tag, x_in, y, a[tag + '_post_g'], 0.5, d_out)
        acts, carry = send(plan, 'dn_dx')
        dz = mm(tag + "_dn_dx", dy, gat[tag + '_w_down'], 'nt', jdim='n', carry=carry)
        exchanged(acts, carry)
        acts, carry = send(plan, 'dn_dw')
        gb[tag + '_w_down'] = mm(tag + "_dn_dw", z, dy, 'tn', jdim='m', out_dtype=BF16, carry=carry)
        exchanged(acts, carry)
        acts, carry = send(plan, 'act_bwd')
        da, db = rowmap(tag + "_act_bwd", act_fn, [R_(a2), R_(b2)], tile=ta, cts=[R_(dz.reshape(N_DEV * s, cs))],
                        wrt=[0, 1], gdt=[BF16, BF16], carry=carry)
        exchanged(acts, carry)
        da, db = da.reshape(N_DEV, s, cs), db.reshape(N_DEV, s, cs)
        names, carry = send(plan, 'gu_dx')
        dh = mm(tag + "_gu_dx", [da, db], [gat[tag + '_w_gate'], gat[tag + '_w_up']], 'nt', jdim='k', carry=carry)
        exchanged(names, carry)
        gb[tag + '_w_gate'] = mm(tag + "_gate_dw", h, da, 'tn', jdim='n', out_dtype=BF16)
        names, carry = send(plan, 'up_dw')
        gb[tag + '_w_up'] = mm(tag + "_up_dw", h, db, 'tn', jdim='n', out_dtype=BF16, carry=carry)
        exchanged(names, carry)
        names, carry = send(plan, 'pre_bwd')
        d_in, gs[tag + '_pre_g'] = pre_norm_bwd(tag, x_in, a[tag + '_pre_g'], dh, d_out, carry)
        exchanged(names, carry)
        return d_in

    hg_out_fn = lambda o, og, g: (_rms(o, g) * jax.nn.silu(og),)
    mla_norm_fn = lambda cq, ckv, gq, gkv: (_rms(cq, gq), _rms(ckv, gkv))
    gate_fn = lambda ga, gb, ya, yb: (jax.nn.sigmoid(ga) * ya + jax.nn.sigmoid(gb) * yb,)
    mla = dict(heads=MLA_HEADS, dq=2 * HEAD_W, dv=HEAD_W, koff=0, kstride=1, voff=1, vstride=2,
               scale=MLA_QK ** -0.5, causal=True)

    def mix_fwd(x_in, plan):
        w_in = _w_in_ext(jnp.concatenate([full['w_in#0'], full['w_in#1']], axis=0))
        h = pre_norm("mix", x_in, a['mix_pre_g'])
        names, carry = fetch(plan, 'in')
        u = mm("mix_in", h, w_in, 'nn', carry=carry)
        gathered(names, carry)
        w_q, w_kv = _w_q_ext(full['mla_w_q_up']), full['mla_w_kv_up']
        hg_ins = [R_(u, 2048, 1), R_(u, 2048, 0), P_(a['hgrn_lb_logits'])]
        log_f, k_in, q_in = rowmap("hg_prep", _hg_prep, hg_ins, [(F32, 2048, 0)] * 3, tile=tp)
        names, carry = fetch(plan, 'scan')
        o_a, states = hg_scan_fwd("hg_scan", q_in, k_in, log_f, u, 32, carry=carry)
        gathered(names, carry)
        out_ins = [R_(o_a, HEAD_W, 0, 1), R_(u, HEAD_W, 48, 1), P_(a['hg_norm_g'], HEAD_W, 0, 1)]
        oag = rowmap("hg_out", hg_out_fn, out_ins, [(BF16, HEAD_W, 1)], tile=th, ncol=HG_HEADS)[0]
        y_a = mm("mix_a", oag, full['w_branch_a'], 'nn')
        norm_ins = [R_(u, 512, 16), R_(u, 512, 17), P_(a['mla_q_norm_g']), P_(a['mla_kv_norm_g'])]
        cqn, ckvn = rowmap("mla_norm", mla_norm_fn, norm_ins, [(BF16, 512, 0)] * 2, tile=tr)
        q_all = mm("mla_qup", cqn, w_q, 'nn')
        kv = mm("mla_kvup", ckvn, w_kv, 'nn')
        rope_ins = [R_(q_all, HEAD_W, 0, 1), R_(q_all, HEAD_W, 16, 1), R_(q_all, HEAD_W, 32, 1),
                    R_(kv, HEAD_W, 0, 2), R_(u, HEAD_W, 104, 0), R_(u, HEAD_W, 105, 0), R_(cos), R_(sin)]
        qf, kf = rowmap("mla_rope", _rope, rope_ins, [(BF16, 2 * HEAD_W, 1)] * 2, tile=th, ncol=MLA_HEADS,
                        rows_inner=False)
        names, carry = fetch(plan, 'attn')
        o_b = attn_fwd("mla_attn", qf, kf, kv, carry=carry, **mla)
        gathered(names, carry)
        y_b = mm("mix_b", o_b, full['w_branch_b'], 'nn')
        gate_ins = [R_(u, 1024, 9, 1), R_(u, 1024, 11, 1), R_(y_a, 1024, 0, 1), R_(y_b, 1024, 0, 1)]
        y = rowmap("mix_gate", gate_fn, gate_ins, [(BF16, 1024, 1)], tile=tr, ncol=2)[0]
        yo = mm("mix_out", y, full['w_out'], 'nn')
        res = (x_in, h, hg_ins, q_in, k_in, log_f, u, states, out_ins, oag, norm_ins, cqn, ckvn, rope_ins, qf, kf,
               kv, o_b, gate_ins, y, yo, w_in, w_q, w_kv)
        return post_norm("mix", x_in, yo, a['mix_post_g'], 1.0), res

    def mix_bwd(res, d_out, plan):
        (x_in, h, hg_ins, q_in, k_in, log_f, u, states, out_ins, oag, norm_ins, cqn, ckvn, rope_ins, qf, kf, kv,
         o_b, gate_ins, y, yo, w_in, w_q, w_kv) = res
        dyo, gs['mix_post_g'] = post_norm_bwd("mix", x_in, yo, a['mix_post_g'], 1.0, d_out)
        dy = mm("mix_out_dx", dyo, full['w_out'], 'nt')
        gb['w_out'] = by_rows(mm("mix_out_dw", y, dyo, 'tn', out_dtype=BF16))
        dga, dgb, dya, dyb = rowmap("mix_gate_bwd", gate_fn, gate_ins, tile=tr, ncol=2, cts=[R_(dy, 1024, 0, 1)],
                                    wrt=[0, 1, 2, 3], gdt=[BF16] * 4)
        gb['w_branch_b'] = by_rows(mm("mix_b_dw", o_b, dyb, 'tn', out_dtype=BF16))
        do_b = mm("mix_b_dx", dyb, full['w_branch_b'], 'nt')
        names, carry = send(plan, 'attn_bwd')
        dqf, dkf, dv = attn_bwd("mla_attn_bwd", qf, kf, kv, do_b, carry=carry, **mla)
        exchanged(names, carry)
        dqn, dqp, dqs, dkn, dkpe, dksw = rowmap(
            "mla_rope_bwd", _rope, rope_ins, tile=th, ncol=MLA_HEADS, rows_inner=False,
            cts=[R_(dqf, 2 * HEAD_W, 0, 1), R_(dkf, 2 * HEAD_W, 0, 1)], wrt=[0, 1, 2, 3, 4, 5],
            gdt=[BF16, BF16, BF16, BF16, F32, F32])
        wide = MLA_HEADS * HEAD_W
        dq_parts = [dqn, dqp, dqs]
        gb['mla_w_q_up'] = by_cols(_w_q_grad([mm("mla_qup_dw%d" % n, cqn, dq_parts[n], 'tn', out_dtype=BF16)
                                              for n in range(3)]))
        dcqn = mm("mla_qup_dx", dq_parts, [w_q[:, n * wide:(n + 1) * wide] for n in range(3)], 'nt')
        w_kv4 = w_kv.reshape(KV_LORA, MLA_HEADS, 2, HEAD_W)
        dw_k = mm("mla_kup_dw", ckvn, dkn, 'tn', out_dtype=BF16).reshape(KV_LORA, MLA_HEADS, 1, HEAD_W)
        dw_v = mm("mla_vup_dw", ckvn, dv, 'tn', out_dtype=BF16).reshape(KV_LORA, MLA_HEADS, 1, HEAD_W)
        gb['mla_w_kv_up'] = by_cols(jnp.concatenate([dw_k, dw_v], axis=2).reshape(KV_LORA, 2 * wide))
        dckvn = mm("mla_kvup_dx", [dkn, dv],
                   [w_kv4[:, :, 0].reshape(KV_LORA, wide), w_kv4[:, :, 1].reshape(KV_LORA, wide)], 'nt')
        dcq, dckv, gs['mla_q_norm_g'], gs['mla_kv_norm_g'] = rowmap(
            "mla_norm_bwd", mla_norm_fn, norm_ins, tile=tr, cts=[R_(dcqn), R_(dckvn)], wrt=[0, 1, 2, 3],
            gdt=[BF16, BF16, F32, F32])
        gb['w_branch_a'] = by_rows(mm("mix_a_dw", oag, dya, 'tn', out_dtype=BF16))
        doag = mm("mix_a_dx", dya, full['w_branch_a'], 'nt')
        acts, carry = send(plan, 'out_bwd')
        do_a, dog, gs['hg_norm_g'] = rowmap("hg_out_bwd", hg_out_fn, out_ins, tile=th, ncol=HG_HEADS,
                                            cts=[R_(doag, HEAD_W, 0, 1)], wrt=[0, 1, 2], gdt=[F32, BF16, F32],
                                            carry=carry)
        exchanged(acts, carry)
        names, carry = send(plan, 'scan_bwd')
        dq_in, dk_in, dlog_f, di = hg_scan_bwd("hg_scan_bwd", q_in, k_in, log_f, u, 32, states, do_a, carry=carry)
        exchanged(names, carry)
        df, dq_hg, gs['hgrn_lb_logits'] = rowmap("hg_prep_bwd", _hg_prep, hg_ins, tile=tp,
                                                 cts=[R_(dlog_f), R_(dk_in), R_(dq_in)], wrt=[0, 1, 2],
                                                 gdt=[BF16, BF16, F32])
        du = jnp.concatenate([dq_hg, df, di.astype(BF16), dog, dcq, dckv, dga, dgb, dkpe.astype(BF16),
                              dksw.astype(BF16), jnp.zeros((s, U_PAD), BF16)], axis=1)
        g_in = by_cols(_w_in_grad(mm("mix_in_dw", h, du, 'tn', out_dtype=BF16)))
        gb['w_in#0'], gb['w_in#1'] = g_in[:, :half_in], g_in[:, half_in:]
        names, carry = send(plan, 'in_dx')
        dh = mm("mix_in_dx", du, w_in, 'nt', carry=carry)
        exchanged(names, carry)
        d_in, gs['mix_pre_g'] = pre_norm_bwd("mix", x_in, a['mix_pre_g'], dh, d_out)
        return d_in

    xa = dict(heads=XA_HEADS, dq=HEAD_W, dv=HEAD_W, koff=0, kstride=1, voff=XA_HEADS, vstride=1,
              scale=HEAD_W ** -0.5, causal=False)
    tm_ = min(nm, 128)

    def xa_fwd(x_in):
        w_xkv = jnp.concatenate([full['xa_w_k'], full['xa_w_v']], axis=1)
        h = pre_norm("xa", x_in, a['xa_pre_g'])
        mn = rowmap("xa_mem", pre_fn, [R_(mem), P_(a['xa_mem_g'])], [(BF16, d, 0)], tile=tm_)[0]
        q = mm("xa_q", h, full['xa_w_q'], 'nn')
        kv = mm("xa_kv", mn, w_xkv, 'nn')
        o = attn_fwd("xa_attn", q, kv, kv, **xa)
        yo = mm("xa_o", o, full['xa_w_o'], 'nn')
        return post_norm("xa", x_in, yo, a['xa_post_g'], 1.0), (x_in, h, mn, q, kv, o, yo, w_xkv)

    def xa_bwd(res, d_out, plan):
        x_in, h, mn, q, kv, o, yo, w_xkv = res
        dyo, gs['xa_post_g'] = post_norm_bwd("xa", x_in, yo, a['xa_post_g'], 1.0, d_out)
        do = mm("xa_o_dx", dyo, full['xa_w_o'], 'nt')
        gb['xa_w_o'] = by_cols(mm("xa_o_dw", o, dyo, 'tn', out_dtype=BF16))
        dq, dk, dv = attn_bwd("xa_attn_bwd", q, kv, kv, do, **xa)
        dkv = jnp.concatenate([dk, dv], axis=1).astype(BF16)
        dw = mm("xa_kv_dw", mn, dkv, 'tn', out_dtype=BF16)
        gb['xa_w_k'], gb['xa_w_v'] = by_rows(dw[:, :XA_HEADS * HEAD_W]), by_rows(dw[:, XA_HEADS * HEAD_W:])
        dmn = mm("xa_kv_dx", dkv, w_xkv, 'nt')
        gs['xa_mem_g'] = rowmap("xa_mem_bwd", pre_fn, [R_(mem), P_(a['xa_mem_g'])], tile=tm_, cts=[R_(dmn)],
                                wrt=[1], gdt=[F32])[0]
        gb['xa_w_q'] = by_rows(mm("xa_q_dw", h, dq, 'tn', out_dtype=BF16))
        dh = mm("xa_q_dx", dq, full['xa_w_q'], 'nt')
        acts, carry = send(plan, 'pre_bwd')
        d_in, gs['xa_pre_g'] = pre_norm_bwd("xa", x_in, a['xa_pre_g'], dh, d_out, carry)
        exchanged(acts, carry)
        return d_in

    landed = {}

    halves = {}
    chip_sums = {}

    def exchanged(acts, carry):
        for (kind, names), part in zip(acts or [], carry.parts if carry else []):
            for n, got in zip(names, part.results):
                if kind == 'halve':
                    halves[n] = got
                else:
                    own = lax.dynamic_index_in_dim(chip_sums[n], my_chip, 0, keepdims=True)
                    landed[n] = lax.dynamic_update_slice_in_dim(got, own, my_chip, 0)

    x1, r1 = ffn_fwd('ffn1', x, {'gate': ['ffn1_w_up'], 'up': ['ffn1_w_down'], 'act': ['w_in#0'], 'dn': ['w_in#1']})
    x2, r2 = mix_fwd(x1, {'in': ['mla_w_q_up', 'mla_w_kv_up', 'w_branch_a', 'w_branch_b', 'w_out'],
                          'scan': xa_names + ['ffn2_w_gate'], 'attn': ['ffn2_w_up']})
    x3, r3 = xa_fwd(x2)
    x4, r4 = ffn_fwd('ffn2', x3, {'gate': ['ffn2_w_down']})

    def loss_fn(y, t):
        diff = y - t
        return diff * (1.0 / d), jnp.mean(diff * diff, axis=-1, keepdims=True)

    d4, row_loss = rowmap("loss", loss_fn, [R_(x4), R_(target)], [(F32, d, 0), (F32, 1, 0)], tile=tr)
    loss = lax.psum(0.5 * jnp.sum(row_loss), ("x", "y", "c"))
    late = ['mla_w_q_up', 'mla_w_kv_up', 'w_branch_a']
    d3 = ffn_bwd('ffn2', r4, d4, {'act_bwd': [('halve', ['ffn2_w_down'])], 'gu_dx': [('spread', ['ffn2_w_down'])],
                                  'up_dw': [('halve', ['ffn2_w_gate'])], 'pre_bwd': [('halve', ['ffn2_w_up'])]})
    d2 = xa_bwd(r3, d3, {'pre_bwd': [('halve', xa_names)]})
    d1 = mix_bwd(r2, d2, {'attn_bwd': [('spread', ['ffn2_w_gate', 'ffn2_w_up']), ('halve', ['w_out', 'w_branch_b'])],
                          'out_bwd': [('halve', late)],
                          'scan_bwd': [('spread', xa_names + ['w_out', 'w_branch_b'] + late)],
                          'in_dx': [('halve', ['w_in#0', 'w_in#1'])]})
    grad_x = ffn_bwd('ffn1', r1, d1, {'dn_dx': [('spread', ['w_in#0'])], 'dn_dw': [('spread', ['w_in#1'])],
                                      'act_bwd': [('halve', ['ffn1_w_down'])],
                                      'gu_dx': [('spread', ['ffn1_w_down'])], 'up_dw': [('halve', ['ffn1_w_gate'])],
                                      'pre_bwd': [('spread', ['ffn1_w_gate']), ('halve', ['ffn1_w_up'])]})
    acts, carry = send({'end': [('spread', ['ffn1_w_up'])]}, 'end')
    run_alone("spread_last", carry)
    exchanged(acts, carry)

    def pack_small(vals):
        flat = jnp.concatenate([vals[n].reshape(-1) for n in SMALL])
        rows = -(-flat.shape[0] // PACK_W)
        rows = -(-rows // 8) * 8
        return jnp.pad(flat, (0, rows * PACK_W - flat.shape[0])).reshape(rows, PACK_W)

    def unpack_small(buf):
        flat, out, at = buf.reshape(-1), {}, 0
        for n in SMALL:
            size = a[n].shape[0] * a[n].shape[1]
            out[n] = flat[at:at + size].reshape(a[n].shape)
            at += size
        return out

    g_small = pack_small(gs)
    small = Gather([g_small])
    run_alone("gather_g_small", small)
    g_small = lax.dynamic_update_index_in_dim(small.results[0], g_small, me, 0)

    grads, delta, new_m, new_v = {}, {}, {}, {}
    packs = adamw("adamw_small", g_small, pack_small(a), pack_small({n: a['m_' + n] for n in SMALL}),
                  pack_small({n: a['v_' + n] for n in SMALL}))
    for dst, buf in zip((grads, delta, new_m, new_v), packs):
        dst.update(unpack_small(buf))
    landed['w_in'] = jnp.concatenate([landed['w_in#0'], landed['w_in#1']], axis=1)
    for n in BIG:
        outs = adamw("adamw_" + n, landed[n], a[n][0], a['m_' + n][0], a['v_' + n][0])
        grads[n], delta[n], new_m[n], new_v[n] = (t.reshape(a[n].shape) for t in outs)

    return (loss, grad_x[None], *[grads[n] for n in WEIGHTS], *[delta[n] for n in WEIGHTS],
            *[new_m[n] for n in WEIGHTS], *[new_v[n] for n in WEIGHTS])


def kernel(x, mem, positions, hgrn_lb_logits, ffn1_pre_g, ffn1_w_gate, ffn1_w_up, ffn1_w_down, ffn1_post_g, mix_pre_g, w_in, hg_norm_g, mla_q_norm_g, mla_w_q_up, mla_kv_norm_g, mla_w_kv_up, w_branch_a, w_branch_b, w_out, mix_post_g, xa_pre_g, xa_mem_g, xa_w_q, xa_w_k, xa_w_v, xa_w_o, xa_post_g, ffn2_pre_g, ffn2_w_gate, ffn2_w_up, ffn2_w_down, ffn2_post_g, loss_target, m_hgrn_lb_logits, m_ffn1_pre_g, m_ffn1_w_gate, m_ffn1_w_up, m_ffn1_w_down, m_ffn1_post_g, m_mix_pre_g, m_w_in, m_hg_norm_g, m_mla_q_norm_g, m_mla_w_q_up, m_mla_kv_norm_g, m_mla_w_kv_up, m_w_branch_a, m_w_branch_b, m_w_out, m_mix_post_g, m_xa_pre_g, m_xa_mem_g, m_xa_w_q, m_xa_w_k, m_xa_w_v, m_xa_w_o, m_xa_post_g, m_ffn2_pre_g, m_ffn2_w_gate, m_ffn2_w_up, m_ffn2_w_down, m_ffn2_post_g, v_hgrn_lb_logits, v_ffn1_pre_g, v_ffn1_w_gate, v_ffn1_w_up, v_ffn1_w_down, v_ffn1_post_g, v_mix_pre_g, v_w_in, v_hg_norm_g, v_mla_q_norm_g, v_mla_w_q_up, v_mla_kv_norm_g, v_mla_w_kv_up, v_w_branch_a, v_w_branch_b, v_w_out, v_mix_post_g, v_xa_pre_g, v_xa_mem_g, v_xa_w_q, v_xa_w_k, v_xa_w_v, v_xa_w_o, v_xa_post_g, v_ffn2_pre_g, v_ffn2_w_gate, v_ffn2_w_up, v_ffn2_w_down, v_ffn2_post_g):
    return _step(dict(locals()))
```

```python
import functools

import jax
import jax.numpy as jnp
from jax import lax
from jax.experimental import pallas as pl
from jax.experimental.pallas import tpu as pltpu

F32 = jnp.float32
BF16 = jnp.bfloat16

N_DEV = 8
D_MODEL = 2048
CHUNK = 64
CHUNK_SHIFT = 6
SUB = 16
HG_HEADS = 16
HG_GROUP = 16
HEAD_W = 128
MLA_HEADS = 16
Q_LORA = 512
KV_LORA = 512
QK_ROPE = 64
MLA_QK = 192
XA_HEADS = 4
ROPE_THETA = 10000.0
EPS = 1e-6
PACK_W = 1024
VMEM_LIMIT = 56 * 1024 * 1024

ADAM_LR = 0.001
ADAM_B1 = 0.9
ADAM_B2 = 0.999
ADAM_EPS = 1e-08
ADAM_WD = 0.01
ADAM_STEP = 10

U_PAD = 256

WEIGHTS = ['hgrn_lb_logits', 'ffn1_pre_g', 'ffn1_w_gate', 'ffn1_w_up', 'ffn1_w_down', 'ffn1_post_g', 'mix_pre_g',
           'w_in', 'hg_norm_g', 'mla_q_norm_g', 'mla_w_q_up', 'mla_kv_norm_g', 'mla_w_kv_up', 'w_branch_a',
           'w_branch_b', 'w_out', 'mix_post_g', 'xa_pre_g', 'xa_mem_g', 'xa_w_q', 'xa_w_k', 'xa_w_v', 'xa_w_o',
           'xa_post_g', 'ffn2_pre_g', 'ffn2_w_gate', 'ffn2_w_up', 'ffn2_w_down', 'ffn2_post_g']
BIG = {'ffn1_w_gate': True, 'ffn1_w_up': True, 'ffn1_w_down': False, 'w_in': True, 'mla_w_q_up': True,
       'mla_w_kv_up': True, 'w_branch_a': False, 'w_branch_b': False, 'w_out': False, 'xa_w_q': False,
       'xa_w_k': False, 'xa_w_v': False, 'xa_w_o': True, 'ffn2_w_gate': True, 'ffn2_w_up': True,
       'ffn2_w_down': False}
SMALL = [n for n in WEIGHTS if n not in BIG]


def _cparams(**kw):
    return pltpu.CompilerParams(vmem_limit_bytes=VMEM_LIMIT, **kw)


def _pick(dim, cands):
    for c in cands:
        if dim % c == 0:
            return c
    return dim


def _place():
    return lax.axis_index("x"), lax.axis_index("y"), lax.axis_index("c")


def _slot(px, py, pc):
    return 4 * px + 2 * py + pc


class Gather:
    def __init__(self, tensors):
        self.operands = list(tensors)
        self.out_shape = [jax.ShapeDtypeStruct((N_DEV,) + t.shape, t.dtype) for t in tensors]
        n = len(tensors)
        self.scratch = [pltpu.SemaphoreType.DMA((n, N_DEV - 1)), pltpu.SemaphoreType.DMA((n, N_DEV - 1))]

    def _copies(self, t, x_ref, out_ref, scr):
        send, recv = scr
        x, y, c = _place()
        me, sibling = (x, y, c), (x, y, 1 - c)
        x_nbr, y_nbr, diag = (1 - x, y, c), (x, 1 - y, c), (1 - x, 1 - y, c)
        relay_from, relay_to = (x ^ (1 - c), y ^ c, c), (x ^ c, y ^ (1 - c), c)

        def copy(k, block, to, src=None):
            rows = out_ref.at[_slot(*block)]
            return pltpu.make_async_remote_copy(src_ref=rows if src is None else src, dst_ref=rows,
                                                send_sem=send.at[t, k], recv_sem=recv.at[t, k], device_id=to,
                                                device_id_type=pl.DeviceIdType.MESH)

        def other(block):
            return block[0], block[1], 1 - c

        sent = [(me, sibling, x_ref), (me, x_nbr, x_ref), (me, y_nbr, x_ref), (relay_from, relay_to, None),
                (x_nbr, sibling, None), (y_nbr, sibling, None), (diag, sibling, None)]
        landing = [sibling, x_nbr, y_nbr, diag, other(x_nbr), other(y_nbr), other(diag)]
        return (lambda k: copy(k, *sent[k])), (lambda k: copy(k, landing[k], me))

    def start(self, ins, outs, scr):
        for t, (x_ref, out_ref) in enumerate(zip(ins, outs)):
            give, _ = self._copies(t, x_ref, out_ref, scr)
            for k in range(3):
                give(k).start()

    def middle(self, ins, outs, scr):
        for t, (x_ref, out_ref) in enumerate(zip(ins, outs)):
            give, take = self._copies(t, x_ref, out_ref, scr)
            take(1).wait_recv()
            take(2).wait_recv()
            for k in (3, 4, 5):
                give(k).start()

    def finish(self, ins, outs, scr):
        for t, (x_ref, out_ref) in enumerate(zip(ins, outs)):
            give, take = self._copies(t, x_ref, out_ref, scr)
            take(3).wait_recv()
            give(6).start()
            for k in (0, 4, 5, 6):
                take(k).wait_recv()
            for k in range(N_DEV - 1):
                give(k).wait_send()

    def set_results(self, res):
        self.results = list(res)


N_CHIP = N_DEV // 2


class Halve:
    def __init__(self, tensors):
        self.operands = list(tensors)
        self.out_shape = [jax.ShapeDtypeStruct((N_CHIP,) + t.shape[1:], t.dtype) for t in tensors]
        n = len(tensors)
        self.scratch = [pltpu.SemaphoreType.DMA((n, N_CHIP)), pltpu.SemaphoreType.DMA((n, N_CHIP))]

    def _copies(self, t, x_ref, theirs_ref, scr):
        send, recv = scr
        x, y, c = _place()
        return [pltpu.make_async_remote_copy(
            src_ref=x_ref.at[2 * q + 1 - c], dst_ref=theirs_ref.at[q], send_sem=send.at[t, q],
            recv_sem=recv.at[t, q], device_id=(x, y, 1 - c), device_id_type=pl.DeviceIdType.MESH)
            for q in range(N_CHIP)]

    def start(self, ins, outs, scr):
        for t, (x_ref, theirs_ref) in enumerate(zip(ins, outs)):
            for give in self._copies(t, x_ref, theirs_ref, scr):
                give.start()

    def middle(self, ins, outs, scr):
        pass

    def finish(self, ins, outs, scr):
        for t, (x_ref, theirs_ref) in enumerate(zip(ins, outs)):
            for give in self._copies(t, x_ref, theirs_ref, scr):
                give.wait_recv()
                give.wait_send()

    def set_results(self, res):
        self.results = list(res)


class Spread:
    def __init__(self, tensors):
        self.operands = list(tensors)
        self.out_shape = [jax.ShapeDtypeStruct(t.shape, t.dtype) for t in tensors]
        n = len(tensors)
        self.scratch = [pltpu.SemaphoreType.DMA((n, N_CHIP - 1)), pltpu.SemaphoreType.DMA((n, N_CHIP - 1))]

    def _copies(self, t, y_ref, out_ref, scr, outgoing):
        send, recv = scr
        x, y, c = _place()
        copies = []
        for k in range(1, N_CHIP):
            px, py = x ^ (k >> 1), y ^ (k & 1)
            copies.append(pltpu.make_async_remote_copy(
                src_ref=y_ref.at[2 * px + py], dst_ref=out_ref.at[2 * x + y if outgoing else 2 * px + py],
                send_sem=send.at[t, k - 1], recv_sem=recv.at[t, k - 1], device_id=(px, py, c),
                device_id_type=pl.DeviceIdType.MESH))
        return copies

    def start(self, ins, outs, scr):
        for t, (y_ref, out_ref) in enumerate(zip(ins, outs)):
            for give in self._copies(t, y_ref, out_ref, scr, True):
                give.start()

    def middle(self, ins, outs, scr):
        pass

    def finish(self, ins, outs, scr):
        for t, (y_ref, out_ref) in enumerate(zip(ins, outs)):
            for take in self._copies(t, y_ref, out_ref, scr, False):
                take.wait_recv()
            for give in self._copies(t, y_ref, out_ref, scr, True):
                give.wait_send()

    def set_results(self, res):
        self.results = list(res)


class Joint:
    def __init__(self, parts):
        self.parts = list(parts)
        self.operands = [o for p in parts for o in p.operands]
        self.out_shape = [o for p in parts for o in p.out_shape]
        self.scratch = [o for p in parts for o in p.scratch]

    def _split(self, ins, outs, scr):
        i = o = s = 0
        for p in self.parts:
            ni, no, ns = len(p.operands), len(p.out_shape), len(p.scratch)
            yield p, ins[i:i + ni], outs[o:o + no], scr[s:s + ns]
            i, o, s = i + ni, o + no, s + ns

    def start(self, ins, outs, scr):
        for p, a, b, c in self._split(ins, outs, scr):
            p.start(a, b, c)

    def middle(self, ins, outs, scr):
        for p, a, b, c in self._split(ins, outs, scr):
            p.middle(a, b, c)

    def finish(self, ins, outs, scr):
        for p, a, b, c in self._split(ins, outs, scr):
            p.finish(a, b, c)

    def set_results(self, res):
        for p, _, part, _ in self._split([], list(res), []):
            p.set_results(part)


_HBM = pl.BlockSpec(memory_space=pltpu.HBM)


def run_alone(name, carry):
    n_in, n_out = len(carry.operands), len(carry.out_shape)

    def body(*refs):
        ins, outs, scr = refs[:n_in], refs[n_in:n_in + n_out], refs[n_in + n_out:]
        carry.start(ins, outs, scr)
        carry.middle(ins, outs, scr)
        carry.finish(ins, outs, scr)

    res = pl.pallas_call(body, name=name, in_specs=[_HBM] * n_in, out_specs=[_HBM] * n_out,
                         out_shape=carry.out_shape, scratch_shapes=carry.scratch)(*carry.operands)
    carry.set_results(list(res))


def _call(name, body, *, grid, in_specs, out_specs, out_shape, args, scratch=(), carry=None):
    in_specs, out_specs, out_shape, scratch = list(in_specs), list(out_specs), list(out_shape), list(scratch)
    if carry is None:
        return list(pl.pallas_call(body, name=name, grid=grid, in_specs=in_specs, out_specs=out_specs,
                                   out_shape=out_shape, scratch_shapes=scratch, compiler_params=_cparams())(*args))
    n_in, n_out, n_scr = len(in_specs), len(out_shape), len(scratch)
    c_in, c_out = len(carry.operands), len(carry.out_shape)

    def wrapped(*refs):
        ins, c_ins = refs[:n_in], refs[n_in:n_in + c_in]
        at = n_in + c_in
        outs, c_outs = refs[at:at + n_out], refs[at + n_out:at + n_out + c_out]
        at += n_out + c_out
        scr, c_scr = refs[at:at + n_scr], refs[at + n_scr:]
        first = functools.reduce(jnp.logical_and, [pl.program_id(d) == 0 for d in range(len(grid))])
        last = functools.reduce(jnp.logical_and, [pl.program_id(d) == grid[d] - 1 for d in range(len(grid))])

        @pl.when(first)
        def _():
            carry.start(c_ins, c_outs, c_scr)

        body(*ins, *outs, *scr)

        step, total = 0, 1
        for d in range(len(grid)):
            step, total = step * grid[d] + pl.program_id(d), total * grid[d]

        @pl.when(step == total // 2)
        def _():
            carry.middle(c_ins, c_outs, c_scr)

        @pl.when(last)
        def _():
            carry.finish(c_ins, c_outs, c_scr)

    res = pl.pallas_call(
        wrapped, name=name, grid=grid, in_specs=in_specs + [_HBM] * c_in, out_specs=out_specs + [_HBM] * c_out,
        out_shape=out_shape + carry.out_shape, scratch_shapes=scratch + carry.scratch, compiler_params=_cparams(),
    )(*args, *carry.operands)
    carry.set_results(list(res[n_out:]))
    return list(res[:n_out])


def R_(arr, w=None, off=0, stride=0):
    return ('r', arr, arr.shape[1] if w is None else w, off, stride)


def P_(arr, w=None, off=0, stride=0):
    return ('p', arr, arr.shape[1] if w is None else w, off, stride)


def rowmap(name, fn, ins, outs=None, *, tile, ncol=1, rows_inner=True, cts=None, wrt=None, gdt=None, cat=False,
           carry=None):
    rows = next(a.shape[0] for k, a, *_ in ins if k == 'r')
    nrow = rows // tile
    assert nrow * tile == rows
    grid = (ncol, nrow) if rows_inner else (nrow, ncol)

    def ij(g0, g1):
        return (g1, g0) if rows_inner else (g0, g1)

    def spec(kind, arr, w, off, stride):
        if kind == 'r':
            return pl.BlockSpec((tile, w), lambda g0, g1: (ij(g0, g1)[0], off + stride * ij(g0, g1)[1]))
        return pl.BlockSpec((arr.shape[0], w), lambda g0, g1: (0, off + stride * ij(g0, g1)[1]))

    ops = list(ins) + list(cts or [])
    in_specs = [spec(*o) for o in ops]
    n_in = len(ins)
    fwd = cts is None
    out_shape, out_specs, acc = [], [], []
    if fwd:
        for dt, w, stride in outs:
            out_shape.append(jax.ShapeDtypeStruct((rows, w * (ncol if stride else 1)), dt))
            out_specs.append(spec('r', None, w, 0, stride))
            acc.append(None)
    elif cat:
        widths = [ins[i][2] for i in wrt]
        assert ncol == 1 and all(ins[i][0] == 'r' for i in wrt)
        out_shape.append(jax.ShapeDtypeStruct((rows, sum(widths)), gdt))
        out_specs.append(spec('r', None, sum(widths), 0, 0))
    else:
        for n, i in enumerate(wrt):
            kind, arr, w, off, stride = ins[i]
            width = w * (ncol if stride else 1)
            if kind == 'r':
                out_shape.append(jax.ShapeDtypeStruct((rows, width), gdt[n]))
                out_specs.append(spec('r', None, w, 0, 1 if stride else 0))
                shared = stride == 0 and ncol > 1
                assert not shared or (not rows_inner and gdt[n] == F32)
                acc.append('col' if shared else None)
            else:
                out_shape.append(jax.ShapeDtypeStruct((arr.shape[0], width), F32))
                out_specs.append(spec('p', arr, w, 0, 1 if stride else 0))
                assert rows_inner or ncol == 1
                acc.append('row')

    def body(*refs):
        i, j = ij(pl.program_id(0), pl.program_id(1))
        vals = [r[...].astype(F32) for r in refs[:n_in]]
        out_refs = refs[len(ops):]
        if fwd:
            for o_ref, o in zip(out_refs, fn(*vals)):
                o_ref[...] = o.astype(o_ref.dtype)
            return

        def f(*d):
            full = list(vals)
            for n, idx in enumerate(wrt):
                full[idx] = d[n]
            return fn(*full)

        _, vjp = jax.vjp(f, *[vals[idx] for idx in wrt])
        grads = vjp(tuple(r[...].astype(F32) for r in refs[n_in:len(ops)]))
        if cat:
            o_ref, at = out_refs[0], 0
            for g in grads:
                o_ref[:, at:at + g.shape[1]] = g.astype(o_ref.dtype)
                at += g.shape[1]
            return
        for o_ref, g, a in zip(out_refs, grads, acc):
            if a is None:
                o_ref[...] = g.astype(o_ref.dtype)
            else:
                first = (i if a == 'row' else j) == 0

                @pl.when(first)
                def _(o_ref=o_ref):
                    o_ref[...] = jnp.zeros_like(o_ref)

                o_ref[...] += g

    return _call(name, body, grid=grid, in_specs=in_specs, out_specs=out_specs, out_shape=out_shape,
                 args=[o[1] for o in ops], carry=carry)


def _rms(x, g):
    return x * lax.rsqrt(jnp.mean(x * x, axis=-1, keepdims=True) + EPS) * g


_DIMS = {'nn': (((1,), (0,)), ((), ())), 'nt': (((1,), (1,)), ((), ())), 'tn': (((0,), (0,)), ((), ()))}


def mm(name, a, b, mode, *, jdim=None, out_dtype=F32, carry=None):
    a_list = list(a) if isinstance(a, (list, tuple)) else [a]
    b_list = list(b) if isinstance(b, (list, tuple)) else [b]
    a_order = ('k', 'm') if mode == 'tn' else ('m', 'k')
    b_order = ('n', 'k') if mode == 'nt' else ('k', 'n')
    size, blocks = {}, 1
    for arr, order in ((a_list[0], a_order), (b_list[0], b_order)):
        shape = arr.shape
        if jdim in order:
            blocks, shape = shape[0], shape[1:]
        for dname, extent in zip(order, shape):
            assert size.setdefault(dname, extent) == extent
    tile = {'m': _pick(size['m'], (1024, 512, 256, 128)), 'n': _pick(size['n'], (512, 256, 128)),
            'k': size['k'] if size['k'] <= 2048 else _pick(size['k'], (2304, 2048, 1024, 512, 256, 128))}
    if jdim is not None:
        tile[jdim] = size[jdim]
    if a_list[0].dtype == F32 and tile['k'] * tile['m'] > (1 << 20):
        tile['m'] = _pick(size['m'], (512, 256, 128))
    grid = tuple(blocks if d == jdim else size[d] // tile[d] for d in ('m', 'n', 'k'))
    nk = grid[2]

    def spec(order):
        shape = tuple(tile[d] for d in order)

        def imap(i, j, k):
            g = {'m': i, 'n': j, 'k': k}
            idx = tuple(0 if d == jdim else g[d] for d in order)
            return ((g[jdim],) + idx) if jdim in order else idx

        return pl.BlockSpec(((None,) + shape) if jdim in order else shape, imap)

    dims = _DIMS[mode]
    nt = len(a_list)

    def product(refs):
        acc = None
        for a_ref, b_ref in zip(refs[:nt], refs[nt:2 * nt]):
            p = lax.dot_general(a_ref[...].astype(BF16), b_ref[...].astype(BF16), dims, preferred_element_type=F32)
            acc = p if acc is None else acc + p
        return acc

    def body_once(*refs):
        refs[2 * nt][...] = product(refs).astype(refs[2 * nt].dtype)

    def body_acc(*refs):
        o_ref, acc_ref = refs[2 * nt], refs[2 * nt + 1]
        k = pl.program_id(2)

        @pl.when(k == 0)
        def _():
            acc_ref[...] = jnp.zeros_like(acc_ref)

        acc_ref[...] += product(refs)

        @pl.when(k == nk - 1)
        def _():
            o_ref[...] = acc_ref[...].astype(o_ref.dtype)

    out_dims = (size['m'], size['n'])
    out_shape = jax.ShapeDtypeStruct(((blocks,) + out_dims) if jdim in ('m', 'n') else out_dims, out_dtype)
    return _call(name, body_once if nk == 1 else body_acc, grid=grid,
                 in_specs=[spec(a_order)] * nt + [spec(b_order)] * nt, out_specs=[spec(('m', 'n'))],
                 out_shape=[out_shape], args=a_list + b_list,
                 scratch=[] if nk == 1 else [pltpu.VMEM((tile['m'], tile['n']), F32)], carry=carry)[0]


def _probs(q, k, i, tq, scale, causal):
    s = lax.dot_general(q, k, _DIMS['nt'], preferred_element_type=F32) * scale
    if causal:
        shape = s.shape
        q_chunk = jnp.right_shift(i * tq + lax.broadcasted_iota(jnp.int32, shape, 0), CHUNK_SHIFT)
        k_chunk = jnp.right_shift(lax.broadcasted_iota(jnp.int32, shape, 1), CHUNK_SHIFT)
        s = jnp.where(k_chunk <= q_chunk, s, -jnp.inf)
    e = jnp.exp(s - jnp.max(s, axis=-1, keepdims=True))
    return e, jnp.sum(e, axis=-1, keepdims=True)


def _per_prefix(work, i, tq, sq, sk, causal):
    if not causal:
        work(sk)
        return
    assert sq == sk and tq % CHUNK == 0
    for j in range(sq // tq):
        @pl.when(i == j)
        def _(j=j):
            work((j + 1) * tq)


def _attn_specs(tq, sk, dq, dv, koff, kstride, voff, vstride):
    return [pl.BlockSpec((tq, dq), lambda h, i: (i, h)),
            pl.BlockSpec((sk, dq), lambda h, i: (0, koff + kstride * h)),
            pl.BlockSpec((sk, dv), lambda h, i: (0, voff + vstride * h))]


def attn_fwd(name, q, k, v, *, heads, dq, dv, koff, kstride, voff, vstride, scale, causal, carry=None):
    sq, sk = q.shape[0], k.shape[0]
    tq = min(sq, 256)

    def body(q_ref, k_ref, v_ref, o_ref):
        i = pl.program_id(1)

        def work(keys):
            e, l = _probs(q_ref[...].astype(BF16), k_ref[0:keys, :].astype(BF16), i, tq, scale, causal)
            o = jnp.dot(e.astype(BF16), v_ref[0:keys, :].astype(BF16), preferred_element_type=F32)
            o_ref[...] = o / l

        _per_prefix(work, i, tq, sq, sk, causal)

    return _call(name, body, grid=(heads, sq // tq),
                 in_specs=_attn_specs(tq, sk, dq, dv, koff, kstride, voff, vstride),
                 out_specs=[pl.BlockSpec((tq, dv), lambda h, i: (i, h))],
                 out_shape=[jax.ShapeDtypeStruct((sq, heads * dv), F32)], args=[q, k, v], carry=carry)[0]


def attn_bwd(name, q, k, v, do, *, heads, dq, dv, koff, kstride, voff, vstride, scale, causal, carry=None):
    sq, sk = q.shape[0], k.shape[0]
    tq = min(sq, 256)

    def body(q_ref, k_ref, v_ref, do_ref, dq_ref, dk_ref, dv_ref):
        i = pl.program_id(1)

        @pl.when(i == 0)
        def _():
            dk_ref[...] = jnp.zeros_like(dk_ref)
            dv_ref[...] = jnp.zeros_like(dv_ref)

        def work(keys):
            qb, kb, vb = q_ref[...].astype(BF16), k_ref[0:keys, :].astype(BF16), v_ref[0:keys, :].astype(BF16)
            dob = do_ref[...].astype(BF16)
            e, l = _probs(qb, kb, i, tq, scale, causal)
            p = e / l
            dp = lax.dot_general(dob, vb, _DIMS['nt'], preferred_element_type=F32)
            ds = (p * (dp - jnp.sum(dp * p, axis=-1, keepdims=True)) * scale).astype(BF16)
            dv_ref[0:keys, :] += lax.dot_general(p.astype(BF16), dob, _DIMS['tn'], preferred_element_type=F32)
            dk_ref[0:keys, :] += lax.dot_general(ds, qb, _DIMS['tn'], preferred_element_type=F32)
            dq_ref[...] = jnp.dot(ds, kb, preferred_element_type=F32)

        _per_prefix(work, i, tq, sq, sk, causal)

    return _call(
        name, body, grid=(heads, sq // tq),
        in_specs=_attn_specs(tq, sk, dq, dv, koff, kstride, voff, vstride) + [pl.BlockSpec((tq, dv), lambda h, i: (i, h))],
        out_specs=[pl.BlockSpec((tq, dq), lambda h, i: (i, h)), pl.BlockSpec((sk, dq), lambda h, i: (0, h)),
                   pl.BlockSpec((sk, dv), lambda h, i: (0, h))],
        out_shape=[jax.ShapeDtypeStruct((sq, heads * dq), F32), jax.ShapeDtypeStruct((sk, heads * dq), F32),
                   jax.ShapeDtypeStruct((sk, heads * dv), F32)],
        args=[q, k, v, do], carry=carry)


def _hg_chunk(q, k, g, v, state):
    c = q.shape[0]
    row = lax.broadcasted_iota(jnp.int32, (c, c), 0)
    col = lax.broadcasted_iota(jnp.int32, (c, c), 1)
    tril = (col <= row).astype(F32)
    b = jnp.dot(tril, g, precision=lax.Precision.HIGHEST, preferred_element_type=F32)
    rows = lax.broadcasted_iota(jnp.int32, (c, 1), 0)
    o = jnp.dot((q * jnp.exp(b)).astype(BF16), state.astype(BF16), preferred_element_type=F32)
    t3 = lax.broadcasted_iota(jnp.int32, (SUB, SUB, 1), 0)
    s3 = lax.broadcasted_iota(jnp.int32, (SUB, SUB, 1), 1)
    parts = []
    for n in range(c // SUB):
        lo = n * SUB
        qn, kn, bn, vn = q[lo:lo + SUB], k[lo:lo + SUB], b[lo:lo + SUB], v[lo:lo + SUB]
        decay = jnp.exp(jnp.where(s3 <= t3, bn[:, None, :] - bn[None, :, :], -jnp.inf))
        sc = jnp.sum(qn[:, None, :] * kn[None, :, :] * decay, axis=-1)
        on = jnp.dot(sc.astype(BF16), vn.astype(BF16), preferred_element_type=F32)
        if n > 0:
            ref = jnp.sum(jnp.where(rows == lo - 1, b, 0.0), axis=0, keepdims=True)
            qd = (qn * jnp.exp(bn - ref)).astype(BF16)
            kd = (k[:lo] * jnp.exp(ref - b[:lo])).astype(BF16)
            so = lax.dot_general(qd, kd, _DIMS['nt'], preferred_element_type=F32)
            on = on + jnp.dot(so.astype(BF16), v[:lo].astype(BF16), preferred_element_type=F32)
        parts.append(on)
    o = o + jnp.concatenate(parts, axis=0)
    b_last = jnp.sum(g, axis=0, keepdims=True)
    ones = jnp.ones((c, 1), F32)
    b_last_col = lax.dot_general(g, ones, _DIMS['tn'], precision=lax.Precision.HIGHEST, preferred_element_type=F32)
    kd = (k * jnp.exp(b_last - b)).astype(BF16)
    new_state = jnp.exp(b_last_col) * state + lax.dot_general(kd, v.astype(BF16), _DIMS['tn'],
                                                              preferred_element_type=F32)
    return o, new_state


def _hg_chunk_bwd(q, k, g, v, state, do, dnew):
    c, kw = q.shape
    hi = lax.Precision.HIGHEST
    row = lax.broadcasted_iota(jnp.int32, (c, c), 0)
    col = lax.broadcasted_iota(jnp.int32, (c, c), 1)
    b = jnp.dot((col <= row).astype(F32), g, precision=hi, preferred_element_type=F32)
    rows = lax.broadcasted_iota(jnp.int32, (c, 1), 0)
    b_last = jnp.sum(g, axis=0, keepdims=True)
    b_last_col = lax.dot_general(g, jnp.ones((c, 1), F32), _DIMS['tn'], precision=hi, preferred_element_type=F32)
    eb, to_end = jnp.exp(b), jnp.exp(b_last - b)
    dob, vb, dnb = do.astype(BF16), v.astype(BF16), dnew.astype(BF16)
    k_end = (k * to_end).astype(BF16)
    dq = eb * lax.dot_general(dob, state.astype(BF16), _DIMS['nt'], preferred_element_type=F32)
    dk = to_end * lax.dot_general(vb, dnb, _DIMS['nt'], preferred_element_type=F32)
    dv = jnp.dot(k_end, dnb, preferred_element_type=F32)
    dstate = jnp.exp(b_last_col) * dnew + lax.dot_general((q * eb).astype(BF16), dob, _DIMS['tn'],
                                                          preferred_element_type=F32)
    new_state = jnp.exp(b_last_col) * state + lax.dot_general(k_end, vb, _DIMS['tn'], preferred_element_type=F32)
    db_end = lax.dot_general(jnp.ones((1, dnew.shape[1]), F32), dnew * new_state, _DIMS['nt'], precision=hi,
                             preferred_element_type=F32)
    t3 = lax.broadcasted_iota(jnp.int32, (SUB, SUB, 1), 0)
    s3 = lax.broadcasted_iota(jnp.int32, (SUB, SUB, 1), 1)
    dq_rows, dk_rows, dv_rows = [], [], []
    for n in range(c // SUB):
        lo = n * SUB
        qn, kn, bn, vn, don = q[lo:lo + SUB], k[lo:lo + SUB], b[lo:lo + SUB], vb[lo:lo + SUB], dob[lo:lo + SUB]
        decay = jnp.exp(jnp.where(s3 <= t3, bn[:, None, :] - bn[None, :, :], -jnp.inf))
        sc = jnp.sum(qn[:, None, :] * kn[None, :, :] * decay, axis=-1)
        pull = lax.dot_general(don, vn, _DIMS['nt'], preferred_element_type=F32)[:, :, None] * decay
        dqn = jnp.sum(pull * kn[None, :, :], axis=1)
        dk_rows.append(jnp.sum(pull * qn[:, None, :], axis=0))
        dv_rows.append(lax.dot_general(sc.astype(BF16), don, _DIMS['tn'], preferred_element_type=F32))
        if n > 0:
            ref = jnp.sum(jnp.where(rows == lo - 1, b, 0.0), axis=0, keepdims=True)
            up, down = jnp.exp(bn - ref), jnp.exp(ref - b[:lo])
            qd, kd = (qn * up).astype(BF16), (k[:lo] * down).astype(BF16)
            so = lax.dot_general(qd, kd, _DIMS['nt'], preferred_element_type=F32).astype(BF16)
            ao = lax.dot_general(don, vb[:lo], _DIMS['nt'], preferred_element_type=F32).astype(BF16)
            dqn = dqn + up * jnp.dot(ao, kd, preferred_element_type=F32)
            rest = jnp.zeros((c - lo, kw), F32)
            dk = dk + jnp.concatenate([down * lax.dot_general(ao, qd, _DIMS['tn'], preferred_element_type=F32),
                                       rest], axis=0)
            dv = dv + jnp.concatenate([lax.dot_general(so, don, _DIMS['tn'], preferred_element_type=F32), rest],
                                      axis=0)
        dq_rows.append(dqn)
    dq = dq + jnp.concatenate(dq_rows, axis=0)
    dk = dk + jnp.concatenate(dk_rows, axis=0)
    dv = dv + jnp.concatenate(dv_rows, axis=0)
    db = q * dq - k * dk + jnp.where(rows == c - 1, db_end, 0.0)
    dg = jnp.dot((col >= row).astype(F32), db, precision=hi, preferred_element_type=F32)
    return dq, dk, dg, dv, dstate


def hg_scan_fwd(name, q, k, g, u, v_off, carry=None):
    s = q.shape[0]
    n = s // CHUNK

    def body(q_ref, k_ref, g_ref, v_ref, o_ref, st_ref, state):
        @pl.when(pl.program_id(1) == 0)
        def _():
            state[...] = jnp.zeros_like(state)

        for j in range(HG_GROUP):
            cols = slice(j * HEAD_W, (j + 1) * HEAD_W)
            st = state[j]
            st_ref[j] = st
            o, new = _hg_chunk(q_ref[:, cols], k_ref[:, cols], g_ref[:, cols], v_ref[:, cols], st)
            o_ref[:, cols] = o
            state[j] = new

    wide = HG_GROUP * HEAD_W
    blk = pl.BlockSpec((CHUNK, wide), lambda h, c: (c, h))
    return _call(
        name, body, grid=(HG_HEADS // HG_GROUP, n),
        in_specs=[blk, blk, blk, pl.BlockSpec((CHUNK, wide), lambda h, c: (c, v_off // HG_GROUP + h))],
        out_specs=[blk, pl.BlockSpec((HG_GROUP, None, HEAD_W, HEAD_W), lambda h, c: (h, c, 0, 0))],
        out_shape=[jax.ShapeDtypeStruct((s, HG_HEADS * HEAD_W), F32),
                   jax.ShapeDtypeStruct((HG_HEADS, n, HEAD_W, HEAD_W), F32)],
        scratch=[pltpu.VMEM((HG_GROUP, HEAD_W, HEAD_W), F32)], args=[q, k, g, u], carry=carry)


def hg_scan_bwd(name, q, k, g, u, v_off, states, do, carry=None):
    s = q.shape[0]
    n = s // CHUNK

    def body(q_ref, k_ref, g_ref, v_ref, st_ref, do_ref, dq_ref, dk_ref, dg_ref, dv_ref, dstate):
        @pl.when(pl.program_id(1) == 0)
        def _():
            dstate[...] = jnp.zeros_like(dstate)

        for j in range(HG_GROUP):
            cols = slice(j * HEAD_W, (j + 1) * HEAD_W)
            dq, dk, dg, dv, dst = _hg_chunk_bwd(q_ref[:, cols], k_ref[:, cols], g_ref[:, cols], v_ref[:, cols],
                                                st_ref[j], do_ref[:, cols], dstate[j])
            dq_ref[:, cols] = dq
            dk_ref[:, cols] = dk
            dg_ref[:, cols] = dg
            dv_ref[:, cols] = dv
            dstate[j] = dst

    wide = HG_GROUP * HEAD_W
    blk = pl.BlockSpec((CHUNK, wide), lambda h, c: (n - 1 - c, h))
    out = jax.ShapeDtypeStruct((s, HG_HEADS * HEAD_W), F32)
    return _call(
        name, body, grid=(HG_HEADS // HG_GROUP, n),
        in_specs=[blk, blk, blk, pl.BlockSpec((CHUNK, wide), lambda h, c: (n - 1 - c, v_off // HG_GROUP + h)),
                  pl.BlockSpec((HG_GROUP, None, HEAD_W, HEAD_W), lambda h, c: (h, n - 1 - c, 0, 0)), blk],
        out_specs=[blk, blk, blk, blk], out_shape=[out, out, out, out],
        scratch=[pltpu.VMEM((HG_GROUP, HEAD_W, HEAD_W), F32)], args=[q, k, g, u, states, do], carry=carry)


def pair_sum(name, blocks, theirs):
    _, rows, cols = theirs.shape
    tile = next((t for t in (1024, 512, 256, 128, 64, 32, 16) if rows % t == 0 and t * cols <= (1 << 20)), rows)

    def body(a_ref, b_ref, o_ref):
        mine = jnp.where(lax.axis_index("c") == 0, a_ref[0].astype(F32), a_ref[1].astype(F32))
        o_ref[...] = (mine + b_ref[...].astype(F32)).astype(o_ref.dtype)

    blk = pl.BlockSpec((None, tile, cols), lambda q, i: (q, i, 0))
    return _call(name, body, grid=(N_CHIP, rows // tile),
                 in_specs=[pl.BlockSpec((None, 2, tile, cols), lambda q, i: (q, 0, i, 0)), blk], out_specs=[blk],
                 out_shape=[jax.ShapeDtypeStruct(theirs.shape, theirs.dtype)],
                 args=[blocks.reshape(N_CHIP, 2, rows, cols), theirs])[0]


def adamw(name, landed, w, m, v):
    rows, cols = w.shape
    slots = landed.shape[0]
    tile = next((t for t in (512, 256, 128, 64, 32, 16, 8) if rows % t == 0 and t * cols <= (1 << 18)), rows)

    def body(l_ref, w_ref, m_ref, v_ref, g_ref, d_ref, nm_ref, nv_ref):
        gv = l_ref[0].astype(F32)
        for s in range(1, slots):
            gv = gv + l_ref[s].astype(F32)
        nm = ADAM_B1 * m_ref[...] + (1.0 - ADAM_B1) * gv
        nv = ADAM_B2 * v_ref[...] + (1.0 - ADAM_B2) * jnp.square(gv)
        m_hat = nm / (1.0 - ADAM_B1 ** ADAM_STEP)
        v_hat = nv / (1.0 - ADAM_B2 ** ADAM_STEP)
        g_ref[...] = gv
        d_ref[...] = -ADAM_LR * (m_hat / (jnp.sqrt(v_hat) + ADAM_EPS) + ADAM_WD * w_ref[...])
        nm_ref[...] = nm
        nv_ref[...] = nv

    blk = pl.BlockSpec((tile, cols), lambda i: (i, 0))
    out = jax.ShapeDtypeStruct((rows, cols), F32)
    return _call(name, body, grid=(rows // tile,),
                 in_specs=[pl.BlockSpec((slots, tile, cols), lambda i: (0, i, 0)), blk, blk, blk],
                 out_specs=[blk] * 4, out_shape=[out] * 4, args=[landed, w, m, v])


def _swap_halves(pe):
    half = QK_ROPE // 2
    return jnp.concatenate([-pe[..., half:], pe[..., :half]], axis=-1)


def _unswap_halves(dsw):
    half = QK_ROPE // 2
    return jnp.concatenate([dsw[..., half:], -dsw[..., :half]], axis=-1)


def _w_in_ext(w):
    kpe = w[:, 9216:9280]
    z = jnp.zeros((w.shape[0], HEAD_W - QK_ROPE), w.dtype)
    pad = jnp.zeros((w.shape[0], U_PAD), w.dtype)
    return jnp.concatenate([w[:, :9216], w[:, 9280:], kpe, z, _swap_halves(kpe), z, pad], axis=1)


def _w_in_grad(d):
    dkpe = d[:, 13312:13376] + _unswap_halves(d[:, 13440:13504])
    return jnp.concatenate([d[:, :9216], dkpe, d[:, 9216:13312]], axis=1)


def _w_q_ext(w):
    w3 = w.reshape(Q_LORA, MLA_HEADS, MLA_QK)
    pe = w3[:, :, HEAD_W:]
    z = jnp.zeros((Q_LORA, MLA_HEADS, HEAD_W - QK_ROPE), w.dtype)
    wide = MLA_HEADS * HEAD_W
    return jnp.concatenate([w3[:, :, :HEAD_W].reshape(Q_LORA, wide),
                            jnp.concatenate([pe, z], axis=2).reshape(Q_LORA, wide),
                            jnp.concatenate([_swap_halves(pe), z], axis=2).reshape(Q_LORA, wide)], axis=1)


def _w_q_grad(parts):
    d3 = [p.reshape(Q_LORA, MLA_HEADS, HEAD_W) for p in parts]
    dpe = d3[1][:, :, :QK_ROPE] + _unswap_halves(d3[2][:, :, :QK_ROPE])
    return jnp.concatenate([d3[0], dpe], axis=2).reshape(Q_LORA, MLA_HEADS * MLA_QK)


def _hg_prep(f_raw, q_hg, logits):
    lb = jax.nn.softmax(logits, axis=0)[0:1, :]
    log_f = jnp.logaddexp(jnp.log(lb), jnp.log1p(-lb) + jax.nn.log_sigmoid(f_raw))
    k_in = (1.0 - lb) * jax.nn.sigmoid(-f_raw)
    return log_f, k_in, jax.nn.silu(q_hg)


def _rope(q_nope, q_pe, q_sw, k_nope, k_pe, k_sw, cos, sin):
    qf = jnp.concatenate([q_nope, q_pe * cos + q_sw * sin], axis=1)
    kf = jnp.concatenate([k_nope, k_pe * cos + k_sw * sin], axis=1)
    return qf, kf


def _step(a):
    x, mem, target = a['x'][0], a['mem'][0], a['loss_target'][0]
    s, d = x.shape
    nm = mem.shape[0]
    ff = N_DEV * a['ffn1_w_gate'].shape[-1]
    cs = ff // N_DEV
    tr = min(s, 256)
    tp = min(s, 128)
    th = s
    ta = 1024

    bf = {n: a[n][0].astype(BF16) for n in BIG}
    half_in = bf['w_in'].shape[0] // 2
    bf['w_in#0'], bf['w_in#1'] = bf['w_in'][:half_in], bf['w_in'][half_in:]
    gat, full = {}, {}

    my_chip = 2 * lax.axis_index("x") + lax.axis_index("y")
    me = 2 * my_chip + lax.axis_index("c")

    def gathered(names, carry):
        for n, g8 in zip(names or [], carry.results if carry else []):
            r, c = bf[n].shape
            g8 = lax.dynamic_update_index_in_dim(g8, bf[n], me, 0)
            gat[n] = g8
            if not n.startswith('ffn'):
                by_col = BIG[n.split('#')[0]]
                full[n] = g8.transpose(1, 0, 2).reshape(r, N_DEV * c) if by_col else g8.reshape(N_DEV * r, c)

    xa_names = ['xa_w_q', 'xa_w_k', 'xa_w_v', 'xa_w_o']
    ffn2_names = ['ffn2_w_gate', 'ffn2_w_up', 'ffn2_w_down']
    first = Gather([bf['ffn1_w_gate']])
    run_alone("gather_first", first)
    gathered(['ffn1_w_gate'], first)

    inv_freq = 1.0 / (ROPE_THETA ** (jnp.arange(0, QK_ROPE, 2, dtype=F32) / QK_ROPE))
    ang = a['positions'][0].astype(F32)[:, None] * inv_freq
    zero = jnp.zeros((s, HEAD_W - QK_ROPE), F32)
    cos = jnp.concatenate([jnp.cos(ang), jnp.cos(ang), zero], axis=1)
    sin = jnp.concatenate([jnp.sin(ang), jnp.sin(ang), zero], axis=1)

    gs = {}
    gb = {}

    def by_rows(g):
        return g.reshape(N_DEV, g.shape[0] // N_DEV, g.shape[1])

    def by_cols(g):
        return g.reshape(g.shape[0], N_DEV, g.shape[1] // N_DEV).transpose(1, 0, 2)

    pre_fn = lambda xv, g: (_rms(xv, g),)
    pre_res_fn = lambda xv, g: (_rms(xv, g), xv)

    def pre_norm(tag, x_in, g):
        return rowmap(tag + "_pre", pre_fn, [R_(x_in), P_(g)], [(BF16, x_in.shape[1], 0)], tile=tr)[0]

    def pre_norm_bwd(tag, x_in, g, dh, d_out, carry=None):
        return rowmap(tag + "_pre_bwd", pre_res_fn, [R_(x_in), P_(g)], tile=tr, cts=[R_(dh), R_(d_out)],
                      wrt=[0, 1], gdt=[F32, F32], carry=carry)

    def post_fn(weight):
        return lambda xv, y, g: (xv + weight * _rms(y, g),)

    def post_norm(tag, x_in, y, g, weight):
        return rowmap(tag + "_post", post_fn(weight), [R_(x_in), R_(y), P_(g)], [(F32, d, 0)], tile=tr)[0]

    def post_norm_bwd(tag, x_in, y, g, weight, d_out):
        return rowmap(tag + "_post_bwd", post_fn(weight), [R_(x_in), R_(y), P_(g)], tile=tr, cts=[R_(d_out)],
                      wrt=[1, 2], gdt=[BF16, F32])

    act_fn = lambda av, bv: (jax.nn.silu(av) * bv,)

    def fetch(plan, key):
        names = plan.get(key)
        return (names, Gather([bf[n] for n in names])) if names else (None, None)

    def send(plan, key):
        acts = plan.get(key)
        if not acts:
            return None, None
        parts = []
        for kind, names in acts:
            if kind == 'spread':
                for n in names:
                    chip_sums[n] = pair_sum("pair_" + n.replace('#', '_'), gb[n], halves[n])
            parts.append(Halve([gb[n] for n in names]) if kind == 'halve' else Spread([chip_sums[n] for n in names]))
        return acts, Joint(parts)

    def ffn_fwd(tag, x_in, plan):
        h = pre_norm(tag, x_in, a[tag + '_pre_g'])
        names, carry = fetch(plan, 'gate')
        av = mm(tag + "_gate", h, gat[tag + '_w_gate'], 'nn', jdim='n', out_dtype=BF16, carry=carry)
        gathered(names, carry)
        names, carry = fetch(plan, 'up')
        bv = mm(tag + "_up", h, gat[tag + '_w_up'], 'nn', jdim='n', out_dtype=BF16, carry=carry)
        gathered(names, carry)
        a2, b2 = av.reshape(N_DEV * s, cs), bv.reshape(N_DEV * s, cs)
        names, carry = fetch(plan, 'act')
        z = rowmap(tag + "_act", act_fn, [R_(a2), R_(b2)], [(BF16, cs, 0)], tile=ta, carry=carry)[0]
        z = z.reshape(N_DEV, s, cs)
        gathered(names, carry)
        names, carry = fetch(plan, 'dn')
        y = mm(tag + "_dn", z, gat[tag + '_w_down'], 'nn', jdim='k', carry=carry)
        gathered(names, carry)
        return post_norm(tag, x_in, y, a[tag + '_post_g'], 0.5), (x_in, h, a2, b2, z, y)

    def ffn_bwd(tag, res, d_out, plan):
        x_in, h, a2, b2, z, y = res
        dy, gs[tag + '_post_g'] = post_norm_bwd(tag, x_in, y, a[tag + '_post_g'], 0.5, d_out)
        acts, carry = send(plan, 'dn_dx')
        dz = mm(tag + "_dn_dx", dy, gat[tag + '_w_down'], 'nt', jdim='n', carry=carry)
        exchanged(acts, carry)
        acts, carry = send(plan, 'dn_dw')
        gb[tag + '_w_down'] = mm(tag + "_dn_dw", z, dy, 'tn', jdim='m', out_dtype=BF16, carry=carry)
        exchanged(acts, carry)
        acts, carry = send(plan, 'act_bwd')
        da, db = rowmap(tag + "_act_bwd", act_fn, [R_(a2), R_(b2)], tile=ta, cts=[R_(dz.reshape(N_DEV * s, cs))],
                        wrt=[0, 1], gdt=[BF16, BF16], carry=carry)
        exchanged(acts, carry)
        da, db = da.reshape(N_DEV, s, cs), db.reshape(N_DEV, s, cs)
        acts, carry = send(plan, 'gate_dw')
        gb[tag + '_w_gate'] = mm(tag + "_gate_dw", h, da, 'tn', jdim='n', out_dtype=BF16, carry=carry)
        exchanged(acts, carry)
        acts, carry = send(plan, 'up_dw')
        gb[tag + '_w_up'] = mm(tag + "_up_dw", h, db, 'tn', jdim='n', out_dtype=BF16, carry=carry)
        exchanged(acts, carry)
        acts, carry = send(plan, 'gu_dx')
        dh = mm(tag + "_gu_dx", [da, db], [gat[tag + '_w_gate'], gat[tag + '_w_up']], 'nt', jdim='k', carry=carry)
        exchanged(acts, carry)
        names, carry = send(plan, 'pre_bwd')
        d_in, gs[tag + '_pre_g'] = pre_norm_bwd(tag, x_in, a[tag + '_pre_g'], dh, d_out, carry)
        exchanged(names, carry)
        return d_in

    hg_out_fn = lambda o, og, g: (_rms(o, g) * jax.nn.silu(og),)
    mla_norm_fn = lambda cq, ckv, gq, gkv: (_rms(cq, gq), _rms(ckv, gkv))
    gate_fn = lambda ga, gb, ya, yb: (jax.nn.sigmoid(ga) * ya + jax.nn.sigmoid(gb) * yb,)
    mla = dict(heads=MLA_HEADS, dq=2 * HEAD_W, dv=HEAD_W, koff=0, kstride=1, voff=1, vstride=2,
               scale=MLA_QK ** -0.5, causal=True)

    def mix_fwd(x_in, plan):
        w_in = _w_in_ext(jnp.concatenate([full['w_in#0'], full['w_in#1']], axis=0))
        h = pre_norm("mix", x_in, a['mix_pre_g'])
        names, carry = fetch(plan, 'in')
        u = mm("mix_in", h, w_in, 'nn', carry=carry)
        gathered(names, carry)
        w_q, w_kv = _w_q_ext(full['mla_w_q_up']), full['mla_w_kv_up']
        hg_ins = [R_(u, 2048, 1), R_(u, 2048, 0), P_(a['hgrn_lb_logits'])]
        log_f, k_in, q_in = rowmap("hg_prep", _hg_prep, hg_ins, [(F32, 2048, 0)] * 3, tile=tp)
        names, carry = fetch(plan, 'scan')
        o_a, states = hg_scan_fwd("hg_scan", q_in, k_in, log_f, u, 32, carry=carry)
        gathered(names, carry)
        out_ins = [R_(o_a, HEAD_W, 0, 1), R_(u, HEAD_W, 48, 1), P_(a['hg_norm_g'], HEAD_W, 0, 1)]
        oag = rowmap("hg_out", hg_out_fn, out_ins, [(BF16, HEAD_W, 1)], tile=th, ncol=HG_HEADS)[0]
        y_a = mm("mix_a", oag, full['w_branch_a'], 'nn')
        norm_ins = [R_(u, 512, 16), R_(u, 512, 17), P_(a['mla_q_norm_g']), P_(a['mla_kv_norm_g'])]
        cqn, ckvn = rowmap("mla_norm", mla_norm_fn, norm_ins, [(BF16, 512, 0)] * 2, tile=tr)
        q_all = mm("mla_qup", cqn, w_q, 'nn')
        kv = mm("mla_kvup", ckvn, w_kv, 'nn')
        rope_ins = [R_(q_all, HEAD_W, 0, 1), R_(q_all, HEAD_W, 16, 1), R_(q_all, HEAD_W, 32, 1),
                    R_(kv, HEAD_W, 0, 2), R_(u, HEAD_W, 104, 0), R_(u, HEAD_W, 105, 0), R_(cos), R_(sin)]
        qf, kf = rowmap("mla_rope", _rope, rope_ins, [(BF16, 2 * HEAD_W, 1)] * 2, tile=th, ncol=MLA_HEADS,
                        rows_inner=False)
        names, carry = fetch(plan, 'attn')
        o_b = attn_fwd("mla_attn", qf, kf, kv, carry=carry, **mla)
        gathered(names, carry)
        y_b = mm("mix_b", o_b, full['w_branch_b'], 'nn')
        gate_ins = [R_(u, 1024, 9, 1), R_(u, 1024, 11, 1), R_(y_a, 1024, 0, 1), R_(y_b, 1024, 0, 1)]
        y = rowmap("mix_gate", gate_fn, gate_ins, [(BF16, 1024, 1)], tile=tr, ncol=2)[0]
        yo = mm("mix_out", y, full['w_out'], 'nn')
        res = (x_in, h, hg_ins, q_in, k_in, log_f, u, states, out_ins, oag, norm_ins, cqn, ckvn, rope_ins, qf, kf,
               kv, o_b, gate_ins, y, yo, w_in, w_q, w_kv)
        return post_norm("mix", x_in, yo, a['mix_post_g'], 1.0), res

    def mix_bwd(res, d_out, plan):
        (x_in, h, hg_ins, q_in, k_in, log_f, u, states, out_ins, oag, norm_ins, cqn, ckvn, rope_ins, qf, kf, kv,
         o_b, gate_ins, y, yo, w_in, w_q, w_kv) = res
        dyo, gs['mix_post_g'] = post_norm_bwd("mix", x_in, yo, a['mix_post_g'], 1.0, d_out)
        dy = mm("mix_out_dx", dyo, full['w_out'], 'nt')
        gb['w_out'] = by_rows(mm("mix_out_dw", y, dyo, 'tn', out_dtype=BF16))
        dga, dgb, dya, dyb = rowmap("mix_gate_bwd", gate_fn, gate_ins, tile=tr, ncol=2, cts=[R_(dy, 1024, 0, 1)],
                                    wrt=[0, 1, 2, 3], gdt=[BF16] * 4)
        gb['w_branch_b'] = by_rows(mm("mix_b_dw", o_b, dyb, 'tn', out_dtype=BF16))
        do_b = mm("mix_b_dx", dyb, full['w_branch_b'], 'nt')
        names, carry = send(plan, 'attn_bwd')
        dqf, dkf, dv = attn_bwd("mla_attn_bwd", qf, kf, kv, do_b, carry=carry, **mla)
        exchanged(names, carry)
        dqn, dqp, dqs, dkn, dkpe, dksw = rowmap(
            "mla_rope_bwd", _rope, rope_ins, tile=th, ncol=MLA_HEADS, rows_inner=False,
            cts=[R_(dqf, 2 * HEAD_W, 0, 1), R_(dkf, 2 * HEAD_W, 0, 1)], wrt=[0, 1, 2, 3, 4, 5],
            gdt=[BF16, BF16, BF16, BF16, F32, F32])
        wide = MLA_HEADS * HEAD_W
        dq_parts = [dqn, dqp, dqs]
        gb['mla_w_q_up'] = by_cols(_w_q_grad([mm("mla_qup_dw%d" % n, cqn, dq_parts[n], 'tn', out_dtype=BF16)
                                              for n in range(3)]))
        dcqn = mm("mla_qup_dx", dq_parts, [w_q[:, n * wide:(n + 1) * wide] for n in range(3)], 'nt')
        w_kv4 = w_kv.reshape(KV_LORA, MLA_HEADS, 2, HEAD_W)
        dw_k = mm("mla_kup_dw", ckvn, dkn, 'tn', out_dtype=BF16).reshape(KV_LORA, MLA_HEADS, 1, HEAD_W)
        dw_v = mm("mla_vup_dw", ckvn, dv, 'tn', out_dtype=BF16).reshape(KV_LORA, MLA_HEADS, 1, HEAD_W)
        gb['mla_w_kv_up'] = by_cols(jnp.concatenate([dw_k, dw_v], axis=2).reshape(KV_LORA, 2 * wide))
        dckvn = mm("mla_kvup_dx", [dkn, dv],
                   [w_kv4[:, :, 0].reshape(KV_LORA, wide), w_kv4[:, :, 1].reshape(KV_LORA, wide)], 'nt')
        dcq, dckv, gs['mla_q_norm_g'], gs['mla_kv_norm_g'] = rowmap(
            "mla_norm_bwd", mla_norm_fn, norm_ins, tile=tr, cts=[R_(dcqn), R_(dckvn)], wrt=[0, 1, 2, 3],
            gdt=[BF16, BF16, F32, F32])
        gb['w_branch_a'] = by_rows(mm("mix_a_dw", oag, dya, 'tn', out_dtype=BF16))
        doag = mm("mix_a_dx", dya, full['w_branch_a'], 'nt')
        acts, carry = send(plan, 'out_bwd')
        do_a, dog, gs['hg_norm_g'] = rowmap("hg_out_bwd", hg_out_fn, out_ins, tile=th, ncol=HG_HEADS,
                                            cts=[R_(doag, HEAD_W, 0, 1)], wrt=[0, 1, 2], gdt=[F32, BF16, F32],
                                            carry=carry)
        exchanged(acts, carry)
        names, carry = send(plan, 'scan_bwd')
        dq_in, dk_in, dlog_f, di = hg_scan_bwd("hg_scan_bwd", q_in, k_in, log_f, u, 32, states, do_a, carry=carry)
        exchanged(names, carry)
        df, dq_hg, gs['hgrn_lb_logits'] = rowmap("hg_prep_bwd", _hg_prep, hg_ins, tile=tp,
                                                 cts=[R_(dlog_f), R_(dk_in), R_(dq_in)], wrt=[0, 1, 2],
                                                 gdt=[BF16, BF16, F32])
        du = jnp.concatenate([dq_hg, df, di.astype(BF16), dog, dcq, dckv, dga, dgb, dkpe.astype(BF16),
                              dksw.astype(BF16), jnp.zeros((s, U_PAD), BF16)], axis=1)
        g_in = by_cols(_w_in_grad(mm("mix_in_dw", h, du, 'tn', out_dtype=BF16)))
        gb['w_in#0'], gb['w_in#1'] = g_in[:, :half_in], g_in[:, half_in:]
        names, carry = send(plan, 'in_dx')
        dh = mm("mix_in_dx", du, w_in, 'nt', carry=carry)
        exchanged(names, carry)
        d_in, gs['mix_pre_g'] = pre_norm_bwd("mix", x_in, a['mix_pre_g'], dh, d_out)
        return d_in

    xa = dict(heads=XA_HEADS, dq=HEAD_W, dv=HEAD_W, koff=0, kstride=1, voff=XA_HEADS, vstride=1,
              scale=HEAD_W ** -0.5, causal=False)
    tm_ = min(nm, 128)

    def xa_fwd(x_in):
        w_xkv = jnp.concatenate([full['xa_w_k'], full['xa_w_v']], axis=1)
        h = pre_norm("xa", x_in, a['xa_pre_g'])
        mn = rowmap("xa_mem", pre_fn, [R_(mem), P_(a['xa_mem_g'])], [(BF16, d, 0)], tile=tm_)[0]
        q = mm("xa_q", h, full['xa_w_q'], 'nn')
        kv = mm("xa_kv", mn, w_xkv, 'nn')
        o = attn_fwd("xa_attn", q, kv, kv, **xa)
        yo = mm("xa_o", o, full['xa_w_o'], 'nn')
        return post_norm("xa", x_in, yo, a['xa_post_g'], 1.0), (x_in, h, mn, q, kv, o, yo, w_xkv)

    def xa_bwd(res, d_out, plan):
        x_in, h, mn, q, kv, o, yo, w_xkv = res
        dyo, gs['xa_post_g'] = post_norm_bwd("xa", x_in, yo, a['xa_post_g'], 1.0, d_out)
        do = mm("xa_o_dx", dyo, full['xa_w_o'], 'nt')
        gb['xa_w_o'] = by_cols(mm("xa_o_dw", o, dyo, 'tn', out_dtype=BF16))
        dq, dk, dv = attn_bwd("xa_attn_bwd", q, kv, kv, do, **xa)
        dkv = jnp.concatenate([dk, dv], axis=1).astype(BF16)
        dw = mm("xa_kv_dw", mn, dkv, 'tn', out_dtype=BF16)
        gb['xa_w_k'], gb['xa_w_v'] = by_rows(dw[:, :XA_HEADS * HEAD_W]), by_rows(dw[:, XA_HEADS * HEAD_W:])
        dmn = mm("xa_kv_dx", dkv, w_xkv, 'nt')
        gs['xa_mem_g'] = rowmap("xa_mem_bwd", pre_fn, [R_(mem), P_(a['xa_mem_g'])], tile=tm_, cts=[R_(dmn)],
                                wrt=[1], gdt=[F32])[0]
        gb['xa_w_q'] = by_rows(mm("xa_q_dw", h, dq, 'tn', out_dtype=BF16))
        dh = mm("xa_q_dx", dq, full['xa_w_q'], 'nt')
        acts, carry = send(plan, 'pre_bwd')
        d_in, gs['xa_pre_g'] = pre_norm_bwd("xa", x_in, a['xa_pre_g'], dh, d_out, carry)
        exchanged(acts, carry)
        return d_in

    landed = {}

    halves = {}
    chip_sums = {}

    def exchanged(acts, carry):
        for (kind, names), part in zip(acts or [], carry.parts if carry else []):
            for n, got in zip(names, part.results):
                if kind == 'halve':
                    halves[n] = got
                else:
                    own = lax.dynamic_index_in_dim(chip_sums[n], my_chip, 0, keepdims=True)
                    landed[n] = lax.dynamic_update_slice_in_dim(got, own, my_chip, 0)

    x1, r1 = ffn_fwd('ffn1', x, {'gate': ['ffn1_w_up'], 'up': ['ffn1_w_down'], 'act': ['w_in#0'], 'dn': ['w_in#1']})
    x2, r2 = mix_fwd(x1, {'in': ['mla_w_q_up', 'mla_w_kv_up', 'w_branch_a', 'w_branch_b', 'w_out'],
                          'scan': xa_names + ['ffn2_w_gate'], 'attn': ['ffn2_w_up']})
    x3, r3 = xa_fwd(x2)
    x4, r4 = ffn_fwd('ffn2', x3, {'gate': ['ffn2_w_down']})

    def loss_fn(y, t):
        diff = y - t
        return diff * (1.0 / d), jnp.mean(diff * diff, axis=-1, keepdims=True)

    d4, row_loss = rowmap("loss", loss_fn, [R_(x4), R_(target)], [(F32, d, 0), (F32, 1, 0)], tile=tr)
    loss = lax.psum(0.5 * jnp.sum(row_loss), ("x", "y", "c"))
    late = ['mla_w_q_up', 'mla_w_kv_up', 'w_branch_a']
    d3 = ffn_bwd('ffn2', r4, d4, {'act_bwd': [('halve', ['ffn2_w_down'])], 'up_dw': [('halve', ['ffn2_w_gate'])],
                                  'gu_dx': [('spread', ['ffn2_w_down']), ('halve', ['ffn2_w_up'])]})
    d2 = xa_bwd(r3, d3, {'pre_bwd': [('halve', xa_names)]})
    d1 = mix_bwd(r2, d2, {'attn_bwd': [('spread', ['ffn2_w_gate', 'ffn2_w_up']), ('halve', ['w_out', 'w_branch_b'])],
                          'out_bwd': [('halve', late)],
                          'scan_bwd': [('spread', xa_names + ['w_out', 'w_branch_b'] + late)],
                          'in_dx': [('halve', ['w_in#0', 'w_in#1'])]})
    grad_x = ffn_bwd('ffn1', r1, d1, {'dn_dx': [('spread', ['w_in#0'])], 'dn_dw': [('spread', ['w_in#1'])],
                                      'act_bwd': [('halve', ['ffn1_w_down'])],
                                      'gate_dw': [('spread', ['ffn1_w_down'])], 'up_dw': [('halve', ['ffn1_w_gate'])],
                                      'gu_dx': [('spread', ['ffn1_w_gate']), ('halve', ['ffn1_w_up'])],
                                      'pre_bwd': [('spread', ['ffn1_w_up'])]})

    def pack_small(vals):
        flat = jnp.concatenate([vals[n].reshape(-1) for n in SMALL])
        rows = -(-flat.shape[0] // PACK_W)
        rows = -(-rows // 8) * 8
        return jnp.pad(flat, (0, rows * PACK_W - flat.shape[0])).reshape(rows, PACK_W)

    def unpack_small(buf):
        flat, out, at = buf.reshape(-1), {}, 0
        for n in SMALL:
            size = a[n].shape[0] * a[n].shape[1]
            out[n] = flat[at:at + size].reshape(a[n].shape)
            at += size
        return out

    g_small = pack_small(gs)
    small = Gather([g_small])
    run_alone("gather_g_small", small)
    g_small = lax.dynamic_update_index_in_dim(small.results[0], g_small, me, 0)

    grads, delta, new_m, new_v = {}, {}, {}, {}
    packs = adamw("adamw_small", g_small, pack_small(a), pack_small({n: a['m_' + n] for n in SMALL}),
                  pack_small({n: a['v_' + n] for n in SMALL}))
    for dst, buf in zip((grads, delta, new_m, new_v), packs):
        dst.update(unpack_small(buf))
    landed['w_in'] = jnp.concatenate([landed['w_in#0'], landed['w_in#1']], axis=1)
    for n in BIG:
        outs = adamw("adamw_" + n, landed[n], a[n][0], a['m_' + n][0], a['v_' + n][0])
        grads[n], delta[n], new_m[n], new_v[n] = (t.reshape(a[n].shape) for t in outs)

    return (loss, grad_x[None], *[grads[n] for n in WEIGHTS], *[delta[n] for n in WEIGHTS],
            *[new_m[n] for n in WEIGHTS], *[new_v[n] for n in WEIGHTS])


def kernel(x, mem, positions, hgrn_lb_logits, ffn1_pre_g, ffn1_w_gate, ffn1_w_up, ffn1_w_down, ffn1_post_g, mix_pre_g, w_in, hg_norm_g, mla_q_norm_g, mla_w_q_up, mla_kv_norm_g, mla_w_kv_up, w_branch_a, w_branch_b, w_out, mix_post_g, xa_pre_g, xa_mem_g, xa_w_q, xa_w_k, xa_w_v, xa_w_o, xa_post_g, ffn2_pre_g, ffn2_w_gate, ffn2_w_up, ffn2_w_down, ffn2_post_g, loss_target, m_hgrn_lb_logits, m_ffn1_pre_g, m_ffn1_w_gate, m_ffn1_w_up, m_ffn1_w_down, m_ffn1_post_g, m_mix_pre_g, m_w_in, m_hg_norm_g, m_mla_q_norm_g, m_mla_w_q_up, m_mla_kv_norm_g, m_mla_w_kv_up, m_w_branch_a, m_w_branch_b, m_w_out, m_mix_post_g, m_xa_pre_g, m_xa_mem_g, m_xa_w_q, m_xa_w_k, m_xa_w_v, m_xa_w_o, m_xa_post_g, m_ffn2_pre_g, m_ffn2_w_gate, m_ffn2_w_up, m_ffn2_w_down, m_ffn2_post_g, v_hgrn_lb_logits, v_ffn1_pre_g, v_ffn1_w_gate, v_ffn1_w_up, v_ffn1_w_down, v_ffn1_post_g, v_mix_pre_g, v_w_in, v_hg_norm_g, v_mla_q_norm_g, v_mla_w_q_up, v_mla_kv_norm_g, v_mla_w_kv_up, v_w_branch_a, v_w_branch_b, v_w_out, v_mix_post_g, v_xa_pre_g, v_xa_mem_g, v_xa_w_q, v_xa_w_k, v_xa_w_v, v_xa_w_o, v_xa_post_g, v_ffn2_pre_g, v_ffn2_w_gate, v_ffn2_w_up, v_ffn2_w_down, v_ffn2_post_g):
    return _step(dict(locals()))
```

```python
import functools

import jax
import jax.numpy as jnp
from jax import lax
from jax.experimental import pallas as pl
from jax.experimental.pallas import tpu as pltpu

F32 = jnp.float32
BF16 = jnp.bfloat16

N_DEV = 8
D_MODEL = 2048
CHUNK = 64
CHUNK_SHIFT = 6
SUB = 16
HG_HEADS = 16
HG_GROUP = 16
HEAD_W = 128
MLA_HEADS = 16
Q_LORA = 512
KV_LORA = 512
QK_ROPE = 64
MLA_QK = 192
XA_HEADS = 4
ROPE_THETA = 10000.0
EPS = 1e-6
PACK_W = 1024
VMEM_LIMIT = 56 * 1024 * 1024

ADAM_LR = 0.001
ADAM_B1 = 0.9
ADAM_B2 = 0.999
ADAM_EPS = 1e-08
ADAM_WD = 0.01
ADAM_STEP = 10

U_PAD = 256

WEIGHTS = ['hgrn_lb_logits', 'ffn1_pre_g', 'ffn1_w_gate', 'ffn1_w_up', 'ffn1_w_down', 'ffn1_post_g', 'mix_pre_g',
           'w_in', 'hg_norm_g', 'mla_q_norm_g', 'mla_w_q_up', 'mla_kv_norm_g', 'mla_w_kv_up', 'w_branch_a',
           'w_branch_b', 'w_out', 'mix_post_g', 'xa_pre_g', 'xa_mem_g', 'xa_w_q', 'xa_w_k', 'xa_w_v', 'xa_w_o',
           'xa_post_g', 'ffn2_pre_g', 'ffn2_w_gate', 'ffn2_w_up', 'ffn2_w_down', 'ffn2_post_g']
BIG = {'ffn1_w_gate': True, 'ffn1_w_up': True, 'ffn1_w_down': False, 'w_in': True, 'mla_w_q_up': True,
       'mla_w_kv_up': True, 'w_branch_a': False, 'w_branch_b': False, 'w_out': False, 'xa_w_q': False,
       'xa_w_k': False, 'xa_w_v': False, 'xa_w_o': True, 'ffn2_w_gate': True, 'ffn2_w_up': True,
       'ffn2_w_down': False}
SMALL = [n for n in WEIGHTS if n not in BIG]


def _cparams(**kw):
    return pltpu.CompilerParams(vmem_limit_bytes=VMEM_LIMIT, **kw)


def _pick(dim, cands):
    for c in cands:
        if dim % c == 0:
            return c
    return dim


def _place():
    return lax.axis_index("x"), lax.axis_index("y"), lax.axis_index("c")


def _slot(px, py, pc):
    return 4 * px + 2 * py + pc


class Gather:
    def __init__(self, tensors):
        self.operands = list(tensors)
        self.out_shape = [jax.ShapeDtypeStruct((N_DEV,) + t.shape, t.dtype) for t in tensors]
        n = len(tensors)
        self.scratch = [pltpu.SemaphoreType.DMA((n, N_DEV - 1)), pltpu.SemaphoreType.DMA((n, N_DEV - 1))]

    def _copies(self, t, x_ref, out_ref, scr):
        send, recv = scr
        x, y, c = _place()
        me, sibling = (x, y, c), (x, y, 1 - c)
        x_nbr, y_nbr, diag = (1 - x, y, c), (x, 1 - y, c), (1 - x, 1 - y, c)
        relay_from, relay_to = (x ^ (1 - c), y ^ c, c), (x ^ c, y ^ (1 - c), c)

        def copy(k, block, to, src=None):
            rows = out_ref.at[_slot(*block)]
            return pltpu.make_async_remote_copy(src_ref=rows if src is None else src, dst_ref=rows,
                                                send_sem=send.at[t, k], recv_sem=recv.at[t, k], device_id=to,
                                                device_id_type=pl.DeviceIdType.MESH)

        def other(block):
            return block[0], block[1], 1 - c

        sent = [(me, sibling, x_ref), (me, x_nbr, x_ref), (me, y_nbr, x_ref), (relay_from, relay_to, None),
                (x_nbr, sibling, None), (y_nbr, sibling, None), (diag, sibling, None)]
        landing = [sibling, x_nbr, y_nbr, diag, other(x_nbr), other(y_nbr), other(diag)]
        return (lambda k: copy(k, *sent[k])), (lambda k: copy(k, landing[k], me))

    def start(self, ins, outs, scr):
        for t, (x_ref, out_ref) in enumerate(zip(ins, outs)):
            give, _ = self._copies(t, x_ref, out_ref, scr)
            for k in range(3):
                give(k).start()

    def middle(self, ins, outs, scr):
        for t, (x_ref, out_ref) in enumerate(zip(ins, outs)):
            give, take = self._copies(t, x_ref, out_ref, scr)
            take(1).wait_recv()
            take(2).wait_recv()
            for k in (3, 4, 5):
                give(k).start()

    def finish(self, ins, outs, scr):
        for t, (x_ref, out_ref) in enumerate(zip(ins, outs)):
            give, take = self._copies(t, x_ref, out_ref, scr)
            take(3).wait_recv()
            give(6).start()
            for k in (0, 4, 5, 6):
                take(k).wait_recv()
            for k in range(N_DEV - 1):
                give(k).wait_send()

    def set_results(self, res):
        self.results = list(res)


N_CHIP = N_DEV // 2


class Halve:
    def __init__(self, tensors):
        self.operands = list(tensors)
        self.out_shape = [jax.ShapeDtypeStruct((N_CHIP,) + t.shape[1:], t.dtype) for t in tensors]
        n = len(tensors)
        self.scratch = [pltpu.SemaphoreType.DMA((n, N_CHIP)), pltpu.SemaphoreType.DMA((n, N_CHIP))]

    def _copies(self, t, x_ref, theirs_ref, scr):
        send, recv = scr
        x, y, c = _place()
        return [pltpu.make_async_remote_copy(
            src_ref=x_ref.at[2 * q + 1 - c], dst_ref=theirs_ref.at[q], send_sem=send.at[t, q],
            recv_sem=recv.at[t, q], device_id=(x, y, 1 - c), device_id_type=pl.DeviceIdType.MESH)
            for q in range(N_CHIP)]

    def start(self, ins, outs, scr):
        for t, (x_ref, theirs_ref) in enumerate(zip(ins, outs)):
            for give in self._copies(t, x_ref, theirs_ref, scr):
                give.start()

    def middle(self, ins, outs, scr):
        pass

    def finish(self, ins, outs, scr):
        for t, (x_ref, theirs_ref) in enumerate(zip(ins, outs)):
            for give in self._copies(t, x_ref, theirs_ref, scr):
                give.wait_recv()
                give.wait_send()

    def set_results(self, res):
        self.results = list(res)


class Spread:
    def __init__(self, tensors):
        self.operands = list(tensors)
        self.out_shape = [jax.ShapeDtypeStruct(t.shape, t.dtype) for t in tensors]
        n = len(tensors)
        self.scratch = [pltpu.SemaphoreType.DMA((n, N_CHIP - 1)), pltpu.SemaphoreType.DMA((n, N_CHIP - 1))]

    def _copies(self, t, y_ref, out_ref, scr, outgoing):
        send, recv = scr
        x, y, c = _place()
        copies = []
        for k in range(1, N_CHIP):
            px, py = x ^ (k >> 1), y ^ (k & 1)
            copies.append(pltpu.make_async_remote_copy(
                src_ref=y_ref.at[2 * px + py], dst_ref=out_ref.at[2 * x + y if outgoing else 2 * px + py],
                send_sem=send.at[t, k - 1], recv_sem=recv.at[t, k - 1], device_id=(px, py, c),
                device_id_type=pl.DeviceIdType.MESH))
        return copies

    def start(self, ins, outs, scr):
        for t, (y_ref, out_ref) in enumerate(zip(ins, outs)):
            for give in self._copies(t, y_ref, out_ref, scr, True):
                give.start()

    def middle(self, ins, outs, scr):
        pass

    def finish(self, ins, outs, scr):
        for t, (y_ref, out_ref) in enumerate(zip(ins, outs)):
            for take in self._copies(t, y_ref, out_ref, scr, False):
                take.wait_recv()
            for give in self._copies(t, y_ref, out_ref, scr, True):
                give.wait_send()

    def set_results(self, res):
        self.results = list(res)


class Joint:
    def __init__(self, parts):
        self.parts = list(parts)
        self.operands = [o for p in parts for o in p.operands]
        self.out_shape = [o for p in parts for o in p.out_shape]
        self.scratch = [o for p in parts for o in p.scratch]

    def _split(self, ins, outs, scr):
        i = o = s = 0
        for p in self.parts:
            ni, no, ns = len(p.operands), len(p.out_shape), len(p.scratch)
            yield p, ins[i:i + ni], outs[o:o + no], scr[s:s + ns]
            i, o, s = i + ni, o + no, s + ns

    def start(self, ins, outs, scr):
        for p, a, b, c in self._split(ins, outs, scr):
            p.start(a, b, c)

    def middle(self, ins, outs, scr):
        for p, a, b, c in self._split(ins, outs, scr):
            p.middle(a, b, c)

    def finish(self, ins, outs, scr):
        for p, a, b, c in self._split(ins, outs, scr):
            p.finish(a, b, c)

    def set_results(self, res):
        for p, _, part, _ in self._split([], list(res), []):
            p.set_results(part)


_HBM = pl.BlockSpec(memory_space=pltpu.HBM)


def run_alone(name, carry):
    n_in, n_out = len(carry.operands), len(carry.out_shape)

    def body(*refs):
        ins, outs, scr = refs[:n_in], refs[n_in:n_in + n_out], refs[n_in + n_out:]
        carry.start(ins, outs, scr)
        carry.middle(ins, outs, scr)
        carry.finish(ins, outs, scr)

    res = pl.pallas_call(body, name=name, in_specs=[_HBM] * n_in, out_specs=[_HBM] * n_out,
                         out_shape=carry.out_shape, scratch_shapes=carry.scratch)(*carry.operands)
    carry.set_results(list(res))


def _call(name, body, *, grid, in_specs, out_specs, out_shape, args, scratch=(), carry=None):
    in_specs, out_specs, out_shape, scratch = list(in_specs), list(out_specs), list(out_shape), list(scratch)
    if carry is None:
        return list(pl.pallas_call(body, name=name, grid=grid, in_specs=in_specs, out_specs=out_specs,
                                   out_shape=out_shape, scratch_shapes=scratch, compiler_params=_cparams())(*args))
    n_in, n_out, n_scr = len(in_specs), len(out_shape), len(scratch)
    c_in, c_out = len(carry.operands), len(carry.out_shape)

    def wrapped(*refs):
        ins, c_ins = refs[:n_in], refs[n_in:n_in + c_in]
        at = n_in + c_in
        outs, c_outs = refs[at:at + n_out], refs[at + n_out:at + n_out + c_out]
        at += n_out + c_out
        scr, c_scr = refs[at:at + n_scr], refs[at + n_scr:]
        first = functools.reduce(jnp.logical_and, [pl.program_id(d) == 0 for d in range(len(grid))])
        last = functools.reduce(jnp.logical_and, [pl.program_id(d) == grid[d] - 1 for d in range(len(grid))])

        @pl.when(first)
        def _():
            carry.start(c_ins, c_outs, c_scr)

        body(*ins, *outs, *scr)

        step, total = 0, 1
        for d in range(len(grid)):
            step, total = step * grid[d] + pl.program_id(d), total * grid[d]

        @pl.when(step == total // 2)
        def _():
            carry.middle(c_ins, c_outs, c_scr)

        @pl.when(last)
        def _():
            carry.finish(c_ins, c_outs, c_scr)

    res = pl.pallas_call(
        wrapped, name=name, grid=grid, in_specs=in_specs + [_HBM] * c_in, out_specs=out_specs + [_HBM] * c_out,
        out_shape=out_shape + carry.out_shape, scratch_shapes=scratch + carry.scratch, compiler_params=_cparams(),
    )(*args, *carry.operands)
    carry.set_results(list(res[n_out:]))
    return list(res[:n_out])


def R_(arr, w=None, off=0, stride=0):
    return ('r', arr, arr.shape[1] if w is None else w, off, stride)


def P_(arr, w=None, off=0, stride=0):
    return ('p', arr, arr.shape[1] if w is None else w, off, stride)


def rowmap(name, fn, ins, outs=None, *, tile, ncol=1, rows_inner=True, cts=None, wrt=None, gdt=None, cat=False,
           carry=None):
    rows = next(a.shape[0] for k, a, *_ in ins if k == 'r')
    nrow = rows // tile
    assert nrow * tile == rows
    grid = (ncol, nrow) if rows_inner else (nrow, ncol)

    def ij(g0, g1):
        return (g1, g0) if rows_inner else (g0, g1)

    def spec(kind, arr, w, off, stride):
        if kind == 'r':
            return pl.BlockSpec((tile, w), lambda g0, g1: (ij(g0, g1)[0], off + stride * ij(g0, g1)[1]))
        return pl.BlockSpec((arr.shape[0], w), lambda g0, g1: (0, off + stride * ij(g0, g1)[1]))

    ops = list(ins) + list(cts or [])
    in_specs = [spec(*o) for o in ops]
    n_in = len(ins)
    fwd = cts is None
    out_shape, out_specs, acc = [], [], []
    if fwd:
        for dt, w, stride in outs:
            out_shape.append(jax.ShapeDtypeStruct((rows, w * (ncol if stride else 1)), dt))
            out_specs.append(spec('r', None, w, 0, stride))
            acc.append(None)
    elif cat:
        widths = [ins[i][2] for i in wrt]
        assert ncol == 1 and all(ins[i][0] == 'r' for i in wrt)
        out_shape.append(jax.ShapeDtypeStruct((rows, sum(widths)), gdt))
        out_specs.append(spec('r', None, sum(widths), 0, 0))
    else:
        for n, i in enumerate(wrt):
            kind, arr, w, off, stride = ins[i]
            width = w * (ncol if stride else 1)
            if kind == 'r':
                out_shape.append(jax.ShapeDtypeStruct((rows, width), gdt[n]))
                out_specs.append(spec('r', None, w, 0, 1 if stride else 0))
                shared = stride == 0 and ncol > 1
                assert not shared or (not rows_inner and gdt[n] == F32)
                acc.append('col' if shared else None)
            else:
                out_shape.append(jax.ShapeDtypeStruct((arr.shape[0], width), F32))
                out_specs.append(spec('p', arr, w, 0, 1 if stride else 0))
                assert rows_inner or ncol == 1
                acc.append('row')

    def body(*refs):
        i, j = ij(pl.program_id(0), pl.program_id(1))
        vals = [r[...].astype(F32) for r in refs[:n_in]]
        out_refs = refs[len(ops):]
        if fwd:
            for o_ref, o in zip(out_refs, fn(*vals)):
                o_ref[...] = o.astype(o_ref.dtype)
            return

        def f(*d):
            full = list(vals)
            for n, idx in enumerate(wrt):
                full[idx] = d[n]
            return fn(*full)

        _, vjp = jax.vjp(f, *[vals[idx] for idx in wrt])
        grads = vjp(tuple(r[...].astype(F32) for r in refs[n_in:len(ops)]))
        if cat:
            o_ref, at = out_refs[0], 0
            for g in grads:
                o_ref[:, at:at + g.shape[1]] = g.astype(o_ref.dtype)
                at += g.shape[1]
            return
        for o_ref, g, a in zip(out_refs, grads, acc):
            if a is None:
                o_ref[...] = g.astype(o_ref.dtype)
            else:
                first = (i if a == 'row' else j) == 0

                @pl.when(first)
                def _(o_ref=o_ref):
                    o_ref[...] = jnp.zeros_like(o_ref)

                o_ref[...] += g

    return _call(name, body, grid=grid, in_specs=in_specs, out_specs=out_specs, out_shape=out_shape,
                 args=[o[1] for o in ops], carry=carry)


def _rms(x, g):
    return x * lax.rsqrt(jnp.mean(x * x, axis=-1, keepdims=True) + EPS) * g


_DIMS = {'nn': (((1,), (0,)), ((), ())), 'nt': (((1,), (1,)), ((), ())), 'tn': (((0,), (0,)), ((), ()))}


def mm(name, a, b, mode, *, jdim=None, out_dtype=F32, carry=None):
    a_list = list(a) if isinstance(a, (list, tuple)) else [a]
    b_list = list(b) if isinstance(b, (list, tuple)) else [b]
    a_order = ('k', 'm') if mode == 'tn' else ('m', 'k')
    b_order = ('n', 'k') if mode == 'nt' else ('k', 'n')
    size, blocks = {}, 1
    for arr, order in ((a_list[0], a_order), (b_list[0], b_order)):
        shape = arr.shape
        if jdim in order:
            blocks, shape = shape[0], shape[1:]
        for dname, extent in zip(order, shape):
            assert size.setdefault(dname, extent) == extent
    tile = {'m': _pick(size['m'], (1024, 512, 256, 128)), 'n': _pick(size['n'], (512, 256, 128)),
            'k': size['k'] if size['k'] <= 2048 else _pick(size['k'], (2304, 2048, 1024, 512, 256, 128))}
    if jdim is not None:
        tile[jdim] = size[jdim]
    if a_list[0].dtype == F32 and tile['k'] * tile['m'] > (1 << 20):
        tile['m'] = _pick(size['m'], (512, 256, 128))
    grid = tuple(blocks if d == jdim else size[d] // tile[d] for d in ('m', 'n', 'k'))
    nk = grid[2]

    def spec(order):
        shape = tuple(tile[d] for d in order)

        def imap(i, j, k):
            g = {'m': i, 'n': j, 'k': k}
            idx = tuple(0 if d == jdim else g[d] for d in order)
            return ((g[jdim],) + idx) if jdim in order else idx

        return pl.BlockSpec(((None,) + shape) if jdim in order else shape, imap)

    dims = _DIMS[mode]
    nt = len(a_list)

    def product(refs):
        acc = None
        for a_ref, b_ref in zip(refs[:nt], refs[nt:2 * nt]):
            p = lax.dot_general(a_ref[...].astype(BF16), b_ref[...].astype(BF16), dims, preferred_element_type=F32)
            acc = p if acc is None else acc + p
        return acc

    def body_once(*refs):
        refs[2 * nt][...] = product(refs).astype(refs[2 * nt].dtype)

    def body_acc(*refs):
        o_ref, acc_ref = refs[2 * nt], refs[2 * nt + 1]
        k = pl.program_id(2)

        @pl.when(k == 0)
        def _():
            acc_ref[...] = jnp.zeros_like(acc_ref)

        acc_ref[...] += product(refs)

        @pl.when(k == nk - 1)
        def _():
            o_ref[...] = acc_ref[...].astype(o_ref.dtype)

    out_dims = (size['m'], size['n'])
    out_shape = jax.ShapeDtypeStruct(((blocks,) + out_dims) if jdim in ('m', 'n') else out_dims, out_dtype)
    return _call(name, body_once if nk == 1 else body_acc, grid=grid,
                 in_specs=[spec(a_order)] * nt + [spec(b_order)] * nt, out_specs=[spec(('m', 'n'))],
                 out_shape=[out_shape], args=a_list + b_list,
                 scratch=[] if nk == 1 else [pltpu.VMEM((tile['m'], tile['n']), F32)], carry=carry)[0]


def _probs(q, k, i, tq, scale, causal):
    s = lax.dot_general(q, k, _DIMS['nt'], preferred_element_type=F32) * scale
    if causal:
        shape = s.shape
        q_chunk = jnp.right_shift(i * tq + lax.broadcasted_iota(jnp.int32, shape, 0), CHUNK_SHIFT)
        k_chunk = jnp.right_shift(lax.broadcasted_iota(jnp.int32, shape, 1), CHUNK_SHIFT)
        s = jnp.where(k_chunk <= q_chunk, s, -jnp.inf)
    e = jnp.exp(s - jnp.max(s, axis=-1, keepdims=True))
    return e, jnp.sum(e, axis=-1, keepdims=True)


def _per_prefix(work, i, tq, sq, sk, causal):
    if not causal:
        work(sk)
        return
    assert sq == sk and tq % CHUNK == 0
    for j in range(sq // tq):
        @pl.when(i == j)
        def _(j=j):
            work((j + 1) * tq)


def _attn_specs(tq, sk, dq, dv, koff, kstride, voff, vstride):
    return [pl.BlockSpec((tq, dq), lambda h, i: (i, h)),
            pl.BlockSpec((sk, dq), lambda h, i: (0, koff + kstride * h)),
            pl.BlockSpec((sk, dv), lambda h, i: (0, voff + vstride * h))]


def attn_fwd(name, q, k, v, *, heads, dq, dv, koff, kstride, voff, vstride, scale, causal, carry=None):
    sq, sk = q.shape[0], k.shape[0]
    tq = min(sq, 256)

    def body(q_ref, k_ref, v_ref, o_ref):
        i = pl.program_id(1)

        def work(keys):
            e, l = _probs(q_ref[...].astype(BF16), k_ref[0:keys, :].astype(BF16), i, tq, scale, causal)
            o = jnp.dot(e.astype(BF16), v_ref[0:keys, :].astype(BF16), preferred_element_type=F32)
            o_ref[...] = o / l

        _per_prefix(work, i, tq, sq, sk, causal)

    return _call(name, body, grid=(heads, sq // tq),
                 in_specs=_attn_specs(tq, sk, dq, dv, koff, kstride, voff, vstride),
                 out_specs=[pl.BlockSpec((tq, dv), lambda h, i: (i, h))],
                 out_shape=[jax.ShapeDtypeStruct((sq, heads * dv), F32)], args=[q, k, v], carry=carry)[0]


def attn_bwd(name, q, k, v, do, *, heads, dq, dv, koff, kstride, voff, vstride, scale, causal, carry=None):
    sq, sk = q.shape[0], k.shape[0]
    tq = min(sq, 256)

    def body(q_ref, k_ref, v_ref, do_ref, dq_ref, dk_ref, dv_ref):
        i = pl.program_id(1)

        @pl.when(i == 0)
        def _():
            dk_ref[...] = jnp.zeros_like(dk_ref)
            dv_ref[...] = jnp.zeros_like(dv_ref)

        def work(keys):
            qb, kb, vb = q_ref[...].astype(BF16), k_ref[0:keys, :].astype(BF16), v_ref[0:keys, :].astype(BF16)
            dob = do_ref[...].astype(BF16)
            e, l = _probs(qb, kb, i, tq, scale, causal)
            p = e / l
            dp = lax.dot_general(dob, vb, _DIMS['nt'], preferred_element_type=F32)
            ds = (p * (dp - jnp.sum(dp * p, axis=-1, keepdims=True)) * scale).astype(BF16)
            dv_ref[0:keys, :] += lax.dot_general(p.astype(BF16), dob, _DIMS['tn'], preferred_element_type=F32)
            dk_ref[0:keys, :] += lax.dot_general(ds, qb, _DIMS['tn'], preferred_element_type=F32)
            dq_ref[...] = jnp.dot(ds, kb, preferred_element_type=F32)

        _per_prefix(work, i, tq, sq, sk, causal)

    return _call(
        name, body, grid=(heads, sq // tq),
        in_specs=_attn_specs(tq, sk, dq, dv, koff, kstride, voff, vstride) + [pl.BlockSpec((tq, dv), lambda h, i: (i, h))],
        out_specs=[pl.BlockSpec((tq, dq), lambda h, i: (i, h)), pl.BlockSpec((sk, dq), lambda h, i: (0, h)),
                   pl.BlockSpec((sk, dv), lambda h, i: (0, h))],
        out_shape=[jax.ShapeDtypeStruct((sq, heads * dq), F32), jax.ShapeDtypeStruct((sk, heads * dq), F32),
                   jax.ShapeDtypeStruct((sk, heads * dv), F32)],
        args=[q, k, v, do], carry=carry)


def _hg_chunk(q, k, g, v, state):
    c = q.shape[0]
    row = lax.broadcasted_iota(jnp.int32, (c, c), 0)
    col = lax.broadcasted_iota(jnp.int32, (c, c), 1)
    tril = (col <= row).astype(F32)
    b = jnp.dot(tril, g, precision=lax.Precision.HIGHEST, preferred_element_type=F32)
    rows = lax.broadcasted_iota(jnp.int32, (c, 1), 0)
    o = jnp.dot((q * jnp.exp(b)).astype(BF16), state.astype(BF16), preferred_element_type=F32)
    t3 = lax.broadcasted_iota(jnp.int32, (SUB, SUB, 1), 0)
    s3 = lax.broadcasted_iota(jnp.int32, (SUB, SUB, 1), 1)
    parts = []
    for n in range(c // SUB):
        lo = n * SUB
        qn, kn, bn, vn = q[lo:lo + SUB], k[lo:lo + SUB], b[lo:lo + SUB], v[lo:lo + SUB]
        decay = jnp.exp(jnp.where(s3 <= t3, bn[:, None, :] - bn[None, :, :], -jnp.inf))
        sc = jnp.sum(qn[:, None, :] * kn[None, :, :] * decay, axis=-1)
        on = jnp.dot(sc.astype(BF16), vn.astype(BF16), preferred_element_type=F32)
        if n > 0:
            ref = jnp.sum(jnp.where(rows == lo - 1, b, 0.0), axis=0, keepdims=True)
            qd = (qn * jnp.exp(bn - ref)).astype(BF16)
            kd = (k[:lo] * jnp.exp(ref - b[:lo])).astype(BF16)
            so = lax.dot_general(qd, kd, _DIMS['nt'], preferred_element_type=F32)
            on = on + jnp.dot(so.astype(BF16), v[:lo].astype(BF16), preferred_element_type=F32)
        parts.append(on)
    o = o + jnp.concatenate(parts, axis=0)
    b_last = jnp.sum(g, axis=0, keepdims=True)
    ones = jnp.ones((c, 1), F32)
    b_last_col = lax.dot_general(g, ones, _DIMS['tn'], precision=lax.Precision.HIGHEST, preferred_element_type=F32)
    kd = (k * jnp.exp(b_last - b)).astype(BF16)
    new_state = jnp.exp(b_last_col) * state + lax.dot_general(kd, v.astype(BF16), _DIMS['tn'],
                                                              preferred_element_type=F32)
    return o, new_state


def _hg_chunk_bwd(q, k, g, v, state, do, dnew):
    c, kw = q.shape
    hi = lax.Precision.HIGHEST
    row = lax.broadcasted_iota(jnp.int32, (c, c), 0)
    col = lax.broadcasted_iota(jnp.int32, (c, c), 1)
    b = jnp.dot((col <= row).astype(F32), g, precision=hi, preferred_element_type=F32)
    rows = lax.broadcasted_iota(jnp.int32, (c, 1), 0)
    b_last = jnp.sum(g, axis=0, keepdims=True)
    b_last_col = lax.dot_general(g, jnp.ones((c, 1), F32), _DIMS['tn'], precision=hi, preferred_element_type=F32)
    eb, to_end = jnp.exp(b), jnp.exp(b_last - b)
    dob, vb, dnb = do.astype(BF16), v.astype(BF16), dnew.astype(BF16)
    k_end = (k * to_end).astype(BF16)
    dq = eb * lax.dot_general(dob, state.astype(BF16), _DIMS['nt'], preferred_element_type=F32)
    dk = to_end * lax.dot_general(vb, dnb, _DIMS['nt'], preferred_element_type=F32)
    dv = jnp.dot(k_end, dnb, preferred_element_type=F32)
    dstate = jnp.exp(b_last_col) * dnew + lax.dot_general((q * eb).astype(BF16), dob, _DIMS['tn'],
                                                          preferred_element_type=F32)
    new_state = jnp.exp(b_last_col) * state + lax.dot_general(k_end, vb, _DIMS['tn'], preferred_element_type=F32)
    db_end = lax.dot_general(jnp.ones((1, dnew.shape[1]), F32), dnew * new_state, _DIMS['nt'], precision=hi,
                             preferred_element_type=F32)
    t3 = lax.broadcasted_iota(jnp.int32, (SUB, SUB, 1), 0)
    s3 = lax.broadcasted_iota(jnp.int32, (SUB, SUB, 1), 1)
    dq_rows, dk_rows, dv_rows = [], [], []
    for n in range(c // SUB):
        lo = n * SUB
        qn, kn, bn, vn, don = q[lo:lo + SUB], k[lo:lo + SUB], b[lo:lo + SUB], vb[lo:lo + SUB], dob[lo:lo + SUB]
        decay = jnp.exp(jnp.where(s3 <= t3, bn[:, None, :] - bn[None, :, :], -jnp.inf))
        sc = jnp.sum(qn[:, None, :] * kn[None, :, :] * decay, axis=-1)
        pull = lax.dot_general(don, vn, _DIMS['nt'], preferred_element_type=F32)[:, :, None] * decay
        dqn = jnp.sum(pull * kn[None, :, :], axis=1)
        dk_rows.append(jnp.sum(pull * qn[:, None, :], axis=0))
        dv_rows.append(lax.dot_general(sc.astype(BF16), don, _DIMS['tn'], preferred_element_type=F32))
        if n > 0:
            ref = jnp.sum(jnp.where(rows == lo - 1, b, 0.0), axis=0, keepdims=True)
            up, down = jnp.exp(bn - ref), jnp.exp(ref - b[:lo])
            qd, kd = (qn * up).astype(BF16), (k[:lo] * down).astype(BF16)
            so = lax.dot_general(qd, kd, _DIMS['nt'], preferred_element_type=F32).astype(BF16)
            ao = lax.dot_general(don, vb[:lo], _DIMS['nt'], preferred_element_type=F32).astype(BF16)
            dqn = dqn + up * jnp.dot(ao, kd, preferred_element_type=F32)
            rest = jnp.zeros((c - lo, kw), F32)
            dk = dk + jnp.concatenate([down * lax.dot_general(ao, qd, _DIMS['tn'], preferred_element_type=F32),
                                       rest], axis=0)
            dv = dv + jnp.concatenate([lax.dot_general(so, don, _DIMS['tn'], preferred_element_type=F32), rest],
                                      axis=0)
        dq_rows.append(dqn)
    dq = dq + jnp.concatenate(dq_rows, axis=0)
    dk = dk + jnp.concatenate(dk_rows, axis=0)
    dv = dv + jnp.concatenate(dv_rows, axis=0)
    db = q * dq - k * dk + jnp.where(rows == c - 1, db_end, 0.0)
    dg = jnp.dot((col >= row).astype(F32), db, precision=hi, preferred_element_type=F32)
    return dq, dk, dg, dv, dstate


def hg_scan_fwd(name, q, k, g, u, v_off, carry=None):
    s = q.shape[0]
    n = s // CHUNK

    def body(q_ref, k_ref, g_ref, v_ref, o_ref, st_ref, state):
        @pl.when(pl.program_id(1) == 0)
        def _():
            state[...] = jnp.zeros_like(state)

        for j in range(HG_GROUP):
            cols = slice(j * HEAD_W, (j + 1) * HEAD_W)
            st = state[j]
            st_ref[j] = st
            o, new = _hg_chunk(q_ref[:, cols], k_ref[:, cols], g_ref[:, cols], v_ref[:, cols], st)
            o_ref[:, cols] = o
            state[j] = new

    wide = HG_GROUP * HEAD_W
    blk = pl.BlockSpec((CHUNK, wide), lambda h, c: (c, h))
    return _call(
        name, body, grid=(HG_HEADS // HG_GROUP, n),
        in_specs=[blk, blk, blk, pl.BlockSpec((CHUNK, wide), lambda h, c: (c, v_off // HG_GROUP + h))],
        out_specs=[blk, pl.BlockSpec((HG_GROUP, None, HEAD_W, HEAD_W), lambda h, c: (h, c, 0, 0))],
        out_shape=[jax.ShapeDtypeStruct((s, HG_HEADS * HEAD_W), F32),
                   jax.ShapeDtypeStruct((HG_HEADS, n, HEAD_W, HEAD_W), F32)],
        scratch=[pltpu.VMEM((HG_GROUP, HEAD_W, HEAD_W), F32)], args=[q, k, g, u], carry=carry)


def hg_scan_bwd(name, q, k, g, u, v_off, states, do, carry=None):
    s = q.shape[0]
    n = s // CHUNK

    def body(q_ref, k_ref, g_ref, v_ref, st_ref, do_ref, dq_ref, dk_ref, dg_ref, dv_ref, dstate):
        @pl.when(pl.program_id(1) == 0)
        def _():
            dstate[...] = jnp.zeros_like(dstate)

        for j in range(HG_GROUP):
            cols = slice(j * HEAD_W, (j + 1) * HEAD_W)
            dq, dk, dg, dv, dst = _hg_chunk_bwd(q_ref[:, cols], k_ref[:, cols], g_ref[:, cols], v_ref[:, cols],
                                                st_ref[j], do_ref[:, cols], dstate[j])
            dq_ref[:, cols] = dq
            dk_ref[:, cols] = dk
            dg_ref[:, cols] = dg
            dv_ref[:, cols] = dv
            dstate[j] = dst

    wide = HG_GROUP * HEAD_W
    blk = pl.BlockSpec((CHUNK, wide), lambda h, c: (n - 1 - c, h))
    out = jax.ShapeDtypeStruct((s, HG_HEADS * HEAD_W), F32)
    return _call(
        name, body, grid=(HG_HEADS // HG_GROUP, n),
        in_specs=[blk, blk, blk, pl.BlockSpec((CHUNK, wide), lambda h, c: (n - 1 - c, v_off // HG_GROUP + h)),
                  pl.BlockSpec((HG_GROUP, None, HEAD_W, HEAD_W), lambda h, c: (h, n - 1 - c, 0, 0)), blk],
        out_specs=[blk, blk, blk, blk], out_shape=[out, out, out, out],
        scratch=[pltpu.VMEM((HG_GROUP, HEAD_W, HEAD_W), F32)], args=[q, k, g, u, states, do], carry=carry)


def _tile2d(rows, cols, limit):
    for tr in range(min(rows, limit // cols) // 16 * 16, 0, -16):
        if rows % tr == 0:
            return tr, cols
    for tc in (2048, 1024, 512, 256, 128):
        if cols % tc == 0 and rows * tc <= limit:
            return rows, tc
    return rows, cols


def pair_sum(name, blocks, theirs):
    _, rows, cols = theirs.shape
    tr, tc = _tile2d(rows, cols, 1 << 20)

    def body(a_ref, b_ref, o_ref):
        mine = jnp.where(lax.axis_index("c") == 0, a_ref[0].astype(F32), a_ref[1].astype(F32))
        o_ref[...] = (mine + b_ref[...].astype(F32)).astype(o_ref.dtype)

    blk = pl.BlockSpec((None, tr, tc), lambda q, i, j: (q, i, j))
    return _call(name, body, grid=(N_CHIP, rows // tr, cols // tc),
                 in_specs=[pl.BlockSpec((None, 2, tr, tc), lambda q, i, j: (q, 0, i, j)), blk], out_specs=[blk],
                 out_shape=[jax.ShapeDtypeStruct(theirs.shape, theirs.dtype)],
                 args=[blocks.reshape(N_CHIP, 2, rows, cols), theirs])[0]


def adamw(name, landed, w, m, v):
    rows, cols = w.shape
    slots = landed.shape[0]
    tr, tc = _tile2d(rows, cols, 1 << 18)

    def body(l_ref, w_ref, m_ref, v_ref, g_ref, d_ref, nm_ref, nv_ref):
        gv = l_ref[0].astype(F32)
        for s in range(1, slots):
            gv = gv + l_ref[s].astype(F32)
        nm = ADAM_B1 * m_ref[...] + (1.0 - ADAM_B1) * gv
        nv = ADAM_B2 * v_ref[...] + (1.0 - ADAM_B2) * jnp.square(gv)
        m_hat = nm / (1.0 - ADAM_B1 ** ADAM_STEP)
        v_hat = nv / (1.0 - ADAM_B2 ** ADAM_STEP)
        g_ref[...] = gv
        d_ref[...] = -ADAM_LR * (m_hat / (jnp.sqrt(v_hat) + ADAM_EPS) + ADAM_WD * w_ref[...])
        nm_ref[...] = nm
        nv_ref[...] = nv

    blk = pl.BlockSpec((tr, tc), lambda i, j: (i, j))
    out = jax.ShapeDtypeStruct((rows, cols), F32)
    return _call(name, body, grid=(rows // tr, cols // tc),
                 in_specs=[pl.BlockSpec((slots, tr, tc), lambda i, j: (0, i, j)), blk, blk, blk],
                 out_specs=[blk] * 4, out_shape=[out] * 4, args=[landed, w, m, v])


def _swap_halves(pe):
    half = QK_ROPE // 2
    return jnp.concatenate([-pe[..., half:], pe[..., :half]], axis=-1)


def _unswap_halves(dsw):
    half = QK_ROPE // 2
    return jnp.concatenate([dsw[..., half:], -dsw[..., :half]], axis=-1)


def _w_in_ext(wt):
    kpe = wt[9216:9280]
    z = jnp.zeros((HEAD_W - QK_ROPE, wt.shape[1]), wt.dtype)
    pad = jnp.zeros((U_PAD, wt.shape[1]), wt.dtype)
    return jnp.concatenate([wt[:9216], wt[9280:], kpe, z, _swap_halves(kpe.T).T, z, pad], axis=0)


def _w_in_grad(dt):
    dkpe = dt[13312:13376] + _unswap_halves(dt[13440:13504].T).T
    return jnp.concatenate([dt[:9216], dkpe, dt[9216:13312]], axis=0)


def _w_q_ext(w):
    w3 = w.reshape(Q_LORA, MLA_HEADS, MLA_QK)
    pe = w3[:, :, HEAD_W:]
    z = jnp.zeros((Q_LORA, MLA_HEADS, HEAD_W - QK_ROPE), w.dtype)
    wide = MLA_HEADS * HEAD_W
    return jnp.concatenate([w3[:, :, :HEAD_W].reshape(Q_LORA, wide),
                            jnp.concatenate([pe, z], axis=2).reshape(Q_LORA, wide),
                            jnp.concatenate([_swap_halves(pe), z], axis=2).reshape(Q_LORA, wide)], axis=1)


def _w_q_grad(parts):
    d3 = [p.reshape(Q_LORA, MLA_HEADS, HEAD_W) for p in parts]
    dpe = d3[1][:, :, :QK_ROPE] + _unswap_halves(d3[2][:, :, :QK_ROPE])
    return jnp.concatenate([d3[0], dpe], axis=2).reshape(Q_LORA, MLA_HEADS * MLA_QK)


def _hg_prep(f_raw, q_hg, logits):
    lb = jax.nn.softmax(logits, axis=0)[0:1, :]
    log_f = jnp.logaddexp(jnp.log(lb), jnp.log1p(-lb) + jax.nn.log_sigmoid(f_raw))
    k_in = (1.0 - lb) * jax.nn.sigmoid(-f_raw)
    return log_f, k_in, jax.nn.silu(q_hg)


def _rope(q_nope, q_pe, q_sw, k_nope, k_pe, k_sw, cos, sin):
    qf = jnp.concatenate([q_nope, q_pe * cos + q_sw * sin], axis=1)
    kf = jnp.concatenate([k_nope, k_pe * cos + k_sw * sin], axis=1)
    return qf, kf


def _step(a):
    x, mem, target = a['x'][0], a['mem'][0], a['loss_target'][0]
    s, d = x.shape
    nm = mem.shape[0]
    ff = N_DEV * a['ffn1_w_gate'].shape[-1]
    cs = ff // N_DEV
    tr = min(s, 256)
    tp = min(s, 128)
    th = s
    ta = 1024

    flipped = ('ffn1_w_gate', 'ffn1_w_up', 'ffn2_w_gate', 'ffn2_w_up', 'w_in')

    def shard(n, moment=''):
        t = a[moment + n][0]
        return t.T if n in flipped else t

    def unflip(n, t):
        return (t.T if n in flipped else t).reshape(a[n].shape)

    bf = {n: shard(n).astype(BF16) for n in BIG}
    half_in = 832
    bf['w_in#0'], bf['w_in#1'] = bf['w_in'][:half_in], bf['w_in'][half_in:]
    gat, full = {}, {}

    my_chip = 2 * lax.axis_index("x") + lax.axis_index("y")
    me = 2 * my_chip + lax.axis_index("c")

    def gathered(names, carry):
        for n, g8 in zip(names or [], carry.results if carry else []):
            r, c = bf[n].shape
            g8 = lax.dynamic_update_index_in_dim(g8, bf[n], me, 0)
            gat[n] = g8
            if not n.startswith(('ffn', 'w_in')):
                full[n] = g8.transpose(1, 0, 2).reshape(r, N_DEV * c) if BIG[n] else g8.reshape(N_DEV * r, c)

    xa_names = ['xa_w_q', 'xa_w_k', 'xa_w_v', 'xa_w_o']
    ffn2_names = ['ffn2_w_gate', 'ffn2_w_up', 'ffn2_w_down']
    first = Gather([bf['ffn1_w_gate']])
    run_alone("gather_first", first)
    gathered(['ffn1_w_gate'], first)

    inv_freq = 1.0 / (ROPE_THETA ** (jnp.arange(0, QK_ROPE, 2, dtype=F32) / QK_ROPE))
    ang = a['positions'][0].astype(F32)[:, None] * inv_freq
    zero = jnp.zeros((s, HEAD_W - QK_ROPE), F32)
    cos = jnp.concatenate([jnp.cos(ang), jnp.cos(ang), zero], axis=1)
    sin = jnp.concatenate([jnp.sin(ang), jnp.sin(ang), zero], axis=1)

    gs = {}
    gb = {}

    def by_rows(g):
        return g.reshape(N_DEV, g.shape[0] // N_DEV, g.shape[1])

    def by_cols(g):
        return g.reshape(g.shape[0], N_DEV, g.shape[1] // N_DEV).transpose(1, 0, 2)

    pre_fn = lambda xv, g: (_rms(xv, g),)
    pre_res_fn = lambda xv, g: (_rms(xv, g), xv)

    def pre_norm(tag, x_in, g):
        return rowmap(tag + "_pre", pre_fn, [R_(x_in), P_(g)], [(BF16, x_in.shape[1], 0)], tile=tr)[0]

    def pre_norm_bwd(tag, x_in, g, dh, d_out, carry=None):
        return rowmap(tag + "_pre_bwd", pre_res_fn, [R_(x_in), P_(g)], tile=tr, cts=[R_(dh), R_(d_out)],
                      wrt=[0, 1], gdt=[F32, F32], carry=carry)

    def post_fn(weight):
        return lambda xv, y, g: (xv + weight * _rms(y, g),)

    def post_norm(tag, x_in, y, g, weight):
        return rowmap(tag + "_post", post_fn(weight), [R_(x_in), R_(y), P_(g)], [(F32, d, 0)], tile=tr)[0]

    def post_norm_bwd(tag, x_in, y, g, weight, d_out):
        return rowmap(tag + "_post_bwd", post_fn(weight), [R_(x_in), R_(y), P_(g)], tile=tr, cts=[R_(d_out)],
                      wrt=[1, 2], gdt=[BF16, F32])

    act_fn = lambda av, bv: (jax.nn.silu(av) * bv,)

    def fetch(plan, key):
        names = plan.get(key)
        return (names, Gather([bf[n] for n in names])) if names else (None, None)

    def send(plan, key):
        acts = plan.get(key)
        if not acts:
            return None, None
        parts = []
        for kind, names in acts:
            if kind == 'spread':
                for n in names:
                    chip_sums[n] = pair_sum("pair_" + n.replace('#', '_'), gb[n], halves[n])
            parts.append(Halve([gb[n] for n in names]) if kind == 'halve' else Spread([chip_sums[n] for n in names]))
        return acts, Joint(parts)

    def ffn_fwd(tag, x_in, plan):
        h = pre_norm(tag, x_in, a[tag + '_pre_g'])
        names, carry = fetch(plan, 'gate')
        av = mm(tag + "_gate", h, gat[tag + '_w_gate'], 'nt', jdim='n', out_dtype=BF16, carry=carry)
        gathered(names, carry)
        names, carry = fetch(plan, 'up')
        bv = mm(tag + "_up", h, gat[tag + '_w_up'], 'nt', jdim='n', out_dtype=BF16, carry=carry)
        gathered(names, carry)
        a2, b2 = av.reshape(N_DEV * s, cs), bv.reshape(N_DEV * s, cs)
        names, carry = fetch(plan, 'act')
        z = rowmap(tag + "_act", act_fn, [R_(a2), R_(b2)], [(BF16, cs, 0)], tile=ta, carry=carry)[0]
        z = z.reshape(N_DEV, s, cs)
        gathered(names, carry)
        names, carry = fetch(plan, 'dn')
        y = mm(tag + "_dn", z, gat[tag + '_w_down'], 'nn', jdim='k', carry=carry)
        gathered(names, carry)
        return post_norm(tag, x_in, y, a[tag + '_post_g'], 0.5), (x_in, h, a2, b2, z, y)

    def ffn_bwd(tag, res, d_out, plan):
        x_in, h, a2, b2, z, y = res
        dy, gs[tag + '_post_g'] = post_norm_bwd(tag, x_in, y, a[tag + '_post_g'], 0.5, d_out)
        acts, carry = send(plan, 'dn_dx')
        dz = mm(tag + "_dn_dx", dy, gat[tag + '_w_down'], 'nt', jdim='n', carry=carry)
        exchanged(acts, carry)
        acts, carry = send(plan, 'dn_dw')
        gb[tag + '_w_down'] = mm(tag + "_dn_dw", z, dy, 'tn', jdim='m', out_dtype=BF16, carry=carry)
        exchanged(acts, carry)
        acts, carry = send(plan, 'act_bwd')
        da, db = rowmap(tag + "_act_bwd", act_fn, [R_(a2), R_(b2)], tile=ta, cts=[R_(dz.reshape(N_DEV * s, cs))],
                        wrt=[0, 1], gdt=[BF16, BF16], carry=carry)
        exchanged(acts, carry)
        da, db = da.reshape(N_DEV, s, cs), db.reshape(N_DEV, s, cs)
        acts, carry = send(plan, 'gate_dw')
        gb[tag + '_w_gate'] = mm(tag + "_gate_dw", da, h, 'tn', jdim='m', out_dtype=BF16, carry=carry)
        exchanged(acts, carry)
        acts, carry = send(plan, 'up_dw')
        gb[tag + '_w_up'] = mm(tag + "_up_dw", db, h, 'tn', jdim='m', out_dtype=BF16, carry=carry)
        exchanged(acts, carry)
        acts, carry = send(plan, 'gu_dx')
        dh = mm(tag + "_gu_dx", [da, db], [gat[tag + '_w_gate'], gat[tag + '_w_up']], 'nn', jdim='k', carry=carry)
        exchanged(acts, carry)
        names, carry = send(plan, 'pre_bwd')
        d_in, gs[tag + '_pre_g'] = pre_norm_bwd(tag, x_in, a[tag + '_pre_g'], dh, d_out, carry)
        exchanged(names, carry)
        return d_in

    hg_out_fn = lambda o, og, g: (_rms(o, g) * jax.nn.silu(og),)
    mla_norm_fn = lambda cq, ckv, gq, gkv: (_rms(cq, gq), _rms(ckv, gkv))
    gate_fn = lambda ga, gb, ya, yb: (jax.nn.sigmoid(ga) * ya + jax.nn.sigmoid(gb) * yb,)
    mla = dict(heads=MLA_HEADS, dq=2 * HEAD_W, dv=HEAD_W, koff=0, kstride=1, voff=1, vstride=2,
               scale=MLA_QK ** -0.5, causal=True)

    def mix_fwd(x_in, plan):
        w_in = jnp.concatenate([gat['w_in#0'], gat['w_in#1']], axis=1)
        w_in = _w_in_ext(w_in.reshape(-1, w_in.shape[2]))
        h = pre_norm("mix", x_in, a['mix_pre_g'])
        names, carry = fetch(plan, 'in')
        u = mm("mix_in", h, w_in, 'nt', carry=carry)
        gathered(names, carry)
        w_q, w_kv = _w_q_ext(full['mla_w_q_up']), full['mla_w_kv_up']
        hg_ins = [R_(u, 2048, 1), R_(u, 2048, 0), P_(a['hgrn_lb_logits'])]
        log_f, k_in, q_in = rowmap("hg_prep", _hg_prep, hg_ins, [(F32, 2048, 0)] * 3, tile=tp)
        names, carry = fetch(plan, 'scan')
        o_a, states = hg_scan_fwd("hg_scan", q_in, k_in, log_f, u, 32, carry=carry)
        gathered(names, carry)
        out_ins = [R_(o_a, HEAD_W, 0, 1), R_(u, HEAD_W, 48, 1), P_(a['hg_norm_g'], HEAD_W, 0, 1)]
        oag = rowmap("hg_out", hg_out_fn, out_ins, [(BF16, HEAD_W, 1)], tile=th, ncol=HG_HEADS)[0]
        y_a = mm("mix_a", oag, full['w_branch_a'], 'nn')
        norm_ins = [R_(u, 512, 16), R_(u, 512, 17), P_(a['mla_q_norm_g']), P_(a['mla_kv_norm_g'])]
        cqn, ckvn = rowmap("mla_norm", mla_norm_fn, norm_ins, [(BF16, 512, 0)] * 2, tile=tr)
        q_all = mm("mla_qup", cqn, w_q, 'nn')
        kv = mm("mla_kvup", ckvn, w_kv, 'nn')
        rope_ins = [R_(q_all, HEAD_W, 0, 1), R_(q_all, HEAD_W, 16, 1), R_(q_all, HEAD_W, 32, 1),
                    R_(kv, HEAD_W, 0, 2), R_(u, HEAD_W, 104, 0), R_(u, HEAD_W, 105, 0), R_(cos), R_(sin)]
        qf, kf = rowmap("mla_rope", _rope, rope_ins, [(BF16, 2 * HEAD_W, 1)] * 2, tile=th, ncol=MLA_HEADS,
                        rows_inner=False)
        names, carry = fetch(plan, 'attn')
        o_b = attn_fwd("mla_attn", qf, kf, kv, carry=carry, **mla)
        gathered(names, carry)
        y_b = mm("mix_b", o_b, full['w_branch_b'], 'nn')
        gate_ins = [R_(u, 1024, 9, 1), R_(u, 1024, 11, 1), R_(y_a, 1024, 0, 1), R_(y_b, 1024, 0, 1)]
        y = rowmap("mix_gate", gate_fn, gate_ins, [(BF16, 1024, 1)], tile=tr, ncol=2)[0]
        yo = mm("mix_out", y, full['w_out'], 'nn')
        res = (x_in, h, hg_ins, q_in, k_in, log_f, u, states, out_ins, oag, norm_ins, cqn, ckvn, rope_ins, qf, kf,
               kv, o_b, gate_ins, y, yo, w_in, w_q, w_kv)
        return post_norm("mix", x_in, yo, a['mix_post_g'], 1.0), res

    def mix_bwd(res, d_out, plan):
        (x_in, h, hg_ins, q_in, k_in, log_f, u, states, out_ins, oag, norm_ins, cqn, ckvn, rope_ins, qf, kf, kv,
         o_b, gate_ins, y, yo, w_in, w_q, w_kv) = res
        dyo, gs['mix_post_g'] = post_norm_bwd("mix", x_in, yo, a['mix_post_g'], 1.0, d_out)
        dy = mm("mix_out_dx", dyo, full['w_out'], 'nt')
        gb['w_out'] = by_rows(mm("mix_out_dw", y, dyo, 'tn', out_dtype=BF16))
        dga, dgb, dya, dyb = rowmap("mix_gate_bwd", gate_fn, gate_ins, tile=tr, ncol=2, cts=[R_(dy, 1024, 0, 1)],
                                    wrt=[0, 1, 2, 3], gdt=[BF16] * 4)
        gb['w_branch_b'] = by_rows(mm("mix_b_dw", o_b, dyb, 'tn', out_dtype=BF16))
        do_b = mm("mix_b_dx", dyb, full['w_branch_b'], 'nt')
        names, carry = send(plan, 'attn_bwd')
        dqf, dkf, dv = attn_bwd("mla_attn_bwd", qf, kf, kv, do_b, carry=carry, **mla)
        exchanged(names, carry)
        dqn, dqp, dqs, dkn, dkpe, dksw = rowmap(
            "mla_rope_bwd", _rope, rope_ins, tile=th, ncol=MLA_HEADS, rows_inner=False,
            cts=[R_(dqf, 2 * HEAD_W, 0, 1), R_(dkf, 2 * HEAD_W, 0, 1)], wrt=[0, 1, 2, 3, 4, 5],
            gdt=[BF16, BF16, BF16, BF16, F32, F32])
        wide = MLA_HEADS * HEAD_W
        dq_parts = [dqn, dqp, dqs]
        gb['mla_w_q_up'] = by_cols(_w_q_grad([mm("mla_qup_dw%d" % n, cqn, dq_parts[n], 'tn', out_dtype=BF16)
                                              for n in range(3)]))
        dcqn = mm("mla_qup_dx", dq_parts, [w_q[:, n * wide:(n + 1) * wide] for n in range(3)], 'nt')
        w_kv4 = w_kv.reshape(KV_LORA, MLA_HEADS, 2, HEAD_W)
        dw_k = mm("mla_kup_dw", ckvn, dkn, 'tn', out_dtype=BF16).reshape(KV_LORA, MLA_HEADS, 1, HEAD_W)
        dw_v = mm("mla_vup_dw", ckvn, dv, 'tn', out_dtype=BF16).reshape(KV_LORA, MLA_HEADS, 1, HEAD_W)
        gb['mla_w_kv_up'] = by_cols(jnp.concatenate([dw_k, dw_v], axis=2).reshape(KV_LORA, 2 * wide))
        dckvn = mm("mla_kvup_dx", [dkn, dv],
                   [w_kv4[:, :, 0].reshape(KV_LORA, wide), w_kv4[:, :, 1].reshape(KV_LORA, wide)], 'nt')
        dcq, dckv, gs['mla_q_norm_g'], gs['mla_kv_norm_g'] = rowmap(
            "mla_norm_bwd", mla_norm_fn, norm_ins, tile=tr, cts=[R_(dcqn), R_(dckvn)], wrt=[0, 1, 2, 3],
            gdt=[BF16, BF16, F32, F32])
        gb['w_branch_a'] = by_rows(mm("mix_a_dw", oag, dya, 'tn', out_dtype=BF16))
        doag = mm("mix_a_dx", dya, full['w_branch_a'], 'nt')
        acts, carry = send(plan, 'out_bwd')
        do_a, dog, gs['hg_norm_g'] = rowmap("hg_out_bwd", hg_out_fn, out_ins, tile=th, ncol=HG_HEADS,
                                            cts=[R_(doag, HEAD_W, 0, 1)], wrt=[0, 1, 2], gdt=[F32, BF16, F32],
                                            carry=carry)
        exchanged(acts, carry)
        names, carry = send(plan, 'scan_bwd')
        dq_in, dk_in, dlog_f, di = hg_scan_bwd("hg_scan_bwd", q_in, k_in, log_f, u, 32, states, do_a, carry=carry)
        exchanged(names, carry)
        df, dq_hg, gs['hgrn_lb_logits'] = rowmap("hg_prep_bwd", _hg_prep, hg_ins, tile=tp,
                                                 cts=[R_(dlog_f), R_(dk_in), R_(dq_in)], wrt=[0, 1, 2],
                                                 gdt=[BF16, BF16, F32])
        du = jnp.concatenate([dq_hg, df, di.astype(BF16), dog, dcq, dckv, dga, dgb, dkpe.astype(BF16),
                              dksw.astype(BF16), jnp.zeros((s, U_PAD), BF16)], axis=1)
        g_in = _w_in_grad(mm("mix_in_dw", du, h, 'tn', out_dtype=BF16))
        g_in = g_in.reshape(N_DEV, g_in.shape[0] // N_DEV, g_in.shape[1])
        gb['w_in#0'], gb['w_in#1'] = g_in[:, :half_in], g_in[:, half_in:]
        names, carry = send(plan, 'in_dx')
        dh = mm("mix_in_dx", du, w_in, 'nn', carry=carry)
        exchanged(names, carry)
        d_in, gs['mix_pre_g'] = pre_norm_bwd("mix", x_in, a['mix_pre_g'], dh, d_out)
        return d_in

    xa = dict(heads=XA_HEADS, dq=HEAD_W, dv=HEAD_W, koff=0, kstride=1, voff=XA_HEADS, vstride=1,
              scale=HEAD_W ** -0.5, causal=False)
    tm_ = min(nm, 128)

    def xa_fwd(x_in):
        w_xkv = jnp.concatenate([full['xa_w_k'], full['xa_w_v']], axis=1)
        h = pre_norm("xa", x_in, a['xa_pre_g'])
        mn = rowmap("xa_mem", pre_fn, [R_(mem), P_(a['xa_mem_g'])], [(BF16, d, 0)], tile=tm_)[0]
        q = mm("xa_q", h, full['xa_w_q'], 'nn')
        kv = mm("xa_kv", mn, w_xkv, 'nn')
        o = attn_fwd("xa_attn", q, kv, kv, **xa)
        yo = mm("xa_o", o, full['xa_w_o'], 'nn')
        return post_norm("xa", x_in, yo, a['xa_post_g'], 1.0), (x_in, h, mn, q, kv, o, yo, w_xkv)

    def xa_bwd(res, d_out, plan):
        x_in, h, mn, q, kv, o, yo, w_xkv = res
        dyo, gs['xa_post_g'] = post_norm_bwd("xa", x_in, yo, a['xa_post_g'], 1.0, d_out)
        do = mm("xa_o_dx", dyo, full['xa_w_o'], 'nt')
        gb['xa_w_o'] = by_cols(mm("xa_o_dw", o, dyo, 'tn', out_dtype=BF16))
        dq, dk, dv = attn_bwd("xa_attn_bwd", q, kv, kv, do, **xa)
        dkv = jnp.concatenate([dk, dv], axis=1).astype(BF16)
        dw = mm("xa_kv_dw", mn, dkv, 'tn', out_dtype=BF16)
        gb['xa_w_k'], gb['xa_w_v'] = by_rows(dw[:, :XA_HEADS * HEAD_W]), by_rows(dw[:, XA_HEADS * HEAD_W:])
        dmn = mm("xa_kv_dx", dkv, w_xkv, 'nt')
        gs['xa_mem_g'] = rowmap("xa_mem_bwd", pre_fn, [R_(mem), P_(a['xa_mem_g'])], tile=tm_, cts=[R_(dmn)],
                                wrt=[1], gdt=[F32])[0]
        gb['xa_w_q'] = by_rows(mm("xa_q_dw", h, dq, 'tn', out_dtype=BF16))
        dh = mm("xa_q_dx", dq, full['xa_w_q'], 'nt')
        acts, carry = send(plan, 'pre_bwd')
        d_in, gs['xa_pre_g'] = pre_norm_bwd("xa", x_in, a['xa_pre_g'], dh, d_out, carry)
        exchanged(acts, carry)
        return d_in

    landed = {}

    halves = {}
    chip_sums = {}

    def exchanged(acts, carry):
        for (kind, names), part in zip(acts or [], carry.parts if carry else []):
            for n, got in zip(names, part.results):
                if kind == 'halve':
                    halves[n] = got
                else:
                    own = lax.dynamic_index_in_dim(chip_sums[n], my_chip, 0, keepdims=True)
                    landed[n] = lax.dynamic_update_slice_in_dim(got, own, my_chip, 0)

    x1, r1 = ffn_fwd('ffn1', x, {'gate': ['ffn1_w_up'], 'up': ['ffn1_w_down'], 'act': ['w_in#0'], 'dn': ['w_in#1']})
    x2, r2 = mix_fwd(x1, {'in': ['mla_w_q_up', 'mla_w_kv_up', 'w_branch_a', 'w_branch_b', 'w_out'],
                          'scan': xa_names + ['ffn2_w_gate'], 'attn': ['ffn2_w_up']})
    x3, r3 = xa_fwd(x2)
    x4, r4 = ffn_fwd('ffn2', x3, {'gate': ['ffn2_w_down']})

    def loss_fn(y, t):
        diff = y - t
        return diff * (1.0 / d), jnp.mean(diff * diff, axis=-1, keepdims=True)

    d4, row_loss = rowmap("loss", loss_fn, [R_(x4), R_(target)], [(F32, d, 0), (F32, 1, 0)], tile=tr)
    loss = lax.psum(0.5 * jnp.sum(row_loss), ("x", "y", "c"))
    late = ['mla_w_q_up', 'mla_w_kv_up', 'w_branch_a']
    d3 = ffn_bwd('ffn2', r4, d4, {'act_bwd': [('halve', ['ffn2_w_down'])], 'up_dw': [('halve', ['ffn2_w_gate'])],
                                  'gu_dx': [('spread', ['ffn2_w_down']), ('halve', ['ffn2_w_up'])]})
    d2 = xa_bwd(r3, d3, {'pre_bwd': [('halve', xa_names)]})
    d1 = mix_bwd(r2, d2, {'attn_bwd': [('spread', ['ffn2_w_gate', 'ffn2_w_up']), ('halve', ['w_out', 'w_branch_b'])],
                          'out_bwd': [('halve', late)],
                          'scan_bwd': [('spread', xa_names + ['w_out', 'w_branch_b'] + late)],
                          'in_dx': [('halve', ['w_in#0', 'w_in#1'])]})
    grad_x = ffn_bwd('ffn1', r1, d1, {'dn_dx': [('spread', ['w_in#0'])], 'dn_dw': [('spread', ['w_in#1'])],
                                      'act_bwd': [('halve', ['ffn1_w_down'])],
                                      'gate_dw': [('spread', ['ffn1_w_down'])], 'up_dw': [('halve', ['ffn1_w_gate'])],
                                      'gu_dx': [('spread', ['ffn1_w_gate']), ('halve', ['ffn1_w_up'])],
                                      'pre_bwd': [('spread', ['ffn1_w_up'])]})

    def pack_small(vals):
        flat = jnp.concatenate([vals[n].reshape(-1) for n in SMALL])
        rows = -(-flat.shape[0] // PACK_W)
        rows = -(-rows // 8) * 8
        return jnp.pad(flat, (0, rows * PACK_W - flat.shape[0])).reshape(rows, PACK_W)

    def unpack_small(buf):
        flat, out, at = buf.reshape(-1), {}, 0
        for n in SMALL:
            size = a[n].shape[0] * a[n].shape[1]
            out[n] = flat[at:at + size].reshape(a[n].shape)
            at += size
        return out

    g_small = pack_small(gs)
    small = Gather([g_small])
    run_alone("gather_g_small", small)
    g_small = lax.dynamic_update_index_in_dim(small.results[0], g_small, me, 0)

    grads, delta, new_m, new_v = {}, {}, {}, {}
    packs = adamw("adamw_small", g_small, pack_small(a), pack_small({n: a['m_' + n] for n in SMALL}),
                  pack_small({n: a['v_' + n] for n in SMALL}))
    for dst, buf in zip((grads, delta, new_m, new_v), packs):
        dst.update(unpack_small(buf))
    landed['w_in'] = jnp.concatenate([landed['w_in#0'], landed['w_in#1']], axis=1)
    for n in BIG:
        outs = adamw("adamw_" + n, landed[n], shard(n), shard(n, 'm_'), shard(n, 'v_'))
        grads[n], delta[n], new_m[n], new_v[n] = (unflip(n, t) for t in outs)

    return (loss, grad_x[None], *[grads[n] for n in WEIGHTS], *[delta[n] for n in WEIGHTS],
            *[new_m[n] for n in WEIGHTS], *[new_v[n] for n in WEIGHTS])


def kernel(x, mem, positions, hgrn_lb_logits, ffn1_pre_g, ffn1_w_gate, ffn1_w_up, ffn1_w_down, ffn1_post_g, mix_pre_g, w_in, hg_norm_g, mla_q_norm_g, mla_w_q_up, mla_kv_norm_g, mla_w_kv_up, w_branch_a, w_branch_b, w_out, mix_post_g, xa_pre_g, xa_mem_g, xa_w_q, xa_w_k, xa_w_v, xa_w_o, xa_post_g, ffn2_pre_g, ffn2_w_gate, ffn2_w_up, ffn2_w_down, ffn2_post_g, loss_target, m_hgrn_lb_logits, m_ffn1_pre_g, m_ffn1_w_gate, m_ffn1_w_up, m_ffn1_w_down, m_ffn1_post_g, m_mix_pre_g, m_w_in, m_hg_norm_g, m_mla_q_norm_g, m_mla_w_q_up, m_mla_kv_norm_g, m_mla_w_kv_up, m_w_branch_a, m_w_branch_b, m_w_out, m_mix_post_g, m_xa_pre_g, m_xa_mem_g, m_xa_w_q, m_xa_w_k, m_xa_w_v, m_xa_w_o, m_xa_post_g, m_ffn2_pre_g, m_ffn2_w_gate, m_ffn2_w_up, m_ffn2_w_down, m_ffn2_post_g, v_hgrn_lb_logits, v_ffn1_pre_g, v_ffn1_w_gate, v_ffn1_w_up, v_ffn1_w_down, v_ffn1_post_g, v_mix_pre_g, v_w_in, v_hg_norm_g, v_mla_q_norm_g, v_mla_w_q_up, v_mla_kv_norm_g, v_mla_w_kv_up, v_w_branch_a, v_w_branch_b, v_w_out, v_mix_post_g, v_xa_pre_g, v_xa_mem_g, v_xa_w_q, v_xa_w_k, v_xa_w_v, v_xa_w_o, v_xa_post_g, v_ffn2_pre_g, v_ffn2_w_gate, v_ffn2_w_up, v_ffn2_w_down, v_ffn2_post_g):
    return _step(dict(locals()))
```

```python
import functools

import jax
import jax.numpy as jnp
from jax import lax
from jax.experimental import pallas as pl
from jax.experimental.pallas import tpu as pltpu

F32 = jnp.float32
BF16 = jnp.bfloat16

N_DEV = 8
D_MODEL = 2048
CHUNK = 64
CHUNK_SHIFT = 6
SUB = 16
HG_HEADS = 16
HG_GROUP = 16
HEAD_W = 128
MLA_HEADS = 16
Q_LORA = 512
KV_LORA = 512
QK_ROPE = 64
MLA_QK = 192
XA_HEADS = 4
ROPE_THETA = 10000.0
EPS = 1e-6
PACK_W = 1024
VMEM_LIMIT = 56 * 1024 * 1024

ADAM_LR = 0.001
ADAM_B1 = 0.9
ADAM_B2 = 0.999
ADAM_EPS = 1e-08
ADAM_WD = 0.01
ADAM_STEP = 10

U_PAD = 256

WEIGHTS = ['hgrn_lb_logits', 'ffn1_pre_g', 'ffn1_w_gate', 'ffn1_w_up', 'ffn1_w_down', 'ffn1_post_g', 'mix_pre_g',
           'w_in', 'hg_norm_g', 'mla_q_norm_g', 'mla_w_q_up', 'mla_kv_norm_g', 'mla_w_kv_up', 'w_branch_a',
           'w_branch_b', 'w_out', 'mix_post_g', 'xa_pre_g', 'xa_mem_g', 'xa_w_q', 'xa_w_k', 'xa_w_v', 'xa_w_o',
           'xa_post_g', 'ffn2_pre_g', 'ffn2_w_gate', 'ffn2_w_up', 'ffn2_w_down', 'ffn2_post_g']
BIG = {'ffn1_w_gate': True, 'ffn1_w_up': True, 'ffn1_w_down': False, 'w_in': True, 'mla_w_q_up': True,
       'mla_w_kv_up': True, 'w_branch_a': False, 'w_branch_b': False, 'w_out': False, 'xa_w_q': False,
       'xa_w_k': False, 'xa_w_v': False, 'xa_w_o': True, 'ffn2_w_gate': True, 'ffn2_w_up': True,
       'ffn2_w_down': False}
SMALL = [n for n in WEIGHTS if n not in BIG]


def _cparams(**kw):
    return pltpu.CompilerParams(vmem_limit_bytes=VMEM_LIMIT, **kw)


def _pick(dim, cands):
    for c in cands:
        if dim % c == 0:
            return c
    return dim


def _place():
    return lax.axis_index("x"), lax.axis_index("y"), lax.axis_index("c")


def _slot(px, py, pc):
    return 4 * px + 2 * py + pc


class Gather:
    def __init__(self, tensors):
        self.operands = list(tensors)
        self.out_shape = [jax.ShapeDtypeStruct((N_DEV,) + t.shape, t.dtype) for t in tensors]
        n = len(tensors)
        self.scratch = [pltpu.SemaphoreType.DMA((n, N_DEV - 1)), pltpu.SemaphoreType.DMA((n, N_DEV - 1))]

    def _copies(self, t, x_ref, out_ref, scr):
        send, recv = scr
        x, y, c = _place()
        me, sibling = (x, y, c), (x, y, 1 - c)
        x_nbr, y_nbr, diag = (1 - x, y, c), (x, 1 - y, c), (1 - x, 1 - y, c)
        relay_from, relay_to = (x ^ (1 - c), y ^ c, c), (x ^ c, y ^ (1 - c), c)

        def copy(k, block, to, src=None):
            rows = out_ref.at[_slot(*block)]
            return pltpu.make_async_remote_copy(src_ref=rows if src is None else src, dst_ref=rows,
                                                send_sem=send.at[t, k], recv_sem=recv.at[t, k], device_id=to,
                                                device_id_type=pl.DeviceIdType.MESH)

        def other(block):
            return block[0], block[1], 1 - c

        sent = [(me, sibling, x_ref), (me, x_nbr, x_ref), (me, y_nbr, x_ref), (relay_from, relay_to, None),
                (x_nbr, sibling, None), (y_nbr, sibling, None), (diag, sibling, None)]
        landing = [sibling, x_nbr, y_nbr, diag, other(x_nbr), other(y_nbr), other(diag)]
        return (lambda k: copy(k, *sent[k])), (lambda k: copy(k, landing[k], me))

    def start(self, ins, outs, scr):
        for t, (x_ref, out_ref) in enumerate(zip(ins, outs)):
            give, _ = self._copies(t, x_ref, out_ref, scr)
            for k in range(3):
                give(k).start()

    def middle(self, ins, outs, scr):
        for t, (x_ref, out_ref) in enumerate(zip(ins, outs)):
            give, take = self._copies(t, x_ref, out_ref, scr)
            take(1).wait_recv()
            take(2).wait_recv()
            for k in (3, 4, 5):
                give(k).start()

    def finish(self, ins, outs, scr):
        for t, (x_ref, out_ref) in enumerate(zip(ins, outs)):
            give, take = self._copies(t, x_ref, out_ref, scr)
            take(3).wait_recv()
            give(6).start()
            for k in (0, 4, 5, 6):
                take(k).wait_recv()
            for k in range(N_DEV - 1):
                give(k).wait_send()

    def set_results(self, res):
        self.results = list(res)


N_CHIP = N_DEV // 2


class Halve:
    def __init__(self, tensors):
        self.operands = list(tensors)
        self.out_shape = [jax.ShapeDtypeStruct((N_CHIP,) + t.shape[1:], t.dtype) for t in tensors]
        n = len(tensors)
        self.scratch = [pltpu.SemaphoreType.DMA((n, N_CHIP)), pltpu.SemaphoreType.DMA((n, N_CHIP))]

    def _copies(self, t, x_ref, theirs_ref, scr):
        send, recv = scr
        x, y, c = _place()
        return [pltpu.make_async_remote_copy(
            src_ref=x_ref.at[2 * q + 1 - c], dst_ref=theirs_ref.at[q], send_sem=send.at[t, q],
            recv_sem=recv.at[t, q], device_id=(x, y, 1 - c), device_id_type=pl.DeviceIdType.MESH)
            for q in range(N_CHIP)]

    def start(self, ins, outs, scr):
        for t, (x_ref, theirs_ref) in enumerate(zip(ins, outs)):
            for give in self._copies(t, x_ref, theirs_ref, scr):
                give.start()

    def middle(self, ins, outs, scr):
        pass

    def finish(self, ins, outs, scr):
        for t, (x_ref, theirs_ref) in enumerate(zip(ins, outs)):
            for give in self._copies(t, x_ref, theirs_ref, scr):
                give.wait_recv()
                give.wait_send()

    def set_results(self, res):
        self.results = list(res)


class Spread:
    def __init__(self, tensors):
        self.operands = list(tensors)
        self.out_shape = [jax.ShapeDtypeStruct(t.shape, t.dtype) for t in tensors]
        n = len(tensors)
        self.scratch = [pltpu.SemaphoreType.DMA((n, N_CHIP - 1)), pltpu.SemaphoreType.DMA((n, N_CHIP - 1))]

    def _copies(self, t, y_ref, out_ref, scr, outgoing):
        send, recv = scr
        x, y, c = _place()
        copies = []
        for k in range(1, N_CHIP):
            px, py = x ^ (k >> 1), y ^ (k & 1)
            copies.append(pltpu.make_async_remote_copy(
                src_ref=y_ref.at[2 * px + py], dst_ref=out_ref.at[2 * x + y if outgoing else 2 * px + py],
                send_sem=send.at[t, k - 1], recv_sem=recv.at[t, k - 1], device_id=(px, py, c),
                device_id_type=pl.DeviceIdType.MESH))
        return copies

    def start(self, ins, outs, scr):
        for t, (y_ref, out_ref) in enumerate(zip(ins, outs)):
            for give in self._copies(t, y_ref, out_ref, scr, True):
                give.start()

    def middle(self, ins, outs, scr):
        pass

    def finish(self, ins, outs, scr):
        for t, (y_ref, out_ref) in enumerate(zip(ins, outs)):
            for take in self._copies(t, y_ref, out_ref, scr, False):
                take.wait_recv()
            for give in self._copies(t, y_ref, out_ref, scr, True):
                give.wait_send()

    def set_results(self, res):
        self.results = list(res)


class Joint:
    def __init__(self, parts):
        self.parts = list(parts)
        self.operands = [o for p in parts for o in p.operands]
        self.out_shape = [o for p in parts for o in p.out_shape]
        self.scratch = [o for p in parts for o in p.scratch]

    def _split(self, ins, outs, scr):
        i = o = s = 0
        for p in self.parts:
            ni, no, ns = len(p.operands), len(p.out_shape), len(p.scratch)
            yield p, ins[i:i + ni], outs[o:o + no], scr[s:s + ns]
            i, o, s = i + ni, o + no, s + ns

    def start(self, ins, outs, scr):
        for p, a, b, c in self._split(ins, outs, scr):
            p.start(a, b, c)

    def middle(self, ins, outs, scr):
        for p, a, b, c in self._split(ins, outs, scr):
            p.middle(a, b, c)

    def finish(self, ins, outs, scr):
        for p, a, b, c in self._split(ins, outs, scr):
            p.finish(a, b, c)

    def set_results(self, res):
        for p, _, part, _ in self._split([], list(res), []):
            p.set_results(part)


_HBM = pl.BlockSpec(memory_space=pltpu.HBM)


def run_alone(name, carry):
    n_in, n_out = len(carry.operands), len(carry.out_shape)

    def body(*refs):
        ins, outs, scr = refs[:n_in], refs[n_in:n_in + n_out], refs[n_in + n_out:]
        carry.start(ins, outs, scr)
        carry.middle(ins, outs, scr)
        carry.finish(ins, outs, scr)

    res = pl.pallas_call(body, name=name, in_specs=[_HBM] * n_in, out_specs=[_HBM] * n_out,
                         out_shape=carry.out_shape, scratch_shapes=carry.scratch)(*carry.operands)
    carry.set_results(list(res))


def _call(name, body, *, grid, in_specs, out_specs, out_shape, args, scratch=(), carry=None):
    in_specs, out_specs, out_shape, scratch = list(in_specs), list(out_specs), list(out_shape), list(scratch)
    if carry is None:
        return list(pl.pallas_call(body, name=name, grid=grid, in_specs=in_specs, out_specs=out_specs,
                                   out_shape=out_shape, scratch_shapes=scratch, compiler_params=_cparams())(*args))
    n_in, n_out, n_scr = len(in_specs), len(out_shape), len(scratch)
    c_in, c_out = len(carry.operands), len(carry.out_shape)

    def wrapped(*refs):
        ins, c_ins = refs[:n_in], refs[n_in:n_in + c_in]
        at = n_in + c_in
        outs, c_outs = refs[at:at + n_out], refs[at + n_out:at + n_out + c_out]
        at += n_out + c_out
        scr, c_scr = refs[at:at + n_scr], refs[at + n_scr:]
        first = functools.reduce(jnp.logical_and, [pl.program_id(d) == 0 for d in range(len(grid))])
        last = functools.reduce(jnp.logical_and, [pl.program_id(d) == grid[d] - 1 for d in range(len(grid))])

        @pl.when(first)
        def _():
            carry.start(c_ins, c_outs, c_scr)

        body(*ins, *outs, *scr)

        step, total = 0, 1
        for d in range(len(grid)):
            step, total = step * grid[d] + pl.program_id(d), total * grid[d]

        @pl.when(step == total // 2)
        def _():
            carry.middle(c_ins, c_outs, c_scr)

        @pl.when(last)
        def _():
            carry.finish(c_ins, c_outs, c_scr)

    res = pl.pallas_call(
        wrapped, name=name, grid=grid, in_specs=in_specs + [_HBM] * c_in, out_specs=out_specs + [_HBM] * c_out,
        out_shape=out_shape + carry.out_shape, scratch_shapes=scratch + carry.scratch, compiler_params=_cparams(),
    )(*args, *carry.operands)
    carry.set_results(list(res[n_out:]))
    return list(res[:n_out])


def R_(arr, w=None, off=0, stride=0):
    return ('r', arr, arr.shape[1] if w is None else w, off, stride)


def P_(arr, w=None, off=0, stride=0):
    return ('p', arr, arr.shape[1] if w is None else w, off, stride)


def rowmap(name, fn, ins, outs=None, *, tile, ncol=1, rows_inner=True, cts=None, wrt=None, gdt=None, cat=False,
           carry=None):
    rows = next(a.shape[0] for k, a, *_ in ins if k == 'r')
    nrow = rows // tile
    assert nrow * tile == rows
    grid = (ncol, nrow) if rows_inner else (nrow, ncol)

    def ij(g0, g1):
        return (g1, g0) if rows_inner else (g0, g1)

    def spec(kind, arr, w, off, stride):
        if kind == 'r':
            return pl.BlockSpec((tile, w), lambda g0, g1: (ij(g0, g1)[0], off + stride * ij(g0, g1)[1]))
        return pl.BlockSpec((arr.shape[0], w), lambda g0, g1: (0, off + stride * ij(g0, g1)[1]))

    ops = list(ins) + list(cts or [])
    in_specs = [spec(*o) for o in ops]
    n_in = len(ins)
    fwd = cts is None
    out_shape, out_specs, acc = [], [], []
    if fwd:
        for dt, w, stride in outs:
            out_shape.append(jax.ShapeDtypeStruct((rows, w * (ncol if stride else 1)), dt))
            out_specs.append(spec('r', None, w, 0, stride))
            acc.append(None)
    elif cat:
        widths = [ins[i][2] for i in wrt]
        assert ncol == 1 and all(ins[i][0] == 'r' for i in wrt)
        out_shape.append(jax.ShapeDtypeStruct((rows, sum(widths)), gdt))
        out_specs.append(spec('r', None, sum(widths), 0, 0))
    else:
        for n, i in enumerate(wrt):
            kind, arr, w, off, stride = ins[i]
            width = w * (ncol if stride else 1)
            if kind == 'r':
                out_shape.append(jax.ShapeDtypeStruct((rows, width), gdt[n]))
                out_specs.append(spec('r', None, w, 0, 1 if stride else 0))
                shared = stride == 0 and ncol > 1
                assert not shared or (not rows_inner and gdt[n] == F32)
                acc.append('col' if shared else None)
            else:
                out_shape.append(jax.ShapeDtypeStruct((arr.shape[0], width), F32))
                out_specs.append(spec('p', arr, w, 0, 1 if stride else 0))
                assert rows_inner or ncol == 1
                acc.append('row')

    def body(*refs):
        i, j = ij(pl.program_id(0), pl.program_id(1))
        vals = [r[...].astype(F32) for r in refs[:n_in]]
        out_refs = refs[len(ops):]
        if fwd:
            for o_ref, o in zip(out_refs, fn(*vals)):
                o_ref[...] = o.astype(o_ref.dtype)
            return

        def f(*d):
            full = list(vals)
            for n, idx in enumerate(wrt):
                full[idx] = d[n]
            return fn(*full)

        _, vjp = jax.vjp(f, *[vals[idx] for idx in wrt])
        grads = vjp(tuple(r[...].astype(F32) for r in refs[n_in:len(ops)]))
        if cat:
            o_ref, at = out_refs[0], 0
            for g in grads:
                o_ref[:, at:at + g.shape[1]] = g.astype(o_ref.dtype)
                at += g.shape[1]
            return
        for o_ref, g, a in zip(out_refs, grads, acc):
            if a is None:
                o_ref[...] = g.astype(o_ref.dtype)
            else:
                first = (i if a == 'row' else j) == 0

                @pl.when(first)
                def _(o_ref=o_ref):
                    o_ref[...] = jnp.zeros_like(o_ref)

                o_ref[...] += g

    return _call(name, body, grid=grid, in_specs=in_specs, out_specs=out_specs, out_shape=out_shape,
                 args=[o[1] for o in ops], carry=carry)


def _rms(x, g):
    return x * lax.rsqrt(jnp.mean(x * x, axis=-1, keepdims=True) + EPS) * g


_DIMS = {'nn': (((1,), (0,)), ((), ())), 'nt': (((1,), (1,)), ((), ())), 'tn': (((0,), (0,)), ((), ()))}


def mm(name, a, b, mode, *, jdim=None, out_dtype=F32, carry=None):
    a_list = list(a) if isinstance(a, (list, tuple)) else [a]
    b_list = list(b) if isinstance(b, (list, tuple)) else [b]
    a_order = ('k', 'm') if mode == 'tn' else ('m', 'k')
    b_order = ('n', 'k') if mode == 'nt' else ('k', 'n')
    size, blocks = {}, 1
    for arr, order in ((a_list[0], a_order), (b_list[0], b_order)):
        shape = arr.shape
        if jdim in order:
            blocks, shape = shape[0], shape[1:]
        for dname, extent in zip(order, shape):
            assert size.setdefault(dname, extent) == extent
    tile = {'m': _pick(size['m'], (1536, 1024, 512, 256, 128)), 'n': _pick(size['n'], (512, 256, 128)),
            'k': size['k'] if size['k'] <= 2048 else _pick(size['k'], (2304, 2048, 1024, 512, 256, 128))}
    if jdim is not None:
        tile[jdim] = size[jdim]
    if a_list[0].dtype == F32 and tile['k'] * tile['m'] > (1 << 20):
        tile['m'] = _pick(size['m'], (512, 256, 128))
    grid = tuple(blocks if d == jdim else size[d] // tile[d] for d in ('m', 'n', 'k'))
    nk = grid[2]

    def spec(order):
        shape = tuple(tile[d] for d in order)

        def imap(i, j, k):
            g = {'m': i, 'n': j, 'k': k}
            idx = tuple(0 if d == jdim else g[d] for d in order)
            return ((g[jdim],) + idx) if jdim in order else idx

        return pl.BlockSpec(((None,) + shape) if jdim in order else shape, imap)

    dims = _DIMS[mode]
    nt = len(a_list)

    def product(refs):
        acc = None
        for a_ref, b_ref in zip(refs[:nt], refs[nt:2 * nt]):
            p = lax.dot_general(a_ref[...].astype(BF16), b_ref[...].astype(BF16), dims, preferred_element_type=F32)
            acc = p if acc is None else acc + p
        return acc

    def body_once(*refs):
        refs[2 * nt][...] = product(refs).astype(refs[2 * nt].dtype)

    def body_acc(*refs):
        o_ref, acc_ref = refs[2 * nt], refs[2 * nt + 1]
        k = pl.program_id(2)

        @pl.when(k == 0)
        def _():
            acc_ref[...] = jnp.zeros_like(acc_ref)

        acc_ref[...] += product(refs)

        @pl.when(k == nk - 1)
        def _():
            o_ref[...] = acc_ref[...].astype(o_ref.dtype)

    out_dims = (size['m'], size['n'])
    out_shape = jax.ShapeDtypeStruct(((blocks,) + out_dims) if jdim in ('m', 'n') else out_dims, out_dtype)
    return _call(name, body_once if nk == 1 else body_acc, grid=grid,
                 in_specs=[spec(a_order)] * nt + [spec(b_order)] * nt, out_specs=[spec(('m', 'n'))],
                 out_shape=[out_shape], args=a_list + b_list,
                 scratch=[] if nk == 1 else [pltpu.VMEM((tile['m'], tile['n']), F32)], carry=carry)[0]


def _probs(q, k, i, tq, scale, causal):
    s = lax.dot_general(q, k, _DIMS['nt'], preferred_element_type=F32) * scale
    if causal:
        shape = s.shape
        q_chunk = jnp.right_shift(i * tq + lax.broadcasted_iota(jnp.int32, shape, 0), CHUNK_SHIFT)
        k_chunk = jnp.right_shift(lax.broadcasted_iota(jnp.int32, shape, 1), CHUNK_SHIFT)
        s = jnp.where(k_chunk <= q_chunk, s, -jnp.inf)
    e = jnp.exp(s - jnp.max(s, axis=-1, keepdims=True))
    return e, jnp.sum(e, axis=-1, keepdims=True)


def _per_prefix(work, i, tq, sq, sk, causal):
    if not causal:
        work(sk)
        return
    assert sq == sk and tq % CHUNK == 0
    for j in range(sq // tq):
        @pl.when(i == j)
        def _(j=j):
            work((j + 1) * tq)


def _attn_specs(tq, sk, dq, dv, koff, kstride, voff, vstride):
    return [pl.BlockSpec((tq, dq), lambda h, i: (i, h)),
            pl.BlockSpec((sk, dq), lambda h, i: (0, koff + kstride * h)),
            pl.BlockSpec((sk, dv), lambda h, i: (0, voff + vstride * h))]


def attn_fwd(name, q, k, v, *, heads, dq, dv, koff, kstride, voff, vstride, scale, causal, carry=None):
    sq, sk = q.shape[0], k.shape[0]
    tq = min(sq, 256)

    def body(q_ref, k_ref, v_ref, o_ref):
        i = pl.program_id(1)

        def work(keys):
            e, l = _probs(q_ref[...].astype(BF16), k_ref[0:keys, :].astype(BF16), i, tq, scale, causal)
            o = jnp.dot(e.astype(BF16), v_ref[0:keys, :].astype(BF16), preferred_element_type=F32)
            o_ref[...] = o / l

        _per_prefix(work, i, tq, sq, sk, causal)

    return _call(name, body, grid=(heads, sq // tq),
                 in_specs=_attn_specs(tq, sk, dq, dv, koff, kstride, voff, vstride),
                 out_specs=[pl.BlockSpec((tq, dv), lambda h, i: (i, h))],
                 out_shape=[jax.ShapeDtypeStruct((sq, heads * dv), F32)], args=[q, k, v], carry=carry)[0]


def attn_bwd(name, q, k, v, do, *, heads, dq, dv, koff, kstride, voff, vstride, scale, causal, carry=None):
    sq, sk = q.shape[0], k.shape[0]
    tq = min(sq, 256)

    def body(q_ref, k_ref, v_ref, do_ref, dq_ref, dk_ref, dv_ref):
        i = pl.program_id(1)

        @pl.when(i == 0)
        def _():
            dk_ref[...] = jnp.zeros_like(dk_ref)
            dv_ref[...] = jnp.zeros_like(dv_ref)

        def work(keys):
            qb, kb, vb = q_ref[...].astype(BF16), k_ref[0:keys, :].astype(BF16), v_ref[0:keys, :].astype(BF16)
            dob = do_ref[...].astype(BF16)
            e, l = _probs(qb, kb, i, tq, scale, causal)
            p = e / l
            dp = lax.dot_general(dob, vb, _DIMS['nt'], preferred_element_type=F32)
            ds = (p * (dp - jnp.sum(dp * p, axis=-1, keepdims=True)) * scale).astype(BF16)
            dv_ref[0:keys, :] += lax.dot_general(p.astype(BF16), dob, _DIMS['tn'], preferred_element_type=F32)
            dk_ref[0:keys, :] += lax.dot_general(ds, qb, _DIMS['tn'], preferred_element_type=F32)
            dq_ref[...] = jnp.dot(ds, kb, preferred_element_type=F32)

        _per_prefix(work, i, tq, sq, sk, causal)

    return _call(
        name, body, grid=(heads, sq // tq),
        in_specs=_attn_specs(tq, sk, dq, dv, koff, kstride, voff, vstride) + [pl.BlockSpec((tq, dv), lambda h, i: (i, h))],
        out_specs=[pl.BlockSpec((tq, dq), lambda h, i: (i, h)), pl.BlockSpec((sk, dq), lambda h, i: (0, h)),
                   pl.BlockSpec((sk, dv), lambda h, i: (0, h))],
        out_shape=[jax.ShapeDtypeStruct((sq, heads * dq), F32), jax.ShapeDtypeStruct((sk, heads * dq), F32),
                   jax.ShapeDtypeStruct((sk, heads * dv), F32)],
        args=[q, k, v, do], carry=carry)


def _hg_chunk(q, k, g, v, state):
    c = q.shape[0]
    row = lax.broadcasted_iota(jnp.int32, (c, c), 0)
    col = lax.broadcasted_iota(jnp.int32, (c, c), 1)
    tril = (col <= row).astype(F32)
    b = jnp.dot(tril, g, precision=lax.Precision.HIGHEST, preferred_element_type=F32)
    rows = lax.broadcasted_iota(jnp.int32, (c, 1), 0)
    o = jnp.dot((q * jnp.exp(b)).astype(BF16), state.astype(BF16), preferred_element_type=F32)
    t3 = lax.broadcasted_iota(jnp.int32, (SUB, SUB, 1), 0)
    s3 = lax.broadcasted_iota(jnp.int32, (SUB, SUB, 1), 1)
    parts = []
    for n in range(c // SUB):
        lo = n * SUB
        qn, kn, bn, vn = q[lo:lo + SUB], k[lo:lo + SUB], b[lo:lo + SUB], v[lo:lo + SUB]
        decay = jnp.exp(jnp.where(s3 <= t3, bn[:, None, :] - bn[None, :, :], -jnp.inf))
        sc = jnp.sum(qn[:, None, :] * kn[None, :, :] * decay, axis=-1)
        on = jnp.dot(sc.astype(BF16), vn.astype(BF16), preferred_element_type=F32)
        if n > 0:
            ref = jnp.sum(jnp.where(rows == lo - 1, b, 0.0), axis=0, keepdims=True)
            qd = (qn * jnp.exp(bn - ref)).astype(BF16)
            kd = (k[:lo] * jnp.exp(ref - b[:lo])).astype(BF16)
            so = lax.dot_general(qd, kd, _DIMS['nt'], preferred_element_type=F32)
            on = on + jnp.dot(so.astype(BF16), v[:lo].astype(BF16), preferred_element_type=F32)
        parts.append(on)
    o = o + jnp.concatenate(parts, axis=0)
    b_last = jnp.sum(g, axis=0, keepdims=True)
    ones = jnp.ones((c, 1), F32)
    b_last_col = lax.dot_general(g, ones, _DIMS['tn'], precision=lax.Precision.HIGHEST, preferred_element_type=F32)
    kd = (k * jnp.exp(b_last - b)).astype(BF16)
    new_state = jnp.exp(b_last_col) * state + lax.dot_general(kd, v.astype(BF16), _DIMS['tn'],
                                                              preferred_element_type=F32)
    return o, new_state


def _hg_chunk_bwd(q, k, g, v, state, do, dnew):
    c, kw = q.shape
    hi = lax.Precision.HIGHEST
    row = lax.broadcasted_iota(jnp.int32, (c, c), 0)
    col = lax.broadcasted_iota(jnp.int32, (c, c), 1)
    b = jnp.dot((col <= row).astype(F32), g, precision=hi, preferred_element_type=F32)
    rows = lax.broadcasted_iota(jnp.int32, (c, 1), 0)
    b_last = jnp.sum(g, axis=0, keepdims=True)
    b_last_col = lax.dot_general(g, jnp.ones((c, 1), F32), _DIMS['tn'], precision=hi, preferred_element_type=F32)
    eb, to_end = jnp.exp(b), jnp.exp(b_last - b)
    dob, vb, dnb = do.astype(BF16), v.astype(BF16), dnew.astype(BF16)
    k_end = (k * to_end).astype(BF16)
    dq = eb * lax.dot_general(dob, state.astype(BF16), _DIMS['nt'], preferred_element_type=F32)
    dk = to_end * lax.dot_general(vb, dnb, _DIMS['nt'], preferred_element_type=F32)
    dv = jnp.dot(k_end, dnb, preferred_element_type=F32)
    dstate = jnp.exp(b_last_col) * dnew + lax.dot_general((q * eb).astype(BF16), dob, _DIMS['tn'],
                                                          preferred_element_type=F32)
    new_state = jnp.exp(b_last_col) * state + lax.dot_general(k_end, vb, _DIMS['tn'], preferred_element_type=F32)
    db_end = lax.dot_general(jnp.ones((1, dnew.shape[1]), F32), dnew * new_state, _DIMS['nt'], precision=hi,
                             preferred_element_type=F32)
    t3 = lax.broadcasted_iota(jnp.int32, (SUB, SUB, 1), 0)
    s3 = lax.broadcasted_iota(jnp.int32, (SUB, SUB, 1), 1)
    dq_rows, dk_rows, dv_rows = [], [], []
    for n in range(c // SUB):
        lo = n * SUB
        qn, kn, bn, vn, don = q[lo:lo + SUB], k[lo:lo + SUB], b[lo:lo + SUB], vb[lo:lo + SUB], dob[lo:lo + SUB]
        decay = jnp.exp(jnp.where(s3 <= t3, bn[:, None, :] - bn[None, :, :], -jnp.inf))
        sc = jnp.sum(qn[:, None, :] * kn[None, :, :] * decay, axis=-1)
        pull = lax.dot_general(don, vn, _DIMS['nt'], preferred_element_type=F32)[:, :, None] * decay
        dqn = jnp.sum(pull * kn[None, :, :], axis=1)
        dk_rows.append(jnp.sum(pull * qn[:, None, :], axis=0))
        dv_rows.append(lax.dot_general(sc.astype(BF16), don, _DIMS['tn'], preferred_element_type=F32))
        if n > 0:
            ref = jnp.sum(jnp.where(rows == lo - 1, b, 0.0), axis=0, keepdims=True)
            up, down = jnp.exp(bn - ref), jnp.exp(ref - b[:lo])
            qd, kd = (qn * up).astype(BF16), (k[:lo] * down).astype(BF16)
            so = lax.dot_general(qd, kd, _DIMS['nt'], preferred_element_type=F32).astype(BF16)
            ao = lax.dot_general(don, vb[:lo], _DIMS['nt'], preferred_element_type=F32).astype(BF16)
            dqn = dqn + up * jnp.dot(ao, kd, preferred_element_type=F32)
            rest = jnp.zeros((c - lo, kw), F32)
            dk = dk + jnp.concatenate([down * lax.dot_general(ao, qd, _DIMS['tn'], preferred_element_type=F32),
                                       rest], axis=0)
            dv = dv + jnp.concatenate([lax.dot_general(so, don, _DIMS['tn'], preferred_element_type=F32), rest],
                                      axis=0)
        dq_rows.append(dqn)
    dq = dq + jnp.concatenate(dq_rows, axis=0)
    dk = dk + jnp.concatenate(dk_rows, axis=0)
    dv = dv + jnp.concatenate(dv_rows, axis=0)
    db = q * dq - k * dk + jnp.where(rows == c - 1, db_end, 0.0)
    dg = jnp.dot((col >= row).astype(F32), db, precision=hi, preferred_element_type=F32)
    return dq, dk, dg, dv, dstate


def hg_scan_fwd(name, q, k, g, u, v_off, carry=None):
    s = q.shape[0]
    n = s // CHUNK

    def body(q_ref, k_ref, g_ref, v_ref, o_ref, st_ref, state):
        @pl.when(pl.program_id(1) == 0)
        def _():
            state[...] = jnp.zeros_like(state)

        for j in range(HG_GROUP):
            cols = slice(j * HEAD_W, (j + 1) * HEAD_W)
            st = state[j]
            st_ref[j] = st
            o, new = _hg_chunk(q_ref[:, cols], k_ref[:, cols], g_ref[:, cols], v_ref[:, cols], st)
            o_ref[:, cols] = o
            state[j] = new

    wide = HG_GROUP * HEAD_W
    blk = pl.BlockSpec((CHUNK, wide), lambda h, c: (c, h))
    return _call(
        name, body, grid=(HG_HEADS // HG_GROUP, n),
        in_specs=[blk, blk, blk, pl.BlockSpec((CHUNK, wide), lambda h, c: (c, v_off // HG_GROUP + h))],
        out_specs=[blk, pl.BlockSpec((HG_GROUP, None, HEAD_W, HEAD_W), lambda h, c: (h, c, 0, 0))],
        out_shape=[jax.ShapeDtypeStruct((s, HG_HEADS * HEAD_W), F32),
                   jax.ShapeDtypeStruct((HG_HEADS, n, HEAD_W, HEAD_W), F32)],
        scratch=[pltpu.VMEM((HG_GROUP, HEAD_W, HEAD_W), F32)], args=[q, k, g, u], carry=carry)


def hg_scan_bwd(name, q, k, g, u, v_off, states, do, carry=None):
    s = q.shape[0]
    n = s // CHUNK

    def body(q_ref, k_ref, g_ref, v_ref, st_ref, do_ref, dq_ref, dk_ref, dg_ref, dv_ref, dstate):
        @pl.when(pl.program_id(1) == 0)
        def _():
            dstate[...] = jnp.zeros_like(dstate)

        for j in range(HG_GROUP):
            cols = slice(j * HEAD_W, (j + 1) * HEAD_W)
            dq, dk, dg, dv, dst = _hg_chunk_bwd(q_ref[:, cols], k_ref[:, cols], g_ref[:, cols], v_ref[:, cols],
                                                st_ref[j], do_ref[:, cols], dstate[j])
            dq_ref[:, cols] = dq
            dk_ref[:, cols] = dk
            dg_ref[:, cols] = dg
            dv_ref[:, cols] = dv
            dstate[j] = dst

    wide = HG_GROUP * HEAD_W
    blk = pl.BlockSpec((CHUNK, wide), lambda h, c: (n - 1 - c, h))
    out = jax.ShapeDtypeStruct((s, HG_HEADS * HEAD_W), F32)
    return _call(
        name, body, grid=(HG_HEADS // HG_GROUP, n),
        in_specs=[blk, blk, blk, pl.BlockSpec((CHUNK, wide), lambda h, c: (n - 1 - c, v_off // HG_GROUP + h)),
                  pl.BlockSpec((HG_GROUP, None, HEAD_W, HEAD_W), lambda h, c: (h, n - 1 - c, 0, 0)), blk],
        out_specs=[blk, blk, blk, blk], out_shape=[out, out, out, out],
        scratch=[pltpu.VMEM((HG_GROUP, HEAD_W, HEAD_W), F32)], args=[q, k, g, u, states, do], carry=carry)


def _tile2d(rows, cols, limit):
    for tr in range(min(rows, limit // cols) // 16 * 16, 0, -16):
        if rows % tr == 0:
            return tr, cols
    for tc in (2048, 1024, 512, 256, 128):
        if cols % tc == 0 and rows * tc <= limit:
            return rows, tc
    return rows, cols


def pair_sum(name, blocks, theirs):
    _, rows, cols = theirs.shape
    tr, tc = _tile2d(rows, cols, 1 << 20)

    def body(a_ref, b_ref, o_ref):
        mine = jnp.where(lax.axis_index("c") == 0, a_ref[0].astype(F32), a_ref[1].astype(F32))
        o_ref[...] = (mine + b_ref[...].astype(F32)).astype(o_ref.dtype)

    blk = pl.BlockSpec((None, tr, tc), lambda q, i, j: (q, i, j))
    return _call(name, body, grid=(N_CHIP, rows // tr, cols // tc),
                 in_specs=[pl.BlockSpec((None, 2, tr, tc), lambda q, i, j: (q, 0, i, j)), blk], out_specs=[blk],
                 out_shape=[jax.ShapeDtypeStruct(theirs.shape, theirs.dtype)],
                 args=[blocks.reshape(N_CHIP, 2, rows, cols), theirs])[0]


def adamw(name, landed, w, m, v):
    rows, cols = w.shape
    slots = landed.shape[0]
    tr, tc = _tile2d(rows, cols, 1 << 18)

    def body(l_ref, w_ref, m_ref, v_ref, g_ref, d_ref, nm_ref, nv_ref):
        gv = l_ref[0].astype(F32)
        for s in range(1, slots):
            gv = gv + l_ref[s].astype(F32)
        nm = ADAM_B1 * m_ref[...] + (1.0 - ADAM_B1) * gv
        nv = ADAM_B2 * v_ref[...] + (1.0 - ADAM_B2) * jnp.square(gv)
        m_hat = nm / (1.0 - ADAM_B1 ** ADAM_STEP)
        v_hat = nv / (1.0 - ADAM_B2 ** ADAM_STEP)
        g_ref[...] = gv
        d_ref[...] = -ADAM_LR * (m_hat / (jnp.sqrt(v_hat) + ADAM_EPS) + ADAM_WD * w_ref[...])
        nm_ref[...] = nm
        nv_ref[...] = nv

    blk = pl.BlockSpec((tr, tc), lambda i, j: (i, j))
    out = jax.ShapeDtypeStruct((rows, cols), F32)
    return _call(name, body, grid=(rows // tr, cols // tc),
                 in_specs=[pl.BlockSpec((slots, tr, tc), lambda i, j: (0, i, j)), blk, blk, blk],
                 out_specs=[blk] * 4, out_shape=[out] * 4, args=[landed, w, m, v])


def _swap_halves(pe):
    half = QK_ROPE // 2
    return jnp.concatenate([-pe[..., half:], pe[..., :half]], axis=-1)


def _unswap_halves(dsw):
    half = QK_ROPE // 2
    return jnp.concatenate([dsw[..., half:], -dsw[..., :half]], axis=-1)


def _w_in_ext(wt):
    kpe = wt[9216:9280]
    z = jnp.zeros((HEAD_W - QK_ROPE, wt.shape[1]), wt.dtype)
    pad = jnp.zeros((U_PAD, wt.shape[1]), wt.dtype)
    return jnp.concatenate([wt[:9216], wt[9280:], kpe, z, _swap_halves(kpe.T).T, z, pad], axis=0)


def _w_in_grad(dt):
    dkpe = dt[13312:13376] + _unswap_halves(dt[13440:13504].T).T
    return jnp.concatenate([dt[:9216], dkpe, dt[9216:13312]], axis=0)


def _w_q_ext(w):
    w3 = w.reshape(Q_LORA, MLA_HEADS, MLA_QK)
    pe = w3[:, :, HEAD_W:]
    z = jnp.zeros((Q_LORA, MLA_HEADS, HEAD_W - QK_ROPE), w.dtype)
    wide = MLA_HEADS * HEAD_W
    return jnp.concatenate([w3[:, :, :HEAD_W].reshape(Q_LORA, wide),
                            jnp.concatenate([pe, z], axis=2).reshape(Q_LORA, wide),
                            jnp.concatenate([_swap_halves(pe), z], axis=2).reshape(Q_LORA, wide)], axis=1)


def _w_q_grad(parts):
    d3 = [p.reshape(Q_LORA, MLA_HEADS, HEAD_W) for p in parts]
    dpe = d3[1][:, :, :QK_ROPE] + _unswap_halves(d3[2][:, :, :QK_ROPE])
    return jnp.concatenate([d3[0], dpe], axis=2).reshape(Q_LORA, MLA_HEADS * MLA_QK)


def _hg_prep(f_raw, q_hg, logits):
    lb = jax.nn.softmax(logits, axis=0)[0:1, :]
    log_f = jnp.logaddexp(jnp.log(lb), jnp.log1p(-lb) + jax.nn.log_sigmoid(f_raw))
    k_in = (1.0 - lb) * jax.nn.sigmoid(-f_raw)
    return log_f, k_in, jax.nn.silu(q_hg)


def _rope(q_nope, q_pe, q_sw, k_nope, k_pe, k_sw, cos, sin):
    qf = jnp.concatenate([q_nope, q_pe * cos + q_sw * sin], axis=1)
    kf = jnp.concatenate([k_nope, k_pe * cos + k_sw * sin], axis=1)
    return qf, kf


def _step(a):
    x, mem, target = a['x'][0], a['mem'][0], a['loss_target'][0]
    s, d = x.shape
    nm = mem.shape[0]
    ff = N_DEV * a['ffn1_w_gate'].shape[-1]
    cs = ff // N_DEV
    tr = min(s, 256)
    tp = min(s, 128)
    th = s
    ta = 1024

    flipped = ('ffn1_w_gate', 'ffn1_w_up', 'ffn2_w_gate', 'ffn2_w_up', 'w_in')

    def shard(n, moment=''):
        t = a[moment + n][0]
        return t.T if n in flipped else t

    def unflip(n, t):
        return (t.T if n in flipped else t).reshape(a[n].shape)

    bf = {n: shard(n).astype(BF16) for n in BIG}
    half_in = 832
    bf['w_in#0'], bf['w_in#1'] = bf['w_in'][:half_in], bf['w_in'][half_in:]
    gat, full = {}, {}

    my_chip = 2 * lax.axis_index("x") + lax.axis_index("y")
    me = 2 * my_chip + lax.axis_index("c")

    def gathered(names, carry):
        for n, g8 in zip(names or [], carry.results if carry else []):
            r, c = bf[n].shape
            g8 = lax.dynamic_update_index_in_dim(g8, bf[n], me, 0)
            gat[n] = g8
            if not n.startswith(('ffn', 'w_in')):
                full[n] = g8.transpose(1, 0, 2).reshape(r, N_DEV * c) if BIG[n] else g8.reshape(N_DEV * r, c)

    xa_names = ['xa_w_q', 'xa_w_k', 'xa_w_v', 'xa_w_o']
    ffn2_names = ['ffn2_w_gate', 'ffn2_w_up', 'ffn2_w_down']
    first = Gather([bf['ffn1_w_gate']])
    run_alone("gather_first", first)
    gathered(['ffn1_w_gate'], first)

    inv_freq = 1.0 / (ROPE_THETA ** (jnp.arange(0, QK_ROPE, 2, dtype=F32) / QK_ROPE))
    ang = a['positions'][0].astype(F32)[:, None] * inv_freq
    zero = jnp.zeros((s, HEAD_W - QK_ROPE), F32)
    cos = jnp.concatenate([jnp.cos(ang), jnp.cos(ang), zero], axis=1)
    sin = jnp.concatenate([jnp.sin(ang), jnp.sin(ang), zero], axis=1)

    gs = {}
    gb = {}

    def by_rows(g):
        return g.reshape(N_DEV, g.shape[0] // N_DEV, g.shape[1])

    def by_cols(g):
        return g.reshape(g.shape[0], N_DEV, g.shape[1] // N_DEV).transpose(1, 0, 2)

    pre_fn = lambda xv, g: (_rms(xv, g),)
    pre_res_fn = lambda xv, g: (_rms(xv, g), xv)

    def pre_norm(tag, x_in, g):
        return rowmap(tag + "_pre", pre_fn, [R_(x_in), P_(g)], [(BF16, x_in.shape[1], 0)], tile=tr)[0]

    def pre_norm_bwd(tag, x_in, g, dh, d_out, carry=None):
        return rowmap(tag + "_pre_bwd", pre_res_fn, [R_(x_in), P_(g)], tile=tr, cts=[R_(dh), R_(d_out)],
                      wrt=[0, 1], gdt=[F32, F32], carry=carry)

    def post_fn(weight):
        return lambda xv, y, g: (xv + weight * _rms(y, g),)

    def post_norm(tag, x_in, y, g, weight):
        return rowmap(tag + "_post", post_fn(weight), [R_(x_in), R_(y), P_(g)], [(F32, d, 0)], tile=tr)[0]

    def post_norm_bwd(tag, x_in, y, g, weight, d_out):
        return rowmap(tag + "_post_bwd", post_fn(weight), [R_(x_in), R_(y), P_(g)], tile=tr, cts=[R_(d_out)],
                      wrt=[1, 2], gdt=[BF16, F32])

    act_fn = lambda av, bv: (jax.nn.silu(av) * bv,)

    def fetch(plan, key):
        names = plan.get(key)
        return (names, Gather([bf[n] for n in names])) if names else (None, None)

    def send(plan, key):
        acts = plan.get(key)
        if not acts:
            return None, None
        parts = []
        for kind, names in acts:
            if kind == 'spread':
                for n in names:
                    chip_sums[n] = pair_sum("pair_" + n.replace('#', '_'), gb[n], halves[n])
            parts.append(Halve([gb[n] for n in names]) if kind == 'halve' else Spread([chip_sums[n] for n in names]))
        return acts, Joint(parts)

    def ffn_fwd(tag, x_in, plan):
        h = pre_norm(tag, x_in, a[tag + '_pre_g'])
        names, carry = fetch(plan, 'gate')
        av = mm(tag + "_gate", h, gat[tag + '_w_gate'], 'nt', jdim='n', out_dtype=BF16, carry=carry)
        gathered(names, carry)
        names, carry = fetch(plan, 'up')
        bv = mm(tag + "_up", h, gat[tag + '_w_up'], 'nt', jdim='n', out_dtype=BF16, carry=carry)
        gathered(names, carry)
        a2, b2 = av.reshape(N_DEV * s, cs), bv.reshape(N_DEV * s, cs)
        names, carry = fetch(plan, 'act')
        z = rowmap(tag + "_act", act_fn, [R_(a2), R_(b2)], [(BF16, cs, 0)], tile=ta, carry=carry)[0]
        z = z.reshape(N_DEV, s, cs)
        gathered(names, carry)
        names, carry = fetch(plan, 'dn')
        y = mm(tag + "_dn", z, gat[tag + '_w_down'], 'nn', jdim='k', carry=carry)
        gathered(names, carry)
        return post_norm(tag, x_in, y, a[tag + '_post_g'], 0.5), (x_in, h, a2, b2, z, y)

    def ffn_bwd(tag, res, d_out, plan):
        x_in, h, a2, b2, z, y = res
        dy, gs[tag + '_post_g'] = post_norm_bwd(tag, x_in, y, a[tag + '_post_g'], 0.5, d_out)
        acts, carry = send(plan, 'dn_dx')
        dz = mm(tag + "_dn_dx", dy, gat[tag + '_w_down'], 'nt', jdim='n', carry=carry)
        exchanged(acts, carry)
        acts, carry = send(plan, 'dn_dw')
        gb[tag + '_w_down'] = mm(tag + "_dn_dw", z, dy, 'tn', jdim='m', out_dtype=BF16, carry=carry)
        exchanged(acts, carry)
        acts, carry = send(plan, 'act_bwd')
        da, db = rowmap(tag + "_act_bwd", act_fn, [R_(a2), R_(b2)], tile=ta, cts=[R_(dz.reshape(N_DEV * s, cs))],
                        wrt=[0, 1], gdt=[BF16, BF16], carry=carry)
        exchanged(acts, carry)
        da, db = da.reshape(N_DEV, s, cs), db.reshape(N_DEV, s, cs)
        acts, carry = send(plan, 'gate_dw')
        gb[tag + '_w_gate'] = mm(tag + "_gate_dw", da, h, 'tn', jdim='m', out_dtype=BF16, carry=carry)
        exchanged(acts, carry)
        acts, carry = send(plan, 'up_dw')
        gb[tag + '_w_up'] = mm(tag + "_up_dw", db, h, 'tn', jdim='m', out_dtype=BF16, carry=carry)
        exchanged(acts, carry)
        acts, carry = send(plan, 'gu_dx')
        dh = mm(tag + "_gu_dx", [da, db], [gat[tag + '_w_gate'], gat[tag + '_w_up']], 'nn', jdim='k', carry=carry)
        exchanged(acts, carry)
        names, carry = send(plan, 'pre_bwd')
        d_in, gs[tag + '_pre_g'] = pre_norm_bwd(tag, x_in, a[tag + '_pre_g'], dh, d_out, carry)
        exchanged(names, carry)
        return d_in

    hg_out_fn = lambda o, og, g: (_rms(o, g) * jax.nn.silu(og),)
    mla_norm_fn = lambda cq, ckv, gq, gkv: (_rms(cq, gq), _rms(ckv, gkv))
    gate_fn = lambda ga, gb, ya, yb: (jax.nn.sigmoid(ga) * ya + jax.nn.sigmoid(gb) * yb,)
    mla = dict(heads=MLA_HEADS, dq=2 * HEAD_W, dv=HEAD_W, koff=0, kstride=1, voff=1, vstride=2,
               scale=MLA_QK ** -0.5, causal=True)

    def mix_fwd(x_in, plan):
        w_in = jnp.concatenate([gat['w_in#0'], gat['w_in#1']], axis=1)
        w_in = _w_in_ext(w_in.reshape(-1, w_in.shape[2]))
        h = pre_norm("mix", x_in, a['mix_pre_g'])
        names, carry = fetch(plan, 'in')
        u = mm("mix_in", h, w_in, 'nt', carry=carry)
        gathered(names, carry)
        w_q, w_kv = _w_q_ext(full['mla_w_q_up']), full['mla_w_kv_up']
        hg_ins = [R_(u, 2048, 1), R_(u, 2048, 0), P_(a['hgrn_lb_logits'])]
        log_f, k_in, q_in = rowmap("hg_prep", _hg_prep, hg_ins, [(F32, 2048, 0)] * 3, tile=tp)
        names, carry = fetch(plan, 'scan')
        o_a, states = hg_scan_fwd("hg_scan", q_in, k_in, log_f, u, 32, carry=carry)
        gathered(names, carry)
        out_ins = [R_(o_a, HEAD_W, 0, 1), R_(u, HEAD_W, 48, 1), P_(a['hg_norm_g'], HEAD_W, 0, 1)]
        oag = rowmap("hg_out", hg_out_fn, out_ins, [(BF16, HEAD_W, 1)], tile=th, ncol=HG_HEADS)[0]
        y_a = mm("mix_a", oag, full['w_branch_a'], 'nn')
        norm_ins = [R_(u, 512, 16), R_(u, 512, 17), P_(a['mla_q_norm_g']), P_(a['mla_kv_norm_g'])]
        cqn, ckvn = rowmap("mla_norm", mla_norm_fn, norm_ins, [(BF16, 512, 0)] * 2, tile=tr)
        q_all = mm("mla_qup", cqn, w_q, 'nn')
        kv = mm("mla_kvup", ckvn, w_kv, 'nn')
        rope_ins = [R_(q_all, HEAD_W, 0, 1), R_(q_all, HEAD_W, 16, 1), R_(q_all, HEAD_W, 32, 1),
                    R_(kv, HEAD_W, 0, 2), R_(u, HEAD_W, 104, 0), R_(u, HEAD_W, 105, 0), R_(cos), R_(sin)]
        qf, kf = rowmap("mla_rope", _rope, rope_ins, [(BF16, 2 * HEAD_W, 1)] * 2, tile=th, ncol=MLA_HEADS,
                        rows_inner=False)
        names, carry = fetch(plan, 'attn')
        o_b = attn_fwd("mla_attn", qf, kf, kv, carry=carry, **mla)
        gathered(names, carry)
        y_b = mm("mix_b", o_b, full['w_branch_b'], 'nn')
        gate_ins = [R_(u, 1024, 9, 1), R_(u, 1024, 11, 1), R_(y_a, 1024, 0, 1), R_(y_b, 1024, 0, 1)]
        y = rowmap("mix_gate", gate_fn, gate_ins, [(BF16, 1024, 1)], tile=tr, ncol=2)[0]
        yo = mm("mix_out", y, full['w_out'], 'nn')
        res = (x_in, h, hg_ins, q_in, k_in, log_f, u, states, out_ins, oag, norm_ins, cqn, ckvn, rope_ins, qf, kf,
               kv, o_b, gate_ins, y, yo, w_in, w_q, w_kv)
        return post_norm("mix", x_in, yo, a['mix_post_g'], 1.0), res

    def mix_bwd(res, d_out, plan):
        (x_in, h, hg_ins, q_in, k_in, log_f, u, states, out_ins, oag, norm_ins, cqn, ckvn, rope_ins, qf, kf, kv,
         o_b, gate_ins, y, yo, w_in, w_q, w_kv) = res
        dyo, gs['mix_post_g'] = post_norm_bwd("mix", x_in, yo, a['mix_post_g'], 1.0, d_out)
        dy = mm("mix_out_dx", dyo, full['w_out'], 'nt')
        gb['w_out'] = by_rows(mm("mix_out_dw", y, dyo, 'tn', out_dtype=BF16))
        dga, dgb, dya, dyb = rowmap("mix_gate_bwd", gate_fn, gate_ins, tile=tr, ncol=2, cts=[R_(dy, 1024, 0, 1)],
                                    wrt=[0, 1, 2, 3], gdt=[BF16] * 4)
        gb['w_branch_b'] = by_rows(mm("mix_b_dw", o_b, dyb, 'tn', out_dtype=BF16))
        do_b = mm("mix_b_dx", dyb, full['w_branch_b'], 'nt')
        names, carry = send(plan, 'attn_bwd')
        dqf, dkf, dv = attn_bwd("mla_attn_bwd", qf, kf, kv, do_b, carry=carry, **mla)
        exchanged(names, carry)
        dqn, dqp, dqs, dkn, dkpe, dksw = rowmap(
            "mla_rope_bwd", _rope, rope_ins, tile=th, ncol=MLA_HEADS, rows_inner=False,
            cts=[R_(dqf, 2 * HEAD_W, 0, 1), R_(dkf, 2 * HEAD_W, 0, 1)], wrt=[0, 1, 2, 3, 4, 5],
            gdt=[BF16, BF16, BF16, BF16, F32, F32])
        wide = MLA_HEADS * HEAD_W
        dq_parts = [dqn, dqp, dqs]
        gb['mla_w_q_up'] = by_cols(_w_q_grad([mm("mla_qup_dw%d" % n, cqn, dq_parts[n], 'tn', out_dtype=BF16)
                                              for n in range(3)]))
        dcqn = mm("mla_qup_dx", dq_parts, [w_q[:, n * wide:(n + 1) * wide] for n in range(3)], 'nt')
        w_kv4 = w_kv.reshape(KV_LORA, MLA_HEADS, 2, HEAD_W)
        dw_k = mm("mla_kup_dw", ckvn, dkn, 'tn', out_dtype=BF16).reshape(KV_LORA, MLA_HEADS, 1, HEAD_W)
        dw_v = mm("mla_vup_dw", ckvn, dv, 'tn', out_dtype=BF16).reshape(KV_LORA, MLA_HEADS, 1, HEAD_W)
        gb['mla_w_kv_up'] = by_cols(jnp.concatenate([dw_k, dw_v], axis=2).reshape(KV_LORA, 2 * wide))
        dckvn = mm("mla_kvup_dx", [dkn, dv],
                   [w_kv4[:, :, 0].reshape(KV_LORA, wide), w_kv4[:, :, 1].reshape(KV_LORA, wide)], 'nt')
        dcq, dckv, gs['mla_q_norm_g'], gs['mla_kv_norm_g'] = rowmap(
            "mla_norm_bwd", mla_norm_fn, norm_ins, tile=tr, cts=[R_(dcqn), R_(dckvn)], wrt=[0, 1, 2, 3],
            gdt=[BF16, BF16, F32, F32])
        gb['w_branch_a'] = by_rows(mm("mix_a_dw", oag, dya, 'tn', out_dtype=BF16))
        doag = mm("mix_a_dx", dya, full['w_branch_a'], 'nt')
        acts, carry = send(plan, 'out_bwd')
        do_a, dog, gs['hg_norm_g'] = rowmap("hg_out_bwd", hg_out_fn, out_ins, tile=th, ncol=HG_HEADS,
                                            cts=[R_(doag, HEAD_W, 0, 1)], wrt=[0, 1, 2], gdt=[F32, BF16, F32],
                                            carry=carry)
        exchanged(acts, carry)
        names, carry = send(plan, 'scan_bwd')
        dq_in, dk_in, dlog_f, di = hg_scan_bwd("hg_scan_bwd", q_in, k_in, log_f, u, 32, states, do_a, carry=carry)
        exchanged(names, carry)
        df, dq_hg, gs['hgrn_lb_logits'] = rowmap("hg_prep_bwd", _hg_prep, hg_ins, tile=tp,
                                                 cts=[R_(dlog_f), R_(dk_in), R_(dq_in)], wrt=[0, 1, 2],
                                                 gdt=[BF16, BF16, F32])
        du = jnp.concatenate([dq_hg, df, di.astype(BF16), dog, dcq, dckv, dga, dgb, dkpe.astype(BF16),
                              dksw.astype(BF16), jnp.zeros((s, U_PAD), BF16)], axis=1)
        g_in = _w_in_grad(mm("mix_in_dw", du, h, 'tn', out_dtype=BF16))
        g_in = g_in.reshape(N_DEV, g_in.shape[0] // N_DEV, g_in.shape[1])
        gb['w_in#0'], gb['w_in#1'] = g_in[:, :half_in], g_in[:, half_in:]
        names, carry = send(plan, 'in_dx')
        dh = mm("mix_in_dx", du, w_in, 'nn', carry=carry)
        exchanged(names, carry)
        d_in, gs['mix_pre_g'] = pre_norm_bwd("mix", x_in, a['mix_pre_g'], dh, d_out)
        return d_in

    xa = dict(heads=XA_HEADS, dq=HEAD_W, dv=HEAD_W, koff=0, kstride=1, voff=XA_HEADS, vstride=1,
              scale=HEAD_W ** -0.5, causal=False)
    tm_ = min(nm, 128)

    def xa_fwd(x_in):
        w_xkv = jnp.concatenate([full['xa_w_k'], full['xa_w_v']], axis=1)
        h = pre_norm("xa", x_in, a['xa_pre_g'])
        mn = rowmap("xa_mem", pre_fn, [R_(mem), P_(a['xa_mem_g'])], [(BF16, d, 0)], tile=tm_)[0]
        q = mm("xa_q", h, full['xa_w_q'], 'nn')
        kv = mm("xa_kv", mn, w_xkv, 'nn')
        o = attn_fwd("xa_attn", q, kv, kv, **xa)
        yo = mm("xa_o", o, full['xa_w_o'], 'nn')
        return post_norm("xa", x_in, yo, a['xa_post_g'], 1.0), (x_in, h, mn, q, kv, o, yo, w_xkv)

    def xa_bwd(res, d_out, plan):
        x_in, h, mn, q, kv, o, yo, w_xkv = res
        dyo, gs['xa_post_g'] = post_norm_bwd("xa", x_in, yo, a['xa_post_g'], 1.0, d_out)
        do = mm("xa_o_dx", dyo, full['xa_w_o'], 'nt')
        gb['xa_w_o'] = by_cols(mm("xa_o_dw", o, dyo, 'tn', out_dtype=BF16))
        dq, dk, dv = attn_bwd("xa_attn_bwd", q, kv, kv, do, **xa)
        dkv = jnp.concatenate([dk, dv], axis=1).astype(BF16)
        dw = mm("xa_kv_dw", mn, dkv, 'tn', out_dtype=BF16)
        gb['xa_w_k'], gb['xa_w_v'] = by_rows(dw[:, :XA_HEADS * HEAD_W]), by_rows(dw[:, XA_HEADS * HEAD_W:])
        dmn = mm("xa_kv_dx", dkv, w_xkv, 'nt')
        gs['xa_mem_g'] = rowmap("xa_mem_bwd", pre_fn, [R_(mem), P_(a['xa_mem_g'])], tile=tm_, cts=[R_(dmn)],
                                wrt=[1], gdt=[F32])[0]
        gb['xa_w_q'] = by_rows(mm("xa_q_dw", h, dq, 'tn', out_dtype=BF16))
        dh = mm("xa_q_dx", dq, full['xa_w_q'], 'nt')
        acts, carry = send(plan, 'pre_bwd')
        d_in, gs['xa_pre_g'] = pre_norm_bwd("xa", x_in, a['xa_pre_g'], dh, d_out, carry)
        exchanged(acts, carry)
        return d_in

    landed = {}

    halves = {}
    chip_sums = {}

    def exchanged(acts, carry):
        for (kind, names), part in zip(acts or [], carry.parts if carry else []):
            for n, got in zip(names, part.results):
                if kind == 'halve':
                    halves[n] = got
                else:
                    own = lax.dynamic_index_in_dim(chip_sums[n], my_chip, 0, keepdims=True)
                    landed[n] = lax.dynamic_update_slice_in_dim(got, own, my_chip, 0)

    x1, r1 = ffn_fwd('ffn1', x, {'gate': ['ffn1_w_up'], 'up': ['ffn1_w_down'], 'act': ['w_in#0'], 'dn': ['w_in#1']})
    x2, r2 = mix_fwd(x1, {'in': ['mla_w_q_up', 'mla_w_kv_up', 'w_branch_a', 'w_branch_b', 'w_out'],
                          'scan': xa_names + ['ffn2_w_gate'], 'attn': ['ffn2_w_up']})
    x3, r3 = xa_fwd(x2)
    x4, r4 = ffn_fwd('ffn2', x3, {'gate': ['ffn2_w_down']})

    def loss_fn(y, t):
        diff = y - t
        return diff * (1.0 / d), jnp.mean(diff * diff, axis=-1, keepdims=True)

    d4, row_loss = rowmap("loss", loss_fn, [R_(x4), R_(target)], [(F32, d, 0), (F32, 1, 0)], tile=tr)
    loss = lax.psum(0.5 * jnp.sum(row_loss), ("x", "y", "c"))
    late = ['mla_w_q_up', 'mla_w_kv_up', 'w_branch_a']
    d3 = ffn_bwd('ffn2', r4, d4, {'act_bwd': [('halve', ['ffn2_w_down'])], 'up_dw': [('halve', ['ffn2_w_gate'])],
                                  'gu_dx': [('spread', ['ffn2_w_down']), ('halve', ['ffn2_w_up'])]})
    d2 = xa_bwd(r3, d3, {'pre_bwd': [('halve', xa_names)]})
    d1 = mix_bwd(r2, d2, {'attn_bwd': [('spread', ['ffn2_w_gate', 'ffn2_w_up']), ('halve', ['w_out', 'w_branch_b'])],
                          'out_bwd': [('halve', late)],
                          'scan_bwd': [('spread', xa_names + ['w_out', 'w_branch_b'] + late)],
                          'in_dx': [('halve', ['w_in#0', 'w_in#1'])]})
    grad_x = ffn_bwd('ffn1', r1, d1, {'dn_dx': [('spread', ['w_in#0'])], 'dn_dw': [('spread', ['w_in#1'])],
                                      'act_bwd': [('halve', ['ffn1_w_down'])],
                                      'gate_dw': [('spread', ['ffn1_w_down'])], 'up_dw': [('halve', ['ffn1_w_gate'])],
                                      'gu_dx': [('spread', ['ffn1_w_gate']), ('halve', ['ffn1_w_up'])],
                                      'pre_bwd': [('spread', ['ffn1_w_up'])]})

    def pack_small(vals):
        flat = jnp.concatenate([vals[n].reshape(-1) for n in SMALL])
        rows = -(-flat.shape[0] // PACK_W)
        rows = -(-rows // 8) * 8
        return jnp.pad(flat, (0, rows * PACK_W - flat.shape[0])).reshape(rows, PACK_W)

    def unpack_small(buf):
        flat, out, at = buf.reshape(-1), {}, 0
        for n in SMALL:
            size = a[n].shape[0] * a[n].shape[1]
            out[n] = flat[at:at + size].reshape(a[n].shape)
            at += size
        return out

    g_small = pack_small(gs)
    small = Gather([g_small])
    run_alone("gather_g_small", small)
    g_small = lax.dynamic_update_index_in_dim(small.results[0], g_small, me, 0)

    grads, delta, new_m, new_v = {}, {}, {}, {}
    packs = adamw("adamw_small", g_small, pack_small(a), pack_small({n: a['m_' + n] for n in SMALL}),
                  pack_small({n: a['v_' + n] for n in SMALL}))
    for dst, buf in zip((grads, delta, new_m, new_v), packs):
        dst.update(unpack_small(buf))
    landed['w_in'] = jnp.concatenate([landed['w_in#0'], landed['w_in#1']], axis=1)
    for n in BIG:
        outs = adamw("adamw_" + n, landed[n], shard(n), shard(n, 'm_'), shard(n, 'v_'))
        grads[n], delta[n], new_m[n], new_v[n] = (unflip(n, t) for t in outs)

    return (loss, grad_x[None], *[grads[n] for n in WEIGHTS], *[delta[n] for n in WEIGHTS],
            *[new_m[n] for n in WEIGHTS], *[new_v[n] for n in WEIGHTS])


def kernel(x, mem, positions, hgrn_lb_logits, ffn1_pre_g, ffn1_w_gate, ffn1_w_up, ffn1_w_down, ffn1_post_g, mix_pre_g, w_in, hg_norm_g, mla_q_norm_g, mla_w_q_up, mla_kv_norm_g, mla_w_kv_up, w_branch_a, w_branch_b, w_out, mix_post_g, xa_pre_g, xa_mem_g, xa_w_q, xa_w_k, xa_w_v, xa_w_o, xa_post_g, ffn2_pre_g, ffn2_w_gate, ffn2_w_up, ffn2_w_down, ffn2_post_g, loss_target, m_hgrn_lb_logits, m_ffn1_pre_g, m_ffn1_w_gate, m_ffn1_w_up, m_ffn1_w_down, m_ffn1_post_g, m_mix_pre_g, m_w_in, m_hg_norm_g, m_mla_q_norm_g, m_mla_w_q_up, m_mla_kv_norm_g, m_mla_w_kv_up, m_w_branch_a, m_w_branch_b, m_w_out, m_mix_post_g, m_xa_pre_g, m_xa_mem_g, m_xa_w_q, m_xa_w_k, m_xa_w_v, m_xa_w_o, m_xa_post_g, m_ffn2_pre_g, m_ffn2_w_gate, m_ffn2_w_up, m_ffn2_w_down, m_ffn2_post_g, v_hgrn_lb_logits, v_ffn1_pre_g, v_ffn1_w_gate, v_ffn1_w_up, v_ffn1_w_down, v_ffn1_post_g, v_mix_pre_g, v_w_in, v_hg_norm_g, v_mla_q_norm_g, v_mla_w_q_up, v_mla_kv_norm_g, v_mla_w_kv_up, v_w_branch_a, v_w_branch_b, v_w_out, v_mix_post_g, v_xa_pre_g, v_xa_mem_g, v_xa_w_q, v_xa_w_k, v_xa_w_v, v_xa_w_o, v_xa_post_g, v_ffn2_pre_g, v_ffn2_w_gate, v_ffn2_w_up, v_ffn2_w_down, v_ffn2_post_g):
    return _step(dict(locals()))
```

```python
import functools

import jax
import jax.numpy as jnp
from jax import lax
from jax.experimental import pallas as pl
from jax.experimental.pallas import tpu as pltpu

F32 = jnp.float32
BF16 = jnp.bfloat16

N_DEV = 8
D_MODEL = 2048
CHUNK = 64
CHUNK_SHIFT = 6
SUB = 16
HG_HEADS = 16
HG_GROUP = 16
HEAD_W = 128
MLA_HEADS = 16
Q_LORA = 512
KV_LORA = 512
QK_ROPE = 64
MLA_QK = 192
XA_HEADS = 4
ROPE_THETA = 10000.0
EPS = 1e-6
PACK_W = 1024
VMEM_LIMIT = 56 * 1024 * 1024

ADAM_LR = 0.001
ADAM_B1 = 0.9
ADAM_B2 = 0.999
ADAM_EPS = 1e-08
ADAM_WD = 0.01
ADAM_STEP = 10

U_PAD = 256

WEIGHTS = ['hgrn_lb_logits', 'ffn1_pre_g', 'ffn1_w_gate', 'ffn1_w_up', 'ffn1_w_down', 'ffn1_post_g', 'mix_pre_g',
           'w_in', 'hg_norm_g', 'mla_q_norm_g', 'mla_w_q_up', 'mla_kv_norm_g', 'mla_w_kv_up', 'w_branch_a',
           'w_branch_b', 'w_out', 'mix_post_g', 'xa_pre_g', 'xa_mem_g', 'xa_w_q', 'xa_w_k', 'xa_w_v', 'xa_w_o',
           'xa_post_g', 'ffn2_pre_g', 'ffn2_w_gate', 'ffn2_w_up', 'ffn2_w_down', 'ffn2_post_g']
BIG = {'ffn1_w_gate': True, 'ffn1_w_up': True, 'ffn1_w_down': False, 'w_in': True, 'mla_w_q_up': True,
       'mla_w_kv_up': True, 'w_branch_a': False, 'w_branch_b': False, 'w_out': False, 'xa_w_q': False,
       'xa_w_k': False, 'xa_w_v': False, 'xa_w_o': True, 'ffn2_w_gate': True, 'ffn2_w_up': True,
       'ffn2_w_down': False}
SMALL = [n for n in WEIGHTS if n not in BIG]


def _cparams(**kw):
    return pltpu.CompilerParams(vmem_limit_bytes=VMEM_LIMIT, **kw)


def _pick(dim, cands):
    for c in cands:
        if dim % c == 0:
            return c
    return dim


def _place():
    return lax.axis_index("x"), lax.axis_index("y"), lax.axis_index("c")


def _slot(px, py, pc):
    return 4 * px + 2 * py + pc


class Gather:
    def __init__(self, tensors):
        self.operands = list(tensors)
        self.out_shape = [jax.ShapeDtypeStruct((N_DEV,) + t.shape, t.dtype) for t in tensors]
        n = len(tensors)
        self.scratch = [pltpu.SemaphoreType.DMA((n, N_DEV - 1)), pltpu.SemaphoreType.DMA((n, N_DEV - 1))]

    def _copies(self, t, x_ref, out_ref, scr):
        send, recv = scr
        x, y, c = _place()
        me, sibling = (x, y, c), (x, y, 1 - c)
        x_nbr, y_nbr, diag = (1 - x, y, c), (x, 1 - y, c), (1 - x, 1 - y, c)
        relay_from, relay_to = (x ^ (1 - c), y ^ c, c), (x ^ c, y ^ (1 - c), c)

        def copy(k, block, to, src=None):
            rows = out_ref.at[_slot(*block)]
            return pltpu.make_async_remote_copy(src_ref=rows if src is None else src, dst_ref=rows,
                                                send_sem=send.at[t, k], recv_sem=recv.at[t, k], device_id=to,
                                                device_id_type=pl.DeviceIdType.MESH)

        def other(block):
            return block[0], block[1], 1 - c

        sent = [(me, sibling, x_ref), (me, x_nbr, x_ref), (me, y_nbr, x_ref), (relay_from, relay_to, None),
                (x_nbr, sibling, None), (y_nbr, sibling, None), (diag, sibling, None)]
        landing = [sibling, x_nbr, y_nbr, diag, other(x_nbr), other(y_nbr), other(diag)]
        return (lambda k: copy(k, *sent[k])), (lambda k: copy(k, landing[k], me))

    def start(self, ins, outs, scr):
        for t, (x_ref, out_ref) in enumerate(zip(ins, outs)):
            give, _ = self._copies(t, x_ref, out_ref, scr)
            for k in range(3):
                give(k).start()

    def middle(self, ins, outs, scr):
        for t, (x_ref, out_ref) in enumerate(zip(ins, outs)):
            give, take = self._copies(t, x_ref, out_ref, scr)
            take(1).wait_recv()
            take(2).wait_recv()
            for k in (3, 4, 5):
                give(k).start()

    def finish(self, ins, outs, scr):
        for t, (x_ref, out_ref) in enumerate(zip(ins, outs)):
            give, take = self._copies(t, x_ref, out_ref, scr)
            take(3).wait_recv()
            give(6).start()
            for k in (0, 4, 5, 6):
                take(k).wait_recv()
            for k in range(N_DEV - 1):
                give(k).wait_send()

    def set_results(self, res):
        self.results = list(res)


N_CHIP = N_DEV // 2


class Halve:
    def __init__(self, tensors):
        self.operands = list(tensors)
        self.out_shape = [jax.ShapeDtypeStruct((N_CHIP,) + t.shape[1:], t.dtype) for t in tensors]
        n = len(tensors)
        self.scratch = [pltpu.SemaphoreType.DMA((n, N_CHIP)), pltpu.SemaphoreType.DMA((n, N_CHIP))]

    def _copies(self, t, x_ref, theirs_ref, scr):
        send, recv = scr
        x, y, c = _place()
        return [pltpu.make_async_remote_copy(
            src_ref=x_ref.at[2 * q + 1 - c], dst_ref=theirs_ref.at[q], send_sem=send.at[t, q],
            recv_sem=recv.at[t, q], device_id=(x, y, 1 - c), device_id_type=pl.DeviceIdType.MESH)
            for q in range(N_CHIP)]

    def start(self, ins, outs, scr):
        for t, (x_ref, theirs_ref) in enumerate(zip(ins, outs)):
            for give in self._copies(t, x_ref, theirs_ref, scr):
                give.start()

    def middle(self, ins, outs, scr):
        pass

    def finish(self, ins, outs, scr):
        for t, (x_ref, theirs_ref) in enumerate(zip(ins, outs)):
            for give in self._copies(t, x_ref, theirs_ref, scr):
                give.wait_recv()
                give.wait_send()

    def set_results(self, res):
        self.results = list(res)


class Spread:
    def __init__(self, tensors):
        self.operands = list(tensors)
        self.out_shape = [jax.ShapeDtypeStruct(t.shape, t.dtype) for t in tensors]
        n = len(tensors)
        self.scratch = [pltpu.SemaphoreType.DMA((n, N_CHIP - 1)), pltpu.SemaphoreType.DMA((n, N_CHIP - 1))]

    def _copies(self, t, y_ref, out_ref, scr, outgoing):
        send, recv = scr
        x, y, c = _place()
        copies = []
        for k in range(1, N_CHIP):
            px, py = x ^ (k >> 1), y ^ (k & 1)
            copies.append(pltpu.make_async_remote_copy(
                src_ref=y_ref.at[2 * px + py], dst_ref=out_ref.at[2 * x + y if outgoing else 2 * px + py],
                send_sem=send.at[t, k - 1], recv_sem=recv.at[t, k - 1], device_id=(px, py, c),
                device_id_type=pl.DeviceIdType.MESH))
        return copies

    def start(self, ins, outs, scr):
        for t, (y_ref, out_ref) in enumerate(zip(ins, outs)):
            for give in self._copies(t, y_ref, out_ref, scr, True):
                give.start()

    def middle(self, ins, outs, scr):
        pass

    def finish(self, ins, outs, scr):
        for t, (y_ref, out_ref) in enumerate(zip(ins, outs)):
            for take in self._copies(t, y_ref, out_ref, scr, False):
                take.wait_recv()
            for give in self._copies(t, y_ref, out_ref, scr, True):
                give.wait_send()

    def set_results(self, res):
        self.results = list(res)


class Joint:
    def __init__(self, parts):
        self.parts = list(parts)
        self.operands = [o for p in parts for o in p.operands]
        self.out_shape = [o for p in parts for o in p.out_shape]
        self.scratch = [o for p in parts for o in p.scratch]

    def _split(self, ins, outs, scr):
        i = o = s = 0
        for p in self.parts:
            ni, no, ns = len(p.operands), len(p.out_shape), len(p.scratch)
            yield p, ins[i:i + ni], outs[o:o + no], scr[s:s + ns]
            i, o, s = i + ni, o + no, s + ns

    def start(self, ins, outs, scr):
        for p, a, b, c in self._split(ins, outs, scr):
            p.start(a, b, c)

    def middle(self, ins, outs, scr):
        for p, a, b, c in self._split(ins, outs, scr):
            p.middle(a, b, c)

    def finish(self, ins, outs, scr):
        for p, a, b, c in self._split(ins, outs, scr):
            p.finish(a, b, c)

    def set_results(self, res):
        for p, _, part, _ in self._split([], list(res), []):
            p.set_results(part)


_HBM = pl.BlockSpec(memory_space=pltpu.HBM)


def run_alone(name, carry):
    n_in, n_out = len(carry.operands), len(carry.out_shape)

    def body(*refs):
        ins, outs, scr = refs[:n_in], refs[n_in:n_in + n_out], refs[n_in + n_out:]
        carry.start(ins, outs, scr)
        carry.middle(ins, outs, scr)
        carry.finish(ins, outs, scr)

    res = pl.pallas_call(body, name=name, in_specs=[_HBM] * n_in, out_specs=[_HBM] * n_out,
                         out_shape=carry.out_shape, scratch_shapes=carry.scratch)(*carry.operands)
    carry.set_results(list(res))


def _call(name, body, *, grid, in_specs, out_specs, out_shape, args, scratch=(), carry=None):
    in_specs, out_specs, out_shape, scratch = list(in_specs), list(out_specs), list(out_shape), list(scratch)
    if carry is None:
        return list(pl.pallas_call(body, name=name, grid=grid, in_specs=in_specs, out_specs=out_specs,
                                   out_shape=out_shape, scratch_shapes=scratch, compiler_params=_cparams())(*args))
    n_in, n_out, n_scr = len(in_specs), len(out_shape), len(scratch)
    c_in, c_out = len(carry.operands), len(carry.out_shape)

    def wrapped(*refs):
        ins, c_ins = refs[:n_in], refs[n_in:n_in + c_in]
        at = n_in + c_in
        outs, c_outs = refs[at:at + n_out], refs[at + n_out:at + n_out + c_out]
        at += n_out + c_out
        scr, c_scr = refs[at:at + n_scr], refs[at + n_scr:]
        first = functools.reduce(jnp.logical_and, [pl.program_id(d) == 0 for d in range(len(grid))])
        last = functools.reduce(jnp.logical_and, [pl.program_id(d) == grid[d] - 1 for d in range(len(grid))])

        @pl.when(first)
        def _():
            carry.start(c_ins, c_outs, c_scr)

        body(*ins, *outs, *scr)

        step, total = 0, 1
        for d in range(len(grid)):
            step, total = step * grid[d] + pl.program_id(d), total * grid[d]

        @pl.when(step == total // 2)
        def _():
            carry.middle(c_ins, c_outs, c_scr)

        @pl.when(last)
        def _():
            carry.finish(c_ins, c_outs, c_scr)

    res = pl.pallas_call(
        wrapped, name=name, grid=grid, in_specs=in_specs + [_HBM] * c_in, out_specs=out_specs + [_HBM] * c_out,
        out_shape=out_shape + carry.out_shape, scratch_shapes=scratch + carry.scratch, compiler_params=_cparams(),
    )(*args, *carry.operands)
    carry.set_results(list(res[n_out:]))
    return list(res[:n_out])


def R_(arr, w=None, off=0, stride=0):
    return ('r', arr, arr.shape[1] if w is None else w, off, stride)


def P_(arr, w=None, off=0, stride=0):
    return ('p', arr, arr.shape[1] if w is None else w, off, stride)


def rowmap(name, fn, ins, outs=None, *, tile, ncol=1, rows_inner=True, cts=None, wrt=None, gdt=None, cat=False,
           carry=None):
    rows = next(a.shape[0] for k, a, *_ in ins if k == 'r')
    nrow = rows // tile
    assert nrow * tile == rows
    grid = (ncol, nrow) if rows_inner else (nrow, ncol)

    def ij(g0, g1):
        return (g1, g0) if rows_inner else (g0, g1)

    def spec(kind, arr, w, off, stride):
        if kind == 'r':
            return pl.BlockSpec((tile, w), lambda g0, g1: (ij(g0, g1)[0], off + stride * ij(g0, g1)[1]))
        return pl.BlockSpec((arr.shape[0], w), lambda g0, g1: (0, off + stride * ij(g0, g1)[1]))

    ops = list(ins) + list(cts or [])
    in_specs = [spec(*o) for o in ops]
    n_in = len(ins)
    fwd = cts is None
    out_shape, out_specs, acc = [], [], []
    if fwd:
        for dt, w, stride in outs:
            out_shape.append(jax.ShapeDtypeStruct((rows, w * (ncol if stride else 1)), dt))
            out_specs.append(spec('r', None, w, 0, stride))
            acc.append(None)
    elif cat:
        widths = [ins[i][2] for i in wrt]
        assert ncol == 1 and all(ins[i][0] == 'r' for i in wrt)
        out_shape.append(jax.ShapeDtypeStruct((rows, sum(widths)), gdt))
        out_specs.append(spec('r', None, sum(widths), 0, 0))
    else:
        for n, i in enumerate(wrt):
            kind, arr, w, off, stride = ins[i]
            width = w * (ncol if stride else 1)
            if kind == 'r':
                out_shape.append(jax.ShapeDtypeStruct((rows, width), gdt[n]))
                out_specs.append(spec('r', None, w, 0, 1 if stride else 0))
                shared = stride == 0 and ncol > 1
                assert not shared or (not rows_inner and gdt[n] == F32)
                acc.append('col' if shared else None)
            else:
                out_shape.append(jax.ShapeDtypeStruct((arr.shape[0], width), F32))
                out_specs.append(spec('p', arr, w, 0, 1 if stride else 0))
                assert rows_inner or ncol == 1
                acc.append('row')

    def body(*refs):
        i, j = ij(pl.program_id(0), pl.program_id(1))
        vals = [r[...].astype(F32) for r in refs[:n_in]]
        out_refs = refs[len(ops):]
        if fwd:
            for o_ref, o in zip(out_refs, fn(*vals)):
                o_ref[...] = o.astype(o_ref.dtype)
            return

        def f(*d):
            full = list(vals)
            for n, idx in enumerate(wrt):
                full[idx] = d[n]
            return fn(*full)

        _, vjp = jax.vjp(f, *[vals[idx] for idx in wrt])
        grads = vjp(tuple(r[...].astype(F32) for r in refs[n_in:len(ops)]))
        if cat:
            o_ref, at = out_refs[0], 0
            for g in grads:
                o_ref[:, at:at + g.shape[1]] = g.astype(o_ref.dtype)
                at += g.shape[1]
            return
        for o_ref, g, a in zip(out_refs, grads, acc):
            if a is None:
                o_ref[...] = g.astype(o_ref.dtype)
            else:
                first = (i if a == 'row' else j) == 0

                @pl.when(first)
                def _(o_ref=o_ref):
                    o_ref[...] = jnp.zeros_like(o_ref)

                o_ref[...] += g

    return _call(name, body, grid=grid, in_specs=in_specs, out_specs=out_specs, out_shape=out_shape,
                 args=[o[1] for o in ops], carry=carry)


def _rms(x, g):
    return x * lax.rsqrt(jnp.mean(x * x, axis=-1, keepdims=True) + EPS) * g


_DIMS = {'nn': (((1,), (0,)), ((), ())), 'nt': (((1,), (1,)), ((), ())), 'tn': (((0,), (0,)), ((), ()))}


def mm(name, a, b, mode, *, jdim=None, out_dtype=F32, carry=None):
    a_list = list(a) if isinstance(a, (list, tuple)) else [a]
    b_list = list(b) if isinstance(b, (list, tuple)) else [b]
    a_order = ('k', 'm') if mode == 'tn' else ('m', 'k')
    b_order = ('n', 'k') if mode == 'nt' else ('k', 'n')
    size, blocks = {}, 1
    for arr, order in ((a_list[0], a_order), (b_list[0], b_order)):
        shape = arr.shape
        if jdim in order:
            blocks, shape = shape[0], shape[1:]
        for dname, extent in zip(order, shape):
            assert size.setdefault(dname, extent) == extent
    tile = {'m': _pick(size['m'], (1536, 1024, 512, 256, 128)), 'n': _pick(size['n'], (1536, 512, 256, 128)),
            'k': size['k'] if size['k'] <= 2048 else _pick(size['k'], (2304, 2048, 1024, 512, 256, 128))}
    if jdim is not None:
        tile[jdim] = size[jdim]
    if a_list[0].dtype == F32 and tile['k'] * tile['m'] > (1 << 20):
        tile['m'] = _pick(size['m'], (512, 256, 128))
    grid = tuple(blocks if d == jdim else size[d] // tile[d] for d in ('m', 'n', 'k'))
    nk = grid[2]

    def spec(order):
        shape = tuple(tile[d] for d in order)

        def imap(i, j, k):
            g = {'m': i, 'n': j, 'k': k}
            idx = tuple(0 if d == jdim else g[d] for d in order)
            return ((g[jdim],) + idx) if jdim in order else idx

        return pl.BlockSpec(((None,) + shape) if jdim in order else shape, imap)

    dims = _DIMS[mode]
    nt = len(a_list)

    def product(refs):
        acc = None
        for a_ref, b_ref in zip(refs[:nt], refs[nt:2 * nt]):
            p = lax.dot_general(a_ref[...].astype(BF16), b_ref[...].astype(BF16), dims, preferred_element_type=F32)
            acc = p if acc is None else acc + p
        return acc

    def body_once(*refs):
        refs[2 * nt][...] = product(refs).astype(refs[2 * nt].dtype)

    def body_acc(*refs):
        o_ref, acc_ref = refs[2 * nt], refs[2 * nt + 1]
        k = pl.program_id(2)

        @pl.when(k == 0)
        def _():
            acc_ref[...] = jnp.zeros_like(acc_ref)

        acc_ref[...] += product(refs)

        @pl.when(k == nk - 1)
        def _():
            o_ref[...] = acc_ref[...].astype(o_ref.dtype)

    out_dims = (size['m'], size['n'])
    out_shape = jax.ShapeDtypeStruct(((blocks,) + out_dims) if jdim in ('m', 'n') else out_dims, out_dtype)
    return _call(name, body_once if nk == 1 else body_acc, grid=grid,
                 in_specs=[spec(a_order)] * nt + [spec(b_order)] * nt, out_specs=[spec(('m', 'n'))],
                 out_shape=[out_shape], args=a_list + b_list,
                 scratch=[] if nk == 1 else [pltpu.VMEM((tile['m'], tile['n']), F32)], carry=carry)[0]


def _probs(q, k, i, tq, scale, causal):
    s = lax.dot_general(q, k, _DIMS['nt'], preferred_element_type=F32) * scale
    if causal:
        shape = s.shape
        q_chunk = jnp.right_shift(i * tq + lax.broadcasted_iota(jnp.int32, shape, 0), CHUNK_SHIFT)
        k_chunk = jnp.right_shift(lax.broadcasted_iota(jnp.int32, shape, 1), CHUNK_SHIFT)
        s = jnp.where(k_chunk <= q_chunk, s, -jnp.inf)
    e = jnp.exp(s - jnp.max(s, axis=-1, keepdims=True))
    return e, jnp.sum(e, axis=-1, keepdims=True)


def _per_prefix(work, i, tq, sq, sk, causal):
    if not causal:
        work(sk)
        return
    assert sq == sk and tq % CHUNK == 0
    for j in range(sq // tq):
        @pl.when(i == j)
        def _(j=j):
            work((j + 1) * tq)


def _attn_specs(tq, sk, dq, dv, koff, kstride, voff, vstride):
    return [pl.BlockSpec((tq, dq), lambda h, i: (i, h)),
            pl.BlockSpec((sk, dq), lambda h, i: (0, koff + kstride * h)),
            pl.BlockSpec((sk, dv), lambda h, i: (0, voff + vstride * h))]


def attn_fwd(name, q, k, v, *, heads, dq, dv, koff, kstride, voff, vstride, scale, causal, carry=None):
    sq, sk = q.shape[0], k.shape[0]
    tq = min(sq, 256)

    def body(q_ref, k_ref, v_ref, o_ref):
        i = pl.program_id(1)

        def work(keys):
            e, l = _probs(q_ref[...].astype(BF16), k_ref[0:keys, :].astype(BF16), i, tq, scale, causal)
            o = jnp.dot(e.astype(BF16), v_ref[0:keys, :].astype(BF16), preferred_element_type=F32)
            o_ref[...] = o / l

        _per_prefix(work, i, tq, sq, sk, causal)

    return _call(name, body, grid=(heads, sq // tq),
                 in_specs=_attn_specs(tq, sk, dq, dv, koff, kstride, voff, vstride),
                 out_specs=[pl.BlockSpec((tq, dv), lambda h, i: (i, h))],
                 out_shape=[jax.ShapeDtypeStruct((sq, heads * dv), F32)], args=[q, k, v], carry=carry)[0]


def attn_bwd(name, q, k, v, do, *, heads, dq, dv, koff, kstride, voff, vstride, scale, causal, carry=None):
    sq, sk = q.shape[0], k.shape[0]
    tq = min(sq, 256)

    def body(q_ref, k_ref, v_ref, do_ref, dq_ref, dk_ref, dv_ref):
        i = pl.program_id(1)

        @pl.when(i == 0)
        def _():
            dk_ref[...] = jnp.zeros_like(dk_ref)
            dv_ref[...] = jnp.zeros_like(dv_ref)

        def work(keys):
            qb, kb, vb = q_ref[...].astype(BF16), k_ref[0:keys, :].astype(BF16), v_ref[0:keys, :].astype(BF16)
            dob = do_ref[...].astype(BF16)
            e, l = _probs(qb, kb, i, tq, scale, causal)
            p = e / l
            dp = lax.dot_general(dob, vb, _DIMS['nt'], preferred_element_type=F32)
            ds = (p * (dp - jnp.sum(dp * p, axis=-1, keepdims=True)) * scale).astype(BF16)
            dv_ref[0:keys, :] += lax.dot_general(p.astype(BF16), dob, _DIMS['tn'], preferred_element_type=F32)
            dk_ref[0:keys, :] += lax.dot_general(ds, qb, _DIMS['tn'], preferred_element_type=F32)
            dq_ref[...] = jnp.dot(ds, kb, preferred_element_type=F32)

        _per_prefix(work, i, tq, sq, sk, causal)

    return _call(
        name, body, grid=(heads, sq // tq),
        in_specs=_attn_specs(tq, sk, dq, dv, koff, kstride, voff, vstride) + [pl.BlockSpec((tq, dv), lambda h, i: (i, h))],
        out_specs=[pl.BlockSpec((tq, dq), lambda h, i: (i, h)), pl.BlockSpec((sk, dq), lambda h, i: (0, h)),
                   pl.BlockSpec((sk, dv), lambda h, i: (0, h))],
        out_shape=[jax.ShapeDtypeStruct((sq, heads * dq), F32), jax.ShapeDtypeStruct((sk, heads * dq), F32),
                   jax.ShapeDtypeStruct((sk, heads * dv), F32)],
        args=[q, k, v, do], carry=carry)


def _hg_chunk(q, k, g, v, state):
    c = q.shape[0]
    row = lax.broadcasted_iota(jnp.int32, (c, c), 0)
    col = lax.broadcasted_iota(jnp.int32, (c, c), 1)
    tril = (col <= row).astype(F32)
    b = jnp.dot(tril, g, precision=lax.Precision.HIGHEST, preferred_element_type=F32)
    rows = lax.broadcasted_iota(jnp.int32, (c, 1), 0)
    o = jnp.dot((q * jnp.exp(b)).astype(BF16), state.astype(BF16), preferred_element_type=F32)
    t3 = lax.broadcasted_iota(jnp.int32, (SUB, SUB, 1), 0)
    s3 = lax.broadcasted_iota(jnp.int32, (SUB, SUB, 1), 1)
    parts = []
    for n in range(c // SUB):
        lo = n * SUB
        qn, kn, bn, vn = q[lo:lo + SUB], k[lo:lo + SUB], b[lo:lo + SUB], v[lo:lo + SUB]
        decay = jnp.exp(jnp.where(s3 <= t3, bn[:, None, :] - bn[None, :, :], -jnp.inf))
        sc = jnp.sum(qn[:, None, :] * kn[None, :, :] * decay, axis=-1)
        on = jnp.dot(sc.astype(BF16), vn.astype(BF16), preferred_element_type=F32)
        if n > 0:
            ref = jnp.sum(jnp.where(rows == lo - 1, b, 0.0), axis=0, keepdims=True)
            qd = (qn * jnp.exp(bn - ref)).astype(BF16)
            kd = (k[:lo] * jnp.exp(ref - b[:lo])).astype(BF16)
            so = lax.dot_general(qd, kd, _DIMS['nt'], preferred_element_type=F32)
            on = on + jnp.dot(so.astype(BF16), v[:lo].astype(BF16), preferred_element_type=F32)
        parts.append(on)
    o = o + jnp.concatenate(parts, axis=0)
    b_last = jnp.sum(g, axis=0, keepdims=True)
    ones = jnp.ones((c, 1), F32)
    b_last_col = lax.dot_general(g, ones, _DIMS['tn'], precision=lax.Precision.HIGHEST, preferred_element_type=F32)
    kd = (k * jnp.exp(b_last - b)).astype(BF16)
    new_state = jnp.exp(b_last_col) * state + lax.dot_general(kd, v.astype(BF16), _DIMS['tn'],
                                                              preferred_element_type=F32)
    return o, new_state


def _hg_chunk_bwd(q, k, g, v, state, do, dnew):
    c, kw = q.shape
    hi = lax.Precision.HIGHEST
    row = lax.broadcasted_iota(jnp.int32, (c, c), 0)
    col = lax.broadcasted_iota(jnp.int32, (c, c), 1)
    b = jnp.dot((col <= row).astype(F32), g, precision=hi, preferred_element_type=F32)
    rows = lax.broadcasted_iota(jnp.int32, (c, 1), 0)
    b_last = jnp.sum(g, axis=0, keepdims=True)
    b_last_col = lax.dot_general(g, jnp.ones((c, 1), F32), _DIMS['tn'], precision=hi, preferred_element_type=F32)
    eb, to_end = jnp.exp(b), jnp.exp(b_last - b)
    dob, vb, dnb = do.astype(BF16), v.astype(BF16), dnew.astype(BF16)
    k_end = (k * to_end).astype(BF16)
    dq = eb * lax.dot_general(dob, state.astype(BF16), _DIMS['nt'], preferred_element_type=F32)
    dk = to_end * lax.dot_general(vb, dnb, _DIMS['nt'], preferred_element_type=F32)
    dv = jnp.dot(k_end, dnb, preferred_element_type=F32)
    dstate = jnp.exp(b_last_col) * dnew + lax.dot_general((q * eb).astype(BF16), dob, _DIMS['tn'],
                                                          preferred_element_type=F32)
    new_state = jnp.exp(b_last_col) * state + lax.dot_general(k_end, vb, _DIMS['tn'], preferred_element_type=F32)
    db_end = lax.dot_general(jnp.ones((1, dnew.shape[1]), F32), dnew * new_state, _DIMS['nt'], precision=hi,
                             preferred_element_type=F32)
    t3 = lax.broadcasted_iota(jnp.int32, (SUB, SUB, 1), 0)
    s3 = lax.broadcasted_iota(jnp.int32, (SUB, SUB, 1), 1)
    dq_rows, dk_rows, dv_rows = [], [], []
    for n in range(c // SUB):
        lo = n * SUB
        qn, kn, bn, vn, don = q[lo:lo + SUB], k[lo:lo + SUB], b[lo:lo + SUB], vb[lo:lo + SUB], dob[lo:lo + SUB]
        decay = jnp.exp(jnp.where(s3 <= t3, bn[:, None, :] - bn[None, :, :], -jnp.inf))
        sc = jnp.sum(qn[:, None, :] * kn[None, :, :] * decay, axis=-1)
        pull = lax.dot_general(don, vn, _DIMS['nt'], preferred_element_type=F32)[:, :, None] * decay
        dqn = jnp.sum(pull * kn[None, :, :], axis=1)
        dk_rows.append(jnp.sum(pull * qn[:, None, :], axis=0))
        dv_rows.append(lax.dot_general(sc.astype(BF16), don, _DIMS['tn'], preferred_element_type=F32))
        if n > 0:
            ref = jnp.sum(jnp.where(rows == lo - 1, b, 0.0), axis=0, keepdims=True)
            up, down = jnp.exp(bn - ref), jnp.exp(ref - b[:lo])
            qd, kd = (qn * up).astype(BF16), (k[:lo] * down).astype(BF16)
            so = lax.dot_general(qd, kd, _DIMS['nt'], preferred_element_type=F32).astype(BF16)
            ao = lax.dot_general(don, vb[:lo], _DIMS['nt'], preferred_element_type=F32).astype(BF16)
            dqn = dqn + up * jnp.dot(ao, kd, preferred_element_type=F32)
            rest = jnp.zeros((c - lo, kw), F32)
            dk = dk + jnp.concatenate([down * lax.dot_general(ao, qd, _DIMS['tn'], preferred_element_type=F32),
                                       rest], axis=0)
            dv = dv + jnp.concatenate([lax.dot_general(so, don, _DIMS['tn'], preferred_element_type=F32), rest],
                                      axis=0)
        dq_rows.append(dqn)
    dq = dq + jnp.concatenate(dq_rows, axis=0)
    dk = dk + jnp.concatenate(dk_rows, axis=0)
    dv = dv + jnp.concatenate(dv_rows, axis=0)
    db = q * dq - k * dk + jnp.where(rows == c - 1, db_end, 0.0)
    dg = jnp.dot((col >= row).astype(F32), db, precision=hi, preferred_element_type=F32)
    return dq, dk, dg, dv, dstate


def hg_scan_fwd(name, q, k, g, u, v_off, carry=None):
    s = q.shape[0]
    n = s // CHUNK

    def body(q_ref, k_ref, g_ref, v_ref, o_ref, st_ref, state):
        @pl.when(pl.program_id(1) == 0)
        def _():
            state[...] = jnp.zeros_like(state)

        for j in range(HG_GROUP):
            cols = slice(j * HEAD_W, (j + 1) * HEAD_W)
            st = state[j]
            st_ref[j] = st
            o, new = _hg_chunk(q_ref[:, cols], k_ref[:, cols], g_ref[:, cols], v_ref[:, cols], st)
            o_ref[:, cols] = o
            state[j] = new

    wide = HG_GROUP * HEAD_W
    blk = pl.BlockSpec((CHUNK, wide), lambda h, c: (c, h))
    return _call(
        name, body, grid=(HG_HEADS // HG_GROUP, n),
        in_specs=[blk, blk, blk, pl.BlockSpec((CHUNK, wide), lambda h, c: (c, v_off // HG_GROUP + h))],
        out_specs=[blk, pl.BlockSpec((HG_GROUP, None, HEAD_W, HEAD_W), lambda h, c: (h, c, 0, 0))],
        out_shape=[jax.ShapeDtypeStruct((s, HG_HEADS * HEAD_W), F32),
                   jax.ShapeDtypeStruct((HG_HEADS, n, HEAD_W, HEAD_W), F32)],
        scratch=[pltpu.VMEM((HG_GROUP, HEAD_W, HEAD_W), F32)], args=[q, k, g, u], carry=carry)


def hg_scan_bwd(name, q, k, g, u, v_off, states, do, carry=None):
    s = q.shape[0]
    n = s // CHUNK

    def body(q_ref, k_ref, g_ref, v_ref, st_ref, do_ref, dq_ref, dk_ref, dg_ref, dv_ref, dstate):
        @pl.when(pl.program_id(1) == 0)
        def _():
            dstate[...] = jnp.zeros_like(dstate)

        for j in range(HG_GROUP):
            cols = slice(j * HEAD_W, (j + 1) * HEAD_W)
            dq, dk, dg, dv, dst = _hg_chunk_bwd(q_ref[:, cols], k_ref[:, cols], g_ref[:, cols], v_ref[:, cols],
                                                st_ref[j], do_ref[:, cols], dstate[j])
            dq_ref[:, cols] = dq
            dk_ref[:, cols] = dk
            dg_ref[:, cols] = dg
            dv_ref[:, cols] = dv
            dstate[j] = dst

    wide = HG_GROUP * HEAD_W
    blk = pl.BlockSpec((CHUNK, wide), lambda h, c: (n - 1 - c, h))
    out = jax.ShapeDtypeStruct((s, HG_HEADS * HEAD_W), F32)
    return _call(
        name, body, grid=(HG_HEADS // HG_GROUP, n),
        in_specs=[blk, blk, blk, pl.BlockSpec((CHUNK, wide), lambda h, c: (n - 1 - c, v_off // HG_GROUP + h)),
                  pl.BlockSpec((HG_GROUP, None, HEAD_W, HEAD_W), lambda h, c: (h, n - 1 - c, 0, 0)), blk],
        out_specs=[blk, blk, blk, blk], out_shape=[out, out, out, out],
        scratch=[pltpu.VMEM((HG_GROUP, HEAD_W, HEAD_W), F32)], args=[q, k, g, u, states, do], carry=carry)


def _tile2d(rows, cols, limit):
    for tr in range(min(rows, limit // cols) // 16 * 16, 0, -16):
        if rows % tr == 0:
            return tr, cols
    for tc in (2048, 1024, 512, 256, 128):
        if cols % tc == 0 and rows * tc <= limit:
            return rows, tc
    return rows, cols


def pair_sum(name, blocks, theirs):
    _, rows, cols = theirs.shape
    tr, tc = _tile2d(rows, cols, 1 << 20)

    def body(a_ref, b_ref, o_ref):
        mine = jnp.where(lax.axis_index("c") == 0, a_ref[0].astype(F32), a_ref[1].astype(F32))
        o_ref[...] = (mine + b_ref[...].astype(F32)).astype(o_ref.dtype)

    blk = pl.BlockSpec((None, tr, tc), lambda q, i, j: (q, i, j))
    return _call(name, body, grid=(N_CHIP, rows // tr, cols // tc),
                 in_specs=[pl.BlockSpec((None, 2, tr, tc), lambda q, i, j: (q, 0, i, j)), blk], out_specs=[blk],
                 out_shape=[jax.ShapeDtypeStruct(theirs.shape, theirs.dtype)],
                 args=[blocks.reshape(N_CHIP, 2, rows, cols), theirs])[0]


def adamw(name, landed, w, m, v):
    rows, cols = w.shape
    slots = landed.shape[0]
    tr, tc = _tile2d(rows, cols, 1 << 18)

    def body(l_ref, w_ref, m_ref, v_ref, g_ref, d_ref, nm_ref, nv_ref):
        gv = l_ref[0].astype(F32)
        for s in range(1, slots):
            gv = gv + l_ref[s].astype(F32)
        nm = ADAM_B1 * m_ref[...] + (1.0 - ADAM_B1) * gv
        nv = ADAM_B2 * v_ref[...] + (1.0 - ADAM_B2) * jnp.square(gv)
        m_hat = nm / (1.0 - ADAM_B1 ** ADAM_STEP)
        v_hat = nv / (1.0 - ADAM_B2 ** ADAM_STEP)
        g_ref[...] = gv
        d_ref[...] = -ADAM_LR * (m_hat / (jnp.sqrt(v_hat) + ADAM_EPS) + ADAM_WD * w_ref[...])
        nm_ref[...] = nm
        nv_ref[...] = nv

    blk = pl.BlockSpec((tr, tc), lambda i, j: (i, j))
    out = jax.ShapeDtypeStruct((rows, cols), F32)
    return _call(name, body, grid=(rows // tr, cols // tc),
                 in_specs=[pl.BlockSpec((slots, tr, tc), lambda i, j: (0, i, j)), blk, blk, blk],
                 out_specs=[blk] * 4, out_shape=[out] * 4, args=[landed, w, m, v])


def _swap_halves(pe):
    half = QK_ROPE // 2
    return jnp.concatenate([-pe[..., half:], pe[..., :half]], axis=-1)


def _unswap_halves(dsw):
    half = QK_ROPE // 2
    return jnp.concatenate([dsw[..., half:], -dsw[..., :half]], axis=-1)


def _w_in_ext(wt):
    kpe = wt[9216:9280]
    z = jnp.zeros((HEAD_W - QK_ROPE, wt.shape[1]), wt.dtype)
    pad = jnp.zeros((U_PAD, wt.shape[1]), wt.dtype)
    return jnp.concatenate([wt[:9216], wt[9280:], kpe, z, _swap_halves(kpe.T).T, z, pad], axis=0)


def _w_in_grad(dt):
    dkpe = dt[13312:13376] + _unswap_halves(dt[13440:13504].T).T
    return jnp.concatenate([dt[:9216], dkpe, dt[9216:13312]], axis=0)


def _w_q_ext(w):
    w3 = w.reshape(Q_LORA, MLA_HEADS, MLA_QK)
    pe = w3[:, :, HEAD_W:]
    z = jnp.zeros((Q_LORA, MLA_HEADS, HEAD_W - QK_ROPE), w.dtype)
    wide = MLA_HEADS * HEAD_W
    return jnp.concatenate([w3[:, :, :HEAD_W].reshape(Q_LORA, wide),
                            jnp.concatenate([pe, z], axis=2).reshape(Q_LORA, wide),
                            jnp.concatenate([_swap_halves(pe), z], axis=2).reshape(Q_LORA, wide)], axis=1)


def _w_q_grad(parts):
    d3 = [p.reshape(Q_LORA, MLA_HEADS, HEAD_W) for p in parts]
    dpe = d3[1][:, :, :QK_ROPE] + _unswap_halves(d3[2][:, :, :QK_ROPE])
    return jnp.concatenate([d3[0], dpe], axis=2).reshape(Q_LORA, MLA_HEADS * MLA_QK)


def _hg_prep(f_raw, q_hg, logits):
    lb = jax.nn.softmax(logits, axis=0)[0:1, :]
    log_f = jnp.logaddexp(jnp.log(lb), jnp.log1p(-lb) + jax.nn.log_sigmoid(f_raw))
    k_in = (1.0 - lb) * jax.nn.sigmoid(-f_raw)
    return log_f, k_in, jax.nn.silu(q_hg)


def _rope(q_nope, q_pe, q_sw, k_nope, k_pe, k_sw, cos, sin):
    qf = jnp.concatenate([q_nope, q_pe * cos + q_sw * sin], axis=1)
    kf = jnp.concatenate([k_nope, k_pe * cos + k_sw * sin], axis=1)
    return qf, kf


def _step(a):
    x, mem, target = a['x'][0], a['mem'][0], a['loss_target'][0]
    s, d = x.shape
    nm = mem.shape[0]
    ff = N_DEV * a['ffn1_w_gate'].shape[-1]
    cs = ff // N_DEV
    tr = min(s, 256)
    tp = min(s, 128)
    th = s
    ta = 1024

    flipped = ('ffn1_w_gate', 'ffn1_w_up', 'ffn2_w_gate', 'ffn2_w_up', 'w_in')

    def shard(n, moment=''):
        t = a[moment + n][0]
        return t.T if n in flipped else t

    def unflip(n, t):
        return (t.T if n in flipped else t).reshape(a[n].shape)

    bf = {n: shard(n).astype(BF16) for n in BIG}
    half_in = 832
    bf['w_in#0'], bf['w_in#1'] = bf['w_in'][:half_in], bf['w_in'][half_in:]
    gat, full = {}, {}

    my_chip = 2 * lax.axis_index("x") + lax.axis_index("y")
    me = 2 * my_chip + lax.axis_index("c")

    def gathered(names, carry):
        for n, g8 in zip(names or [], carry.results if carry else []):
            r, c = bf[n].shape
            g8 = lax.dynamic_update_index_in_dim(g8, bf[n], me, 0)
            gat[n] = g8
            if not n.startswith(('ffn', 'w_in')):
                full[n] = g8.transpose(1, 0, 2).reshape(r, N_DEV * c) if BIG[n] else g8.reshape(N_DEV * r, c)

    xa_names = ['xa_w_q', 'xa_w_k', 'xa_w_v', 'xa_w_o']
    ffn2_names = ['ffn2_w_gate', 'ffn2_w_up', 'ffn2_w_down']
    first = Gather([bf['ffn1_w_gate']])
    run_alone("gather_first", first)
    gathered(['ffn1_w_gate'], first)

    inv_freq = 1.0 / (ROPE_THETA ** (jnp.arange(0, QK_ROPE, 2, dtype=F32) / QK_ROPE))
    ang = a['positions'][0].astype(F32)[:, None] * inv_freq
    zero = jnp.zeros((s, HEAD_W - QK_ROPE), F32)
    cos = jnp.concatenate([jnp.cos(ang), jnp.cos(ang), zero], axis=1)
    sin = jnp.concatenate([jnp.sin(ang), jnp.sin(ang), zero], axis=1)

    gs = {}
    gb = {}

    def by_rows(g):
        return g.reshape(N_DEV, g.shape[0] // N_DEV, g.shape[1])

    def by_cols(g):
        return g.reshape(g.shape[0], N_DEV, g.shape[1] // N_DEV).transpose(1, 0, 2)

    pre_fn = lambda xv, g: (_rms(xv, g),)
    pre_res_fn = lambda xv, g: (_rms(xv, g), xv)

    def pre_norm(tag, x_in, g):
        return rowmap(tag + "_pre", pre_fn, [R_(x_in), P_(g)], [(BF16, x_in.shape[1], 0)], tile=tr)[0]

    def pre_norm_bwd(tag, x_in, g, dh, d_out, carry=None):
        return rowmap(tag + "_pre_bwd", pre_res_fn, [R_(x_in), P_(g)], tile=tr, cts=[R_(dh), R_(d_out)],
                      wrt=[0, 1], gdt=[F32, F32], carry=carry)

    def post_fn(weight):
        return lambda xv, y, g: (xv + weight * _rms(y, g),)

    def post_norm(tag, x_in, y, g, weight):
        return rowmap(tag + "_post", post_fn(weight), [R_(x_in), R_(y), P_(g)], [(F32, d, 0)], tile=tr)[0]

    def post_norm_bwd(tag, x_in, y, g, weight, d_out):
        return rowmap(tag + "_post_bwd", post_fn(weight), [R_(x_in), R_(y), P_(g)], tile=tr, cts=[R_(d_out)],
                      wrt=[1, 2], gdt=[BF16, F32])

    act_fn = lambda av, bv: (jax.nn.silu(av) * bv,)

    def fetch(plan, key):
        names = plan.get(key)
        return (names, Gather([bf[n] for n in names])) if names else (None, None)

    def send(plan, key):
        acts = plan.get(key)
        if not acts:
            return None, None
        parts = []
        for kind, names in acts:
            if kind == 'spread':
                for n in names:
                    chip_sums[n] = pair_sum("pair_" + n.replace('#', '_'), gb[n], halves[n])
            parts.append(Halve([gb[n] for n in names]) if kind == 'halve' else Spread([chip_sums[n] for n in names]))
        return acts, Joint(parts)

    def ffn_fwd(tag, x_in, plan):
        h = pre_norm(tag, x_in, a[tag + '_pre_g'])
        names, carry = fetch(plan, 'gate')
        av = mm(tag + "_gate", h, gat[tag + '_w_gate'], 'nt', jdim='n', out_dtype=BF16, carry=carry)
        gathered(names, carry)
        names, carry = fetch(plan, 'up')
        bv = mm(tag + "_up", h, gat[tag + '_w_up'], 'nt', jdim='n', out_dtype=BF16, carry=carry)
        gathered(names, carry)
        a2, b2 = av.reshape(N_DEV * s, cs), bv.reshape(N_DEV * s, cs)
        names, carry = fetch(plan, 'act')
        z = rowmap(tag + "_act", act_fn, [R_(a2), R_(b2)], [(BF16, cs, 0)], tile=ta, carry=carry)[0]
        z = z.reshape(N_DEV, s, cs)
        gathered(names, carry)
        names, carry = fetch(plan, 'dn')
        y = mm(tag + "_dn", z, gat[tag + '_w_down'], 'nn', jdim='k', carry=carry)
        gathered(names, carry)
        return post_norm(tag, x_in, y, a[tag + '_post_g'], 0.5), (x_in, h, a2, b2, z, y)

    def ffn_bwd(tag, res, d_out, plan):
        x_in, h, a2, b2, z, y = res
        dy, gs[tag + '_post_g'] = post_norm_bwd(tag, x_in, y, a[tag + '_post_g'], 0.5, d_out)
        acts, carry = send(plan, 'dn_dx')
        dz = mm(tag + "_dn_dx", dy, gat[tag + '_w_down'], 'nt', jdim='n', carry=carry)
        exchanged(acts, carry)
        acts, carry = send(plan, 'dn_dw')
        gb[tag + '_w_down'] = mm(tag + "_dn_dw", z, dy, 'tn', jdim='m', out_dtype=BF16, carry=carry)
        exchanged(acts, carry)
        acts, carry = send(plan, 'act_bwd')
        da, db = rowmap(tag + "_act_bwd", act_fn, [R_(a2), R_(b2)], tile=ta, cts=[R_(dz.reshape(N_DEV * s, cs))],
                        wrt=[0, 1], gdt=[BF16, BF16], carry=carry)
        exchanged(acts, carry)
        da, db = da.reshape(N_DEV, s, cs), db.reshape(N_DEV, s, cs)
        acts, carry = send(plan, 'gate_dw')
        gb[tag + '_w_gate'] = mm(tag + "_gate_dw", da, h, 'tn', jdim='m', out_dtype=BF16, carry=carry)
        exchanged(acts, carry)
        acts, carry = send(plan, 'up_dw')
        gb[tag + '_w_up'] = mm(tag + "_up_dw", db, h, 'tn', jdim='m', out_dtype=BF16, carry=carry)
        exchanged(acts, carry)
        acts, carry = send(plan, 'gu_dx')
        dh = mm(tag + "_gu_dx", [da, db], [gat[tag + '_w_gate'], gat[tag + '_w_up']], 'nn', jdim='k', carry=carry)
        exchanged(acts, carry)
        names, carry = send(plan, 'pre_bwd')
        d_in, gs[tag + '_pre_g'] = pre_norm_bwd(tag, x_in, a[tag + '_pre_g'], dh, d_out, carry)
        exchanged(names, carry)
        return d_in

    hg_out_fn = lambda o, og, g: (_rms(o, g) * jax.nn.silu(og),)
    mla_norm_fn = lambda cq, ckv, gq, gkv: (_rms(cq, gq), _rms(ckv, gkv))
    gate_fn = lambda ga, gb, ya, yb: (jax.nn.sigmoid(ga) * ya + jax.nn.sigmoid(gb) * yb,)
    mla = dict(heads=MLA_HEADS, dq=2 * HEAD_W, dv=HEAD_W, koff=0, kstride=1, voff=1, vstride=2,
               scale=MLA_QK ** -0.5, causal=True)

    def mix_fwd(x_in, plan):
        w_in = jnp.concatenate([gat['w_in#0'], gat['w_in#1']], axis=1)
        w_in = _w_in_ext(w_in.reshape(-1, w_in.shape[2]))
        h = pre_norm("mix", x_in, a['mix_pre_g'])
        names, carry = fetch(plan, 'in')
        u = mm("mix_in", h, w_in, 'nt', carry=carry)
        gathered(names, carry)
        w_q, w_kv = _w_q_ext(full['mla_w_q_up']), full['mla_w_kv_up']
        hg_ins = [R_(u, 2048, 1), R_(u, 2048, 0), P_(a['hgrn_lb_logits'])]
        log_f, k_in, q_in = rowmap("hg_prep", _hg_prep, hg_ins, [(F32, 2048, 0)] * 3, tile=tp)
        names, carry = fetch(plan, 'scan')
        o_a, states = hg_scan_fwd("hg_scan", q_in, k_in, log_f, u, 32, carry=carry)
        gathered(names, carry)
        out_ins = [R_(o_a, HEAD_W, 0, 1), R_(u, HEAD_W, 48, 1), P_(a['hg_norm_g'], HEAD_W, 0, 1)]
        oag = rowmap("hg_out", hg_out_fn, out_ins, [(BF16, HEAD_W, 1)], tile=th, ncol=HG_HEADS)[0]
        y_a = mm("mix_a", oag, full['w_branch_a'], 'nn')
        norm_ins = [R_(u, 512, 16), R_(u, 512, 17), P_(a['mla_q_norm_g']), P_(a['mla_kv_norm_g'])]
        cqn, ckvn = rowmap("mla_norm", mla_norm_fn, norm_ins, [(BF16, 512, 0)] * 2, tile=tr)
        q_all = mm("mla_qup", cqn, w_q, 'nn')
        kv = mm("mla_kvup", ckvn, w_kv, 'nn')
        rope_ins = [R_(q_all, HEAD_W, 0, 1), R_(q_all, HEAD_W, 16, 1), R_(q_all, HEAD_W, 32, 1),
                    R_(kv, HEAD_W, 0, 2), R_(u, HEAD_W, 104, 0), R_(u, HEAD_W, 105, 0), R_(cos), R_(sin)]
        qf, kf = rowmap("mla_rope", _rope, rope_ins, [(BF16, 2 * HEAD_W, 1)] * 2, tile=th, ncol=MLA_HEADS,
                        rows_inner=False)
        names, carry = fetch(plan, 'attn')
        o_b = attn_fwd("mla_attn", qf, kf, kv, carry=carry, **mla)
        gathered(names, carry)
        y_b = mm("mix_b", o_b, full['w_branch_b'], 'nn')
        gate_ins = [R_(u, 1024, 9, 1), R_(u, 1024, 11, 1), R_(y_a, 1024, 0, 1), R_(y_b, 1024, 0, 1)]
        y = rowmap("mix_gate", gate_fn, gate_ins, [(BF16, 1024, 1)], tile=tr, ncol=2)[0]
        yo = mm("mix_out", y, full['w_out'], 'nn')
        res = (x_in, h, hg_ins, q_in, k_in, log_f, u, states, out_ins, oag, norm_ins, cqn, ckvn, rope_ins, qf, kf,
               kv, o_b, gate_ins, y, yo, w_in, w_q, w_kv)
        return post_norm("mix", x_in, yo, a['mix_post_g'], 1.0), res

    def mix_bwd(res, d_out, plan):
        (x_in, h, hg_ins, q_in, k_in, log_f, u, states, out_ins, oag, norm_ins, cqn, ckvn, rope_ins, qf, kf, kv,
         o_b, gate_ins, y, yo, w_in, w_q, w_kv) = res
        dyo, gs['mix_post_g'] = post_norm_bwd("mix", x_in, yo, a['mix_post_g'], 1.0, d_out)
        dy = mm("mix_out_dx", dyo, full['w_out'], 'nt')
        gb['w_out'] = by_rows(mm("mix_out_dw", y, dyo, 'tn', out_dtype=BF16))
        dga, dgb, dya, dyb = rowmap("mix_gate_bwd", gate_fn, gate_ins, tile=tr, ncol=2, cts=[R_(dy, 1024, 0, 1)],
                                    wrt=[0, 1, 2, 3], gdt=[BF16] * 4)
        gb['w_branch_b'] = by_rows(mm("mix_b_dw", o_b, dyb, 'tn', out_dtype=BF16))
        do_b = mm("mix_b_dx", dyb, full['w_branch_b'], 'nt')
        names, carry = send(plan, 'attn_bwd')
        dqf, dkf, dv = attn_bwd("mla_attn_bwd", qf, kf, kv, do_b, carry=carry, **mla)
        exchanged(names, carry)
        dqn, dqp, dqs, dkn, dkpe, dksw = rowmap(
            "mla_rope_bwd", _rope, rope_ins, tile=th, ncol=MLA_HEADS, rows_inner=False,
            cts=[R_(dqf, 2 * HEAD_W, 0, 1), R_(dkf, 2 * HEAD_W, 0, 1)], wrt=[0, 1, 2, 3, 4, 5],
            gdt=[BF16, BF16, BF16, BF16, F32, F32])
        wide = MLA_HEADS * HEAD_W
        dq_parts = [dqn, dqp, dqs]
        gb['mla_w_q_up'] = by_cols(_w_q_grad([mm("mla_qup_dw%d" % n, cqn, dq_parts[n], 'tn', out_dtype=BF16)
                                              for n in range(3)]))
        dcqn = mm("mla_qup_dx", dq_parts, [w_q[:, n * wide:(n + 1) * wide] for n in range(3)], 'nt')
        w_kv4 = w_kv.reshape(KV_LORA, MLA_HEADS, 2, HEAD_W)
        dw_k = mm("mla_kup_dw", ckvn, dkn, 'tn', out_dtype=BF16).reshape(KV_LORA, MLA_HEADS, 1, HEAD_W)
        dw_v = mm("mla_vup_dw", ckvn, dv, 'tn', out_dtype=BF16).reshape(KV_LORA, MLA_HEADS, 1, HEAD_W)
        gb['mla_w_kv_up'] = by_cols(jnp.concatenate([dw_k, dw_v], axis=2).reshape(KV_LORA, 2 * wide))
        dckvn = mm("mla_kvup_dx", [dkn, dv],
                   [w_kv4[:, :, 0].reshape(KV_LORA, wide), w_kv4[:, :, 1].reshape(KV_LORA, wide)], 'nt')
        dcq, dckv, gs['mla_q_norm_g'], gs['mla_kv_norm_g'] = rowmap(
            "mla_norm_bwd", mla_norm_fn, norm_ins, tile=tr, cts=[R_(dcqn), R_(dckvn)], wrt=[0, 1, 2, 3],
            gdt=[BF16, BF16, F32, F32])
        gb['w_branch_a'] = by_rows(mm("mix_a_dw", oag, dya, 'tn', out_dtype=BF16))
        doag = mm("mix_a_dx", dya, full['w_branch_a'], 'nt')
        acts, carry = send(plan, 'out_bwd')
        do_a, dog, gs['hg_norm_g'] = rowmap("hg_out_bwd", hg_out_fn, out_ins, tile=th, ncol=HG_HEADS,
                                            cts=[R_(doag, HEAD_W, 0, 1)], wrt=[0, 1, 2], gdt=[F32, BF16, F32],
                                            carry=carry)
        exchanged(acts, carry)
        names, carry = send(plan, 'scan_bwd')
        dq_in, dk_in, dlog_f, di = hg_scan_bwd("hg_scan_bwd", q_in, k_in, log_f, u, 32, states, do_a, carry=carry)
        exchanged(names, carry)
        df, dq_hg, gs['hgrn_lb_logits'] = rowmap("hg_prep_bwd", _hg_prep, hg_ins, tile=tp,
                                                 cts=[R_(dlog_f), R_(dk_in), R_(dq_in)], wrt=[0, 1, 2],
                                                 gdt=[BF16, BF16, F32])
        du = jnp.concatenate([dq_hg, df, di.astype(BF16), dog, dcq, dckv, dga, dgb, dkpe.astype(BF16),
                              dksw.astype(BF16), jnp.zeros((s, U_PAD), BF16)], axis=1)
        g_in = _w_in_grad(mm("mix_in_dw", du, h, 'tn', out_dtype=BF16))
        g_in = g_in.reshape(N_DEV, g_in.shape[0] // N_DEV, g_in.shape[1])
        gb['w_in#0'], gb['w_in#1'] = g_in[:, :half_in], g_in[:, half_in:]
        names, carry = send(plan, 'in_dx')
        dh = mm("mix_in_dx", du, w_in, 'nn', carry=carry)
        exchanged(names, carry)
        d_in, gs['mix_pre_g'] = pre_norm_bwd("mix", x_in, a['mix_pre_g'], dh, d_out)
        return d_in

    xa = dict(heads=XA_HEADS, dq=HEAD_W, dv=HEAD_W, koff=0, kstride=1, voff=XA_HEADS, vstride=1,
              scale=HEAD_W ** -0.5, causal=False)
    tm_ = min(nm, 128)

    def xa_fwd(x_in):
        w_xkv = jnp.concatenate([full['xa_w_k'], full['xa_w_v']], axis=1)
        h = pre_norm("xa", x_in, a['xa_pre_g'])
        mn = rowmap("xa_mem", pre_fn, [R_(mem), P_(a['xa_mem_g'])], [(BF16, d, 0)], tile=tm_)[0]
        q = mm("xa_q", h, full['xa_w_q'], 'nn')
        kv = mm("xa_kv", mn, w_xkv, 'nn')
        o = attn_fwd("xa_attn", q, kv, kv, **xa)
        yo = mm("xa_o", o, full['xa_w_o'], 'nn')
        return post_norm("xa", x_in, yo, a['xa_post_g'], 1.0), (x_in, h, mn, q, kv, o, yo, w_xkv)

    def xa_bwd(res, d_out, plan):
        x_in, h, mn, q, kv, o, yo, w_xkv = res
        dyo, gs['xa_post_g'] = post_norm_bwd("xa", x_in, yo, a['xa_post_g'], 1.0, d_out)
        do = mm("xa_o_dx", dyo, full['xa_w_o'], 'nt')
        gb['xa_w_o'] = by_cols(mm("xa_o_dw", o, dyo, 'tn', out_dtype=BF16))
        dq, dk, dv = attn_bwd("xa_attn_bwd", q, kv, kv, do, **xa)
        dkv = jnp.concatenate([dk, dv], axis=1).astype(BF16)
        dw = mm("xa_kv_dw", mn, dkv, 'tn', out_dtype=BF16)
        gb['xa_w_k'], gb['xa_w_v'] = by_rows(dw[:, :XA_HEADS * HEAD_W]), by_rows(dw[:, XA_HEADS * HEAD_W:])
        dmn = mm("xa_kv_dx", dkv, w_xkv, 'nt')
        gs['xa_mem_g'] = rowmap("xa_mem_bwd", pre_fn, [R_(mem), P_(a['xa_mem_g'])], tile=tm_, cts=[R_(dmn)],
                                wrt=[1], gdt=[F32])[0]
        gb['xa_w_q'] = by_rows(mm("xa_q_dw", h, dq, 'tn', out_dtype=BF16))
        dh = mm("xa_q_dx", dq, full['xa_w_q'], 'nt')
        acts, carry = send(plan, 'pre_bwd')
        d_in, gs['xa_pre_g'] = pre_norm_bwd("xa", x_in, a['xa_pre_g'], dh, d_out, carry)
        exchanged(acts, carry)
        return d_in

    landed = {}

    halves = {}
    chip_sums = {}

    def exchanged(acts, carry):
        for (kind, names), part in zip(acts or [], carry.parts if carry else []):
            for n, got in zip(names, part.results):
                if kind == 'halve':
                    halves[n] = got
                else:
                    own = lax.dynamic_index_in_dim(chip_sums[n], my_chip, 0, keepdims=True)
                    landed[n] = lax.dynamic_update_slice_in_dim(got, own, my_chip, 0)

    x1, r1 = ffn_fwd('ffn1', x, {'gate': ['ffn1_w_up'], 'up': ['ffn1_w_down'], 'act': ['w_in#0'], 'dn': ['w_in#1']})
    x2, r2 = mix_fwd(x1, {'in': ['mla_w_q_up', 'mla_w_kv_up', 'w_branch_a', 'w_branch_b', 'w_out'],
                          'scan': xa_names + ['ffn2_w_gate'], 'attn': ['ffn2_w_up']})
    x3, r3 = xa_fwd(x2)
    x4, r4 = ffn_fwd('ffn2', x3, {'gate': ['ffn2_w_down']})

    def loss_fn(y, t):
        diff = y - t
        return diff * (1.0 / d), jnp.mean(diff * diff, axis=-1, keepdims=True)

    d4, row_loss = rowmap("loss", loss_fn, [R_(x4), R_(target)], [(F32, d, 0), (F32, 1, 0)], tile=tr)
    loss = lax.psum(0.5 * jnp.sum(row_loss), ("x", "y", "c"))
    late = ['mla_w_q_up', 'mla_w_kv_up', 'w_branch_a']
    d3 = ffn_bwd('ffn2', r4, d4, {'act_bwd': [('halve', ['ffn2_w_down'])], 'up_dw': [('halve', ['ffn2_w_gate'])],
                                  'gu_dx': [('spread', ['ffn2_w_down']), ('halve', ['ffn2_w_up'])]})
    d2 = xa_bwd(r3, d3, {'pre_bwd': [('halve', xa_names)]})
    d1 = mix_bwd(r2, d2, {'attn_bwd': [('spread', ['ffn2_w_gate', 'ffn2_w_up']), ('halve', ['w_out', 'w_branch_b'])],
                          'out_bwd': [('halve', late)],
                          'scan_bwd': [('spread', xa_names + ['w_out', 'w_branch_b'] + late)],
                          'in_dx': [('halve', ['w_in#0', 'w_in#1'])]})
    grad_x = ffn_bwd('ffn1', r1, d1, {'dn_dx': [('spread', ['w_in#0'])], 'dn_dw': [('spread', ['w_in#1'])],
                                      'act_bwd': [('halve', ['ffn1_w_down'])],
                                      'gate_dw': [('spread', ['ffn1_w_down'])], 'up_dw': [('halve', ['ffn1_w_gate'])],
                                      'gu_dx': [('spread', ['ffn1_w_gate']), ('halve', ['ffn1_w_up'])],
                                      'pre_bwd': [('spread', ['ffn1_w_up'])]})

    def pack_small(vals):
        flat = jnp.concatenate([vals[n].reshape(-1) for n in SMALL])
        rows = -(-flat.shape[0] // PACK_W)
        rows = -(-rows // 8) * 8
        return jnp.pad(flat, (0, rows * PACK_W - flat.shape[0])).reshape(rows, PACK_W)

    def unpack_small(buf):
        flat, out, at = buf.reshape(-1), {}, 0
        for n in SMALL:
            size = a[n].shape[0] * a[n].shape[1]
            out[n] = flat[at:at + size].reshape(a[n].shape)
            at += size
        return out

    g_small = pack_small(gs)
    small = Gather([g_small])
    run_alone("gather_g_small", small)
    g_small = lax.dynamic_update_index_in_dim(small.results[0], g_small, me, 0)

    grads, delta, new_m, new_v = {}, {}, {}, {}
    packs = adamw("adamw_small", g_small, pack_small(a), pack_small({n: a['m_' + n] for n in SMALL}),
                  pack_small({n: a['v_' + n] for n in SMALL}))
    for dst, buf in zip((grads, delta, new_m, new_v), packs):
        dst.update(unpack_small(buf))
    landed['w_in'] = jnp.concatenate([landed['w_in#0'], landed['w_in#1']], axis=1)
    for n in BIG:
        outs = adamw("adamw_" + n, landed[n], shard(n), shard(n, 'm_'), shard(n, 'v_'))
        grads[n], delta[n], new_m[n], new_v[n] = (unflip(n, t) for t in outs)

    return (loss, grad_x[None], *[grads[n] for n in WEIGHTS], *[delta[n] for n in WEIGHTS],
            *[new_m[n] for n in WEIGHTS], *[new_v[n] for n in WEIGHTS])


def kernel(x, mem, positions, hgrn_lb_logits, ffn1_pre_g, ffn1_w_gate, ffn1_w_up, ffn1_w_down, ffn1_post_g, mix_pre_g, w_in, hg_norm_g, mla_q_norm_g, mla_w_q_up, mla_kv_norm_g, mla_w_kv_up, w_branch_a, w_branch_b, w_out, mix_post_g, xa_pre_g, xa_mem_g, xa_w_q, xa_w_k, xa_w_v, xa_w_o, xa_post_g, ffn2_pre_g, ffn2_w_gate, ffn2_w_up, ffn2_w_down, ffn2_post_g, loss_target, m_hgrn_lb_logits, m_ffn1_pre_g, m_ffn1_w_gate, m_ffn1_w_up, m_ffn1_w_down, m_ffn1_post_g, m_mix_pre_g, m_w_in, m_hg_norm_g, m_mla_q_norm_g, m_mla_w_q_up, m_mla_kv_norm_g, m_mla_w_kv_up, m_w_branch_a, m_w_branch_b, m_w_out, m_mix_post_g, m_xa_pre_g, m_xa_mem_g, m_xa_w_q, m_xa_w_k, m_xa_w_v, m_xa_w_o, m_xa_post_g, m_ffn2_pre_g, m_ffn2_w_gate, m_ffn2_w_up, m_ffn2_w_down, m_ffn2_post_g, v_hgrn_lb_logits, v_ffn1_pre_g, v_ffn1_w_gate, v_ffn1_w_up, v_ffn1_w_down, v_ffn1_post_g, v_mix_pre_g, v_w_in, v_hg_norm_g, v_mla_q_norm_g, v_mla_w_q_up, v_mla_kv_norm_g, v_mla_w_kv_up, v_w_branch_a, v_w_branch_b, v_w_out, v_mix_post_g, v_xa_pre_g, v_xa_mem_g, v_xa_w_q, v_xa_w_k, v_xa_w_v, v_xa_w_o, v_xa_post_g, v_ffn2_pre_g, v_ffn2_w_gate, v_ffn2_w_up, v_ffn2_w_down, v_ffn2_post_g):
    return _step(dict(locals()))
```

```python
import functools

import jax
import jax.numpy as jnp
from jax import lax
from jax.experimental import pallas as pl
from jax.experimental.pallas import tpu as pltpu

F32 = jnp.float32
BF16 = jnp.bfloat16

N_DEV = 8
D_MODEL = 2048
CHUNK = 64
CHUNK_SHIFT = 6
SUB = 16
HG_HEADS = 16
HG_GROUP = 16
HEAD_W = 128
MLA_HEADS = 16
Q_LORA = 512
KV_LORA = 512
QK_ROPE = 64
MLA_QK = 192
XA_HEADS = 4
ROPE_THETA = 10000.0
EPS = 1e-6
PACK_W = 1024
VMEM_LIMIT = 56 * 1024 * 1024

ADAM_LR = 0.001
ADAM_B1 = 0.9
ADAM_B2 = 0.999
ADAM_EPS = 1e-08
ADAM_WD = 0.01
ADAM_STEP = 10

U_PAD = 256

WEIGHTS = ['hgrn_lb_logits', 'ffn1_pre_g', 'ffn1_w_gate', 'ffn1_w_up', 'ffn1_w_down', 'ffn1_post_g', 'mix_pre_g',
           'w_in', 'hg_norm_g', 'mla_q_norm_g', 'mla_w_q_up', 'mla_kv_norm_g', 'mla_w_kv_up', 'w_branch_a',
           'w_branch_b', 'w_out', 'mix_post_g', 'xa_pre_g', 'xa_mem_g', 'xa_w_q', 'xa_w_k', 'xa_w_v', 'xa_w_o',
           'xa_post_g', 'ffn2_pre_g', 'ffn2_w_gate', 'ffn2_w_up', 'ffn2_w_down', 'ffn2_post_g']
BIG = {'ffn1_w_gate': True, 'ffn1_w_up': True, 'ffn1_w_down': False, 'w_in': True, 'mla_w_q_up': True,
       'mla_w_kv_up': True, 'w_branch_a': False, 'w_branch_b': False, 'w_out': False, 'xa_w_q': False,
       'xa_w_k': False, 'xa_w_v': False, 'xa_w_o': True, 'ffn2_w_gate': True, 'ffn2_w_up': True,
       'ffn2_w_down': False}
SMALL = [n for n in WEIGHTS if n not in BIG]


def _cparams(**kw):
    return pltpu.CompilerParams(vmem_limit_bytes=VMEM_LIMIT, **kw)


def _pick(dim, cands):
    for c in cands:
        if dim % c == 0:
            return c
    return dim


def _place():
    return lax.axis_index("x"), lax.axis_index("y"), lax.axis_index("c")


def _slot(px, py, pc):
    return 4 * px + 2 * py + pc


class Gather:
    def __init__(self, tensors):
        self.operands = list(tensors)
        self.out_shape = [jax.ShapeDtypeStruct((N_DEV,) + t.shape, t.dtype) for t in tensors]
        n = len(tensors)
        self.scratch = [pltpu.SemaphoreType.DMA((n, N_DEV - 1)), pltpu.SemaphoreType.DMA((n, N_DEV - 1))]

    def _copies(self, t, x_ref, out_ref, scr):
        send, recv = scr
        x, y, c = _place()
        me, sibling = (x, y, c), (x, y, 1 - c)
        x_nbr, y_nbr, diag = (1 - x, y, c), (x, 1 - y, c), (1 - x, 1 - y, c)
        relay_from, relay_to = (x ^ (1 - c), y ^ c, c), (x ^ c, y ^ (1 - c), c)

        def copy(k, block, to, src=None):
            rows = out_ref.at[_slot(*block)]
            return pltpu.make_async_remote_copy(src_ref=rows if src is None else src, dst_ref=rows,
                                                send_sem=send.at[t, k], recv_sem=recv.at[t, k], device_id=to,
                                                device_id_type=pl.DeviceIdType.MESH)

        def other(block):
            return block[0], block[1], 1 - c

        sent = [(me, sibling, x_ref), (me, x_nbr, x_ref), (me, y_nbr, x_ref), (relay_from, relay_to, None),
                (x_nbr, sibling, None), (y_nbr, sibling, None), (diag, sibling, None)]
        landing = [sibling, x_nbr, y_nbr, diag, other(x_nbr), other(y_nbr), other(diag)]
        return (lambda k: copy(k, *sent[k])), (lambda k: copy(k, landing[k], me))

    def start(self, ins, outs, scr):
        for t, (x_ref, out_ref) in enumerate(zip(ins, outs)):
            give, _ = self._copies(t, x_ref, out_ref, scr)
            for k in range(3):
                give(k).start()

    def middle(self, ins, outs, scr):
        for t, (x_ref, out_ref) in enumerate(zip(ins, outs)):
            give, take = self._copies(t, x_ref, out_ref, scr)
            take(1).wait_recv()
            take(2).wait_recv()
            for k in (3, 4, 5):
                give(k).start()

    def finish(self, ins, outs, scr):
        for t, (x_ref, out_ref) in enumerate(zip(ins, outs)):
            give, take = self._copies(t, x_ref, out_ref, scr)
            take(3).wait_recv()
            give(6).start()
            for k in (0, 4, 5, 6):
                take(k).wait_recv()
            for k in range(N_DEV - 1):
                give(k).wait_send()

    def set_results(self, res):
        self.results = list(res)


N_CHIP = N_DEV // 2


class Halve:
    def __init__(self, tensors):
        self.operands = list(tensors)
        self.out_shape = [jax.ShapeDtypeStruct((N_CHIP,) + t.shape[1:], t.dtype) for t in tensors]
        n = len(tensors)
        self.scratch = [pltpu.SemaphoreType.DMA((n, N_CHIP)), pltpu.SemaphoreType.DMA((n, N_CHIP))]

    def _copies(self, t, x_ref, theirs_ref, scr):
        send, recv = scr
        x, y, c = _place()
        return [pltpu.make_async_remote_copy(
            src_ref=x_ref.at[2 * q + 1 - c], dst_ref=theirs_ref.at[q], send_sem=send.at[t, q],
            recv_sem=recv.at[t, q], device_id=(x, y, 1 - c), device_id_type=pl.DeviceIdType.MESH)
            for q in range(N_CHIP)]

    def start(self, ins, outs, scr):
        for t, (x_ref, theirs_ref) in enumerate(zip(ins, outs)):
            for give in self._copies(t, x_ref, theirs_ref, scr):
                give.start()

    def middle(self, ins, outs, scr):
        pass

    def finish(self, ins, outs, scr):
        for t, (x_ref, theirs_ref) in enumerate(zip(ins, outs)):
            for give in self._copies(t, x_ref, theirs_ref, scr):
                give.wait_recv()
                give.wait_send()

    def set_results(self, res):
        self.results = list(res)


class Spread:
    def __init__(self, tensors):
        self.operands = list(tensors)
        self.out_shape = [jax.ShapeDtypeStruct(t.shape, t.dtype) for t in tensors]
        n = len(tensors)
        self.scratch = [pltpu.SemaphoreType.DMA((n, N_CHIP - 1)), pltpu.SemaphoreType.DMA((n, N_CHIP - 1))]

    def _copies(self, t, y_ref, out_ref, scr, outgoing):
        send, recv = scr
        x, y, c = _place()
        copies = []
        for k in range(1, N_CHIP):
            px, py = x ^ (k >> 1), y ^ (k & 1)
            copies.append(pltpu.make_async_remote_copy(
                src_ref=y_ref.at[2 * px + py], dst_ref=out_ref.at[2 * x + y if outgoing else 2 * px + py],
                send_sem=send.at[t, k - 1], recv_sem=recv.at[t, k - 1], device_id=(px, py, c),
                device_id_type=pl.DeviceIdType.MESH))
        return copies

    def start(self, ins, outs, scr):
        for t, (y_ref, out_ref) in enumerate(zip(ins, outs)):
            for give in self._copies(t, y_ref, out_ref, scr, True):
                give.start()

    def middle(self, ins, outs, scr):
        pass

    def finish(self, ins, outs, scr):
        for t, (y_ref, out_ref) in enumerate(zip(ins, outs)):
            for take in self._copies(t, y_ref, out_ref, scr, False):
                take.wait_recv()
            for give in self._copies(t, y_ref, out_ref, scr, True):
                give.wait_send()

    def set_results(self, res):
        self.results = list(res)


class Joint:
    def __init__(self, parts):
        self.parts = list(parts)
        self.operands = [o for p in parts for o in p.operands]
        self.out_shape = [o for p in parts for o in p.out_shape]
        self.scratch = [o for p in parts for o in p.scratch]

    def _split(self, ins, outs, scr):
        i = o = s = 0
        for p in self.parts:
            ni, no, ns = len(p.operands), len(p.out_shape), len(p.scratch)
            yield p, ins[i:i + ni], outs[o:o + no], scr[s:s + ns]
            i, o, s = i + ni, o + no, s + ns

    def start(self, ins, outs, scr):
        for p, a, b, c in self._split(ins, outs, scr):
            p.start(a, b, c)

    def middle(self, ins, outs, scr):
        for p, a, b, c in self._split(ins, outs, scr):
            p.middle(a, b, c)

    def finish(self, ins, outs, scr):
        for p, a, b, c in self._split(ins, outs, scr):
            p.finish(a, b, c)

    def set_results(self, res):
        for p, _, part, _ in self._split([], list(res), []):
            p.set_results(part)


_HBM = pl.BlockSpec(memory_space=pltpu.HBM)


def run_alone(name, carry):
    n_in, n_out = len(carry.operands), len(carry.out_shape)

    def body(*refs):
        ins, outs, scr = refs[:n_in], refs[n_in:n_in + n_out], refs[n_in + n_out:]
        carry.start(ins, outs, scr)
        carry.middle(ins, outs, scr)
        carry.finish(ins, outs, scr)

    res = pl.pallas_call(body, name=name, in_specs=[_HBM] * n_in, out_specs=[_HBM] * n_out,
                         out_shape=carry.out_shape, scratch_shapes=carry.scratch)(*carry.operands)
    carry.set_results(list(res))


def _call(name, body, *, grid, in_specs, out_specs, out_shape, args, scratch=(), carry=None):
    in_specs, out_specs, out_shape, scratch = list(in_specs), list(out_specs), list(out_shape), list(scratch)
    if carry is None:
        return list(pl.pallas_call(body, name=name, grid=grid, in_specs=in_specs, out_specs=out_specs,
                                   out_shape=out_shape, scratch_shapes=scratch, compiler_params=_cparams())(*args))
    n_in, n_out, n_scr = len(in_specs), len(out_shape), len(scratch)
    c_in, c_out = len(carry.operands), len(carry.out_shape)

    def wrapped(*refs):
        ins, c_ins = refs[:n_in], refs[n_in:n_in + c_in]
        at = n_in + c_in
        outs, c_outs = refs[at:at + n_out], refs[at + n_out:at + n_out + c_out]
        at += n_out + c_out
        scr, c_scr = refs[at:at + n_scr], refs[at + n_scr:]
        first = functools.reduce(jnp.logical_and, [pl.program_id(d) == 0 for d in range(len(grid))])
        last = functools.reduce(jnp.logical_and, [pl.program_id(d) == grid[d] - 1 for d in range(len(grid))])

        @pl.when(first)
        def _():
            carry.start(c_ins, c_outs, c_scr)

        body(*ins, *outs, *scr)

        step, total = 0, 1
        for d in range(len(grid)):
            step, total = step * grid[d] + pl.program_id(d), total * grid[d]

        @pl.when(step == total // 2)
        def _():
            carry.middle(c_ins, c_outs, c_scr)

        @pl.when(last)
        def _():
            carry.finish(c_ins, c_outs, c_scr)

    res = pl.pallas_call(
        wrapped, name=name, grid=grid, in_specs=in_specs + [_HBM] * c_in, out_specs=out_specs + [_HBM] * c_out,
        out_shape=out_shape + carry.out_shape, scratch_shapes=scratch + carry.scratch, compiler_params=_cparams(),
    )(*args, *carry.operands)
    carry.set_results(list(res[n_out:]))
    return list(res[:n_out])


def R_(arr, w=None, off=0, stride=0):
    return ('r', arr, arr.shape[1] if w is None else w, off, stride)


def P_(arr, w=None, off=0, stride=0):
    return ('p', arr, arr.shape[1] if w is None else w, off, stride)


def rowmap(name, fn, ins, outs=None, *, tile, ncol=1, rows_inner=True, cts=None, wrt=None, gdt=None, cat=False,
           carry=None):
    rows = next(a.shape[0] for k, a, *_ in ins if k == 'r')
    nrow = rows // tile
    assert nrow * tile == rows
    grid = (ncol, nrow) if rows_inner else (nrow, ncol)

    def ij(g0, g1):
        return (g1, g0) if rows_inner else (g0, g1)

    def spec(kind, arr, w, off, stride):
        if kind == 'r':
            return pl.BlockSpec((tile, w), lambda g0, g1: (ij(g0, g1)[0], off + stride * ij(g0, g1)[1]))
        return pl.BlockSpec((arr.shape[0], w), lambda g0, g1: (0, off + stride * ij(g0, g1)[1]))

    ops = list(ins) + list(cts or [])
    in_specs = [spec(*o) for o in ops]
    n_in = len(ins)
    fwd = cts is None
    out_shape, out_specs, acc = [], [], []
    if fwd:
        for dt, w, stride in outs:
            out_shape.append(jax.ShapeDtypeStruct((rows, w * (ncol if stride else 1)), dt))
            out_specs.append(spec('r', None, w, 0, stride))
            acc.append(None)
    elif cat:
        widths = [ins[i][2] for i in wrt]
        assert ncol == 1 and all(ins[i][0] == 'r' for i in wrt)
        out_shape.append(jax.ShapeDtypeStruct((rows, sum(widths)), gdt))
        out_specs.append(spec('r', None, sum(widths), 0, 0))
    else:
        for n, i in enumerate(wrt):
            kind, arr, w, off, stride = ins[i]
            width = w * (ncol if stride else 1)
            if kind == 'r':
                out_shape.append(jax.ShapeDtypeStruct((rows, width), gdt[n]))
                out_specs.append(spec('r', None, w, 0, 1 if stride else 0))
                shared = stride == 0 and ncol > 1
                assert not shared or (not rows_inner and gdt[n] == F32)
                acc.append('col' if shared else None)
            else:
                out_shape.append(jax.ShapeDtypeStruct((arr.shape[0], width), F32))
                out_specs.append(spec('p', arr, w, 0, 1 if stride else 0))
                assert rows_inner or ncol == 1
                acc.append('row')

    def body(*refs):
        i, j = ij(pl.program_id(0), pl.program_id(1))
        vals = [r[...].astype(F32) for r in refs[:n_in]]
        out_refs = refs[len(ops):]
        if fwd:
            for o_ref, o in zip(out_refs, fn(*vals)):
                o_ref[...] = o.astype(o_ref.dtype)
            return

        def f(*d):
            full = list(vals)
            for n, idx in enumerate(wrt):
                full[idx] = d[n]
            return fn(*full)

        _, vjp = jax.vjp(f, *[vals[idx] for idx in wrt])
        grads = vjp(tuple(r[...].astype(F32) for r in refs[n_in:len(ops)]))
        if cat:
            o_ref, at = out_refs[0], 0
            for g in grads:
                o_ref[:, at:at + g.shape[1]] = g.astype(o_ref.dtype)
                at += g.shape[1]
            return
        for o_ref, g, a in zip(out_refs, grads, acc):
            if a is None:
                o_ref[...] = g.astype(o_ref.dtype)
            else:
                first = (i if a == 'row' else j) == 0

                @pl.when(first)
                def _(o_ref=o_ref):
                    o_ref[...] = jnp.zeros_like(o_ref)

                o_ref[...] += g

    return _call(name, body, grid=grid, in_specs=in_specs, out_specs=out_specs, out_shape=out_shape,
                 args=[o[1] for o in ops], carry=carry)


def _rms(x, g):
    return x * lax.rsqrt(jnp.mean(x * x, axis=-1, keepdims=True) + EPS) * g


_DIMS = {'nn': (((1,), (0,)), ((), ())), 'nt': (((1,), (1,)), ((), ())), 'tn': (((0,), (0,)), ((), ()))}


def mm(name, a, b, mode, *, jdim=None, out_dtype=F32, carry=None):
    a_list = list(a) if isinstance(a, (list, tuple)) else [a]
    b_list = list(b) if isinstance(b, (list, tuple)) else [b]
    a_order = ('k', 'm') if mode == 'tn' else ('m', 'k')
    b_order = ('n', 'k') if mode == 'nt' else ('k', 'n')
    size, blocks = {}, 1
    for arr, order in ((a_list[0], a_order), (b_list[0], b_order)):
        shape = arr.shape
        if jdim in order:
            blocks, shape = shape[0], shape[1:]
        for dname, extent in zip(order, shape):
            assert size.setdefault(dname, extent) == extent
    tile = {'m': _pick(size['m'], (1536, 1024, 512, 256, 128)), 'n': _pick(size['n'], (1536, 512, 256, 128)),
            'k': size['k'] if size['k'] <= 2048 else _pick(size['k'], (2304, 2048, 1024, 512, 256, 128))}
    if jdim is not None:
        tile[jdim] = size[jdim]
    if a_list[0].dtype == F32 and tile['k'] * tile['m'] > (1 << 20):
        tile['m'] = _pick(size['m'], (512, 256, 128))
    grid = tuple(blocks if d == jdim else size[d] // tile[d] for d in ('m', 'n', 'k'))
    nk = grid[2]

    def spec(order):
        shape = tuple(tile[d] for d in order)

        def imap(i, j, k):
            g = {'m': i, 'n': j, 'k': k}
            idx = tuple(0 if d == jdim else g[d] for d in order)
            return ((g[jdim],) + idx) if jdim in order else idx

        return pl.BlockSpec(((None,) + shape) if jdim in order else shape, imap)

    dims = _DIMS[mode]
    nt = len(a_list)

    def product(refs):
        acc = None
        for a_ref, b_ref in zip(refs[:nt], refs[nt:2 * nt]):
            p = lax.dot_general(a_ref[...].astype(BF16), b_ref[...].astype(BF16), dims, preferred_element_type=F32)
            acc = p if acc is None else acc + p
        return acc

    def body_once(*refs):
        refs[2 * nt][...] = product(refs).astype(refs[2 * nt].dtype)

    def body_acc(*refs):
        o_ref, acc_ref = refs[2 * nt], refs[2 * nt + 1]
        k = pl.program_id(2)

        @pl.when(k == 0)
        def _():
            acc_ref[...] = jnp.zeros_like(acc_ref)

        acc_ref[...] += product(refs)

        @pl.when(k == nk - 1)
        def _():
            o_ref[...] = acc_ref[...].astype(o_ref.dtype)

    out_dims = (size['m'], size['n'])
    out_shape = jax.ShapeDtypeStruct(((blocks,) + out_dims) if jdim in ('m', 'n') else out_dims, out_dtype)
    return _call(name, body_once if nk == 1 else body_acc, grid=grid,
                 in_specs=[spec(a_order)] * nt + [spec(b_order)] * nt, out_specs=[spec(('m', 'n'))],
                 out_shape=[out_shape], args=a_list + b_list,
                 scratch=[] if nk == 1 else [pltpu.VMEM((tile['m'], tile['n']), F32)], carry=carry)[0]


def _probs(q, k, i, tq, scale, causal):
    s = lax.dot_general(q, k, _DIMS['nt'], preferred_element_type=F32) * scale
    if causal:
        shape = s.shape
        q_chunk = jnp.right_shift(i * tq + lax.broadcasted_iota(jnp.int32, shape, 0), CHUNK_SHIFT)
        k_chunk = jnp.right_shift(lax.broadcasted_iota(jnp.int32, shape, 1), CHUNK_SHIFT)
        s = jnp.where(k_chunk <= q_chunk, s, -jnp.inf)
    e = jnp.exp(s - jnp.max(s, axis=-1, keepdims=True))
    return e, jnp.sum(e, axis=-1, keepdims=True)


def _per_prefix(work, i, tq, sq, sk, causal):
    if not causal:
        work(sk)
        return
    assert sq == sk and tq % CHUNK == 0
    for j in range(sq // tq):
        @pl.when(i == j)
        def _(j=j):
            work((j + 1) * tq)


def _attn_specs(tq, sk, dq, dv, koff, kstride, voff, vstride):
    return [pl.BlockSpec((tq, dq), lambda h, i: (i, h)),
            pl.BlockSpec((sk, dq), lambda h, i: (0, koff + kstride * h)),
            pl.BlockSpec((sk, dv), lambda h, i: (0, voff + vstride * h))]


def attn_fwd(name, q, k, v, *, heads, dq, dv, koff, kstride, voff, vstride, scale, causal, carry=None):
    sq, sk = q.shape[0], k.shape[0]
    tq = min(sq, 256)

    def body(q_ref, k_ref, v_ref, o_ref):
        i = pl.program_id(1)

        def work(keys):
            e, l = _probs(q_ref[...].astype(BF16), k_ref[0:keys, :].astype(BF16), i, tq, scale, causal)
            o = jnp.dot(e.astype(BF16), v_ref[0:keys, :].astype(BF16), preferred_element_type=F32)
            o_ref[...] = o / l

        _per_prefix(work, i, tq, sq, sk, causal)

    return _call(name, body, grid=(heads, sq // tq),
                 in_specs=_attn_specs(tq, sk, dq, dv, koff, kstride, voff, vstride),
                 out_specs=[pl.BlockSpec((tq, dv), lambda h, i: (i, h))],
                 out_shape=[jax.ShapeDtypeStruct((sq, heads * dv), F32)], args=[q, k, v], carry=carry)[0]


def attn_bwd(name, q, k, v, do, *, heads, dq, dv, koff, kstride, voff, vstride, scale, causal, carry=None):
    sq, sk = q.shape[0], k.shape[0]
    tq = min(sq, 256)

    def body(q_ref, k_ref, v_ref, do_ref, dq_ref, dk_ref, dv_ref):
        i = pl.program_id(1)

        @pl.when(i == 0)
        def _():
            dk_ref[...] = jnp.zeros_like(dk_ref)
            dv_ref[...] = jnp.zeros_like(dv_ref)

        def work(keys):
            qb, kb, vb = q_ref[...].astype(BF16), k_ref[0:keys, :].astype(BF16), v_ref[0:keys, :].astype(BF16)
            dob = do_ref[...].astype(BF16)
            e, l = _probs(qb, kb, i, tq, scale, causal)
            p = e / l
            dp = lax.dot_general(dob, vb, _DIMS['nt'], preferred_element_type=F32)
            ds = (p * (dp - jnp.sum(dp * p, axis=-1, keepdims=True)) * scale).astype(BF16)
            dv_ref[0:keys, :] += lax.dot_general(p.astype(BF16), dob, _DIMS['tn'], preferred_element_type=F32)
            dk_ref[0:keys, :] += lax.dot_general(ds, qb, _DIMS['tn'], preferred_element_type=F32)
            dq_ref[...] = jnp.dot(ds, kb, preferred_element_type=F32)

        _per_prefix(work, i, tq, sq, sk, causal)

    return _call(
        name, body, grid=(heads, sq // tq),
        in_specs=_attn_specs(tq, sk, dq, dv, koff, kstride, voff, vstride) + [pl.BlockSpec((tq, dv), lambda h, i: (i, h))],
        out_specs=[pl.BlockSpec((tq, dq), lambda h, i: (i, h)), pl.BlockSpec((sk, dq), lambda h, i: (0, h)),
                   pl.BlockSpec((sk, dv), lambda h, i: (0, h))],
        out_shape=[jax.ShapeDtypeStruct((sq, heads * dq), F32), jax.ShapeDtypeStruct((sk, heads * dq), F32),
                   jax.ShapeDtypeStruct((sk, heads * dv), F32)],
        args=[q, k, v, do], carry=carry)


def _hg_chunk(q, k, g, v, state):
    c = q.shape[0]
    row = lax.broadcasted_iota(jnp.int32, (c, c), 0)
    col = lax.broadcasted_iota(jnp.int32, (c, c), 1)
    tril = (col <= row).astype(F32)
    b = jnp.dot(tril, g, precision=lax.Precision.HIGHEST, preferred_element_type=F32)
    rows = lax.broadcasted_iota(jnp.int32, (c, 1), 0)
    o = jnp.dot((q * jnp.exp(b)).astype(BF16), state.astype(BF16), preferred_element_type=F32)
    t3 = lax.broadcasted_iota(jnp.int32, (SUB, SUB, 1), 0)
    s3 = lax.broadcasted_iota(jnp.int32, (SUB, SUB, 1), 1)
    parts = []
    for n in range(c // SUB):
        lo = n * SUB
        qn, kn, bn, vn = q[lo:lo + SUB], k[lo:lo + SUB], b[lo:lo + SUB], v[lo:lo + SUB]
        decay = jnp.exp(jnp.where(s3 <= t3, bn[:, None, :] - bn[None, :, :], -jnp.inf))
        sc = jnp.sum(qn[:, None, :] * kn[None, :, :] * decay, axis=-1)
        on = jnp.dot(sc.astype(BF16), vn.astype(BF16), preferred_element_type=F32)
        if n > 0:
            ref = jnp.sum(jnp.where(rows == lo - 1, b, 0.0), axis=0, keepdims=True)
            qd = (qn * jnp.exp(bn - ref)).astype(BF16)
            kd = (k[:lo] * jnp.exp(ref - b[:lo])).astype(BF16)
            so = lax.dot_general(qd, kd, _DIMS['nt'], preferred_element_type=F32)
            on = on + jnp.dot(so.astype(BF16), v[:lo].astype(BF16), preferred_element_type=F32)
        parts.append(on)
    o = o + jnp.concatenate(parts, axis=0)
    b_last = jnp.sum(g, axis=0, keepdims=True)
    ones = jnp.ones((c, 1), F32)
    b_last_col = lax.dot_general(g, ones, _DIMS['tn'], precision=lax.Precision.HIGHEST, preferred_element_type=F32)
    kd = (k * jnp.exp(b_last - b)).astype(BF16)
    new_state = jnp.exp(b_last_col) * state + lax.dot_general(kd, v.astype(BF16), _DIMS['tn'],
                                                              preferred_element_type=F32)
    return o, new_state


def _hg_chunk_bwd(q, k, g, v, state, do, dnew):
    c, kw = q.shape
    hi = lax.Precision.HIGHEST
    row = lax.broadcasted_iota(jnp.int32, (c, c), 0)
    col = lax.broadcasted_iota(jnp.int32, (c, c), 1)
    b = jnp.dot((col <= row).astype(F32), g, precision=hi, preferred_element_type=F32)
    rows = lax.broadcasted_iota(jnp.int32, (c, 1), 0)
    b_last = jnp.sum(g, axis=0, keepdims=True)
    b_last_col = lax.dot_general(g, jnp.ones((c, 1), F32), _DIMS['tn'], precision=hi, preferred_element_type=F32)
    eb, to_end = jnp.exp(b), jnp.exp(b_last - b)
    dob, vb, dnb = do.astype(BF16), v.astype(BF16), dnew.astype(BF16)
    k_end = (k * to_end).astype(BF16)
    dq = eb * lax.dot_general(dob, state.astype(BF16), _DIMS['nt'], preferred_element_type=F32)
    dk = to_end * lax.dot_general(vb, dnb, _DIMS['nt'], preferred_element_type=F32)
    dv = jnp.dot(k_end, dnb, preferred_element_type=F32)
    dstate = jnp.exp(b_last_col) * dnew + lax.dot_general((q * eb).astype(BF16), dob, _DIMS['tn'],
                                                          preferred_element_type=F32)
    new_state = jnp.exp(b_last_col) * state + lax.dot_general(k_end, vb, _DIMS['tn'], preferred_element_type=F32)
    db_end = lax.dot_general(jnp.ones((1, dnew.shape[1]), F32), dnew * new_state, _DIMS['nt'], precision=hi,
                             preferred_element_type=F32)
    t3 = lax.broadcasted_iota(jnp.int32, (SUB, SUB, 1), 0)
    s3 = lax.broadcasted_iota(jnp.int32, (SUB, SUB, 1), 1)
    dq_rows, dk_rows, dv_rows = [], [], []
    for n in range(c // SUB):
        lo = n * SUB
        qn, kn, bn, vn, don = q[lo:lo + SUB], k[lo:lo + SUB], b[lo:lo + SUB], vb[lo:lo + SUB], dob[lo:lo + SUB]
        decay = jnp.exp(jnp.where(s3 <= t3, bn[:, None, :] - bn[None, :, :], -jnp.inf))
        sc = jnp.sum(qn[:, None, :] * kn[None, :, :] * decay, axis=-1)
        pull = lax.dot_general(don, vn, _DIMS['nt'], preferred_element_type=F32)[:, :, None] * decay
        dqn = jnp.sum(pull * kn[None, :, :], axis=1)
        dk_rows.append(jnp.sum(pull * qn[:, None, :], axis=0))
        dv_rows.append(lax.dot_general(sc.astype(BF16), don, _DIMS['tn'], preferred_element_type=F32))
        if n > 0:
            ref = jnp.sum(jnp.where(rows == lo - 1, b, 0.0), axis=0, keepdims=True)
            up, down = jnp.exp(bn - ref), jnp.exp(ref - b[:lo])
            qd, kd = (qn * up).astype(BF16), (k[:lo] * down).astype(BF16)
            so = lax.dot_general(qd, kd, _DIMS['nt'], preferred_element_type=F32).astype(BF16)
            ao = lax.dot_general(don, vb[:lo], _DIMS['nt'], preferred_element_type=F32).astype(BF16)
            dqn = dqn + up * jnp.dot(ao, kd, preferred_element_type=F32)
            rest = jnp.zeros((c - lo, kw), F32)
            dk = dk + jnp.concatenate([down * lax.dot_general(ao, qd, _DIMS['tn'], preferred_element_type=F32),
                                       rest], axis=0)
            dv = dv + jnp.concatenate([lax.dot_general(so, don, _DIMS['tn'], preferred_element_type=F32), rest],
                                      axis=0)
        dq_rows.append(dqn)
    dq = dq + jnp.concatenate(dq_rows, axis=0)
    dk = dk + jnp.concatenate(dk_rows, axis=0)
    dv = dv + jnp.concatenate(dv_rows, axis=0)
    db = q * dq - k * dk + jnp.where(rows == c - 1, db_end, 0.0)
    dg = jnp.dot((col >= row).astype(F32), db, precision=hi, preferred_element_type=F32)
    return dq, dk, dg, dv, dstate


def hg_scan_fwd(name, q, k, g, u, v_off, carry=None):
    s = q.shape[0]
    n = s // CHUNK

    def body(q_ref, k_ref, g_ref, v_ref, o_ref, st_ref, state):
        @pl.when(pl.program_id(1) == 0)
        def _():
            state[...] = jnp.zeros_like(state)

        for j in range(HG_GROUP):
            cols = slice(j * HEAD_W, (j + 1) * HEAD_W)
            st = state[j]
            st_ref[j] = st
            o, new = _hg_chunk(q_ref[:, cols], k_ref[:, cols], g_ref[:, cols], v_ref[:, cols], st)
            o_ref[:, cols] = o
            state[j] = new

    wide = HG_GROUP * HEAD_W
    blk = pl.BlockSpec((CHUNK, wide), lambda h, c: (c, h))
    return _call(
        name, body, grid=(HG_HEADS // HG_GROUP, n),
        in_specs=[blk, blk, blk, pl.BlockSpec((CHUNK, wide), lambda h, c: (c, v_off // HG_GROUP + h))],
        out_specs=[blk, pl.BlockSpec((HG_GROUP, None, HEAD_W, HEAD_W), lambda h, c: (h, c, 0, 0))],
        out_shape=[jax.ShapeDtypeStruct((s, HG_HEADS * HEAD_W), F32),
                   jax.ShapeDtypeStruct((HG_HEADS, n, HEAD_W, HEAD_W), F32)],
        scratch=[pltpu.VMEM((HG_GROUP, HEAD_W, HEAD_W), F32)], args=[q, k, g, u], carry=carry)


def hg_scan_bwd(name, q, k, g, u, v_off, states, do, carry=None):
    s = q.shape[0]
    n = s // CHUNK

    def body(q_ref, k_ref, g_ref, v_ref, st_ref, do_ref, dq_ref, dk_ref, dg_ref, dv_ref, dstate):
        @pl.when(pl.program_id(1) == 0)
        def _():
            dstate[...] = jnp.zeros_like(dstate)

        for j in range(HG_GROUP):
            cols = slice(j * HEAD_W, (j + 1) * HEAD_W)
            dq, dk, dg, dv, dst = _hg_chunk_bwd(q_ref[:, cols], k_ref[:, cols], g_ref[:, cols], v_ref[:, cols],
                                                st_ref[j], do_ref[:, cols], dstate[j])
            dq_ref[:, cols] = dq
            dk_ref[:, cols] = dk
            dg_ref[:, cols] = dg
            dv_ref[:, cols] = dv
            dstate[j] = dst

    wide = HG_GROUP * HEAD_W
    blk = pl.BlockSpec((CHUNK, wide), lambda h, c: (n - 1 - c, h))
    out = jax.ShapeDtypeStruct((s, HG_HEADS * HEAD_W), F32)
    return _call(
        name, body, grid=(HG_HEADS // HG_GROUP, n),
        in_specs=[blk, blk, blk, pl.BlockSpec((CHUNK, wide), lambda h, c: (n - 1 - c, v_off // HG_GROUP + h)),
                  pl.BlockSpec((HG_GROUP, None, HEAD_W, HEAD_W), lambda h, c: (h, n - 1 - c, 0, 0)), blk],
        out_specs=[blk, blk, blk, blk], out_shape=[out, out, out, out],
        scratch=[pltpu.VMEM((HG_GROUP, HEAD_W, HEAD_W), F32)], args=[q, k, g, u, states, do], carry=carry)


def _tile2d(rows, cols, limit):
    for tr in range(min(rows, limit // cols) // 16 * 16, 0, -16):
        if rows % tr == 0:
            return tr, cols
    for tc in (2048, 1024, 512, 256, 128):
        if cols % tc == 0 and rows * tc <= limit:
            return rows, tc
    return rows, cols


def pair_sum(name, blocks, theirs):
    _, rows, cols = theirs.shape
    tr, tc = _tile2d(rows, cols, 1 << 20)

    def body(a_ref, b_ref, o_ref):
        mine = jnp.where(lax.axis_index("c") == 0, a_ref[0].astype(F32), a_ref[1].astype(F32))
        o_ref[...] = (mine + b_ref[...].astype(F32)).astype(o_ref.dtype)

    blk = pl.BlockSpec((None, tr, tc), lambda q, i, j: (q, i, j))
    return _call(name, body, grid=(N_CHIP, rows // tr, cols // tc),
                 in_specs=[pl.BlockSpec((None, 2, tr, tc), lambda q, i, j: (q, 0, i, j)), blk], out_specs=[blk],
                 out_shape=[jax.ShapeDtypeStruct(theirs.shape, theirs.dtype)],
                 args=[blocks.reshape(N_CHIP, 2, rows, cols), theirs])[0]


def adamw(name, landed, w, m, v):
    rows, cols = w.shape
    slots = landed.shape[0]
    tr, tc = _tile2d(rows, cols, 1 << 18)

    def body(l_ref, w_ref, m_ref, v_ref, g_ref, d_ref, nm_ref, nv_ref):
        gv = l_ref[0].astype(F32)
        for s in range(1, slots):
            gv = gv + l_ref[s].astype(F32)
        nm = ADAM_B1 * m_ref[...] + (1.0 - ADAM_B1) * gv
        nv = ADAM_B2 * v_ref[...] + (1.0 - ADAM_B2) * jnp.square(gv)
        m_hat = nm / (1.0 - ADAM_B1 ** ADAM_STEP)
        v_hat = nv / (1.0 - ADAM_B2 ** ADAM_STEP)
        g_ref[...] = gv
        d_ref[...] = -ADAM_LR * (m_hat / (jnp.sqrt(v_hat) + ADAM_EPS) + ADAM_WD * w_ref[...])
        nm_ref[...] = nm
        nv_ref[...] = nv

    blk = pl.BlockSpec((tr, tc), lambda i, j: (i, j))
    out = jax.ShapeDtypeStruct((rows, cols), F32)
    return _call(name, body, grid=(rows // tr, cols // tc),
                 in_specs=[pl.BlockSpec((slots, tr, tc), lambda i, j: (0, i, j)), blk, blk, blk],
                 out_specs=[blk] * 4, out_shape=[out] * 4, args=[landed, w, m, v])


def _swap_halves(pe):
    half = QK_ROPE // 2
    return jnp.concatenate([-pe[..., half:], pe[..., :half]], axis=-1)


def _unswap_halves(dsw):
    half = QK_ROPE // 2
    return jnp.concatenate([dsw[..., half:], -dsw[..., :half]], axis=-1)


def _w_in_ext(wt):
    kpe = wt[9216:9280]
    z = jnp.zeros((HEAD_W - QK_ROPE, wt.shape[1]), wt.dtype)
    pad = jnp.zeros((U_PAD, wt.shape[1]), wt.dtype)
    return jnp.concatenate([wt[:9216], wt[9280:], kpe, z, _swap_halves(kpe.T).T, z, pad], axis=0)


def _w_in_grad(dt):
    dkpe = dt[13312:13376] + _unswap_halves(dt[13440:13504].T).T
    return jnp.concatenate([dt[:9216], dkpe, dt[9216:13312]], axis=0)


def _w_q_ext(w):
    w3 = w.reshape(Q_LORA, MLA_HEADS, MLA_QK)
    pe = w3[:, :, HEAD_W:]
    z = jnp.zeros((Q_LORA, MLA_HEADS, HEAD_W - QK_ROPE), w.dtype)
    wide = MLA_HEADS * HEAD_W
    return jnp.concatenate([w3[:, :, :HEAD_W].reshape(Q_LORA, wide),
                            jnp.concatenate([pe, z], axis=2).reshape(Q_LORA, wide),
                            jnp.concatenate([_swap_halves(pe), z], axis=2).reshape(Q_LORA, wide)], axis=1)


def _w_q_grad(parts):
    d3 = [p.reshape(Q_LORA, MLA_HEADS, HEAD_W) for p in parts]
    dpe = d3[1][:, :, :QK_ROPE] + _unswap_halves(d3[2][:, :, :QK_ROPE])
    return jnp.concatenate([d3[0], dpe], axis=2).reshape(Q_LORA, MLA_HEADS * MLA_QK)


def _hg_prep(f_raw, q_hg, logits):
    lb = jax.nn.softmax(logits, axis=0)[0:1, :]
    log_f = jnp.logaddexp(jnp.log(lb), jnp.log1p(-lb) + jax.nn.log_sigmoid(f_raw))
    k_in = (1.0 - lb) * jax.nn.sigmoid(-f_raw)
    return log_f, k_in, jax.nn.silu(q_hg)


def _rope(q_nope, q_pe, q_sw, k_nope, k_pe, k_sw, cos, sin):
    qf = jnp.concatenate([q_nope, q_pe * cos + q_sw * sin], axis=1)
    kf = jnp.concatenate([k_nope, k_pe * cos + k_sw * sin], axis=1)
    return qf, kf


def _step(a):
    x, mem, target = a['x'][0], a['mem'][0], a['loss_target'][0]
    s, d = x.shape
    nm = mem.shape[0]
    ff = N_DEV * a['ffn1_w_gate'].shape[-1]
    cs = ff // N_DEV
    tr = min(s, 256)
    tp = min(s, 256)
    th = s
    ta = 1024

    flipped = ('ffn1_w_gate', 'ffn1_w_up', 'ffn2_w_gate', 'ffn2_w_up', 'w_in')

    def shard(n, moment=''):
        t = a[moment + n][0]
        return t.T if n in flipped else t

    def unflip(n, t):
        return (t.T if n in flipped else t).reshape(a[n].shape)

    bf = {n: shard(n).astype(BF16) for n in BIG}
    half_in = 832
    bf['w_in#0'], bf['w_in#1'] = bf['w_in'][:half_in], bf['w_in'][half_in:]
    gat, full = {}, {}

    my_chip = 2 * lax.axis_index("x") + lax.axis_index("y")
    me = 2 * my_chip + lax.axis_index("c")

    def gathered(names, carry):
        for n, g8 in zip(names or [], carry.results if carry else []):
            r, c = bf[n].shape
            g8 = lax.dynamic_update_index_in_dim(g8, bf[n], me, 0)
            gat[n] = g8
            if not n.startswith(('ffn', 'w_in')):
                full[n] = g8.transpose(1, 0, 2).reshape(r, N_DEV * c) if BIG[n] else g8.reshape(N_DEV * r, c)

    xa_names = ['xa_w_q', 'xa_w_k', 'xa_w_v', 'xa_w_o']
    ffn2_names = ['ffn2_w_gate', 'ffn2_w_up', 'ffn2_w_down']
    first = Gather([bf['ffn1_w_gate']])
    run_alone("gather_first", first)
    gathered(['ffn1_w_gate'], first)

    inv_freq = 1.0 / (ROPE_THETA ** (jnp.arange(0, QK_ROPE, 2, dtype=F32) / QK_ROPE))
    ang = a['positions'][0].astype(F32)[:, None] * inv_freq
    zero = jnp.zeros((s, HEAD_W - QK_ROPE), F32)
    cos = jnp.concatenate([jnp.cos(ang), jnp.cos(ang), zero], axis=1)
    sin = jnp.concatenate([jnp.sin(ang), jnp.sin(ang), zero], axis=1)

    gs = {}
    gb = {}

    def by_rows(g):
        return g.reshape(N_DEV, g.shape[0] // N_DEV, g.shape[1])

    def by_cols(g):
        return g.reshape(g.shape[0], N_DEV, g.shape[1] // N_DEV).transpose(1, 0, 2)

    pre_fn = lambda xv, g: (_rms(xv, g),)
    pre_res_fn = lambda xv, g: (_rms(xv, g), xv)

    def pre_norm(tag, x_in, g):
        return rowmap(tag + "_pre", pre_fn, [R_(x_in), P_(g)], [(BF16, x_in.shape[1], 0)], tile=tr)[0]

    def pre_norm_bwd(tag, x_in, g, dh, d_out, carry=None):
        return rowmap(tag + "_pre_bwd", pre_res_fn, [R_(x_in), P_(g)], tile=tr, cts=[R_(dh), R_(d_out)],
                      wrt=[0, 1], gdt=[F32, F32], carry=carry)

    def post_fn(weight):
        return lambda xv, y, g: (xv + weight * _rms(y, g),)

    def post_norm(tag, x_in, y, g, weight):
        return rowmap(tag + "_post", post_fn(weight), [R_(x_in), R_(y), P_(g)], [(F32, d, 0)], tile=tr)[0]

    def post_norm_bwd(tag, x_in, y, g, weight, d_out):
        return rowmap(tag + "_post_bwd", post_fn(weight), [R_(x_in), R_(y), P_(g)], tile=tr, cts=[R_(d_out)],
                      wrt=[1, 2], gdt=[BF16, F32])

    act_fn = lambda av, bv: (jax.nn.silu(av) * bv,)

    def fetch(plan, key):
        names = plan.get(key)
        return (names, Gather([bf[n] for n in names])) if names else (None, None)

    def send(plan, key):
        acts = plan.get(key)
        if not acts:
            return None, None
        parts = []
        for kind, names in acts:
            if kind == 'spread':
                for n in names:
                    chip_sums[n] = pair_sum("pair_" + n.replace('#', '_'), gb[n], halves[n])
            parts.append(Halve([gb[n] for n in names]) if kind == 'halve' else Spread([chip_sums[n] for n in names]))
        return acts, Joint(parts)

    def ffn_fwd(tag, x_in, plan):
        h = pre_norm(tag, x_in, a[tag + '_pre_g'])
        names, carry = fetch(plan, 'gate')
        av = mm(tag + "_gate", h, gat[tag + '_w_gate'], 'nt', jdim='n', out_dtype=BF16, carry=carry)
        gathered(names, carry)
        names, carry = fetch(plan, 'up')
        bv = mm(tag + "_up", h, gat[tag + '_w_up'], 'nt', jdim='n', out_dtype=BF16, carry=carry)
        gathered(names, carry)
        a2, b2 = av.reshape(N_DEV * s, cs), bv.reshape(N_DEV * s, cs)
        names, carry = fetch(plan, 'act')
        z = rowmap(tag + "_act", act_fn, [R_(a2), R_(b2)], [(BF16, cs, 0)], tile=ta, carry=carry)[0]
        z = z.reshape(N_DEV, s, cs)
        gathered(names, carry)
        names, carry = fetch(plan, 'dn')
        y = mm(tag + "_dn", z, gat[tag + '_w_down'], 'nn', jdim='k', carry=carry)
        gathered(names, carry)
        return post_norm(tag, x_in, y, a[tag + '_post_g'], 0.5), (x_in, h, a2, b2, z, y)

    def ffn_bwd(tag, res, d_out, plan):
        x_in, h, a2, b2, z, y = res
        dy, gs[tag + '_post_g'] = post_norm_bwd(tag, x_in, y, a[tag + '_post_g'], 0.5, d_out)
        acts, carry = send(plan, 'dn_dx')
        dz = mm(tag + "_dn_dx", dy, gat[tag + '_w_down'], 'nt', jdim='n', carry=carry)
        exchanged(acts, carry)
        acts, carry = send(plan, 'dn_dw')
        gb[tag + '_w_down'] = mm(tag + "_dn_dw", z, dy, 'tn', jdim='m', out_dtype=BF16, carry=carry)
        exchanged(acts, carry)
        acts, carry = send(plan, 'act_bwd')
        da, db = rowmap(tag + "_act_bwd", act_fn, [R_(a2), R_(b2)], tile=ta, cts=[R_(dz.reshape(N_DEV * s, cs))],
                        wrt=[0, 1], gdt=[BF16, BF16], carry=carry)
        exchanged(acts, carry)
        da, db = da.reshape(N_DEV, s, cs), db.reshape(N_DEV, s, cs)
        acts, carry = send(plan, 'gate_dw')
        gb[tag + '_w_gate'] = mm(tag + "_gate_dw", da, h, 'tn', jdim='m', out_dtype=BF16, carry=carry)
        exchanged(acts, carry)
        acts, carry = send(plan, 'up_dw')
        gb[tag + '_w_up'] = mm(tag + "_up_dw", db, h, 'tn', jdim='m', out_dtype=BF16, carry=carry)
        exchanged(acts, carry)
        acts, carry = send(plan, 'gu_dx')
        dh = mm(tag + "_gu_dx", [da, db], [gat[tag + '_w_gate'], gat[tag + '_w_up']], 'nn', jdim='k', carry=carry)
        exchanged(acts, carry)
        names, carry = send(plan, 'pre_bwd')
        d_in, gs[tag + '_pre_g'] = pre_norm_bwd(tag, x_in, a[tag + '_pre_g'], dh, d_out, carry)
        exchanged(names, carry)
        return d_in

    hg_out_fn = lambda o, og, g: (_rms(o, g) * jax.nn.silu(og),)
    mla_norm_fn = lambda cq, ckv, gq, gkv: (_rms(cq, gq), _rms(ckv, gkv))
    gate_fn = lambda ga, gb, ya, yb: (jax.nn.sigmoid(ga) * ya + jax.nn.sigmoid(gb) * yb,)
    mla = dict(heads=MLA_HEADS, dq=2 * HEAD_W, dv=HEAD_W, koff=0, kstride=1, voff=1, vstride=2,
               scale=MLA_QK ** -0.5, causal=True)

    def mix_fwd(x_in, plan):
        w_in = jnp.concatenate([gat['w_in#0'], gat['w_in#1']], axis=1)
        w_in = _w_in_ext(w_in.reshape(-1, w_in.shape[2]))
        h = pre_norm("mix", x_in, a['mix_pre_g'])
        names, carry = fetch(plan, 'in')
        u = mm("mix_in", h, w_in, 'nt', carry=carry)
        gathered(names, carry)
        w_q, w_kv = _w_q_ext(full['mla_w_q_up']), full['mla_w_kv_up']
        hg_ins = [R_(u, 2048, 1), R_(u, 2048, 0), P_(a['hgrn_lb_logits'])]
        log_f, k_in, q_in = rowmap("hg_prep", _hg_prep, hg_ins, [(F32, 2048, 0)] * 3, tile=tp)
        names, carry = fetch(plan, 'scan')
        o_a, states = hg_scan_fwd("hg_scan", q_in, k_in, log_f, u, 32, carry=carry)
        gathered(names, carry)
        out_ins = [R_(o_a, HEAD_W, 0, 1), R_(u, HEAD_W, 48, 1), P_(a['hg_norm_g'], HEAD_W, 0, 1)]
        oag = rowmap("hg_out", hg_out_fn, out_ins, [(BF16, HEAD_W, 1)], tile=th, ncol=HG_HEADS)[0]
        y_a = mm("mix_a", oag, full['w_branch_a'], 'nn')
        norm_ins = [R_(u, 512, 16), R_(u, 512, 17), P_(a['mla_q_norm_g']), P_(a['mla_kv_norm_g'])]
        cqn, ckvn = rowmap("mla_norm", mla_norm_fn, norm_ins, [(BF16, 512, 0)] * 2, tile=tr)
        q_all = mm("mla_qup", cqn, w_q, 'nn')
        kv = mm("mla_kvup", ckvn, w_kv, 'nn')
        rope_ins = [R_(q_all, HEAD_W, 0, 1), R_(q_all, HEAD_W, 16, 1), R_(q_all, HEAD_W, 32, 1),
                    R_(kv, HEAD_W, 0, 2), R_(u, HEAD_W, 104, 0), R_(u, HEAD_W, 105, 0), R_(cos), R_(sin)]
        qf, kf = rowmap("mla_rope", _rope, rope_ins, [(BF16, 2 * HEAD_W, 1)] * 2, tile=th, ncol=MLA_HEADS,
                        rows_inner=False)
        names, carry = fetch(plan, 'attn')
        o_b = attn_fwd("mla_attn", qf, kf, kv, carry=carry, **mla)
        gathered(names, carry)
        y_b = mm("mix_b", o_b, full['w_branch_b'], 'nn')
        gate_ins = [R_(u, 1024, 9, 1), R_(u, 1024, 11, 1), R_(y_a, 1024, 0, 1), R_(y_b, 1024, 0, 1)]
        y = rowmap("mix_gate", gate_fn, gate_ins, [(BF16, 1024, 1)], tile=tr, ncol=2)[0]
        yo = mm("mix_out", y, full['w_out'], 'nn')
        res = (x_in, h, hg_ins, q_in, k_in, log_f, u, states, out_ins, oag, norm_ins, cqn, ckvn, rope_ins, qf, kf,
               kv, o_b, gate_ins, y, yo, w_in, w_q, w_kv)
        return post_norm("mix", x_in, yo, a['mix_post_g'], 1.0), res

    def mix_bwd(res, d_out, plan):
        (x_in, h, hg_ins, q_in, k_in, log_f, u, states, out_ins, oag, norm_ins, cqn, ckvn, rope_ins, qf, kf, kv,
         o_b, gate_ins, y, yo, w_in, w_q, w_kv) = res
        dyo, gs['mix_post_g'] = post_norm_bwd("mix", x_in, yo, a['mix_post_g'], 1.0, d_out)
        dy = mm("mix_out_dx", dyo, full['w_out'], 'nt')
        gb['w_out'] = by_rows(mm("mix_out_dw", y, dyo, 'tn', out_dtype=BF16))
        dga, dgb, dya, dyb = rowmap("mix_gate_bwd", gate_fn, gate_ins, tile=tr, ncol=2, cts=[R_(dy, 1024, 0, 1)],
                                    wrt=[0, 1, 2, 3], gdt=[BF16] * 4)
        gb['w_branch_b'] = by_rows(mm("mix_b_dw", o_b, dyb, 'tn', out_dtype=BF16))
        do_b = mm("mix_b_dx", dyb, full['w_branch_b'], 'nt')
        names, carry = send(plan, 'attn_bwd')
        dqf, dkf, dv = attn_bwd("mla_attn_bwd", qf, kf, kv, do_b, carry=carry, **mla)
        exchanged(names, carry)
        dqn, dqp, dqs, dkn, dkpe, dksw = rowmap(
            "mla_rope_bwd", _rope, rope_ins, tile=th, ncol=MLA_HEADS, rows_inner=False,
            cts=[R_(dqf, 2 * HEAD_W, 0, 1), R_(dkf, 2 * HEAD_W, 0, 1)], wrt=[0, 1, 2, 3, 4, 5],
            gdt=[BF16, BF16, BF16, BF16, F32, F32])
        wide = MLA_HEADS * HEAD_W
        dq_parts = [dqn, dqp, dqs]
        gb['mla_w_q_up'] = by_cols(_w_q_grad([mm("mla_qup_dw%d" % n, cqn, dq_parts[n], 'tn', out_dtype=BF16)
                                              for n in range(3)]))
        dcqn = mm("mla_qup_dx", dq_parts, [w_q[:, n * wide:(n + 1) * wide] for n in range(3)], 'nt')
        w_kv4 = w_kv.reshape(KV_LORA, MLA_HEADS, 2, HEAD_W)
        dw_k = mm("mla_kup_dw", ckvn, dkn, 'tn', out_dtype=BF16).reshape(KV_LORA, MLA_HEADS, 1, HEAD_W)
        dw_v = mm("mla_vup_dw", ckvn, dv, 'tn', out_dtype=BF16).reshape(KV_LORA, MLA_HEADS, 1, HEAD_W)
        gb['mla_w_kv_up'] = by_cols(jnp.concatenate([dw_k, dw_v], axis=2).reshape(KV_LORA, 2 * wide))
        dckvn = mm("mla_kvup_dx", [dkn, dv],
                   [w_kv4[:, :, 0].reshape(KV_LORA, wide), w_kv4[:, :, 1].reshape(KV_LORA, wide)], 'nt')
        dcq, dckv, gs['mla_q_norm_g'], gs['mla_kv_norm_g'] = rowmap(
            "mla_norm_bwd", mla_norm_fn, norm_ins, tile=tr, cts=[R_(dcqn), R_(dckvn)], wrt=[0, 1, 2, 3],
            gdt=[BF16, BF16, F32, F32])
        gb['w_branch_a'] = by_rows(mm("mix_a_dw", oag, dya, 'tn', out_dtype=BF16))
        doag = mm("mix_a_dx", dya, full['w_branch_a'], 'nt')
        acts, carry = send(plan, 'out_bwd')
        do_a, dog, gs['hg_norm_g'] = rowmap("hg_out_bwd", hg_out_fn, out_ins, tile=th, ncol=HG_HEADS,
                                            cts=[R_(doag, HEAD_W, 0, 1)], wrt=[0, 1, 2], gdt=[F32, BF16, F32],
                                            carry=carry)
        exchanged(acts, carry)
        names, carry = send(plan, 'scan_bwd')
        dq_in, dk_in, dlog_f, di = hg_scan_bwd("hg_scan_bwd", q_in, k_in, log_f, u, 32, states, do_a, carry=carry)
        exchanged(names, carry)
        df, dq_hg, gs['hgrn_lb_logits'] = rowmap("hg_prep_bwd", _hg_prep, hg_ins, tile=tp,
                                                 cts=[R_(dlog_f), R_(dk_in), R_(dq_in)], wrt=[0, 1, 2],
                                                 gdt=[BF16, BF16, F32])
        du = jnp.concatenate([dq_hg, df, di.astype(BF16), dog, dcq, dckv, dga, dgb, dkpe.astype(BF16),
                              dksw.astype(BF16), jnp.zeros((s, U_PAD), BF16)], axis=1)
        g_in = _w_in_grad(mm("mix_in_dw", du, h, 'tn', out_dtype=BF16))
        g_in = g_in.reshape(N_DEV, g_in.shape[0] // N_DEV, g_in.shape[1])
        gb['w_in#0'], gb['w_in#1'] = g_in[:, :half_in], g_in[:, half_in:]
        names, carry = send(plan, 'in_dx')
        dh = mm("mix_in_dx", du, w_in, 'nn', carry=carry)
        exchanged(names, carry)
        d_in, gs['mix_pre_g'] = pre_norm_bwd("mix", x_in, a['mix_pre_g'], dh, d_out)
        return d_in

    xa = dict(heads=XA_HEADS, dq=HEAD_W, dv=HEAD_W, koff=0, kstride=1, voff=XA_HEADS, vstride=1,
              scale=HEAD_W ** -0.5, causal=False)
    tm_ = min(nm, 128)

    def xa_fwd(x_in):
        w_xkv = jnp.concatenate([full['xa_w_k'], full['xa_w_v']], axis=1)
        h = pre_norm("xa", x_in, a['xa_pre_g'])
        mn = rowmap("xa_mem", pre_fn, [R_(mem), P_(a['xa_mem_g'])], [(BF16, d, 0)], tile=tm_)[0]
        q = mm("xa_q", h, full['xa_w_q'], 'nn')
        kv = mm("xa_kv", mn, w_xkv, 'nn')
        o = attn_fwd("xa_attn", q, kv, kv, **xa)
        yo = mm("xa_o", o, full['xa_w_o'], 'nn')
        return post_norm("xa", x_in, yo, a['xa_post_g'], 1.0), (x_in, h, mn, q, kv, o, yo, w_xkv)

    def xa_bwd(res, d_out, plan):
        x_in, h, mn, q, kv, o, yo, w_xkv = res
        dyo, gs['xa_post_g'] = post_norm_bwd("xa", x_in, yo, a['xa_post_g'], 1.0, d_out)
        do = mm("xa_o_dx", dyo, full['xa_w_o'], 'nt')
        gb['xa_w_o'] = by_cols(mm("xa_o_dw", o, dyo, 'tn', out_dtype=BF16))
        dq, dk, dv = attn_bwd("xa_attn_bwd", q, kv, kv, do, **xa)
        dkv = jnp.concatenate([dk, dv], axis=1).astype(BF16)
        dw = mm("xa_kv_dw", mn, dkv, 'tn', out_dtype=BF16)
        gb['xa_w_k'], gb['xa_w_v'] = by_rows(dw[:, :XA_HEADS * HEAD_W]), by_rows(dw[:, XA_HEADS * HEAD_W:])
        dmn = mm("xa_kv_dx", dkv, w_xkv, 'nt')
        gs['xa_mem_g'] = rowmap("xa_mem_bwd", pre_fn, [R_(mem), P_(a['xa_mem_g'])], tile=tm_, cts=[R_(dmn)],
                                wrt=[1], gdt=[F32])[0]
        gb['xa_w_q'] = by_rows(mm("xa_q_dw", h, dq, 'tn', out_dtype=BF16))
        dh = mm("xa_q_dx", dq, full['xa_w_q'], 'nt')
        acts, carry = send(plan, 'pre_bwd')
        d_in, gs['xa_pre_g'] = pre_norm_bwd("xa", x_in, a['xa_pre_g'], dh, d_out, carry)
        exchanged(acts, carry)
        return d_in

    landed = {}

    halves = {}
    chip_sums = {}

    def exchanged(acts, carry):
        for (kind, names), part in zip(acts or [], carry.parts if carry else []):
            for n, got in zip(names, part.results):
                if kind == 'halve':
                    halves[n] = got
                else:
                    own = lax.dynamic_index_in_dim(chip_sums[n], my_chip, 0, keepdims=True)
                    landed[n] = lax.dynamic_update_slice_in_dim(got, own, my_chip, 0)

    x1, r1 = ffn_fwd('ffn1', x, {'gate': ['ffn1_w_up'], 'up': ['ffn1_w_down'], 'act': ['w_in#0'], 'dn': ['w_in#1']})
    x2, r2 = mix_fwd(x1, {'in': ['mla_w_q_up', 'mla_w_kv_up', 'w_branch_a', 'w_branch_b', 'w_out'],
                          'scan': xa_names + ['ffn2_w_gate'], 'attn': ['ffn2_w_up']})
    x3, r3 = xa_fwd(x2)
    x4, r4 = ffn_fwd('ffn2', x3, {'gate': ['ffn2_w_down']})

    def loss_fn(y, t):
        diff = y - t
        return diff * (1.0 / d), jnp.mean(diff * diff, axis=-1, keepdims=True)

    d4, row_loss = rowmap("loss", loss_fn, [R_(x4), R_(target)], [(F32, d, 0), (F32, 1, 0)], tile=tr)
    loss = lax.psum(0.5 * jnp.sum(row_loss), ("x", "y", "c"))
    late = ['mla_w_q_up', 'mla_w_kv_up', 'w_branch_a']
    d3 = ffn_bwd('ffn2', r4, d4, {'act_bwd': [('halve', ['ffn2_w_down'])], 'up_dw': [('halve', ['ffn2_w_gate'])],
                                  'gu_dx': [('spread', ['ffn2_w_down']), ('halve', ['ffn2_w_up'])]})
    d2 = xa_bwd(r3, d3, {'pre_bwd': [('halve', xa_names)]})
    d1 = mix_bwd(r2, d2, {'attn_bwd': [('spread', ['ffn2_w_gate', 'ffn2_w_up']), ('halve', ['w_out', 'w_branch_b'])],
                          'out_bwd': [('halve', late)],
                          'scan_bwd': [('spread', xa_names + ['w_out', 'w_branch_b'] + late)],
                          'in_dx': [('halve', ['w_in#0', 'w_in#1'])]})
    grad_x = ffn_bwd('ffn1', r1, d1, {'dn_dx': [('spread', ['w_in#0'])], 'dn_dw': [('spread', ['w_in#1'])],
                                      'act_bwd': [('halve', ['ffn1_w_down'])],
                                      'gate_dw': [('spread', ['ffn1_w_down'])], 'up_dw': [('halve', ['ffn1_w_gate'])],
                                      'gu_dx': [('spread', ['ffn1_w_gate']), ('halve', ['ffn1_w_up'])],
                                      'pre_bwd': [('spread', ['ffn1_w_up'])]})

    def pack_small(vals):
        flat = jnp.concatenate([vals[n].reshape(-1) for n in SMALL])
        rows = -(-flat.shape[0] // PACK_W)
        rows = -(-rows // 8) * 8
        return jnp.pad(flat, (0, rows * PACK_W - flat.shape[0])).reshape(rows, PACK_W)

    def unpack_small(buf):
        flat, out, at = buf.reshape(-1), {}, 0
        for n in SMALL:
            size = a[n].shape[0] * a[n].shape[1]
            out[n] = flat[at:at + size].reshape(a[n].shape)
            at += size
        return out

    g_small = pack_small(gs)
    small = Gather([g_small])
    run_alone("gather_g_small", small)
    g_small = lax.dynamic_update_index_in_dim(small.results[0], g_small, me, 0)

    grads, delta, new_m, new_v = {}, {}, {}, {}
    packs = adamw("adamw_small", g_small, pack_small(a), pack_small({n: a['m_' + n] for n in SMALL}),
                  pack_small({n: a['v_' + n] for n in SMALL}))
    for dst, buf in zip((grads, delta, new_m, new_v), packs):
        dst.update(unpack_small(buf))
    landed['w_in'] = jnp.concatenate([landed['w_in#0'], landed['w_in#1']], axis=1)
    for n in BIG:
        outs = adamw("adamw_" + n, landed[n], shard(n), shard(n, 'm_'), shard(n, 'v_'))
        grads[n], delta[n], new_m[n], new_v[n] = (unflip(n, t) for t in outs)

    return (loss, grad_x[None], *[grads[n] for n in WEIGHTS], *[delta[n] for n in WEIGHTS],
            *[new_m[n] for n in WEIGHTS], *[new_v[n] for n in WEIGHTS])


def kernel(x, mem, positions, hgrn_lb_logits, ffn1_pre_g, ffn1_w_gate, ffn1_w_up, ffn1_w_down, ffn1_post_g, mix_pre_g, w_in, hg_norm_g, mla_q_norm_g, mla_w_q_up, mla_kv_norm_g, mla_w_kv_up, w_branch_a, w_branch_b, w_out, mix_post_g, xa_pre_g, xa_mem_g, xa_w_q, xa_w_k, xa_w_v, xa_w_o, xa_post_g, ffn2_pre_g, ffn2_w_gate, ffn2_w_up, ffn2_w_down, ffn2_post_g, loss_target, m_hgrn_lb_logits, m_ffn1_pre_g, m_ffn1_w_gate, m_ffn1_w_up, m_ffn1_w_down, m_ffn1_post_g, m_mix_pre_g, m_w_in, m_hg_norm_g, m_mla_q_norm_g, m_mla_w_q_up, m_mla_kv_norm_g, m_mla_w_kv_up, m_w_branch_a, m_w_branch_b, m_w_out, m_mix_post_g, m_xa_pre_g, m_xa_mem_g, m_xa_w_q, m_xa_w_k, m_xa_w_v, m_xa_w_o, m_xa_post_g, m_ffn2_pre_g, m_ffn2_w_gate, m_ffn2_w_up, m_ffn2_w_down, m_ffn2_post_g, v_hgrn_lb_logits, v_ffn1_pre_g, v_ffn1_w_gate, v_ffn1_w_up, v_ffn1_w_down, v_ffn1_post_g, v_mix_pre_g, v_w_in, v_hg_norm_g, v_mla_q_norm_g, v_mla_w_q_up, v_mla_kv_norm_g, v_mla_w_kv_up, v_w_branch_a, v_w_branch_b, v_w_out, v_mix_post_g, v_xa_pre_g, v_xa_mem_g, v_xa_w_q, v_xa_w_k, v_xa_w_v, v_xa_w_o, v_xa_post_g, v_ffn2_pre_g, v_ffn2_w_gate, v_ffn2_w_up, v_ffn2_w_down, v_ffn2_post_g):
    return _step(dict(locals()))
```
